```python
import math
import jax
import jax.numpy as jnp
from jax import lax
import numpy as np

D_MODEL = 1024
BATCH = 8
SEQ = 2048
DEPTH = 2
DEC_BATCH = 32
DEC_SEQ = 8
PAST_LEN = 8192
PAGE_SIZE = 128

HEAD_DIM = 64
N_EVEN = (DEPTH + 1) // 2
N_ODD = DEPTH // 2
D_FF = 2816
PLE_DIM = 256
HG_HEADS = 8
HG_DK = 64
HG_DV = 64
HG_CHUNK = 64
NSA_HEADS = 12
NSA_KV = 3
NSA_GROUP = NSA_HEADS // NSA_KV
CMP_BLOCK = 32
CMP_STRIDE = 16
CMP_RATIO = CMP_BLOCK // CMP_STRIDE
SEL_BLOCK = 64
SEL_TOPK = 16
NSA_WINDOW = 512
NSA_QBLOCK = 32
WIN_QBLOCK = 128
FORCE_SCORE = 1e4
DIL_PATTERN = ((128, 1), (512, 4), (2048, 16))
DIL_GROUPS = 3
DIL_HEADS = 4
DIL_QBLOCK = 128
RP_BUCKETS = 32
RP_MAX_DIST = 2048
N_BIAS_HEADS = 12
NEG_POS = -(2 ** 30)
EVEN_WIDTHS = (HG_HEADS * HG_DK, HG_HEADS * HG_DK, HG_HEADS * HG_DV, HG_HEADS * HG_DV,
               NSA_HEADS * HEAD_DIM) + (NSA_KV * HEAD_DIM,) * 6 + (NSA_HEADS * 3,)
EVEN_IN = sum(EVEN_WIDTHS)
EVEN_OUT = HG_HEADS * HG_DV + NSA_HEADS * HEAD_DIM
ODD_WIDTHS = (DIL_GROUPS * DIL_HEADS * HEAD_DIM,) * 3
ODD_IN = sum(ODD_WIDTHS)
ODD_OUT = DIL_HEADS * HEAD_DIM

kernel_name = 'hybrid_hgrn2_nsa_dilated_step'


def rms_norm(x, gain, eps=1e-6):
    xf = x.astype(jnp.float32)
    y = xf * lax.rsqrt(jnp.mean(xf * xf, axis=-1, keepdims=True) + eps)
    return (y * gain.astype(jnp.float32)).astype(x.dtype)


def split_cols(a, widths):
    out, s = [], 0
    for w in widths:
        out.append(a[..., s:s + w])
        s += w
    return out


def swiglu(x, w_in, w_out):
    g, u = jnp.split(x @ w_in, 2, axis=-1)
    return (jax.nn.silu(g) * u) @ w_out


def rel_bucket(dist):
    exact = RP_BUCKETS // 2
    d = jnp.maximum(dist, 0)
    log_ratio = jnp.log(jnp.maximum(d, 1).astype(jnp.float32) / exact) / math.log(RP_MAX_DIST / exact)
    large = jnp.minimum(exact + (log_ratio * (RP_BUCKETS - exact)).astype(jnp.int32), RP_BUCKETS - 1)
    return jnp.where(d < exact, d, large)


def masked_softmax(logits, mask):
    s = jnp.where(mask, logits, jnp.finfo(jnp.float32).min)
    m = jnp.max(s, axis=-1, keepdims=True)
    e = jnp.where(mask, jnp.exp(s - m), 0.0)
    den = jnp.maximum(jnp.sum(e, axis=-1, keepdims=True), 1e-30)
    return e / den, (m + jnp.log(den))[..., 0]


def hgrn2_scan(q, log_f, v, s0):
    B, T, H, DK = q.shape
    C = math.gcd(T, HG_CHUNK)
    nC = T // C
    k = -jnp.expm1(log_f)

    def to_chunks(a):
        return a.reshape(B, nC, C, H, a.shape[-1]).transpose(1, 0, 3, 2, 4)

    causal = jnp.tril(jnp.ones((C, C), bool))[:, :, None]

    def step(S, xs):
        qb, lfb, kb, vb = xs
        b = jnp.cumsum(lfb, axis=2)
        diff = b[:, :, :, None, :] - b[:, :, None, :, :]
        decay = jnp.exp(jnp.where(causal, diff, -jnp.inf))
        A = jnp.einsum('bhtk,bhsk,bhtsk->bhts', qb, kb, decay)
        o = jnp.einsum('bhts,bhsv->bhtv', A, vb) + jnp.einsum('bhtk,bhkv->bhtv', qb * jnp.exp(b), S)
        bC = b[:, :, -1:, :]
        S_new = jnp.exp(bC[:, :, 0, :])[..., None] * S + jnp.einsum('bhsk,bhsv->bhkv', kb * jnp.exp(bC - b), vb)
        return S_new, o

    S_fin, o = lax.scan(step, s0, (to_chunks(q), to_chunks(log_f), to_chunks(k), to_chunks(v)))
    return o.transpose(1, 0, 3, 2, 4).reshape(B, T, H, -1), S_fin


def nsa_compress(raw, pe, w1, w2):
    B, L, G, dh = raw.shape
    n_chunks = L // CMP_STRIDE
    n_cmp = n_chunks - CMP_RATIO + 1
    chunks = raw[:, :n_chunks * CMP_STRIDE].reshape(B, n_chunks, CMP_STRIDE, G, dh)
    w1r = w1.reshape(CMP_RATIO, CMP_STRIDE, dh, -1)
    per = pe.reshape(CMP_RATIO, CMP_STRIDE, dh)
    hid = sum(jnp.einsum('bnsgd,sde->bnge', chunks[:, r:r + n_cmp] + per[r][:, None, :], w1r[r])
              for r in range(CMP_RATIO))
    return jnp.einsum('bnge,ed->bngd', jax.nn.silu(hid), w2)


def nsa_cmp_sel(q, q_pos, kc, vc, ks, vs, rel_bias):
    B, Tq, H, dh = q.shape
    L = ks.shape[1]
    n_cmp = kc.shape[1]
    n_slc = -(-L // SEL_BLOCK)
    k_sel = min(SEL_TOPK, n_slc)
    scale = dh ** -0.5
    c_lo = np.arange(n_cmp) * CMP_STRIDE
    c_hi = c_lo + CMP_BLOCK - 1
    s_lo = np.arange(n_slc) * SEL_BLOCK
    overlap = ((c_lo[:, None] <= s_lo[None, :] + SEL_BLOCK - 1) & (c_hi[:, None] >= s_lo[None, :])).astype(np.float32)
    cmp_end = jnp.asarray(c_hi, jnp.int32)
    tbl = rel_bias.astype(jnp.float32).reshape(RP_BUCKETS, NSA_KV, NSA_GROUP)
    tbl_g = tbl.transpose(1, 0, 2)
    qb_size = math.gcd(Tq, NSA_QBLOCK)
    nb = Tq // qb_size
    qblocks = q.reshape(B, nb, qb_size, NSA_KV, NSA_GROUP, dh).swapaxes(0, 1)
    pblocks = q_pos.reshape(nb, qb_size)
    b_ix = jnp.arange(B)[:, None, None, None]
    g_ix = jnp.arange(NSA_KV)[None, :, None, None]
    jj = jnp.arange(n_slc)

    def block(args):
        qb, pos = args
        dist_c = pos[:, None] - cmp_end[None, :]
        lc = jnp.einsum('bqghd,bngd->bghqn', qb, kc).astype(jnp.float32) * scale
        lc = lc + jnp.moveaxis(tbl[rel_bucket(dist_c)], (2, 3), (0, 1))[None]
        pc, _ = masked_softmax(lc, (dist_c >= 0)[None, None, None])
        o_c = jnp.einsum('bghqn,bngd->bqghd', pc.astype(vc.dtype), vc)
        imp = jnp.einsum('bghqn,nj->bgqj', pc, overlap)
        cur = pos // SEL_BLOCK
        valid = jj[None, :] <= cur[:, None]
        forced = (jj[None, :] == 0) | (jj[None, :] == cur[:, None]) | (jj[None, :] == cur[:, None] - 1)
        imp = jnp.where(forced[None, None], FORCE_SCORE, jnp.where(valid[None, None], imp, -1.0))
        _, sel = lax.top_k(imp, k_sel)
        tok = (sel[..., None] * SEL_BLOCK + jnp.arange(SEL_BLOCK)).reshape(B, NSA_KV, qb_size, k_sel * SEL_BLOCK)
        tok_c = jnp.minimum(tok, L - 1)
        kk = ks[b_ix, tok_c, g_ix]
        vv = vs[b_ix, tok_c, g_ix]
        dist_s = pos[None, None, :, None] - tok
        ls = jnp.einsum('bqghd,bgqkd->bghqk', qb, kk).astype(jnp.float32) * scale
        ls = ls + jnp.moveaxis(tbl_g[g_ix, rel_bucket(dist_s)], -1, 2)
        ps, _ = masked_softmax(ls, (dist_s >= 0)[:, :, None])
        o_s = jnp.einsum('bghqk,bgqkd->bqghd', ps.astype(vv.dtype), vv)
        return o_c, o_s

    o_c, o_s = lax.map(block, (qblocks, pblocks))
    o_c = o_c.swapaxes(0, 1).reshape(B, Tq, H, dh)
    o_s = o_s.swapaxes(0, 1).reshape(B, Tq, H, dh)
    return o_c, o_s


def window_attn(q, k, v, q_pos, k_pos, rel_bias):
    B, Tq, H, dh = q.shape
    Lk = k.shape[1]
    W = NSA_WINDOW
    qb_size = math.gcd(Tq, WIN_QBLOCK)
    nb = Tq // qb_size
    span = qb_size + W
    off = Lk - Tq
    pad = ((0, 0), (W, 0), (0, 0), (0, 0))
    kp = jnp.pad(k, pad)
    vp = jnp.pad(v, pad)
    kpos = jnp.concatenate([jnp.full((W,), NEG_POS, jnp.int32), k_pos.astype(jnp.int32)])
    tbl = rel_bias.astype(jnp.float32).reshape(RP_BUCKETS, NSA_KV, NSA_GROUP)
    qblocks = q.reshape(B, nb, qb_size, NSA_KV, NSA_GROUP, dh).swapaxes(0, 1)
    pblocks = q_pos.reshape(nb, qb_size)
    starts = off + jnp.arange(nb, dtype=jnp.int32) * qb_size
    scale = dh ** -0.5

    def block(args):
        qb, pos, st = args
        kb = lax.dynamic_slice_in_dim(kp, st, span, axis=1)
        vb = lax.dynamic_slice_in_dim(vp, st, span, axis=1)
        kpb = lax.dynamic_slice_in_dim(kpos, st, span, axis=0)
        dist = pos[:, None] - kpb[None, :]
        logits = jnp.einsum('bqghd,bkgd->bghqk', qb, kb).astype(jnp.float32) * scale
        logits = logits + jnp.moveaxis(tbl[rel_bucket(dist)], (2, 3), (0, 1))[None]
        p, _ = masked_softmax(logits, (dist >= 0) & (dist <= W))
        return jnp.einsum('bghqk,bkgd->bqghd', p.astype(vb.dtype), vb)

    o = lax.map(block, (qblocks, pblocks, starts))
    return o.swapaxes(0, 1).reshape(B, Tq, H, dh)


def dilated_group(q, k, v, offs, bias_h):
    B, Tq, h, dh = q.shape
    Lk = k.shape[1]
    M = offs.shape[0]
    rows = (Lk - Tq) + jnp.arange(Tq, dtype=jnp.int32)[:, None] - offs[None, :]
    qb_size = math.gcd(Tq, DIL_QBLOCK)
    nb = Tq // qb_size
    qblocks = q.reshape(B, nb, qb_size, h, dh).swapaxes(0, 1)
    rblocks = rows.reshape(nb, qb_size, M)
    bias = bias_h.T[None, :, None, :]
    scale = dh ** -0.5

    def block(args):
        qb, r = args
        rc = jnp.maximum(r, 0)
        kk = k[:, rc]
        vv = v[:, rc]
        logits = jnp.einsum('bqhd,bqmhd->bhqm', qb, kk).astype(jnp.float32) * scale + bias
        p, lse = masked_softmax(logits, (r >= 0)[None, None])
        return jnp.einsum('bhqm,bqmhd->bqhd', p.astype(vv.dtype), vv), lse

    o, lse = lax.map(block, (qblocks, rblocks))
    o = o.swapaxes(0, 1).reshape(B, Tq, h, dh)
    lse = lse.transpose(1, 0, 3, 2).reshape(B, Tq, h)
    return o, lse


def mix_even(xn, past, lb, w_in, w_out, hg_norm, qk_norm, cmp_pe, cmp_w1, cmp_w2, rel_bias):
    B, T, _ = xn.shape
    hq, hf, hi, hgate, nq, kcr, vcr, ksr, vsr, kwr, vwr, ng = split_cols(xn @ w_in, EVEN_WIDTHS)
    lb = lb.reshape(HG_HEADS, HG_DK)
    log_f = jnp.log(lb + (1.0 - lb) * jax.nn.sigmoid(hf.astype(jnp.float32).reshape(B, T, HG_HEADS, HG_DK)))
    s0 = jnp.zeros((B, HG_HEADS, HG_DK, HG_DV), jnp.float32) if past is None else past[0].astype(jnp.float32)
    o_h, s_fin = hgrn2_scan(hq.astype(jnp.float32).reshape(B, T, HG_HEADS, HG_DK), log_f,
                            hi.astype(jnp.float32).reshape(B, T, HG_HEADS, HG_DV), s0)
    o_h = rms_norm(o_h, hg_norm) * jax.nn.silu(hgate.astype(jnp.float32).reshape(B, T, HG_HEADS, HG_DV))
    o_h = o_h.reshape(B, T, -1).astype(xn.dtype)
    def heads(a, n):
        return a.reshape(B, T, n, HEAD_DIM)
    q = rms_norm(heads(nq, NSA_HEADS), qk_norm[0])
    cmp_rows = jnp.stack([heads(kcr, NSA_KV), heads(vcr, NSA_KV)], axis=2)
    slc_rows = jnp.stack([rms_norm(heads(ksr, NSA_KV), qk_norm[2]), heads(vsr, NSA_KV)], axis=2)
    win_rows = jnp.stack([rms_norm(heads(kwr, NSA_KV), qk_norm[3]), heads(vwr, NSA_KV)], axis=2)
    if past is None:
        P = 0
        cmp_all, slc_all, win_all = cmp_rows, slc_rows, win_rows
        new_win = win_rows[:, -min(NSA_WINDOW, T):]
    else:
        P = past[1].shape[1]
        cmp_all = jnp.concatenate([past[1], cmp_rows], axis=1)
        slc_all = jnp.concatenate([past[2], slc_rows], axis=1)
        win_all = jnp.concatenate([past[3], win_rows], axis=1)
        new_win = win_all[:, -past[3].shape[1]:]
    q_pos = P + jnp.arange(T, dtype=jnp.int32)
    L_win = win_all.shape[1]
    k_pos_win = (P + T - L_win) + jnp.arange(L_win, dtype=jnp.int32)
    kc = rms_norm(nsa_compress(cmp_all[:, :, 0], cmp_pe[0], cmp_w1[0], cmp_w2[0]), qk_norm[1])
    vc = nsa_compress(cmp_all[:, :, 1], cmp_pe[1], cmp_w1[1], cmp_w2[1])
    o_c, o_s = nsa_cmp_sel(q, q_pos, kc, vc, slc_all[:, :, 0], slc_all[:, :, 1], rel_bias)
    o_w = window_attn(q, win_all[:, :, 0], win_all[:, :, 1], q_pos, k_pos_win, rel_bias)
    g = jax.nn.sigmoid(ng.astype(jnp.float32)).reshape(B, T, NSA_HEADS, 3)
    o_n = g[..., 0:1] * o_c + g[..., 1:2] * o_s + g[..., 2:3] * o_w
    out = jnp.concatenate([o_h, o_n.reshape(B, T, -1).astype(xn.dtype)], axis=-1) @ w_out
    return out, (s_fin, cmp_rows, slc_rows, new_win)


def mix_odd(xn, past, w_in, w_out, qk_norm, rel_bias):
    B, T, _ = xn.shape
    q, k, v = split_cols(xn @ w_in, ODD_WIDTHS)
    shp = (B, T, DIL_GROUPS, DIL_HEADS, HEAD_DIM)
    q = rms_norm(q.reshape(shp), qk_norm[0])
    k = rms_norm(k.reshape(shp), qk_norm[1])
    v = v.reshape(shp)
    tbl = rel_bias.astype(jnp.float32)
    outs, lses, bufs = [], [], []
    for g, (win, dil) in enumerate(DIL_PATTERN):
        rows = jnp.stack([k[:, :, g], v[:, :, g]], axis=2)
        if past is None:
            kv_all = rows
            bufs.append(rows[:, -min(win, T):])
        else:
            kv_all = jnp.concatenate([past[g], rows], axis=1)
            bufs.append(kv_all[:, -past[g].shape[1]:])
        offs = jnp.arange(win // dil + 1, dtype=jnp.int32) * dil
        bias_h = tbl[rel_bucket(offs)][:, g * DIL_HEADS:(g + 1) * DIL_HEADS]
        o, lse = dilated_group(q[:, :, g], kv_all[:, :, 0], kv_all[:, :, 1], offs, bias_h)
        outs.append(o)
        lses.append(lse)
    w = jax.nn.softmax(jnp.stack(lses, axis=0), axis=0)
    o = jnp.sum(w[..., None] * jnp.stack(outs, axis=0), axis=0)
    return o.reshape(B, T, -1).astype(xn.dtype) @ w_out, bufs


def setup_inputs(seed: int = 0) -> dict:
    key = jax.random.key(seed)
    keys = iter(jax.random.split(key, 40))

    def nrm(shape, scale=1.0):
        return jax.random.normal(next(keys), shape, jnp.float32) * scale

    def gain(shape):
        return 1.0 + 0.02 * nrm(shape)

    n_pages = PAST_LEN // PAGE_SIZE
    n_used = DEC_BATCH * n_pages
    n_phys = n_used + n_used // 4
    page_table = jax.random.permutation(next(keys), n_phys)[:n_used].reshape(DEC_BATCH, n_pages).astype(jnp.int32)
    return {
        'x_prompt': nrm((BATCH, SEQ, D_MODEL)),
        'x_sample': nrm((DEC_BATCH, DEC_SEQ, D_MODEL)),
        'p_prompt': nrm((DEPTH, BATCH, SEQ, PLE_DIM)),
        'p_sample': nrm((DEPTH, DEC_BATCH, DEC_SEQ, PLE_DIM)),
        'state_hgrn': nrm((N_EVEN, DEC_BATCH, HG_HEADS, HG_DK, HG_DV), 0.5),
        'cache_nsa_cmp_kv': nrm((N_EVEN, n_phys, PAGE_SIZE, 2, NSA_KV, HEAD_DIM)),
        'cache_nsa_slc_kv': nrm((N_EVEN, n_phys, PAGE_SIZE, 2, NSA_KV, HEAD_DIM)),
        'state_nsa_win_kv': nrm((N_EVEN, DEC_BATCH, min(NSA_WINDOW, PAST_LEN), 2, NSA_KV, HEAD_DIM)),
        'state_dil_kv_0': nrm((N_ODD, DEC_BATCH, min(DIL_PATTERN[0][0], PAST_LEN), 2, DIL_HEADS, HEAD_DIM)),
        'state_dil_kv_1': nrm((N_ODD, DEC_BATCH, min(DIL_PATTERN[1][0], PAST_LEN), 2, DIL_HEADS, HEAD_DIM)),
        'state_dil_kv_2': nrm((N_ODD, DEC_BATCH, min(DIL_PATTERN[2][0], PAST_LEN), 2, DIL_HEADS, HEAD_DIM)),
        'page_table': page_table,
        'rel_bias': nrm((RP_BUCKETS, N_BIAS_HEADS), 0.5),
        'norm_ffn1': gain((DEPTH, D_MODEL)),
        'norm_mix': gain((DEPTH, D_MODEL)),
        'norm_ffn2': gain((DEPTH, D_MODEL)),
        'norm_ple': gain((DEPTH, D_MODEL)),
        'w_ffn1_in': nrm((DEPTH, D_MODEL, 2 * D_FF), D_MODEL ** -0.5),
        'w_ffn1_out': nrm((DEPTH, D_FF, D_MODEL), D_FF ** -0.5),
        'w_ffn2_in': nrm((DEPTH, D_MODEL, 2 * D_FF), D_MODEL ** -0.5),
        'w_ffn2_out': nrm((DEPTH, D_FF, D_MODEL), D_FF ** -0.5),
        'w_ple_gate': nrm((DEPTH, D_MODEL, D_MODEL), D_MODEL ** -0.5),
        'w_ple_proj': nrm((DEPTH, PLE_DIM, D_MODEL), PLE_DIM ** -0.5),
        'w_in_even': nrm((N_EVEN, D_MODEL, EVEN_IN), D_MODEL ** -0.5),
        'w_out_even': nrm((N_EVEN, EVEN_OUT, D_MODEL), EVEN_OUT ** -0.5),
        'hgrn_lb_logits': nrm((N_EVEN + 1, HG_HEADS * HG_DK)),
        'hgrn_norm': gain((N_EVEN, HG_DV)),
        'nsa_qk_norm': gain((N_EVEN, 4, HEAD_DIM)),
        'nsa_cmp_pe': nrm((N_EVEN, 2, CMP_BLOCK, HEAD_DIM), 0.1),
        'nsa_cmp_w1': nrm((N_EVEN, 2, CMP_BLOCK * HEAD_DIM, HEAD_DIM), (CMP_BLOCK * HEAD_DIM) ** -0.5),
        'nsa_cmp_w2': nrm((N_EVEN, 2, HEAD_DIM, HEAD_DIM), HEAD_DIM ** -0.5),
        'w_in_odd': nrm((N_ODD, D_MODEL, ODD_IN), D_MODEL ** -0.5),
        'w_out_odd': nrm((N_ODD, ODD_OUT, D_MODEL), ODD_OUT ** -0.5),
        'dil_qk_norm': gain((N_ODD, 2, HEAD_DIM)),
    }


def reference(x_prompt, x_sample, p_prompt, p_sample, state_hgrn, cache_nsa_cmp_kv, cache_nsa_slc_kv,
              state_nsa_win_kv, state_dil_kv_0, state_dil_kv_1, state_dil_kv_2, page_table, rel_bias,
              norm_ffn1, norm_mix, norm_ffn2, norm_ple, w_ffn1_in, w_ffn1_out, w_ffn2_in, w_ffn2_out,
              w_ple_gate, w_ple_proj, w_in_even, w_out_even, hgrn_lb_logits, hgrn_norm, nsa_qk_norm,
              nsa_cmp_pe, nsa_cmp_w1, nsa_cmp_w2, w_in_odd, w_out_odd, dil_qk_norm):
    dil_states = (state_dil_kv_0, state_dil_kv_1, state_dil_kv_2)

    def gather_pages(pool):
        rows = pool[page_table]
        return rows.reshape(page_table.shape[0], -1, *pool.shape[2:])

    def run(h, p, sample):
        hg, cmp, slc, win = [], [], [], []
        dil = ([], [], [])
        for li in range(DEPTH):
            h = h + 0.5 * swiglu(rms_norm(h, norm_ffn1[li]), w_ffn1_in[li], w_ffn1_out[li])
            xn = rms_norm(h, norm_mix[li])
            if li % 2 == 0:
                ei = li // 2
                past = None
                if sample:
                    past = (state_hgrn[ei], gather_pages(cache_nsa_cmp_kv[ei]),
                            gather_pages(cache_nsa_slc_kv[ei]), state_nsa_win_kv[ei])
                lb = jnp.cumsum(jax.nn.softmax(hgrn_lb_logits.astype(jnp.float32), axis=0), axis=0)[ei]
                mix, (s_new, c_rows, s_rows, w_buf) = mix_even(
                    xn, past, lb, w_in_even[ei], w_out_even[ei], hgrn_norm[ei], nsa_qk_norm[ei],
                    nsa_cmp_pe[ei], nsa_cmp_w1[ei], nsa_cmp_w2[ei], rel_bias)
                hg.append(s_new)
                cmp.append(c_rows)
                slc.append(s_rows)
                win.append(w_buf)
            else:
                oi = li // 2
                past = tuple(s[oi] for s in dil_states) if sample else None
                mix, bufs = mix_odd(xn, past, w_in_odd[oi], w_out_odd[oi], dil_qk_norm[oi], rel_bias)
                for lst, b in zip(dil, bufs):
                    lst.append(b)
            h = h + mix
            h = h + 0.5 * swiglu(rms_norm(h, norm_ffn2[li]), w_ffn2_in[li], w_ffn2_out[li])
            gate = jax.nn.sigmoid(rms_norm(h, norm_ple[li]) @ w_ple_gate[li])
            h = h + gate * (p[li].astype(h.dtype) @ w_ple_proj[li])
        return h, (jnp.stack(hg), jnp.stack(cmp), jnp.stack(slc), jnp.stack(win),
                   jnp.stack(dil[0]), jnp.stack(dil[1]), jnp.stack(dil[2]))

    y_prompt, (hg_p, cmp_p, slc_p, win_p, d0_p, d1_p, d2_p) = run(x_prompt, p_prompt, False)
    y_sample, (hg_s, cmp_s, slc_s, win_s, d0_s, d1_s, d2_s) = run(x_sample, p_sample, True)
    return (y_prompt, y_sample, hg_p, hg_s, cmp_p, cmp_s, slc_p, slc_s, win_p, win_s,
            d0_p, d0_s, d1_p, d1_s, d2_p, d2_s)
```

```python
import functools
import math

import numpy as np
import jax
import jax.numpy as jnp
from jax import lax
from jax.experimental import pallas as pl
from jax.experimental.pallas import tpu as pltpu

F32 = jnp.float32
BF16 = jnp.bfloat16
HIGHEST = lax.Precision.HIGHEST

V7X_VMEM_BYTES = 64 * 1024 * 1024
VMEM_LIMIT = V7X_VMEM_BYTES * 7 // 8
LANES = 128

EPS = 1e-6
NEG = -1e30
NEG_TEST = -1e29

HEAD_DIM = 64
HG_HEADS = 8
HG_CHUNK = 64
NSA_HEADS = 12
NSA_KV = 3
NSA_GROUP = NSA_HEADS // NSA_KV
CMP_BLOCK = 32
CMP_STRIDE = 16
SEL_BLOCK = 64
SEL_TOPK = 16
NSA_WINDOW = 512
FORCE_SCORE = 1e4
PAGE = 128
DIL_PATTERN = ((128, 1), (512, 4), (2048, 16))
DIL_HEADS = 4
RP_BUCKETS = 32
RP_MAX_DIST = 2048


def _cparams(*sem):
    return pltpu.CompilerParams(dimension_semantics=sem, vmem_limit_bytes=VMEM_LIMIT)


def _const(shape):
    nd = len(shape)
    return pl.BlockSpec(shape, lambda *_: (0,) * nd, pipeline_mode=pl.Buffered(1))


def _dot(a, b):
    return jnp.dot(a, b, preferred_element_type=F32)


def _dot_nt(a, b):
    return lax.dot_general(a, b, (((1,), (1,)), ((), ())), preferred_element_type=F32)


def _dot_tn(a, b):
    return lax.dot_general(a, b, (((0,), (0,)), ((), ())), preferred_element_type=F32)


def _rms(x, gain):
    return x * lax.rsqrt(jnp.mean(x * x, axis=-1, keepdims=True) + EPS) * gain


def _seg_ms(x, bd):
    sq = x * x
    hi = sq.astype(BF16)
    lo = (sq - hi.astype(F32)).astype(BF16)
    return (_dot(hi, bd) + _dot(lo, bd)) * (1.0 / HEAD_DIM)


def _seg_rms(x, bd, gain):
    return x * lax.rsqrt(_seg_ms(x, bd) + EPS) * gain


def _norm_k_rows(x, bd, gain, kmask):
    r = lax.rsqrt(_seg_ms(x, bd) + EPS)
    return x * jnp.where(kmask > 0.5, r, 1.0) * gain


def _block_diag_ones(width):
    i = np.arange(width) // HEAD_DIM
    return jnp.asarray((i[:, None] == i[None, :]).astype(np.float32), BF16)


def _rel_bucket(dist):
    exact = RP_BUCKETS // 2
    d = jnp.maximum(dist, 0)
    log_ratio = jnp.log(jnp.maximum(d, 1).astype(F32) / exact) / math.log(RP_MAX_DIST / exact)
    large = jnp.minimum(exact + (log_ratio * (RP_BUCKETS - exact)).astype(jnp.int32), RP_BUCKETS - 1)
    return jnp.where(d < exact, d, large)


def _bias_tile(rel_bias, dist, valid):
    t = rel_bias.astype(F32)[_rel_bucket(dist)]
    t = jnp.where(valid[..., None], t, NEG)
    return jnp.moveaxis(t, -1, 0)


def _softmax_piece(s, mask, m_prev, l_prev, acc_prev, v):
    m_new = jnp.maximum(m_prev, jnp.max(s, axis=-1, keepdims=True))
    alpha = jnp.exp(m_prev - m_new)
    p = jnp.where(mask, jnp.exp(s - m_new), 0.0)
    l_new = alpha * l_prev + jnp.sum(p, axis=-1, keepdims=True)
    acc_new = alpha * acc_prev + _dot(p.astype(BF16), v)
    return m_new, l_new, acc_new


def _ffn_body(x_ref, g_ref, win_ref, wout_ref, *rest, dff, ple):
    x = x_ref[...]
    xn = _rms(x, g_ref[...]).astype(BF16)
    gu = _dot(xn, win_ref[...])
    gg = gu[:, :dff]
    a = (gg * jax.nn.sigmoid(gg) * gu[:, dff:]).astype(BF16)
    h = x + 0.5 * _dot(a, wout_ref[...])
    if ple:
        p_ref, gp_ref, wg_ref, wp_ref, o_ref = rest
        hn = _rms(h, gp_ref[...]).astype(BF16)
        gate = jax.nn.sigmoid(_dot(hn, wg_ref[...]))
        h = h + gate * _dot(p_ref[...].astype(BF16), wp_ref[...])
    else:
        (o_ref,) = rest
    o_ref[...] = h


def _ffn(h, gain, w_in, w_out, ple=None, *, tm):
    n, d = h.shape
    dff = w_out.shape[0]
    row = lambda i: (i, 0)
    in_specs = [pl.BlockSpec((tm, d), row), _const((1, d)), _const(w_in.shape), _const(w_out.shape)]
    args = [h, gain.reshape(1, d), w_in, w_out]
    if ple is not None:
        p, gp, wg, wp = ple
        in_specs += [pl.BlockSpec((tm, p.shape[1]), row), _const((1, d)), _const(wg.shape), _const(wp.shape)]
        args += [p, gp.reshape(1, d), wg, wp]
    return pl.pallas_call(
        functools.partial(_ffn_body, dff=dff, ple=ple is not None),
        grid=(n // tm,), in_specs=in_specs, out_specs=pl.BlockSpec((tm, d), row),
        out_shape=jax.ShapeDtypeStruct((n, d), F32), compiler_params=_cparams("parallel"),
        name="ffn_ple" if ple is not None else "ffn",
    )(*args)


EVEN_HG = 4 * HG_HEADS * HEAD_DIM
EVEN_Q = NSA_HEADS * HEAD_DIM
ROWS_NSA = 2 * NSA_KV * HEAD_DIM
EVEN_PAD = EVEN_HG + EVEN_Q + 3 * ROWS_NSA + LANES


def _inproj_even_body(x_ref, g_ref, w_ref, gk_ref, km_ref, bd_ref,
                      hg_ref, q_ref, cmp_ref, slc_ref, win_ref, slcg_ref, wing_ref, gate_ref):
    xn = _rms(x_ref[...], g_ref[...]).astype(BF16)
    p = _dot(xn, w_ref[...])
    c0 = EVEN_HG
    c1 = c0 + EVEN_Q
    hg_ref[...] = p[:, :c0]
    q_ref[...] = p[:, c0:c1]
    cmp_ref[...] = p[:, c1:c1 + ROWS_NSA]
    bd = bd_ref[...]
    km = km_ref[...]
    slc = _norm_k_rows(p[:, c1 + ROWS_NSA:c1 + 2 * ROWS_NSA], bd, gk_ref[0:1, :], km)
    win = _norm_k_rows(p[:, c1 + 2 * ROWS_NSA:c1 + 3 * ROWS_NSA], bd, gk_ref[1:2, :], km)
    slc_ref[...] = slc
    win_ref[...] = win
    sig = jax.nn.sigmoid(p[:, c1 + 3 * ROWS_NSA:])
    kw = NSA_KV * HEAD_DIM
    for g in range(NSA_KV):
        ks = slice(HEAD_DIM * g, HEAD_DIM * (g + 1))
        vs = slice(kw + HEAD_DIM * g, kw + HEAD_DIM * (g + 1))
        slcg_ref[g] = jnp.concatenate([slc[:, ks], slc[:, vs]], axis=1)
        wing_ref[g] = jnp.concatenate([win[:, ks], win[:, vs]], axis=1)
        gate_ref[g] = sig if g == 0 else pltpu.roll(sig, LANES - 3 * NSA_GROUP * g, 1)


def _inproj_even(h, gain, w_pad, qk_norm, *, tm):
    n, d = h.shape
    kw = NSA_KV * HEAD_DIM
    ones = jnp.ones((kw,), F32)
    gk = jnp.stack([jnp.concatenate([jnp.tile(qk_norm[2], NSA_KV), ones]),
                    jnp.concatenate([jnp.tile(qk_norm[3], NSA_KV), ones])])
    km = jnp.concatenate([ones, 0.0 * ones]).reshape(1, ROWS_NSA)
    row = lambda i: (i, 0)
    grp = lambda i: (0, i, 0)
    shp = lambda *s: jax.ShapeDtypeStruct(s, F32)
    return pl.pallas_call(
        _inproj_even_body, grid=(n // tm,),
        in_specs=[pl.BlockSpec((tm, d), row), _const((1, d)), _const(w_pad.shape), _const((2, ROWS_NSA)),
                  _const((1, ROWS_NSA)), _const((ROWS_NSA, ROWS_NSA))],
        out_specs=[pl.BlockSpec((tm, EVEN_HG), row), pl.BlockSpec((tm, EVEN_Q), row),
                   pl.BlockSpec((tm, ROWS_NSA), row), pl.BlockSpec((tm, ROWS_NSA), row),
                   pl.BlockSpec((tm, ROWS_NSA), row), pl.BlockSpec((NSA_KV, tm, LANES), grp),
                   pl.BlockSpec((NSA_KV, tm, LANES), grp), pl.BlockSpec((NSA_KV, tm, LANES), grp)],
        out_shape=[shp(n, EVEN_HG), shp(n, EVEN_Q), shp(n, ROWS_NSA), shp(n, ROWS_NSA), shp(n, ROWS_NSA),
                   shp(NSA_KV, n, LANES), shp(NSA_KV, n, LANES), shp(NSA_KV, n, LANES)],
        compiler_params=_cparams("parallel"), name="inproj_even",
    )(h, gain.reshape(1, d), w_pad, gk, km, _block_diag_ones(ROWS_NSA))


ODD_W = 3 * DIL_HEADS * HEAD_DIM
ODD_GW = DIL_HEADS * HEAD_DIM


def _inproj_odd_body(x_ref, g_ref, w_ref, gq_ref, gk_ref, bd_ref, q_ref, kv_ref):
    xn = _rms(x_ref[...], g_ref[...]).astype(BF16)
    p = _dot(xn, w_ref[...])
    bd = bd_ref[...]
    q = _seg_rms(p[:, :ODD_W], bd, gq_ref[...]) * (HEAD_DIM ** -0.5)
    k = _seg_rms(p[:, ODD_W:2 * ODD_W], bd, gk_ref[...])
    v = p[:, 2 * ODD_W:]
    for g in range(3):
        s = slice(ODD_GW * g, ODD_GW * (g + 1))
        q_ref[g] = q[:, s]
        kv_ref[g] = jnp.concatenate([k[:, s], v[:, s]], axis=1)


def _inproj_odd(h, gain, w, qk_norm, *, tm):
    n, d = h.shape
    row = lambda i: (i, 0)
    nh = 3 * DIL_HEADS
    return pl.pallas_call(
        _inproj_odd_body, grid=(n // tm,),
        in_specs=[pl.BlockSpec((tm, d), row), _const((1, d)), _const(w.shape), _const((1, ODD_W)),
                  _const((1, ODD_W)), _const((ODD_W, ODD_W))],
        out_specs=[pl.BlockSpec((3, tm, ODD_GW), lambda i: (0, i, 0)),
                   pl.BlockSpec((3, tm, 2 * ODD_GW), lambda i: (0, i, 0))],
        out_shape=[jax.ShapeDtypeStruct((3, n, ODD_GW), F32), jax.ShapeDtypeStruct((3, n, 2 * ODD_GW), F32)],
        compiler_params=_cparams("parallel"), name="inproj_odd",
    )(h, gain.reshape(1, d), w, jnp.tile(qk_norm[0], nh).reshape(1, ODD_W),
      jnp.tile(qk_norm[1], nh).reshape(1, ODD_W), _block_diag_ones(ODD_W))


def _outproj2_body(h_ref, a_ref, b_ref, wa_ref, wb_ref, o_ref):
    o_ref[...] = (h_ref[...] + _dot(a_ref[...].astype(BF16), wa_ref[...])
                  + _dot(b_ref[...].astype(BF16), wb_ref[...]))


def _outproj2(h, a, b, wa, wb, *, tm):
    n, d = h.shape
    row = lambda i: (i, 0)
    return pl.pallas_call(
        _outproj2_body, grid=(n // tm,),
        in_specs=[pl.BlockSpec((tm, d), row), pl.BlockSpec((tm, a.shape[1]), row),
                  pl.BlockSpec((tm, b.shape[1]), row), _const(wa.shape), _const(wb.shape)],
        out_specs=pl.BlockSpec((tm, d), row), out_shape=jax.ShapeDtypeStruct((n, d), F32),
        compiler_params=_cparams("parallel"), name="outproj_even",
    )(h, a, b, wa, wb)


def _outproj_merge_body(h_ref, o0_ref, o1_ref, o2_ref, l0_ref, l1_ref, l2_ref, w_ref, o_ref):
    mixed = _merge_groups([o0_ref[...], o1_ref[...], o2_ref[...]], [l0_ref[...], l1_ref[...], l2_ref[...]])
    o_ref[...] = h_ref[...] + _dot(mixed.astype(BF16), w_ref[...])


def _outproj_merge(h, outs, lses, w, *, tm):
    n, d = h.shape
    row = lambda i: (i, 0)
    part = pl.BlockSpec((tm, ODD_GW), row)
    return pl.pallas_call(
        _outproj_merge_body, grid=(n // tm,),
        in_specs=[pl.BlockSpec((tm, d), row)] + [part] * 6 + [_const(w.shape)],
        out_specs=pl.BlockSpec((tm, d), row), out_shape=jax.ShapeDtypeStruct((n, d), F32),
        compiler_params=_cparams("parallel"), name="outproj_odd",
    )(h, *outs, *lses, w)


def _hgrn_body(hq_ref, hf_ref, hi_ref, hgt_ref, lb_ref, gn_ref, bd_ref, s0_ref, o_ref, sfin_ref, st_ref, *, C):
    c = pl.program_id(1)

    @pl.when(c == 0)
    def _():
        st_ref[...] = s0_ref[0]

    dk = HEAD_DIM
    lb = lb_ref[...]
    f = lb + (1.0 - lb) * jax.nn.sigmoid(hf_ref[...])
    lf = jnp.log(f)
    kk = 1.0 - f
    ti = lax.broadcasted_iota(jnp.int32, (C, C), 0)
    si = lax.broadcasted_iota(jnp.int32, (C, C), 1)
    b = jnp.dot(jnp.where(si <= ti, 1.0, 0.0), lf, precision=HIGHEST, preferred_element_type=F32)
    q = hq_ref[...]
    v = hi_ref[...].astype(BF16)
    heads = range(HG_HEADS)
    hs = [slice(dk * h, dk * (h + 1)) for h in heads]

    qb = q.astype(BF16)
    kb = kk.astype(BF16)
    a = [jnp.where(ti == si, _dot_nt(qb[:, hs[h]], kb[:, hs[h]]), 0.0) for h in heads]
    lvl = 1
    while (1 << lvl) <= C:
        m = 1 << lvl
        half = m // 2
        ref_row = ((ti >> lvl) << lvl) + (half - 1)
        r = jnp.dot(jnp.where(si == ref_row, 1.0, 0.0), b, precision=HIGHEST, preferred_element_type=F32)
        qt = (q * jnp.exp(jnp.minimum(b - r, 0.0))).astype(BF16)
        kt = (kk * jnp.exp(jnp.minimum(r - b, 0.0))).astype(BF16)
        pair = ((ti >> lvl) == (si >> lvl)) & ((ti & (m - 1)) >= half) & ((si & (m - 1)) < half)
        a = [a[h] + jnp.where(pair, _dot_nt(qt[:, hs[h]], kt[:, hs[h]]), 0.0) for h in heads]
        lvl += 1

    b_last = b[C - 1:C, :]
    qe = (q * jnp.exp(b)).astype(BF16)
    kdec = (kk * jnp.exp(b_last - b)).astype(BF16)
    e_last = jnp.exp(b_last)
    outs = []
    for h in heads:
        st = st_ref[h]
        outs.append(_dot(a[h].astype(BF16), v[:, hs[h]]) + _dot_nt(qe[:, hs[h]], st.astype(BF16)))
        st_ref[h] = st * e_last[:, hs[h]] + _dot_tn(v[:, hs[h]], kdec[:, hs[h]])
    o = jnp.concatenate(outs, axis=1)
    gt = hgt_ref[...]
    o_ref[...] = _seg_rms(o, bd_ref[...], gn_ref[...]) * (gt * jax.nn.sigmoid(gt))

    @pl.when(c == pl.num_programs(1) - 1)
    def _():
        sfin_ref[0] = st_ref[...]


def _hgrn(hg, lb, hg_norm, s0, *, batch, seq):
    C = math.gcd(seq, HG_CHUNK)
    nc = seq // C
    w = HG_HEADS * HEAD_DIM
    col = lambda j: pl.BlockSpec((C, w), lambda b, c: (b * nc + c, j))
    st_spec = pl.BlockSpec((1, HG_HEADS, HEAD_DIM, HEAD_DIM), lambda b, c: (b, 0, 0, 0))
    o, sfin = pl.pallas_call(
        functools.partial(_hgrn_body, C=C), grid=(batch, nc),
        in_specs=[col(0), col(1), col(2), col(3), _const((1, w)), _const((1, w)), _const((w, w)), st_spec],
        out_specs=[pl.BlockSpec((C, w), lambda b, c: (b * nc + c, 0)), st_spec],
        out_shape=[jax.ShapeDtypeStruct((batch * seq, w), F32),
                   jax.ShapeDtypeStruct((batch, HG_HEADS, HEAD_DIM, HEAD_DIM), F32)],
        scratch_shapes=[pltpu.VMEM((HG_HEADS, HEAD_DIM, HEAD_DIM), F32)],
        compiler_params=_cparams("parallel", "arbitrary"), name="hgrn_scan",
    )(hg, hg, hg, hg, lb.reshape(1, w), jnp.tile(hg_norm, HG_HEADS).reshape(1, w), _block_diag_ones(w),
      jnp.swapaxes(s0, 2, 3))
    return o, jnp.swapaxes(sfin, 2, 3)


CHUNKS_PER_PAGE = PAGE // CMP_STRIDE
PAGE_LANES = CMP_STRIDE * ROWS_NSA
CMP_HID = 2 * NSA_KV * 2 * HEAD_DIM
CMP_PP = 8


def _compress_body(pt_ref, *refs, n_chunks):
    pages = refs[:CMP_PP]
    wbig_ref, pe_ref, w2_ref, gk_ref, km_ref, bd_ref, rows_ref, grp_ref, xs_ref = refs[CMP_PP:]
    j = pl.program_id(1)
    for i in range(0, CMP_PP, 2):
        blk = jnp.concatenate([pages[i][0], pages[i + 1][0]], axis=0).astype(BF16)
        r0 = pl.multiple_of((j * CMP_PP + i) * CHUNKS_PER_PAGE, 2 * CHUNKS_PER_PAGE)
        xs_ref[pl.ds(r0, 2 * CHUNKS_PER_PAGE), :] = blk

    @pl.when(j == pl.num_programs(1) - 1)
    def _():
        wbig = wbig_ref[...]
        y = _dot(xs_ref[...], wbig)
        ysh = pltpu.roll(y, n_chunks - 1, 0)
        hid = y + pltpu.roll(ysh, CMP_HID - HEAD_DIM, 1)
        cf = _dot(pe_ref[...].astype(BF16), wbig)
        lane = lax.broadcasted_iota(jnp.int32, (8, CMP_HID), 1)
        cs = jnp.where(((lane >> 6) & 1) == 0, jnp.broadcast_to(cf[0:1], (8, CMP_HID)),
                       jnp.broadcast_to(cf[1:2], (8, CMP_HID)))
        cs = cs + pltpu.roll(cs, CMP_HID - HEAD_DIM, 1)
        hid = hid + cs[0:1]
        act = (hid * jax.nn.sigmoid(hid)).astype(BF16)
        out = _dot(act, w2_ref[...])
        out = _norm_k_rows(out, bd_ref[...], gk_ref[...], km_ref[...])
        rows_ref[0] = out
        kw = NSA_KV * HEAD_DIM
        for g in range(NSA_KV):
            grp_ref[0, g] = jnp.concatenate([out[:, HEAD_DIM * g:HEAD_DIM * (g + 1)],
                                             out[:, kw + HEAD_DIM * g:kw + HEAD_DIM * (g + 1)]], axis=1)


def _compress_weights(cmp_pe, cmp_w1, cmp_w2, k_gain):
    ratio = CMP_BLOCK // CMP_STRIDE
    w1 = cmp_w1.reshape(2, ratio, CMP_STRIDE, HEAD_DIM, HEAD_DIM)
    eye_kv = jnp.eye(2, dtype=F32)
    eye_g = jnp.eye(NSA_KV, dtype=F32)
    wbig = jnp.einsum('krsde,kK,gG->skgdKGre', w1, eye_kv, eye_g).reshape(PAGE_LANES, CMP_HID).astype(BF16)
    pe = cmp_pe.reshape(2, ratio, CMP_STRIDE, HEAD_DIM)
    pe_rows = jnp.broadcast_to(pe.transpose(1, 2, 0, 3)[:, :, :, None, :],
                               (ratio, CMP_STRIDE, 2, NSA_KV, HEAD_DIM)).reshape(ratio, PAGE_LANES)
    pe_rows = jnp.concatenate([pe_rows, jnp.zeros((8 - ratio, PAGE_LANES), F32)], axis=0)
    sel_r0 = jnp.asarray([1.0, 0.0], F32)
    w2big = jnp.einsum('ked,kK,gG,r->kgreKGd', cmp_w2, eye_kv, eye_g, sel_r0).reshape(CMP_HID, ROWS_NSA).astype(BF16)
    kw = NSA_KV * HEAD_DIM
    gk = jnp.concatenate([jnp.tile(k_gain, NSA_KV), jnp.ones((kw,), F32)]).reshape(1, ROWS_NSA)
    km = jnp.concatenate([jnp.ones((kw,), F32), jnp.zeros((kw,), F32)]).reshape(1, ROWS_NSA)
    return wbig, pe_rows, w2big, gk, km


def _compress(pool, table, cw):
    wbig, pe_rows, w2big, gk, km = cw
    bsz, npg = table.shape
    n_chunks = npg * CHUNKS_PER_PAGE
    pool = pool.reshape(pool.shape[0], CHUNKS_PER_PAGE, PAGE_LANES)
    page = lambda i: pl.BlockSpec((1, CHUNKS_PER_PAGE, PAGE_LANES), lambda b, j, pt: (pt[b, j * CMP_PP + i], 0, 0))
    return pl.pallas_call(
        functools.partial(_compress_body, n_chunks=n_chunks),
        grid_spec=pltpu.PrefetchScalarGridSpec(
            num_scalar_prefetch=1, grid=(bsz, npg // CMP_PP),
            in_specs=[page(i) for i in range(CMP_PP)] + [
                _const(wbig.shape), _const(pe_rows.shape), _const(w2big.shape), _const((1, ROWS_NSA)),
                _const((1, ROWS_NSA)), _const((ROWS_NSA, ROWS_NSA))],
            out_specs=[pl.BlockSpec((1, n_chunks, ROWS_NSA), lambda b, j, pt: (b, 0, 0)),
                       pl.BlockSpec((1, NSA_KV, n_chunks, LANES), lambda b, j, pt: (b, 0, 0, 0))],
            scratch_shapes=[pltpu.VMEM((n_chunks, PAGE_LANES), BF16)]),
        out_shape=[jax.ShapeDtypeStruct((bsz, n_chunks, ROWS_NSA), F32),
                   jax.ShapeDtypeStruct((bsz, NSA_KV, n_chunks, LANES), F32)],
        compiler_params=_cparams("parallel", "arbitrary"), name="nsa_compress",
    )(table, *([pool] * CMP_PP), wbig, pe_rows, w2big, gk, km, _block_diag_ones(ROWS_NSA))


def _select_blocks(imp, pos, n_slc, k_sel):
    lane = lax.broadcasted_iota(jnp.int32, imp.shape, 1)
    cur = pos >> 6
    forced = (lane == 0) | (lane == cur) | (lane == cur - 1)
    score = jnp.where(forced, FORCE_SCORE, jnp.where(lane <= cur, imp, -1.0))
    rank = jnp.zeros(imp.shape, F32)
    for i in range(n_slc):
        ci = score[:, i:i + 1]
        rank = rank + jnp.where(ci > score, 1.0, jnp.where(ci == score, jnp.where(lane > i, 1.0, 0.0), 0.0))
    return jnp.where(lane < n_slc, jnp.where(rank < k_sel, 1.0, 0.0), 0.0)


NSA_TQ = 256


def _nsa_prompt_body(q_ref, gate_ref, kcv_ref, slc_ref, win_ref, gq_ref, bd_ref, bc_ref, bs_ref, ov_ref, ex_ref,
                     o_ref, sel_scr, m_scr, l_scr, acc_scr, *, seq, n_slc, k_sel):
    tq = NSA_TQ
    rows = NSA_GROUP * tq
    n_tiles = seq // tq
    hd = HEAD_DIM
    kc = kcv_ref[0, 0, :, :hd].astype(BF16)
    vc = kcv_ref[0, 0, :, hd:].astype(BF16)
    n_c = kc.shape[0]
    ii = lax.broadcasted_iota(jnp.int32, (tq, tq), 0)
    jj = lax.broadcasted_iota(jnp.int32, (tq, tq), 1)

    def flash(qs, kv_ref, t, lo, hi, extra_mask):
        m_scr[...] = jnp.full((rows, 1), NEG, F32)
        l_scr[...] = jnp.zeros((rows, 1), F32)
        acc_scr[...] = jnp.zeros((rows, hd), F32)

        def step(jt, carry):
            k0 = pl.multiple_of(jt * tq, tq)
            kv = kv_ref[0, pl.ds(k0, tq), :]
            bias = bs_ref[0, t - jt]
            ok = (bias > NEG_TEST).reshape(NSA_GROUP, tq, tq) & extra_mask(jt)[None]
            s = (_dot_nt(qs, kv[:, :hd].astype(BF16)) + bias).reshape(NSA_GROUP, tq, tq)
            s = jnp.where(ok, s, NEG).reshape(rows, tq)
            m, l, acc = _softmax_piece(s, ok.reshape(rows, tq), m_scr[...], l_scr[...], acc_scr[...],
                                       kv[:, hd:].astype(BF16))
            m_scr[...] = m
            l_scr[...] = l
            acc_scr[...] = acc
            return carry

        lax.fori_loop(lo, hi + 1, step, 0)
        return acc_scr[...] / jnp.maximum(l_scr[...], 1e-30)

    def q_tile(t, carry):
        r0 = pl.multiple_of(t * tq, tq)
        qn = _seg_rms(q_ref[pl.ds(r0, tq), :], bd_ref[...], gq_ref[...]) * (hd ** -0.5)
        qs = jnp.concatenate([qn[:, hd * h:hd * (h + 1)] for h in range(NSA_GROUP)], axis=0).astype(BF16)
        bias_c = bc_ref[0, :, pl.ds(r0, tq), :].reshape(rows, n_c)
        ok_c = bias_c > NEG_TEST
        s_c = _dot_nt(qs, kc) + bias_c
        m_c = jnp.max(s_c, axis=-1, keepdims=True)
        e_c = jnp.where(ok_c, jnp.exp(s_c - m_c), 0.0)
        p_c = (e_c / jnp.maximum(jnp.sum(e_c, axis=-1, keepdims=True), 1e-30)).astype(BF16)
        o_c = _dot(p_c, vc)
        d = _dot(p_c, ov_ref[...])
        imp = d[0:tq] + d[tq:2 * tq] + d[2 * tq:3 * tq] + d[3 * tq:4 * tq]
        pos = r0 + lax.broadcasted_iota(jnp.int32, (tq, 1), 0)
        sel = _select_blocks(imp, pos, n_slc, k_sel).astype(BF16)
        for jt in range(n_tiles):
            sel_scr[jt] = _dot(sel, ex_ref[:, jt * tq:(jt + 1) * tq])
        o_s = flash(qs, slc_ref, t, 0, t, lambda jt: sel_scr[jt] > 0.5)
        o_w = flash(qs, win_ref, t, jnp.maximum(t - NSA_WINDOW // tq, 0), t,
                    lambda jt: (t - jt) * tq + ii - jj <= NSA_WINDOW)
        gates = gate_ref[0, pl.ds(r0, tq), :]
        outs = []
        for h in range(NSA_GROUP):
            hr = slice(h * tq, (h + 1) * tq)
            outs.append(gates[:, 3 * h:3 * h + 1] * o_c[hr] + gates[:, 3 * h + 1:3 * h + 2] * o_s[hr]
                        + gates[:, 3 * h + 2:3 * h + 3] * o_w[hr])
        o_ref[pl.ds(r0, tq), :] = jnp.concatenate(outs, axis=1)
        return carry

    lax.fori_loop(0, n_tiles, q_tile, 0)


def _nsa_prompt_tables(rel_bias, seq, n_c):
    tq = NSA_TQ
    n_tiles = seq // tq
    n_cmp = seq // CMP_STRIDE - CMP_BLOCK // CMP_STRIDE + 1
    n_slc = -(-seq // SEL_BLOCK)
    c_hi = jnp.arange(n_c) * CMP_STRIDE + CMP_BLOCK - 1
    dist_c = jnp.arange(seq)[:, None] - c_hi[None, :]
    bc = _bias_tile(rel_bias, dist_c, (dist_c >= 0) & (jnp.arange(n_c) < n_cmp)[None, :])
    bc = bc.reshape(NSA_KV, NSA_GROUP, seq, n_c)
    dist_s = (jnp.arange(n_tiles)[:, None, None] * tq + jnp.arange(tq)[None, :, None] - jnp.arange(tq)[None, None, :])
    bs = _bias_tile(rel_bias, dist_s, dist_s >= 0)
    bs = bs.reshape(NSA_KV, NSA_GROUP, n_tiles, tq, tq).transpose(0, 2, 1, 3, 4).reshape(
        NSA_KV, n_tiles, NSA_GROUP * tq, tq)
    c_lo = np.arange(n_c) * CMP_STRIDE
    s_lo = np.arange(LANES) * SEL_BLOCK
    ov = ((c_lo[:, None] <= s_lo[None, :] + SEL_BLOCK - 1) & (c_lo[:, None] + CMP_BLOCK - 1 >= s_lo[None, :])
          & (np.arange(n_c) < n_cmp)[:, None] & (np.arange(LANES) < n_slc)[None, :])
    ex = np.arange(LANES)[:, None] == (np.arange(seq)[None, :] // SEL_BLOCK)
    return bc, bs, jnp.asarray(ov.astype(np.float32), BF16), jnp.asarray(ex.astype(np.float32), BF16), n_slc


def _nsa_prompt(q, gates, kcv, slcg, wing, q_gain, tables, *, batch, seq):
    bc, bs, ov, ex, n_slc = tables
    n_c = kcv.shape[2]
    n_tiles = seq // NSA_TQ
    gw = NSA_GROUP * HEAD_DIM
    rows = NSA_GROUP * NSA_TQ
    per_g = lambda shape: pl.BlockSpec((1,) + shape, lambda g, b: (g,) + (0,) * len(shape),
                                       pipeline_mode=pl.Buffered(1))
    return pl.pallas_call(
        functools.partial(_nsa_prompt_body, seq=seq, n_slc=n_slc, k_sel=min(SEL_TOPK, n_slc)),
        grid=(NSA_KV, batch),
        in_specs=[pl.BlockSpec((seq, gw), lambda g, b: (b, g)),
                  pl.BlockSpec((1, seq, LANES), lambda g, b: (g, b, 0)),
                  pl.BlockSpec((1, 1, n_c, LANES), lambda g, b: (b, g, 0, 0)),
                  pl.BlockSpec((1, seq, LANES), lambda g, b: (g, b, 0)),
                  pl.BlockSpec((1, seq, LANES), lambda g, b: (g, b, 0)),
                  _const((1, gw)), _const((gw, gw)),
                  per_g((NSA_GROUP, seq, n_c)), per_g((n_tiles, rows, NSA_TQ)),
                  _const(ov.shape), _const(ex.shape)],
        out_specs=pl.BlockSpec((seq, gw), lambda g, b: (b, g)),
        out_shape=jax.ShapeDtypeStruct((batch * seq, NSA_KV * gw), F32),
        scratch_shapes=[pltpu.VMEM((n_tiles, NSA_TQ, NSA_TQ), F32), pltpu.VMEM((rows, 1), F32),
                        pltpu.VMEM((rows, 1), F32), pltpu.VMEM((rows, HEAD_DIM), F32)],
        compiler_params=_cparams("arbitrary", "parallel"), name="nsa_prompt",
    )(q, gates, kcv, slcg, wing, jnp.tile(q_gain, NSA_GROUP).reshape(1, gw), _block_diag_ones(gw), bc, bs, ov, ex)


SMP_PP = 8
NSA_KW = NSA_KV * HEAD_DIM
SEL_LANES = 2 * LANES


def _nsa_sample_body(pt_ref, q_ref, gate_ref, ckv_ref, *refs, n_slc, k_sel, past, tq):
    pages = refs[:SMP_PP]
    (slcn_ref, wst_ref, winn_ref, gq_ref, bd_ref, bc_ref, bs_ref, bsn_ref, bw_ref, bwn_ref, ov_ref, ex_ref, exn_ref,
     o_ref, qbd_scr, sel_scr, kbuf_scr, m_scr, l_scr, acc_scr, oc_scr, ow_scr, pad_scr) = refs[SMP_PP:]
    j = pl.program_id(1)
    hd = HEAD_DIM
    kw = NSA_KW
    rows = NSA_HEADS * tq
    init = (jnp.full((rows, 1), NEG, F32), jnp.zeros((rows, 1), F32), jnp.zeros((rows, kw), F32))

    @pl.when(j == 0)
    def _():
        qn = _seg_rms(q_ref[...], bd_ref[...], gq_ref[...]) * (hd ** -0.5)
        zero = jnp.zeros((tq, hd), F32)
        blocks = []
        for h in range(NSA_HEADS):
            qh = qn[:, hd * h:hd * (h + 1)]
            blocks.append(jnp.concatenate([qh if g == h // NSA_GROUP else zero for g in range(NSA_KV)], axis=1))
        qbd = jnp.concatenate(blocks, axis=0).astype(BF16)
        qbd_scr[...] = qbd
        ckv = ckv_ref[0]
        bias_c = bc_ref[...]
        ok_c = bias_c > NEG_TEST
        s_c = _dot_nt(qbd, ckv[:, :kw].astype(BF16)) + bias_c
        m_c = jnp.max(s_c, axis=-1, keepdims=True)
        e_c = jnp.where(ok_c, jnp.exp(s_c - m_c), 0.0)
        p_c = (e_c / jnp.maximum(jnp.sum(e_c, axis=-1, keepdims=True), 1e-30)).astype(BF16)
        oc_scr[...] = _dot(p_c, ckv[:, kw:].astype(BF16))
        d = _dot(p_c, ov_ref[...])
        gr = NSA_GROUP * tq
        imp = jnp.concatenate(
            [sum(d[g * gr + h * tq:g * gr + (h + 1) * tq] for h in range(NSA_GROUP)) for g in range(NSA_KV)], axis=0)
        pos = past + (lax.broadcasted_iota(jnp.int32, (NSA_KV * tq, 1), 0) & (tq - 1))
        sel = _select_blocks(imp, pos, n_slc, k_sel)
        sel = jnp.concatenate([sel[g * tq:(g + 1) * tq] for g in range(NSA_KV) for _ in range(NSA_GROUP)],
                              axis=0).astype(BF16)
        sel_scr[...] = sel
        pad_scr[...] = jnp.zeros(pad_scr.shape, F32)
        pad_scr[0:tq, :] = slcn_ref[...]
        kn = pad_scr[...].astype(BF16)
        bias_n = bsn_ref[...]
        ok_n = (bias_n > NEG_TEST) & (_dot(sel, exn_ref[...]) > 0.5)
        s_n = jnp.where(ok_n, _dot_nt(qbd, kn[:, :kw]) + bias_n, NEG)
        m, l, acc = _softmax_piece(s_n, ok_n, *init, kn[:, kw:])
        m_scr[...] = m
        l_scr[...] = l
        acc_scr[...] = acc
        pad_scr[0:tq, :] = winn_ref[...]
        wst = wst_ref[0].astype(BF16)
        wn = pad_scr[...].astype(BF16)
        bias_w = bw_ref[...]
        mw = _softmax_piece(_dot_nt(qbd, wst[:, :kw]) + bias_w, bias_w > NEG_TEST, *init, wst[:, kw:])
        bias_wn = bwn_ref[...]
        mw = _softmax_piece(_dot_nt(qbd, wn[:, :kw]) + bias_wn, bias_wn > NEG_TEST, *mw, wn[:, kw:])
        ow_scr[...] = mw[2] / jnp.maximum(mw[1], 1e-30)

    for i in range(SMP_PP):
        kbuf_scr[i * PAGE:(i + 1) * PAGE, :] = pages[i][0].astype(BF16)
    kb = kbuf_scr[...]
    bias = bs_ref[j]
    ok = (bias > NEG_TEST) & (_dot(sel_scr[...], ex_ref[j]) > 0.5)
    s = jnp.where(ok, _dot_nt(qbd_scr[...], kb[:, :kw]) + bias, NEG)
    m, l, acc = _softmax_piece(s, ok, m_scr[...], l_scr[...], acc_scr[...], kb[:, kw:])
    m_scr[...] = m
    l_scr[...] = l
    acc_scr[...] = acc

    @pl.when(j == pl.num_programs(1) - 1)
    def _():
        o_s = acc_scr[...] / jnp.maximum(l_scr[...], 1e-30)
        o_c = oc_scr[...]
        o_w = ow_scr[...]
        outs = []
        for h in range(NSA_HEADS):
            g, hg = divmod(h, NSA_GROUP)
            gt = gate_ref[g]
            rs = slice(h * tq, (h + 1) * tq)
            ls = slice(hd * g, hd * (g + 1))
            outs.append(gt[:, 3 * hg:3 * hg + 1] * o_c[rs, ls] + gt[:, 3 * hg + 1:3 * hg + 2] * o_s[rs, ls]
                        + gt[:, 3 * hg + 2:3 * hg + 3] * o_w[rs, ls])
        o_ref[...] = jnp.concatenate(outs, axis=1)


def _nsa_sample_tables(rel_bias, past, tq, n_c, w_state):
    rows = NSA_HEADS * tq
    total = past + tq
    n_cmp = total // CMP_STRIDE - CMP_BLOCK // CMP_STRIDE + 1
    n_slc = -(-total // SEL_BLOCK)
    pos = past + jnp.arange(tq)
    c_hi = jnp.arange(n_c) * CMP_STRIDE + CMP_BLOCK - 1
    dist_c = pos[:, None] - c_hi[None, :]
    bc = _bias_tile(rel_bias, dist_c, (dist_c >= 0) & (jnp.arange(n_c) < n_cmp)[None, :]).reshape(rows, n_c)
    n_steps = past // (SMP_PP * PAGE)
    dist_s = pos[:, None] - jnp.arange(past)[None, :]
    bs = _bias_tile(rel_bias, dist_s, dist_s >= 0).reshape(rows, n_steps, SMP_PP * PAGE).transpose(1, 0, 2)
    jn = jnp.arange(PAGE)
    dist_n = jnp.arange(tq)[:, None] - jn[None, :]
    new_ok = (dist_n >= 0) & (jn < tq)[None, :]
    bsn = _bias_tile(rel_bias, dist_n, new_ok).reshape(rows, PAGE)
    dist_w = jnp.arange(tq)[:, None] + w_state - jnp.arange(w_state)[None, :]
    bw = _bias_tile(rel_bias, dist_w, (dist_w >= 0) & (dist_w <= NSA_WINDOW)).reshape(rows, w_state)
    bwn = _bias_tile(rel_bias, dist_n, new_ok & (dist_n <= NSA_WINDOW)).reshape(rows, PAGE)
    c_lo = np.arange(n_c) * CMP_STRIDE
    s_lo = np.arange(SEL_LANES) * SEL_BLOCK
    ov = ((c_lo[:, None] <= s_lo[None, :] + SEL_BLOCK - 1) & (c_lo[:, None] + CMP_BLOCK - 1 >= s_lo[None, :])
          & (np.arange(n_c) < n_cmp)[:, None] & (np.arange(SEL_LANES) < n_slc)[None, :])
    blk = np.arange(SEL_LANES)
    ex = (blk[:, None] == (np.arange(past)[None, :] // SEL_BLOCK)).reshape(SEL_LANES, n_steps, SMP_PP * PAGE)
    ex = ex.transpose(1, 0, 2)
    exn = (blk[:, None] == ((past + np.arange(PAGE))[None, :] // SEL_BLOCK)) & (np.arange(PAGE) < tq)[None, :]
    as_bf = lambda a: jnp.asarray(a.astype(np.float32), BF16)
    return bc, bs, bsn, bw, bwn, as_bf(ov), as_bf(ex), as_bf(exn), n_slc


def _nsa_sample(q, gates, crows, slc_pool, table, slc_new, win_state, win_new, q_gain, tables, *, tq):
    bc, bs, bsn, bw, bwn, ov, ex, exn, n_slc = tables
    bsz, npg = table.shape
    past = npg * PAGE
    n_c = crows.shape[1]
    w_state = win_state.shape[1]
    rows = NSA_HEADS * tq
    qw = NSA_HEADS * HEAD_DIM
    page = lambda i: pl.BlockSpec((1, PAGE, ROWS_NSA), lambda b, j, pt: (pt[b, j * SMP_PP + i], 0, 0))
    per_b = lambda shape: pl.BlockSpec((1,) + shape, lambda b, j, pt: (b,) + (0,) * len(shape))
    new_rows = pl.BlockSpec((tq, ROWS_NSA), lambda b, j, pt: (b, 0))
    return pl.pallas_call(
        functools.partial(_nsa_sample_body, n_slc=n_slc, k_sel=min(SEL_TOPK, n_slc), past=past, tq=tq),
        grid_spec=pltpu.PrefetchScalarGridSpec(
            num_scalar_prefetch=1, grid=(bsz, npg // SMP_PP),
            in_specs=[pl.BlockSpec((tq, qw), lambda b, j, pt: (b, 0)),
                      pl.BlockSpec((NSA_KV, tq, LANES), lambda b, j, pt: (0, b, 0)),
                      per_b((n_c, ROWS_NSA))] + [page(i) for i in range(SMP_PP)] + [
                      new_rows, per_b((w_state, ROWS_NSA)), new_rows,
                      _const((1, qw)), _const((qw, qw)), _const(bc.shape), _const(bs.shape), _const(bsn.shape),
                      _const(bw.shape), _const(bwn.shape), _const(ov.shape), _const(ex.shape), _const(exn.shape)],
            out_specs=pl.BlockSpec((tq, qw), lambda b, j, pt: (b, 0)),
            scratch_shapes=[pltpu.VMEM((rows, NSA_KW), BF16), pltpu.VMEM((rows, SEL_LANES), BF16),
                            pltpu.VMEM((SMP_PP * PAGE, ROWS_NSA), BF16), pltpu.VMEM((rows, 1), F32),
                            pltpu.VMEM((rows, 1), F32), pltpu.VMEM((rows, NSA_KW), F32),
                            pltpu.VMEM((rows, NSA_KW), F32), pltpu.VMEM((rows, NSA_KW), F32),
                            pltpu.VMEM((PAGE, ROWS_NSA), F32)]),
        out_shape=jax.ShapeDtypeStruct((bsz * tq, qw), F32),
        compiler_params=_cparams("parallel", "arbitrary"), name="nsa_sample",
    )(table, q, gates, crows, *([slc_pool] * SMP_PP), slc_new, win_state, win_new,
      jnp.tile(q_gain, NSA_HEADS).reshape(1, qw), _block_diag_ones(qw), bc, bs, bsn, bw, bwn, ov, ex, exn)


DIL_TQ = 128


def _head_of_lane(shape):
    return lax.broadcasted_iota(jnp.int32, shape, 1) >> 6


def _stack_heads(qt, lane_head):
    return jnp.concatenate([jnp.where(lane_head == h, qt, 0.0) for h in range(DIL_HEADS)], axis=0).astype(BF16)


def _unstack_heads(r, lane_head, t):
    return sum(jnp.where(lane_head == h, r[h * t:(h + 1) * t], 0.0) for h in range(DIL_HEADS))


def _dil_prompt_body(q_ref, kv_ref, bias_ref, o_ref, lse_ref, *, n_sub, dil):
    tq = DIL_TQ
    gw = ODD_GW
    lane_head = _head_of_lane((tq, gw))
    for r in range(dil):
        ql = slice(gw * r, gw * (r + 1))
        kl = slice(2 * gw * r, 2 * gw * r + gw)
        vl = slice(2 * gw * r + gw, 2 * gw * (r + 1))

        def tile(t, carry, ql=ql, kl=kl, vl=vl):
            start = pl.multiple_of(t * tq, tq)
            prev = pl.multiple_of(jnp.maximum(t - 1, 0) * tq, tq)
            cur_rows = pl.ds(start, tq)
            prev_rows = pl.ds(prev, tq)
            kk = jnp.concatenate([kv_ref[prev_rows, kl], kv_ref[cur_rows, kl]], axis=0).astype(BF16)
            vv = jnp.concatenate([kv_ref[prev_rows, vl], kv_ref[cur_rows, vl]], axis=0).astype(BF16)
            bias = bias_ref[jnp.minimum(t, 1)]
            s = _dot_nt(_stack_heads(q_ref[cur_rows, ql], lane_head), kk) + bias
            m = jnp.max(s, axis=-1, keepdims=True)
            e = jnp.where(bias > NEG_TEST, jnp.exp(s - m), 0.0)
            l = jnp.maximum(jnp.sum(e, axis=-1, keepdims=True), 1e-30)
            res = _dot((e / l).astype(BF16), vv)
            lse = jnp.broadcast_to(m + jnp.log(l), (DIL_HEADS * tq, gw))
            o_ref[cur_rows, ql] = _unstack_heads(res, lane_head, tq)
            lse_ref[cur_rows, ql] = _unstack_heads(lse, lane_head, tq)
            return carry

        lax.fori_loop(0, n_sub // tq, tile, 0)


def _merge_groups(outs, lses):
    mx = functools.reduce(jnp.maximum, lses)
    ws = [jnp.exp(l - mx) for l in lses]
    return sum(w * o for w, o in zip(ws, outs)) / sum(ws)


def _dil_prompt_tables(rel_bias):
    tq = DIL_TQ
    i = jnp.arange(tq)[:, None]
    jp = jnp.arange(2 * tq)[None, :]
    m = i + tq - jp
    tiles = []
    for gi, (win, dil) in enumerate(DIL_PATTERN):
        assert win // dil == tq
        ok = (m >= 0) & (m <= tq)
        both = []
        for first in (True, False):
            t = _bias_tile(rel_bias, m * dil, ok & (jp >= tq) if first else ok)
            both.append(t[gi * DIL_HEADS:(gi + 1) * DIL_HEADS].reshape(DIL_HEADS * tq, 2 * tq))
        tiles.append(jnp.stack(both))
    return tiles


def _dil_prompt(q3, kv3, tables, *, batch, seq):
    gw = ODD_GW
    n = batch * seq
    outs, lses = [], []
    for gi, (_, dil) in enumerate(DIL_PATTERN):
        n_sub = seq // dil
        assert n_sub % DIL_TQ == 0
        qv = q3[gi].reshape(n // dil, dil * gw)
        kvv = kv3[gi].reshape(n // dil, dil * 2 * gw)
        o_spec = pl.BlockSpec((n_sub, dil * gw), lambda b: (b, 0))
        o, lse = pl.pallas_call(
            functools.partial(_dil_prompt_body, n_sub=n_sub, dil=dil), grid=(batch,),
            in_specs=[o_spec, pl.BlockSpec((n_sub, dil * 2 * gw), lambda b: (b, 0)), _const(tables[gi].shape)],
            out_specs=[o_spec, o_spec],
            out_shape=[jax.ShapeDtypeStruct(qv.shape, F32)] * 2,
            compiler_params=_cparams("parallel"), name=f"dilated_prompt_g{gi}",
        )(qv, kvv, tables[gi])
        outs.append(o.reshape(n, gw))
        lses.append(lse.reshape(n, gw))
    return outs, lses


def _dil_sample_body(q_ref, kvn_ref, st0_ref, st1_ref, st2_ref, b0_ref, b1_ref, b2_ref, bn_ref, o_ref, lse_ref, pad_scr,
                     *, tq):
    gw = ODD_GW
    lane_head = _head_of_lane((tq, gw))
    rows = DIL_HEADS * tq
    pad_scr[...] = jnp.zeros(pad_scr.shape, F32)
    for gi, (st_ref, b_ref) in enumerate(((st0_ref, b0_ref), (st1_ref, b1_ref), (st2_ref, b2_ref))):
        qs = _stack_heads(q_ref[gi], lane_head)
        st = st_ref[0].astype(BF16)
        pad_scr[0:tq, :] = kvn_ref[gi]
        kn = pad_scr[...].astype(BF16)
        bias = b_ref[...]
        acc = (jnp.full((rows, 1), NEG, F32), jnp.zeros((rows, 1), F32), jnp.zeros((rows, gw), F32))
        acc = _softmax_piece(_dot_nt(qs, st[:, :gw]) + bias, bias > NEG_TEST, *acc, st[:, gw:])
        bias_n = bn_ref[gi]
        m, l, a = _softmax_piece(_dot_nt(qs, kn[:, :gw]) + bias_n, bias_n > NEG_TEST, *acc, kn[:, gw:])
        l = jnp.maximum(l, 1e-30)
        o_ref[gi] = _unstack_heads(a / l, lane_head, tq)
        lse_ref[gi] = _unstack_heads(jnp.broadcast_to(m + jnp.log(l), (rows, gw)), lane_head, tq)


def _dil_sample_tables(rel_bias, tq, state_lens):
    rows = DIL_HEADS * tq
    i = jnp.arange(tq)[:, None]
    per_state, per_new = [], []
    jn = jnp.arange(PAGE)[None, :]
    for gi, ((win, dil), ln) in enumerate(zip(DIL_PATTERN, state_lens)):
        hs = slice(gi * DIL_HEADS, (gi + 1) * DIL_HEADS)
        d = i + ln - jnp.arange(ln)[None, :]
        per_state.append(_bias_tile(rel_bias, d, (d % dil == 0) & (d <= win))[hs].reshape(rows, ln))
        dn = i - jn
        per_new.append(_bias_tile(rel_bias, dn, (dn >= 0) & (dn % dil == 0) & (dn <= win) & (jn < tq))[hs]
                       .reshape(rows, PAGE))
    return per_state, jnp.stack(per_new)


def _dil_sample(q3, kv3, states, tables, *, tq):
    per_state, bias_new = tables
    bsz = states[0].shape[0]
    gw = ODD_GW
    st_spec = lambda s: pl.BlockSpec((1,) + s.shape[1:], lambda b: (b, 0, 0))
    grp = lambda w: pl.BlockSpec((3, tq, w), lambda b: (0, b, 0))
    return pl.pallas_call(
        functools.partial(_dil_sample_body, tq=tq), grid=(bsz,),
        in_specs=[grp(gw), grp(2 * gw)] + [st_spec(s) for s in states] + [_const(t.shape) for t in per_state]
        + [_const(bias_new.shape)],
        out_specs=[grp(gw), grp(gw)],
        out_shape=[jax.ShapeDtypeStruct((3, bsz * tq, gw), F32)] * 2,
        scratch_shapes=[pltpu.VMEM((PAGE, 2 * gw), F32)],
        compiler_params=_cparams("parallel"), name="dilated_sample",
    )(q3, kv3, *states, *per_state, bias_new)


def _pad_even_w(w_in):
    return jnp.concatenate([w_in, jnp.zeros((w_in.shape[0], EVEN_PAD - w_in.shape[1]), w_in.dtype)], axis=1).astype(BF16)


def _even_mixer_prompt(h, gain, w_in, w_out, lb, hg_norm, qk_norm, cmp_pe, cmp_w1, cmp_w2, rel_bias, *, batch, seq, tm):
    n = batch * seq
    hg, q, cmp_rows, slc_rows, win_rows, slcg, wing, gates = _inproj_even(h, gain, _pad_even_w(w_in), qk_norm, tm=tm)
    s0 = jnp.zeros((batch, HG_HEADS, HEAD_DIM, HEAD_DIM), F32)
    o_h, s_fin = _hgrn(hg, lb, hg_norm, s0, batch=batch, seq=seq)
    cw = _compress_weights(cmp_pe, cmp_w1, cmp_w2, qk_norm[1])
    npg = seq // PAGE
    table = jnp.arange(batch * npg, dtype=jnp.int32).reshape(batch, npg)
    _, kcv = _compress(cmp_rows.reshape(n // PAGE, PAGE, ROWS_NSA), table, cw)
    tables = _nsa_prompt_tables(rel_bias, seq, kcv.shape[2])
    o_n = _nsa_prompt(q, gates, kcv, slcg, wing, qk_norm[0], tables, batch=batch, seq=seq)
    wo = w_out.astype(BF16)
    hw = HG_HEADS * HEAD_DIM
    h = _outproj2(h, o_h, o_n, wo[:hw], wo[hw:], tm=tm)
    return h, {"hgrn": s_fin, "cmp": cmp_rows, "slc": slc_rows, "win": win_rows}


def _even_mixer_sample(h, gain, w_in, w_out, lb, hg_norm, qk_norm, cmp_pe, cmp_w1, cmp_w2, rel_bias,
                       state_hgrn, cmp_pool, slc_pool, win_state, page_table, *, batch, tq, tm):
    assert tq < CMP_STRIDE and tq & (tq - 1) == 0
    hg, q, cmp_rows, slc_rows, win_rows, _, _, gates = _inproj_even(h, gain, _pad_even_w(w_in), qk_norm, tm=tm)
    o_h, s_fin = _hgrn(hg, lb, hg_norm, state_hgrn, batch=batch, seq=tq)
    cw = _compress_weights(cmp_pe, cmp_w1, cmp_w2, qk_norm[1])
    pool = lambda p: p.reshape(p.shape[0], PAGE, ROWS_NSA)
    crows, _ = _compress(pool(cmp_pool), page_table, cw)
    win_state = win_state.reshape(batch, win_state.shape[1], ROWS_NSA)
    past = page_table.shape[1] * PAGE
    tables = _nsa_sample_tables(rel_bias, past, tq, crows.shape[1], win_state.shape[1])
    o_n = _nsa_sample(q, gates, crows, pool(slc_pool), page_table, slc_rows, win_state, win_rows, qk_norm[0],
                      tables, tq=tq)
    wo = w_out.astype(BF16)
    hw = HG_HEADS * HEAD_DIM
    h = _outproj2(h, o_h, o_n, wo[:hw], wo[hw:], tm=tm)
    new_win = jnp.concatenate([win_state, win_rows.reshape(batch, tq, ROWS_NSA)], axis=1)[:, tq:]
    return h, {"hgrn": s_fin, "cmp": cmp_rows, "slc": slc_rows, "win": new_win}


def _odd_mixer_prompt(h, gain, w_in, w_out, qk_norm, rel_bias, *, batch, seq, tm):
    q3, kv3 = _inproj_odd(h, gain, w_in.astype(BF16), qk_norm, tm=tm)
    outs, lses = _dil_prompt(q3, kv3, _dil_prompt_tables(rel_bias), batch=batch, seq=seq)
    return _outproj_merge(h, outs, lses, w_out.astype(BF16), tm=tm), kv3


def _odd_mixer_sample(h, gain, w_in, w_out, qk_norm, rel_bias, states, *, batch, tq, tm):
    q3, kv3 = _inproj_odd(h, gain, w_in.astype(BF16), qk_norm, tm=tm)
    states = [s.reshape(batch, s.shape[1], 2 * ODD_GW) for s in states]
    tables = _dil_sample_tables(rel_bias, tq, [s.shape[1] for s in states])
    o3, lse3 = _dil_sample(q3, kv3, states, tables, tq=tq)
    new = [jnp.concatenate([s, kv3[g].reshape(batch, tq, 2 * ODD_GW)], axis=1)[:, tq:] for g, s in enumerate(states)]
    h = _outproj_merge(h, [o3[g] for g in range(3)], [lse3[g] for g in range(3)], w_out.astype(BF16), tm=tm)
    return h, new


TM_PROMPT = 512
TM_SAMPLE = 256


def kernel(x_prompt, x_sample, p_prompt, p_sample, state_hgrn, cache_nsa_cmp_kv, cache_nsa_slc_kv, state_nsa_win_kv,
           state_dil_kv_0, state_dil_kv_1, state_dil_kv_2, page_table, rel_bias, norm_ffn1, norm_mix, norm_ffn2,
           norm_ple, w_ffn1_in, w_ffn1_out, w_ffn2_in, w_ffn2_out, w_ple_gate, w_ple_proj, w_in_even, w_out_even,
           hgrn_lb_logits, hgrn_norm, nsa_qk_norm, nsa_cmp_pe, nsa_cmp_w1, nsa_cmp_w2, w_in_odd, w_out_odd,
           dil_qk_norm):
    depth = norm_ffn1.shape[0]
    dil_states = (state_dil_kv_0, state_dil_kv_1, state_dil_kv_2)
    bf = lambda w: w.astype(BF16)
    w1i, w1o, w2i, w2o, wpg, wpp = (bf(w) for w in (w_ffn1_in, w_ffn1_out, w_ffn2_in, w_ffn2_out, w_ple_gate,
                                                     w_ple_proj))
    lb_all = jnp.cumsum(jax.nn.softmax(hgrn_lb_logits.astype(F32), axis=0), axis=0)

    def run(x, p, sample):
        batch, seq, d = x.shape
        n = batch * seq
        tm = TM_SAMPLE if sample else TM_PROMPT
        h = x.reshape(n, d)
        p = p.reshape(depth, n, p.shape[-1])
        even, odd = [], []
        for li in range(depth):
            h = _ffn(h, norm_ffn1[li], w1i[li], w1o[li], tm=tm)
            if li % 2 == 0:
                ei = li // 2
                args = (h, norm_mix[li], w_in_even[ei], w_out_even[ei], lb_all[ei], hgrn_norm[ei], nsa_qk_norm[ei],
                        nsa_cmp_pe[ei], nsa_cmp_w1[ei], nsa_cmp_w2[ei], rel_bias)
                if sample:
                    h, st = _even_mixer_sample(*args, state_hgrn[ei], cache_nsa_cmp_kv[ei], cache_nsa_slc_kv[ei],
                                               state_nsa_win_kv[ei], page_table, batch=batch, tq=seq, tm=tm)
                else:
                    h, st = _even_mixer_prompt(*args, batch=batch, seq=seq, tm=tm)
                    st["win"] = st["win"].reshape(batch, seq, ROWS_NSA)[:, -min(NSA_WINDOW, seq):]
                even.append(st)
            else:
                oi = li // 2
                args = (h, norm_mix[li], w_in_odd[oi], w_out_odd[oi], dil_qk_norm[oi], rel_bias)
                if sample:
                    h, bufs = _odd_mixer_sample(*args, [s[oi] for s in dil_states], batch=batch, tq=seq, tm=tm)
                else:
                    h, kv3 = _odd_mixer_prompt(*args, batch=batch, seq=seq, tm=tm)
                    bufs = [kv3[g].reshape(batch, seq, 2 * ODD_GW)[:, -min(w, seq):]
                            for g, (w, _) in enumerate(DIL_PATTERN)]
                odd.append(bufs)
            h = _ffn(h, norm_ffn2[li], w2i[li], w2o[li], ple=(p[li], norm_ple[li], wpg[li], wpp[li]), tm=tm)
        rows = lambda a: a.reshape(batch, -1, 2, NSA_KV, HEAD_DIM)
        drows = lambda a: a.reshape(batch, -1, 2, DIL_HEADS, HEAD_DIM)
        return (h.reshape(batch, seq, d),
                jnp.stack([s["hgrn"] for s in even]), jnp.stack([rows(s["cmp"]) for s in even]),
                jnp.stack([rows(s["slc"]) for s in even]), jnp.stack([rows(s["win"]) for s in even]),
                jnp.stack([drows(b[0]) for b in odd]), jnp.stack([drows(b[1]) for b in odd]),
                jnp.stack([drows(b[2]) for b in odd]))

    y_p, hg_p, cmp_p, slc_p, win_p, d0_p, d1_p, d2_p = run(x_prompt, p_prompt, False)
    y_s, hg_s, cmp_s, slc_s, win_s, d0_s, d1_s, d2_s = run(x_sample, p_sample, True)
    return (y_p, y_s, hg_p, hg_s, cmp_p, cmp_s, slc_p, slc_s, win_p, win_s, d0_p, d0_s, d1_p, d1_s, d2_p, d2_s)
```

```python
import functools
import math

import numpy as np
import jax
import jax.numpy as jnp
from jax import lax
from jax.experimental import pallas as pl
from jax.experimental.pallas import tpu as pltpu

F32 = jnp.float32
BF16 = jnp.bfloat16
HIGHEST = lax.Precision.HIGHEST

V7X_VMEM_BYTES = 64 * 1024 * 1024
VMEM_LIMIT = V7X_VMEM_BYTES * 7 // 8
LANES = 128

EPS = 1e-6
NEG = -1e30
NEG_TEST = -1e29

HEAD_DIM = 64
HG_HEADS = 8
HG_CHUNK = 64
NSA_HEADS = 12
NSA_KV = 3
NSA_GROUP = NSA_HEADS // NSA_KV
CMP_BLOCK = 32
CMP_STRIDE = 16
SEL_BLOCK = 64
SEL_TOPK = 16
NSA_WINDOW = 512
FORCE_SCORE = 1e4
PAGE = 128
DIL_PATTERN = ((128, 1), (512, 4), (2048, 16))
DIL_HEADS = 4
RP_BUCKETS = 32
RP_MAX_DIST = 2048


def _cparams(*sem):
    return pltpu.CompilerParams(dimension_semantics=sem, vmem_limit_bytes=VMEM_LIMIT)


def _const(shape):
    nd = len(shape)
    return pl.BlockSpec(shape, lambda *_: (0,) * nd, pipeline_mode=pl.Buffered(1))


def _dot(a, b):
    return jnp.dot(a, b, preferred_element_type=F32)


def _dot_nt(a, b):
    return lax.dot_general(a, b, (((1,), (1,)), ((), ())), preferred_element_type=F32)


def _dot_tn(a, b):
    return lax.dot_general(a, b, (((0,), (0,)), ((), ())), preferred_element_type=F32)


def _rms(x, gain):
    return x * lax.rsqrt(jnp.mean(x * x, axis=-1, keepdims=True) + EPS) * gain


def _seg_ms(x, bd):
    sq = x * x
    hi = sq.astype(BF16)
    lo = (sq - hi.astype(F32)).astype(BF16)
    return (_dot(hi, bd) + _dot(lo, bd)) * (1.0 / HEAD_DIM)


def _seg_rms(x, bd, gain):
    return x * lax.rsqrt(_seg_ms(x, bd) + EPS) * gain


def _norm_k_rows(x, bd, gain, kmask):
    r = lax.rsqrt(_seg_ms(x, bd) + EPS)
    return x * jnp.where(kmask > 0.5, r, 1.0) * gain


def _block_diag_ones(width):
    i = np.arange(width) // HEAD_DIM
    return jnp.asarray((i[:, None] == i[None, :]).astype(np.float32), BF16)


def _rel_bucket(dist):
    exact = RP_BUCKETS // 2
    d = jnp.maximum(dist, 0)
    log_ratio = jnp.log(jnp.maximum(d, 1).astype(F32) / exact) / math.log(RP_MAX_DIST / exact)
    large = jnp.minimum(exact + (log_ratio * (RP_BUCKETS - exact)).astype(jnp.int32), RP_BUCKETS - 1)
    return jnp.where(d < exact, d, large)


def _bias_tile(rel_bias, dist, valid):
    onehot = (_rel_bucket(dist)[..., None] == jnp.arange(RP_BUCKETS)).astype(F32)
    t = jnp.einsum('...k,kh->h...', onehot, rel_bias.astype(F32), precision=HIGHEST)
    return jnp.where(valid[None], t, NEG)


def _toeplitz(ext, n, m):
    p = n + m - 1
    u = jnp.concatenate([jnp.flip(ext[..., :m], -1), jnp.flip(ext[..., m:], -1)], axis=-1)
    reps = (1,) * (ext.ndim - 1) + (n,)
    t = jnp.tile(u, reps)[..., :n * (p - 1)].reshape(ext.shape[:-1] + (n, p - 1))
    return t[..., :m]


def _softmax_piece(s, mask, m_prev, l_prev, acc_prev, v, v_t=False):
    m_new = jnp.maximum(m_prev, jnp.max(s, axis=-1, keepdims=True))
    alpha = jnp.exp(m_prev - m_new)
    p = jnp.where(mask, jnp.exp(s - m_new), 0.0)
    l_new = alpha * l_prev + jnp.sum(p, axis=-1, keepdims=True)
    pb = p.astype(BF16)
    acc_new = alpha * acc_prev + (_dot_nt(pb, v) if v_t else _dot(pb, v))
    return m_new, l_new, acc_new


def _ffn_body(x_ref, g_ref, win_ref, wout_ref, *rest, dff, ple):
    x = x_ref[...]
    xn = _rms(x, g_ref[...]).astype(BF16)
    gu = _dot(xn, win_ref[...])
    gg = gu[:, :dff]
    a = (gg * jax.nn.sigmoid(gg) * gu[:, dff:]).astype(BF16)
    h = x + 0.5 * _dot(a, wout_ref[...])
    if ple:
        p_ref, gp_ref, wg_ref, wp_ref, o_ref = rest
        hn = _rms(h, gp_ref[...]).astype(BF16)
        gate = jax.nn.sigmoid(_dot(hn, wg_ref[...]))
        h = h + gate * _dot(p_ref[...].astype(BF16), wp_ref[...])
    else:
        (o_ref,) = rest
    o_ref[...] = h


def _ffn(h, gain, w_in, w_out, ple=None, *, tm):
    n, d = h.shape
    dff = w_out.shape[0]
    row = lambda i: (i, 0)
    in_specs = [pl.BlockSpec((tm, d), row), _const((1, d)), _const(w_in.shape), _const(w_out.shape)]
    args = [h, gain.reshape(1, d), w_in, w_out]
    if ple is not None:
        p, gp, wg, wp = ple
        in_specs += [pl.BlockSpec((tm, p.shape[1]), row), _const((1, d)), _const(wg.shape), _const(wp.shape)]
        args += [p, gp.reshape(1, d), wg, wp]
    return pl.pallas_call(
        functools.partial(_ffn_body, dff=dff, ple=ple is not None),
        grid=(n // tm,), in_specs=in_specs, out_specs=pl.BlockSpec((tm, d), row),
        out_shape=jax.ShapeDtypeStruct((n, d), F32), compiler_params=_cparams("parallel"),
        name="ffn_ple" if ple is not None else "ffn",
    )(*args)


EVEN_HG = 4 * HG_HEADS * HEAD_DIM
EVEN_Q = NSA_HEADS * HEAD_DIM
ROWS_NSA = 2 * NSA_KV * HEAD_DIM
EVEN_PAD = EVEN_HG + EVEN_Q + 3 * ROWS_NSA + LANES


def _inproj_even_body(x_ref, g_ref, w_ref, gk_ref, km_ref, bd_ref,
                      hg_ref, q_ref, cmp_ref, slc_ref, win_ref, slcg_ref, wing_ref, gate_ref):
    xn = _rms(x_ref[...], g_ref[...]).astype(BF16)
    p = _dot(xn, w_ref[...])
    c0 = EVEN_HG
    c1 = c0 + EVEN_Q
    hg_ref[...] = p[:, :c0]
    q_ref[...] = p[:, c0:c1]
    cmp_ref[...] = p[:, c1:c1 + ROWS_NSA]
    bd = bd_ref[...]
    km = km_ref[...]
    slc = _norm_k_rows(p[:, c1 + ROWS_NSA:c1 + 2 * ROWS_NSA], bd, gk_ref[0:1, :], km)
    win = _norm_k_rows(p[:, c1 + 2 * ROWS_NSA:c1 + 3 * ROWS_NSA], bd, gk_ref[1:2, :], km)
    slc_ref[...] = slc
    win_ref[...] = win
    sig = jax.nn.sigmoid(p[:, c1 + 3 * ROWS_NSA:])
    kw = NSA_KV * HEAD_DIM
    for g in range(NSA_KV):
        ks = slice(HEAD_DIM * g, HEAD_DIM * (g + 1))
        vs = slice(kw + HEAD_DIM * g, kw + HEAD_DIM * (g + 1))
        slcg_ref[g] = jnp.concatenate([slc[:, ks], slc[:, vs]], axis=1)
        wing_ref[g] = jnp.concatenate([win[:, ks], win[:, vs]], axis=1)
        gate_ref[g] = sig if g == 0 else pltpu.roll(sig, LANES - 3 * NSA_GROUP * g, 1)


def _inproj_even(h, gain, w_pad, qk_norm, *, tm):
    n, d = h.shape
    kw = NSA_KV * HEAD_DIM
    ones = jnp.ones((kw,), F32)
    gk = jnp.stack([jnp.concatenate([jnp.tile(qk_norm[2], NSA_KV), ones]),
                    jnp.concatenate([jnp.tile(qk_norm[3], NSA_KV), ones])])
    km = jnp.concatenate([ones, 0.0 * ones]).reshape(1, ROWS_NSA)
    row = lambda i: (i, 0)
    grp = lambda i: (0, i, 0)
    shp = lambda *s: jax.ShapeDtypeStruct(s, F32)
    return pl.pallas_call(
        _inproj_even_body, grid=(n // tm,),
        in_specs=[pl.BlockSpec((tm, d), row), _const((1, d)), _const(w_pad.shape), _const((2, ROWS_NSA)),
                  _const((1, ROWS_NSA)), _const((ROWS_NSA, ROWS_NSA))],
        out_specs=[pl.BlockSpec((tm, EVEN_HG), row), pl.BlockSpec((tm, EVEN_Q), row),
                   pl.BlockSpec((tm, ROWS_NSA), row), pl.BlockSpec((tm, ROWS_NSA), row),
                   pl.BlockSpec((tm, ROWS_NSA), row), pl.BlockSpec((NSA_KV, tm, LANES), grp),
                   pl.BlockSpec((NSA_KV, tm, LANES), grp), pl.BlockSpec((NSA_KV, tm, LANES), grp)],
        out_shape=[shp(n, EVEN_HG), shp(n, EVEN_Q), shp(n, ROWS_NSA), shp(n, ROWS_NSA), shp(n, ROWS_NSA),
                   shp(NSA_KV, n, LANES), shp(NSA_KV, n, LANES), shp(NSA_KV, n, LANES)],
        compiler_params=_cparams("parallel"), name="inproj_even",
    )(h, gain.reshape(1, d), w_pad, gk, km, _block_diag_ones(ROWS_NSA))


ODD_W = 3 * DIL_HEADS * HEAD_DIM
ODD_GW = DIL_HEADS * HEAD_DIM


def _inproj_odd_body(x_ref, g_ref, w_ref, gq_ref, gk_ref, bd_ref, q_ref, kv_ref):
    xn = _rms(x_ref[...], g_ref[...]).astype(BF16)
    p = _dot(xn, w_ref[...])
    bd = bd_ref[...]
    q = _seg_rms(p[:, :ODD_W], bd, gq_ref[...]) * (HEAD_DIM ** -0.5)
    k = _seg_rms(p[:, ODD_W:2 * ODD_W], bd, gk_ref[...])
    v = p[:, 2 * ODD_W:]
    for g in range(3):
        s = slice(ODD_GW * g, ODD_GW * (g + 1))
        q_ref[g] = q[:, s]
        kv_ref[g] = jnp.concatenate([k[:, s], v[:, s]], axis=1)


def _inproj_odd(h, gain, w, qk_norm, *, tm):
    n, d = h.shape
    row = lambda i: (i, 0)
    nh = 3 * DIL_HEADS
    return pl.pallas_call(
        _inproj_odd_body, grid=(n // tm,),
        in_specs=[pl.BlockSpec((tm, d), row), _const((1, d)), _const(w.shape), _const((1, ODD_W)),
                  _const((1, ODD_W)), _const((ODD_W, ODD_W))],
        out_specs=[pl.BlockSpec((3, tm, ODD_GW), lambda i: (0, i, 0)),
                   pl.BlockSpec((3, tm, 2 * ODD_GW), lambda i: (0, i, 0))],
        out_shape=[jax.ShapeDtypeStruct((3, n, ODD_GW), F32), jax.ShapeDtypeStruct((3, n, 2 * ODD_GW), F32)],
        compiler_params=_cparams("parallel"), name="inproj_odd",
    )(h, gain.reshape(1, d), w, jnp.tile(qk_norm[0], nh).reshape(1, ODD_W),
      jnp.tile(qk_norm[1], nh).reshape(1, ODD_W), _block_diag_ones(ODD_W))


def _outproj2_body(h_ref, a_ref, b_ref, wa_ref, wb_ref, o_ref):
    o_ref[...] = (h_ref[...] + _dot(a_ref[...].astype(BF16), wa_ref[...])
                  + _dot(b_ref[...].astype(BF16), wb_ref[...]))


def _outproj2(h, a, b, wa, wb, *, tm):
    n, d = h.shape
    row = lambda i: (i, 0)
    return pl.pallas_call(
        _outproj2_body, grid=(n // tm,),
        in_specs=[pl.BlockSpec((tm, d), row), pl.BlockSpec((tm, a.shape[1]), row),
                  pl.BlockSpec((tm, b.shape[1]), row), _const(wa.shape), _const(wb.shape)],
        out_specs=pl.BlockSpec((tm, d), row), out_shape=jax.ShapeDtypeStruct((n, d), F32),
        compiler_params=_cparams("parallel"), name="outproj_even",
    )(h, a, b, wa, wb)


def _outproj_merge_body(h_ref, o0_ref, o1_ref, o2_ref, l0_ref, l1_ref, l2_ref, w_ref, o_ref):
    mixed = _merge_groups([o0_ref[...], o1_ref[...], o2_ref[...]], [l0_ref[...], l1_ref[...], l2_ref[...]])
    o_ref[...] = h_ref[...] + _dot(mixed.astype(BF16), w_ref[...])


def _outproj_merge(h, outs, lses, w, *, tm):
    n, d = h.shape
    row = lambda i: (i, 0)
    part = pl.BlockSpec((tm, ODD_GW), row)
    return pl.pallas_call(
        _outproj_merge_body, grid=(n // tm,),
        in_specs=[pl.BlockSpec((tm, d), row)] + [part] * 6 + [_const(w.shape)],
        out_specs=pl.BlockSpec((tm, d), row), out_shape=jax.ShapeDtypeStruct((n, d), F32),
        compiler_params=_cparams("parallel"), name="outproj_odd",
    )(h, *outs, *lses, w)


def _hgrn_body(hq_ref, hf_ref, hi_ref, hgt_ref, lb_ref, gn_ref, bd_ref, s0_ref, o_ref, sfin_ref, st_ref, *, C):
    c = pl.program_id(1)

    @pl.when(c == 0)
    def _():
        st_ref[...] = s0_ref[0]

    dk = HEAD_DIM
    lb = lb_ref[...]
    f = lb + (1.0 - lb) * jax.nn.sigmoid(hf_ref[...])
    lf = jnp.log(f)
    kk = 1.0 - f
    ti = lax.broadcasted_iota(jnp.int32, (C, C), 0)
    si = lax.broadcasted_iota(jnp.int32, (C, C), 1)
    b = jnp.dot(jnp.where(si <= ti, 1.0, 0.0), lf, precision=HIGHEST, preferred_element_type=F32)
    q = hq_ref[...]
    v = hi_ref[...].astype(BF16)
    heads = range(HG_HEADS)
    hs = [slice(dk * h, dk * (h + 1)) for h in heads]

    qb = q.astype(BF16)
    kb = kk.astype(BF16)
    a = [jnp.where(ti == si, _dot_nt(qb[:, hs[h]], kb[:, hs[h]]), 0.0) for h in heads]
    lvl = 1
    while (1 << lvl) <= C:
        m = 1 << lvl
        half = m // 2
        ref_row = ((ti >> lvl) << lvl) + (half - 1)
        r = jnp.dot(jnp.where(si == ref_row, 1.0, 0.0), b, precision=HIGHEST, preferred_element_type=F32)
        qt = (q * jnp.exp(jnp.minimum(b - r, 0.0))).astype(BF16)
        kt = (kk * jnp.exp(jnp.minimum(r - b, 0.0))).astype(BF16)
        pair = ((ti >> lvl) == (si >> lvl)) & ((ti & (m - 1)) >= half) & ((si & (m - 1)) < half)
        a = [a[h] + jnp.where(pair, _dot_nt(qt[:, hs[h]], kt[:, hs[h]]), 0.0) for h in heads]
        lvl += 1

    b_last = b[C - 1:C, :]
    qe = (q * jnp.exp(b)).astype(BF16)
    kdec = (kk * jnp.exp(b_last - b)).astype(BF16)
    e_last = jnp.exp(b_last)
    outs = []
    for h in heads:
        st = st_ref[h]
        outs.append(_dot(a[h].astype(BF16), v[:, hs[h]]) + _dot_nt(qe[:, hs[h]], st.astype(BF16)))
        st_ref[h] = st * e_last[:, hs[h]] + _dot_tn(v[:, hs[h]], kdec[:, hs[h]])
    o = jnp.concatenate(outs, axis=1)
    gt = hgt_ref[...]
    o_ref[...] = _seg_rms(o, bd_ref[...], gn_ref[...]) * (gt * jax.nn.sigmoid(gt))

    @pl.when(c == pl.num_programs(1) - 1)
    def _():
        sfin_ref[0] = st_ref[...]


def _hgrn(hg, lb, hg_norm, s0, *, batch, seq):
    C = math.gcd(seq, HG_CHUNK)
    nc = seq // C
    w = HG_HEADS * HEAD_DIM
    col = lambda j: pl.BlockSpec((C, w), lambda b, c: (b * nc + c, j))
    st_spec = pl.BlockSpec((1, HG_HEADS, HEAD_DIM, HEAD_DIM), lambda b, c: (b, 0, 0, 0))
    o, sfin = pl.pallas_call(
        functools.partial(_hgrn_body, C=C), grid=(batch, nc),
        in_specs=[col(0), col(1), col(2), col(3), _const((1, w)), _const((1, w)), _const((w, w)), st_spec],
        out_specs=[pl.BlockSpec((C, w), lambda b, c: (b * nc + c, 0)), st_spec],
        out_shape=[jax.ShapeDtypeStruct((batch * seq, w), F32),
                   jax.ShapeDtypeStruct((batch, HG_HEADS, HEAD_DIM, HEAD_DIM), F32)],
        scratch_shapes=[pltpu.VMEM((HG_HEADS, HEAD_DIM, HEAD_DIM), F32)],
        compiler_params=_cparams("parallel", "arbitrary"), name="hgrn_scan",
    )(hg, hg, hg, hg, lb.reshape(1, w), jnp.tile(hg_norm, HG_HEADS).reshape(1, w), _block_diag_ones(w),
      jnp.swapaxes(s0, 2, 3))
    return o, jnp.swapaxes(sfin, 2, 3)


CHUNKS_PER_PAGE = PAGE // CMP_STRIDE
PAGE_LANES = CMP_STRIDE * ROWS_NSA
CMP_HID = 2 * NSA_KV * 2 * HEAD_DIM
CMP_PP = 8


def _compress_body(pt_ref, *refs, n_chunks, transposed):
    pages = refs[:CMP_PP]
    wbig_ref, pe_ref, w2_ref, gk_ref, km_ref, bd_ref, rows_ref, grp_ref, xs_ref = refs[CMP_PP:CMP_PP + 9]
    pg_scr = refs[CMP_PP + 9:]
    nb = ROWS_NSA // LANES
    j = pl.program_id(1)
    for i in range(0, CMP_PP, 2):
        for k in range(2):
            for c in range(nb):
                cs = slice(c * LANES, (c + 1) * LANES)
                pg_scr[k * nb + c][...] = pages[i + k][0, cs, :].T if transposed else pages[i + k][0, :, cs]
        r0 = pl.multiple_of((j * CMP_PP + i) * CHUNKS_PER_PAGE, 2 * CHUNKS_PER_PAGE)
        for s in range(CMP_STRIDE):
            rows_s = pl.ds(s, CHUNKS_PER_PAGE, stride=CMP_STRIDE)
            for c in range(nb):
                blk = jnp.concatenate([pg_scr[c][rows_s, :], pg_scr[nb + c][rows_s, :]], axis=0).astype(BF16)
                lo = s * ROWS_NSA + c * LANES
                xs_ref[pl.ds(r0, 2 * CHUNKS_PER_PAGE), lo:lo + LANES] = blk

    @pl.when(j == pl.num_programs(1) - 1)
    def _():
        wbig = wbig_ref[...]
        y = _dot(xs_ref[...], wbig)
        ysh = pltpu.roll(y, n_chunks - 1, 0)
        hid = y + pltpu.roll(ysh, CMP_HID - HEAD_DIM, 1)
        cf = _dot(pe_ref[...].astype(BF16), wbig)
        lane = lax.broadcasted_iota(jnp.int32, (8, CMP_HID), 1)
        cs = jnp.where(((lane >> 6) & 1) == 0, jnp.broadcast_to(cf[0:1], (8, CMP_HID)),
                       jnp.broadcast_to(cf[1:2], (8, CMP_HID)))
        cs = cs + pltpu.roll(cs, CMP_HID - HEAD_DIM, 1)
        hid = hid + cs[0:1]
        act = (hid * jax.nn.sigmoid(hid)).astype(BF16)
        out = _dot(act, w2_ref[...])
        out = _norm_k_rows(out, bd_ref[...], gk_ref[...], km_ref[...])
        rows_ref[0] = out
        kw = NSA_KV * HEAD_DIM
        for g in range(NSA_KV):
            grp_ref[0, g] = jnp.concatenate([out[:, HEAD_DIM * g:HEAD_DIM * (g + 1)],
                                             out[:, kw + HEAD_DIM * g:kw + HEAD_DIM * (g + 1)]], axis=1)


def _compress_weights(cmp_pe, cmp_w1, cmp_w2, k_gain):
    ratio = CMP_BLOCK // CMP_STRIDE
    w1 = cmp_w1.reshape(2, ratio, CMP_STRIDE, HEAD_DIM, HEAD_DIM)
    eye_kv = jnp.eye(2, dtype=F32)
    eye_g = jnp.eye(NSA_KV, dtype=F32)
    wbig = jnp.einsum('krsde,kK,gG->skgdKGre', w1, eye_kv, eye_g).reshape(PAGE_LANES, CMP_HID).astype(BF16)
    pe = cmp_pe.reshape(2, ratio, CMP_STRIDE, HEAD_DIM)
    pe_rows = jnp.broadcast_to(pe.transpose(1, 2, 0, 3)[:, :, :, None, :],
                               (ratio, CMP_STRIDE, 2, NSA_KV, HEAD_DIM)).reshape(ratio, PAGE_LANES)
    pe_rows = jnp.concatenate([pe_rows, jnp.zeros((8 - ratio, PAGE_LANES), F32)], axis=0)
    sel_r0 = jnp.asarray([1.0, 0.0], F32)
    w2big = jnp.einsum('ked,kK,gG,r->kgreKGd', cmp_w2, eye_kv, eye_g, sel_r0).reshape(CMP_HID, ROWS_NSA).astype(BF16)
    kw = NSA_KV * HEAD_DIM
    gk = jnp.concatenate([jnp.tile(k_gain, NSA_KV), jnp.ones((kw,), F32)]).reshape(1, ROWS_NSA)
    km = jnp.concatenate([jnp.ones((kw,), F32), jnp.zeros((kw,), F32)]).reshape(1, ROWS_NSA)
    return wbig, pe_rows, w2big, gk, km


def _compress(pool, table, cw, *, transposed):
    wbig, pe_rows, w2big, gk, km = cw
    bsz, npg = table.shape
    n_chunks = npg * CHUNKS_PER_PAGE
    page = lambda i: pl.BlockSpec((1,) + pool.shape[1:], lambda b, j, pt: (pt[b, j * CMP_PP + i], 0, 0))
    return pl.pallas_call(
        functools.partial(_compress_body, n_chunks=n_chunks, transposed=transposed),
        grid_spec=pltpu.PrefetchScalarGridSpec(
            num_scalar_prefetch=1, grid=(bsz, npg // CMP_PP),
            in_specs=[page(i) for i in range(CMP_PP)] + [
                _const(wbig.shape), _const(pe_rows.shape), _const(w2big.shape), _const((1, ROWS_NSA)),
                _const((1, ROWS_NSA)), _const((ROWS_NSA, ROWS_NSA))],
            out_specs=[pl.BlockSpec((1, n_chunks, ROWS_NSA), lambda b, j, pt: (b, 0, 0)),
                       pl.BlockSpec((1, NSA_KV, n_chunks, LANES), lambda b, j, pt: (b, 0, 0, 0))],
            scratch_shapes=[pltpu.VMEM((n_chunks, PAGE_LANES), BF16)]
            + [pltpu.VMEM((PAGE, LANES), F32)] * (2 * ROWS_NSA // LANES)),
        out_shape=[jax.ShapeDtypeStruct((bsz, n_chunks, ROWS_NSA), F32),
                   jax.ShapeDtypeStruct((bsz, NSA_KV, n_chunks, LANES), F32)],
        compiler_params=_cparams("parallel", "arbitrary"), name="nsa_compress",
    )(table, *([pool] * CMP_PP), wbig, pe_rows, w2big, gk, km, _block_diag_ones(ROWS_NSA))


def _select_blocks(imp, pos, n_slc, k_sel):
    lane = lax.broadcasted_iota(jnp.int32, imp.shape, 1)
    cur = pos >> 6
    forced = (lane == 0) | (lane == cur) | (lane == cur - 1)
    score = jnp.where(forced, FORCE_SCORE, jnp.where(lane <= cur, imp, -1.0))
    rank = jnp.zeros(imp.shape, F32)
    for i in range(n_slc):
        ci = score[:, i:i + 1]
        rank = rank + jnp.where(ci > score, 1.0, jnp.where(ci == score, jnp.where(lane > i, 1.0, 0.0), 0.0))
    return jnp.where(lane < n_slc, jnp.where(rank < k_sel, 1.0, 0.0), 0.0)


NSA_TQ = 256


def _nsa_prompt_body(q_ref, gate_ref, kcv_ref, slc_ref, win_ref, gq_ref, bd_ref, bc_ref, bs_ref, ov_ref, ex_ref,
                     o_ref, sel_scr, m_scr, l_scr, acc_scr, *, seq, n_slc, k_sel):
    tq = NSA_TQ
    rows = NSA_GROUP * tq
    n_tiles = seq // tq
    hd = HEAD_DIM
    kc = kcv_ref[0, 0, :, :hd].astype(BF16)
    vc = kcv_ref[0, 0, :, hd:].astype(BF16)
    n_c = kc.shape[0]
    ii = lax.broadcasted_iota(jnp.int32, (tq, tq), 0)
    jj = lax.broadcasted_iota(jnp.int32, (tq, tq), 1)

    def flash(qs, kv_ref, t, lo, hi, extra_mask):
        m_scr[...] = jnp.full((rows, 1), NEG, F32)
        l_scr[...] = jnp.zeros((rows, 1), F32)
        acc_scr[...] = jnp.zeros((rows, hd), F32)

        def step(jt, carry):
            k0 = pl.multiple_of(jt * tq, tq)
            kv = kv_ref[0, pl.ds(k0, tq), :]
            bias = bs_ref[0, t - jt]
            ok = (bias > NEG_TEST).reshape(NSA_GROUP, tq, tq) & extra_mask(jt)[None]
            s = (_dot_nt(qs, kv[:, :hd].astype(BF16)) + bias).reshape(NSA_GROUP, tq, tq)
            s = jnp.where(ok, s, NEG).reshape(rows, tq)
            m, l, acc = _softmax_piece(s, ok.reshape(rows, tq), m_scr[...], l_scr[...], acc_scr[...],
                                       kv[:, hd:].astype(BF16))
            m_scr[...] = m
            l_scr[...] = l
            acc_scr[...] = acc
            return carry

        lax.fori_loop(lo, hi + 1, step, 0)
        return acc_scr[...] / jnp.maximum(l_scr[...], 1e-30)

    def q_tile(t, carry):
        r0 = pl.multiple_of(t * tq, tq)
        qn = _seg_rms(q_ref[pl.ds(r0, tq), :], bd_ref[...], gq_ref[...]) * (hd ** -0.5)
        qs = jnp.concatenate([qn[:, hd * h:hd * (h + 1)] for h in range(NSA_GROUP)], axis=0).astype(BF16)
        bias_c = bc_ref[0, :, pl.ds(r0, tq), :].reshape(rows, n_c)
        ok_c = bias_c > NEG_TEST
        s_c = _dot_nt(qs, kc) + bias_c
        m_c = jnp.max(s_c, axis=-1, keepdims=True)
        e_c = jnp.where(ok_c, jnp.exp(s_c - m_c), 0.0)
        p_c = (e_c / jnp.maximum(jnp.sum(e_c, axis=-1, keepdims=True), 1e-30)).astype(BF16)
        o_c = _dot(p_c, vc)
        d = _dot(p_c, ov_ref[...])
        imp = d[0:tq] + d[tq:2 * tq] + d[2 * tq:3 * tq] + d[3 * tq:4 * tq]
        pos = r0 + lax.broadcasted_iota(jnp.int32, (tq, 1), 0)
        sel = _select_blocks(imp, pos, n_slc, k_sel).astype(BF16)
        for jt in range(n_tiles):
            sel_scr[jt] = _dot(sel, ex_ref[:, jt * tq:(jt + 1) * tq])
        o_s = flash(qs, slc_ref, t, 0, t, lambda jt: sel_scr[jt] > 0.5)
        o_w = flash(qs, win_ref, t, jnp.maximum(t - NSA_WINDOW // tq, 0), t,
                    lambda jt: (t - jt) * tq + ii - jj <= NSA_WINDOW)
        gates = gate_ref[0, pl.ds(r0, tq), :]
        outs = []
        for h in range(NSA_GROUP):
            hr = slice(h * tq, (h + 1) * tq)
            outs.append(gates[:, 3 * h:3 * h + 1] * o_c[hr] + gates[:, 3 * h + 1:3 * h + 2] * o_s[hr]
                        + gates[:, 3 * h + 2:3 * h + 3] * o_w[hr])
        o_ref[pl.ds(r0, tq), :] = jnp.concatenate(outs, axis=1)
        return carry

    lax.fori_loop(0, n_tiles, q_tile, 0)


def _nsa_prompt_tables(rel_bias, seq, n_c):
    tq = NSA_TQ
    n_tiles = seq // tq
    n_cmp = seq // CMP_STRIDE - CMP_BLOCK // CMP_STRIDE + 1
    n_slc = -(-seq // SEL_BLOCK)
    n_a = seq // CMP_STRIDE
    t_ext = jnp.arange(n_a + n_c - 1) - (n_c - 1)
    d_ext = CMP_STRIDE * t_ext[None, :] + jnp.arange(CMP_STRIDE)[:, None] - (CMP_BLOCK - 1)
    bc = _toeplitz(_bias_tile(rel_bias, d_ext, d_ext >= 0), n_a, n_c)
    bc = jnp.where(jnp.arange(n_c) < n_cmp, bc.transpose(0, 2, 1, 3).reshape(NSA_HEADS, seq, n_c), NEG)
    bc = bc.reshape(NSA_KV, NSA_GROUP, seq, n_c)
    d_ext = jnp.arange(n_tiles)[:, None] * tq + jnp.arange(2 * tq - 1)[None, :] - (tq - 1)
    bs = _toeplitz(_bias_tile(rel_bias, d_ext, d_ext >= 0), tq, tq)
    bs = bs.reshape(NSA_KV, NSA_GROUP, n_tiles, tq, tq).transpose(0, 2, 1, 3, 4).reshape(
        NSA_KV, n_tiles, NSA_GROUP * tq, tq)
    c_lo = np.arange(n_c) * CMP_STRIDE
    s_lo = np.arange(LANES) * SEL_BLOCK
    ov = ((c_lo[:, None] <= s_lo[None, :] + SEL_BLOCK - 1) & (c_lo[:, None] + CMP_BLOCK - 1 >= s_lo[None, :])
          & (np.arange(n_c) < n_cmp)[:, None] & (np.arange(LANES) < n_slc)[None, :])
    ex = np.arange(LANES)[:, None] == (np.arange(seq)[None, :] // SEL_BLOCK)
    return bc, bs, jnp.asarray(ov.astype(np.float32), BF16), jnp.asarray(ex.astype(np.float32), BF16), n_slc


def _nsa_prompt(q, gates, kcv, slcg, wing, q_gain, tables, *, batch, seq):
    bc, bs, ov, ex, n_slc = tables
    n_c = kcv.shape[2]
    n_tiles = seq // NSA_TQ
    gw = NSA_GROUP * HEAD_DIM
    rows = NSA_GROUP * NSA_TQ
    per_g = lambda shape: pl.BlockSpec((1,) + shape, lambda g, b: (g,) + (0,) * len(shape),
                                       pipeline_mode=pl.Buffered(1))
    return pl.pallas_call(
        functools.partial(_nsa_prompt_body, seq=seq, n_slc=n_slc, k_sel=min(SEL_TOPK, n_slc)),
        grid=(NSA_KV, batch),
        in_specs=[pl.BlockSpec((seq, gw), lambda g, b: (b, g)),
                  pl.BlockSpec((1, seq, LANES), lambda g, b: (g, b, 0)),
                  pl.BlockSpec((1, 1, n_c, LANES), lambda g, b: (b, g, 0, 0)),
                  pl.BlockSpec((1, seq, LANES), lambda g, b: (g, b, 0)),
                  pl.BlockSpec((1, seq, LANES), lambda g, b: (g, b, 0)),
                  _const((1, gw)), _const((gw, gw)),
                  per_g((NSA_GROUP, seq, n_c)), per_g((n_tiles, rows, NSA_TQ)),
                  _const(ov.shape), _const(ex.shape)],
        out_specs=pl.BlockSpec((seq, gw), lambda g, b: (b, g)),
        out_shape=jax.ShapeDtypeStruct((batch * seq, NSA_KV * gw), F32),
        scratch_shapes=[pltpu.VMEM((n_tiles, NSA_TQ, NSA_TQ), F32), pltpu.VMEM((rows, 1), F32),
                        pltpu.VMEM((rows, 1), F32), pltpu.VMEM((rows, HEAD_DIM), F32)],
        compiler_params=_cparams("arbitrary", "parallel"), name="nsa_prompt",
    )(q, gates, kcv, slcg, wing, jnp.tile(q_gain, NSA_GROUP).reshape(1, gw), _block_diag_ones(gw), bc, bs, ov, ex)


SMP_PP = 8
NSA_KW = NSA_KV * HEAD_DIM
SEL_LANES = 2 * LANES


def _nsa_sample_body(pt_ref, q_ref, gate_ref, ckv_ref, *refs, n_slc, k_sel, past, tq):
    pages = refs[:SMP_PP]
    (slcn_ref, wst_ref, winn_ref, gq_ref, bd_ref, bc_ref, bs_ref, bsn_ref, bw_ref, bwn_ref, ov_ref, ex_ref, exn_ref,
     o_ref, qbd_scr, sel_scr, kbuf_scr, m_scr, l_scr, acc_scr, oc_scr, ow_scr, pad_scr) = refs[SMP_PP:]
    j = pl.program_id(1)
    hd = HEAD_DIM
    kw = NSA_KW
    rows = NSA_HEADS * tq
    init = (jnp.full((rows, 1), NEG, F32), jnp.zeros((rows, 1), F32), jnp.zeros((rows, kw), F32))

    @pl.when(j == 0)
    def _():
        qn = _seg_rms(q_ref[...], bd_ref[...], gq_ref[...]) * (hd ** -0.5)
        zero = jnp.zeros((tq, hd), F32)
        blocks = []
        for h in range(NSA_HEADS):
            qh = qn[:, hd * h:hd * (h + 1)]
            blocks.append(jnp.concatenate([qh if g == h // NSA_GROUP else zero for g in range(NSA_KV)], axis=1))
        qbd = jnp.concatenate(blocks, axis=0).astype(BF16)
        qbd_scr[...] = qbd
        ckv = ckv_ref[0]
        bias_c = bc_ref[...]
        ok_c = bias_c > NEG_TEST
        s_c = _dot_nt(qbd, ckv[:, :kw].astype(BF16)) + bias_c
        m_c = jnp.max(s_c, axis=-1, keepdims=True)
        e_c = jnp.where(ok_c, jnp.exp(s_c - m_c), 0.0)
        p_c = (e_c / jnp.maximum(jnp.sum(e_c, axis=-1, keepdims=True), 1e-30)).astype(BF16)
        oc_scr[...] = _dot(p_c, ckv[:, kw:].astype(BF16))
        d = _dot(p_c, ov_ref[...])
        gr = NSA_GROUP * tq
        imp = jnp.concatenate(
            [sum(d[g * gr + h * tq:g * gr + (h + 1) * tq] for h in range(NSA_GROUP)) for g in range(NSA_KV)], axis=0)
        pos = past + (lax.broadcasted_iota(jnp.int32, (NSA_KV * tq, 1), 0) & (tq - 1))
        sel = _select_blocks(imp, pos, n_slc, k_sel)
        sel = jnp.concatenate([sel[g * tq:(g + 1) * tq] for g in range(NSA_KV) for _ in range(NSA_GROUP)],
                              axis=0).astype(BF16)
        sel_scr[...] = sel
        pad_scr[...] = jnp.zeros(pad_scr.shape, F32)
        pad_scr[0:tq, :] = slcn_ref[...]
        kn = pad_scr[...].astype(BF16)
        bias_n = bsn_ref[...]
        ok_n = (bias_n > NEG_TEST) & (_dot(sel, exn_ref[...]) > 0.5)
        s_n = jnp.where(ok_n, _dot_nt(qbd, kn[:, :kw]) + bias_n, NEG)
        m, l, acc = _softmax_piece(s_n, ok_n, *init, kn[:, kw:])
        m_scr[...] = m
        l_scr[...] = l
        acc_scr[...] = acc
        pad_scr[0:tq, :] = winn_ref[...]
        wst = wst_ref[0].astype(BF16)
        wn = pad_scr[...].astype(BF16)
        bias_w = bw_ref[...]
        mw = _softmax_piece(_dot(qbd, wst[:kw]) + bias_w, bias_w > NEG_TEST, *init, wst[kw:], v_t=True)
        bias_wn = bwn_ref[...]
        mw = _softmax_piece(_dot_nt(qbd, wn[:, :kw]) + bias_wn, bias_wn > NEG_TEST, *mw, wn[:, kw:])
        ow_scr[...] = mw[2] / jnp.maximum(mw[1], 1e-30)

    for i in range(SMP_PP):
        kbuf_scr[:, i * PAGE:(i + 1) * PAGE] = pages[i][0].astype(BF16)
    kb = kbuf_scr[...]
    bias = bs_ref[j]
    ok = (bias > NEG_TEST) & (_dot(sel_scr[...], ex_ref[j]) > 0.5)
    s = jnp.where(ok, _dot(qbd_scr[...], kb[:kw]) + bias, NEG)
    m, l, acc = _softmax_piece(s, ok, m_scr[...], l_scr[...], acc_scr[...], kb[kw:], v_t=True)
    m_scr[...] = m
    l_scr[...] = l
    acc_scr[...] = acc

    @pl.when(j == pl.num_programs(1) - 1)
    def _():
        o_s = acc_scr[...] / jnp.maximum(l_scr[...], 1e-30)
        o_c = oc_scr[...]
        o_w = ow_scr[...]
        outs = []
        for h in range(NSA_HEADS):
            g, hg = divmod(h, NSA_GROUP)
            gt = gate_ref[g]
            rs = slice(h * tq, (h + 1) * tq)
            ls = slice(hd * g, hd * (g + 1))
            outs.append(gt[:, 3 * hg:3 * hg + 1] * o_c[rs, ls] + gt[:, 3 * hg + 1:3 * hg + 2] * o_s[rs, ls]
                        + gt[:, 3 * hg + 2:3 * hg + 3] * o_w[rs, ls])
        o_ref[...] = jnp.concatenate(outs, axis=1)


def _nsa_sample_tables(rel_bias, past, tq, n_c, w_state):
    rows = NSA_HEADS * tq
    total = past + tq
    n_cmp = total // CMP_STRIDE - CMP_BLOCK // CMP_STRIDE + 1
    n_slc = -(-total // SEL_BLOCK)
    pos = past + jnp.arange(tq)
    c_hi = jnp.arange(n_c) * CMP_STRIDE + CMP_BLOCK - 1
    dist_c = pos[:, None] - c_hi[None, :]
    bc = _bias_tile(rel_bias, dist_c, (dist_c >= 0) & (jnp.arange(n_c) < n_cmp)[None, :]).reshape(rows, n_c)
    n_steps = past // (SMP_PP * PAGE)
    d_ext = jnp.arange(tq + past - 1) + 1
    bs = _toeplitz(_bias_tile(rel_bias, d_ext, d_ext >= 0), tq, past)
    bs = bs.reshape(rows, n_steps, SMP_PP * PAGE).transpose(1, 0, 2)
    jn = jnp.arange(PAGE)
    dist_n = jnp.arange(tq)[:, None] - jn[None, :]
    new_ok = (dist_n >= 0) & (jn < tq)[None, :]
    bsn = _bias_tile(rel_bias, dist_n, new_ok).reshape(rows, PAGE)
    dist_w = jnp.arange(tq)[:, None] + w_state - jnp.arange(w_state)[None, :]
    bw = _bias_tile(rel_bias, dist_w, (dist_w >= 0) & (dist_w <= NSA_WINDOW)).reshape(rows, w_state)
    bwn = _bias_tile(rel_bias, dist_n, new_ok & (dist_n <= NSA_WINDOW)).reshape(rows, PAGE)
    c_lo = np.arange(n_c) * CMP_STRIDE
    s_lo = np.arange(SEL_LANES) * SEL_BLOCK
    ov = ((c_lo[:, None] <= s_lo[None, :] + SEL_BLOCK - 1) & (c_lo[:, None] + CMP_BLOCK - 1 >= s_lo[None, :])
          & (np.arange(n_c) < n_cmp)[:, None] & (np.arange(SEL_LANES) < n_slc)[None, :])
    blk = np.arange(SEL_LANES)
    ex = (blk[:, None] == (np.arange(past)[None, :] // SEL_BLOCK)).reshape(SEL_LANES, n_steps, SMP_PP * PAGE)
    ex = ex.transpose(1, 0, 2)
    exn = (blk[:, None] == ((past + np.arange(PAGE))[None, :] // SEL_BLOCK)) & (np.arange(PAGE) < tq)[None, :]
    as_bf = lambda a: jnp.asarray(a.astype(np.float32), BF16)
    return bc, bs, bsn, bw, bwn, as_bf(ov), as_bf(ex), as_bf(exn), n_slc


def _nsa_sample(q, gates, crows, slc_pool_t, table, slc_new, win_state_t, win_new, q_gain, tables, *, tq):
    bc, bs, bsn, bw, bwn, ov, ex, exn, n_slc = tables
    bsz, npg = table.shape
    past = npg * PAGE
    n_c = crows.shape[1]
    w_state = win_state_t.shape[2]
    rows = NSA_HEADS * tq
    qw = NSA_HEADS * HEAD_DIM
    page = lambda i: pl.BlockSpec((1, ROWS_NSA, PAGE), lambda b, j, pt: (pt[b, j * SMP_PP + i], 0, 0))
    per_b = lambda shape: pl.BlockSpec((1,) + shape, lambda b, j, pt: (b,) + (0,) * len(shape))
    new_rows = pl.BlockSpec((tq, ROWS_NSA), lambda b, j, pt: (b, 0))
    return pl.pallas_call(
        functools.partial(_nsa_sample_body, n_slc=n_slc, k_sel=min(SEL_TOPK, n_slc), past=past, tq=tq),
        grid_spec=pltpu.PrefetchScalarGridSpec(
            num_scalar_prefetch=1, grid=(bsz, npg // SMP_PP),
            in_specs=[pl.BlockSpec((tq, qw), lambda b, j, pt: (b, 0)),
                      pl.BlockSpec((NSA_KV, tq, LANES), lambda b, j, pt: (0, b, 0)),
                      per_b((n_c, ROWS_NSA))] + [page(i) for i in range(SMP_PP)] + [
                      new_rows, per_b((ROWS_NSA, w_state)), new_rows,
                      _const((1, qw)), _const((qw, qw)), _const(bc.shape), _const(bs.shape), _const(bsn.shape),
                      _const(bw.shape), _const(bwn.shape), _const(ov.shape), _const(ex.shape), _const(exn.shape)],
            out_specs=pl.BlockSpec((tq, qw), lambda b, j, pt: (b, 0)),
            scratch_shapes=[pltpu.VMEM((rows, NSA_KW), BF16), pltpu.VMEM((rows, SEL_LANES), BF16),
                            pltpu.VMEM((ROWS_NSA, SMP_PP * PAGE), BF16), pltpu.VMEM((rows, 1), F32),
                            pltpu.VMEM((rows, 1), F32), pltpu.VMEM((rows, NSA_KW), F32),
                            pltpu.VMEM((rows, NSA_KW), F32), pltpu.VMEM((rows, NSA_KW), F32),
                            pltpu.VMEM((PAGE, ROWS_NSA), F32)]),
        out_shape=jax.ShapeDtypeStruct((bsz * tq, qw), F32),
        compiler_params=_cparams("parallel", "arbitrary"), name="nsa_sample",
    )(table, q, gates, crows, *([slc_pool_t] * SMP_PP), slc_new, win_state_t, win_new,
      jnp.tile(q_gain, NSA_HEADS).reshape(1, qw), _block_diag_ones(qw), bc, bs, bsn, bw, bwn, ov, ex, exn)


DIL_TQ = 128


def _head_of_lane(shape):
    return lax.broadcasted_iota(jnp.int32, shape, 1) >> 6


def _stack_heads(qt, lane_head):
    return jnp.concatenate([jnp.where(lane_head == h, qt, 0.0) for h in range(DIL_HEADS)], axis=0).astype(BF16)


def _unstack_heads(r, lane_head, t):
    return sum(jnp.where(lane_head == h, r[h * t:(h + 1) * t], 0.0) for h in range(DIL_HEADS))


def _dil_prompt_body(q_ref, kv_ref, bias_ref, o_ref, lse_ref, *, n_sub, dil):
    tq = DIL_TQ
    gw = ODD_GW
    lane_head = _head_of_lane((tq, gw))
    for r in range(dil):
        ql = slice(gw * r, gw * (r + 1))
        kl = slice(2 * gw * r, 2 * gw * r + gw)
        vl = slice(2 * gw * r + gw, 2 * gw * (r + 1))

        def tile(t, carry, ql=ql, kl=kl, vl=vl):
            start = pl.multiple_of(t * tq, tq)
            prev = pl.multiple_of(jnp.maximum(t - 1, 0) * tq, tq)
            cur_rows = pl.ds(start, tq)
            prev_rows = pl.ds(prev, tq)
            kk = jnp.concatenate([kv_ref[prev_rows, kl], kv_ref[cur_rows, kl]], axis=0).astype(BF16)
            vv = jnp.concatenate([kv_ref[prev_rows, vl], kv_ref[cur_rows, vl]], axis=0).astype(BF16)
            bias = bias_ref[jnp.minimum(t, 1)]
            s = _dot_nt(_stack_heads(q_ref[cur_rows, ql], lane_head), kk) + bias
            m = jnp.max(s, axis=-1, keepdims=True)
            e = jnp.where(bias > NEG_TEST, jnp.exp(s - m), 0.0)
            l = jnp.maximum(jnp.sum(e, axis=-1, keepdims=True), 1e-30)
            res = _dot((e / l).astype(BF16), vv)
            lse = jnp.broadcast_to(m + jnp.log(l), (DIL_HEADS * tq, gw))
            o_ref[cur_rows, ql] = _unstack_heads(res, lane_head, tq)
            lse_ref[cur_rows, ql] = _unstack_heads(lse, lane_head, tq)
            return carry

        lax.fori_loop(0, n_sub // tq, tile, 0)


def _merge_groups(outs, lses):
    mx = functools.reduce(jnp.maximum, lses)
    ws = [jnp.exp(l - mx) for l in lses]
    return sum(w * o for w, o in zip(ws, outs)) / sum(ws)


def _dil_prompt_tables(rel_bias):
    tq = DIL_TQ
    m_ext = jnp.arange(3 * tq - 1) - (tq - 1)
    cur_tile = jnp.arange(2 * tq) >= tq
    tiles = []
    for gi, (win, dil) in enumerate(DIL_PATTERN):
        assert win // dil == tq
        t = _toeplitz(_bias_tile(rel_bias, m_ext * dil, (m_ext >= 0) & (m_ext <= tq)), tq, 2 * tq)
        t = t[gi * DIL_HEADS:(gi + 1) * DIL_HEADS].reshape(DIL_HEADS * tq, 2 * tq)
        tiles.append(jnp.stack([jnp.where(cur_tile, t, NEG), t]))
    return tiles


def _dil_prompt(q3, kv3, tables, *, batch, seq):
    gw = ODD_GW
    n = batch * seq
    outs, lses = [], []
    for gi, (_, dil) in enumerate(DIL_PATTERN):
        n_sub = seq // dil
        assert n_sub % DIL_TQ == 0
        qv = q3[gi].reshape(n // dil, dil * gw)
        kvv = kv3[gi].reshape(n // dil, dil * 2 * gw)
        o_spec = pl.BlockSpec((n_sub, dil * gw), lambda b: (b, 0))
        o, lse = pl.pallas_call(
            functools.partial(_dil_prompt_body, n_sub=n_sub, dil=dil), grid=(batch,),
            in_specs=[o_spec, pl.BlockSpec((n_sub, dil * 2 * gw), lambda b: (b, 0)), _const(tables[gi].shape)],
            out_specs=[o_spec, o_spec],
            out_shape=[jax.ShapeDtypeStruct(qv.shape, F32)] * 2,
            compiler_params=_cparams("parallel"), name=f"dilated_prompt_g{gi}",
        )(qv, kvv, tables[gi])
        outs.append(o.reshape(n, gw))
        lses.append(lse.reshape(n, gw))
    return outs, lses


def _dil_sample_body(q_ref, kvn_ref, st0_ref, st1_ref, st2_ref, b0_ref, b1_ref, b2_ref, bn_ref, o_ref, lse_ref, pad_scr,
                     *, tq):
    gw = ODD_GW
    lane_head = _head_of_lane((tq, gw))
    rows = DIL_HEADS * tq
    pad_scr[...] = jnp.zeros(pad_scr.shape, F32)
    for gi, (st_ref, b_ref) in enumerate(((st0_ref, b0_ref), (st1_ref, b1_ref), (st2_ref, b2_ref))):
        qs = _stack_heads(q_ref[gi], lane_head)
        st = st_ref[0].astype(BF16)
        pad_scr[0:tq, :] = kvn_ref[gi]
        kn = pad_scr[...].astype(BF16)
        bias = b_ref[...]
        acc = (jnp.full((rows, 1), NEG, F32), jnp.zeros((rows, 1), F32), jnp.zeros((rows, gw), F32))
        acc = _softmax_piece(_dot(qs, st[:gw]) + bias, bias > NEG_TEST, *acc, st[gw:], v_t=True)
        bias_n = bn_ref[gi]
        m, l, a = _softmax_piece(_dot_nt(qs, kn[:, :gw]) + bias_n, bias_n > NEG_TEST, *acc, kn[:, gw:])
        l = jnp.maximum(l, 1e-30)
        o_ref[gi] = _unstack_heads(a / l, lane_head, tq)
        lse_ref[gi] = _unstack_heads(jnp.broadcast_to(m + jnp.log(l), (rows, gw)), lane_head, tq)


def _dil_sample_tables(rel_bias, tq, state_lens):
    rows = DIL_HEADS * tq
    i = jnp.arange(tq)[:, None]
    per_state, per_new = [], []
    jn = jnp.arange(PAGE)[None, :]
    for gi, ((win, dil), ln) in enumerate(zip(DIL_PATTERN, state_lens)):
        hs = slice(gi * DIL_HEADS, (gi + 1) * DIL_HEADS)
        d = i + ln - jnp.arange(ln)[None, :]
        per_state.append(_bias_tile(rel_bias, d, (d % dil == 0) & (d <= win))[hs].reshape(rows, ln))
        dn = i - jn
        per_new.append(_bias_tile(rel_bias, dn, (dn >= 0) & (dn % dil == 0) & (dn <= win) & (jn < tq))[hs]
                       .reshape(rows, PAGE))
    return per_state, jnp.stack(per_new)


def _dil_sample(q3, kv3, states, tables, *, tq):
    per_state, bias_new = tables
    bsz = states[0].shape[0]
    gw = ODD_GW
    st_spec = lambda s: pl.BlockSpec((1,) + s.shape[1:], lambda b: (b, 0, 0))
    grp = lambda w: pl.BlockSpec((3, tq, w), lambda b: (0, b, 0))
    return pl.pallas_call(
        functools.partial(_dil_sample_body, tq=tq), grid=(bsz,),
        in_specs=[grp(gw), grp(2 * gw)] + [st_spec(s) for s in states] + [_const(t.shape) for t in per_state]
        + [_const(bias_new.shape)],
        out_specs=[grp(gw), grp(gw)],
        out_shape=[jax.ShapeDtypeStruct((3, bsz * tq, gw), F32)] * 2,
        scratch_shapes=[pltpu.VMEM((PAGE, 2 * gw), F32)],
        compiler_params=_cparams("parallel"), name="dilated_sample",
    )(q3, kv3, *states, *per_state, bias_new)


def _pad_even_w(w_in):
    return jnp.concatenate([w_in, jnp.zeros((w_in.shape[0], EVEN_PAD - w_in.shape[1]), w_in.dtype)], axis=1).astype(BF16)


def _even_mixer_prompt(h, gain, w_in, w_out, lb, hg_norm, qk_norm, cmp_pe, cmp_w1, cmp_w2, rel_bias, *, batch, seq, tm):
    n = batch * seq
    hg, q, cmp_rows, slc_rows, win_rows, slcg, wing, gates = _inproj_even(h, gain, _pad_even_w(w_in), qk_norm, tm=tm)
    s0 = jnp.zeros((batch, HG_HEADS, HEAD_DIM, HEAD_DIM), F32)
    o_h, s_fin = _hgrn(hg, lb, hg_norm, s0, batch=batch, seq=seq)
    cw = _compress_weights(cmp_pe, cmp_w1, cmp_w2, qk_norm[1])
    npg = seq // PAGE
    table = jnp.arange(batch * npg, dtype=jnp.int32).reshape(batch, npg)
    _, kcv = _compress(cmp_rows.reshape(n // PAGE, PAGE, ROWS_NSA), table, cw, transposed=False)
    tables = _nsa_prompt_tables(rel_bias, seq, kcv.shape[2])
    o_n = _nsa_prompt(q, gates, kcv, slcg, wing, qk_norm[0], tables, batch=batch, seq=seq)
    wo = w_out.astype(BF16)
    hw = HG_HEADS * HEAD_DIM
    h = _outproj2(h, o_h, o_n, wo[:hw], wo[hw:], tm=tm)
    return h, {"hgrn": s_fin, "cmp": cmp_rows, "slc": slc_rows, "win": win_rows}


def _even_mixer_sample(h, gain, w_in, w_out, lb, hg_norm, qk_norm, cmp_pe, cmp_w1, cmp_w2, rel_bias,
                       state_hgrn, cmp_pool, slc_pool, win_state, page_table, *, batch, tq, tm):
    assert tq < CMP_STRIDE and tq & (tq - 1) == 0
    hg, q, cmp_rows, slc_rows, win_rows, _, _, gates = _inproj_even(h, gain, _pad_even_w(w_in), qk_norm, tm=tm)
    o_h, s_fin = _hgrn(hg, lb, hg_norm, state_hgrn, batch=batch, seq=tq)
    cw = _compress_weights(cmp_pe, cmp_w1, cmp_w2, qk_norm[1])
    rows_t = lambda a: jnp.moveaxis(a, 1, -1).reshape(a.shape[0], ROWS_NSA, a.shape[1])
    crows, _ = _compress(rows_t(cmp_pool), page_table, cw, transposed=True)
    past = page_table.shape[1] * PAGE
    w_state = win_state.shape[1]
    tables = _nsa_sample_tables(rel_bias, past, tq, crows.shape[1], w_state)
    o_n = _nsa_sample(q, gates, crows, rows_t(slc_pool), page_table, slc_rows, rows_t(win_state), win_rows,
                      qk_norm[0], tables, tq=tq)
    wo = w_out.astype(BF16)
    hw = HG_HEADS * HEAD_DIM
    h = _outproj2(h, o_h, o_n, wo[:hw], wo[hw:], tm=tm)
    new_win = jnp.concatenate([win_state[:, tq:], win_rows.reshape((batch, tq) + win_state.shape[2:])], axis=1)
    return h, {"hgrn": s_fin, "cmp": cmp_rows, "slc": slc_rows, "win": new_win}


def _odd_mixer_prompt(h, gain, w_in, w_out, qk_norm, rel_bias, *, batch, seq, tm):
    q3, kv3 = _inproj_odd(h, gain, w_in.astype(BF16), qk_norm, tm=tm)
    outs, lses = _dil_prompt(q3, kv3, _dil_prompt_tables(rel_bias), batch=batch, seq=seq)
    return _outproj_merge(h, outs, lses, w_out.astype(BF16), tm=tm), kv3


def _odd_mixer_sample(h, gain, w_in, w_out, qk_norm, rel_bias, states, *, batch, tq, tm):
    q3, kv3 = _inproj_odd(h, gain, w_in.astype(BF16), qk_norm, tm=tm)
    tables = _dil_sample_tables(rel_bias, tq, [s.shape[1] for s in states])
    states_t = [jnp.moveaxis(s, 1, -1).reshape(batch, 2 * ODD_GW, s.shape[1]) for s in states]
    o3, lse3 = _dil_sample(q3, kv3, states_t, tables, tq=tq)
    new = [jnp.concatenate([s[:, tq:], kv3[g].reshape((batch, tq) + s.shape[2:])], axis=1) for g, s in enumerate(states)]
    h = _outproj_merge(h, [o3[g] for g in range(3)], [lse3[g] for g in range(3)], w_out.astype(BF16), tm=tm)
    return h, new


TM_PROMPT = 512
TM_SAMPLE = 256


def kernel(x_prompt, x_sample, p_prompt, p_sample, state_hgrn, cache_nsa_cmp_kv, cache_nsa_slc_kv, state_nsa_win_kv,
           state_dil_kv_0, state_dil_kv_1, state_dil_kv_2, page_table, rel_bias, norm_ffn1, norm_mix, norm_ffn2,
           norm_ple, w_ffn1_in, w_ffn1_out, w_ffn2_in, w_ffn2_out, w_ple_gate, w_ple_proj, w_in_even, w_out_even,
           hgrn_lb_logits, hgrn_norm, nsa_qk_norm, nsa_cmp_pe, nsa_cmp_w1, nsa_cmp_w2, w_in_odd, w_out_odd,
           dil_qk_norm):
    depth = norm_ffn1.shape[0]
    dil_states = (state_dil_kv_0, state_dil_kv_1, state_dil_kv_2)
    bf = lambda w: w.astype(BF16)
    w1i, w1o, w2i, w2o, wpg, wpp = (bf(w) for w in (w_ffn1_in, w_ffn1_out, w_ffn2_in, w_ffn2_out, w_ple_gate,
                                                     w_ple_proj))
    lb_all = jnp.cumsum(jax.nn.softmax(hgrn_lb_logits.astype(F32), axis=0), axis=0)

    def run(x, p, sample):
        batch, seq, d = x.shape
        n = batch * seq
        tm = TM_SAMPLE if sample else TM_PROMPT
        h = x.reshape(n, d)
        p = p.reshape(depth, n, p.shape[-1])
        even, odd = [], []
        for li in range(depth):
            h = _ffn(h, norm_ffn1[li], w1i[li], w1o[li], tm=tm)
            if li % 2 == 0:
                ei = li // 2
                args = (h, norm_mix[li], w_in_even[ei], w_out_even[ei], lb_all[ei], hgrn_norm[ei], nsa_qk_norm[ei],
                        nsa_cmp_pe[ei], nsa_cmp_w1[ei], nsa_cmp_w2[ei], rel_bias)
                if sample:
                    h, st = _even_mixer_sample(*args, state_hgrn[ei], cache_nsa_cmp_kv[ei], cache_nsa_slc_kv[ei],
                                               state_nsa_win_kv[ei], page_table, batch=batch, tq=seq, tm=tm)
                else:
                    h, st = _even_mixer_prompt(*args, batch=batch, seq=seq, tm=tm)
                    st["win"] = st["win"].reshape(batch, seq, ROWS_NSA)[:, -min(NSA_WINDOW, seq):]
                even.append(st)
            else:
                oi = li // 2
                args = (h, norm_mix[li], w_in_odd[oi], w_out_odd[oi], dil_qk_norm[oi], rel_bias)
                if sample:
                    h, bufs = _odd_mixer_sample(*args, [s[oi] for s in dil_states], batch=batch, tq=seq, tm=tm)
                else:
                    h, kv3 = _odd_mixer_prompt(*args, batch=batch, seq=seq, tm=tm)
                    bufs = [kv3[g].reshape(batch, seq, 2 * ODD_GW)[:, -min(w, seq):]
                            for g, (w, _) in enumerate(DIL_PATTERN)]
                odd.append(bufs)
            h = _ffn(h, norm_ffn2[li], w2i[li], w2o[li], ple=(p[li], norm_ple[li], wpg[li], wpp[li]), tm=tm)
        rows = lambda a: a.reshape(batch, -1, 2, NSA_KV, HEAD_DIM)
        drows = lambda a: a.reshape(batch, -1, 2, DIL_HEADS, HEAD_DIM)
        return (h.reshape(batch, seq, d),
                jnp.stack([s["hgrn"] for s in even]), jnp.stack([rows(s["cmp"]) for s in even]),
                jnp.stack([rows(s["slc"]) for s in even]), jnp.stack([rows(s["win"]) for s in even]),
                jnp.stack([drows(b[0]) for b in odd]), jnp.stack([drows(b[1]) for b in odd]),
                jnp.stack([drows(b[2]) for b in odd]))

    y_p, hg_p, cmp_p, slc_p, win_p, d0_p, d1_p, d2_p = run(x_prompt, p_prompt, False)
    y_s, hg_s, cmp_s, slc_s, win_s, d0_s, d1_s, d2_s = run(x_sample, p_sample, True)
    return (y_p, y_s, hg_p, hg_s, cmp_p, cmp_s, slc_p, slc_s, win_p, win_s, d0_p, d0_s, d1_p, d1_s, d2_p, d2_s)
```

```python
import functools
import math

import numpy as np
import jax
import jax.numpy as jnp
from jax import lax
from jax.experimental import pallas as pl
from jax.experimental.pallas import tpu as pltpu

F32 = jnp.float32
BF16 = jnp.bfloat16
HIGHEST = lax.Precision.HIGHEST

V7X_VMEM_BYTES = 64 * 1024 * 1024
VMEM_LIMIT = V7X_VMEM_BYTES * 7 // 8
LANES = 128

EPS = 1e-6
NEG = -1e30
NEG_TEST = -1e29

HEAD_DIM = 64
HG_HEADS = 8
HG_CHUNK = 64
NSA_HEADS = 12
NSA_KV = 3
NSA_GROUP = NSA_HEADS // NSA_KV
CMP_BLOCK = 32
CMP_STRIDE = 16
SEL_BLOCK = 64
SEL_TOPK = 16
NSA_WINDOW = 512
FORCE_SCORE = 1e4
PAGE = 128
DIL_PATTERN = ((128, 1), (512, 4), (2048, 16))
DIL_HEADS = 4
RP_BUCKETS = 32
RP_MAX_DIST = 2048


def _cparams(*sem):
    return pltpu.CompilerParams(dimension_semantics=sem, vmem_limit_bytes=VMEM_LIMIT)


def _const(shape):
    nd = len(shape)
    return pl.BlockSpec(shape, lambda *_: (0,) * nd, pipeline_mode=pl.Buffered(1))


def _dot(a, b):
    return jnp.dot(a, b, preferred_element_type=F32)


def _dot_nt(a, b):
    return lax.dot_general(a, b, (((1,), (1,)), ((), ())), preferred_element_type=F32)


def _dot_tn(a, b):
    return lax.dot_general(a, b, (((0,), (0,)), ((), ())), preferred_element_type=F32)


def _rms(x, gain):
    return x * lax.rsqrt(jnp.mean(x * x, axis=-1, keepdims=True) + EPS) * gain


def _seg_ms(x, bd):
    sq = x * x
    hi = sq.astype(BF16)
    lo = (sq - hi.astype(F32)).astype(BF16)
    return (_dot(hi, bd) + _dot(lo, bd)) * (1.0 / HEAD_DIM)


def _seg_rms(x, bd, gain):
    return x * lax.rsqrt(_seg_ms(x, bd) + EPS) * gain


def _norm_k_rows(x, bd, gain, kmask):
    r = lax.rsqrt(_seg_ms(x, bd) + EPS)
    return x * jnp.where(kmask > 0.5, r, 1.0) * gain


def _block_diag_ones(width):
    i = np.arange(width) // HEAD_DIM
    return jnp.asarray((i[:, None] == i[None, :]).astype(np.float32), BF16)


def _rel_bucket(dist):
    exact = RP_BUCKETS // 2
    d = jnp.maximum(dist, 0)
    log_ratio = jnp.log(jnp.maximum(d, 1).astype(F32) / exact) / math.log(RP_MAX_DIST / exact)
    large = jnp.minimum(exact + (log_ratio * (RP_BUCKETS - exact)).astype(jnp.int32), RP_BUCKETS - 1)
    return jnp.where(d < exact, d, large)


def _bias_tile(rel_bias, dist, valid):
    onehot = (_rel_bucket(dist)[..., None] == jnp.arange(RP_BUCKETS)).astype(F32)
    t = jnp.einsum('...k,kh->h...', onehot, rel_bias.astype(F32), precision=HIGHEST)
    return jnp.where(valid[None], t, NEG)


def _toeplitz(ext, n, m):
    p = n + m - 1
    u = jnp.concatenate([jnp.flip(ext[..., :m], -1), jnp.flip(ext[..., m:], -1)], axis=-1)
    reps = (1,) * (ext.ndim - 1) + (n,)
    t = jnp.tile(u, reps)[..., :n * (p - 1)].reshape(ext.shape[:-1] + (n, p - 1))
    return t[..., :m]


def _softmax_piece(s, mask, m_prev, l_prev, acc_prev, v, v_t=False):
    m_new = jnp.maximum(m_prev, jnp.max(s, axis=-1, keepdims=True))
    alpha = jnp.exp(m_prev - m_new)
    p = jnp.where(mask, jnp.exp(s - m_new), 0.0)
    l_new = alpha * l_prev + jnp.sum(p, axis=-1, keepdims=True)
    pb = p.astype(BF16)
    acc_new = alpha * acc_prev + (_dot_nt(pb, v) if v_t else _dot(pb, v))
    return m_new, l_new, acc_new


def _ffn_body(x_ref, g_ref, win_ref, wout_ref, *rest, dff, ple):
    x = x_ref[...]
    xn = _rms(x, g_ref[...]).astype(BF16)
    gu = _dot(xn, win_ref[...])
    gg = gu[:, :dff]
    a = (gg * jax.nn.sigmoid(gg) * gu[:, dff:]).astype(BF16)
    h = x + 0.5 * _dot(a, wout_ref[...])
    if ple:
        p_ref, gp_ref, wg_ref, wp_ref, o_ref = rest
        hn = _rms(h, gp_ref[...]).astype(BF16)
        gate = jax.nn.sigmoid(_dot(hn, wg_ref[...]))
        h = h + gate * _dot(p_ref[...].astype(BF16), wp_ref[...])
    else:
        (o_ref,) = rest
    o_ref[...] = h


def _ffn(h, gain, w_in, w_out, ple=None, *, tm):
    n, d = h.shape
    dff = w_out.shape[0]
    row = lambda i: (i, 0)
    in_specs = [pl.BlockSpec((tm, d), row), _const((1, d)), _const(w_in.shape), _const(w_out.shape)]
    args = [h, gain.reshape(1, d), w_in, w_out]
    if ple is not None:
        p, gp, wg, wp = ple
        in_specs += [pl.BlockSpec((tm, p.shape[1]), row), _const((1, d)), _const(wg.shape), _const(wp.shape)]
        args += [p, gp.reshape(1, d), wg, wp]
    return pl.pallas_call(
        functools.partial(_ffn_body, dff=dff, ple=ple is not None),
        grid=(n // tm,), in_specs=in_specs, out_specs=pl.BlockSpec((tm, d), row),
        out_shape=jax.ShapeDtypeStruct((n, d), F32), compiler_params=_cparams("parallel"),
        name="ffn_ple" if ple is not None else "ffn",
    )(*args)


EVEN_HG = 4 * HG_HEADS * HEAD_DIM
EVEN_Q = NSA_HEADS * HEAD_DIM
ROWS_NSA = 2 * NSA_KV * HEAD_DIM
EVEN_PAD = EVEN_HG + EVEN_Q + 3 * ROWS_NSA + LANES


GATE_ROWS = 16


def _inproj_even_body(x_ref, g_ref, w_ref, gk_ref, km_ref, bd_ref,
                      hg_ref, q_ref, cmp_ref, slc_ref, win_ref, slcg_ref, wing_ref, gate_ref, *gate_t_ref):
    xn = _rms(x_ref[...], g_ref[...]).astype(BF16)
    p = _dot(xn, w_ref[...])
    c0 = EVEN_HG
    c1 = c0 + EVEN_Q
    hg_ref[...] = p[:, :c0]
    q_ref[...] = p[:, c0:c1]
    cmp_ref[...] = p[:, c1:c1 + ROWS_NSA]
    bd = bd_ref[...]
    km = km_ref[...]
    slc = _norm_k_rows(p[:, c1 + ROWS_NSA:c1 + 2 * ROWS_NSA], bd, gk_ref[0:1, :], km)
    win = _norm_k_rows(p[:, c1 + 2 * ROWS_NSA:c1 + 3 * ROWS_NSA], bd, gk_ref[1:2, :], km)
    slc_ref[...] = slc
    win_ref[...] = win
    sig = jax.nn.sigmoid(p[:, c1 + 3 * ROWS_NSA:])
    kw = NSA_KV * HEAD_DIM
    for g in range(NSA_KV):
        ks = slice(HEAD_DIM * g, HEAD_DIM * (g + 1))
        vs = slice(kw + HEAD_DIM * g, kw + HEAD_DIM * (g + 1))
        slcg_ref[g] = jnp.concatenate([slc[:, ks], slc[:, vs]], axis=1)
        wing_ref[g] = jnp.concatenate([win[:, ks], win[:, vs]], axis=1)
        gate_ref[g] = sig if g == 0 else pltpu.roll(sig, LANES - 3 * NSA_GROUP * g, 1)
    if gate_t_ref:
        sig_t = sig.T
        for g in range(NSA_KV):
            for half in range(sig.shape[0] // NSA_TQ):
                gate_t_ref[0][g, half] = sig_t[3 * NSA_GROUP * g:3 * NSA_GROUP * g + GATE_ROWS,
                                               half * NSA_TQ:(half + 1) * NSA_TQ]


def _inproj_even(h, gain, w_pad, qk_norm, *, tm, gates_t=False):
    n, d = h.shape
    kw = NSA_KV * HEAD_DIM
    ones = jnp.ones((kw,), F32)
    gk = jnp.stack([jnp.concatenate([jnp.tile(qk_norm[2], NSA_KV), ones]),
                    jnp.concatenate([jnp.tile(qk_norm[3], NSA_KV), ones])])
    km = jnp.concatenate([ones, 0.0 * ones]).reshape(1, ROWS_NSA)
    row = lambda i: (i, 0)
    grp = lambda i: (0, i, 0)
    shp = lambda *s: jax.ShapeDtypeStruct(s, F32)
    out_specs = [pl.BlockSpec((tm, EVEN_HG), row), pl.BlockSpec((tm, EVEN_Q), row),
                 pl.BlockSpec((tm, ROWS_NSA), row), pl.BlockSpec((tm, ROWS_NSA), row),
                 pl.BlockSpec((tm, ROWS_NSA), row), pl.BlockSpec((NSA_KV, tm, LANES), grp),
                 pl.BlockSpec((NSA_KV, tm, LANES), grp), pl.BlockSpec((NSA_KV, tm, LANES), grp)]
    out_shape = [shp(n, EVEN_HG), shp(n, EVEN_Q), shp(n, ROWS_NSA), shp(n, ROWS_NSA), shp(n, ROWS_NSA),
                 shp(NSA_KV, n, LANES), shp(NSA_KV, n, LANES), shp(NSA_KV, n, LANES)]
    if gates_t:
        out_specs.append(pl.BlockSpec((NSA_KV, tm // NSA_TQ, GATE_ROWS, NSA_TQ), lambda i: (0, i, 0, 0)))
        out_shape.append(shp(NSA_KV, n // NSA_TQ, GATE_ROWS, NSA_TQ))
    return pl.pallas_call(
        _inproj_even_body, grid=(n // tm,),
        in_specs=[pl.BlockSpec((tm, d), row), _const((1, d)), _const(w_pad.shape), _const((2, ROWS_NSA)),
                  _const((1, ROWS_NSA)), _const((ROWS_NSA, ROWS_NSA))],
        out_specs=out_specs, out_shape=out_shape,
        compiler_params=_cparams("parallel"), name="inproj_even",
    )(h, gain.reshape(1, d), w_pad, gk, km, _block_diag_ones(ROWS_NSA))


ODD_W = 3 * DIL_HEADS * HEAD_DIM
ODD_GW = DIL_HEADS * HEAD_DIM


def _inproj_odd_body(x_ref, g_ref, w_ref, gq_ref, gk_ref, bd_ref, q_ref, kv_ref):
    xn = _rms(x_ref[...], g_ref[...]).astype(BF16)
    p = _dot(xn, w_ref[...])
    bd = bd_ref[...]
    q = _seg_rms(p[:, :ODD_W], bd, gq_ref[...]) * (HEAD_DIM ** -0.5)
    k = _seg_rms(p[:, ODD_W:2 * ODD_W], bd, gk_ref[...])
    v = p[:, 2 * ODD_W:]
    for g in range(3):
        s = slice(ODD_GW * g, ODD_GW * (g + 1))
        q_ref[g] = q[:, s]
        kv_ref[g] = jnp.concatenate([k[:, s], v[:, s]], axis=1)


def _inproj_odd(h, gain, w, qk_norm, *, tm):
    n, d = h.shape
    row = lambda i: (i, 0)
    nh = 3 * DIL_HEADS
    return pl.pallas_call(
        _inproj_odd_body, grid=(n // tm,),
        in_specs=[pl.BlockSpec((tm, d), row), _const((1, d)), _const(w.shape), _const((1, ODD_W)),
                  _const((1, ODD_W)), _const((ODD_W, ODD_W))],
        out_specs=[pl.BlockSpec((3, tm, ODD_GW), lambda i: (0, i, 0)),
                   pl.BlockSpec((3, tm, 2 * ODD_GW), lambda i: (0, i, 0))],
        out_shape=[jax.ShapeDtypeStruct((3, n, ODD_GW), F32), jax.ShapeDtypeStruct((3, n, 2 * ODD_GW), F32)],
        compiler_params=_cparams("parallel"), name="inproj_odd",
    )(h, gain.reshape(1, d), w, jnp.tile(qk_norm[0], nh).reshape(1, ODD_W),
      jnp.tile(qk_norm[1], nh).reshape(1, ODD_W), _block_diag_ones(ODD_W))


def _outproj2_body(h_ref, a_ref, b_ref, wa_ref, wb_ref, o_ref):
    o_ref[...] = (h_ref[...] + _dot(a_ref[...].astype(BF16), wa_ref[...])
                  + _dot(b_ref[...].astype(BF16), wb_ref[...]))


def _outproj2(h, a, b, wa, wb, *, tm):
    n, d = h.shape
    row = lambda i: (i, 0)
    return pl.pallas_call(
        _outproj2_body, grid=(n // tm,),
        in_specs=[pl.BlockSpec((tm, d), row), pl.BlockSpec((tm, a.shape[1]), row),
                  pl.BlockSpec((tm, b.shape[1]), row), _const(wa.shape), _const(wb.shape)],
        out_specs=pl.BlockSpec((tm, d), row), out_shape=jax.ShapeDtypeStruct((n, d), F32),
        compiler_params=_cparams("parallel"), name="outproj_even",
    )(h, a, b, wa, wb)


def _outproj2t_body(h_ref, a_ref, bt_ref, wa_ref, wb_ref, o_ref):
    acc = h_ref[...] + _dot(a_ref[...].astype(BF16), wa_ref[...])
    tq = bt_ref.shape[2]
    for i in range(bt_ref.shape[0]):
        rs = slice(i * tq, (i + 1) * tq)
        o_ref[rs, :] = acc[rs, :] + _dot_tn(bt_ref[i].astype(BF16), wb_ref[...])


def _outproj2t(h, a, bt, wa, wb, *, tm):
    n, d = h.shape
    row = lambda i: (i, 0)
    tq = bt.shape[2]
    return pl.pallas_call(
        _outproj2t_body, grid=(n // tm,),
        in_specs=[pl.BlockSpec((tm, d), row), pl.BlockSpec((tm, a.shape[1]), row),
                  pl.BlockSpec((tm // tq, bt.shape[1], tq), lambda i: (i, 0, 0)), _const(wa.shape), _const(wb.shape)],
        out_specs=pl.BlockSpec((tm, d), row), out_shape=jax.ShapeDtypeStruct((n, d), F32),
        compiler_params=_cparams("parallel"), name="outproj_even_t",
    )(h, a, bt, wa, wb)


def _outproj_merge_body(h_ref, o0_ref, o1_ref, o2_ref, l0_ref, l1_ref, l2_ref, w_ref, o_ref):
    mixed = _merge_groups([o0_ref[...], o1_ref[...], o2_ref[...]], [l0_ref[...], l1_ref[...], l2_ref[...]])
    o_ref[...] = h_ref[...] + _dot(mixed.astype(BF16), w_ref[...])


def _outproj_merge(h, outs, lses, w, *, tm):
    n, d = h.shape
    row = lambda i: (i, 0)
    part = pl.BlockSpec((tm, ODD_GW), row)
    return pl.pallas_call(
        _outproj_merge_body, grid=(n // tm,),
        in_specs=[pl.BlockSpec((tm, d), row)] + [part] * 6 + [_const(w.shape)],
        out_specs=pl.BlockSpec((tm, d), row), out_shape=jax.ShapeDtypeStruct((n, d), F32),
        compiler_params=_cparams("parallel"), name="outproj_odd",
    )(h, *outs, *lses, w)


def _hgrn_body(hq_ref, hf_ref, hi_ref, hgt_ref, lb_ref, gn_ref, bd_ref, s0_ref, o_ref, sfin_ref, st_ref, *, C):
    c = pl.program_id(1)

    @pl.when(c == 0)
    def _():
        st_ref[...] = s0_ref[0]

    dk = HEAD_DIM
    lb = lb_ref[...]
    f = lb + (1.0 - lb) * jax.nn.sigmoid(hf_ref[...])
    lf = jnp.log(f)
    kk = 1.0 - f
    ti = lax.broadcasted_iota(jnp.int32, (C, C), 0)
    si = lax.broadcasted_iota(jnp.int32, (C, C), 1)
    b = jnp.dot(jnp.where(si <= ti, 1.0, 0.0), lf, precision=HIGHEST, preferred_element_type=F32)
    q = hq_ref[...]
    v = hi_ref[...].astype(BF16)
    heads = range(HG_HEADS)
    hs = [slice(dk * h, dk * (h + 1)) for h in heads]

    qb = q.astype(BF16)
    kb = kk.astype(BF16)
    a = [jnp.where(ti == si, _dot_nt(qb[:, hs[h]], kb[:, hs[h]]), 0.0) for h in heads]
    lvl = 1
    while (1 << lvl) <= C:
        m = 1 << lvl
        half = m // 2
        ref_row = ((ti >> lvl) << lvl) + (half - 1)
        r = jnp.dot(jnp.where(si == ref_row, 1.0, 0.0), b, precision=HIGHEST, preferred_element_type=F32)
        qt = (q * jnp.exp(jnp.minimum(b - r, 0.0))).astype(BF16)
        kt = (kk * jnp.exp(jnp.minimum(r - b, 0.0))).astype(BF16)
        pair = ((ti >> lvl) == (si >> lvl)) & ((ti & (m - 1)) >= half) & ((si & (m - 1)) < half)
        a = [a[h] + jnp.where(pair, _dot_nt(qt[:, hs[h]], kt[:, hs[h]]), 0.0) for h in heads]
        lvl += 1

    b_last = b[C - 1:C, :]
    qe = (q * jnp.exp(b)).astype(BF16)
    kdec = (kk * jnp.exp(b_last - b)).astype(BF16)
    e_last = jnp.exp(b_last)
    outs = []
    for h in heads:
        st = st_ref[h]
        outs.append(_dot(a[h].astype(BF16), v[:, hs[h]]) + _dot_nt(qe[:, hs[h]], st.astype(BF16)))
        st_ref[h] = st * e_last[:, hs[h]] + _dot_tn(v[:, hs[h]], kdec[:, hs[h]])
    o = jnp.concatenate(outs, axis=1)
    gt = hgt_ref[...]
    o_ref[...] = _seg_rms(o, bd_ref[...], gn_ref[...]) * (gt * jax.nn.sigmoid(gt))

    @pl.when(c == pl.num_programs(1) - 1)
    def _():
        sfin_ref[0] = st_ref[...]


def _hgrn(hg, lb, hg_norm, s0, *, batch, seq):
    C = math.gcd(seq, HG_CHUNK)
    nc = seq // C
    w = HG_HEADS * HEAD_DIM
    col = lambda j: pl.BlockSpec((C, w), lambda b, c: (b * nc + c, j))
    st_spec = pl.BlockSpec((1, HG_HEADS, HEAD_DIM, HEAD_DIM), lambda b, c: (b, 0, 0, 0))
    o, sfin = pl.pallas_call(
        functools.partial(_hgrn_body, C=C), grid=(batch, nc),
        in_specs=[col(0), col(1), col(2), col(3), _const((1, w)), _const((1, w)), _const((w, w)), st_spec],
        out_specs=[pl.BlockSpec((C, w), lambda b, c: (b * nc + c, 0)), st_spec],
        out_shape=[jax.ShapeDtypeStruct((batch * seq, w), F32),
                   jax.ShapeDtypeStruct((batch, HG_HEADS, HEAD_DIM, HEAD_DIM), F32)],
        scratch_shapes=[pltpu.VMEM((HG_HEADS, HEAD_DIM, HEAD_DIM), F32)],
        compiler_params=_cparams("parallel", "arbitrary"), name="hgrn_scan",
    )(hg, hg, hg, hg, lb.reshape(1, w), jnp.tile(hg_norm, HG_HEADS).reshape(1, w), _block_diag_ones(w),
      jnp.swapaxes(s0, 2, 3))
    return o, jnp.swapaxes(sfin, 2, 3)


CHUNKS_PER_PAGE = PAGE // CMP_STRIDE
PAGE_LANES = CMP_STRIDE * ROWS_NSA
CMP_HID = 2 * NSA_KV * 2 * HEAD_DIM
CMP_PP = 8


def _compress_body(pt_ref, *refs, n_chunks, transposed):
    pages = refs[:CMP_PP]
    wbig_ref, pe_ref, w2_ref, gk_ref, km_ref, bd_ref, rows_ref, grp_ref, xs_ref = refs[CMP_PP:CMP_PP + 9]
    pg_scr = refs[CMP_PP + 9:]
    nb = ROWS_NSA // LANES
    j = pl.program_id(1)
    for i in range(0, CMP_PP, 2):
        for k in range(2):
            for c in range(nb):
                cs = slice(c * LANES, (c + 1) * LANES)
                pg_scr[k * nb + c][...] = pages[i + k][0, cs, :].T if transposed else pages[i + k][0, :, cs]
        r0 = pl.multiple_of((j * CMP_PP + i) * CHUNKS_PER_PAGE, 2 * CHUNKS_PER_PAGE)
        for s in range(CMP_STRIDE):
            rows_s = pl.ds(s, CHUNKS_PER_PAGE, stride=CMP_STRIDE)
            for c in range(nb):
                blk = jnp.concatenate([pg_scr[c][rows_s, :], pg_scr[nb + c][rows_s, :]], axis=0).astype(BF16)
                lo = s * ROWS_NSA + c * LANES
                xs_ref[pl.ds(r0, 2 * CHUNKS_PER_PAGE), lo:lo + LANES] = blk

    @pl.when(j == pl.num_programs(1) - 1)
    def _():
        wbig = wbig_ref[...]
        y = _dot(xs_ref[...], wbig)
        ysh = pltpu.roll(y, n_chunks - 1, 0)
        hid = y + pltpu.roll(ysh, CMP_HID - HEAD_DIM, 1)
        cf = _dot(pe_ref[...].astype(BF16), wbig)
        lane = lax.broadcasted_iota(jnp.int32, (8, CMP_HID), 1)
        cs = jnp.where(((lane >> 6) & 1) == 0, jnp.broadcast_to(cf[0:1], (8, CMP_HID)),
                       jnp.broadcast_to(cf[1:2], (8, CMP_HID)))
        cs = cs + pltpu.roll(cs, CMP_HID - HEAD_DIM, 1)
        hid = hid + cs[0:1]
        act = (hid * jax.nn.sigmoid(hid)).astype(BF16)
        out = _dot(act, w2_ref[...])
        out = _norm_k_rows(out, bd_ref[...], gk_ref[...], km_ref[...])
        rows_ref[0] = out
        kw = NSA_KV * HEAD_DIM
        for g in range(NSA_KV):
            grp_ref[0, g] = jnp.concatenate([out[:, HEAD_DIM * g:HEAD_DIM * (g + 1)],
                                             out[:, kw + HEAD_DIM * g:kw + HEAD_DIM * (g + 1)]], axis=1)


def _compress_weights(cmp_pe, cmp_w1, cmp_w2, k_gain):
    ratio = CMP_BLOCK // CMP_STRIDE
    w1 = cmp_w1.reshape(2, ratio, CMP_STRIDE, HEAD_DIM, HEAD_DIM)
    eye_kv = jnp.eye(2, dtype=F32)
    eye_g = jnp.eye(NSA_KV, dtype=F32)
    wbig = jnp.einsum('krsde,kK,gG->skgdKGre', w1, eye_kv, eye_g).reshape(PAGE_LANES, CMP_HID).astype(BF16)
    pe = cmp_pe.reshape(2, ratio, CMP_STRIDE, HEAD_DIM)
    pe_rows = jnp.broadcast_to(pe.transpose(1, 2, 0, 3)[:, :, :, None, :],
                               (ratio, CMP_STRIDE, 2, NSA_KV, HEAD_DIM)).reshape(ratio, PAGE_LANES)
    pe_rows = jnp.concatenate([pe_rows, jnp.zeros((8 - ratio, PAGE_LANES), F32)], axis=0)
    sel_r0 = jnp.asarray([1.0, 0.0], F32)
    w2big = jnp.einsum('ked,kK,gG,r->kgreKGd', cmp_w2, eye_kv, eye_g, sel_r0).reshape(CMP_HID, ROWS_NSA).astype(BF16)
    kw = NSA_KV * HEAD_DIM
    gk = jnp.concatenate([jnp.tile(k_gain, NSA_KV), jnp.ones((kw,), F32)]).reshape(1, ROWS_NSA)
    km = jnp.concatenate([jnp.ones((kw,), F32), jnp.zeros((kw,), F32)]).reshape(1, ROWS_NSA)
    return wbig, pe_rows, w2big, gk, km


def _compress(pool, table, cw, *, transposed):
    wbig, pe_rows, w2big, gk, km = cw
    bsz, npg = table.shape
    n_chunks = npg * CHUNKS_PER_PAGE
    page = lambda i: pl.BlockSpec((1,) + pool.shape[1:], lambda b, j, pt: (pt[b, j * CMP_PP + i], 0, 0))
    return pl.pallas_call(
        functools.partial(_compress_body, n_chunks=n_chunks, transposed=transposed),
        grid_spec=pltpu.PrefetchScalarGridSpec(
            num_scalar_prefetch=1, grid=(bsz, npg // CMP_PP),
            in_specs=[page(i) for i in range(CMP_PP)] + [
                _const(wbig.shape), _const(pe_rows.shape), _const(w2big.shape), _const((1, ROWS_NSA)),
                _const((1, ROWS_NSA)), _const((ROWS_NSA, ROWS_NSA))],
            out_specs=[pl.BlockSpec((1, n_chunks, ROWS_NSA), lambda b, j, pt: (b, 0, 0)),
                       pl.BlockSpec((1, NSA_KV, n_chunks, LANES), lambda b, j, pt: (b, 0, 0, 0))],
            scratch_shapes=[pltpu.VMEM((n_chunks, PAGE_LANES), BF16)]
            + [pltpu.VMEM((PAGE, LANES), F32)] * (2 * ROWS_NSA // LANES)),
        out_shape=[jax.ShapeDtypeStruct((bsz, n_chunks, ROWS_NSA), F32),
                   jax.ShapeDtypeStruct((bsz, NSA_KV, n_chunks, LANES), F32)],
        compiler_params=_cparams("parallel", "arbitrary"), name="nsa_compress",
    )(table, *([pool] * CMP_PP), wbig, pe_rows, w2big, gk, km, _block_diag_ones(ROWS_NSA))


def _select_blocks(imp, pos, n_slc, k_sel):
    lane = lax.broadcasted_iota(jnp.int32, imp.shape, 1)
    cur = pos >> 6
    forced = (lane == 0) | (lane == cur) | (lane == cur - 1)
    score = jnp.where(forced, FORCE_SCORE, jnp.where(lane <= cur, imp, -1.0))
    rank = jnp.zeros(imp.shape, F32)
    for i in range(n_slc):
        ci = score[:, i:i + 1]
        rank = rank + jnp.where(ci > score, 1.0, jnp.where(ci == score, jnp.where(lane > i, 1.0, 0.0), 0.0))
    return jnp.where(lane < n_slc, jnp.where(rank < k_sel, 1.0, 0.0), 0.0)


NSA_TQ = 256


def _nsa_prompt_body(q_ref, gate_ref, kcv_ref, slc_ref, win_ref, gq_ref, bd_ref, bc_ref, bs_ref, ov_ref, ex_ref,
                     o_ref, sel_scr, m_scr, l_scr, acc_scr, *, seq, n_slc, k_sel):
    tq = NSA_TQ
    rows = NSA_GROUP * tq
    n_tiles = seq // tq
    hd = HEAD_DIM
    kc = kcv_ref[0, 0, :, :hd].astype(BF16)
    vc = kcv_ref[0, 0, :, hd:].astype(BF16)
    n_c = kc.shape[0]
    ii = lax.broadcasted_iota(jnp.int32, (tq, tq), 0)
    jj = lax.broadcasted_iota(jnp.int32, (tq, tq), 1)

    def flash(qs, kv_ref, t, lo, hi, extra_mask):
        m_scr[...] = jnp.full((rows, 1), NEG, F32)
        l_scr[...] = jnp.zeros((rows, 1), F32)
        acc_scr[...] = jnp.zeros((rows, hd), F32)

        def step(jt, carry):
            k0 = pl.multiple_of(jt * tq, tq)
            kv = kv_ref[0, pl.ds(k0, tq), :]
            bias = bs_ref[0, t - jt]
            ok = (bias > NEG_TEST).reshape(NSA_GROUP, tq, tq) & extra_mask(jt)[None]
            s = (_dot_nt(qs, kv[:, :hd].astype(BF16)) + bias).reshape(NSA_GROUP, tq, tq)
            s = jnp.where(ok, s, NEG).reshape(rows, tq)
            m, l, acc = _softmax_piece(s, ok.reshape(rows, tq), m_scr[...], l_scr[...], acc_scr[...],
                                       kv[:, hd:].astype(BF16))
            m_scr[...] = m
            l_scr[...] = l
            acc_scr[...] = acc
            return carry

        lax.fori_loop(lo, hi + 1, step, 0)
        return acc_scr[...] / jnp.maximum(l_scr[...], 1e-30)

    def q_tile(t, carry):
        r0 = pl.multiple_of(t * tq, tq)
        qn = _seg_rms(q_ref[pl.ds(r0, tq), :], bd_ref[...], gq_ref[...]) * (hd ** -0.5)
        qs = jnp.concatenate([qn[:, hd * h:hd * (h + 1)] for h in range(NSA_GROUP)], axis=0).astype(BF16)
        bias_c = bc_ref[0, :, pl.ds(r0, tq), :].reshape(rows, n_c)
        ok_c = bias_c > NEG_TEST
        s_c = _dot_nt(qs, kc) + bias_c
        m_c = jnp.max(s_c, axis=-1, keepdims=True)
        e_c = jnp.where(ok_c, jnp.exp(s_c - m_c), 0.0)
        p_c = (e_c / jnp.maximum(jnp.sum(e_c, axis=-1, keepdims=True), 1e-30)).astype(BF16)
        o_c = _dot(p_c, vc)
        d = _dot(p_c, ov_ref[...])
        imp = d[0:tq] + d[tq:2 * tq] + d[2 * tq:3 * tq] + d[3 * tq:4 * tq]
        pos = r0 + lax.broadcasted_iota(jnp.int32, (tq, 1), 0)
        sel = _select_blocks(imp, pos, n_slc, k_sel).astype(BF16)
        for jt in range(n_tiles):
            sel_scr[jt] = _dot(sel, ex_ref[:, jt * tq:(jt + 1) * tq])
        o_s = flash(qs, slc_ref, t, 0, t, lambda jt: sel_scr[jt] > 0.5)
        o_w = flash(qs, win_ref, t, jnp.maximum(t - NSA_WINDOW // tq, 0), t,
                    lambda jt: (t - jt) * tq + ii - jj <= NSA_WINDOW)
        gates = gate_ref[0, pl.ds(r0, tq), :]
        outs = []
        for h in range(NSA_GROUP):
            hr = slice(h * tq, (h + 1) * tq)
            outs.append(gates[:, 3 * h:3 * h + 1] * o_c[hr] + gates[:, 3 * h + 1:3 * h + 2] * o_s[hr]
                        + gates[:, 3 * h + 2:3 * h + 3] * o_w[hr])
        o_ref[pl.ds(r0, tq), :] = jnp.concatenate(outs, axis=1)
        return carry

    lax.fori_loop(0, n_tiles, q_tile, 0)


def _nsa_prompt_tables(rel_bias, seq, n_c):
    tq = NSA_TQ
    n_tiles = seq // tq
    n_cmp = seq // CMP_STRIDE - CMP_BLOCK // CMP_STRIDE + 1
    n_slc = -(-seq // SEL_BLOCK)
    n_a = seq // CMP_STRIDE
    t_ext = jnp.arange(n_a + n_c - 1) - (n_c - 1)
    d_ext = CMP_STRIDE * t_ext[None, :] + jnp.arange(CMP_STRIDE)[:, None] - (CMP_BLOCK - 1)
    bc = _toeplitz(_bias_tile(rel_bias, d_ext, d_ext >= 0), n_a, n_c)
    bc = jnp.where(jnp.arange(n_c) < n_cmp, bc.transpose(0, 2, 1, 3).reshape(NSA_HEADS, seq, n_c), NEG)
    bc = bc.reshape(NSA_KV, NSA_GROUP, seq, n_c)
    d_ext = jnp.arange(n_tiles)[:, None] * tq + jnp.arange(2 * tq - 1)[None, :] - (tq - 1)
    bs = _toeplitz(_bias_tile(rel_bias, d_ext, d_ext >= 0), tq, tq)
    bs = bs.reshape(NSA_KV, NSA_GROUP, n_tiles, tq, tq).transpose(0, 2, 1, 3, 4).reshape(
        NSA_KV, n_tiles, NSA_GROUP * tq, tq)
    c_lo = np.arange(n_c) * CMP_STRIDE
    s_lo = np.arange(LANES) * SEL_BLOCK
    ov = ((c_lo[:, None] <= s_lo[None, :] + SEL_BLOCK - 1) & (c_lo[:, None] + CMP_BLOCK - 1 >= s_lo[None, :])
          & (np.arange(n_c) < n_cmp)[:, None] & (np.arange(LANES) < n_slc)[None, :])
    ex = np.arange(LANES)[:, None] == (np.arange(seq)[None, :] // SEL_BLOCK)
    return bc, bs, jnp.asarray(ov.astype(np.float32), BF16), jnp.asarray(ex.astype(np.float32), BF16), n_slc


def _nsa_prompt(q, gates, kcv, slcg, wing, q_gain, tables, *, batch, seq):
    bc, bs, ov, ex, n_slc = tables
    n_c = kcv.shape[2]
    n_tiles = seq // NSA_TQ
    gw = NSA_GROUP * HEAD_DIM
    rows = NSA_GROUP * NSA_TQ
    per_g = lambda shape: pl.BlockSpec((1,) + shape, lambda g, b: (g,) + (0,) * len(shape),
                                       pipeline_mode=pl.Buffered(1))
    return pl.pallas_call(
        functools.partial(_nsa_prompt_body, seq=seq, n_slc=n_slc, k_sel=min(SEL_TOPK, n_slc)),
        grid=(NSA_KV, batch),
        in_specs=[pl.BlockSpec((seq, gw), lambda g, b: (b, g)),
                  pl.BlockSpec((1, seq, LANES), lambda g, b: (g, b, 0)),
                  pl.BlockSpec((1, 1, n_c, LANES), lambda g, b: (b, g, 0, 0)),
                  pl.BlockSpec((1, seq, LANES), lambda g, b: (g, b, 0)),
                  pl.BlockSpec((1, seq, LANES), lambda g, b: (g, b, 0)),
                  _const((1, gw)), _const((gw, gw)),
                  per_g((NSA_GROUP, seq, n_c)), per_g((n_tiles, rows, NSA_TQ)),
                  _const(ov.shape), _const(ex.shape)],
        out_specs=pl.BlockSpec((seq, gw), lambda g, b: (b, g)),
        out_shape=jax.ShapeDtypeStruct((batch * seq, NSA_KV * gw), F32),
        scratch_shapes=[pltpu.VMEM((n_tiles, NSA_TQ, NSA_TQ), F32), pltpu.VMEM((rows, 1), F32),
                        pltpu.VMEM((rows, 1), F32), pltpu.VMEM((rows, HEAD_DIM), F32)],
        compiler_params=_cparams("arbitrary", "parallel"), name="nsa_prompt",
    )(q, gates, kcv, slcg, wing, jnp.tile(q_gain, NSA_GROUP).reshape(1, gw), _block_diag_ones(gw), bc, bs, ov, ex)


NSA_QC = LANES


def _select_blocks_t(imp, pos, n_slc, k_sel):
    blk = lax.broadcasted_iota(jnp.int32, imp.shape, 0)
    cur = pos >> 6
    forced = (blk == 0) | (blk == cur) | (blk == cur - 1)
    score = jnp.where(forced, FORCE_SCORE, jnp.where(blk <= cur, imp, -1.0))
    rank = jnp.zeros(imp.shape, F32)
    for i in range(n_slc):
        ci = score[i:i + 1, :]
        rank = rank + jnp.where(ci > score, 1.0, jnp.where(ci == score, jnp.where(blk > i, 1.0, 0.0), 0.0))
    return jnp.where(rank < k_sel, 1.0, 0.0)


def _nsa_prompt_t_body(q_ref, gate_ref, kcv_ref, slc_ref, win_ref, gq_ref, bd_ref, eye_ref, bc_ref, bs_ref, bw_ref,
                       ov_ref, ex_ref, o_ref, qt_scr, ks_scr, kw_scr, vs_scr, vw_scr, sel_scr, m_scr, l_scr, acc_scr,
                       *, seq, n_slc, k_sel):
    tq = NSA_TQ
    qc = NSA_QC
    hd = HEAD_DIM
    n_tiles = seq // tq
    width = NSA_GROUP * tq
    n_chunks = width // qc
    eye = eye_ref[...]

    qn = (_seg_rms(q_ref[...], bd_ref[...], gq_ref[...]) * (hd ** -0.5)).astype(BF16)
    for t in range(n_tiles):
        rs = slice(t * tq, (t + 1) * tq)
        for h in range(NSA_GROUP):
            qt_scr[t, :, h * tq:(h + 1) * tq] = _dot_nt(eye, qn[rs, h * hd:(h + 1) * hd]).astype(BF16)
        for kv_ref, k_scr, v_scr in ((slc_ref, ks_scr, vs_scr), (win_ref, kw_scr, vw_scr)):
            kv = kv_ref[0, rs, :].astype(BF16)
            k_scr[t] = kv[:, :hd]
            v_scr[t] = _dot_nt(eye, kv[:, hd:]).astype(BF16)
    kc = kcv_ref[0, 0, :, :hd].astype(BF16)
    vct = _dot_nt(eye, kcv_ref[0, 0, :, hd:].astype(BF16)).astype(BF16)
    def flash_init():
        m_scr[...] = jnp.full((1, width), NEG, F32)
        l_scr[...] = jnp.zeros((1, width), F32)
        acc_scr[...] = jnp.zeros((hd, width), F32)

    def flash_step(k_scr, v_scr, t, jt, bias_tile, sel_add):
        s_all = _dot(k_scr[jt], qt_scr[t])
        m_all = m_scr[...]
        l_all = l_scr[...]
        alphas, ls, ms, ps = [], [], [], []
        for c in range(n_chunks):
            cs = slice(c * qc, (c + 1) * qc)
            qoff = (c * qc) % tq
            s = s_all[:, cs] + bias_tile(cs)
            if sel_add is not None:
                s = s + sel_add[:, qoff:qoff + qc]
            m_new = jnp.maximum(m_all[:, cs], jnp.max(s, axis=0, keepdims=True))
            alpha = jnp.exp(m_all[:, cs] - m_new)
            p = jnp.exp(s - m_new)
            ls.append(alpha * l_all[:, cs] + jnp.sum(p, axis=0, keepdims=True))
            ps.append(p.astype(BF16))
            alphas.append(alpha)
            ms.append(m_new)
        cat = lambda xs: jnp.concatenate(xs, axis=1)
        acc_scr[...] = cat(alphas) * acc_scr[...] + _dot(v_scr[jt], cat(ps))
        m_scr[...] = cat(ms)
        l_scr[...] = cat(ls)

    def flash_result():
        return acc_scr[...] / jnp.maximum(l_scr[...], 1e-30)

    def q_tile(t, carry):
        bias_c = bc_ref[0, t]
        s_c = _dot(kc, qt_scr[t]) + bias_c
        m_c = jnp.max(s_c, axis=0, keepdims=True)
        e_c = jnp.where(bias_c > NEG_TEST, jnp.exp(s_c - m_c), 0.0)
        p_c = (e_c / jnp.maximum(jnp.sum(e_c, axis=0, keepdims=True), 1e-30)).astype(BF16)
        o_c = _dot(vct, p_c)
        imp = sum(_dot(ov_ref[...], p_c[:, h * tq:(h + 1) * tq]) for h in range(NSA_GROUP))
        pos = t * tq + lax.broadcasted_iota(jnp.int32, (1, tq), 1)
        sel = _select_blocks_t(imp[:n_slc], pos, n_slc, k_sel)
        sel_scr[...] = jnp.zeros(sel_scr.shape, BF16)
        sel_scr[0:n_slc, :] = ((1.0 - sel) * NEG).astype(BF16)
        flash_init()

        def slc_step(jt, carry):
            flash_step(ks_scr, vs_scr, t, jt, lambda cs: bs_ref[0, t - jt, :, cs], _dot(ex_ref[jt], sel_scr[...]))
            return carry

        lax.fori_loop(0, t + 1, slc_step, 0)
        o_s = flash_result()
        flash_init()
        far = NSA_WINDOW // tq
        for delta in range(far, -1, -1):
            tile_of = (lambda cs: bw_ref[0, :, cs]) if delta == far else (lambda cs, d=delta: bs_ref[0, d, :, cs])
            pl.when(t >= delta)(functools.partial(flash_step, kw_scr, vw_scr, t, t - delta, tile_of, None))
        o_w = flash_result()
        gates = gate_ref[0, t]
        for h in range(NSA_GROUP):
            hs = slice(h * tq, (h + 1) * tq)
            o_ref[t, h * hd:(h + 1) * hd, :] = (gates[3 * h:3 * h + 1] * o_c[:, hs]
                                                + gates[3 * h + 1:3 * h + 2] * o_s[:, hs]
                                                + gates[3 * h + 2:3 * h + 3] * o_w[:, hs])
        return carry

    lax.fori_loop(0, n_tiles, q_tile, 0)


def _nsa_prompt_t_tables(rel_bias, seq, n_c):
    tq = NSA_TQ
    n_tiles = seq // tq
    bc, bs, _, _, n_slc = _nsa_prompt_tables(rel_bias, seq, n_c)
    bc = bc.reshape(NSA_KV, NSA_GROUP, n_tiles, tq, n_c).transpose(0, 2, 4, 1, 3).reshape(
        NSA_KV, n_tiles, n_c, NSA_GROUP * tq)
    bs = bs.reshape(NSA_KV, n_tiles, NSA_GROUP, tq, tq).transpose(0, 1, 4, 2, 3).reshape(
        NSA_KV, n_tiles, tq, NSA_GROUP * tq)
    n_cmp = seq // CMP_STRIDE - CMP_BLOCK // CMP_STRIDE + 1
    c_lo = np.arange(n_c) * CMP_STRIDE
    s_lo = np.arange(LANES) * SEL_BLOCK
    ov = ((c_lo[None, :] <= s_lo[:, None] + SEL_BLOCK - 1) & (c_lo[None, :] + CMP_BLOCK - 1 >= s_lo[:, None])
          & (np.arange(n_c) < n_cmp)[None, :] & (np.arange(LANES) < n_slc)[:, None])
    ex = (np.arange(seq)[:, None] // SEL_BLOCK == np.arange(LANES)[None, :]).reshape(n_tiles, tq, LANES)
    as_bf = lambda a: jnp.asarray(a.astype(np.float32), BF16)
    far = NSA_WINDOW // tq
    inside = np.tile(np.arange(tq)[None, :], (1, NSA_GROUP)) <= np.arange(tq)[:, None]
    bw = jnp.where(jnp.asarray(inside), bs[:, far], NEG)
    return bc, bs, bw, as_bf(ov), as_bf(ex), n_slc


def _nsa_prompt_t(q, gates_t, kcv, slcg, wing, q_gain, tables, *, batch, seq):
    bc, bs, bw, ov, ex, n_slc = tables
    n_c = kcv.shape[2]
    tq = NSA_TQ
    n_tiles = seq // tq
    assert n_tiles > NSA_WINDOW // tq
    gw = NSA_GROUP * HEAD_DIM
    width = NSA_GROUP * tq
    per_g = lambda shape: pl.BlockSpec((1,) + shape, lambda g, b: (g,) + (0,) * len(shape),
                                       pipeline_mode=pl.Buffered(1))
    eye = jnp.eye(HEAD_DIM, dtype=BF16)
    return pl.pallas_call(
        functools.partial(_nsa_prompt_t_body, seq=seq, n_slc=n_slc, k_sel=min(SEL_TOPK, n_slc)),
        grid=(NSA_KV, batch),
        in_specs=[pl.BlockSpec((seq, gw), lambda g, b: (b, g)),
                  pl.BlockSpec((1, n_tiles, 16, tq), lambda g, b: (g, b, 0, 0)),
                  pl.BlockSpec((1, 1, n_c, LANES), lambda g, b: (b, g, 0, 0)),
                  pl.BlockSpec((1, seq, LANES), lambda g, b: (g, b, 0)),
                  pl.BlockSpec((1, seq, LANES), lambda g, b: (g, b, 0)),
                  _const((1, gw)), _const((gw, gw)), _const(eye.shape),
                  per_g((n_tiles, n_c, width)), per_g((n_tiles, tq, width)), per_g((tq, width)),
                  _const(ov.shape), _const(ex.shape)],
        out_specs=pl.BlockSpec((n_tiles, gw, tq), lambda g, b: (b, g, 0)),
        out_shape=jax.ShapeDtypeStruct((batch * n_tiles, NSA_KV * gw, tq), F32),
        scratch_shapes=[pltpu.VMEM((n_tiles, HEAD_DIM, width), BF16),
                        pltpu.VMEM((n_tiles, tq, HEAD_DIM), BF16), pltpu.VMEM((n_tiles, tq, HEAD_DIM), BF16),
                        pltpu.VMEM((n_tiles, HEAD_DIM, tq), BF16), pltpu.VMEM((n_tiles, HEAD_DIM, tq), BF16),
                        pltpu.VMEM((LANES, tq), BF16), pltpu.VMEM((1, width), F32), pltpu.VMEM((1, width), F32),
                        pltpu.VMEM((HEAD_DIM, width), F32)],
        compiler_params=_cparams("arbitrary", "parallel"), name="nsa_prompt",
    )(q, gates_t, kcv, slcg, wing, jnp.tile(q_gain, NSA_GROUP).reshape(1, gw), _block_diag_ones(gw), eye,
      bc, bs, bw, ov, ex)


SMP_PP = 8
NSA_KW = NSA_KV * HEAD_DIM
SEL_LANES = 2 * LANES


def _nsa_sample_body(pt_ref, q_ref, gate_ref, ckv_ref, *refs, n_slc, k_sel, past, tq):
    pages = refs[:SMP_PP]
    (slcn_ref, wst_ref, winn_ref, gq_ref, bd_ref, bc_ref, bs_ref, bsn_ref, bw_ref, bwn_ref, ov_ref, ex_ref, exn_ref,
     o_ref, qbd_scr, sel_scr, kbuf_scr, m_scr, l_scr, acc_scr, oc_scr, ow_scr, pad_scr) = refs[SMP_PP:]
    j = pl.program_id(1)
    hd = HEAD_DIM
    kw = NSA_KW
    rows = NSA_HEADS * tq
    init = (jnp.full((rows, 1), NEG, F32), jnp.zeros((rows, 1), F32), jnp.zeros((rows, kw), F32))

    @pl.when(j == 0)
    def _():
        qn = _seg_rms(q_ref[...], bd_ref[...], gq_ref[...]) * (hd ** -0.5)
        zero = jnp.zeros((tq, hd), F32)
        blocks = []
        for h in range(NSA_HEADS):
            qh = qn[:, hd * h:hd * (h + 1)]
            blocks.append(jnp.concatenate([qh if g == h // NSA_GROUP else zero for g in range(NSA_KV)], axis=1))
        qbd = jnp.concatenate(blocks, axis=0).astype(BF16)
        qbd_scr[...] = qbd
        ckv = ckv_ref[0]
        bias_c = bc_ref[...]
        ok_c = bias_c > NEG_TEST
        s_c = _dot_nt(qbd, ckv[:, :kw].astype(BF16)) + bias_c
        m_c = jnp.max(s_c, axis=-1, keepdims=True)
        e_c = jnp.where(ok_c, jnp.exp(s_c - m_c), 0.0)
        p_c = (e_c / jnp.maximum(jnp.sum(e_c, axis=-1, keepdims=True), 1e-30)).astype(BF16)
        oc_scr[...] = _dot(p_c, ckv[:, kw:].astype(BF16))
        d = _dot(p_c, ov_ref[...])
        gr = NSA_GROUP * tq
        imp = jnp.concatenate(
            [sum(d[g * gr + h * tq:g * gr + (h + 1) * tq] for h in range(NSA_GROUP)) for g in range(NSA_KV)], axis=0)
        pos = past + (lax.broadcasted_iota(jnp.int32, (NSA_KV * tq, 1), 0) & (tq - 1))
        sel = _select_blocks(imp, pos, n_slc, k_sel)
        sel = jnp.concatenate([sel[g * tq:(g + 1) * tq] for g in range(NSA_KV) for _ in range(NSA_GROUP)],
                              axis=0).astype(BF16)
        sel_scr[...] = sel
        pad_scr[...] = jnp.zeros(pad_scr.shape, F32)
        pad_scr[0:tq, :] = slcn_ref[...]
        kn = pad_scr[...].astype(BF16)
        bias_n = bsn_ref[...]
        ok_n = (bias_n > NEG_TEST) & (_dot(sel, exn_ref[...]) > 0.5)
        s_n = jnp.where(ok_n, _dot_nt(qbd, kn[:, :kw]) + bias_n, NEG)
        m, l, acc = _softmax_piece(s_n, ok_n, *init, kn[:, kw:])
        m_scr[...] = m
        l_scr[...] = l
        acc_scr[...] = acc
        pad_scr[0:tq, :] = winn_ref[...]
        wst = wst_ref[0].astype(BF16)
        wn = pad_scr[...].astype(BF16)
        bias_w = bw_ref[...]
        mw = _softmax_piece(_dot(qbd, wst[:kw]) + bias_w, bias_w > NEG_TEST, *init, wst[kw:], v_t=True)
        bias_wn = bwn_ref[...]
        mw = _softmax_piece(_dot_nt(qbd, wn[:, :kw]) + bias_wn, bias_wn > NEG_TEST, *mw, wn[:, kw:])
        ow_scr[...] = mw[2] / jnp.maximum(mw[1], 1e-30)

    for i in range(SMP_PP):
        kbuf_scr[:, i * PAGE:(i + 1) * PAGE] = pages[i][0].astype(BF16)
    kb = kbuf_scr[...]
    bias = bs_ref[j]
    ok = (bias > NEG_TEST) & (_dot(sel_scr[...], ex_ref[j]) > 0.5)
    s = jnp.where(ok, _dot(qbd_scr[...], kb[:kw]) + bias, NEG)
    m, l, acc = _softmax_piece(s, ok, m_scr[...], l_scr[...], acc_scr[...], kb[kw:], v_t=True)
    m_scr[...] = m
    l_scr[...] = l
    acc_scr[...] = acc

    @pl.when(j == pl.num_programs(1) - 1)
    def _():
        o_s = acc_scr[...] / jnp.maximum(l_scr[...], 1e-30)
        o_c = oc_scr[...]
        o_w = ow_scr[...]
        outs = []
        for h in range(NSA_HEADS):
            g, hg = divmod(h, NSA_GROUP)
            gt = gate_ref[g]
            rs = slice(h * tq, (h + 1) * tq)
            ls = slice(hd * g, hd * (g + 1))
            outs.append(gt[:, 3 * hg:3 * hg + 1] * o_c[rs, ls] + gt[:, 3 * hg + 1:3 * hg + 2] * o_s[rs, ls]
                        + gt[:, 3 * hg + 2:3 * hg + 3] * o_w[rs, ls])
        o_ref[...] = jnp.concatenate(outs, axis=1)


def _nsa_sample_tables(rel_bias, past, tq, n_c, w_state):
    rows = NSA_HEADS * tq
    total = past + tq
    n_cmp = total // CMP_STRIDE - CMP_BLOCK // CMP_STRIDE + 1
    n_slc = -(-total // SEL_BLOCK)
    pos = past + jnp.arange(tq)
    c_hi = jnp.arange(n_c) * CMP_STRIDE + CMP_BLOCK - 1
    dist_c = pos[:, None] - c_hi[None, :]
    bc = _bias_tile(rel_bias, dist_c, (dist_c >= 0) & (jnp.arange(n_c) < n_cmp)[None, :]).reshape(rows, n_c)
    n_steps = past // (SMP_PP * PAGE)
    d_ext = jnp.arange(tq + past - 1) + 1
    bs = _toeplitz(_bias_tile(rel_bias, d_ext, d_ext >= 0), tq, past)
    bs = bs.reshape(rows, n_steps, SMP_PP * PAGE).transpose(1, 0, 2)
    jn = jnp.arange(PAGE)
    dist_n = jnp.arange(tq)[:, None] - jn[None, :]
    new_ok = (dist_n >= 0) & (jn < tq)[None, :]
    bsn = _bias_tile(rel_bias, dist_n, new_ok).reshape(rows, PAGE)
    dist_w = jnp.arange(tq)[:, None] + w_state - jnp.arange(w_state)[None, :]
    bw = _bias_tile(rel_bias, dist_w, (dist_w >= 0) & (dist_w <= NSA_WINDOW)).reshape(rows, w_state)
    bwn = _bias_tile(rel_bias, dist_n, new_ok & (dist_n <= NSA_WINDOW)).reshape(rows, PAGE)
    c_lo = np.arange(n_c) * CMP_STRIDE
    s_lo = np.arange(SEL_LANES) * SEL_BLOCK
    ov = ((c_lo[:, None] <= s_lo[None, :] + SEL_BLOCK - 1) & (c_lo[:, None] + CMP_BLOCK - 1 >= s_lo[None, :])
          & (np.arange(n_c) < n_cmp)[:, None] & (np.arange(SEL_LANES) < n_slc)[None, :])
    blk = np.arange(SEL_LANES)
    ex = (blk[:, None] == (np.arange(past)[None, :] // SEL_BLOCK)).reshape(SEL_LANES, n_steps, SMP_PP * PAGE)
    ex = ex.transpose(1, 0, 2)
    exn = (blk[:, None] == ((past + np.arange(PAGE))[None, :] // SEL_BLOCK)) & (np.arange(PAGE) < tq)[None, :]
    as_bf = lambda a: jnp.asarray(a.astype(np.float32), BF16)
    return bc, bs, bsn, bw, bwn, as_bf(ov), as_bf(ex), as_bf(exn), n_slc


def _nsa_sample(q, gates, crows, slc_pool_t, table, slc_new, win_state_t, win_new, q_gain, tables, *, tq):
    bc, bs, bsn, bw, bwn, ov, ex, exn, n_slc = tables
    bsz, npg = table.shape
    past = npg * PAGE
    n_c = crows.shape[1]
    w_state = win_state_t.shape[2]
    rows = NSA_HEADS * tq
    qw = NSA_HEADS * HEAD_DIM
    page = lambda i: pl.BlockSpec((1, ROWS_NSA, PAGE), lambda b, j, pt: (pt[b, j * SMP_PP + i], 0, 0))
    per_b = lambda shape: pl.BlockSpec((1,) + shape, lambda b, j, pt: (b,) + (0,) * len(shape))
    new_rows = pl.BlockSpec((tq, ROWS_NSA), lambda b, j, pt: (b, 0))
    return pl.pallas_call(
        functools.partial(_nsa_sample_body, n_slc=n_slc, k_sel=min(SEL_TOPK, n_slc), past=past, tq=tq),
        grid_spec=pltpu.PrefetchScalarGridSpec(
            num_scalar_prefetch=1, grid=(bsz, npg // SMP_PP),
            in_specs=[pl.BlockSpec((tq, qw), lambda b, j, pt: (b, 0)),
                      pl.BlockSpec((NSA_KV, tq, LANES), lambda b, j, pt: (0, b, 0)),
                      per_b((n_c, ROWS_NSA))] + [page(i) for i in range(SMP_PP)] + [
                      new_rows, per_b((ROWS_NSA, w_state)), new_rows,
                      _const((1, qw)), _const((qw, qw)), _const(bc.shape), _const(bs.shape), _const(bsn.shape),
                      _const(bw.shape), _const(bwn.shape), _const(ov.shape), _const(ex.shape), _const(exn.shape)],
            out_specs=pl.BlockSpec((tq, qw), lambda b, j, pt: (b, 0)),
            scratch_shapes=[pltpu.VMEM((rows, NSA_KW), BF16), pltpu.VMEM((rows, SEL_LANES), BF16),
                            pltpu.VMEM((ROWS_NSA, SMP_PP * PAGE), BF16), pltpu.VMEM((rows, 1), F32),
                            pltpu.VMEM((rows, 1), F32), pltpu.VMEM((rows, NSA_KW), F32),
                            pltpu.VMEM((rows, NSA_KW), F32), pltpu.VMEM((rows, NSA_KW), F32),
                            pltpu.VMEM((PAGE, ROWS_NSA), F32)]),
        out_shape=jax.ShapeDtypeStruct((bsz * tq, qw), F32),
        compiler_params=_cparams("parallel", "arbitrary"), name="nsa_sample",
    )(table, q, gates, crows, *([slc_pool_t] * SMP_PP), slc_new, win_state_t, win_new,
      jnp.tile(q_gain, NSA_HEADS).reshape(1, qw), _block_diag_ones(qw), bc, bs, bsn, bw, bwn, ov, ex, exn)


DIL_TQ = 128


def _head_of_lane(shape):
    return lax.broadcasted_iota(jnp.int32, shape, 1) >> 6


def _stack_heads(qt, lane_head):
    return jnp.concatenate([jnp.where(lane_head == h, qt, 0.0) for h in range(DIL_HEADS)], axis=0).astype(BF16)


def _unstack_heads(r, lane_head, t):
    return sum(jnp.where(lane_head == h, r[h * t:(h + 1) * t], 0.0) for h in range(DIL_HEADS))


def _dil_prompt_body(q_ref, kv_ref, bias_ref, o_ref, lse_ref, *, n_sub, dil):
    tq = DIL_TQ
    gw = ODD_GW
    lane_head = _head_of_lane((tq, gw))
    for r in range(dil):
        ql = slice(gw * r, gw * (r + 1))
        kl = slice(2 * gw * r, 2 * gw * r + gw)
        vl = slice(2 * gw * r + gw, 2 * gw * (r + 1))

        def tile(t, carry, ql=ql, kl=kl, vl=vl):
            start = pl.multiple_of(t * tq, tq)
            prev = pl.multiple_of(jnp.maximum(t - 1, 0) * tq, tq)
            cur_rows = pl.ds(start, tq)
            prev_rows = pl.ds(prev, tq)
            kk = jnp.concatenate([kv_ref[prev_rows, kl], kv_ref[cur_rows, kl]], axis=0).astype(BF16)
            vv = jnp.concatenate([kv_ref[prev_rows, vl], kv_ref[cur_rows, vl]], axis=0).astype(BF16)
            bias = bias_ref[jnp.minimum(t, 1)]
            s = _dot_nt(_stack_heads(q_ref[cur_rows, ql], lane_head), kk) + bias
            m = jnp.max(s, axis=-1, keepdims=True)
            e = jnp.where(bias > NEG_TEST, jnp.exp(s - m), 0.0)
            l = jnp.maximum(jnp.sum(e, axis=-1, keepdims=True), 1e-30)
            res = _dot((e / l).astype(BF16), vv)
            lse = jnp.broadcast_to(m + jnp.log(l), (DIL_HEADS * tq, gw))
            o_ref[cur_rows, ql] = _unstack_heads(res, lane_head, tq)
            lse_ref[cur_rows, ql] = _unstack_heads(lse, lane_head, tq)
            return carry

        lax.fori_loop(0, n_sub // tq, tile, 0)


def _merge_groups(outs, lses):
    mx = functools.reduce(jnp.maximum, lses)
    ws = [jnp.exp(l - mx) for l in lses]
    return sum(w * o for w, o in zip(ws, outs)) / sum(ws)


def _dil_prompt_tables(rel_bias):
    tq = DIL_TQ
    m_ext = jnp.arange(3 * tq - 1) - (tq - 1)
    cur_tile = jnp.arange(2 * tq) >= tq
    tiles = []
    for gi, (win, dil) in enumerate(DIL_PATTERN):
        assert win // dil == tq
        t = _toeplitz(_bias_tile(rel_bias, m_ext * dil, (m_ext >= 0) & (m_ext <= tq)), tq, 2 * tq)
        t = t[gi * DIL_HEADS:(gi + 1) * DIL_HEADS].reshape(DIL_HEADS * tq, 2 * tq)
        tiles.append(jnp.stack([jnp.where(cur_tile, t, NEG), t]))
    return tiles


def _dil_prompt(q3, kv3, tables, *, batch, seq):
    gw = ODD_GW
    n = batch * seq
    outs, lses = [], []
    for gi, (_, dil) in enumerate(DIL_PATTERN):
        n_sub = seq // dil
        assert n_sub % DIL_TQ == 0
        qv = q3[gi].reshape(n // dil, dil * gw)
        kvv = kv3[gi].reshape(n // dil, dil * 2 * gw)
        o_spec = pl.BlockSpec((n_sub, dil * gw), lambda b: (b, 0))
        o, lse = pl.pallas_call(
            functools.partial(_dil_prompt_body, n_sub=n_sub, dil=dil), grid=(batch,),
            in_specs=[o_spec, pl.BlockSpec((n_sub, dil * 2 * gw), lambda b: (b, 0)), _const(tables[gi].shape)],
            out_specs=[o_spec, o_spec],
            out_shape=[jax.ShapeDtypeStruct(qv.shape, F32)] * 2,
            compiler_params=_cparams("parallel"), name=f"dilated_prompt_g{gi}",
        )(qv, kvv, tables[gi])
        outs.append(o.reshape(n, gw))
        lses.append(lse.reshape(n, gw))
    return outs, lses


def _dil_sample_body(q_ref, kvn_ref, st0_ref, st1_ref, st2_ref, b0_ref, b1_ref, b2_ref, bn_ref, o_ref, lse_ref, pad_scr,
                     *, tq):
    gw = ODD_GW
    lane_head = _head_of_lane((tq, gw))
    rows = DIL_HEADS * tq
    pad_scr[...] = jnp.zeros(pad_scr.shape, F32)
    for gi, (st_ref, b_ref) in enumerate(((st0_ref, b0_ref), (st1_ref, b1_ref), (st2_ref, b2_ref))):
        qs = _stack_heads(q_ref[gi], lane_head)
        st = st_ref[0].astype(BF16)
        pad_scr[0:tq, :] = kvn_ref[gi]
        kn = pad_scr[...].astype(BF16)
        bias = b_ref[...]
        acc = (jnp.full((rows, 1), NEG, F32), jnp.zeros((rows, 1), F32), jnp.zeros((rows, gw), F32))
        acc = _softmax_piece(_dot(qs, st[:gw]) + bias, bias > NEG_TEST, *acc, st[gw:], v_t=True)
        bias_n = bn_ref[gi]
        m, l, a = _softmax_piece(_dot_nt(qs, kn[:, :gw]) + bias_n, bias_n > NEG_TEST, *acc, kn[:, gw:])
        l = jnp.maximum(l, 1e-30)
        o_ref[gi] = _unstack_heads(a / l, lane_head, tq)
        lse_ref[gi] = _unstack_heads(jnp.broadcast_to(m + jnp.log(l), (rows, gw)), lane_head, tq)


def _dil_sample_tables(rel_bias, tq, state_lens):
    rows = DIL_HEADS * tq
    i = jnp.arange(tq)[:, None]
    per_state, per_new = [], []
    jn = jnp.arange(PAGE)[None, :]
    for gi, ((win, dil), ln) in enumerate(zip(DIL_PATTERN, state_lens)):
        hs = slice(gi * DIL_HEADS, (gi + 1) * DIL_HEADS)
        d = i + ln - jnp.arange(ln)[None, :]
        per_state.append(_bias_tile(rel_bias, d, (d % dil == 0) & (d <= win))[hs].reshape(rows, ln))
        dn = i - jn
        per_new.append(_bias_tile(rel_bias, dn, (dn >= 0) & (dn % dil == 0) & (dn <= win) & (jn < tq))[hs]
                       .reshape(rows, PAGE))
    return per_state, jnp.stack(per_new)


def _dil_sample(q3, kv3, states, tables, *, tq):
    per_state, bias_new = tables
    bsz = states[0].shape[0]
    gw = ODD_GW
    st_spec = lambda s: pl.BlockSpec((1,) + s.shape[1:], lambda b: (b, 0, 0))
    grp = lambda w: pl.BlockSpec((3, tq, w), lambda b: (0, b, 0))
    return pl.pallas_call(
        functools.partial(_dil_sample_body, tq=tq), grid=(bsz,),
        in_specs=[grp(gw), grp(2 * gw)] + [st_spec(s) for s in states] + [_const(t.shape) for t in per_state]
        + [_const(bias_new.shape)],
        out_specs=[grp(gw), grp(gw)],
        out_shape=[jax.ShapeDtypeStruct((3, bsz * tq, gw), F32)] * 2,
        scratch_shapes=[pltpu.VMEM((PAGE, 2 * gw), F32)],
        compiler_params=_cparams("parallel"), name="dilated_sample",
    )(q3, kv3, *states, *per_state, bias_new)


def _pad_even_w(w_in):
    return jnp.concatenate([w_in, jnp.zeros((w_in.shape[0], EVEN_PAD - w_in.shape[1]), w_in.dtype)], axis=1).astype(BF16)


def _even_mixer_prompt(h, gain, w_in, w_out, lb, hg_norm, qk_norm, cmp_pe, cmp_w1, cmp_w2, rel_bias, *, batch, seq, tm):
    n = batch * seq
    hg, q, cmp_rows, slc_rows, win_rows, slcg, wing, _, gates_t = _inproj_even(
        h, gain, _pad_even_w(w_in), qk_norm, tm=tm, gates_t=True)
    s0 = jnp.zeros((batch, HG_HEADS, HEAD_DIM, HEAD_DIM), F32)
    o_h, s_fin = _hgrn(hg, lb, hg_norm, s0, batch=batch, seq=seq)
    cw = _compress_weights(cmp_pe, cmp_w1, cmp_w2, qk_norm[1])
    npg = seq // PAGE
    table = jnp.arange(batch * npg, dtype=jnp.int32).reshape(batch, npg)
    _, kcv = _compress(cmp_rows.reshape(n // PAGE, PAGE, ROWS_NSA), table, cw, transposed=False)
    tables = _nsa_prompt_t_tables(rel_bias, seq, kcv.shape[2])
    o_nt = _nsa_prompt_t(q, gates_t, kcv, slcg, wing, qk_norm[0], tables, batch=batch, seq=seq)
    wo = w_out.astype(BF16)
    hw = HG_HEADS * HEAD_DIM
    h = _outproj2t(h, o_h, o_nt, wo[:hw], wo[hw:], tm=tm)
    return h, {"hgrn": s_fin, "cmp": cmp_rows, "slc": slc_rows, "win": win_rows}


def _even_mixer_sample(h, gain, w_in, w_out, lb, hg_norm, qk_norm, cmp_pe, cmp_w1, cmp_w2, rel_bias,
                       state_hgrn, cmp_pool, slc_pool, win_state, page_table, *, batch, tq, tm):
    assert tq < CMP_STRIDE and tq & (tq - 1) == 0
    hg, q, cmp_rows, slc_rows, win_rows, _, _, gates = _inproj_even(h, gain, _pad_even_w(w_in), qk_norm, tm=tm)
    o_h, s_fin = _hgrn(hg, lb, hg_norm, state_hgrn, batch=batch, seq=tq)
    cw = _compress_weights(cmp_pe, cmp_w1, cmp_w2, qk_norm[1])
    rows_t = lambda a: jnp.moveaxis(a, 1, -1).reshape(a.shape[0], ROWS_NSA, a.shape[1])
    crows, _ = _compress(rows_t(cmp_pool), page_table, cw, transposed=True)
    past = page_table.shape[1] * PAGE
    w_state = win_state.shape[1]
    tables = _nsa_sample_tables(rel_bias, past, tq, crows.shape[1], w_state)
    o_n = _nsa_sample(q, gates, crows, rows_t(slc_pool), page_table, slc_rows, rows_t(win_state), win_rows,
                      qk_norm[0], tables, tq=tq)
    wo = w_out.astype(BF16)
    hw = HG_HEADS * HEAD_DIM
    h = _outproj2(h, o_h, o_n, wo[:hw], wo[hw:], tm=tm)
    new_win = jnp.concatenate([win_state[:, tq:], win_rows.reshape((batch, tq) + win_state.shape[2:])], axis=1)
    return h, {"hgrn": s_fin, "cmp": cmp_rows, "slc": slc_rows, "win": new_win}


def _odd_mixer_prompt(h, gain, w_in, w_out, qk_norm, rel_bias, *, batch, seq, tm):
    q3, kv3 = _inproj_odd(h, gain, w_in.astype(BF16), qk_norm, tm=tm)
    outs, lses = _dil_prompt(q3, kv3, _dil_prompt_tables(rel_bias), batch=batch, seq=seq)
    return _outproj_merge(h, outs, lses, w_out.astype(BF16), tm=tm), kv3


def _odd_mixer_sample(h, gain, w_in, w_out, qk_norm, rel_bias, states, *, batch, tq, tm):
    q3, kv3 = _inproj_odd(h, gain, w_in.astype(BF16), qk_norm, tm=tm)
    tables = _dil_sample_tables(rel_bias, tq, [s.shape[1] for s in states])
    states_t = [jnp.moveaxis(s, 1, -1).reshape(batch, 2 * ODD_GW, s.shape[1]) for s in states]
    o3, lse3 = _dil_sample(q3, kv3, states_t, tables, tq=tq)
    new = [jnp.concatenate([s[:, tq:], kv3[g].reshape((batch, tq) + s.shape[2:])], axis=1) for g, s in enumerate(states)]
    h = _outproj_merge(h, [o3[g] for g in range(3)], [lse3[g] for g in range(3)], w_out.astype(BF16), tm=tm)
    return h, new


TM_PROMPT = 512
TM_SAMPLE = 256


def kernel(x_prompt, x_sample, p_prompt, p_sample, state_hgrn, cache_nsa_cmp_kv, cache_nsa_slc_kv, state_nsa_win_kv,
           state_dil_kv_0, state_dil_kv_1, state_dil_kv_2, page_table, rel_bias, norm_ffn1, norm_mix, norm_ffn2,
           norm_ple, w_ffn1_in, w_ffn1_out, w_ffn2_in, w_ffn2_out, w_ple_gate, w_ple_proj, w_in_even, w_out_even,
           hgrn_lb_logits, hgrn_norm, nsa_qk_norm, nsa_cmp_pe, nsa_cmp_w1, nsa_cmp_w2, w_in_odd, w_out_odd,
           dil_qk_norm):
    depth = norm_ffn1.shape[0]
    dil_states = (state_dil_kv_0, state_dil_kv_1, state_dil_kv_2)
    bf = lambda w: w.astype(BF16)
    w1i, w1o, w2i, w2o, wpg, wpp = (bf(w) for w in (w_ffn1_in, w_ffn1_out, w_ffn2_in, w_ffn2_out, w_ple_gate,
                                                     w_ple_proj))
    lb_all = jnp.cumsum(jax.nn.softmax(hgrn_lb_logits.astype(F32), axis=0), axis=0)

    def run(x, p, sample):
        batch, seq, d = x.shape
        n = batch * seq
        tm = TM_SAMPLE if sample else TM_PROMPT
        h = x.reshape(n, d)
        p = p.reshape(depth, n, p.shape[-1])
        even, odd = [], []
        for li in range(depth):
            h = _ffn(h, norm_ffn1[li], w1i[li], w1o[li], tm=tm)
            if li % 2 == 0:
                ei = li // 2
                args = (h, norm_mix[li], w_in_even[ei], w_out_even[ei], lb_all[ei], hgrn_norm[ei], nsa_qk_norm[ei],
                        nsa_cmp_pe[ei], nsa_cmp_w1[ei], nsa_cmp_w2[ei], rel_bias)
                if sample:
                    h, st = _even_mixer_sample(*args, state_hgrn[ei], cache_nsa_cmp_kv[ei], cache_nsa_slc_kv[ei],
                                               state_nsa_win_kv[ei], page_table, batch=batch, tq=seq, tm=tm)
                else:
                    h, st = _even_mixer_prompt(*args, batch=batch, seq=seq, tm=tm)
                    st["win"] = st["win"].reshape(batch, seq, ROWS_NSA)[:, -min(NSA_WINDOW, seq):]
                even.append(st)
            else:
                oi = li // 2
                args = (h, norm_mix[li], w_in_odd[oi], w_out_odd[oi], dil_qk_norm[oi], rel_bias)
                if sample:
                    h, bufs = _odd_mixer_sample(*args, [s[oi] for s in dil_states], batch=batch, tq=seq, tm=tm)
                else:
                    h, kv3 = _odd_mixer_prompt(*args, batch=batch, seq=seq, tm=tm)
                    bufs = [kv3[g].reshape(batch, seq, 2 * ODD_GW)[:, -min(w, seq):]
                            for g, (w, _) in enumerate(DIL_PATTERN)]
                odd.append(bufs)
            h = _ffn(h, norm_ffn2[li], w2i[li], w2o[li], ple=(p[li], norm_ple[li], wpg[li], wpp[li]), tm=tm)
        rows = lambda a: a.reshape(batch, -1, 2, NSA_KV, HEAD_DIM)
        drows = lambda a: a.reshape(batch, -1, 2, DIL_HEADS, HEAD_DIM)
        return (h.reshape(batch, seq, d),
                jnp.stack([s["hgrn"] for s in even]), jnp.stack([rows(s["cmp"]) for s in even]),
                jnp.stack([rows(s["slc"]) for s in even]), jnp.stack([rows(s["win"]) for s in even]),
                jnp.stack([drows(b[0]) for b in odd]), jnp.stack([drows(b[1]) for b in odd]),
                jnp.stack([drows(b[2]) for b in odd]))

    y_p, hg_p, cmp_p, slc_p, win_p, d0_p, d1_p, d2_p = run(x_prompt, p_prompt, False)
    y_s, hg_s, cmp_s, slc_s, win_s, d0_s, d1_s, d2_s = run(x_sample, p_sample, True)
    return (y_p, y_s, hg_p, hg_s, cmp_p, cmp_s, slc_p, slc_s, win_p, win_s, d0_p, d0_s, d1_p, d1_s, d2_p, d2_s)
```

```python
import functools
import math

import numpy as np
import jax
import jax.numpy as jnp
from jax import lax
from jax.experimental import pallas as pl
from jax.experimental.pallas import tpu as pltpu

F32 = jnp.float32
BF16 = jnp.bfloat16
HIGHEST = lax.Precision.HIGHEST

V7X_VMEM_BYTES = 64 * 1024 * 1024
VMEM_LIMIT = V7X_VMEM_BYTES * 7 // 8
LANES = 128

EPS = 1e-6
NEG = -1e30
NEG_TEST = -1e29

HEAD_DIM = 64
HG_HEADS = 8
HG_CHUNK = 64
NSA_HEADS = 12
NSA_KV = 3
NSA_GROUP = NSA_HEADS // NSA_KV
CMP_BLOCK = 32
CMP_STRIDE = 16
SEL_BLOCK = 64
SEL_TOPK = 16
NSA_WINDOW = 512
FORCE_SCORE = 1e4
PAGE = 128
DIL_PATTERN = ((128, 1), (512, 4), (2048, 16))
DIL_HEADS = 4
RP_BUCKETS = 32
RP_MAX_DIST = 2048


def _cparams(*sem):
    return pltpu.CompilerParams(dimension_semantics=sem, vmem_limit_bytes=VMEM_LIMIT)


def _const(shape):
    nd = len(shape)
    return pl.BlockSpec(shape, lambda *_: (0,) * nd, pipeline_mode=pl.Buffered(1))


def _dot(a, b):
    return jnp.dot(a, b, preferred_element_type=F32)


def _dot_nt(a, b):
    return lax.dot_general(a, b, (((1,), (1,)), ((), ())), preferred_element_type=F32)


def _dot_tn(a, b):
    return lax.dot_general(a, b, (((0,), (0,)), ((), ())), preferred_element_type=F32)


def _rms(x, gain):
    return x * lax.rsqrt(jnp.mean(x * x, axis=-1, keepdims=True) + EPS) * gain


def _seg_ms(x, bd):
    sq = x * x
    hi = sq.astype(BF16)
    lo = (sq - hi.astype(F32)).astype(BF16)
    return (_dot(hi, bd) + _dot(lo, bd)) * (1.0 / HEAD_DIM)


def _seg_rms(x, bd, gain):
    return x * lax.rsqrt(_seg_ms(x, bd) + EPS) * gain


def _norm_k_rows(x, bd, gain, kmask):
    r = lax.rsqrt(_seg_ms(x, bd) + EPS)
    return x * jnp.where(kmask > 0.5, r, 1.0) * gain


def _block_diag_ones(width):
    i = np.arange(width) // HEAD_DIM
    return jnp.asarray((i[:, None] == i[None, :]).astype(np.float32), BF16)


def _rel_bucket(dist):
    exact = RP_BUCKETS // 2
    d = jnp.maximum(dist, 0)
    log_ratio = jnp.log(jnp.maximum(d, 1).astype(F32) / exact) / math.log(RP_MAX_DIST / exact)
    large = jnp.minimum(exact + (log_ratio * (RP_BUCKETS - exact)).astype(jnp.int32), RP_BUCKETS - 1)
    return jnp.where(d < exact, d, large)


def _bias_tile(rel_bias, dist, valid):
    onehot = (_rel_bucket(dist)[..., None] == jnp.arange(RP_BUCKETS)).astype(F32)
    t = jnp.einsum('...k,kh->h...', onehot, rel_bias.astype(F32), precision=HIGHEST)
    return jnp.where(valid[None], t, NEG)


def _toeplitz(ext, n, m):
    p = n + m - 1
    u = jnp.concatenate([jnp.flip(ext[..., :m], -1), jnp.flip(ext[..., m:], -1)], axis=-1)
    reps = (1,) * (ext.ndim - 1) + (n,)
    t = jnp.tile(u, reps)[..., :n * (p - 1)].reshape(ext.shape[:-1] + (n, p - 1))
    return t[..., :m]


def _softmax_piece(s, mask, m_prev, l_prev, acc_prev, v, v_t=False):
    m_new = jnp.maximum(m_prev, jnp.max(s, axis=-1, keepdims=True))
    alpha = jnp.exp(m_prev - m_new)
    p = jnp.where(mask, jnp.exp(s - m_new), 0.0)
    l_new = alpha * l_prev + jnp.sum(p, axis=-1, keepdims=True)
    pb = p.astype(BF16)
    acc_new = alpha * acc_prev + (_dot_nt(pb, v) if v_t else _dot(pb, v))
    return m_new, l_new, acc_new


def _ffn_body(x_ref, g_ref, win_ref, wout_ref, *rest, dff, ple):
    x = x_ref[...]
    xn = _rms(x, g_ref[...]).astype(BF16)
    gu = _dot(xn, win_ref[...])
    gg = gu[:, :dff]
    a = (gg * jax.nn.sigmoid(gg) * gu[:, dff:]).astype(BF16)
    h = x + 0.5 * _dot(a, wout_ref[...])
    if ple:
        p_ref, gp_ref, wg_ref, wp_ref, o_ref = rest
        hn = _rms(h, gp_ref[...]).astype(BF16)
        gate = jax.nn.sigmoid(_dot(hn, wg_ref[...]))
        h = h + gate * _dot(p_ref[...].astype(BF16), wp_ref[...])
    else:
        (o_ref,) = rest
    o_ref[...] = h


def _ffn(h, gain, w_in, w_out, ple=None, *, tm):
    n, d = h.shape
    dff = w_out.shape[0]
    row = lambda i: (i, 0)
    in_specs = [pl.BlockSpec((tm, d), row), _const((1, d)), _const(w_in.shape), _const(w_out.shape)]
    args = [h, gain.reshape(1, d), w_in, w_out]
    if ple is not None:
        p, gp, wg, wp = ple
        in_specs += [pl.BlockSpec((tm, p.shape[1]), row), _const((1, d)), _const(wg.shape), _const(wp.shape)]
        args += [p, gp.reshape(1, d), wg, wp]
    return pl.pallas_call(
        functools.partial(_ffn_body, dff=dff, ple=ple is not None),
        grid=(n // tm,), in_specs=in_specs, out_specs=pl.BlockSpec((tm, d), row),
        out_shape=jax.ShapeDtypeStruct((n, d), F32), compiler_params=_cparams("parallel"),
        name="ffn_ple" if ple is not None else "ffn",
    )(*args)


EVEN_HG = 4 * HG_HEADS * HEAD_DIM
EVEN_Q = NSA_HEADS * HEAD_DIM
ROWS_NSA = 2 * NSA_KV * HEAD_DIM
EVEN_PAD = EVEN_HG + EVEN_Q + 3 * ROWS_NSA + LANES


GATE_ROWS = 16


def _inproj_even_body(x_ref, g_ref, w_ref, gk_ref, km_ref, bd_ref,
                      hg_ref, q_ref, cmp_ref, slc_ref, win_ref, slcg_ref, wing_ref, gate_ref, *gate_t_ref):
    xn = _rms(x_ref[...], g_ref[...]).astype(BF16)
    p = _dot(xn, w_ref[...])
    c0 = EVEN_HG
    c1 = c0 + EVEN_Q
    hg_ref[...] = p[:, :c0]
    q_ref[...] = p[:, c0:c1]
    cmp_ref[...] = p[:, c1:c1 + ROWS_NSA]
    bd = bd_ref[...]
    km = km_ref[...]
    slc = _norm_k_rows(p[:, c1 + ROWS_NSA:c1 + 2 * ROWS_NSA], bd, gk_ref[0:1, :], km)
    win = _norm_k_rows(p[:, c1 + 2 * ROWS_NSA:c1 + 3 * ROWS_NSA], bd, gk_ref[1:2, :], km)
    slc_ref[...] = slc
    win_ref[...] = win
    sig = jax.nn.sigmoid(p[:, c1 + 3 * ROWS_NSA:])
    kw = NSA_KV * HEAD_DIM
    for g in range(NSA_KV):
        ks = slice(HEAD_DIM * g, HEAD_DIM * (g + 1))
        vs = slice(kw + HEAD_DIM * g, kw + HEAD_DIM * (g + 1))
        slcg_ref[g] = jnp.concatenate([slc[:, ks], slc[:, vs]], axis=1)
        wing_ref[g] = jnp.concatenate([win[:, ks], win[:, vs]], axis=1)
        gate_ref[g] = sig if g == 0 else pltpu.roll(sig, LANES - 3 * NSA_GROUP * g, 1)
    if gate_t_ref:
        sig_t = sig.T
        for g in range(NSA_KV):
            for half in range(sig.shape[0] // NSA_TQ):
                gate_t_ref[0][g, half] = sig_t[3 * NSA_GROUP * g:3 * NSA_GROUP * g + GATE_ROWS,
                                               half * NSA_TQ:(half + 1) * NSA_TQ]


def _inproj_even(h, gain, w_pad, qk_norm, *, tm, gates_t=False):
    n, d = h.shape
    kw = NSA_KV * HEAD_DIM
    ones = jnp.ones((kw,), F32)
    gk = jnp.stack([jnp.concatenate([jnp.tile(qk_norm[2], NSA_KV), ones]),
                    jnp.concatenate([jnp.tile(qk_norm[3], NSA_KV), ones])])
    km = jnp.concatenate([ones, 0.0 * ones]).reshape(1, ROWS_NSA)
    row = lambda i: (i, 0)
    grp = lambda i: (0, i, 0)
    shp = lambda *s: jax.ShapeDtypeStruct(s, F32)
    out_specs = [pl.BlockSpec((tm, EVEN_HG), row), pl.BlockSpec((tm, EVEN_Q), row),
                 pl.BlockSpec((tm, ROWS_NSA), row), pl.BlockSpec((tm, ROWS_NSA), row),
                 pl.BlockSpec((tm, ROWS_NSA), row), pl.BlockSpec((NSA_KV, tm, LANES), grp),
                 pl.BlockSpec((NSA_KV, tm, LANES), grp), pl.BlockSpec((NSA_KV, tm, LANES), grp)]
    out_shape = [shp(n, EVEN_HG), shp(n, EVEN_Q), shp(n, ROWS_NSA), shp(n, ROWS_NSA), shp(n, ROWS_NSA),
                 shp(NSA_KV, n, LANES), shp(NSA_KV, n, LANES), shp(NSA_KV, n, LANES)]
    if gates_t:
        out_specs.append(pl.BlockSpec((NSA_KV, tm // NSA_TQ, GATE_ROWS, NSA_TQ), lambda i: (0, i, 0, 0)))
        out_shape.append(shp(NSA_KV, n // NSA_TQ, GATE_ROWS, NSA_TQ))
    return pl.pallas_call(
        _inproj_even_body, grid=(n // tm,),
        in_specs=[pl.BlockSpec((tm, d), row), _const((1, d)), _const(w_pad.shape), _const((2, ROWS_NSA)),
                  _const((1, ROWS_NSA)), _const((ROWS_NSA, ROWS_NSA))],
        out_specs=out_specs, out_shape=out_shape,
        compiler_params=_cparams("parallel"), name="inproj_even",
    )(h, gain.reshape(1, d), w_pad, gk, km, _block_diag_ones(ROWS_NSA))


ODD_W = 3 * DIL_HEADS * HEAD_DIM
ODD_GW = DIL_HEADS * HEAD_DIM


def _inproj_odd_body(x_ref, g_ref, w_ref, gq_ref, gk_ref, bd_ref, q_ref, kv_ref):
    xn = _rms(x_ref[...], g_ref[...]).astype(BF16)
    p = _dot(xn, w_ref[...])
    bd = bd_ref[...]
    q = _seg_rms(p[:, :ODD_W], bd, gq_ref[...]) * (HEAD_DIM ** -0.5)
    k = _seg_rms(p[:, ODD_W:2 * ODD_W], bd, gk_ref[...])
    v = p[:, 2 * ODD_W:]
    for g in range(3):
        s = slice(ODD_GW * g, ODD_GW * (g + 1))
        q_ref[g] = q[:, s]
        kv_ref[g] = jnp.concatenate([k[:, s], v[:, s]], axis=1)


def _inproj_odd(h, gain, w, qk_norm, *, tm):
    n, d = h.shape
    row = lambda i: (i, 0)
    nh = 3 * DIL_HEADS
    return pl.pallas_call(
        _inproj_odd_body, grid=(n // tm,),
        in_specs=[pl.BlockSpec((tm, d), row), _const((1, d)), _const(w.shape), _const((1, ODD_W)),
                  _const((1, ODD_W)), _const((ODD_W, ODD_W))],
        out_specs=[pl.BlockSpec((3, tm, ODD_GW), lambda i: (0, i, 0)),
                   pl.BlockSpec((3, tm, 2 * ODD_GW), lambda i: (0, i, 0))],
        out_shape=[jax.ShapeDtypeStruct((3, n, ODD_GW), F32), jax.ShapeDtypeStruct((3, n, 2 * ODD_GW), F32)],
        compiler_params=_cparams("parallel"), name="inproj_odd",
    )(h, gain.reshape(1, d), w, jnp.tile(qk_norm[0], nh).reshape(1, ODD_W),
      jnp.tile(qk_norm[1], nh).reshape(1, ODD_W), _block_diag_ones(ODD_W))


def _outproj2_body(h_ref, a_ref, b_ref, wa_ref, wb_ref, o_ref):
    o_ref[...] = (h_ref[...] + _dot(a_ref[...].astype(BF16), wa_ref[...])
                  + _dot(b_ref[...].astype(BF16), wb_ref[...]))


def _outproj2(h, a, b, wa, wb, *, tm):
    n, d = h.shape
    row = lambda i: (i, 0)
    return pl.pallas_call(
        _outproj2_body, grid=(n // tm,),
        in_specs=[pl.BlockSpec((tm, d), row), pl.BlockSpec((tm, a.shape[1]), row),
                  pl.BlockSpec((tm, b.shape[1]), row), _const(wa.shape), _const(wb.shape)],
        out_specs=pl.BlockSpec((tm, d), row), out_shape=jax.ShapeDtypeStruct((n, d), F32),
        compiler_params=_cparams("parallel"), name="outproj_even",
    )(h, a, b, wa, wb)


def _outproj2t_body(h_ref, a_ref, bt_ref, wa_ref, wb_ref, o_ref):
    acc = h_ref[...] + _dot(a_ref[...].astype(BF16), wa_ref[...])
    tq = bt_ref.shape[2]
    for i in range(bt_ref.shape[0]):
        rs = slice(i * tq, (i + 1) * tq)
        o_ref[rs, :] = acc[rs, :] + _dot_tn(bt_ref[i].astype(BF16), wb_ref[...])


def _outproj2t(h, a, bt, wa, wb, *, tm):
    n, d = h.shape
    row = lambda i: (i, 0)
    tq = bt.shape[2]
    return pl.pallas_call(
        _outproj2t_body, grid=(n // tm,),
        in_specs=[pl.BlockSpec((tm, d), row), pl.BlockSpec((tm, a.shape[1]), row),
                  pl.BlockSpec((tm // tq, bt.shape[1], tq), lambda i: (i, 0, 0)), _const(wa.shape), _const(wb.shape)],
        out_specs=pl.BlockSpec((tm, d), row), out_shape=jax.ShapeDtypeStruct((n, d), F32),
        compiler_params=_cparams("parallel"), name="outproj_even_t",
    )(h, a, bt, wa, wb)


def _outproj_merge_body(h_ref, o0_ref, o1_ref, o2_ref, l0_ref, l1_ref, l2_ref, w_ref, o_ref):
    mixed = _merge_groups([o0_ref[...], o1_ref[...], o2_ref[...]], [l0_ref[...], l1_ref[...], l2_ref[...]])
    o_ref[...] = h_ref[...] + _dot(mixed.astype(BF16), w_ref[...])


def _outproj_merge(h, outs, lses, w, *, tm):
    n, d = h.shape
    row = lambda i: (i, 0)
    part = pl.BlockSpec((tm, ODD_GW), row)
    return pl.pallas_call(
        _outproj_merge_body, grid=(n // tm,),
        in_specs=[pl.BlockSpec((tm, d), row)] + [part] * 6 + [_const(w.shape)],
        out_specs=pl.BlockSpec((tm, d), row), out_shape=jax.ShapeDtypeStruct((n, d), F32),
        compiler_params=_cparams("parallel"), name="outproj_odd",
    )(h, *outs, *lses, w)


def _hgrn_prefix_matrix(C):
    t = np.arange(C)[:, None]
    j = np.arange(C)[None, :]
    mats = [j <= t]
    lvl = 1
    while (1 << lvl) <= C:
        mats.append(j <= ((t >> lvl) << lvl) + (1 << lvl) // 2 - 1)
        lvl += 1
    return jnp.asarray(np.concatenate(mats, axis=0).astype(np.float32), BF16)


def _hgrn_body(hq_ref, hf_ref, hi_ref, hgt_ref, lb_ref, gn_ref, bd_ref, pm_ref, s0_ref, o_ref, sfin_ref, st_ref, *, C):
    c = pl.program_id(1)

    @pl.when(c == 0)
    def _():
        st_ref[...] = s0_ref[0]

    dk = HEAD_DIM
    w = HG_HEADS * dk
    lb = lb_ref[...]
    f = lb + (1.0 - lb) * jax.nn.sigmoid(hf_ref[...])
    lf = jnp.log(f)
    kk = 1.0 - f
    ti = lax.broadcasted_iota(jnp.int32, (C, C), 0)
    si = lax.broadcasted_iota(jnp.int32, (C, C), 1)
    p1 = lf.astype(BF16)
    r1 = lf - p1.astype(F32)
    p2 = r1.astype(BF16)
    p3 = (r1 - p2.astype(F32)).astype(BF16)
    sums = _dot(pm_ref[...], jnp.concatenate([p1, p2, p3], axis=1))
    sums = sums[:, :w] + sums[:, w:2 * w] + sums[:, 2 * w:]
    b = sums[0:C]
    q = hq_ref[...]
    v = hi_ref[...].astype(BF16)
    heads = range(HG_HEADS)
    hs = [slice(dk * h, dk * (h + 1)) for h in heads]

    qb = q.astype(BF16)
    kb = kk.astype(BF16)
    a = [jnp.where(ti == si, _dot_nt(qb[:, hs[h]], kb[:, hs[h]]), 0.0) for h in heads]
    lvl = 1
    while (1 << lvl) <= C:
        m = 1 << lvl
        half = m // 2
        r = sums[lvl * C:(lvl + 1) * C]
        e = jnp.exp(-jnp.abs(b - r))
        qt = (q * e).astype(BF16)
        kt = (kk * e).astype(BF16)
        pair = ((ti >> lvl) == (si >> lvl)) & ((ti & (m - 1)) >= half) & ((si & (m - 1)) < half)
        a = [a[h] + jnp.where(pair, _dot_nt(qt[:, hs[h]], kt[:, hs[h]]), 0.0) for h in heads]
        lvl += 1

    b_last = b[C - 1:C, :]
    qe = (q * jnp.exp(b)).astype(BF16)
    kdec = (kk * jnp.exp(b_last - b)).astype(BF16)
    e_last = jnp.exp(b_last)
    outs = []
    for h in heads:
        st = st_ref[h]
        outs.append(_dot(a[h].astype(BF16), v[:, hs[h]]) + _dot_nt(qe[:, hs[h]], st.astype(BF16)))
        st_ref[h] = st * e_last[:, hs[h]] + _dot_tn(v[:, hs[h]], kdec[:, hs[h]])
    o = jnp.concatenate(outs, axis=1)
    gt = hgt_ref[...]
    o_ref[...] = _seg_rms(o, bd_ref[...], gn_ref[...]) * (gt * jax.nn.sigmoid(gt))

    @pl.when(c == pl.num_programs(1) - 1)
    def _():
        sfin_ref[0] = st_ref[...]


def _hgrn(hg, lb, hg_norm, s0, *, batch, seq):
    C = math.gcd(seq, HG_CHUNK)
    nc = seq // C
    w = HG_HEADS * HEAD_DIM
    col = lambda j: pl.BlockSpec((C, w), lambda b, c: (b * nc + c, j))
    st_spec = pl.BlockSpec((1, HG_HEADS, HEAD_DIM, HEAD_DIM), lambda b, c: (b, 0, 0, 0))
    pm = _hgrn_prefix_matrix(C)
    o, sfin = pl.pallas_call(
        functools.partial(_hgrn_body, C=C), grid=(batch, nc),
        in_specs=[col(0), col(1), col(2), col(3), _const((1, w)), _const((1, w)), _const((w, w)), _const(pm.shape),
                  st_spec],
        out_specs=[pl.BlockSpec((C, w), lambda b, c: (b * nc + c, 0)), st_spec],
        out_shape=[jax.ShapeDtypeStruct((batch * seq, w), F32),
                   jax.ShapeDtypeStruct((batch, HG_HEADS, HEAD_DIM, HEAD_DIM), F32)],
        scratch_shapes=[pltpu.VMEM((HG_HEADS, HEAD_DIM, HEAD_DIM), F32)],
        compiler_params=_cparams("parallel", "arbitrary"), name="hgrn_scan",
    )(hg, hg, hg, hg, lb.reshape(1, w), jnp.tile(hg_norm, HG_HEADS).reshape(1, w), _block_diag_ones(w), pm,
      jnp.swapaxes(s0, 2, 3))
    return o, jnp.swapaxes(sfin, 2, 3)


CHUNKS_PER_PAGE = PAGE // CMP_STRIDE
PAGE_LANES = CMP_STRIDE * ROWS_NSA
CMP_HID = 2 * NSA_KV * 2 * HEAD_DIM
CMP_PP = 8


def _compress_body(pt_ref, *refs, n_chunks, transposed):
    pages = refs[:CMP_PP]
    wbig_ref, pe_ref, w2_ref, gk_ref, km_ref, bd_ref, rows_ref, grp_ref, xs_ref = refs[CMP_PP:CMP_PP + 9]
    pg_scr = refs[CMP_PP + 9:]
    nb = ROWS_NSA // LANES
    j = pl.program_id(1)
    for i in range(0, CMP_PP, 2):
        for k in range(2):
            for c in range(nb):
                cs = slice(c * LANES, (c + 1) * LANES)
                pg_scr[k * nb + c][...] = pages[i + k][0, cs, :].T if transposed else pages[i + k][0, :, cs]
        r0 = pl.multiple_of((j * CMP_PP + i) * CHUNKS_PER_PAGE, 2 * CHUNKS_PER_PAGE)
        for s in range(CMP_STRIDE):
            rows_s = pl.ds(s, CHUNKS_PER_PAGE, stride=CMP_STRIDE)
            for c in range(nb):
                blk = jnp.concatenate([pg_scr[c][rows_s, :], pg_scr[nb + c][rows_s, :]], axis=0).astype(BF16)
                lo = s * ROWS_NSA + c * LANES
                xs_ref[pl.ds(r0, 2 * CHUNKS_PER_PAGE), lo:lo + LANES] = blk

    @pl.when(j == pl.num_programs(1) - 1)
    def _():
        wbig = wbig_ref[...]
        y = _dot(xs_ref[...], wbig)
        ysh = pltpu.roll(y, n_chunks - 1, 0)
        hid = y + pltpu.roll(ysh, CMP_HID - HEAD_DIM, 1)
        cf = _dot(pe_ref[...].astype(BF16), wbig)
        lane = lax.broadcasted_iota(jnp.int32, (8, CMP_HID), 1)
        cs = jnp.where(((lane >> 6) & 1) == 0, jnp.broadcast_to(cf[0:1], (8, CMP_HID)),
                       jnp.broadcast_to(cf[1:2], (8, CMP_HID)))
        cs = cs + pltpu.roll(cs, CMP_HID - HEAD_DIM, 1)
        hid = hid + cs[0:1]
        act = (hid * jax.nn.sigmoid(hid)).astype(BF16)
        out = _dot(act, w2_ref[...])
        out = _norm_k_rows(out, bd_ref[...], gk_ref[...], km_ref[...])
        rows_ref[0] = out
        kw = NSA_KV * HEAD_DIM
        for g in range(NSA_KV):
            grp_ref[0, g] = jnp.concatenate([out[:, HEAD_DIM * g:HEAD_DIM * (g + 1)],
                                             out[:, kw + HEAD_DIM * g:kw + HEAD_DIM * (g + 1)]], axis=1)


def _compress_weights(cmp_pe, cmp_w1, cmp_w2, k_gain):
    ratio = CMP_BLOCK // CMP_STRIDE
    w1 = cmp_w1.reshape(2, ratio, CMP_STRIDE, HEAD_DIM, HEAD_DIM)
    eye_kv = jnp.eye(2, dtype=F32)
    eye_g = jnp.eye(NSA_KV, dtype=F32)
    wbig = jnp.einsum('krsde,kK,gG->skgdKGre', w1, eye_kv, eye_g).reshape(PAGE_LANES, CMP_HID).astype(BF16)
    pe = cmp_pe.reshape(2, ratio, CMP_STRIDE, HEAD_DIM)
    pe_rows = jnp.broadcast_to(pe.transpose(1, 2, 0, 3)[:, :, :, None, :],
                               (ratio, CMP_STRIDE, 2, NSA_KV, HEAD_DIM)).reshape(ratio, PAGE_LANES)
    pe_rows = jnp.concatenate([pe_rows, jnp.zeros((8 - ratio, PAGE_LANES), F32)], axis=0)
    sel_r0 = jnp.asarray([1.0, 0.0], F32)
    w2big = jnp.einsum('ked,kK,gG,r->kgreKGd', cmp_w2, eye_kv, eye_g, sel_r0).reshape(CMP_HID, ROWS_NSA).astype(BF16)
    kw = NSA_KV * HEAD_DIM
    gk = jnp.concatenate([jnp.tile(k_gain, NSA_KV), jnp.ones((kw,), F32)]).reshape(1, ROWS_NSA)
    km = jnp.concatenate([jnp.ones((kw,), F32), jnp.zeros((kw,), F32)]).reshape(1, ROWS_NSA)
    return wbig, pe_rows, w2big, gk, km


def _compress(pool, table, cw, *, transposed):
    wbig, pe_rows, w2big, gk, km = cw
    bsz, npg = table.shape
    n_chunks = npg * CHUNKS_PER_PAGE
    page = lambda i: pl.BlockSpec((1,) + pool.shape[1:], lambda b, j, pt: (pt[b, j * CMP_PP + i], 0, 0))
    return pl.pallas_call(
        functools.partial(_compress_body, n_chunks=n_chunks, transposed=transposed),
        grid_spec=pltpu.PrefetchScalarGridSpec(
            num_scalar_prefetch=1, grid=(bsz, npg // CMP_PP),
            in_specs=[page(i) for i in range(CMP_PP)] + [
                _const(wbig.shape), _const(pe_rows.shape), _const(w2big.shape), _const((1, ROWS_NSA)),
                _const((1, ROWS_NSA)), _const((ROWS_NSA, ROWS_NSA))],
            out_specs=[pl.BlockSpec((1, n_chunks, ROWS_NSA), lambda b, j, pt: (b, 0, 0)),
                       pl.BlockSpec((1, NSA_KV, n_chunks, LANES), lambda b, j, pt: (b, 0, 0, 0))],
            scratch_shapes=[pltpu.VMEM((n_chunks, PAGE_LANES), BF16)]
            + [pltpu.VMEM((PAGE, LANES), F32)] * (2 * ROWS_NSA // LANES)),
        out_shape=[jax.ShapeDtypeStruct((bsz, n_chunks, ROWS_NSA), F32),
                   jax.ShapeDtypeStruct((bsz, NSA_KV, n_chunks, LANES), F32)],
        compiler_params=_cparams("parallel", "arbitrary"), name="nsa_compress",
    )(table, *([pool] * CMP_PP), wbig, pe_rows, w2big, gk, km, _block_diag_ones(ROWS_NSA))


def _select_blocks(imp, pos, n_slc, k_sel):
    lane = lax.broadcasted_iota(jnp.int32, imp.shape, 1)
    cur = pos >> 6
    forced = (lane == 0) | (lane == cur) | (lane == cur - 1)
    score = jnp.where(forced, FORCE_SCORE, jnp.where(lane <= cur, imp, -1.0))
    rank = jnp.zeros(imp.shape, F32)
    for i in range(n_slc):
        ci = score[:, i:i + 1]
        rank = rank + jnp.where(ci > score, 1.0, jnp.where(ci == score, jnp.where(lane > i, 1.0, 0.0), 0.0))
    return jnp.where(lane < n_slc, jnp.where(rank < k_sel, 1.0, 0.0), 0.0)


NSA_TQ = 256


def _nsa_prompt_body(q_ref, gate_ref, kcv_ref, slc_ref, win_ref, gq_ref, bd_ref, bc_ref, bs_ref, ov_ref, ex_ref,
                     o_ref, sel_scr, m_scr, l_scr, acc_scr, *, seq, n_slc, k_sel):
    tq = NSA_TQ
    rows = NSA_GROUP * tq
    n_tiles = seq // tq
    hd = HEAD_DIM
    kc = kcv_ref[0, 0, :, :hd].astype(BF16)
    vc = kcv_ref[0, 0, :, hd:].astype(BF16)
    n_c = kc.shape[0]
    ii = lax.broadcasted_iota(jnp.int32, (tq, tq), 0)
    jj = lax.broadcasted_iota(jnp.int32, (tq, tq), 1)

    def flash(qs, kv_ref, t, lo, hi, extra_mask):
        m_scr[...] = jnp.full((rows, 1), NEG, F32)
        l_scr[...] = jnp.zeros((rows, 1), F32)
        acc_scr[...] = jnp.zeros((rows, hd), F32)

        def step(jt, carry):
            k0 = pl.multiple_of(jt * tq, tq)
            kv = kv_ref[0, pl.ds(k0, tq), :]
            bias = bs_ref[0, t - jt]
            ok = (bias > NEG_TEST).reshape(NSA_GROUP, tq, tq) & extra_mask(jt)[None]
            s = (_dot_nt(qs, kv[:, :hd].astype(BF16)) + bias).reshape(NSA_GROUP, tq, tq)
            s = jnp.where(ok, s, NEG).reshape(rows, tq)
            m, l, acc = _softmax_piece(s, ok.reshape(rows, tq), m_scr[...], l_scr[...], acc_scr[...],
                                       kv[:, hd:].astype(BF16))
            m_scr[...] = m
            l_scr[...] = l
            acc_scr[...] = acc
            return carry

        lax.fori_loop(lo, hi + 1, step, 0)
        return acc_scr[...] / jnp.maximum(l_scr[...], 1e-30)

    def q_tile(t, carry):
        r0 = pl.multiple_of(t * tq, tq)
        qn = _seg_rms(q_ref[pl.ds(r0, tq), :], bd_ref[...], gq_ref[...]) * (hd ** -0.5)
        qs = jnp.concatenate([qn[:, hd * h:hd * (h + 1)] for h in range(NSA_GROUP)], axis=0).astype(BF16)
        bias_c = bc_ref[0, :, pl.ds(r0, tq), :].reshape(rows, n_c)
        ok_c = bias_c > NEG_TEST
        s_c = _dot_nt(qs, kc) + bias_c
        m_c = jnp.max(s_c, axis=-1, keepdims=True)
        e_c = jnp.where(ok_c, jnp.exp(s_c - m_c), 0.0)
        p_c = (e_c / jnp.maximum(jnp.sum(e_c, axis=-1, keepdims=True), 1e-30)).astype(BF16)
        o_c = _dot(p_c, vc)
        d = _dot(p_c, ov_ref[...])
        imp = d[0:tq] + d[tq:2 * tq] + d[2 * tq:3 * tq] + d[3 * tq:4 * tq]
        pos = r0 + lax.broadcasted_iota(jnp.int32, (tq, 1), 0)
        sel = _select_blocks(imp, pos, n_slc, k_sel).astype(BF16)
        for jt in range(n_tiles):
            sel_scr[jt] = _dot(sel, ex_ref[:, jt * tq:(jt + 1) * tq])
        o_s = flash(qs, slc_ref, t, 0, t, lambda jt: sel_scr[jt] > 0.5)
        o_w = flash(qs, win_ref, t, jnp.maximum(t - NSA_WINDOW // tq, 0), t,
                    lambda jt: (t - jt) * tq + ii - jj <= NSA_WINDOW)
        gates = gate_ref[0, pl.ds(r0, tq), :]
        outs = []
        for h in range(NSA_GROUP):
            hr = slice(h * tq, (h + 1) * tq)
            outs.append(gates[:, 3 * h:3 * h + 1] * o_c[hr] + gates[:, 3 * h + 1:3 * h + 2] * o_s[hr]
                        + gates[:, 3 * h + 2:3 * h + 3] * o_w[hr])
        o_ref[pl.ds(r0, tq), :] = jnp.concatenate(outs, axis=1)
        return carry

    lax.fori_loop(0, n_tiles, q_tile, 0)


def _nsa_prompt_tables(rel_bias, seq, n_c):
    tq = NSA_TQ
    n_tiles = seq // tq
    n_cmp = seq // CMP_STRIDE - CMP_BLOCK // CMP_STRIDE + 1
    n_slc = -(-seq // SEL_BLOCK)
    n_a = seq // CMP_STRIDE
    t_ext = jnp.arange(n_a + n_c - 1) - (n_c - 1)
    d_ext = CMP_STRIDE * t_ext[None, :] + jnp.arange(CMP_STRIDE)[:, None] - (CMP_BLOCK - 1)
    bc = _toeplitz(_bias_tile(rel_bias, d_ext, d_ext >= 0), n_a, n_c)
    bc = jnp.where(jnp.arange(n_c) < n_cmp, bc.transpose(0, 2, 1, 3).reshape(NSA_HEADS, seq, n_c), NEG)
    bc = bc.reshape(NSA_KV, NSA_GROUP, seq, n_c)
    d_ext = jnp.arange(n_tiles)[:, None] * tq + jnp.arange(2 * tq - 1)[None, :] - (tq - 1)
    bs = _toeplitz(_bias_tile(rel_bias, d_ext, d_ext >= 0), tq, tq)
    bs = bs.reshape(NSA_KV, NSA_GROUP, n_tiles, tq, tq).transpose(0, 2, 1, 3, 4).reshape(
        NSA_KV, n_tiles, NSA_GROUP * tq, tq)
    c_lo = np.arange(n_c) * CMP_STRIDE
    s_lo = np.arange(LANES) * SEL_BLOCK
    ov = ((c_lo[:, None] <= s_lo[None, :] + SEL_BLOCK - 1) & (c_lo[:, None] + CMP_BLOCK - 1 >= s_lo[None, :])
          & (np.arange(n_c) < n_cmp)[:, None] & (np.arange(LANES) < n_slc)[None, :])
    ex = np.arange(LANES)[:, None] == (np.arange(seq)[None, :] // SEL_BLOCK)
    return bc, bs, jnp.asarray(ov.astype(np.float32), BF16), jnp.asarray(ex.astype(np.float32), BF16), n_slc


def _nsa_prompt(q, gates, kcv, slcg, wing, q_gain, tables, *, batch, seq):
    bc, bs, ov, ex, n_slc = tables
    n_c = kcv.shape[2]
    n_tiles = seq // NSA_TQ
    gw = NSA_GROUP * HEAD_DIM
    rows = NSA_GROUP * NSA_TQ
    per_g = lambda shape: pl.BlockSpec((1,) + shape, lambda g, b: (g,) + (0,) * len(shape),
                                       pipeline_mode=pl.Buffered(1))
    return pl.pallas_call(
        functools.partial(_nsa_prompt_body, seq=seq, n_slc=n_slc, k_sel=min(SEL_TOPK, n_slc)),
        grid=(NSA_KV, batch),
        in_specs=[pl.BlockSpec((seq, gw), lambda g, b: (b, g)),
                  pl.BlockSpec((1, seq, LANES), lambda g, b: (g, b, 0)),
                  pl.BlockSpec((1, 1, n_c, LANES), lambda g, b: (b, g, 0, 0)),
                  pl.BlockSpec((1, seq, LANES), lambda g, b: (g, b, 0)),
                  pl.BlockSpec((1, seq, LANES), lambda g, b: (g, b, 0)),
                  _const((1, gw)), _const((gw, gw)),
                  per_g((NSA_GROUP, seq, n_c)), per_g((n_tiles, rows, NSA_TQ)),
                  _const(ov.shape), _const(ex.shape)],
        out_specs=pl.BlockSpec((seq, gw), lambda g, b: (b, g)),
        out_shape=jax.ShapeDtypeStruct((batch * seq, NSA_KV * gw), F32),
        scratch_shapes=[pltpu.VMEM((n_tiles, NSA_TQ, NSA_TQ), F32), pltpu.VMEM((rows, 1), F32),
                        pltpu.VMEM((rows, 1), F32), pltpu.VMEM((rows, HEAD_DIM), F32)],
        compiler_params=_cparams("arbitrary", "parallel"), name="nsa_prompt",
    )(q, gates, kcv, slcg, wing, jnp.tile(q_gain, NSA_GROUP).reshape(1, gw), _block_diag_ones(gw), bc, bs, ov, ex)


NSA_QC = LANES


def _select_blocks_t(imp, pos, n_slc, k_sel):
    blk = lax.broadcasted_iota(jnp.int32, imp.shape, 0)
    cur = pos >> 6
    forced = (blk == 0) | (blk == cur) | (blk == cur - 1)
    score = jnp.where(forced, FORCE_SCORE, jnp.where(blk <= cur, imp, -1.0))
    rank = jnp.zeros(imp.shape, F32)
    for i in range(n_slc):
        ci = score[i:i + 1, :]
        rank = rank + jnp.where(ci > score, 1.0, jnp.where(ci == score, jnp.where(blk > i, 1.0, 0.0), 0.0))
    return jnp.where(rank < k_sel, 1.0, 0.0)


def _nsa_prompt_t_body(q_ref, gate_ref, kcv_ref, slc_ref, win_ref, gq_ref, bd_ref, eye_ref, fs_ref, fc_ref,
                       ov_ref, ex_ref, o_ref, bc_ref, bs_ref, bw_ref, qt_scr, ks_scr, kw_scr, vs_scr, vw_scr, sel_scr, m_scr, l_scr, acc_scr,
                       *, seq, n_slc, k_sel):
    tq = NSA_TQ
    qc = NSA_QC
    hd = HEAD_DIM
    n_tiles = seq // tq
    width = NSA_GROUP * tq
    n_chunks = width // qc
    eye = eye_ref[...]
    n_c = kcv_ref.shape[2]
    far = NSA_WINDOW // tq

    @pl.when(pl.program_id(1) == 0)
    def _():
        key_i = lax.broadcasted_iota(jnp.int32, (tq, tq), 0)
        qry_i = lax.broadcasted_iota(jnp.int32, (tq, tq), 1)
        blk_i = lax.broadcasted_iota(jnp.int32, (n_c, tq), 0)
        n_cmp = seq // CMP_STRIDE - CMP_BLOCK // CMP_STRIDE + 1
        span = tq + CMP_STRIDE * n_c
        for h in range(NSA_GROUP):
            hs = slice(h * tq, (h + 1) * tq)
            for d in range(n_tiles):
                u = jnp.broadcast_to(fs_ref[0, h:h + 1, d * tq:(d + 2) * tq], (tq, 2 * tq))
                tile = pltpu.roll(u, tq, 1, stride=1, stride_axis=0)[:, :tq]
                bs_ref[d, :, hs] = tile
                if d == far:
                    bw_ref[0, :, hs] = jnp.where(qry_i <= key_i, tile, NEG)
            for t in range(n_tiles):
                u = jnp.broadcast_to(fc_ref[0, h:h + 1, t * tq:t * tq + span], (n_c, span))
                tile = pltpu.roll(u, tq, 1, stride=CMP_STRIDE, stride_axis=0)[:, :tq]
                bc_ref[t, :, hs] = jnp.where(blk_i < n_cmp, tile, NEG)

    qn = (_seg_rms(q_ref[...], bd_ref[...], gq_ref[...]) * (hd ** -0.5)).astype(BF16)
    for t in range(n_tiles):
        rs = slice(t * tq, (t + 1) * tq)
        for h in range(NSA_GROUP):
            qt_scr[t, :, h * tq:(h + 1) * tq] = _dot_nt(eye, qn[rs, h * hd:(h + 1) * hd]).astype(BF16)
        for kv_ref, k_scr, v_scr in ((slc_ref, ks_scr, vs_scr), (win_ref, kw_scr, vw_scr)):
            kv = kv_ref[0, rs, :].astype(BF16)
            k_scr[t] = kv[:, :hd]
            v_scr[t] = _dot_nt(eye, kv[:, hd:]).astype(BF16)
    kc = kcv_ref[0, 0, :, :hd].astype(BF16)
    vct = _dot_nt(eye, kcv_ref[0, 0, :, hd:].astype(BF16)).astype(BF16)
    def flash_init():
        m_scr[...] = jnp.full((1, width), NEG, F32)
        l_scr[...] = jnp.zeros((1, width), F32)
        acc_scr[...] = jnp.zeros((hd, width), F32)

    def flash_step(k_scr, v_scr, t, jt, bias_tile, sel_add):
        s_all = _dot(k_scr[jt], qt_scr[t])
        m_all = m_scr[...]
        l_all = l_scr[...]
        alphas, ls, ms, ps = [], [], [], []
        for c in range(n_chunks):
            cs = slice(c * qc, (c + 1) * qc)
            qoff = (c * qc) % tq
            s = s_all[:, cs] + bias_tile(cs)
            if sel_add is not None:
                s = s + sel_add[:, qoff:qoff + qc]
            m_new = jnp.maximum(m_all[:, cs], jnp.max(s, axis=0, keepdims=True))
            alpha = jnp.exp(m_all[:, cs] - m_new)
            p = jnp.exp(s - m_new)
            ls.append(alpha * l_all[:, cs] + jnp.sum(p, axis=0, keepdims=True))
            ps.append(p.astype(BF16))
            alphas.append(alpha)
            ms.append(m_new)
        cat = lambda xs: jnp.concatenate(xs, axis=1)
        acc_scr[...] = cat(alphas) * acc_scr[...] + _dot(v_scr[jt], cat(ps))
        m_scr[...] = cat(ms)
        l_scr[...] = cat(ls)

    def flash_result():
        return acc_scr[...] / jnp.maximum(l_scr[...], 1e-30)

    def q_tile(t, carry):
        bias_c = bc_ref[t]
        s_c = _dot(kc, qt_scr[t]) + bias_c
        m_c = jnp.max(s_c, axis=0, keepdims=True)
        e_c = jnp.where(bias_c > NEG_TEST, jnp.exp(s_c - m_c), 0.0)
        p_c = (e_c / jnp.maximum(jnp.sum(e_c, axis=0, keepdims=True), 1e-30)).astype(BF16)
        o_c = _dot(vct, p_c)
        imp = sum(_dot(ov_ref[...], p_c[:, h * tq:(h + 1) * tq]) for h in range(NSA_GROUP))
        pos = t * tq + lax.broadcasted_iota(jnp.int32, (1, tq), 1)
        sel = _select_blocks_t(imp[:n_slc], pos, n_slc, k_sel)
        sel_scr[...] = jnp.zeros(sel_scr.shape, BF16)
        sel_scr[0:n_slc, :] = ((1.0 - sel) * NEG).astype(BF16)
        flash_init()

        def slc_step(jt, carry):
            flash_step(ks_scr, vs_scr, t, jt, lambda cs: bs_ref[t - jt, :, cs], _dot(ex_ref[jt], sel_scr[...]))
            return carry

        lax.fori_loop(0, t + 1, slc_step, 0)
        o_s = flash_result()
        flash_init()
        far = NSA_WINDOW // tq
        for delta in range(far, -1, -1):
            tile_of = (lambda cs: bw_ref[0, :, cs]) if delta == far else (lambda cs, d=delta: bs_ref[d, :, cs])
            pl.when(t >= delta)(functools.partial(flash_step, kw_scr, vw_scr, t, t - delta, tile_of, None))
        o_w = flash_result()
        gates = gate_ref[0, t]
        for h in range(NSA_GROUP):
            hs = slice(h * tq, (h + 1) * tq)
            o_ref[t, h * hd:(h + 1) * hd, :] = (gates[3 * h:3 * h + 1] * o_c[:, hs]
                                                + gates[3 * h + 1:3 * h + 2] * o_s[:, hs]
                                                + gates[3 * h + 2:3 * h + 3] * o_w[:, hs])
        return carry

    lax.fori_loop(0, n_tiles, q_tile, 0)


def _nsa_prompt_t_tables(rel_bias, seq, n_c):
    tq = NSA_TQ
    n_tiles = seq // tq
    n_slc = -(-seq // SEL_BLOCK)
    ds = jnp.arange((n_tiles + 1) * tq) - tq
    fs = _bias_tile(rel_bias, ds, ds >= 0).reshape(NSA_KV, NSA_GROUP, -1)
    dc = jnp.arange(n_tiles * tq + CMP_STRIDE * n_c) - (CMP_STRIDE * n_c + CMP_BLOCK - 1)
    fc = _bias_tile(rel_bias, dc, dc >= 0).reshape(NSA_KV, NSA_GROUP, -1)
    n_cmp = seq // CMP_STRIDE - CMP_BLOCK // CMP_STRIDE + 1
    c_lo = np.arange(n_c) * CMP_STRIDE
    s_lo = np.arange(LANES) * SEL_BLOCK
    ov = ((c_lo[None, :] <= s_lo[:, None] + SEL_BLOCK - 1) & (c_lo[None, :] + CMP_BLOCK - 1 >= s_lo[:, None])
          & (np.arange(n_c) < n_cmp)[None, :] & (np.arange(LANES) < n_slc)[:, None])
    ex = (np.arange(seq)[:, None] // SEL_BLOCK == np.arange(LANES)[None, :]).reshape(n_tiles, tq, LANES)
    as_bf = lambda a: jnp.asarray(a.astype(np.float32), BF16)
    return fs, fc, as_bf(ov), as_bf(ex), n_slc


def _nsa_prompt_t(q, gates_t, kcv, slcg, wing, q_gain, tables, *, batch, seq):
    fs, fc, ov, ex, n_slc = tables
    n_c = kcv.shape[2]
    tq = NSA_TQ
    n_tiles = seq // tq
    assert n_tiles > NSA_WINDOW // tq
    gw = NSA_GROUP * HEAD_DIM
    width = NSA_GROUP * tq
    per_g = lambda shape: pl.BlockSpec((1,) + shape, lambda g, b: (g,) + (0,) * len(shape),
                                       pipeline_mode=pl.Buffered(1))
    eye = jnp.eye(HEAD_DIM, dtype=BF16)
    return pl.pallas_call(
        functools.partial(_nsa_prompt_t_body, seq=seq, n_slc=n_slc, k_sel=min(SEL_TOPK, n_slc)),
        grid=(NSA_KV, batch),
        in_specs=[pl.BlockSpec((seq, gw), lambda g, b: (b, g)),
                  pl.BlockSpec((1, n_tiles, 16, tq), lambda g, b: (g, b, 0, 0)),
                  pl.BlockSpec((1, 1, n_c, LANES), lambda g, b: (b, g, 0, 0)),
                  pl.BlockSpec((1, seq, LANES), lambda g, b: (g, b, 0)),
                  pl.BlockSpec((1, seq, LANES), lambda g, b: (g, b, 0)),
                  _const((1, gw)), _const((gw, gw)), _const(eye.shape),
                  per_g(fs.shape[1:]), per_g(fc.shape[1:]), _const(ov.shape), _const(ex.shape)],
        out_specs=pl.BlockSpec((n_tiles, gw, tq), lambda g, b: (b, g, 0)),
        out_shape=jax.ShapeDtypeStruct((batch * n_tiles, NSA_KV * gw, tq), F32),
        scratch_shapes=[pltpu.VMEM((n_tiles, n_c, width), F32), pltpu.VMEM((n_tiles, tq, width), F32),
                        pltpu.VMEM((1, tq, width), F32), pltpu.VMEM((n_tiles, HEAD_DIM, width), BF16),
                        pltpu.VMEM((n_tiles, tq, HEAD_DIM), BF16), pltpu.VMEM((n_tiles, tq, HEAD_DIM), BF16),
                        pltpu.VMEM((n_tiles, HEAD_DIM, tq), BF16), pltpu.VMEM((n_tiles, HEAD_DIM, tq), BF16),
                        pltpu.VMEM((LANES, tq), BF16), pltpu.VMEM((1, width), F32), pltpu.VMEM((1, width), F32),
                        pltpu.VMEM((HEAD_DIM, width), F32)],
        compiler_params=_cparams("arbitrary", "arbitrary"), name="nsa_prompt",
    )(q, gates_t, kcv, slcg, wing, jnp.tile(q_gain, NSA_GROUP).reshape(1, gw), _block_diag_ones(gw), eye,
      fs, fc, ov, ex)


SMP_PP = 8
NSA_KW = NSA_KV * HEAD_DIM
SEL_LANES = 2 * LANES


def _nsa_sample_body(pt_ref, q_ref, gate_ref, ckv_ref, *refs, n_slc, k_sel, past, tq):
    pages = refs[:SMP_PP]
    (slcn_ref, wst_ref, winn_ref, gq_ref, bd_ref, bc_ref, bs_ref, bsn_ref, bw_ref, bwn_ref, ov_ref, ex_ref, exn_ref,
     o_ref, nwin_ref, qbd_scr, sel_scr, kbuf_scr, m_scr, l_scr, acc_scr, oc_scr, ow_scr, pad_scr) = refs[SMP_PP:]
    j = pl.program_id(1)
    hd = HEAD_DIM
    kw = NSA_KW
    rows = NSA_HEADS * tq
    init = (jnp.full((rows, 1), NEG, F32), jnp.zeros((rows, 1), F32), jnp.zeros((rows, kw), F32))

    @pl.when(j == 0)
    def _():
        qn = _seg_rms(q_ref[...], bd_ref[...], gq_ref[...]) * (hd ** -0.5)
        zero = jnp.zeros((tq, hd), F32)
        blocks = []
        for h in range(NSA_HEADS):
            qh = qn[:, hd * h:hd * (h + 1)]
            blocks.append(jnp.concatenate([qh if g == h // NSA_GROUP else zero for g in range(NSA_KV)], axis=1))
        qbd = jnp.concatenate(blocks, axis=0).astype(BF16)
        qbd_scr[...] = qbd
        ckv = ckv_ref[0]
        bias_c = bc_ref[...]
        ok_c = bias_c > NEG_TEST
        s_c = _dot_nt(qbd, ckv[:, :kw].astype(BF16)) + bias_c
        m_c = jnp.max(s_c, axis=-1, keepdims=True)
        e_c = jnp.where(ok_c, jnp.exp(s_c - m_c), 0.0)
        p_c = (e_c / jnp.maximum(jnp.sum(e_c, axis=-1, keepdims=True), 1e-30)).astype(BF16)
        oc_scr[...] = _dot(p_c, ckv[:, kw:].astype(BF16))
        d = _dot(p_c, ov_ref[...])
        gr = NSA_GROUP * tq
        imp = jnp.concatenate(
            [sum(d[g * gr + h * tq:g * gr + (h + 1) * tq] for h in range(NSA_GROUP)) for g in range(NSA_KV)], axis=0)
        pos = past + (lax.broadcasted_iota(jnp.int32, (NSA_KV * tq, 1), 0) & (tq - 1))
        sel = _select_blocks(imp, pos, n_slc, k_sel)
        sel = jnp.concatenate([sel[g * tq:(g + 1) * tq] for g in range(NSA_KV) for _ in range(NSA_GROUP)],
                              axis=0).astype(BF16)
        sel_scr[...] = sel
        pad_scr[...] = jnp.zeros(pad_scr.shape, F32)
        pad_scr[0:tq, :] = slcn_ref[...]
        kn = pad_scr[...].astype(BF16)
        bias_n = bsn_ref[...]
        ok_n = (bias_n > NEG_TEST) & (_dot(sel, exn_ref[...]) > 0.5)
        s_n = jnp.where(ok_n, _dot_nt(qbd, kn[:, :kw]) + bias_n, NEG)
        m, l, acc = _softmax_piece(s_n, ok_n, *init, kn[:, kw:])
        m_scr[...] = m
        l_scr[...] = l
        acc_scr[...] = acc
        pad_scr[0:tq, :] = winn_ref[...]
        nwin_ref[0] = _shifted_buffer(wst_ref[0], pad_scr[...], tq)
        wst = wst_ref[0].astype(BF16)
        wn = pad_scr[...].astype(BF16)
        bias_w = bw_ref[...]
        mw = _softmax_piece(_dot(qbd, wst[:kw]) + bias_w, bias_w > NEG_TEST, *init, wst[kw:], v_t=True)
        bias_wn = bwn_ref[...]
        mw = _softmax_piece(_dot_nt(qbd, wn[:, :kw]) + bias_wn, bias_wn > NEG_TEST, *mw, wn[:, kw:])
        ow_scr[...] = mw[2] / jnp.maximum(mw[1], 1e-30)

    for i in range(SMP_PP):
        kbuf_scr[:, i * PAGE:(i + 1) * PAGE] = pages[i][0].astype(BF16)
    kb = kbuf_scr[...]
    bias = bs_ref[j]
    ok = (bias > NEG_TEST) & (_dot(sel_scr[...], ex_ref[j]) > 0.5)
    s = jnp.where(ok, _dot(qbd_scr[...], kb[:kw]) + bias, NEG)
    m, l, acc = _softmax_piece(s, ok, m_scr[...], l_scr[...], acc_scr[...], kb[kw:], v_t=True)
    m_scr[...] = m
    l_scr[...] = l
    acc_scr[...] = acc

    @pl.when(j == pl.num_programs(1) - 1)
    def _():
        o_s = acc_scr[...] / jnp.maximum(l_scr[...], 1e-30)
        o_c = oc_scr[...]
        o_w = ow_scr[...]
        outs = []
        for h in range(NSA_HEADS):
            g, hg = divmod(h, NSA_GROUP)
            gt = gate_ref[g]
            rs = slice(h * tq, (h + 1) * tq)
            ls = slice(hd * g, hd * (g + 1))
            outs.append(gt[:, 3 * hg:3 * hg + 1] * o_c[rs, ls] + gt[:, 3 * hg + 1:3 * hg + 2] * o_s[rs, ls]
                        + gt[:, 3 * hg + 2:3 * hg + 3] * o_w[rs, ls])
        o_ref[...] = jnp.concatenate(outs, axis=1)


def _nsa_sample_tables(rel_bias, past, tq, n_c, w_state):
    rows = NSA_HEADS * tq
    total = past + tq
    n_cmp = total // CMP_STRIDE - CMP_BLOCK // CMP_STRIDE + 1
    n_slc = -(-total // SEL_BLOCK)
    pos = past + jnp.arange(tq)
    c_hi = jnp.arange(n_c) * CMP_STRIDE + CMP_BLOCK - 1
    dist_c = pos[:, None] - c_hi[None, :]
    bc = _bias_tile(rel_bias, dist_c, (dist_c >= 0) & (jnp.arange(n_c) < n_cmp)[None, :]).reshape(rows, n_c)
    n_steps = past // (SMP_PP * PAGE)
    d_ext = jnp.arange(tq + past - 1) + 1
    bs = _toeplitz(_bias_tile(rel_bias, d_ext, d_ext >= 0), tq, past)
    bs = bs.reshape(rows, n_steps, SMP_PP * PAGE).transpose(1, 0, 2)
    jn = jnp.arange(PAGE)
    dist_n = jnp.arange(tq)[:, None] - jn[None, :]
    new_ok = (dist_n >= 0) & (jn < tq)[None, :]
    bsn = _bias_tile(rel_bias, dist_n, new_ok).reshape(rows, PAGE)
    dist_w = jnp.arange(tq)[:, None] + w_state - jnp.arange(w_state)[None, :]
    bw = _bias_tile(rel_bias, dist_w, (dist_w >= 0) & (dist_w <= NSA_WINDOW)).reshape(rows, w_state)
    bwn = _bias_tile(rel_bias, dist_n, new_ok & (dist_n <= NSA_WINDOW)).reshape(rows, PAGE)
    c_lo = np.arange(n_c) * CMP_STRIDE
    s_lo = np.arange(SEL_LANES) * SEL_BLOCK
    ov = ((c_lo[:, None] <= s_lo[None, :] + SEL_BLOCK - 1) & (c_lo[:, None] + CMP_BLOCK - 1 >= s_lo[None, :])
          & (np.arange(n_c) < n_cmp)[:, None] & (np.arange(SEL_LANES) < n_slc)[None, :])
    blk = np.arange(SEL_LANES)
    ex = (blk[:, None] == (np.arange(past)[None, :] // SEL_BLOCK)).reshape(SEL_LANES, n_steps, SMP_PP * PAGE)
    ex = ex.transpose(1, 0, 2)
    exn = (blk[:, None] == ((past + np.arange(PAGE))[None, :] // SEL_BLOCK)) & (np.arange(PAGE) < tq)[None, :]
    as_bf = lambda a: jnp.asarray(a.astype(np.float32), BF16)
    return bc, bs, bsn, bw, bwn, as_bf(ov), as_bf(ex), as_bf(exn), n_slc


def _nsa_sample(q, gates, crows, slc_pool_t, table, slc_new, win_state_t, win_new, q_gain, tables, *, tq):
    bc, bs, bsn, bw, bwn, ov, ex, exn, n_slc = tables
    bsz, npg = table.shape
    past = npg * PAGE
    n_c = crows.shape[1]
    w_state = win_state_t.shape[2]
    rows = NSA_HEADS * tq
    qw = NSA_HEADS * HEAD_DIM
    page = lambda i: pl.BlockSpec((1, ROWS_NSA, PAGE), lambda b, j, pt: (pt[b, j * SMP_PP + i], 0, 0))
    per_b = lambda shape: pl.BlockSpec((1,) + shape, lambda b, j, pt: (b,) + (0,) * len(shape))
    new_rows = pl.BlockSpec((tq, ROWS_NSA), lambda b, j, pt: (b, 0))
    return pl.pallas_call(
        functools.partial(_nsa_sample_body, n_slc=n_slc, k_sel=min(SEL_TOPK, n_slc), past=past, tq=tq),
        grid_spec=pltpu.PrefetchScalarGridSpec(
            num_scalar_prefetch=1, grid=(bsz, npg // SMP_PP),
            in_specs=[pl.BlockSpec((tq, qw), lambda b, j, pt: (b, 0)),
                      pl.BlockSpec((NSA_KV, tq, LANES), lambda b, j, pt: (0, b, 0)),
                      per_b((n_c, ROWS_NSA))] + [page(i) for i in range(SMP_PP)] + [
                      new_rows, per_b((ROWS_NSA, w_state)), new_rows,
                      _const((1, qw)), _const((qw, qw)), _const(bc.shape), _const(bs.shape), _const(bsn.shape),
                      _const(bw.shape), _const(bwn.shape), _const(ov.shape), _const(ex.shape), _const(exn.shape)],
            out_specs=[pl.BlockSpec((tq, qw), lambda b, j, pt: (b, 0)), per_b((ROWS_NSA, w_state))],
            scratch_shapes=[pltpu.VMEM((rows, NSA_KW), BF16), pltpu.VMEM((rows, SEL_LANES), BF16),
                            pltpu.VMEM((ROWS_NSA, SMP_PP * PAGE), BF16), pltpu.VMEM((rows, 1), F32),
                            pltpu.VMEM((rows, 1), F32), pltpu.VMEM((rows, NSA_KW), F32),
                            pltpu.VMEM((rows, NSA_KW), F32), pltpu.VMEM((rows, NSA_KW), F32),
                            pltpu.VMEM((PAGE, ROWS_NSA), F32)]),
        out_shape=[jax.ShapeDtypeStruct((bsz * tq, qw), F32), jax.ShapeDtypeStruct(win_state_t.shape, F32)],
        compiler_params=_cparams("parallel", "arbitrary"), name="nsa_sample",
    )(table, q, gates, crows, *([slc_pool_t] * SMP_PP), slc_new, win_state_t, win_new,
      jnp.tile(q_gain, NSA_HEADS).reshape(1, qw), _block_diag_ones(qw), bc, bs, bsn, bw, bwn, ov, ex, exn)


DIL_TQ = 128


def _head_of_lane(shape):
    return lax.broadcasted_iota(jnp.int32, shape, 1) >> 6


def _stack_heads(qt, lane_head):
    return jnp.concatenate([jnp.where(lane_head == h, qt, 0.0) for h in range(DIL_HEADS)], axis=0).astype(BF16)


def _unstack_heads(r, lane_head, t):
    return sum(jnp.where(lane_head == h, r[h * t:(h + 1) * t], 0.0) for h in range(DIL_HEADS))


def _dil_prompt_body(q_ref, kv_ref, bias_ref, o_ref, lse_ref, *, n_sub, dil):
    tq = DIL_TQ
    gw = ODD_GW
    lane_head = _head_of_lane((tq, gw))
    for r in range(dil):
        ql = slice(gw * r, gw * (r + 1))
        kl = slice(2 * gw * r, 2 * gw * r + gw)
        vl = slice(2 * gw * r + gw, 2 * gw * (r + 1))

        def tile(t, carry, ql=ql, kl=kl, vl=vl):
            start = pl.multiple_of(t * tq, tq)
            prev = pl.multiple_of(jnp.maximum(t - 1, 0) * tq, tq)
            cur_rows = pl.ds(start, tq)
            prev_rows = pl.ds(prev, tq)
            kk = jnp.concatenate([kv_ref[prev_rows, kl], kv_ref[cur_rows, kl]], axis=0).astype(BF16)
            vv = jnp.concatenate([kv_ref[prev_rows, vl], kv_ref[cur_rows, vl]], axis=0).astype(BF16)
            bias = bias_ref[jnp.minimum(t, 1)]
            s = _dot_nt(_stack_heads(q_ref[cur_rows, ql], lane_head), kk) + bias
            m = jnp.max(s, axis=-1, keepdims=True)
            e = jnp.where(bias > NEG_TEST, jnp.exp(s - m), 0.0)
            l = jnp.maximum(jnp.sum(e, axis=-1, keepdims=True), 1e-30)
            res = _dot((e / l).astype(BF16), vv)
            lse = jnp.broadcast_to(m + jnp.log(l), (DIL_HEADS * tq, gw))
            o_ref[cur_rows, ql] = _unstack_heads(res, lane_head, tq)
            lse_ref[cur_rows, ql] = _unstack_heads(lse, lane_head, tq)
            return carry

        lax.fori_loop(0, n_sub // tq, tile, 0)


def _merge_groups(outs, lses):
    mx = functools.reduce(jnp.maximum, lses)
    ws = [jnp.exp(l - mx) for l in lses]
    return sum(w * o for w, o in zip(ws, outs)) / sum(ws)


def _dil_prompt_tables(rel_bias):
    tq = DIL_TQ
    m_ext = jnp.arange(3 * tq - 1) - (tq - 1)
    cur_tile = jnp.arange(2 * tq) >= tq
    tiles = []
    for gi, (win, dil) in enumerate(DIL_PATTERN):
        assert win // dil == tq
        t = _toeplitz(_bias_tile(rel_bias, m_ext * dil, (m_ext >= 0) & (m_ext <= tq)), tq, 2 * tq)
        t = t[gi * DIL_HEADS:(gi + 1) * DIL_HEADS].reshape(DIL_HEADS * tq, 2 * tq)
        tiles.append(jnp.stack([jnp.where(cur_tile, t, NEG), t]))
    return tiles


def _dil_prompt(q3, kv3, tables, *, batch, seq):
    gw = ODD_GW
    n = batch * seq
    outs, lses = [], []
    for gi, (_, dil) in enumerate(DIL_PATTERN):
        n_sub = seq // dil
        assert n_sub % DIL_TQ == 0
        qv = q3[gi].reshape(n // dil, dil * gw)
        kvv = kv3[gi].reshape(n // dil, dil * 2 * gw)
        o_spec = pl.BlockSpec((n_sub, dil * gw), lambda b: (b, 0))
        o, lse = pl.pallas_call(
            functools.partial(_dil_prompt_body, n_sub=n_sub, dil=dil), grid=(batch,),
            in_specs=[o_spec, pl.BlockSpec((n_sub, dil * 2 * gw), lambda b: (b, 0)), _const(tables[gi].shape)],
            out_specs=[o_spec, o_spec],
            out_shape=[jax.ShapeDtypeStruct(qv.shape, F32)] * 2,
            compiler_params=_cparams("parallel"), name=f"dilated_prompt_g{gi}",
        )(qv, kvv, tables[gi])
        outs.append(o.reshape(n, gw))
        lses.append(lse.reshape(n, gw))
    return outs, lses


def _shifted_buffer(st, new_rows_padded, tq):
    ln = st.shape[1]
    shifted = pltpu.roll(st, ln - tq, 1)
    new_t = pltpu.roll(new_rows_padded.T, LANES - tq, 1)
    lane = lax.broadcasted_iota(jnp.int32, (st.shape[0], LANES), 1)
    tail = jnp.where(lane >= LANES - tq, new_t, shifted[:, ln - LANES:])
    return tail if ln == LANES else jnp.concatenate([shifted[:, :ln - LANES], tail], axis=1)


def _dil_sample_body(q_ref, kvn_ref, st0_ref, st1_ref, st2_ref, b0_ref, b1_ref, b2_ref, bn_ref, o_ref, lse_ref,
                     n0_ref, n1_ref, n2_ref, pad_scr, *, tq):
    gw = ODD_GW
    lane_head = _head_of_lane((tq, gw))
    rows = DIL_HEADS * tq
    pad_scr[...] = jnp.zeros(pad_scr.shape, F32)
    for gi, (st_ref, b_ref, new_ref) in enumerate(((st0_ref, b0_ref, n0_ref), (st1_ref, b1_ref, n1_ref),
                                                    (st2_ref, b2_ref, n2_ref))):
        qs = _stack_heads(q_ref[gi], lane_head)
        pad_scr[0:tq, :] = kvn_ref[gi]
        new_ref[0] = _shifted_buffer(st_ref[0], pad_scr[...], tq)
        st = st_ref[0].astype(BF16)
        kn = pad_scr[...].astype(BF16)
        bias = b_ref[...]
        acc = (jnp.full((rows, 1), NEG, F32), jnp.zeros((rows, 1), F32), jnp.zeros((rows, gw), F32))
        acc = _softmax_piece(_dot(qs, st[:gw]) + bias, bias > NEG_TEST, *acc, st[gw:], v_t=True)
        bias_n = bn_ref[gi]
        m, l, a = _softmax_piece(_dot_nt(qs, kn[:, :gw]) + bias_n, bias_n > NEG_TEST, *acc, kn[:, gw:])
        l = jnp.maximum(l, 1e-30)
        o_ref[gi] = _unstack_heads(a / l, lane_head, tq)
        lse_ref[gi] = _unstack_heads(jnp.broadcast_to(m + jnp.log(l), (rows, gw)), lane_head, tq)


def _dil_sample_tables(rel_bias, tq, state_lens):
    rows = DIL_HEADS * tq
    i = jnp.arange(tq)[:, None]
    per_state, per_new = [], []
    jn = jnp.arange(PAGE)[None, :]
    for gi, ((win, dil), ln) in enumerate(zip(DIL_PATTERN, state_lens)):
        hs = slice(gi * DIL_HEADS, (gi + 1) * DIL_HEADS)
        d = i + ln - jnp.arange(ln)[None, :]
        per_state.append(_bias_tile(rel_bias, d, (d % dil == 0) & (d <= win))[hs].reshape(rows, ln))
        dn = i - jn
        per_new.append(_bias_tile(rel_bias, dn, (dn >= 0) & (dn % dil == 0) & (dn <= win) & (jn < tq))[hs]
                       .reshape(rows, PAGE))
    return per_state, jnp.stack(per_new)


def _dil_sample(q3, kv3, states, tables, *, tq):
    per_state, bias_new = tables
    bsz = states[0].shape[0]
    gw = ODD_GW
    st_spec = lambda s: pl.BlockSpec((1,) + s.shape[1:], lambda b: (b, 0, 0))
    grp = lambda w: pl.BlockSpec((3, tq, w), lambda b: (0, b, 0))
    return pl.pallas_call(
        functools.partial(_dil_sample_body, tq=tq), grid=(bsz,),
        in_specs=[grp(gw), grp(2 * gw)] + [st_spec(s) for s in states] + [_const(t.shape) for t in per_state]
        + [_const(bias_new.shape)],
        out_specs=[grp(gw), grp(gw)] + [st_spec(s) for s in states],
        out_shape=[jax.ShapeDtypeStruct((3, bsz * tq, gw), F32)] * 2
        + [jax.ShapeDtypeStruct(s.shape, F32) for s in states],
        scratch_shapes=[pltpu.VMEM((PAGE, 2 * gw), F32)],
        compiler_params=_cparams("parallel"), name="dilated_sample",
    )(q3, kv3, *states, *per_state, bias_new)


def _pad_even_w(w_in):
    return jnp.concatenate([w_in, jnp.zeros((w_in.shape[0], EVEN_PAD - w_in.shape[1]), w_in.dtype)], axis=1).astype(BF16)


def _even_mixer_prompt(h, gain, w_in, w_out, lb, hg_norm, qk_norm, cmp_pe, cmp_w1, cmp_w2, rel_bias, *, batch, seq, tm):
    n = batch * seq
    hg, q, cmp_rows, slc_rows, win_rows, slcg, wing, _, gates_t = _inproj_even(
        h, gain, _pad_even_w(w_in), qk_norm, tm=tm, gates_t=True)
    s0 = jnp.zeros((batch, HG_HEADS, HEAD_DIM, HEAD_DIM), F32)
    o_h, s_fin = _hgrn(hg, lb, hg_norm, s0, batch=batch, seq=seq)
    cw = _compress_weights(cmp_pe, cmp_w1, cmp_w2, qk_norm[1])
    npg = seq // PAGE
    table = jnp.arange(batch * npg, dtype=jnp.int32).reshape(batch, npg)
    _, kcv = _compress(cmp_rows.reshape(n // PAGE, PAGE, ROWS_NSA), table, cw, transposed=False)
    tables = _nsa_prompt_t_tables(rel_bias, seq, kcv.shape[2])
    o_nt = _nsa_prompt_t(q, gates_t, kcv, slcg, wing, qk_norm[0], tables, batch=batch, seq=seq)
    wo = w_out.astype(BF16)
    hw = HG_HEADS * HEAD_DIM
    h = _outproj2t(h, o_h, o_nt, wo[:hw], wo[hw:], tm=tm)
    return h, {"hgrn": s_fin, "cmp": cmp_rows, "slc": slc_rows, "win": win_rows}


def _even_mixer_sample(h, gain, w_in, w_out, lb, hg_norm, qk_norm, cmp_pe, cmp_w1, cmp_w2, rel_bias,
                       state_hgrn, cmp_pool, slc_pool, win_state, page_table, *, batch, tq, tm):
    assert tq < CMP_STRIDE and tq & (tq - 1) == 0
    hg, q, cmp_rows, slc_rows, win_rows, _, _, gates = _inproj_even(h, gain, _pad_even_w(w_in), qk_norm, tm=tm)
    o_h, s_fin = _hgrn(hg, lb, hg_norm, state_hgrn, batch=batch, seq=tq)
    cw = _compress_weights(cmp_pe, cmp_w1, cmp_w2, qk_norm[1])
    rows_t = lambda a: jnp.moveaxis(a, 1, -1).reshape(a.shape[0], ROWS_NSA, a.shape[1])
    crows, _ = _compress(rows_t(cmp_pool), page_table, cw, transposed=True)
    past = page_table.shape[1] * PAGE
    w_state = win_state.shape[1]
    tables = _nsa_sample_tables(rel_bias, past, tq, crows.shape[1], w_state)
    o_n, new_win_t = _nsa_sample(q, gates, crows, rows_t(slc_pool), page_table, slc_rows, rows_t(win_state), win_rows,
                                 qk_norm[0], tables, tq=tq)
    wo = w_out.astype(BF16)
    hw = HG_HEADS * HEAD_DIM
    h = _outproj2(h, o_h, o_n, wo[:hw], wo[hw:], tm=tm)
    new_win = jnp.moveaxis(new_win_t.reshape((batch,) + win_state.shape[2:] + (w_state,)), -1, 1)
    return h, {"hgrn": s_fin, "cmp": cmp_rows, "slc": slc_rows, "win": new_win}


def _odd_mixer_prompt(h, gain, w_in, w_out, qk_norm, rel_bias, *, batch, seq, tm):
    q3, kv3 = _inproj_odd(h, gain, w_in.astype(BF16), qk_norm, tm=tm)
    outs, lses = _dil_prompt(q3, kv3, _dil_prompt_tables(rel_bias), batch=batch, seq=seq)
    return _outproj_merge(h, outs, lses, w_out.astype(BF16), tm=tm), kv3


def _odd_mixer_sample(h, gain, w_in, w_out, qk_norm, rel_bias, states, *, batch, tq, tm):
    q3, kv3 = _inproj_odd(h, gain, w_in.astype(BF16), qk_norm, tm=tm)
    tables = _dil_sample_tables(rel_bias, tq, [s.shape[1] for s in states])
    states_t = [jnp.moveaxis(s, 1, -1).reshape(batch, 2 * ODD_GW, s.shape[1]) for s in states]
    o3, lse3, *new_t = _dil_sample(q3, kv3, states_t, tables, tq=tq)
    new = [jnp.moveaxis(n.reshape(s.shape[:1] + s.shape[2:] + s.shape[1:2]), -1, 1) for n, s in zip(new_t, states)]
    h = _outproj_merge(h, [o3[g] for g in range(3)], [lse3[g] for g in range(3)], w_out.astype(BF16), tm=tm)
    return h, new


TM_PROMPT = 512
TM_SAMPLE = 256


def kernel(x_prompt, x_sample, p_prompt, p_sample, state_hgrn, cache_nsa_cmp_kv, cache_nsa_slc_kv, state_nsa_win_kv,
           state_dil_kv_0, state_dil_kv_1, state_dil_kv_2, page_table, rel_bias, norm_ffn1, norm_mix, norm_ffn2,
           norm_ple, w_ffn1_in, w_ffn1_out, w_ffn2_in, w_ffn2_out, w_ple_gate, w_ple_proj, w_in_even, w_out_even,
           hgrn_lb_logits, hgrn_norm, nsa_qk_norm, nsa_cmp_pe, nsa_cmp_w1, nsa_cmp_w2, w_in_odd, w_out_odd,
           dil_qk_norm):
    depth = norm_ffn1.shape[0]
    dil_states = (state_dil_kv_0, state_dil_kv_1, state_dil_kv_2)
    bf = lambda w: w.astype(BF16)
    w1i, w1o, w2i, w2o, wpg, wpp = (bf(w) for w in (w_ffn1_in, w_ffn1_out, w_ffn2_in, w_ffn2_out, w_ple_gate,
                                                     w_ple_proj))
    lb_all = jnp.cumsum(jax.nn.softmax(hgrn_lb_logits.astype(F32), axis=0), axis=0)

    def run(x, p, sample):
        batch, seq, d = x.shape
        n = batch * seq
        tm = TM_SAMPLE if sample else TM_PROMPT
        h = x.reshape(n, d)
        p = p.reshape(depth, n, p.shape[-1])
        even, odd = [], []
        for li in range(depth):
            h = _ffn(h, norm_ffn1[li], w1i[li], w1o[li], tm=tm)
            if li % 2 == 0:
                ei = li // 2
                args = (h, norm_mix[li], w_in_even[ei], w_out_even[ei], lb_all[ei], hgrn_norm[ei], nsa_qk_norm[ei],
                        nsa_cmp_pe[ei], nsa_cmp_w1[ei], nsa_cmp_w2[ei], rel_bias)
                if sample:
                    h, st = _even_mixer_sample(*args, state_hgrn[ei], cache_nsa_cmp_kv[ei], cache_nsa_slc_kv[ei],
                                               state_nsa_win_kv[ei], page_table, batch=batch, tq=seq, tm=tm)
                else:
                    h, st = _even_mixer_prompt(*args, batch=batch, seq=seq, tm=tm)
                    st["win"] = st["win"].reshape(batch, seq, ROWS_NSA)[:, -min(NSA_WINDOW, seq):]
                even.append(st)
            else:
                oi = li // 2
                args = (h, norm_mix[li], w_in_odd[oi], w_out_odd[oi], dil_qk_norm[oi], rel_bias)
                if sample:
                    h, bufs = _odd_mixer_sample(*args, [s[oi] for s in dil_states], batch=batch, tq=seq, tm=tm)
                else:
                    h, kv3 = _odd_mixer_prompt(*args, batch=batch, seq=seq, tm=tm)
                    bufs = [kv3[g].reshape(batch, seq, 2 * ODD_GW)[:, -min(w, seq):]
                            for g, (w, _) in enumerate(DIL_PATTERN)]
                odd.append(bufs)
            h = _ffn(h, norm_ffn2[li], w2i[li], w2o[li], ple=(p[li], norm_ple[li], wpg[li], wpp[li]), tm=tm)
        rows = lambda a: a.reshape(batch, -1, 2, NSA_KV, HEAD_DIM)
        drows = lambda a: a.reshape(batch, -1, 2, DIL_HEADS, HEAD_DIM)
        return (h.reshape(batch, seq, d),
                jnp.stack([s["hgrn"] for s in even]), jnp.stack([rows(s["cmp"]) for s in even]),
                jnp.stack([rows(s["slc"]) for s in even]), jnp.stack([rows(s["win"]) for s in even]),
                jnp.stack([drows(b[0]) for b in odd]), jnp.stack([drows(b[1]) for b in odd]),
                jnp.stack([drows(b[2]) for b in odd]))

    y_p, hg_p, cmp_p, slc_p, win_p, d0_p, d1_p, d2_p = run(x_prompt, p_prompt, False)
    y_s, hg_s, cmp_s, slc_s, win_s, d0_s, d1_s, d2_s = run(x_sample, p_sample, True)
    return (y_p, y_s, hg_p, hg_s, cmp_p, cmp_s, slc_p, slc_s, win_p, win_s, d0_p, d0_s, d1_p, d1_s, d2_p, d2_s)
```

```python
import functools
import math

import numpy as np
import jax
import jax.numpy as jnp
from jax import lax
from jax.experimental import pallas as pl
from jax.experimental.pallas import tpu as pltpu

F32 = jnp.float32
BF16 = jnp.bfloat16
HIGHEST = lax.Precision.HIGHEST

V7X_VMEM_BYTES = 64 * 1024 * 1024
VMEM_LIMIT = V7X_VMEM_BYTES * 7 // 8
LANES = 128

EPS = 1e-6
NEG = -1e30
NEG_TEST = -1e29

HEAD_DIM = 64
HG_HEADS = 8
HG_CHUNK = 64
NSA_HEADS = 12
NSA_KV = 3
NSA_GROUP = NSA_HEADS // NSA_KV
CMP_BLOCK = 32
CMP_STRIDE = 16
SEL_BLOCK = 64
SEL_TOPK = 16
NSA_WINDOW = 512
FORCE_SCORE = 1e4
PAGE = 128
DIL_PATTERN = ((128, 1), (512, 4), (2048, 16))
DIL_HEADS = 4
RP_BUCKETS = 32
RP_MAX_DIST = 2048


def _cparams(*sem):
    return pltpu.CompilerParams(dimension_semantics=sem, vmem_limit_bytes=VMEM_LIMIT)


def _const(shape):
    nd = len(shape)
    return pl.BlockSpec(shape, lambda *_: (0,) * nd, pipeline_mode=pl.Buffered(1))


def _dot(a, b):
    return jnp.dot(a, b, preferred_element_type=F32)


def _dot_nt(a, b):
    return lax.dot_general(a, b, (((1,), (1,)), ((), ())), preferred_element_type=F32)


def _dot_tn(a, b):
    return lax.dot_general(a, b, (((0,), (0,)), ((), ())), preferred_element_type=F32)


def _rms(x, gain):
    return x * lax.rsqrt(jnp.mean(x * x, axis=-1, keepdims=True) + EPS) * gain


def _seg_ms(x, bd):
    sq = x * x
    hi = sq.astype(BF16)
    lo = (sq - hi.astype(F32)).astype(BF16)
    return (_dot(hi, bd) + _dot(lo, bd)) * (1.0 / HEAD_DIM)


def _seg_rms(x, bd, gain):
    return x * lax.rsqrt(_seg_ms(x, bd) + EPS) * gain


def _norm_k_rows(x, bd, gain, kmask):
    r = lax.rsqrt(_seg_ms(x, bd) + EPS)
    return x * jnp.where(kmask > 0.5, r, 1.0) * gain


def _block_diag_ones(width):
    i = np.arange(width) // HEAD_DIM
    return jnp.asarray((i[:, None] == i[None, :]).astype(np.float32), BF16)


def _rel_bucket(dist):
    exact = RP_BUCKETS // 2
    d = jnp.maximum(dist, 0)
    log_ratio = jnp.log(jnp.maximum(d, 1).astype(F32) / exact) / math.log(RP_MAX_DIST / exact)
    large = jnp.minimum(exact + (log_ratio * (RP_BUCKETS - exact)).astype(jnp.int32), RP_BUCKETS - 1)
    return jnp.where(d < exact, d, large)


def _bias_tile(rel_bias, dist, valid):
    onehot = (_rel_bucket(dist)[..., None] == jnp.arange(RP_BUCKETS)).astype(F32)
    t = jnp.einsum('...k,kh->h...', onehot, rel_bias.astype(F32), precision=HIGHEST)
    return jnp.where(valid[None], t, NEG)


def _toeplitz(ext, n, m):
    p = n + m - 1
    u = jnp.concatenate([jnp.flip(ext[..., :m], -1), jnp.flip(ext[..., m:], -1)], axis=-1)
    reps = (1,) * (ext.ndim - 1) + (n,)
    t = jnp.tile(u, reps)[..., :n * (p - 1)].reshape(ext.shape[:-1] + (n, p - 1))
    return t[..., :m]


def _softmax_piece(s, mask, m_prev, l_prev, acc_prev, v, v_t=False):
    m_new = jnp.maximum(m_prev, jnp.max(s, axis=-1, keepdims=True))
    alpha = jnp.exp(m_prev - m_new)
    p = jnp.where(mask, jnp.exp(s - m_new), 0.0)
    l_new = alpha * l_prev + jnp.sum(p, axis=-1, keepdims=True)
    pb = p.astype(BF16)
    acc_new = alpha * acc_prev + (_dot_nt(pb, v) if v_t else _dot(pb, v))
    return m_new, l_new, acc_new


def _ffn_body(x_ref, g_ref, win_ref, wout_ref, *rest, dff, ple):
    x = x_ref[...]
    xn = _rms(x, g_ref[...]).astype(BF16)
    gu = _dot(xn, win_ref[...])
    gg = gu[:, :dff]
    a = (gg * jax.nn.sigmoid(gg) * gu[:, dff:]).astype(BF16)
    h = x + 0.5 * _dot(a, wout_ref[...])
    if ple:
        p_ref, gp_ref, wg_ref, wp_ref, o_ref = rest
        hn = _rms(h, gp_ref[...]).astype(BF16)
        gate = jax.nn.sigmoid(_dot(hn, wg_ref[...]))
        h = h + gate * _dot(p_ref[...].astype(BF16), wp_ref[...])
    else:
        (o_ref,) = rest
    o_ref[...] = h


def _ffn(h, gain, w_in, w_out, ple=None, *, tm):
    n, d = h.shape
    dff = w_out.shape[0]
    row = lambda i: (i, 0)
    in_specs = [pl.BlockSpec((tm, d), row), _const((1, d)), _const(w_in.shape), _const(w_out.shape)]
    args = [h, gain.reshape(1, d), w_in, w_out]
    if ple is not None:
        p, gp, wg, wp = ple
        in_specs += [pl.BlockSpec((tm, p.shape[1]), row), _const((1, d)), _const(wg.shape), _const(wp.shape)]
        args += [p, gp.reshape(1, d), wg, wp]
    return pl.pallas_call(
        functools.partial(_ffn_body, dff=dff, ple=ple is not None),
        grid=(n // tm,), in_specs=in_specs, out_specs=pl.BlockSpec((tm, d), row),
        out_shape=jax.ShapeDtypeStruct((n, d), F32), compiler_params=_cparams("parallel"),
        name="ffn_ple" if ple is not None else "ffn",
    )(*args)


EVEN_HG = 4 * HG_HEADS * HEAD_DIM
EVEN_Q = NSA_HEADS * HEAD_DIM
ROWS_NSA = 2 * NSA_KV * HEAD_DIM
EVEN_PAD = EVEN_HG + EVEN_Q + 3 * ROWS_NSA + LANES


GATE_ROWS = 16


def _inproj_even_body(x_ref, g_ref, w_ref, gk_ref, km_ref, bd_ref,
                      hg_ref, q_ref, cmp_ref, slc_ref, win_ref, slcg_ref, wing_ref, gate_ref, *gate_t_ref):
    xn = _rms(x_ref[...], g_ref[...]).astype(BF16)
    p = _dot(xn, w_ref[...])
    c0 = EVEN_HG
    c1 = c0 + EVEN_Q
    hg_ref[...] = p[:, :c0]
    q_ref[...] = p[:, c0:c1]
    cmp_ref[...] = p[:, c1:c1 + ROWS_NSA]
    bd = bd_ref[...]
    km = km_ref[...]
    slc = _norm_k_rows(p[:, c1 + ROWS_NSA:c1 + 2 * ROWS_NSA], bd, gk_ref[0:1, :], km)
    win = _norm_k_rows(p[:, c1 + 2 * ROWS_NSA:c1 + 3 * ROWS_NSA], bd, gk_ref[1:2, :], km)
    slc_ref[...] = slc
    win_ref[...] = win
    sig = jax.nn.sigmoid(p[:, c1 + 3 * ROWS_NSA:])
    kw = NSA_KV * HEAD_DIM
    for g in range(NSA_KV):
        ks = slice(HEAD_DIM * g, HEAD_DIM * (g + 1))
        vs = slice(kw + HEAD_DIM * g, kw + HEAD_DIM * (g + 1))
        slcg_ref[g] = jnp.concatenate([slc[:, ks], slc[:, vs]], axis=1)
        wing_ref[g] = jnp.concatenate([win[:, ks], win[:, vs]], axis=1)
        gate_ref[g] = sig if g == 0 else pltpu.roll(sig, LANES - 3 * NSA_GROUP * g, 1)
    if gate_t_ref:
        sig_t = sig.T
        for g in range(NSA_KV):
            for half in range(sig.shape[0] // NSA_TQ):
                gate_t_ref[0][g, half] = sig_t[3 * NSA_GROUP * g:3 * NSA_GROUP * g + GATE_ROWS,
                                               half * NSA_TQ:(half + 1) * NSA_TQ]


def _inproj_even(h, gain, w_pad, qk_norm, *, tm, gates_t=False):
    n, d = h.shape
    kw = NSA_KV * HEAD_DIM
    ones = jnp.ones((kw,), F32)
    gk = jnp.stack([jnp.concatenate([jnp.tile(qk_norm[2], NSA_KV), ones]),
                    jnp.concatenate([jnp.tile(qk_norm[3], NSA_KV), ones])])
    km = jnp.concatenate([ones, 0.0 * ones]).reshape(1, ROWS_NSA)
    row = lambda i: (i, 0)
    grp = lambda i: (0, i, 0)
    shp = lambda *s: jax.ShapeDtypeStruct(s, F32)
    out_specs = [pl.BlockSpec((tm, EVEN_HG), row), pl.BlockSpec((tm, EVEN_Q), row),
                 pl.BlockSpec((tm, ROWS_NSA), row), pl.BlockSpec((tm, ROWS_NSA), row),
                 pl.BlockSpec((tm, ROWS_NSA), row), pl.BlockSpec((NSA_KV, tm, LANES), grp),
                 pl.BlockSpec((NSA_KV, tm, LANES), grp), pl.BlockSpec((NSA_KV, tm, LANES), grp)]
    out_shape = [shp(n, EVEN_HG), shp(n, EVEN_Q), shp(n, ROWS_NSA), shp(n, ROWS_NSA), shp(n, ROWS_NSA),
                 shp(NSA_KV, n, LANES), shp(NSA_KV, n, LANES), shp(NSA_KV, n, LANES)]
    if gates_t:
        out_specs.append(pl.BlockSpec((NSA_KV, tm // NSA_TQ, GATE_ROWS, NSA_TQ), lambda i: (0, i, 0, 0)))
        out_shape.append(shp(NSA_KV, n // NSA_TQ, GATE_ROWS, NSA_TQ))
    return pl.pallas_call(
        _inproj_even_body, grid=(n // tm,),
        in_specs=[pl.BlockSpec((tm, d), row), _const((1, d)), _const(w_pad.shape), _const((2, ROWS_NSA)),
                  _const((1, ROWS_NSA)), _const((ROWS_NSA, ROWS_NSA))],
        out_specs=out_specs, out_shape=out_shape,
        compiler_params=_cparams("parallel"), name="inproj_even",
    )(h, gain.reshape(1, d), w_pad, gk, km, _block_diag_ones(ROWS_NSA))


ODD_W = 3 * DIL_HEADS * HEAD_DIM
ODD_GW = DIL_HEADS * HEAD_DIM


def _to_residue_view(src, dst_ref, stage_ref, dil):
    rows, w = src.shape
    for blk in range(w // LANES):
        stage_ref[...] = src[:, blk * LANES:(blk + 1) * LANES]
        for r in range(dil):
            lo = r * w + blk * LANES
            dst_ref[:, lo:lo + LANES] = stage_ref[pl.ds(r, rows // dil, stride=dil), :]


def _from_residue_view(src_ref, stage_ref, dil, w):
    rows = src_ref.shape[0] * dil
    parts = []
    for blk in range(w // LANES):
        for r in range(dil):
            lo = r * w + blk * LANES
            stage_ref[pl.ds(r, rows // dil, stride=dil), :] = src_ref[:, lo:lo + LANES]
        parts.append(stage_ref[...])
    return jnp.concatenate(parts, axis=1)


def _inproj_odd_body(x_ref, g_ref, w_ref, gq_ref, gk_ref, bd_ref, q_ref, kv_ref, *views):
    xn = _rms(x_ref[...], g_ref[...]).astype(BF16)
    p = _dot(xn, w_ref[...])
    bd = bd_ref[...]
    q = _seg_rms(p[:, :ODD_W], bd, gq_ref[...]) * (HEAD_DIM ** -0.5)
    k = _seg_rms(p[:, ODD_W:2 * ODD_W], bd, gk_ref[...])
    v = p[:, 2 * ODD_W:]
    for g in range(3):
        s = slice(ODD_GW * g, ODD_GW * (g + 1))
        kv = jnp.concatenate([k[:, s], v[:, s]], axis=1)
        q_ref[g] = q[:, s]
        kv_ref[g] = kv
        dil = DIL_PATTERN[g][1]
        if views and dil > 1:
            stage_ref = views[-1]
            _to_residue_view(q[:, s], views[2 * (g - 1)], stage_ref, dil)
            _to_residue_view(kv, views[2 * (g - 1) + 1], stage_ref, dil)


def _inproj_odd(h, gain, w, qk_norm, *, tm, views=False):
    n, d = h.shape
    row = lambda i: (i, 0)
    nh = 3 * DIL_HEADS
    out_specs = [pl.BlockSpec((3, tm, ODD_GW), lambda i: (0, i, 0)),
                 pl.BlockSpec((3, tm, 2 * ODD_GW), lambda i: (0, i, 0))]
    out_shape = [jax.ShapeDtypeStruct((3, n, ODD_GW), F32), jax.ShapeDtypeStruct((3, n, 2 * ODD_GW), F32)]
    if views:
        for _, dil in DIL_PATTERN[1:]:
            for width in (ODD_GW, 2 * ODD_GW):
                out_specs.append(pl.BlockSpec((tm // dil, dil * width), row))
                out_shape.append(jax.ShapeDtypeStruct((n // dil, dil * width), F32))
    return pl.pallas_call(
        _inproj_odd_body, grid=(n // tm,),
        in_specs=[pl.BlockSpec((tm, d), row), _const((1, d)), _const(w.shape), _const((1, ODD_W)),
                  _const((1, ODD_W)), _const((ODD_W, ODD_W))],
        out_specs=out_specs, out_shape=out_shape,
        scratch_shapes=[pltpu.VMEM((tm, LANES), F32)] if views else [],
        compiler_params=_cparams("parallel"), name="inproj_odd",
    )(h, gain.reshape(1, d), w, jnp.tile(qk_norm[0], nh).reshape(1, ODD_W),
      jnp.tile(qk_norm[1], nh).reshape(1, ODD_W), _block_diag_ones(ODD_W))


def _outproj2_body(h_ref, a_ref, b_ref, wa_ref, wb_ref, o_ref):
    o_ref[...] = (h_ref[...] + _dot(a_ref[...].astype(BF16), wa_ref[...])
                  + _dot(b_ref[...].astype(BF16), wb_ref[...]))


def _outproj2(h, a, b, wa, wb, *, tm):
    n, d = h.shape
    row = lambda i: (i, 0)
    return pl.pallas_call(
        _outproj2_body, grid=(n // tm,),
        in_specs=[pl.BlockSpec((tm, d), row), pl.BlockSpec((tm, a.shape[1]), row),
                  pl.BlockSpec((tm, b.shape[1]), row), _const(wa.shape), _const(wb.shape)],
        out_specs=pl.BlockSpec((tm, d), row), out_shape=jax.ShapeDtypeStruct((n, d), F32),
        compiler_params=_cparams("parallel"), name="outproj_even",
    )(h, a, b, wa, wb)


def _outproj2t_body(h_ref, a_ref, bt_ref, wa_ref, wb_ref, o_ref):
    acc = h_ref[...] + _dot(a_ref[...].astype(BF16), wa_ref[...])
    tq = bt_ref.shape[2]
    for i in range(bt_ref.shape[0]):
        rs = slice(i * tq, (i + 1) * tq)
        o_ref[rs, :] = acc[rs, :] + _dot_tn(bt_ref[i].astype(BF16), wb_ref[...])


def _outproj2t(h, a, bt, wa, wb, *, tm):
    n, d = h.shape
    row = lambda i: (i, 0)
    tq = bt.shape[2]
    return pl.pallas_call(
        _outproj2t_body, grid=(n // tm,),
        in_specs=[pl.BlockSpec((tm, d), row), pl.BlockSpec((tm, a.shape[1]), row),
                  pl.BlockSpec((tm // tq, bt.shape[1], tq), lambda i: (i, 0, 0)), _const(wa.shape), _const(wb.shape)],
        out_specs=pl.BlockSpec((tm, d), row), out_shape=jax.ShapeDtypeStruct((n, d), F32),
        compiler_params=_cparams("parallel"), name="outproj_even_t",
    )(h, a, bt, wa, wb)


def _outproj_merge_body(h_ref, o0_ref, o1_ref, o2_ref, l0_ref, l1_ref, l2_ref, w_ref, o_ref, *stage, views):
    if views:
        load = lambda ref, g: (ref[...] if DIL_PATTERN[g][1] == 1
                               else _from_residue_view(ref, stage[0], DIL_PATTERN[g][1], ODD_GW))
    else:
        load = lambda ref, g: ref[...]
    mixed = _merge_groups([load(r, g) for g, r in enumerate((o0_ref, o1_ref, o2_ref))],
                          [load(r, g) for g, r in enumerate((l0_ref, l1_ref, l2_ref))])
    o_ref[...] = h_ref[...] + _dot(mixed.astype(BF16), w_ref[...])


def _outproj_merge(h, outs, lses, w, *, tm, views=False):
    n, d = h.shape
    row = lambda i: (i, 0)
    parts = [pl.BlockSpec((tm * a.shape[0] // n, a.shape[1]), row) for a in list(outs) + list(lses)]
    return pl.pallas_call(
        functools.partial(_outproj_merge_body, views=views), grid=(n // tm,),
        in_specs=[pl.BlockSpec((tm, d), row)] + parts + [_const(w.shape)],
        out_specs=pl.BlockSpec((tm, d), row), out_shape=jax.ShapeDtypeStruct((n, d), F32),
        scratch_shapes=[pltpu.VMEM((tm, LANES), F32)] if views else [],
        compiler_params=_cparams("parallel"), name="outproj_odd",
    )(h, *outs, *lses, w)


def _hgrn_prefix_matrix(C):
    t = np.arange(C)[:, None]
    j = np.arange(C)[None, :]
    mats = [j <= t]
    lvl = 1
    while (1 << lvl) <= C:
        mats.append(j <= ((t >> lvl) << lvl) + (1 << lvl) // 2 - 1)
        lvl += 1
    return jnp.asarray(np.concatenate(mats, axis=0).astype(np.float32), BF16)


def _hgrn_body(hq_ref, hf_ref, hi_ref, hgt_ref, lb_ref, gn_ref, bd_ref, pm_ref, s0_ref, o_ref, sfin_ref, st_ref, *, C):
    c = pl.program_id(1)

    @pl.when(c == 0)
    def _():
        st_ref[...] = s0_ref[0]

    dk = HEAD_DIM
    w = HG_HEADS * dk
    lb = lb_ref[...]
    f = lb + (1.0 - lb) * jax.nn.sigmoid(hf_ref[...])
    lf = jnp.log(f)
    kk = 1.0 - f
    ti = lax.broadcasted_iota(jnp.int32, (C, C), 0)
    si = lax.broadcasted_iota(jnp.int32, (C, C), 1)
    p1 = lf.astype(BF16)
    r1 = lf - p1.astype(F32)
    p2 = r1.astype(BF16)
    p3 = (r1 - p2.astype(F32)).astype(BF16)
    sums = _dot(pm_ref[...], jnp.concatenate([p1, p2, p3], axis=1))
    sums = sums[:, :w] + sums[:, w:2 * w] + sums[:, 2 * w:]
    b = sums[0:C]
    q = hq_ref[...]
    v = hi_ref[...].astype(BF16)
    heads = range(HG_HEADS)
    hs = [slice(dk * h, dk * (h + 1)) for h in heads]

    qb = q.astype(BF16)
    kb = kk.astype(BF16)
    a = [jnp.where(ti == si, _dot_nt(qb[:, hs[h]], kb[:, hs[h]]), 0.0) for h in heads]
    lvl = 1
    while (1 << lvl) <= C:
        m = 1 << lvl
        half = m // 2
        r = sums[lvl * C:(lvl + 1) * C]
        e = jnp.exp(-jnp.abs(b - r))
        qt = (q * e).astype(BF16)
        kt = (kk * e).astype(BF16)
        pair = ((ti >> lvl) == (si >> lvl)) & ((ti & (m - 1)) >= half) & ((si & (m - 1)) < half)
        a = [a[h] + jnp.where(pair, _dot_nt(qt[:, hs[h]], kt[:, hs[h]]), 0.0) for h in heads]
        lvl += 1

    b_last = b[C - 1:C, :]
    qe = (q * jnp.exp(b)).astype(BF16)
    kdec = (kk * jnp.exp(b_last - b)).astype(BF16)
    e_last = jnp.exp(b_last)
    outs = []
    for h in heads:
        st = st_ref[h]
        outs.append(_dot(a[h].astype(BF16), v[:, hs[h]]) + _dot_nt(qe[:, hs[h]], st.astype(BF16)))
        st_ref[h] = st * e_last[:, hs[h]] + _dot_tn(v[:, hs[h]], kdec[:, hs[h]])
    o = jnp.concatenate(outs, axis=1)
    gt = hgt_ref[...]
    o_ref[...] = _seg_rms(o, bd_ref[...], gn_ref[...]) * (gt * jax.nn.sigmoid(gt))

    @pl.when(c == pl.num_programs(1) - 1)
    def _():
        sfin_ref[0] = st_ref[...]


def _hgrn(hg, lb, hg_norm, s0, *, batch, seq):
    C = math.gcd(seq, HG_CHUNK)
    nc = seq // C
    w = HG_HEADS * HEAD_DIM
    col = lambda j: pl.BlockSpec((C, w), lambda b, c: (b * nc + c, j))
    st_spec = pl.BlockSpec((1, HG_HEADS, HEAD_DIM, HEAD_DIM), lambda b, c: (b, 0, 0, 0))
    pm = _hgrn_prefix_matrix(C)
    o, sfin = pl.pallas_call(
        functools.partial(_hgrn_body, C=C), grid=(batch, nc),
        in_specs=[col(0), col(1), col(2), col(3), _const((1, w)), _const((1, w)), _const((w, w)), _const(pm.shape),
                  st_spec],
        out_specs=[pl.BlockSpec((C, w), lambda b, c: (b * nc + c, 0)), st_spec],
        out_shape=[jax.ShapeDtypeStruct((batch * seq, w), F32),
                   jax.ShapeDtypeStruct((batch, HG_HEADS, HEAD_DIM, HEAD_DIM), F32)],
        scratch_shapes=[pltpu.VMEM((HG_HEADS, HEAD_DIM, HEAD_DIM), F32)],
        compiler_params=_cparams("parallel", "arbitrary"), name="hgrn_scan",
    )(hg, hg, hg, hg, lb.reshape(1, w), jnp.tile(hg_norm, HG_HEADS).reshape(1, w), _block_diag_ones(w), pm,
      jnp.swapaxes(s0, 2, 3))
    return o, jnp.swapaxes(sfin, 2, 3)


CHUNKS_PER_PAGE = PAGE // CMP_STRIDE
PAGE_LANES = CMP_STRIDE * ROWS_NSA
CMP_HID = 2 * NSA_KV * 2 * HEAD_DIM
CMP_PP = 8


def _compress_body(pt_ref, *refs, n_chunks, transposed):
    pages = refs[:CMP_PP]
    wbig_ref, pe_ref, w2_ref, gk_ref, km_ref, bd_ref, rows_ref, grp_ref, xs_ref = refs[CMP_PP:CMP_PP + 9]
    pg_scr = refs[CMP_PP + 9:]
    nb = ROWS_NSA // LANES
    j = pl.program_id(1)
    for i in range(0, CMP_PP, 2):
        for k in range(2):
            for c in range(nb):
                cs = slice(c * LANES, (c + 1) * LANES)
                pg_scr[k * nb + c][...] = pages[i + k][0, cs, :].T if transposed else pages[i + k][0, :, cs]
        r0 = pl.multiple_of((j * CMP_PP + i) * CHUNKS_PER_PAGE, 2 * CHUNKS_PER_PAGE)
        for s in range(CMP_STRIDE):
            rows_s = pl.ds(s, CHUNKS_PER_PAGE, stride=CMP_STRIDE)
            for c in range(nb):
                blk = jnp.concatenate([pg_scr[c][rows_s, :], pg_scr[nb + c][rows_s, :]], axis=0).astype(BF16)
                lo = (c * CMP_STRIDE + s) * LANES
                xs_ref[pl.ds(r0, 2 * CHUNKS_PER_PAGE), lo:lo + LANES] = blk

    @pl.when(j == pl.num_programs(1) - 1)
    def _():
        blk_w = CMP_STRIDE * LANES
        first = lambda x: jnp.concatenate(
            [_dot(x[:, c * blk_w:(c + 1) * blk_w], wbig_ref[c]) for c in range(nb)], axis=1)
        y = first(xs_ref[...])
        ysh = pltpu.roll(y, n_chunks - 1, 0)
        hid = y + pltpu.roll(ysh, CMP_HID - HEAD_DIM, 1)
        cf = first(pe_ref[...].astype(BF16))
        lane = lax.broadcasted_iota(jnp.int32, (8, CMP_HID), 1)
        cs = jnp.where(((lane >> 6) & 1) == 0, jnp.broadcast_to(cf[0:1], (8, CMP_HID)),
                       jnp.broadcast_to(cf[1:2], (8, CMP_HID)))
        cs = cs + pltpu.roll(cs, CMP_HID - HEAD_DIM, 1)
        hid = hid + cs[0:1]
        act = (hid * jax.nn.sigmoid(hid)).astype(BF16)
        out = _dot(act, w2_ref[...])
        out = _norm_k_rows(out, bd_ref[...], gk_ref[...], km_ref[...])
        rows_ref[0] = out
        kw = NSA_KV * HEAD_DIM
        for g in range(NSA_KV):
            grp_ref[0, g] = jnp.concatenate([out[:, HEAD_DIM * g:HEAD_DIM * (g + 1)],
                                             out[:, kw + HEAD_DIM * g:kw + HEAD_DIM * (g + 1)]], axis=1)


def _compress_weights(cmp_pe, cmp_w1, cmp_w2, k_gain):
    ratio = CMP_BLOCK // CMP_STRIDE
    w1 = cmp_w1.reshape(2, ratio, CMP_STRIDE, HEAD_DIM, HEAD_DIM)
    eye_kv = jnp.eye(2, dtype=F32)
    eye_g = jnp.eye(NSA_KV, dtype=F32)
    nb = ROWS_NSA // LANES
    member_kv = np.arange(2 * NSA_KV) // NSA_KV
    w1m = w1[member_kv].reshape(nb, 2, ratio, CMP_STRIDE, HEAD_DIM, HEAD_DIM)
    wbig = jnp.einsum('chrsde,hH->cshdHre', w1m, eye_kv).reshape(nb, CMP_STRIDE * LANES, 2 * ratio * HEAD_DIM)
    wbig = wbig.astype(BF16)
    pe = cmp_pe.reshape(2, ratio, CMP_STRIDE, HEAD_DIM)
    pe_rows = pe[member_kv].reshape(nb, 2, ratio, CMP_STRIDE, HEAD_DIM).transpose(2, 0, 3, 1, 4)
    pe_rows = pe_rows.reshape(ratio, PAGE_LANES)
    pe_rows = jnp.concatenate([pe_rows, jnp.zeros((8 - ratio, PAGE_LANES), F32)], axis=0)
    sel_r0 = jnp.asarray([1.0, 0.0], F32)
    w2big = jnp.einsum('ked,kK,gG,r->kgreKGd', cmp_w2, eye_kv, eye_g, sel_r0).reshape(CMP_HID, ROWS_NSA).astype(BF16)
    kw = NSA_KV * HEAD_DIM
    gk = jnp.concatenate([jnp.tile(k_gain, NSA_KV), jnp.ones((kw,), F32)]).reshape(1, ROWS_NSA)
    km = jnp.concatenate([jnp.ones((kw,), F32), jnp.zeros((kw,), F32)]).reshape(1, ROWS_NSA)
    return wbig, pe_rows, w2big, gk, km


def _compress(pool, table, cw, *, transposed):
    wbig, pe_rows, w2big, gk, km = cw
    bsz, npg = table.shape
    n_chunks = npg * CHUNKS_PER_PAGE
    page = lambda i: pl.BlockSpec((1,) + pool.shape[1:], lambda b, j, pt: (pt[b, j * CMP_PP + i], 0, 0))
    return pl.pallas_call(
        functools.partial(_compress_body, n_chunks=n_chunks, transposed=transposed),
        grid_spec=pltpu.PrefetchScalarGridSpec(
            num_scalar_prefetch=1, grid=(bsz, npg // CMP_PP),
            in_specs=[page(i) for i in range(CMP_PP)] + [
                _const(wbig.shape), _const(pe_rows.shape), _const(w2big.shape), _const((1, ROWS_NSA)),
                _const((1, ROWS_NSA)), _const((ROWS_NSA, ROWS_NSA))],
            out_specs=[pl.BlockSpec((1, n_chunks, ROWS_NSA), lambda b, j, pt: (b, 0, 0)),
                       pl.BlockSpec((1, NSA_KV, n_chunks, LANES), lambda b, j, pt: (b, 0, 0, 0))],
            scratch_shapes=[pltpu.VMEM((n_chunks, PAGE_LANES), BF16)]
            + [pltpu.VMEM((PAGE, LANES), F32)] * (2 * ROWS_NSA // LANES)),
        out_shape=[jax.ShapeDtypeStruct((bsz, n_chunks, ROWS_NSA), F32),
                   jax.ShapeDtypeStruct((bsz, NSA_KV, n_chunks, LANES), F32)],
        compiler_params=_cparams("parallel", "arbitrary"), name="nsa_compress",
    )(table, *([pool] * CMP_PP), wbig, pe_rows, w2big, gk, km, _block_diag_ones(ROWS_NSA))


def _select_blocks(imp, pos, n_slc, k_sel):
    lane = lax.broadcasted_iota(jnp.int32, imp.shape, 1)
    cur = pos >> 6
    forced = (lane == 0) | (lane == cur) | (lane == cur - 1)
    score = jnp.where(forced, FORCE_SCORE, jnp.where(lane <= cur, imp, -1.0))
    rank = jnp.zeros(imp.shape, F32)
    for i in range(n_slc):
        ci = score[:, i:i + 1]
        rank = rank + jnp.where(ci > score, 1.0, jnp.where(ci == score, jnp.where(lane > i, 1.0, 0.0), 0.0))
    return jnp.where(lane < n_slc, jnp.where(rank < k_sel, 1.0, 0.0), 0.0)


NSA_TQ = 256


def _nsa_prompt_body(q_ref, gate_ref, kcv_ref, slc_ref, win_ref, gq_ref, bd_ref, bc_ref, bs_ref, ov_ref, ex_ref,
                     o_ref, sel_scr, m_scr, l_scr, acc_scr, *, seq, n_slc, k_sel):
    tq = NSA_TQ
    rows = NSA_GROUP * tq
    n_tiles = seq // tq
    hd = HEAD_DIM
    kc = kcv_ref[0, 0, :, :hd].astype(BF16)
    vc = kcv_ref[0, 0, :, hd:].astype(BF16)
    n_c = kc.shape[0]
    ii = lax.broadcasted_iota(jnp.int32, (tq, tq), 0)
    jj = lax.broadcasted_iota(jnp.int32, (tq, tq), 1)

    def flash(qs, kv_ref, t, lo, hi, extra_mask):
        m_scr[...] = jnp.full((rows, 1), NEG, F32)
        l_scr[...] = jnp.zeros((rows, 1), F32)
        acc_scr[...] = jnp.zeros((rows, hd), F32)

        def step(jt, carry):
            k0 = pl.multiple_of(jt * tq, tq)
            kv = kv_ref[0, pl.ds(k0, tq), :]
            bias = bs_ref[0, t - jt]
            ok = (bias > NEG_TEST).reshape(NSA_GROUP, tq, tq) & extra_mask(jt)[None]
            s = (_dot_nt(qs, kv[:, :hd].astype(BF16)) + bias).reshape(NSA_GROUP, tq, tq)
            s = jnp.where(ok, s, NEG).reshape(rows, tq)
            m, l, acc = _softmax_piece(s, ok.reshape(rows, tq), m_scr[...], l_scr[...], acc_scr[...],
                                       kv[:, hd:].astype(BF16))
            m_scr[...] = m
            l_scr[...] = l
            acc_scr[...] = acc
            return carry

        lax.fori_loop(lo, hi + 1, step, 0)
        return acc_scr[...] / jnp.maximum(l_scr[...], 1e-30)

    def q_tile(t, carry):
        r0 = pl.multiple_of(t * tq, tq)
        qn = _seg_rms(q_ref[pl.ds(r0, tq), :], bd_ref[...], gq_ref[...]) * (hd ** -0.5)
        qs = jnp.concatenate([qn[:, hd * h:hd * (h + 1)] for h in range(NSA_GROUP)], axis=0).astype(BF16)
        bias_c = bc_ref[0, :, pl.ds(r0, tq), :].reshape(rows, n_c)
        ok_c = bias_c > NEG_TEST
        s_c = _dot_nt(qs, kc) + bias_c
        m_c = jnp.max(s_c, axis=-1, keepdims=True)
        e_c = jnp.where(ok_c, jnp.exp(s_c - m_c), 0.0)
        p_c = (e_c / jnp.maximum(jnp.sum(e_c, axis=-1, keepdims=True), 1e-30)).astype(BF16)
        o_c = _dot(p_c, vc)
        d = _dot(p_c, ov_ref[...])
        imp = d[0:tq] + d[tq:2 * tq] + d[2 * tq:3 * tq] + d[3 * tq:4 * tq]
        pos = r0 + lax.broadcasted_iota(jnp.int32, (tq, 1), 0)
        sel = _select_blocks(imp, pos, n_slc, k_sel).astype(BF16)
        for jt in range(n_tiles):
            sel_scr[jt] = _dot(sel, ex_ref[:, jt * tq:(jt + 1) * tq])
        o_s = flash(qs, slc_ref, t, 0, t, lambda jt: sel_scr[jt] > 0.5)
        o_w = flash(qs, win_ref, t, jnp.maximum(t - NSA_WINDOW // tq, 0), t,
                    lambda jt: (t - jt) * tq + ii - jj <= NSA_WINDOW)
        gates = gate_ref[0, pl.ds(r0, tq), :]
        outs = []
        for h in range(NSA_GROUP):
            hr = slice(h * tq, (h + 1) * tq)
            outs.append(gates[:, 3 * h:3 * h + 1] * o_c[hr] + gates[:, 3 * h + 1:3 * h + 2] * o_s[hr]
                        + gates[:, 3 * h + 2:3 * h + 3] * o_w[hr])
        o_ref[pl.ds(r0, tq), :] = jnp.concatenate(outs, axis=1)
        return carry

    lax.fori_loop(0, n_tiles, q_tile, 0)


def _nsa_prompt_tables(rel_bias, seq, n_c):
    tq = NSA_TQ
    n_tiles = seq // tq
    n_cmp = seq // CMP_STRIDE - CMP_BLOCK // CMP_STRIDE + 1
    n_slc = -(-seq // SEL_BLOCK)
    n_a = seq // CMP_STRIDE
    t_ext = jnp.arange(n_a + n_c - 1) - (n_c - 1)
    d_ext = CMP_STRIDE * t_ext[None, :] + jnp.arange(CMP_STRIDE)[:, None] - (CMP_BLOCK - 1)
    bc = _toeplitz(_bias_tile(rel_bias, d_ext, d_ext >= 0), n_a, n_c)
    bc = jnp.where(jnp.arange(n_c) < n_cmp, bc.transpose(0, 2, 1, 3).reshape(NSA_HEADS, seq, n_c), NEG)
    bc = bc.reshape(NSA_KV, NSA_GROUP, seq, n_c)
    d_ext = jnp.arange(n_tiles)[:, None] * tq + jnp.arange(2 * tq - 1)[None, :] - (tq - 1)
    bs = _toeplitz(_bias_tile(rel_bias, d_ext, d_ext >= 0), tq, tq)
    bs = bs.reshape(NSA_KV, NSA_GROUP, n_tiles, tq, tq).transpose(0, 2, 1, 3, 4).reshape(
        NSA_KV, n_tiles, NSA_GROUP * tq, tq)
    c_lo = np.arange(n_c) * CMP_STRIDE
    s_lo = np.arange(LANES) * SEL_BLOCK
    ov = ((c_lo[:, None] <= s_lo[None, :] + SEL_BLOCK - 1) & (c_lo[:, None] + CMP_BLOCK - 1 >= s_lo[None, :])
          & (np.arange(n_c) < n_cmp)[:, None] & (np.arange(LANES) < n_slc)[None, :])
    ex = np.arange(LANES)[:, None] == (np.arange(seq)[None, :] // SEL_BLOCK)
    return bc, bs, jnp.asarray(ov.astype(np.float32), BF16), jnp.asarray(ex.astype(np.float32), BF16), n_slc


def _nsa_prompt(q, gates, kcv, slcg, wing, q_gain, tables, *, batch, seq):
    bc, bs, ov, ex, n_slc = tables
    n_c = kcv.shape[2]
    n_tiles = seq // NSA_TQ
    gw = NSA_GROUP * HEAD_DIM
    rows = NSA_GROUP * NSA_TQ
    per_g = lambda shape: pl.BlockSpec((1,) + shape, lambda g, b: (g,) + (0,) * len(shape),
                                       pipeline_mode=pl.Buffered(1))
    return pl.pallas_call(
        functools.partial(_nsa_prompt_body, seq=seq, n_slc=n_slc, k_sel=min(SEL_TOPK, n_slc)),
        grid=(NSA_KV, batch),
        in_specs=[pl.BlockSpec((seq, gw), lambda g, b: (b, g)),
                  pl.BlockSpec((1, seq, LANES), lambda g, b: (g, b, 0)),
                  pl.BlockSpec((1, 1, n_c, LANES), lambda g, b: (b, g, 0, 0)),
                  pl.BlockSpec((1, seq, LANES), lambda g, b: (g, b, 0)),
                  pl.BlockSpec((1, seq, LANES), lambda g, b: (g, b, 0)),
                  _const((1, gw)), _const((gw, gw)),
                  per_g((NSA_GROUP, seq, n_c)), per_g((n_tiles, rows, NSA_TQ)),
                  _const(ov.shape), _const(ex.shape)],
        out_specs=pl.BlockSpec((seq, gw), lambda g, b: (b, g)),
        out_shape=jax.ShapeDtypeStruct((batch * seq, NSA_KV * gw), F32),
        scratch_shapes=[pltpu.VMEM((n_tiles, NSA_TQ, NSA_TQ), F32), pltpu.VMEM((rows, 1), F32),
                        pltpu.VMEM((rows, 1), F32), pltpu.VMEM((rows, HEAD_DIM), F32)],
        compiler_params=_cparams("arbitrary", "parallel"), name="nsa_prompt",
    )(q, gates, kcv, slcg, wing, jnp.tile(q_gain, NSA_GROUP).reshape(1, gw), _block_diag_ones(gw), bc, bs, ov, ex)


NSA_QC = LANES


def _select_blocks_t(imp, pos, n_slc, k_sel):
    blk = lax.broadcasted_iota(jnp.int32, imp.shape, 0)
    cur = pos >> 6
    forced = (blk == 0) | (blk == cur) | (blk == cur - 1)
    score = jnp.where(forced, FORCE_SCORE, jnp.where(blk <= cur, imp, -1.0))
    rank = jnp.zeros(imp.shape, F32)
    for i in range(n_slc):
        ci = score[i:i + 1, :]
        rank = rank + jnp.where(ci > score, 1.0, jnp.where(ci == score, jnp.where(blk > i, 1.0, 0.0), 0.0))
    return jnp.where(rank < k_sel, 1.0, 0.0)


def _nsa_prompt_t_body(q_ref, gate_ref, kcv_ref, slc_ref, win_ref, gq_ref, bd_ref, eye_ref, fs_ref, fc_ref,
                       ov_ref, ex_ref, o_ref, bc_ref, bs_ref, bw_ref, qt_scr, ks_scr, kw_scr, vs_scr, vw_scr, sel_scr, m_scr, l_scr, acc_scr,
                       *, seq, n_slc, k_sel):
    tq = NSA_TQ
    qc = NSA_QC
    hd = HEAD_DIM
    n_tiles = seq // tq
    width = NSA_GROUP * tq
    n_chunks = width // qc
    eye = eye_ref[...]
    n_c = kcv_ref.shape[2]
    far = NSA_WINDOW // tq

    @pl.when(pl.program_id(1) == 0)
    def _():
        key_i = lax.broadcasted_iota(jnp.int32, (tq, tq), 0)
        qry_i = lax.broadcasted_iota(jnp.int32, (tq, tq), 1)
        blk_i = lax.broadcasted_iota(jnp.int32, (n_c, tq), 0)
        n_cmp = seq // CMP_STRIDE - CMP_BLOCK // CMP_STRIDE + 1
        span = tq + CMP_STRIDE * n_c
        for h in range(NSA_GROUP):
            hs = slice(h * tq, (h + 1) * tq)
            for d in range(n_tiles):
                u = jnp.broadcast_to(fs_ref[0, h:h + 1, d * tq:(d + 2) * tq], (tq, 2 * tq))
                tile = pltpu.roll(u, tq, 1, stride=1, stride_axis=0)[:, :tq]
                bs_ref[d, :, hs] = tile
                if d == far:
                    bw_ref[0, :, hs] = jnp.where(qry_i <= key_i, tile, NEG)
            for t in range(n_tiles):
                u = jnp.broadcast_to(fc_ref[0, h:h + 1, t * tq:t * tq + span], (n_c, span))
                tile = pltpu.roll(u, tq, 1, stride=CMP_STRIDE, stride_axis=0)[:, :tq]
                bc_ref[t, :, hs] = jnp.where(blk_i < n_cmp, tile, NEG)

    qn = (_seg_rms(q_ref[...], bd_ref[...], gq_ref[...]) * (hd ** -0.5)).astype(BF16)
    for t in range(n_tiles):
        rs = slice(t * tq, (t + 1) * tq)
        for h in range(NSA_GROUP):
            qt_scr[t, :, h * tq:(h + 1) * tq] = _dot_nt(eye, qn[rs, h * hd:(h + 1) * hd]).astype(BF16)
        for kv_ref, k_scr, v_scr in ((slc_ref, ks_scr, vs_scr), (win_ref, kw_scr, vw_scr)):
            kv = kv_ref[0, rs, :].astype(BF16)
            k_scr[t] = kv[:, :hd]
            v_scr[t] = _dot_nt(eye, kv[:, hd:]).astype(BF16)
    kc = kcv_ref[0, 0, :, :hd].astype(BF16)
    vct = _dot_nt(eye, kcv_ref[0, 0, :, hd:].astype(BF16)).astype(BF16)
    def flash_init():
        m_scr[...] = jnp.full((1, width), NEG, F32)
        l_scr[...] = jnp.zeros((1, width), F32)
        acc_scr[...] = jnp.zeros((hd, width), F32)

    def flash_step(k_scr, v_scr, t, jt, bias_tile, sel_add):
        s_all = _dot(k_scr[jt], qt_scr[t])
        m_all = m_scr[...]
        l_all = l_scr[...]
        alphas, ls, ms, ps = [], [], [], []
        for c in range(n_chunks):
            cs = slice(c * qc, (c + 1) * qc)
            qoff = (c * qc) % tq
            s = s_all[:, cs] + bias_tile(cs)
            if sel_add is not None:
                s = s + sel_add[:, qoff:qoff + qc]
            m_new = jnp.maximum(m_all[:, cs], jnp.max(s, axis=0, keepdims=True))
            alpha = jnp.exp(m_all[:, cs] - m_new)
            p = jnp.exp(s - m_new)
            ls.append(alpha * l_all[:, cs] + jnp.sum(p, axis=0, keepdims=True))
            ps.append(p.astype(BF16))
            alphas.append(alpha)
            ms.append(m_new)
        cat = lambda xs: jnp.concatenate(xs, axis=1)
        acc_scr[...] = cat(alphas) * acc_scr[...] + _dot(v_scr[jt], cat(ps))
        m_scr[...] = cat(ms)
        l_scr[...] = cat(ls)

    def flash_result():
        return acc_scr[...] / jnp.maximum(l_scr[...], 1e-30)

    def q_tile(t, carry):
        bias_c = bc_ref[t]
        s_c = _dot(kc, qt_scr[t]) + bias_c
        m_c = jnp.max(s_c, axis=0, keepdims=True)
        e_c = jnp.where(bias_c > NEG_TEST, jnp.exp(s_c - m_c), 0.0)
        p_c = (e_c / jnp.maximum(jnp.sum(e_c, axis=0, keepdims=True), 1e-30)).astype(BF16)
        o_c = _dot(vct, p_c)
        imp = sum(_dot(ov_ref[...], p_c[:, h * tq:(h + 1) * tq]) for h in range(NSA_GROUP))
        pos = t * tq + lax.broadcasted_iota(jnp.int32, (1, tq), 1)
        sel = _select_blocks_t(imp[:n_slc], pos, n_slc, k_sel)
        sel_scr[...] = jnp.zeros(sel_scr.shape, BF16)
        sel_scr[0:n_slc, :] = ((1.0 - sel) * NEG).astype(BF16)
        flash_init()

        def slc_step(jt, carry):
            flash_step(ks_scr, vs_scr, t, jt, lambda cs: bs_ref[t - jt, :, cs], _dot(ex_ref[jt], sel_scr[...]))
            return carry

        lax.fori_loop(0, t + 1, slc_step, 0)
        o_s = flash_result()
        flash_init()
        far = NSA_WINDOW // tq
        for delta in range(far, -1, -1):
            tile_of = (lambda cs: bw_ref[0, :, cs]) if delta == far else (lambda cs, d=delta: bs_ref[d, :, cs])
            pl.when(t >= delta)(functools.partial(flash_step, kw_scr, vw_scr, t, t - delta, tile_of, None))
        o_w = flash_result()
        gates = gate_ref[0, t]
        for h in range(NSA_GROUP):
            hs = slice(h * tq, (h + 1) * tq)
            o_ref[t, h * hd:(h + 1) * hd, :] = (gates[3 * h:3 * h + 1] * o_c[:, hs]
                                                + gates[3 * h + 1:3 * h + 2] * o_s[:, hs]
                                                + gates[3 * h + 2:3 * h + 3] * o_w[:, hs])
        return carry

    lax.fori_loop(0, n_tiles, q_tile, 0)


def _nsa_prompt_t_tables(rel_bias, seq, n_c):
    tq = NSA_TQ
    n_tiles = seq // tq
    n_slc = -(-seq // SEL_BLOCK)
    ds = jnp.arange((n_tiles + 1) * tq) - tq
    fs = _bias_tile(rel_bias, ds, ds >= 0).reshape(NSA_KV, NSA_GROUP, -1)
    dc = jnp.arange(n_tiles * tq + CMP_STRIDE * n_c) - (CMP_STRIDE * n_c + CMP_BLOCK - 1)
    fc = _bias_tile(rel_bias, dc, dc >= 0).reshape(NSA_KV, NSA_GROUP, -1)
    n_cmp = seq // CMP_STRIDE - CMP_BLOCK // CMP_STRIDE + 1
    c_lo = np.arange(n_c) * CMP_STRIDE
    s_lo = np.arange(LANES) * SEL_BLOCK
    ov = ((c_lo[None, :] <= s_lo[:, None] + SEL_BLOCK - 1) & (c_lo[None, :] + CMP_BLOCK - 1 >= s_lo[:, None])
          & (np.arange(n_c) < n_cmp)[None, :] & (np.arange(LANES) < n_slc)[:, None])
    ex = (np.arange(seq)[:, None] // SEL_BLOCK == np.arange(LANES)[None, :]).reshape(n_tiles, tq, LANES)
    as_bf = lambda a: jnp.asarray(a.astype(np.float32), BF16)
    return fs, fc, as_bf(ov), as_bf(ex), n_slc


def _nsa_prompt_t(q, gates_t, kcv, slcg, wing, q_gain, tables, *, batch, seq):
    fs, fc, ov, ex, n_slc = tables
    n_c = kcv.shape[2]
    tq = NSA_TQ
    n_tiles = seq // tq
    assert n_tiles > NSA_WINDOW // tq
    gw = NSA_GROUP * HEAD_DIM
    width = NSA_GROUP * tq
    per_g = lambda shape: pl.BlockSpec((1,) + shape, lambda g, b: (g,) + (0,) * len(shape),
                                       pipeline_mode=pl.Buffered(1))
    eye = jnp.eye(HEAD_DIM, dtype=BF16)
    return pl.pallas_call(
        functools.partial(_nsa_prompt_t_body, seq=seq, n_slc=n_slc, k_sel=min(SEL_TOPK, n_slc)),
        grid=(NSA_KV, batch),
        in_specs=[pl.BlockSpec((seq, gw), lambda g, b: (b, g)),
                  pl.BlockSpec((1, n_tiles, 16, tq), lambda g, b: (g, b, 0, 0)),
                  pl.BlockSpec((1, 1, n_c, LANES), lambda g, b: (b, g, 0, 0)),
                  pl.BlockSpec((1, seq, LANES), lambda g, b: (g, b, 0)),
                  pl.BlockSpec((1, seq, LANES), lambda g, b: (g, b, 0)),
                  _const((1, gw)), _const((gw, gw)), _const(eye.shape),
                  per_g(fs.shape[1:]), per_g(fc.shape[1:]), _const(ov.shape), _const(ex.shape)],
        out_specs=pl.BlockSpec((n_tiles, gw, tq), lambda g, b: (b, g, 0)),
        out_shape=jax.ShapeDtypeStruct((batch * n_tiles, NSA_KV * gw, tq), F32),
        scratch_shapes=[pltpu.VMEM((n_tiles, n_c, width), F32), pltpu.VMEM((n_tiles, tq, width), F32),
                        pltpu.VMEM((1, tq, width), F32), pltpu.VMEM((n_tiles, HEAD_DIM, width), BF16),
                        pltpu.VMEM((n_tiles, tq, HEAD_DIM), BF16), pltpu.VMEM((n_tiles, tq, HEAD_DIM), BF16),
                        pltpu.VMEM((n_tiles, HEAD_DIM, tq), BF16), pltpu.VMEM((n_tiles, HEAD_DIM, tq), BF16),
                        pltpu.VMEM((LANES, tq), BF16), pltpu.VMEM((1, width), F32), pltpu.VMEM((1, width), F32),
                        pltpu.VMEM((HEAD_DIM, width), F32)],
        compiler_params=_cparams("arbitrary", "arbitrary"), name="nsa_prompt",
    )(q, gates_t, kcv, slcg, wing, jnp.tile(q_gain, NSA_GROUP).reshape(1, gw), _block_diag_ones(gw), eye,
      fs, fc, ov, ex)


SMP_PP = 8
NSA_KW = NSA_KV * HEAD_DIM
SEL_LANES = 2 * LANES


def _nsa_sample_body(pt_ref, q_ref, gate_ref, ckv_ref, *refs, n_slc, k_sel, past, tq):
    pages = refs[:SMP_PP]
    (slcn_ref, wst_ref, winn_ref, gq_ref, bd_ref, bc_ref, bs_ref, bsn_ref, bw_ref, bwn_ref, ov_ref, ex_ref, exn_ref,
     o_ref, nwin_ref, qbd_scr, sel_scr, kbuf_scr, vbuf_scr, s_scr, sn_scr, vn_scr, oc_scr, ow_scr,
     pad_scr) = refs[SMP_PP:]
    j = pl.program_id(1)
    hd = HEAD_DIM
    kw = NSA_KW
    rows = NSA_HEADS * tq
    init = (jnp.full((rows, 1), NEG, F32), jnp.zeros((rows, 1), F32), jnp.zeros((rows, kw), F32))

    @pl.when(j == 0)
    def _():
        qn = _seg_rms(q_ref[...], bd_ref[...], gq_ref[...]) * (hd ** -0.5)
        zero = jnp.zeros((tq, hd), F32)
        blocks = []
        for h in range(NSA_HEADS):
            qh = qn[:, hd * h:hd * (h + 1)]
            blocks.append(jnp.concatenate([qh if g == h // NSA_GROUP else zero for g in range(NSA_KV)], axis=1))
        qbd = jnp.concatenate(blocks, axis=0).astype(BF16)
        qbd_scr[...] = qbd
        ckv = ckv_ref[0]
        bias_c = bc_ref[...]
        ok_c = bias_c > NEG_TEST
        s_c = _dot_nt(qbd, ckv[:, :kw].astype(BF16)) + bias_c
        m_c = jnp.max(s_c, axis=-1, keepdims=True)
        e_c = jnp.where(ok_c, jnp.exp(s_c - m_c), 0.0)
        p_c = (e_c / jnp.maximum(jnp.sum(e_c, axis=-1, keepdims=True), 1e-30)).astype(BF16)
        oc_scr[...] = _dot(p_c, ckv[:, kw:].astype(BF16))
        d = _dot(p_c, ov_ref[...])
        gr = NSA_GROUP * tq
        imp = jnp.concatenate(
            [sum(d[g * gr + h * tq:g * gr + (h + 1) * tq] for h in range(NSA_GROUP)) for g in range(NSA_KV)], axis=0)
        pos = past + (lax.broadcasted_iota(jnp.int32, (NSA_KV * tq, 1), 0) & (tq - 1))
        sel = _select_blocks(imp, pos, n_slc, k_sel)
        sel = jnp.concatenate([sel[g * tq:(g + 1) * tq] for g in range(NSA_KV) for _ in range(NSA_GROUP)], axis=0)
        sel = ((1.0 - sel) * NEG).astype(BF16)
        sel_scr[...] = sel
        pad_scr[...] = jnp.zeros(pad_scr.shape, F32)
        pad_scr[0:tq, :] = slcn_ref[...]
        kn = pad_scr[...].astype(BF16)
        sn_scr[...] = _dot_nt(qbd, kn[:, :kw]) + bsn_ref[...] + _dot(sel, exn_ref[...])
        vn_scr[...] = kn[:, kw:]
        pad_scr[0:tq, :] = winn_ref[...]
        nwin_ref[0] = _shifted_buffer(wst_ref[0], pad_scr[...], tq)
        wst = wst_ref[0].astype(BF16)
        wn = pad_scr[...].astype(BF16)
        bias_w = bw_ref[...]
        mw = _softmax_piece(_dot(qbd, wst[:kw]) + bias_w, bias_w > NEG_TEST, *init, wst[kw:], v_t=True)
        bias_wn = bwn_ref[...]
        mw = _softmax_piece(_dot_nt(qbd, wn[:, :kw]) + bias_wn, bias_wn > NEG_TEST, *mw, wn[:, kw:])
        ow_scr[...] = mw[2] / jnp.maximum(mw[1], 1e-30)

    for i in range(SMP_PP):
        pg = pages[i][0].astype(BF16)
        kbuf_scr[:, i * PAGE:(i + 1) * PAGE] = pg[:kw]
        vbuf_scr[j, :, i * PAGE:(i + 1) * PAGE] = pg[kw:]
    s_scr[j] = _dot(qbd_scr[...], kbuf_scr[...]) + bs_ref[j] + _dot(sel_scr[...], ex_ref[j])

    @pl.when(j == pl.num_programs(1) - 1)
    def _():
        s_all = s_scr[...]
        s_new = sn_scr[...]
        m = jnp.maximum(jnp.max(jnp.max(s_all, axis=0), axis=-1, keepdims=True),
                        jnp.max(s_new, axis=-1, keepdims=True))
        p_new = jnp.exp(s_new - m)
        den = jnp.sum(p_new, axis=-1, keepdims=True)
        acc = _dot(p_new.astype(BF16), vn_scr[...])
        for st in range(s_all.shape[0]):
            p = jnp.exp(s_all[st] - m)
            den = den + jnp.sum(p, axis=-1, keepdims=True)
            acc = acc + _dot_nt(p.astype(BF16), vbuf_scr[st])
        o_s = acc / jnp.maximum(den, 1e-30)
        o_c = oc_scr[...]
        o_w = ow_scr[...]
        outs = []
        for h in range(NSA_HEADS):
            g, hg = divmod(h, NSA_GROUP)
            gt = gate_ref[g]
            rs = slice(h * tq, (h + 1) * tq)
            ls = slice(hd * g, hd * (g + 1))
            outs.append(gt[:, 3 * hg:3 * hg + 1] * o_c[rs, ls] + gt[:, 3 * hg + 1:3 * hg + 2] * o_s[rs, ls]
                        + gt[:, 3 * hg + 2:3 * hg + 3] * o_w[rs, ls])
        o_ref[...] = jnp.concatenate(outs, axis=1)


def _nsa_sample_tables(rel_bias, past, tq, n_c, w_state):
    rows = NSA_HEADS * tq
    total = past + tq
    n_cmp = total // CMP_STRIDE - CMP_BLOCK // CMP_STRIDE + 1
    n_slc = -(-total // SEL_BLOCK)
    pos = past + jnp.arange(tq)
    c_hi = jnp.arange(n_c) * CMP_STRIDE + CMP_BLOCK - 1
    dist_c = pos[:, None] - c_hi[None, :]
    bc = _bias_tile(rel_bias, dist_c, (dist_c >= 0) & (jnp.arange(n_c) < n_cmp)[None, :]).reshape(rows, n_c)
    n_steps = past // (SMP_PP * PAGE)
    d_ext = jnp.arange(tq + past - 1) + 1
    bs = _toeplitz(_bias_tile(rel_bias, d_ext, d_ext >= 0), tq, past)
    bs = bs.reshape(rows, n_steps, SMP_PP * PAGE).transpose(1, 0, 2)
    jn = jnp.arange(PAGE)
    dist_n = jnp.arange(tq)[:, None] - jn[None, :]
    new_ok = (dist_n >= 0) & (jn < tq)[None, :]
    bsn = _bias_tile(rel_bias, dist_n, new_ok).reshape(rows, PAGE)
    dist_w = jnp.arange(tq)[:, None] + w_state - jnp.arange(w_state)[None, :]
    bw = _bias_tile(rel_bias, dist_w, (dist_w >= 0) & (dist_w <= NSA_WINDOW)).reshape(rows, w_state)
    bwn = _bias_tile(rel_bias, dist_n, new_ok & (dist_n <= NSA_WINDOW)).reshape(rows, PAGE)
    c_lo = np.arange(n_c) * CMP_STRIDE
    s_lo = np.arange(SEL_LANES) * SEL_BLOCK
    ov = ((c_lo[:, None] <= s_lo[None, :] + SEL_BLOCK - 1) & (c_lo[:, None] + CMP_BLOCK - 1 >= s_lo[None, :])
          & (np.arange(n_c) < n_cmp)[:, None] & (np.arange(SEL_LANES) < n_slc)[None, :])
    blk = np.arange(SEL_LANES)
    ex = (blk[:, None] == (np.arange(past)[None, :] // SEL_BLOCK)).reshape(SEL_LANES, n_steps, SMP_PP * PAGE)
    ex = ex.transpose(1, 0, 2)
    exn = (blk[:, None] == ((past + np.arange(PAGE))[None, :] // SEL_BLOCK)) & (np.arange(PAGE) < tq)[None, :]
    as_bf = lambda a: jnp.asarray(a.astype(np.float32), BF16)
    return bc, bs, bsn, bw, bwn, as_bf(ov), as_bf(ex), as_bf(exn), n_slc


def _nsa_sample(q, gates, crows, slc_pool_t, table, slc_new, win_state_t, win_new, q_gain, tables, *, tq):
    bc, bs, bsn, bw, bwn, ov, ex, exn, n_slc = tables
    bsz, npg = table.shape
    past = npg * PAGE
    n_c = crows.shape[1]
    w_state = win_state_t.shape[2]
    rows = NSA_HEADS * tq
    qw = NSA_HEADS * HEAD_DIM
    page = lambda i: pl.BlockSpec((1, ROWS_NSA, PAGE), lambda b, j, pt: (pt[b, j * SMP_PP + i], 0, 0))
    per_b = lambda shape: pl.BlockSpec((1,) + shape, lambda b, j, pt: (b,) + (0,) * len(shape))
    new_rows = pl.BlockSpec((tq, ROWS_NSA), lambda b, j, pt: (b, 0))
    return pl.pallas_call(
        functools.partial(_nsa_sample_body, n_slc=n_slc, k_sel=min(SEL_TOPK, n_slc), past=past, tq=tq),
        grid_spec=pltpu.PrefetchScalarGridSpec(
            num_scalar_prefetch=1, grid=(bsz, npg // SMP_PP),
            in_specs=[pl.BlockSpec((tq, qw), lambda b, j, pt: (b, 0)),
                      pl.BlockSpec((NSA_KV, tq, LANES), lambda b, j, pt: (0, b, 0)),
                      per_b((n_c, ROWS_NSA))] + [page(i) for i in range(SMP_PP)] + [
                      new_rows, per_b((ROWS_NSA, w_state)), new_rows,
                      _const((1, qw)), _const((qw, qw)), _const(bc.shape), _const(bs.shape), _const(bsn.shape),
                      _const(bw.shape), _const(bwn.shape), _const(ov.shape), _const(ex.shape), _const(exn.shape)],
            out_specs=[pl.BlockSpec((tq, qw), lambda b, j, pt: (b, 0)), per_b((ROWS_NSA, w_state))],
            scratch_shapes=[pltpu.VMEM((rows, NSA_KW), BF16), pltpu.VMEM((rows, SEL_LANES), BF16),
                            pltpu.VMEM((NSA_KW, SMP_PP * PAGE), BF16),
                            pltpu.VMEM((npg // SMP_PP, NSA_KW, SMP_PP * PAGE), BF16),
                            pltpu.VMEM((npg // SMP_PP, rows, SMP_PP * PAGE), F32),
                            pltpu.VMEM((rows, PAGE), F32), pltpu.VMEM((PAGE, NSA_KW), BF16),
                            pltpu.VMEM((rows, NSA_KW), F32), pltpu.VMEM((rows, NSA_KW), F32),
                            pltpu.VMEM((PAGE, ROWS_NSA), F32)]),
        out_shape=[jax.ShapeDtypeStruct((bsz * tq, qw), F32), jax.ShapeDtypeStruct(win_state_t.shape, F32)],
        compiler_params=_cparams("parallel", "arbitrary"), name="nsa_sample",
    )(table, q, gates, crows, *([slc_pool_t] * SMP_PP), slc_new, win_state_t, win_new,
      jnp.tile(q_gain, NSA_HEADS).reshape(1, qw), _block_diag_ones(qw), bc, bs, bsn, bw, bwn, ov, ex, exn)


DIL_TQ = 128


def _head_of_lane(shape):
    return lax.broadcasted_iota(jnp.int32, shape, 1) >> 6


def _stack_heads(qt, lane_head):
    return jnp.concatenate([jnp.where(lane_head == h, qt, 0.0) for h in range(DIL_HEADS)], axis=0).astype(BF16)


def _unstack_heads(r, lane_head, t):
    return sum(jnp.where(lane_head == h, r[h * t:(h + 1) * t], 0.0) for h in range(DIL_HEADS))


def _dil_prompt_body(q_ref, kv_ref, bias_ref, o_ref, lse_ref, *, n_sub, dil):
    tq = DIL_TQ
    gw = ODD_GW
    lane_head = _head_of_lane((tq, gw))
    for r in range(dil):
        ql = slice(gw * r, gw * (r + 1))
        kl = slice(2 * gw * r, 2 * gw * r + gw)
        vl = slice(2 * gw * r + gw, 2 * gw * (r + 1))

        def tile(t, carry, ql=ql, kl=kl, vl=vl):
            start = pl.multiple_of(t * tq, tq)
            prev = pl.multiple_of(jnp.maximum(t - 1, 0) * tq, tq)
            cur_rows = pl.ds(start, tq)
            prev_rows = pl.ds(prev, tq)
            kk = jnp.concatenate([kv_ref[prev_rows, kl], kv_ref[cur_rows, kl]], axis=0).astype(BF16)
            vv = jnp.concatenate([kv_ref[prev_rows, vl], kv_ref[cur_rows, vl]], axis=0).astype(BF16)
            bias = bias_ref[jnp.minimum(t, 1)]
            s = _dot_nt(_stack_heads(q_ref[cur_rows, ql], lane_head), kk) + bias
            m = jnp.max(s, axis=-1, keepdims=True)
            e = jnp.where(bias > NEG_TEST, jnp.exp(s - m), 0.0)
            l = jnp.maximum(jnp.sum(e, axis=-1, keepdims=True), 1e-30)
            res = _dot((e / l).astype(BF16), vv)
            lse = jnp.broadcast_to(m + jnp.log(l), (DIL_HEADS * tq, gw))
            o_ref[cur_rows, ql] = _unstack_heads(res, lane_head, tq)
            lse_ref[cur_rows, ql] = _unstack_heads(lse, lane_head, tq)
            return carry

        lax.fori_loop(0, n_sub // tq, tile, 0)


def _merge_groups(outs, lses):
    mx = functools.reduce(jnp.maximum, lses)
    ws = [jnp.exp(l - mx) for l in lses]
    return sum(w * o for w, o in zip(ws, outs)) / sum(ws)


def _dil_prompt_tables(rel_bias):
    tq = DIL_TQ
    m_ext = jnp.arange(3 * tq - 1) - (tq - 1)
    cur_tile = jnp.arange(2 * tq) >= tq
    tiles = []
    for gi, (win, dil) in enumerate(DIL_PATTERN):
        assert win // dil == tq
        t = _toeplitz(_bias_tile(rel_bias, m_ext * dil, (m_ext >= 0) & (m_ext <= tq)), tq, 2 * tq)
        t = t[gi * DIL_HEADS:(gi + 1) * DIL_HEADS].reshape(DIL_HEADS * tq, 2 * tq)
        tiles.append(jnp.stack([jnp.where(cur_tile, t, NEG), t]))
    return tiles


def _dil_prompt(q_views, kv_views, tables, *, batch, seq):
    gw = ODD_GW
    outs, lses = [], []
    for gi, (_, dil) in enumerate(DIL_PATTERN):
        n_sub = seq // dil
        assert n_sub % DIL_TQ == 0
        qv, kvv = q_views[gi], kv_views[gi]
        o_spec = pl.BlockSpec((n_sub, dil * gw), lambda b: (b, 0))
        o, lse = pl.pallas_call(
            functools.partial(_dil_prompt_body, n_sub=n_sub, dil=dil), grid=(batch,),
            in_specs=[o_spec, pl.BlockSpec((n_sub, dil * 2 * gw), lambda b: (b, 0)), _const(tables[gi].shape)],
            out_specs=[o_spec, o_spec],
            out_shape=[jax.ShapeDtypeStruct(qv.shape, F32)] * 2,
            compiler_params=_cparams("parallel"), name=f"dilated_prompt_g{gi}",
        )(qv, kvv, tables[gi])
        outs.append(o)
        lses.append(lse)
    return outs, lses


def _shifted_buffer(st, new_rows_padded, tq):
    ln = st.shape[1]
    shifted = pltpu.roll(st, ln - tq, 1)
    new_t = pltpu.roll(new_rows_padded.T, LANES - tq, 1)
    lane = lax.broadcasted_iota(jnp.int32, (st.shape[0], LANES), 1)
    tail = jnp.where(lane >= LANES - tq, new_t, shifted[:, ln - LANES:])
    return tail if ln == LANES else jnp.concatenate([shifted[:, :ln - LANES], tail], axis=1)


def _dil_sample_body(q_ref, kvn_ref, st0_ref, st1_ref, st2_ref, b0_ref, b1_ref, b2_ref, bn_ref, o_ref, lse_ref,
                     n0_ref, n1_ref, n2_ref, pad_scr, *, tq):
    gw = ODD_GW
    lane_head = _head_of_lane((tq, gw))
    rows = DIL_HEADS * tq
    pad_scr[...] = jnp.zeros(pad_scr.shape, F32)
    for gi, (st_ref, b_ref, new_ref) in enumerate(((st0_ref, b0_ref, n0_ref), (st1_ref, b1_ref, n1_ref),
                                                    (st2_ref, b2_ref, n2_ref))):
        qs = _stack_heads(q_ref[gi], lane_head)
        pad_scr[0:tq, :] = kvn_ref[gi]
        new_ref[0] = _shifted_buffer(st_ref[0], pad_scr[...], tq)
        st = st_ref[0].astype(BF16)
        kn = pad_scr[...].astype(BF16)
        bias = b_ref[...]
        acc = (jnp.full((rows, 1), NEG, F32), jnp.zeros((rows, 1), F32), jnp.zeros((rows, gw), F32))
        acc = _softmax_piece(_dot(qs, st[:gw]) + bias, bias > NEG_TEST, *acc, st[gw:], v_t=True)
        bias_n = bn_ref[gi]
        m, l, a = _softmax_piece(_dot_nt(qs, kn[:, :gw]) + bias_n, bias_n > NEG_TEST, *acc, kn[:, gw:])
        l = jnp.maximum(l, 1e-30)
        o_ref[gi] = _unstack_heads(a / l, lane_head, tq)
        lse_ref[gi] = _unstack_heads(jnp.broadcast_to(m + jnp.log(l), (rows, gw)), lane_head, tq)


def _dil_sample_tables(rel_bias, tq, state_lens):
    rows = DIL_HEADS * tq
    i = jnp.arange(tq)[:, None]
    per_state, per_new = [], []
    jn = jnp.arange(PAGE)[None, :]
    for gi, ((win, dil), ln) in enumerate(zip(DIL_PATTERN, state_lens)):
        hs = slice(gi * DIL_HEADS, (gi + 1) * DIL_HEADS)
        d = i + ln - jnp.arange(ln)[None, :]
        per_state.append(_bias_tile(rel_bias, d, (d % dil == 0) & (d <= win))[hs].reshape(rows, ln))
        dn = i - jn
        per_new.append(_bias_tile(rel_bias, dn, (dn >= 0) & (dn % dil == 0) & (dn <= win) & (jn < tq))[hs]
                       .reshape(rows, PAGE))
    return per_state, jnp.stack(per_new)


def _dil_sample(q3, kv3, states, tables, *, tq):
    per_state, bias_new = tables
    bsz = states[0].shape[0]
    gw = ODD_GW
    st_spec = lambda s: pl.BlockSpec((1,) + s.shape[1:], lambda b: (b, 0, 0))
    grp = lambda w: pl.BlockSpec((3, tq, w), lambda b: (0, b, 0))
    return pl.pallas_call(
        functools.partial(_dil_sample_body, tq=tq), grid=(bsz,),
        in_specs=[grp(gw), grp(2 * gw)] + [st_spec(s) for s in states] + [_const(t.shape) for t in per_state]
        + [_const(bias_new.shape)],
        out_specs=[grp(gw), grp(gw)] + [st_spec(s) for s in states],
        out_shape=[jax.ShapeDtypeStruct((3, bsz * tq, gw), F32)] * 2
        + [jax.ShapeDtypeStruct(s.shape, F32) for s in states],
        scratch_shapes=[pltpu.VMEM((PAGE, 2 * gw), F32)],
        compiler_params=_cparams("parallel"), name="dilated_sample",
    )(q3, kv3, *states, *per_state, bias_new)


def _pad_even_w(w_in):
    return jnp.concatenate([w_in, jnp.zeros((w_in.shape[0], EVEN_PAD - w_in.shape[1]), w_in.dtype)], axis=1).astype(BF16)


def _even_mixer_prompt(h, gain, w_in, w_out, lb, hg_norm, qk_norm, cmp_pe, cmp_w1, cmp_w2, rel_bias, *, batch, seq, tm):
    n = batch * seq
    hg, q, cmp_rows, slc_rows, win_rows, slcg, wing, _, gates_t = _inproj_even(
        h, gain, _pad_even_w(w_in), qk_norm, tm=tm, gates_t=True)
    s0 = jnp.zeros((batch, HG_HEADS, HEAD_DIM, HEAD_DIM), F32)
    o_h, s_fin = _hgrn(hg, lb, hg_norm, s0, batch=batch, seq=seq)
    cw = _compress_weights(cmp_pe, cmp_w1, cmp_w2, qk_norm[1])
    npg = seq // PAGE
    table = jnp.arange(batch * npg, dtype=jnp.int32).reshape(batch, npg)
    _, kcv = _compress(cmp_rows.reshape(n // PAGE, PAGE, ROWS_NSA), table, cw, transposed=False)
    tables = _nsa_prompt_t_tables(rel_bias, seq, kcv.shape[2])
    o_nt = _nsa_prompt_t(q, gates_t, kcv, slcg, wing, qk_norm[0], tables, batch=batch, seq=seq)
    wo = w_out.astype(BF16)
    hw = HG_HEADS * HEAD_DIM
    h = _outproj2t(h, o_h, o_nt, wo[:hw], wo[hw:], tm=tm)
    return h, {"hgrn": s_fin, "cmp": cmp_rows, "slc": slc_rows, "win": win_rows}


def _even_mixer_sample(h, gain, w_in, w_out, lb, hg_norm, qk_norm, cmp_pe, cmp_w1, cmp_w2, rel_bias,
                       state_hgrn, cmp_pool, slc_pool, win_state, page_table, *, batch, tq, tm):
    assert tq < CMP_STRIDE and tq & (tq - 1) == 0
    hg, q, cmp_rows, slc_rows, win_rows, _, _, gates = _inproj_even(h, gain, _pad_even_w(w_in), qk_norm, tm=tm)
    o_h, s_fin = _hgrn(hg, lb, hg_norm, state_hgrn, batch=batch, seq=tq)
    cw = _compress_weights(cmp_pe, cmp_w1, cmp_w2, qk_norm[1])
    rows_t = lambda a: jnp.moveaxis(a, 1, -1).reshape(a.shape[0], ROWS_NSA, a.shape[1])
    crows, _ = _compress(rows_t(cmp_pool), page_table, cw, transposed=True)
    past = page_table.shape[1] * PAGE
    w_state = win_state.shape[1]
    tables = _nsa_sample_tables(rel_bias, past, tq, crows.shape[1], w_state)
    o_n, new_win_t = _nsa_sample(q, gates, crows, rows_t(slc_pool), page_table, slc_rows, rows_t(win_state), win_rows,
                                 qk_norm[0], tables, tq=tq)
    wo = w_out.astype(BF16)
    hw = HG_HEADS * HEAD_DIM
    h = _outproj2(h, o_h, o_n, wo[:hw], wo[hw:], tm=tm)
    new_win = jnp.moveaxis(new_win_t.reshape((batch,) + win_state.shape[2:] + (w_state,)), -1, 1)
    return h, {"hgrn": s_fin, "cmp": cmp_rows, "slc": slc_rows, "win": new_win}


def _odd_mixer_prompt(h, gain, w_in, w_out, qk_norm, rel_bias, *, batch, seq, tm):
    q3, kv3, qv1, kvv1, qv2, kvv2 = _inproj_odd(h, gain, w_in.astype(BF16), qk_norm, tm=tm, views=True)
    outs, lses = _dil_prompt([q3[0], qv1, qv2], [kv3[0], kvv1, kvv2], _dil_prompt_tables(rel_bias),
                             batch=batch, seq=seq)
    return _outproj_merge(h, outs, lses, w_out.astype(BF16), tm=tm, views=True), kv3


def _odd_mixer_sample(h, gain, w_in, w_out, qk_norm, rel_bias, states, *, batch, tq, tm):
    q3, kv3 = _inproj_odd(h, gain, w_in.astype(BF16), qk_norm, tm=tm)
    tables = _dil_sample_tables(rel_bias, tq, [s.shape[1] for s in states])
    states_t = [jnp.moveaxis(s, 1, -1).reshape(batch, 2 * ODD_GW, s.shape[1]) for s in states]
    o3, lse3, *new_t = _dil_sample(q3, kv3, states_t, tables, tq=tq)
    new = [jnp.moveaxis(n.reshape(s.shape[:1] + s.shape[2:] + s.shape[1:2]), -1, 1) for n, s in zip(new_t, states)]
    h = _outproj_merge(h, [o3[g] for g in range(3)], [lse3[g] for g in range(3)], w_out.astype(BF16), tm=tm)
    return h, new


TM_PROMPT = 512
TM_SAMPLE = 256


def kernel(x_prompt, x_sample, p_prompt, p_sample, state_hgrn, cache_nsa_cmp_kv, cache_nsa_slc_kv, state_nsa_win_kv,
           state_dil_kv_0, state_dil_kv_1, state_dil_kv_2, page_table, rel_bias, norm_ffn1, norm_mix, norm_ffn2,
           norm_ple, w_ffn1_in, w_ffn1_out, w_ffn2_in, w_ffn2_out, w_ple_gate, w_ple_proj, w_in_even, w_out_even,
           hgrn_lb_logits, hgrn_norm, nsa_qk_norm, nsa_cmp_pe, nsa_cmp_w1, nsa_cmp_w2, w_in_odd, w_out_odd,
           dil_qk_norm):
    depth = norm_ffn1.shape[0]
    dil_states = (state_dil_kv_0, state_dil_kv_1, state_dil_kv_2)
    bf = lambda w: w.astype(BF16)
    w1i, w1o, w2i, w2o, wpg, wpp = (bf(w) for w in (w_ffn1_in, w_ffn1_out, w_ffn2_in, w_ffn2_out, w_ple_gate,
                                                     w_ple_proj))
    lb_all = jnp.cumsum(jax.nn.softmax(hgrn_lb_logits.astype(F32), axis=0), axis=0)

    def run(x, p, sample):
        batch, seq, d = x.shape
        n = batch * seq
        tm = TM_SAMPLE if sample else TM_PROMPT
        h = x.reshape(n, d)
        p = p.reshape(depth, n, p.shape[-1])
        even, odd = [], []
        for li in range(depth):
            h = _ffn(h, norm_ffn1[li], w1i[li], w1o[li], tm=tm)
            if li % 2 == 0:
                ei = li // 2
                args = (h, norm_mix[li], w_in_even[ei], w_out_even[ei], lb_all[ei], hgrn_norm[ei], nsa_qk_norm[ei],
                        nsa_cmp_pe[ei], nsa_cmp_w1[ei], nsa_cmp_w2[ei], rel_bias)
                if sample:
                    h, st = _even_mixer_sample(*args, state_hgrn[ei], cache_nsa_cmp_kv[ei], cache_nsa_slc_kv[ei],
                                               state_nsa_win_kv[ei], page_table, batch=batch, tq=seq, tm=tm)
                else:
                    h, st = _even_mixer_prompt(*args, batch=batch, seq=seq, tm=tm)
                    st["win"] = st["win"].reshape(batch, seq, ROWS_NSA)[:, -min(NSA_WINDOW, seq):]
                even.append(st)
            else:
                oi = li // 2
                args = (h, norm_mix[li], w_in_odd[oi], w_out_odd[oi], dil_qk_norm[oi], rel_bias)
                if sample:
                    h, bufs = _odd_mixer_sample(*args, [s[oi] for s in dil_states], batch=batch, tq=seq, tm=tm)
                else:
                    h, kv3 = _odd_mixer_prompt(*args, batch=batch, seq=seq, tm=tm)
                    bufs = [kv3[g].reshape(batch, seq, 2 * ODD_GW)[:, -min(w, seq):]
                            for g, (w, _) in enumerate(DIL_PATTERN)]
                odd.append(bufs)
            h = _ffn(h, norm_ffn2[li], w2i[li], w2o[li], ple=(p[li], norm_ple[li], wpg[li], wpp[li]), tm=tm)
        rows = lambda a: a.reshape(batch, -1, 2, NSA_KV, HEAD_DIM)
        drows = lambda a: a.reshape(batch, -1, 2, DIL_HEADS, HEAD_DIM)
        return (h.reshape(batch, seq, d),
                jnp.stack([s["hgrn"] for s in even]), jnp.stack([rows(s["cmp"]) for s in even]),
                jnp.stack([rows(s["slc"]) for s in even]), jnp.stack([rows(s["win"]) for s in even]),
                jnp.stack([drows(b[0]) for b in odd]), jnp.stack([drows(b[1]) for b in odd]),
                jnp.stack([drows(b[2]) for b in odd]))

    y_p, hg_p, cmp_p, slc_p, win_p, d0_p, d1_p, d2_p = run(x_prompt, p_prompt, False)
    y_s, hg_s, cmp_s, slc_s, win_s, d0_s, d1_s, d2_s = run(x_sample, p_sample, True)
    return (y_p, y_s, hg_p, hg_s, cmp_p, cmp_s, slc_p, slc_s, win_p, win_s, d0_p, d0_s, d1_p, d1_s, d2_p, d2_s)
```

```python
import functools
import math

import numpy as np
import jax
import jax.numpy as jnp
from jax import lax
from jax.experimental import pallas as pl
from jax.experimental.pallas import tpu as pltpu

F32 = jnp.float32
BF16 = jnp.bfloat16
HIGHEST = lax.Precision.HIGHEST

V7X_VMEM_BYTES = 64 * 1024 * 1024
VMEM_LIMIT = V7X_VMEM_BYTES * 7 // 8
LANES = 128

EPS = 1e-6
NEG = -1e30
NEG_TEST = -1e29

HEAD_DIM = 64
HG_HEADS = 8
HG_CHUNK = 64
NSA_HEADS = 12
NSA_KV = 3
NSA_GROUP = NSA_HEADS // NSA_KV
CMP_BLOCK = 32
CMP_STRIDE = 16
SEL_BLOCK = 64
SEL_TOPK = 16
NSA_WINDOW = 512
FORCE_SCORE = 1e4
PAGE = 128
DIL_PATTERN = ((128, 1), (512, 4), (2048, 16))
DIL_HEADS = 4
RP_BUCKETS = 32
RP_MAX_DIST = 2048


def _cparams(*sem):
    return pltpu.CompilerParams(dimension_semantics=sem, vmem_limit_bytes=VMEM_LIMIT)


def _const(shape):
    nd = len(shape)
    return pl.BlockSpec(shape, lambda *_: (0,) * nd, pipeline_mode=pl.Buffered(1))


def _dot(a, b):
    return jnp.dot(a, b, preferred_element_type=F32)


def _dot_nt(a, b):
    return lax.dot_general(a, b, (((1,), (1,)), ((), ())), preferred_element_type=F32)


def _dot_tn(a, b):
    return lax.dot_general(a, b, (((0,), (0,)), ((), ())), preferred_element_type=F32)


def _rms(x, gain):
    return x * lax.rsqrt(jnp.mean(x * x, axis=-1, keepdims=True) + EPS) * gain


def _seg_ms(x, bd):
    sq = x * x
    hi = sq.astype(BF16)
    lo = (sq - hi.astype(F32)).astype(BF16)
    bw = bd.shape[0]
    cols = []
    for c0 in range(0, x.shape[1], bw):
        wd = min(bw, x.shape[1] - c0)
        blk = bd[:wd, :wd]
        cols.append(_dot(hi[:, c0:c0 + wd], blk) + _dot(lo[:, c0:c0 + wd], blk))
    return (cols[0] if len(cols) == 1 else jnp.concatenate(cols, axis=1)) * (1.0 / HEAD_DIM)


def _seg_rms(x, bd, gain):
    return x * lax.rsqrt(_seg_ms(x, bd) + EPS) * gain


def _norm_k_rows(x, bd, gain, kmask):
    bw = bd.shape[0]
    r = lax.rsqrt(_seg_ms(x[:, :bw], bd) + EPS)
    r = jnp.concatenate([r, jnp.ones((x.shape[0], x.shape[1] - bw), F32)], axis=1)
    return x * jnp.where(kmask > 0.5, r, 1.0) * gain


V7X_MXU_WIDTH = 256


def _block_diag_ones(width):
    i = np.arange(min(width, V7X_MXU_WIDTH)) // HEAD_DIM
    return jnp.asarray((i[:, None] == i[None, :]).astype(np.float32), BF16)


def _rel_bucket(dist):
    exact = RP_BUCKETS // 2
    d = jnp.maximum(dist, 0)
    log_ratio = jnp.log(jnp.maximum(d, 1).astype(F32) / exact) / math.log(RP_MAX_DIST / exact)
    large = jnp.minimum(exact + (log_ratio * (RP_BUCKETS - exact)).astype(jnp.int32), RP_BUCKETS - 1)
    return jnp.where(d < exact, d, large)


def _bias_tile(rel_bias, dist, valid):
    onehot = (_rel_bucket(dist)[..., None] == jnp.arange(RP_BUCKETS)).astype(F32)
    t = jnp.einsum('...k,kh->h...', onehot, rel_bias.astype(F32), precision=HIGHEST)
    return jnp.where(valid[None], t, NEG)


def _toeplitz(ext, n, m):
    p = n + m - 1
    u = jnp.concatenate([jnp.flip(ext[..., :m], -1), jnp.flip(ext[..., m:], -1)], axis=-1)
    reps = (1,) * (ext.ndim - 1) + (n,)
    t = jnp.tile(u, reps)[..., :n * (p - 1)].reshape(ext.shape[:-1] + (n, p - 1))
    return t[..., :m]


def _softmax_piece(s, mask, m_prev, l_prev, acc_prev, v, v_t=False):
    m_new = jnp.maximum(m_prev, jnp.max(s, axis=-1, keepdims=True))
    alpha = jnp.exp(m_prev - m_new)
    p = jnp.where(mask, jnp.exp(s - m_new), 0.0)
    l_new = alpha * l_prev + jnp.sum(p, axis=-1, keepdims=True)
    pb = p.astype(BF16)
    acc_new = alpha * acc_prev + (_dot_nt(pb, v) if v_t else _dot(pb, v))
    return m_new, l_new, acc_new


def _ffn_body(x_ref, g_ref, win_ref, wout_ref, *rest, dff, ple):
    x = x_ref[...]
    xn = _rms(x, g_ref[...]).astype(BF16)
    gu = _dot(xn, win_ref[...])
    gg = gu[:, :dff]
    a = (gg * jax.nn.sigmoid(gg) * gu[:, dff:]).astype(BF16)
    h = x + 0.5 * _dot(a, wout_ref[...])
    if ple:
        p_ref, gp_ref, wg_ref, wp_ref, o_ref = rest
        hn = _rms(h, gp_ref[...]).astype(BF16)
        gate = jax.nn.sigmoid(_dot(hn, wg_ref[...]))
        h = h + gate * _dot(p_ref[...].astype(BF16), wp_ref[...])
    else:
        (o_ref,) = rest
    o_ref[...] = h


def _ffn(h, gain, w_in, w_out, ple=None, *, tm):
    n, d = h.shape
    dff = w_out.shape[0]
    row = lambda i: (i, 0)
    in_specs = [pl.BlockSpec((tm, d), row), _const((1, d)), _const(w_in.shape), _const(w_out.shape)]
    args = [h, gain.reshape(1, d), w_in, w_out]
    if ple is not None:
        p, gp, wg, wp = ple
        in_specs += [pl.BlockSpec((tm, p.shape[1]), row), _const((1, d)), _const(wg.shape), _const(wp.shape)]
        args += [p, gp.reshape(1, d), wg, wp]
    return pl.pallas_call(
        functools.partial(_ffn_body, dff=dff, ple=ple is not None),
        grid=(n // tm,), in_specs=in_specs, out_specs=pl.BlockSpec((tm, d), row),
        out_shape=jax.ShapeDtypeStruct((n, d), F32), compiler_params=_cparams("parallel"),
        name="ffn_ple" if ple is not None else "ffn",
    )(*args)


EVEN_HG = 4 * HG_HEADS * HEAD_DIM
EVEN_Q = NSA_HEADS * HEAD_DIM
ROWS_NSA = 2 * NSA_KV * HEAD_DIM
EVEN_PAD = EVEN_HG + EVEN_Q + 3 * ROWS_NSA + LANES


GATE_ROWS = 16


def _inproj_even_body(x_ref, g_ref, w_ref, gk_ref, km_ref, bd_ref,
                      hg_ref, q_ref, cmp_ref, slc_ref, win_ref, slcg_ref, wing_ref, gate_ref, *gate_t_ref):
    xn = _rms(x_ref[...], g_ref[...]).astype(BF16)
    p = _dot(xn, w_ref[...])
    c0 = EVEN_HG
    c1 = c0 + EVEN_Q
    hg_ref[...] = p[:, :c0]
    q_ref[...] = p[:, c0:c1]
    cmp_ref[...] = p[:, c1:c1 + ROWS_NSA]
    bd = bd_ref[...]
    km = km_ref[...]
    slc = _norm_k_rows(p[:, c1 + ROWS_NSA:c1 + 2 * ROWS_NSA], bd, gk_ref[0:1, :], km)
    win = _norm_k_rows(p[:, c1 + 2 * ROWS_NSA:c1 + 3 * ROWS_NSA], bd, gk_ref[1:2, :], km)
    slc_ref[...] = slc
    win_ref[...] = win
    sig = jax.nn.sigmoid(p[:, c1 + 3 * ROWS_NSA:])
    kw = NSA_KV * HEAD_DIM
    for g in range(NSA_KV):
        ks = slice(HEAD_DIM * g, HEAD_DIM * (g + 1))
        vs = slice(kw + HEAD_DIM * g, kw + HEAD_DIM * (g + 1))
        slcg_ref[g] = jnp.concatenate([slc[:, ks], slc[:, vs]], axis=1)
        wing_ref[g] = jnp.concatenate([win[:, ks], win[:, vs]], axis=1)
        gate_ref[g] = sig if g == 0 else pltpu.roll(sig, LANES - 3 * NSA_GROUP * g, 1)
    if gate_t_ref:
        sig_t = sig.T
        for g in range(NSA_KV):
            for half in range(sig.shape[0] // NSA_TQ):
                gate_t_ref[0][g, half] = sig_t[3 * NSA_GROUP * g:3 * NSA_GROUP * g + GATE_ROWS,
                                               half * NSA_TQ:(half + 1) * NSA_TQ]
        for dst_ref, rows in zip(gate_t_ref[1:], (p[:, c1:c1 + ROWS_NSA], slc, win)):
            _store_transposed(dst_ref, rows)


def _store_transposed(dst_ref, x):
    for c in range(x.shape[1] // LANES):
        dst_ref[0, c * LANES:(c + 1) * LANES, :] = x[:, c * LANES:(c + 1) * LANES].T


def _inproj_even(h, gain, w_pad, qk_norm, *, tm, gates_t=False, seq=None):
    n, d = h.shape
    kw = NSA_KV * HEAD_DIM
    ones = jnp.ones((kw,), F32)
    gk = jnp.stack([jnp.concatenate([jnp.tile(qk_norm[2], NSA_KV), ones]),
                    jnp.concatenate([jnp.tile(qk_norm[3], NSA_KV), ones])])
    km = jnp.concatenate([ones, 0.0 * ones]).reshape(1, ROWS_NSA)
    row = lambda i: (i, 0)
    grp = lambda i: (0, i, 0)
    shp = lambda *s: jax.ShapeDtypeStruct(s, F32)
    out_specs = [pl.BlockSpec((tm, EVEN_HG), row), pl.BlockSpec((tm, EVEN_Q), row),
                 pl.BlockSpec((tm, ROWS_NSA), row), pl.BlockSpec((tm, ROWS_NSA), row),
                 pl.BlockSpec((tm, ROWS_NSA), row), pl.BlockSpec((NSA_KV, tm, LANES), grp),
                 pl.BlockSpec((NSA_KV, tm, LANES), grp), pl.BlockSpec((NSA_KV, tm, LANES), grp)]
    out_shape = [shp(n, EVEN_HG), shp(n, EVEN_Q), shp(n, ROWS_NSA), shp(n, ROWS_NSA), shp(n, ROWS_NSA),
                 shp(NSA_KV, n, LANES), shp(NSA_KV, n, LANES), shp(NSA_KV, n, LANES)]
    if gates_t:
        out_specs.append(pl.BlockSpec((NSA_KV, tm // NSA_TQ, GATE_ROWS, NSA_TQ), lambda i: (0, i, 0, 0)))
        out_shape.append(shp(NSA_KV, n // NSA_TQ, GATE_ROWS, NSA_TQ))
        tpb = seq // tm
        for _ in range(3):
            out_specs.append(pl.BlockSpec((1, ROWS_NSA, tm), lambda i: (i // tpb, 0, i % tpb)))
            out_shape.append(shp(n // seq, ROWS_NSA, seq))
    return pl.pallas_call(
        _inproj_even_body, grid=(n // tm,),
        in_specs=[pl.BlockSpec((tm, d), row), _const((1, d)), _const(w_pad.shape), _const((2, ROWS_NSA)),
                  _const((1, ROWS_NSA)), _const(_block_diag_ones(ROWS_NSA).shape)],
        out_specs=out_specs, out_shape=out_shape,
        compiler_params=_cparams("parallel"), name="inproj_even",
    )(h, gain.reshape(1, d), w_pad, gk, km, _block_diag_ones(ROWS_NSA))


ODD_W = 3 * DIL_HEADS * HEAD_DIM
ODD_GW = DIL_HEADS * HEAD_DIM


def _to_residue_view(src, dst_ref, stage_ref, dil):
    rows, w = src.shape
    for blk in range(w // LANES):
        stage_ref[...] = src[:, blk * LANES:(blk + 1) * LANES]
        for r in range(dil):
            lo = r * w + blk * LANES
            dst_ref[:, lo:lo + LANES] = stage_ref[pl.ds(r, rows // dil, stride=dil), :]


def _from_residue_view(src_ref, stage_ref, dil, w):
    rows = src_ref.shape[0] * dil
    parts = []
    for blk in range(w // LANES):
        for r in range(dil):
            lo = r * w + blk * LANES
            stage_ref[pl.ds(r, rows // dil, stride=dil), :] = src_ref[:, lo:lo + LANES]
        parts.append(stage_ref[...])
    return jnp.concatenate(parts, axis=1)


def _inproj_odd_body(x_ref, g_ref, w_ref, gq_ref, gk_ref, bd_ref, q_ref, kv_ref, *views):
    xn = _rms(x_ref[...], g_ref[...]).astype(BF16)
    p = _dot(xn, w_ref[...])
    bd = bd_ref[...]
    q = _seg_rms(p[:, :ODD_W], bd, gq_ref[...]) * (HEAD_DIM ** -0.5)
    k = _seg_rms(p[:, ODD_W:2 * ODD_W], bd, gk_ref[...])
    v = p[:, 2 * ODD_W:]
    for g in range(3):
        s = slice(ODD_GW * g, ODD_GW * (g + 1))
        kv = jnp.concatenate([k[:, s], v[:, s]], axis=1)
        q_ref[g] = q[:, s]
        kv_ref[g] = kv
        dil = DIL_PATTERN[g][1]
        if views:
            stage_ref = views[-1]
            if dil > 1:
                _to_residue_view(q[:, s], views[2 * (g - 1)], stage_ref, dil)
                _to_residue_view(kv, views[2 * (g - 1) + 1], stage_ref, dil)
            _store_transposed(views[4 + g], kv)


def _inproj_odd(h, gain, w, qk_norm, *, tm, views=False, seq=None):
    n, d = h.shape
    row = lambda i: (i, 0)
    nh = 3 * DIL_HEADS
    out_specs = [pl.BlockSpec((3, tm, ODD_GW), lambda i: (0, i, 0)),
                 pl.BlockSpec((3, tm, 2 * ODD_GW), lambda i: (0, i, 0))]
    out_shape = [jax.ShapeDtypeStruct((3, n, ODD_GW), F32), jax.ShapeDtypeStruct((3, n, 2 * ODD_GW), F32)]
    if views:
        for _, dil in DIL_PATTERN[1:]:
            for width in (ODD_GW, 2 * ODD_GW):
                out_specs.append(pl.BlockSpec((tm // dil, dil * width), row))
                out_shape.append(jax.ShapeDtypeStruct((n // dil, dil * width), F32))
        tpb = seq // tm
        for _ in DIL_PATTERN:
            out_specs.append(pl.BlockSpec((1, 2 * ODD_GW, tm), lambda i: (i // tpb, 0, i % tpb)))
            out_shape.append(jax.ShapeDtypeStruct((n // seq, 2 * ODD_GW, seq), F32))
    return pl.pallas_call(
        _inproj_odd_body, grid=(n // tm,),
        in_specs=[pl.BlockSpec((tm, d), row), _const((1, d)), _const(w.shape), _const((1, ODD_W)),
                  _const((1, ODD_W)), _const(_block_diag_ones(ODD_W).shape)],
        out_specs=out_specs, out_shape=out_shape,
        scratch_shapes=[pltpu.VMEM((tm, LANES), F32)] if views else [],
        compiler_params=_cparams("parallel"), name="inproj_odd",
    )(h, gain.reshape(1, d), w, jnp.tile(qk_norm[0], nh).reshape(1, ODD_W),
      jnp.tile(qk_norm[1], nh).reshape(1, ODD_W), _block_diag_ones(ODD_W))


def _outproj2_body(h_ref, a_ref, b_ref, wa_ref, wb_ref, o_ref):
    o_ref[...] = (h_ref[...] + _dot(a_ref[...].astype(BF16), wa_ref[...])
                  + _dot(b_ref[...].astype(BF16), wb_ref[...]))


def _outproj2(h, a, b, wa, wb, *, tm):
    n, d = h.shape
    row = lambda i: (i, 0)
    return pl.pallas_call(
        _outproj2_body, grid=(n // tm,),
        in_specs=[pl.BlockSpec((tm, d), row), pl.BlockSpec((tm, a.shape[1]), row),
                  pl.BlockSpec((tm, b.shape[1]), row), _const(wa.shape), _const(wb.shape)],
        out_specs=pl.BlockSpec((tm, d), row), out_shape=jax.ShapeDtypeStruct((n, d), F32),
        compiler_params=_cparams("parallel"), name="outproj_even",
    )(h, a, b, wa, wb)


def _outproj2t_body(h_ref, a_ref, bt_ref, wa_ref, wb_ref, o_ref):
    acc = h_ref[...] + _dot(a_ref[...].astype(BF16), wa_ref[...])
    tq = bt_ref.shape[2]
    for i in range(bt_ref.shape[0]):
        rs = slice(i * tq, (i + 1) * tq)
        o_ref[rs, :] = acc[rs, :] + _dot_tn(bt_ref[i].astype(BF16), wb_ref[...])


def _outproj2t(h, a, bt, wa, wb, *, tm):
    n, d = h.shape
    row = lambda i: (i, 0)
    tq = bt.shape[2]
    return pl.pallas_call(
        _outproj2t_body, grid=(n // tm,),
        in_specs=[pl.BlockSpec((tm, d), row), pl.BlockSpec((tm, a.shape[1]), row),
                  pl.BlockSpec((tm // tq, bt.shape[1], tq), lambda i: (i, 0, 0)), _const(wa.shape), _const(wb.shape)],
        out_specs=pl.BlockSpec((tm, d), row), out_shape=jax.ShapeDtypeStruct((n, d), F32),
        compiler_params=_cparams("parallel"), name="outproj_even_t",
    )(h, a, bt, wa, wb)


def _outproj_merge_body(h_ref, o0_ref, o1_ref, o2_ref, l0_ref, l1_ref, l2_ref, w_ref, o_ref, *stage, views):
    if views:
        load = lambda ref, g: (ref[...] if DIL_PATTERN[g][1] == 1
                               else _from_residue_view(ref, stage[0], DIL_PATTERN[g][1], ODD_GW))
    else:
        load = lambda ref, g: ref[...]
    mixed = _merge_groups([load(r, g) for g, r in enumerate((o0_ref, o1_ref, o2_ref))],
                          [load(r, g) for g, r in enumerate((l0_ref, l1_ref, l2_ref))])
    o_ref[...] = h_ref[...] + _dot(mixed.astype(BF16), w_ref[...])


def _outproj_merge(h, outs, lses, w, *, tm, views=False):
    n, d = h.shape
    row = lambda i: (i, 0)
    parts = [pl.BlockSpec((tm * a.shape[0] // n, a.shape[1]), row) for a in list(outs) + list(lses)]
    return pl.pallas_call(
        functools.partial(_outproj_merge_body, views=views), grid=(n // tm,),
        in_specs=[pl.BlockSpec((tm, d), row)] + parts + [_const(w.shape)],
        out_specs=pl.BlockSpec((tm, d), row), out_shape=jax.ShapeDtypeStruct((n, d), F32),
        scratch_shapes=[pltpu.VMEM((tm, LANES), F32)] if views else [],
        compiler_params=_cparams("parallel"), name="outproj_odd",
    )(h, *outs, *lses, w)


def _hgrn_prefix_matrix(C):
    t = np.arange(C)[:, None]
    j = np.arange(C)[None, :]
    mats = [j <= t]
    lvl = 1
    while (1 << lvl) <= C:
        mats.append(j <= ((t >> lvl) << lvl) + (1 << lvl) // 2 - 1)
        lvl += 1
    return jnp.asarray(np.concatenate(mats, axis=0).astype(np.float32), BF16)


def _hgrn_body(hq_ref, hf_ref, hi_ref, hgt_ref, lb_ref, gn_ref, bd_ref, pm_ref, s0_ref, o_ref, sfin_ref, st_ref, *, C):
    c = pl.program_id(1)

    @pl.when(c == 0)
    def _():
        st_ref[...] = s0_ref[0]

    dk = HEAD_DIM
    w = HG_HEADS * dk
    lb = lb_ref[...]
    f = lb + (1.0 - lb) * jax.nn.sigmoid(hf_ref[...])
    lf = jnp.log(f)
    kk = 1.0 - f
    ti = lax.broadcasted_iota(jnp.int32, (C, C), 0)
    si = lax.broadcasted_iota(jnp.int32, (C, C), 1)
    p1 = lf.astype(BF16)
    r1 = lf - p1.astype(F32)
    p2 = r1.astype(BF16)
    p3 = (r1 - p2.astype(F32)).astype(BF16)
    sums = _dot(pm_ref[...], jnp.concatenate([p1, p2, p3], axis=1))
    sums = sums[:, :w] + sums[:, w:2 * w] + sums[:, 2 * w:]
    b = sums[0:C]
    q = hq_ref[...]
    v = hi_ref[...].astype(BF16)
    heads = range(HG_HEADS)
    hs = [slice(dk * h, dk * (h + 1)) for h in heads]

    qb = q.astype(BF16)
    kb = kk.astype(BF16)
    a = [jnp.where(ti == si, _dot_nt(qb[:, hs[h]], kb[:, hs[h]]), 0.0) for h in heads]
    lvl = 1
    while (1 << lvl) <= C:
        m = 1 << lvl
        half = m // 2
        r = sums[lvl * C:(lvl + 1) * C]
        e = jnp.exp(-jnp.abs(b - r))
        qt = (q * e).astype(BF16)
        kt = (kk * e).astype(BF16)
        pair = ((ti >> lvl) == (si >> lvl)) & ((ti & (m - 1)) >= half) & ((si & (m - 1)) < half)
        a = [a[h] + jnp.where(pair, _dot_nt(qt[:, hs[h]], kt[:, hs[h]]), 0.0) for h in heads]
        lvl += 1

    b_last = b[C - 1:C, :]
    qe = (q * jnp.exp(b)).astype(BF16)
    kdec = (kk * jnp.exp(b_last - b)).astype(BF16)
    e_last = jnp.exp(b_last)
    outs = []
    for h in heads:
        st = st_ref[h]
        outs.append(_dot(a[h].astype(BF16), v[:, hs[h]]) + _dot_nt(qe[:, hs[h]], st.astype(BF16)))
        st_ref[h] = st * e_last[:, hs[h]] + _dot_tn(v[:, hs[h]], kdec[:, hs[h]])
    o = jnp.concatenate(outs, axis=1)
    gt = hgt_ref[...]
    o_ref[...] = _seg_rms(o, bd_ref[...], gn_ref[...]) * (gt * jax.nn.sigmoid(gt))

    @pl.when(c == pl.num_programs(1) - 1)
    def _():
        sfin_ref[0] = st_ref[...]


def _hgrn(hg, lb, hg_norm, s0, *, batch, seq):
    C = math.gcd(seq, HG_CHUNK)
    nc = seq // C
    w = HG_HEADS * HEAD_DIM
    col = lambda j: pl.BlockSpec((C, w), lambda b, c: (b * nc + c, j))
    st_spec = pl.BlockSpec((1, HG_HEADS, HEAD_DIM, HEAD_DIM), lambda b, c: (b, 0, 0, 0))
    pm = _hgrn_prefix_matrix(C)
    o, sfin = pl.pallas_call(
        functools.partial(_hgrn_body, C=C), grid=(batch, nc),
        in_specs=[col(0), col(1), col(2), col(3), _const((1, w)), _const((1, w)), _const(_block_diag_ones(w).shape), _const(pm.shape),
                  st_spec],
        out_specs=[pl.BlockSpec((C, w), lambda b, c: (b * nc + c, 0)), st_spec],
        out_shape=[jax.ShapeDtypeStruct((batch * seq, w), F32),
                   jax.ShapeDtypeStruct((batch, HG_HEADS, HEAD_DIM, HEAD_DIM), F32)],
        scratch_shapes=[pltpu.VMEM((HG_HEADS, HEAD_DIM, HEAD_DIM), F32)],
        compiler_params=_cparams("parallel", "arbitrary"), name="hgrn_scan",
    )(hg, hg, hg, hg, lb.reshape(1, w), jnp.tile(hg_norm, HG_HEADS).reshape(1, w), _block_diag_ones(w), pm,
      jnp.swapaxes(s0, 2, 3))
    return o, jnp.swapaxes(sfin, 2, 3)


CHUNKS_PER_PAGE = PAGE // CMP_STRIDE
PAGE_LANES = CMP_STRIDE * ROWS_NSA
CMP_HID = 2 * NSA_KV * 2 * HEAD_DIM
CMP_PP = 8


def _compress_body(pt_ref, *refs, n_chunks, transposed):
    pages = refs[:CMP_PP]
    wbig_ref, pe_ref, w2_ref, gk_ref, km_ref, bd_ref, rows_ref, grp_ref, xs_ref = refs[CMP_PP:CMP_PP + 9]
    pg_scr = refs[CMP_PP + 9:]
    nb = ROWS_NSA // LANES
    j = pl.program_id(1)
    for i in range(0, CMP_PP, 2):
        for k in range(2):
            for c in range(nb):
                cs = slice(c * LANES, (c + 1) * LANES)
                pg_scr[k * nb + c][...] = pages[i + k][0, cs, :].T if transposed else pages[i + k][0, :, cs]
        r0 = pl.multiple_of((j * CMP_PP + i) * CHUNKS_PER_PAGE, 2 * CHUNKS_PER_PAGE)
        for s in range(CMP_STRIDE):
            rows_s = pl.ds(s, CHUNKS_PER_PAGE, stride=CMP_STRIDE)
            for c in range(nb):
                blk = jnp.concatenate([pg_scr[c][rows_s, :], pg_scr[nb + c][rows_s, :]], axis=0).astype(BF16)
                lo = (c * CMP_STRIDE + s) * LANES
                xs_ref[pl.ds(r0, 2 * CHUNKS_PER_PAGE), lo:lo + LANES] = blk

    @pl.when(j == pl.num_programs(1) - 1)
    def _():
        blk_w = CMP_STRIDE * LANES
        first = lambda x: jnp.concatenate(
            [_dot(x[:, c * blk_w:(c + 1) * blk_w], wbig_ref[c]) for c in range(nb)], axis=1)
        y = first(xs_ref[...])
        ysh = pltpu.roll(y, n_chunks - 1, 0)
        hid = y + pltpu.roll(ysh, CMP_HID - HEAD_DIM, 1)
        cf = first(pe_ref[...].astype(BF16))
        lane = lax.broadcasted_iota(jnp.int32, (8, CMP_HID), 1)
        cs = jnp.where(((lane >> 6) & 1) == 0, jnp.broadcast_to(cf[0:1], (8, CMP_HID)),
                       jnp.broadcast_to(cf[1:2], (8, CMP_HID)))
        cs = cs + pltpu.roll(cs, CMP_HID - HEAD_DIM, 1)
        hid = hid + cs[0:1]
        act = (hid * jax.nn.sigmoid(hid)).astype(BF16)
        out = _dot(act, w2_ref[...])
        out = _norm_k_rows(out, bd_ref[...], gk_ref[...], km_ref[...])
        rows_ref[0] = out
        kw = NSA_KV * HEAD_DIM
        for g in range(NSA_KV):
            grp_ref[0, g] = jnp.concatenate([out[:, HEAD_DIM * g:HEAD_DIM * (g + 1)],
                                             out[:, kw + HEAD_DIM * g:kw + HEAD_DIM * (g + 1)]], axis=1)


def _compress_weights(cmp_pe, cmp_w1, cmp_w2, k_gain):
    ratio = CMP_BLOCK // CMP_STRIDE
    w1 = cmp_w1.reshape(2, ratio, CMP_STRIDE, HEAD_DIM, HEAD_DIM)
    eye_kv = jnp.eye(2, dtype=F32)
    eye_g = jnp.eye(NSA_KV, dtype=F32)
    nb = ROWS_NSA // LANES
    member_kv = np.arange(2 * NSA_KV) // NSA_KV
    w1m = w1[member_kv].reshape(nb, 2, ratio, CMP_STRIDE, HEAD_DIM, HEAD_DIM)
    wbig = jnp.einsum('chrsde,hH->cshdHre', w1m, eye_kv).reshape(nb, CMP_STRIDE * LANES, 2 * ratio * HEAD_DIM)
    wbig = wbig.astype(BF16)
    pe = cmp_pe.reshape(2, ratio, CMP_STRIDE, HEAD_DIM)
    pe_rows = pe[member_kv].reshape(nb, 2, ratio, CMP_STRIDE, HEAD_DIM).transpose(2, 0, 3, 1, 4)
    pe_rows = pe_rows.reshape(ratio, PAGE_LANES)
    pe_rows = jnp.concatenate([pe_rows, jnp.zeros((8 - ratio, PAGE_LANES), F32)], axis=0)
    sel_r0 = jnp.asarray([1.0, 0.0], F32)
    w2big = jnp.einsum('ked,kK,gG,r->kgreKGd', cmp_w2, eye_kv, eye_g, sel_r0).reshape(CMP_HID, ROWS_NSA).astype(BF16)
    kw = NSA_KV * HEAD_DIM
    gk = jnp.concatenate([jnp.tile(k_gain, NSA_KV), jnp.ones((kw,), F32)]).reshape(1, ROWS_NSA)
    km = jnp.concatenate([jnp.ones((kw,), F32), jnp.zeros((kw,), F32)]).reshape(1, ROWS_NSA)
    return wbig, pe_rows, w2big, gk, km


def _compress(pool, table, cw, *, transposed):
    wbig, pe_rows, w2big, gk, km = cw
    bsz, npg = table.shape
    n_chunks = npg * CHUNKS_PER_PAGE
    page = lambda i: pl.BlockSpec((1,) + pool.shape[1:], lambda b, j, pt: (pt[b, j * CMP_PP + i], 0, 0))
    return pl.pallas_call(
        functools.partial(_compress_body, n_chunks=n_chunks, transposed=transposed),
        grid_spec=pltpu.PrefetchScalarGridSpec(
            num_scalar_prefetch=1, grid=(bsz, npg // CMP_PP),
            in_specs=[page(i) for i in range(CMP_PP)] + [
                _const(wbig.shape), _const(pe_rows.shape), _const(w2big.shape), _const((1, ROWS_NSA)),
                _const((1, ROWS_NSA)), _const(_block_diag_ones(ROWS_NSA).shape)],
            out_specs=[pl.BlockSpec((1, n_chunks, ROWS_NSA), lambda b, j, pt: (b, 0, 0)),
                       pl.BlockSpec((1, NSA_KV, n_chunks, LANES), lambda b, j, pt: (b, 0, 0, 0))],
            scratch_shapes=[pltpu.VMEM((n_chunks, PAGE_LANES), BF16)]
            + [pltpu.VMEM((PAGE, LANES), F32)] * (2 * ROWS_NSA // LANES)),
        out_shape=[jax.ShapeDtypeStruct((bsz, n_chunks, ROWS_NSA), F32),
                   jax.ShapeDtypeStruct((bsz, NSA_KV, n_chunks, LANES), F32)],
        compiler_params=_cparams("parallel", "arbitrary"), name="nsa_compress",
    )(table, *([pool] * CMP_PP), wbig, pe_rows, w2big, gk, km, _block_diag_ones(ROWS_NSA))


def _select_blocks(imp, pos, n_slc, k_sel):
    lane = lax.broadcasted_iota(jnp.int32, imp.shape, 1)
    cur = pos >> 6
    forced = (lane == 0) | (lane == cur) | (lane == cur - 1)
    score = jnp.where(forced, FORCE_SCORE, jnp.where(lane <= cur, imp, -1.0))
    rank = jnp.zeros(imp.shape, F32)
    for i in range(n_slc):
        ci = score[:, i:i + 1]
        rank = rank + jnp.where(ci > score, 1.0, jnp.where(ci == score, jnp.where(lane > i, 1.0, 0.0), 0.0))
    return jnp.where(lane < n_slc, jnp.where(rank < k_sel, 1.0, 0.0), 0.0)


NSA_TQ = 256


def _nsa_prompt_body(q_ref, gate_ref, kcv_ref, slc_ref, win_ref, gq_ref, bd_ref, bc_ref, bs_ref, ov_ref, ex_ref,
                     o_ref, sel_scr, m_scr, l_scr, acc_scr, *, seq, n_slc, k_sel):
    tq = NSA_TQ
    rows = NSA_GROUP * tq
    n_tiles = seq // tq
    hd = HEAD_DIM
    kc = kcv_ref[0, 0, :, :hd].astype(BF16)
    vc = kcv_ref[0, 0, :, hd:].astype(BF16)
    n_c = kc.shape[0]
    ii = lax.broadcasted_iota(jnp.int32, (tq, tq), 0)
    jj = lax.broadcasted_iota(jnp.int32, (tq, tq), 1)

    def flash(qs, kv_ref, t, lo, hi, extra_mask):
        m_scr[...] = jnp.full((rows, 1), NEG, F32)
        l_scr[...] = jnp.zeros((rows, 1), F32)
        acc_scr[...] = jnp.zeros((rows, hd), F32)

        def step(jt, carry):
            k0 = pl.multiple_of(jt * tq, tq)
            kv = kv_ref[0, pl.ds(k0, tq), :]
            bias = bs_ref[0, t - jt]
            ok = (bias > NEG_TEST).reshape(NSA_GROUP, tq, tq) & extra_mask(jt)[None]
            s = (_dot_nt(qs, kv[:, :hd].astype(BF16)) + bias).reshape(NSA_GROUP, tq, tq)
            s = jnp.where(ok, s, NEG).reshape(rows, tq)
            m, l, acc = _softmax_piece(s, ok.reshape(rows, tq), m_scr[...], l_scr[...], acc_scr[...],
                                       kv[:, hd:].astype(BF16))
            m_scr[...] = m
            l_scr[...] = l
            acc_scr[...] = acc
            return carry

        lax.fori_loop(lo, hi + 1, step, 0)
        return acc_scr[...] / jnp.maximum(l_scr[...], 1e-30)

    def q_tile(t, carry):
        r0 = pl.multiple_of(t * tq, tq)
        qn = _seg_rms(q_ref[pl.ds(r0, tq), :], bd_ref[...], gq_ref[...]) * (hd ** -0.5)
        qs = jnp.concatenate([qn[:, hd * h:hd * (h + 1)] for h in range(NSA_GROUP)], axis=0).astype(BF16)
        bias_c = bc_ref[0, :, pl.ds(r0, tq), :].reshape(rows, n_c)
        ok_c = bias_c > NEG_TEST
        s_c = _dot_nt(qs, kc) + bias_c
        m_c = jnp.max(s_c, axis=-1, keepdims=True)
        e_c = jnp.where(ok_c, jnp.exp(s_c - m_c), 0.0)
        p_c = (e_c / jnp.maximum(jnp.sum(e_c, axis=-1, keepdims=True), 1e-30)).astype(BF16)
        o_c = _dot(p_c, vc)
        d = _dot(p_c, ov_ref[...])
        imp = d[0:tq] + d[tq:2 * tq] + d[2 * tq:3 * tq] + d[3 * tq:4 * tq]
        pos = r0 + lax.broadcasted_iota(jnp.int32, (tq, 1), 0)
        sel = _select_blocks(imp, pos, n_slc, k_sel).astype(BF16)
        for jt in range(n_tiles):
            sel_scr[jt] = _dot(sel, ex_ref[:, jt * tq:(jt + 1) * tq])
        o_s = flash(qs, slc_ref, t, 0, t, lambda jt: sel_scr[jt] > 0.5)
        o_w = flash(qs, win_ref, t, jnp.maximum(t - NSA_WINDOW // tq, 0), t,
                    lambda jt: (t - jt) * tq + ii - jj <= NSA_WINDOW)
        gates = gate_ref[0, pl.ds(r0, tq), :]
        outs = []
        for h in range(NSA_GROUP):
            hr = slice(h * tq, (h + 1) * tq)
            outs.append(gates[:, 3 * h:3 * h + 1] * o_c[hr] + gates[:, 3 * h + 1:3 * h + 2] * o_s[hr]
                        + gates[:, 3 * h + 2:3 * h + 3] * o_w[hr])
        o_ref[pl.ds(r0, tq), :] = jnp.concatenate(outs, axis=1)
        return carry

    lax.fori_loop(0, n_tiles, q_tile, 0)


def _nsa_prompt_tables(rel_bias, seq, n_c):
    tq = NSA_TQ
    n_tiles = seq // tq
    n_cmp = seq // CMP_STRIDE - CMP_BLOCK // CMP_STRIDE + 1
    n_slc = -(-seq // SEL_BLOCK)
    n_a = seq // CMP_STRIDE
    t_ext = jnp.arange(n_a + n_c - 1) - (n_c - 1)
    d_ext = CMP_STRIDE * t_ext[None, :] + jnp.arange(CMP_STRIDE)[:, None] - (CMP_BLOCK - 1)
    bc = _toeplitz(_bias_tile(rel_bias, d_ext, d_ext >= 0), n_a, n_c)
    bc = jnp.where(jnp.arange(n_c) < n_cmp, bc.transpose(0, 2, 1, 3).reshape(NSA_HEADS, seq, n_c), NEG)
    bc = bc.reshape(NSA_KV, NSA_GROUP, seq, n_c)
    d_ext = jnp.arange(n_tiles)[:, None] * tq + jnp.arange(2 * tq - 1)[None, :] - (tq - 1)
    bs = _toeplitz(_bias_tile(rel_bias, d_ext, d_ext >= 0), tq, tq)
    bs = bs.reshape(NSA_KV, NSA_GROUP, n_tiles, tq, tq).transpose(0, 2, 1, 3, 4).reshape(
        NSA_KV, n_tiles, NSA_GROUP * tq, tq)
    c_lo = np.arange(n_c) * CMP_STRIDE
    s_lo = np.arange(LANES) * SEL_BLOCK
    ov = ((c_lo[:, None] <= s_lo[None, :] + SEL_BLOCK - 1) & (c_lo[:, None] + CMP_BLOCK - 1 >= s_lo[None, :])
          & (np.arange(n_c) < n_cmp)[:, None] & (np.arange(LANES) < n_slc)[None, :])
    ex = np.arange(LANES)[:, None] == (np.arange(seq)[None, :] // SEL_BLOCK)
    return bc, bs, jnp.asarray(ov.astype(np.float32), BF16), jnp.asarray(ex.astype(np.float32), BF16), n_slc


def _nsa_prompt(q, gates, kcv, slcg, wing, q_gain, tables, *, batch, seq):
    bc, bs, ov, ex, n_slc = tables
    n_c = kcv.shape[2]
    n_tiles = seq // NSA_TQ
    gw = NSA_GROUP * HEAD_DIM
    rows = NSA_GROUP * NSA_TQ
    per_g = lambda shape: pl.BlockSpec((1,) + shape, lambda g, b: (g,) + (0,) * len(shape),
                                       pipeline_mode=pl.Buffered(1))
    return pl.pallas_call(
        functools.partial(_nsa_prompt_body, seq=seq, n_slc=n_slc, k_sel=min(SEL_TOPK, n_slc)),
        grid=(NSA_KV, batch),
        in_specs=[pl.BlockSpec((seq, gw), lambda g, b: (b, g)),
                  pl.BlockSpec((1, seq, LANES), lambda g, b: (g, b, 0)),
                  pl.BlockSpec((1, 1, n_c, LANES), lambda g, b: (b, g, 0, 0)),
                  pl.BlockSpec((1, seq, LANES), lambda g, b: (g, b, 0)),
                  pl.BlockSpec((1, seq, LANES), lambda g, b: (g, b, 0)),
                  _const((1, gw)), _const(_block_diag_ones(gw).shape),
                  per_g((NSA_GROUP, seq, n_c)), per_g((n_tiles, rows, NSA_TQ)),
                  _const(ov.shape), _const(ex.shape)],
        out_specs=pl.BlockSpec((seq, gw), lambda g, b: (b, g)),
        out_shape=jax.ShapeDtypeStruct((batch * seq, NSA_KV * gw), F32),
        scratch_shapes=[pltpu.VMEM((n_tiles, NSA_TQ, NSA_TQ), F32), pltpu.VMEM((rows, 1), F32),
                        pltpu.VMEM((rows, 1), F32), pltpu.VMEM((rows, HEAD_DIM), F32)],
        compiler_params=_cparams("arbitrary", "parallel"), name="nsa_prompt",
    )(q, gates, kcv, slcg, wing, jnp.tile(q_gain, NSA_GROUP).reshape(1, gw), _block_diag_ones(gw), bc, bs, ov, ex)


NSA_QC = LANES


def _select_blocks_t(imp, pos, n_slc, k_sel):
    blk = lax.broadcasted_iota(jnp.int32, imp.shape, 0)
    cur = pos >> 6
    forced = (blk == 0) | (blk == cur) | (blk == cur - 1)
    score = jnp.where(forced, FORCE_SCORE, jnp.where(blk <= cur, imp, -1.0))
    rank = jnp.zeros(imp.shape, F32)
    for i in range(n_slc):
        ci = score[i:i + 1, :]
        rank = rank + jnp.where(ci > score, 1.0, jnp.where(ci == score, jnp.where(blk > i, 1.0, 0.0), 0.0))
    return jnp.where(rank < k_sel, 1.0, 0.0)


def _nsa_prompt_t_body(q_ref, gate_ref, kcv_ref, slc_ref, win_ref, gq_ref, bd_ref, eye_ref, fs_ref, fc_ref,
                       ov_ref, ex_ref, o_ref, bc_ref, bs_ref, bw_ref, qt_scr, ks_scr, kw_scr, vs_scr, vw_scr, sel_scr, m_scr, l_scr, acc_scr,
                       *, seq, n_slc, k_sel):
    tq = NSA_TQ
    qc = NSA_QC
    hd = HEAD_DIM
    n_tiles = seq // tq
    width = NSA_GROUP * tq
    n_chunks = width // qc
    eye = eye_ref[...]
    n_c = kcv_ref.shape[2]
    far = NSA_WINDOW // tq

    @pl.when(pl.program_id(1) == 0)
    def _():
        key_i = lax.broadcasted_iota(jnp.int32, (tq, tq), 0)
        qry_i = lax.broadcasted_iota(jnp.int32, (tq, tq), 1)
        blk_i = lax.broadcasted_iota(jnp.int32, (n_c, tq), 0)
        n_cmp = seq // CMP_STRIDE - CMP_BLOCK // CMP_STRIDE + 1
        span = tq + CMP_STRIDE * n_c
        for h in range(NSA_GROUP):
            hs = slice(h * tq, (h + 1) * tq)
            for d in range(n_tiles):
                u = jnp.broadcast_to(fs_ref[0, h:h + 1, d * tq:(d + 2) * tq], (tq, 2 * tq))
                tile = pltpu.roll(u, tq, 1, stride=1, stride_axis=0)[:, :tq]
                bs_ref[d, :, hs] = tile
                if d == far:
                    bw_ref[0, :, hs] = jnp.where(qry_i <= key_i, tile, NEG)
            for t in range(n_tiles):
                u = jnp.broadcast_to(fc_ref[0, h:h + 1, t * tq:t * tq + span], (n_c, span))
                tile = pltpu.roll(u, tq, 1, stride=CMP_STRIDE, stride_axis=0)[:, :tq]
                bc_ref[t, :, hs] = jnp.where(blk_i < n_cmp, tile, NEG)

    qn = (_seg_rms(q_ref[...], bd_ref[...], gq_ref[...]) * (hd ** -0.5)).astype(BF16)
    for t in range(n_tiles):
        rs = slice(t * tq, (t + 1) * tq)
        for h in range(NSA_GROUP):
            qt_scr[t, :, h * tq:(h + 1) * tq] = _dot_nt(eye, qn[rs, h * hd:(h + 1) * hd]).astype(BF16)
        for kv_ref, k_scr, v_scr in ((slc_ref, ks_scr, vs_scr), (win_ref, kw_scr, vw_scr)):
            kv = kv_ref[0, rs, :].astype(BF16)
            k_scr[t] = kv[:, :hd]
            v_scr[t] = _dot_nt(eye, kv[:, hd:]).astype(BF16)
    kc = kcv_ref[0, 0, :, :hd].astype(BF16)
    vct = _dot_nt(eye, kcv_ref[0, 0, :, hd:].astype(BF16)).astype(BF16)
    def flash_init():
        m_scr[...] = jnp.full((1, width), NEG, F32)
        l_scr[...] = jnp.zeros((1, width), F32)
        acc_scr[...] = jnp.zeros((hd, width), F32)

    def flash_update(k_scr, v_scr, t, tiles, running):
        s_alls = [_dot(k_scr[jt], qt_scr[t]) for jt, _, _ in tiles]
        if running:
            m_all = m_scr[...]
            l_all = l_scr[...]
        alphas, ls, ms, ps = [], [], [], [[] for _ in tiles]
        for c in range(n_chunks):
            cs = slice(c * qc, (c + 1) * qc)
            qoff = (c * qc) % tq
            ss = []
            for (_, bias_tile, sel_add), s_all in zip(tiles, s_alls):
                s = s_all[:, cs] + bias_tile(cs)
                ss.append(s if sel_add is None else s + sel_add[:, qoff:qoff + qc])
            m_new = functools.reduce(jnp.maximum, [jnp.max(s, axis=0, keepdims=True) for s in ss])
            if running:
                m_new = jnp.maximum(m_all[:, cs], m_new)
                alpha = jnp.exp(m_all[:, cs] - m_new)
                alphas.append(alpha)
            l_new = alpha * l_all[:, cs] if running else 0.0
            for k, s in enumerate(ss):
                p = jnp.exp(s - m_new)
                l_new = l_new + jnp.sum(p, axis=0, keepdims=True)
                ps[k].append(p.astype(BF16))
            ls.append(l_new)
            ms.append(m_new)
        cat = lambda xs: jnp.concatenate(xs, axis=1)
        pv = sum(_dot(v_scr[jt], cat(ps[k])) for k, (jt, _, _) in enumerate(tiles))
        acc_scr[...] = cat(alphas) * acc_scr[...] + pv if running else pv
        m_scr[...] = cat(ms)
        l_scr[...] = cat(ls)

    def flash_result():
        return acc_scr[...] / jnp.maximum(l_scr[...], 1e-30)

    def q_tile(t, carry):
        bias_c = bc_ref[t]
        s_c = _dot(kc, qt_scr[t]) + bias_c
        m_c = jnp.max(s_c, axis=0, keepdims=True)
        e_c = jnp.where(bias_c > NEG_TEST, jnp.exp(s_c - m_c), 0.0)
        p_c = (e_c / jnp.maximum(jnp.sum(e_c, axis=0, keepdims=True), 1e-30)).astype(BF16)
        o_c = _dot(vct, p_c)
        imp = sum(_dot(ov_ref[...], p_c[:, h * tq:(h + 1) * tq]) for h in range(NSA_GROUP))
        pos = t * tq + lax.broadcasted_iota(jnp.int32, (1, tq), 1)
        sel = _select_blocks_t(imp[:n_slc], pos, n_slc, k_sel)
        sel_scr[...] = jnp.zeros(sel_scr.shape, BF16)
        sel_scr[0:n_slc, :] = ((1.0 - sel) * NEG).astype(BF16)
        flash_init()
        slc_tile = lambda jt: (jt, lambda cs: bs_ref[t - jt, :, cs], _dot(ex_ref[jt], sel_scr[...]))

        def slc_pair(i, carry):
            flash_update(ks_scr, vs_scr, t, [slc_tile(2 * i), slc_tile(2 * i + 1)], True)
            return carry

        lax.fori_loop(0, (t + 1) // 2, slc_pair, 0)
        pl.when((t & 1) == 0)(lambda: flash_update(ks_scr, vs_scr, t, [slc_tile(t)], True))
        o_s = flash_result()
        far = NSA_WINDOW // tq
        win_tile = lambda d: (t - d, (lambda cs: bw_ref[0, :, cs]) if d == far else (lambda cs: bs_ref[d, :, cs]),
                              None)
        for n_back in range(far + 1):
            cond = (t == n_back) if n_back < far else (t >= far)
            pl.when(cond)(functools.partial(flash_update, kw_scr, vw_scr, t,
                                            [win_tile(d) for d in range(n_back, -1, -1)], False))
        o_w = flash_result()
        gates = gate_ref[0, t]
        for h in range(NSA_GROUP):
            hs = slice(h * tq, (h + 1) * tq)
            o_ref[t, h * hd:(h + 1) * hd, :] = (gates[3 * h:3 * h + 1] * o_c[:, hs]
                                                + gates[3 * h + 1:3 * h + 2] * o_s[:, hs]
                                                + gates[3 * h + 2:3 * h + 3] * o_w[:, hs])
        return carry

    lax.fori_loop(0, n_tiles, q_tile, 0)


def _nsa_prompt_t_tables(rel_bias, seq, n_c):
    tq = NSA_TQ
    n_tiles = seq // tq
    n_slc = -(-seq // SEL_BLOCK)
    ds = jnp.arange((n_tiles + 1) * tq) - tq
    fs = _bias_tile(rel_bias, ds, ds >= 0).reshape(NSA_KV, NSA_GROUP, -1)
    dc = jnp.arange(n_tiles * tq + CMP_STRIDE * n_c) - (CMP_STRIDE * n_c + CMP_BLOCK - 1)
    fc = _bias_tile(rel_bias, dc, dc >= 0).reshape(NSA_KV, NSA_GROUP, -1)
    n_cmp = seq // CMP_STRIDE - CMP_BLOCK // CMP_STRIDE + 1
    c_lo = np.arange(n_c) * CMP_STRIDE
    s_lo = np.arange(LANES) * SEL_BLOCK
    ov = ((c_lo[None, :] <= s_lo[:, None] + SEL_BLOCK - 1) & (c_lo[None, :] + CMP_BLOCK - 1 >= s_lo[:, None])
          & (np.arange(n_c) < n_cmp)[None, :] & (np.arange(LANES) < n_slc)[:, None])
    ex = (np.arange(seq)[:, None] // SEL_BLOCK == np.arange(LANES)[None, :]).reshape(n_tiles, tq, LANES)
    as_bf = lambda a: jnp.asarray(a.astype(np.float32), BF16)
    return fs, fc, as_bf(ov), as_bf(ex), n_slc


def _nsa_prompt_t(q, gates_t, kcv, slcg, wing, q_gain, tables, *, batch, seq):
    fs, fc, ov, ex, n_slc = tables
    n_c = kcv.shape[2]
    tq = NSA_TQ
    n_tiles = seq // tq
    assert n_tiles > NSA_WINDOW // tq
    gw = NSA_GROUP * HEAD_DIM
    width = NSA_GROUP * tq
    per_g = lambda shape: pl.BlockSpec((1,) + shape, lambda g, b: (g,) + (0,) * len(shape),
                                       pipeline_mode=pl.Buffered(1))
    eye = jnp.eye(HEAD_DIM, dtype=BF16)
    return pl.pallas_call(
        functools.partial(_nsa_prompt_t_body, seq=seq, n_slc=n_slc, k_sel=min(SEL_TOPK, n_slc)),
        grid=(NSA_KV, batch),
        in_specs=[pl.BlockSpec((seq, gw), lambda g, b: (b, g)),
                  pl.BlockSpec((1, n_tiles, 16, tq), lambda g, b: (g, b, 0, 0)),
                  pl.BlockSpec((1, 1, n_c, LANES), lambda g, b: (b, g, 0, 0)),
                  pl.BlockSpec((1, seq, LANES), lambda g, b: (g, b, 0)),
                  pl.BlockSpec((1, seq, LANES), lambda g, b: (g, b, 0)),
                  _const((1, gw)), _const(_block_diag_ones(gw).shape), _const(eye.shape),
                  per_g(fs.shape[1:]), per_g(fc.shape[1:]), _const(ov.shape), _const(ex.shape)],
        out_specs=pl.BlockSpec((n_tiles, gw, tq), lambda g, b: (b, g, 0)),
        out_shape=jax.ShapeDtypeStruct((batch * n_tiles, NSA_KV * gw, tq), F32),
        scratch_shapes=[pltpu.VMEM((n_tiles, n_c, width), F32), pltpu.VMEM((n_tiles, tq, width), F32),
                        pltpu.VMEM((1, tq, width), F32), pltpu.VMEM((n_tiles, HEAD_DIM, width), BF16),
                        pltpu.VMEM((n_tiles, tq, HEAD_DIM), BF16), pltpu.VMEM((n_tiles, tq, HEAD_DIM), BF16),
                        pltpu.VMEM((n_tiles, HEAD_DIM, tq), BF16), pltpu.VMEM((n_tiles, HEAD_DIM, tq), BF16),
                        pltpu.VMEM((LANES, tq), BF16), pltpu.VMEM((1, width), F32), pltpu.VMEM((1, width), F32),
                        pltpu.VMEM((HEAD_DIM, width), F32)],
        compiler_params=_cparams("arbitrary", "arbitrary"), name="nsa_prompt",
    )(q, gates_t, kcv, slcg, wing, jnp.tile(q_gain, NSA_GROUP).reshape(1, gw), _block_diag_ones(gw), eye,
      fs, fc, ov, ex)


SMP_PP = 8
NSA_KW = NSA_KV * HEAD_DIM
SEL_LANES = 2 * LANES


def _nsa_sample_body(pt_ref, q_ref, gate_ref, ckv_ref, *refs, n_slc, k_sel, past, tq):
    pages = refs[:SMP_PP]
    (slcn_ref, wst_ref, winn_ref, gq_ref, bd_ref, bc_ref, bs_ref, bsn_ref, bw_ref, bwn_ref, ov_ref, ex_ref, exn_ref,
     o_ref, nwin_ref, qbd_scr, sel_scr, kbuf_scr, vbuf_scr, s_scr, sn_scr, vn_scr, oc_scr, ow_scr,
     pad_scr) = refs[SMP_PP:]
    j = pl.program_id(1)
    hd = HEAD_DIM
    kw = NSA_KW
    rows = NSA_HEADS * tq
    init = (jnp.full((rows, 1), NEG, F32), jnp.zeros((rows, 1), F32), jnp.zeros((rows, kw), F32))

    @pl.when(j == 0)
    def _():
        qn = _seg_rms(q_ref[...], bd_ref[...], gq_ref[...]) * (hd ** -0.5)
        zero = jnp.zeros((tq, hd), F32)
        blocks = []
        for h in range(NSA_HEADS):
            qh = qn[:, hd * h:hd * (h + 1)]
            blocks.append(jnp.concatenate([qh if g == h // NSA_GROUP else zero for g in range(NSA_KV)], axis=1))
        qbd = jnp.concatenate(blocks, axis=0).astype(BF16)
        qbd_scr[...] = qbd
        ckv = ckv_ref[0]
        bias_c = bc_ref[...]
        ok_c = bias_c > NEG_TEST
        s_c = _dot_nt(qbd, ckv[:, :kw].astype(BF16)) + bias_c
        m_c = jnp.max(s_c, axis=-1, keepdims=True)
        e_c = jnp.where(ok_c, jnp.exp(s_c - m_c), 0.0)
        p_c = (e_c / jnp.maximum(jnp.sum(e_c, axis=-1, keepdims=True), 1e-30)).astype(BF16)
        oc_scr[...] = _dot(p_c, ckv[:, kw:].astype(BF16))
        d = _dot(p_c, ov_ref[...])
        gr = NSA_GROUP * tq
        imp = jnp.concatenate(
            [sum(d[g * gr + h * tq:g * gr + (h + 1) * tq] for h in range(NSA_GROUP)) for g in range(NSA_KV)], axis=0)
        pos = past + (lax.broadcasted_iota(jnp.int32, (NSA_KV * tq, 1), 0) & (tq - 1))
        sel = _select_blocks(imp, pos, n_slc, k_sel)
        sel = jnp.concatenate([sel[g * tq:(g + 1) * tq] for g in range(NSA_KV) for _ in range(NSA_GROUP)], axis=0)
        sel = ((1.0 - sel) * NEG).astype(BF16)
        sel_scr[...] = sel
        pad_scr[...] = jnp.zeros(pad_scr.shape, F32)
        pad_scr[0:tq, :] = slcn_ref[...]
        kn = pad_scr[...].astype(BF16)
        sn_scr[...] = _dot_nt(qbd, kn[:, :kw]) + bsn_ref[...] + _dot(sel, exn_ref[...])
        vn_scr[...] = kn[:, kw:]
        pad_scr[0:tq, :] = winn_ref[...]
        nwin_ref[0] = _shifted_buffer(wst_ref[0], pad_scr[...], tq)
        wst = wst_ref[0].astype(BF16)
        wn = pad_scr[...].astype(BF16)
        bias_w = bw_ref[...]
        mw = _softmax_piece(_dot(qbd, wst[:kw]) + bias_w, bias_w > NEG_TEST, *init, wst[kw:], v_t=True)
        bias_wn = bwn_ref[...]
        mw = _softmax_piece(_dot_nt(qbd, wn[:, :kw]) + bias_wn, bias_wn > NEG_TEST, *mw, wn[:, kw:])
        ow_scr[...] = mw[2] / jnp.maximum(mw[1], 1e-30)

    for i in range(SMP_PP):
        pg = pages[i][0].astype(BF16)
        kbuf_scr[:, i * PAGE:(i + 1) * PAGE] = pg[:kw]
        vbuf_scr[j, :, i * PAGE:(i + 1) * PAGE] = pg[kw:]
    s_scr[j] = _dot(qbd_scr[...], kbuf_scr[...]) + bs_ref[j] + _dot(sel_scr[...], ex_ref[j])

    @pl.when(j == pl.num_programs(1) - 1)
    def _():
        s_all = s_scr[...]
        s_new = sn_scr[...]
        m = jnp.maximum(jnp.max(jnp.max(s_all, axis=0), axis=-1, keepdims=True),
                        jnp.max(s_new, axis=-1, keepdims=True))
        p_new = jnp.exp(s_new - m)
        den = jnp.sum(p_new, axis=-1, keepdims=True)
        acc = _dot(p_new.astype(BF16), vn_scr[...])
        for st in range(s_all.shape[0]):
            p = jnp.exp(s_all[st] - m)
            den = den + jnp.sum(p, axis=-1, keepdims=True)
            acc = acc + _dot_nt(p.astype(BF16), vbuf_scr[st])
        o_s = acc / jnp.maximum(den, 1e-30)
        o_c = oc_scr[...]
        o_w = ow_scr[...]
        outs = []
        for h in range(NSA_HEADS):
            g, hg = divmod(h, NSA_GROUP)
            gt = gate_ref[g]
            rs = slice(h * tq, (h + 1) * tq)
            ls = slice(hd * g, hd * (g + 1))
            outs.append(gt[:, 3 * hg:3 * hg + 1] * o_c[rs, ls] + gt[:, 3 * hg + 1:3 * hg + 2] * o_s[rs, ls]
                        + gt[:, 3 * hg + 2:3 * hg + 3] * o_w[rs, ls])
        o_ref[...] = jnp.concatenate(outs, axis=1)


def _nsa_sample_tables(rel_bias, past, tq, n_c, w_state):
    rows = NSA_HEADS * tq
    total = past + tq
    n_cmp = total // CMP_STRIDE - CMP_BLOCK // CMP_STRIDE + 1
    n_slc = -(-total // SEL_BLOCK)
    pos = past + jnp.arange(tq)
    c_hi = jnp.arange(n_c) * CMP_STRIDE + CMP_BLOCK - 1
    dist_c = pos[:, None] - c_hi[None, :]
    bc = _bias_tile(rel_bias, dist_c, (dist_c >= 0) & (jnp.arange(n_c) < n_cmp)[None, :]).reshape(rows, n_c)
    n_steps = past // (SMP_PP * PAGE)
    d_ext = jnp.arange(tq + past - 1) + 1
    bs = _toeplitz(_bias_tile(rel_bias, d_ext, d_ext >= 0), tq, past)
    bs = bs.reshape(rows, n_steps, SMP_PP * PAGE).transpose(1, 0, 2)
    jn = jnp.arange(PAGE)
    dist_n = jnp.arange(tq)[:, None] - jn[None, :]
    new_ok = (dist_n >= 0) & (jn < tq)[None, :]
    bsn = _bias_tile(rel_bias, dist_n, new_ok).reshape(rows, PAGE)
    dist_w = jnp.arange(tq)[:, None] + w_state - jnp.arange(w_state)[None, :]
    bw = _bias_tile(rel_bias, dist_w, (dist_w >= 0) & (dist_w <= NSA_WINDOW)).reshape(rows, w_state)
    bwn = _bias_tile(rel_bias, dist_n, new_ok & (dist_n <= NSA_WINDOW)).reshape(rows, PAGE)
    c_lo = np.arange(n_c) * CMP_STRIDE
    s_lo = np.arange(SEL_LANES) * SEL_BLOCK
    ov = ((c_lo[:, None] <= s_lo[None, :] + SEL_BLOCK - 1) & (c_lo[:, None] + CMP_BLOCK - 1 >= s_lo[None, :])
          & (np.arange(n_c) < n_cmp)[:, None] & (np.arange(SEL_LANES) < n_slc)[None, :])
    blk = np.arange(SEL_LANES)
    ex = (blk[:, None] == (np.arange(past)[None, :] // SEL_BLOCK)).reshape(SEL_LANES, n_steps, SMP_PP * PAGE)
    ex = ex.transpose(1, 0, 2)
    exn = (blk[:, None] == ((past + np.arange(PAGE))[None, :] // SEL_BLOCK)) & (np.arange(PAGE) < tq)[None, :]
    as_bf = lambda a: jnp.asarray(a.astype(np.float32), BF16)
    return bc, bs, bsn, bw, bwn, as_bf(ov), as_bf(ex), as_bf(exn), n_slc


def _nsa_sample(q, gates, crows, slc_pool_t, table, slc_new, win_state_t, win_new, q_gain, tables, *, tq):
    bc, bs, bsn, bw, bwn, ov, ex, exn, n_slc = tables
    bsz, npg = table.shape
    past = npg * PAGE
    n_c = crows.shape[1]
    w_state = win_state_t.shape[2]
    rows = NSA_HEADS * tq
    qw = NSA_HEADS * HEAD_DIM
    page = lambda i: pl.BlockSpec((1, ROWS_NSA, PAGE), lambda b, j, pt: (pt[b, j * SMP_PP + i], 0, 0))
    per_b = lambda shape: pl.BlockSpec((1,) + shape, lambda b, j, pt: (b,) + (0,) * len(shape))
    new_rows = pl.BlockSpec((tq, ROWS_NSA), lambda b, j, pt: (b, 0))
    return pl.pallas_call(
        functools.partial(_nsa_sample_body, n_slc=n_slc, k_sel=min(SEL_TOPK, n_slc), past=past, tq=tq),
        grid_spec=pltpu.PrefetchScalarGridSpec(
            num_scalar_prefetch=1, grid=(bsz, npg // SMP_PP),
            in_specs=[pl.BlockSpec((tq, qw), lambda b, j, pt: (b, 0)),
                      pl.BlockSpec((NSA_KV, tq, LANES), lambda b, j, pt: (0, b, 0)),
                      per_b((n_c, ROWS_NSA))] + [page(i) for i in range(SMP_PP)] + [
                      new_rows, per_b((ROWS_NSA, w_state)), new_rows,
                      _const((1, qw)), _const(_block_diag_ones(qw).shape), _const(bc.shape), _const(bs.shape), _const(bsn.shape),
                      _const(bw.shape), _const(bwn.shape), _const(ov.shape), _const(ex.shape), _const(exn.shape)],
            out_specs=[pl.BlockSpec((tq, qw), lambda b, j, pt: (b, 0)), per_b((ROWS_NSA, w_state))],
            scratch_shapes=[pltpu.VMEM((rows, NSA_KW), BF16), pltpu.VMEM((rows, SEL_LANES), BF16),
                            pltpu.VMEM((NSA_KW, SMP_PP * PAGE), BF16),
                            pltpu.VMEM((npg // SMP_PP, NSA_KW, SMP_PP * PAGE), BF16),
                            pltpu.VMEM((npg // SMP_PP, rows, SMP_PP * PAGE), F32),
                            pltpu.VMEM((rows, PAGE), F32), pltpu.VMEM((PAGE, NSA_KW), BF16),
                            pltpu.VMEM((rows, NSA_KW), F32), pltpu.VMEM((rows, NSA_KW), F32),
                            pltpu.VMEM((PAGE, ROWS_NSA), F32)]),
        out_shape=[jax.ShapeDtypeStruct((bsz * tq, qw), F32), jax.ShapeDtypeStruct(win_state_t.shape, F32)],
        compiler_params=_cparams("parallel", "arbitrary"), name="nsa_sample",
    )(table, q, gates, crows, *([slc_pool_t] * SMP_PP), slc_new, win_state_t, win_new,
      jnp.tile(q_gain, NSA_HEADS).reshape(1, qw), _block_diag_ones(qw), bc, bs, bsn, bw, bwn, ov, ex, exn)


DIL_TQ = 128


def _head_of_lane(shape):
    return lax.broadcasted_iota(jnp.int32, shape, 1) >> 6


def _stack_heads(qt, lane_head):
    return jnp.concatenate([jnp.where(lane_head == h, qt, 0.0) for h in range(DIL_HEADS)], axis=0).astype(BF16)


def _unstack_heads(r, lane_head, t):
    return sum(jnp.where(lane_head == h, r[h * t:(h + 1) * t], 0.0) for h in range(DIL_HEADS))


def _dil_prompt_body(q_ref, kv_ref, bias_ref, o_ref, lse_ref, *, n_sub, dil):
    tq = DIL_TQ
    gw = ODD_GW
    lane_head = _head_of_lane((tq, gw))
    for r in range(dil):
        ql = slice(gw * r, gw * (r + 1))
        kl = slice(2 * gw * r, 2 * gw * r + gw)
        vl = slice(2 * gw * r + gw, 2 * gw * (r + 1))

        def tile(t, carry, ql=ql, kl=kl, vl=vl):
            start = pl.multiple_of(t * tq, tq)
            prev = pl.multiple_of(jnp.maximum(t - 1, 0) * tq, tq)
            cur_rows = pl.ds(start, tq)
            prev_rows = pl.ds(prev, tq)
            kk = jnp.concatenate([kv_ref[prev_rows, kl], kv_ref[cur_rows, kl]], axis=0).astype(BF16)
            vv = jnp.concatenate([kv_ref[prev_rows, vl], kv_ref[cur_rows, vl]], axis=0).astype(BF16)
            bias = bias_ref[jnp.minimum(t, 1)]
            s = _dot_nt(_stack_heads(q_ref[cur_rows, ql], lane_head), kk) + bias
            m = jnp.max(s, axis=-1, keepdims=True)
            e = jnp.where(bias > NEG_TEST, jnp.exp(s - m), 0.0)
            l = jnp.maximum(jnp.sum(e, axis=-1, keepdims=True), 1e-30)
            res = _dot((e / l).astype(BF16), vv)
            lse = jnp.broadcast_to(m + jnp.log(l), (DIL_HEADS * tq, gw))
            o_ref[cur_rows, ql] = _unstack_heads(res, lane_head, tq)
            lse_ref[cur_rows, ql] = _unstack_heads(lse, lane_head, tq)
            return carry

        lax.fori_loop(0, n_sub // tq, tile, 0)


def _merge_groups(outs, lses):
    mx = functools.reduce(jnp.maximum, lses)
    ws = [jnp.exp(l - mx) for l in lses]
    return sum(w * o for w, o in zip(ws, outs)) / sum(ws)


def _dil_prompt_tables(rel_bias):
    tq = DIL_TQ
    m_ext = jnp.arange(3 * tq - 1) - (tq - 1)
    cur_tile = jnp.arange(2 * tq) >= tq
    tiles = []
    for gi, (win, dil) in enumerate(DIL_PATTERN):
        assert win // dil == tq
        t = _toeplitz(_bias_tile(rel_bias, m_ext * dil, (m_ext >= 0) & (m_ext <= tq)), tq, 2 * tq)
        t = t[gi * DIL_HEADS:(gi + 1) * DIL_HEADS].reshape(DIL_HEADS * tq, 2 * tq)
        tiles.append(jnp.stack([jnp.where(cur_tile, t, NEG), t]))
    return tiles


def _dil_prompt(q_views, kv_views, tables, *, batch, seq):
    gw = ODD_GW
    outs, lses = [], []
    for gi, (_, dil) in enumerate(DIL_PATTERN):
        n_sub = seq // dil
        assert n_sub % DIL_TQ == 0
        qv, kvv = q_views[gi], kv_views[gi]
        o_spec = pl.BlockSpec((n_sub, dil * gw), lambda b: (b, 0))
        o, lse = pl.pallas_call(
            functools.partial(_dil_prompt_body, n_sub=n_sub, dil=dil), grid=(batch,),
            in_specs=[o_spec, pl.BlockSpec((n_sub, dil * 2 * gw), lambda b: (b, 0)), _const(tables[gi].shape)],
            out_specs=[o_spec, o_spec],
            out_shape=[jax.ShapeDtypeStruct(qv.shape, F32)] * 2,
            compiler_params=_cparams("parallel"), name=f"dilated_prompt_g{gi}",
        )(qv, kvv, tables[gi])
        outs.append(o)
        lses.append(lse)
    return outs, lses


def _shifted_buffer(st, new_rows_padded, tq):
    ln = st.shape[1]
    shifted = pltpu.roll(st, ln - tq, 1)
    new_t = pltpu.roll(new_rows_padded.T, LANES - tq, 1)
    lane = lax.broadcasted_iota(jnp.int32, (st.shape[0], LANES), 1)
    tail = jnp.where(lane >= LANES - tq, new_t, shifted[:, ln - LANES:])
    return tail if ln == LANES else jnp.concatenate([shifted[:, :ln - LANES], tail], axis=1)


def _dil_sample_body(q_ref, kvn_ref, st0_ref, st1_ref, st2_ref, b0_ref, b1_ref, b2_ref, bn_ref, o_ref, lse_ref,
                     n0_ref, n1_ref, n2_ref, pad_scr, *, tq):
    gw = ODD_GW
    lane_head = _head_of_lane((tq, gw))
    rows = DIL_HEADS * tq
    pad_scr[...] = jnp.zeros(pad_scr.shape, F32)
    for gi, (st_ref, b_ref, new_ref) in enumerate(((st0_ref, b0_ref, n0_ref), (st1_ref, b1_ref, n1_ref),
                                                    (st2_ref, b2_ref, n2_ref))):
        qs = _stack_heads(q_ref[gi], lane_head)
        pad_scr[0:tq, :] = kvn_ref[gi]
        new_ref[0] = _shifted_buffer(st_ref[0], pad_scr[...], tq)
        st = st_ref[0].astype(BF16)
        kn = pad_scr[...].astype(BF16)
        bias = b_ref[...]
        acc = (jnp.full((rows, 1), NEG, F32), jnp.zeros((rows, 1), F32), jnp.zeros((rows, gw), F32))
        acc = _softmax_piece(_dot(qs, st[:gw]) + bias, bias > NEG_TEST, *acc, st[gw:], v_t=True)
        bias_n = bn_ref[gi]
        m, l, a = _softmax_piece(_dot_nt(qs, kn[:, :gw]) + bias_n, bias_n > NEG_TEST, *acc, kn[:, gw:])
        l = jnp.maximum(l, 1e-30)
        o_ref[gi] = _unstack_heads(a / l, lane_head, tq)
        lse_ref[gi] = _unstack_heads(jnp.broadcast_to(m + jnp.log(l), (rows, gw)), lane_head, tq)


def _dil_sample_tables(rel_bias, tq, state_lens):
    rows = DIL_HEADS * tq
    i = jnp.arange(tq)[:, None]
    per_state, per_new = [], []
    jn = jnp.arange(PAGE)[None, :]
    for gi, ((win, dil), ln) in enumerate(zip(DIL_PATTERN, state_lens)):
        hs = slice(gi * DIL_HEADS, (gi + 1) * DIL_HEADS)
        d = i + ln - jnp.arange(ln)[None, :]
        per_state.append(_bias_tile(rel_bias, d, (d % dil == 0) & (d <= win))[hs].reshape(rows, ln))
        dn = i - jn
        per_new.append(_bias_tile(rel_bias, dn, (dn >= 0) & (dn % dil == 0) & (dn <= win) & (jn < tq))[hs]
                       .reshape(rows, PAGE))
    return per_state, jnp.stack(per_new)


def _dil_sample(q3, kv3, states, tables, *, tq):
    per_state, bias_new = tables
    bsz = states[0].shape[0]
    gw = ODD_GW
    st_spec = lambda s: pl.BlockSpec((1,) + s.shape[1:], lambda b: (b, 0, 0))
    grp = lambda w: pl.BlockSpec((3, tq, w), lambda b: (0, b, 0))
    return pl.pallas_call(
        functools.partial(_dil_sample_body, tq=tq), grid=(bsz,),
        in_specs=[grp(gw), grp(2 * gw)] + [st_spec(s) for s in states] + [_const(t.shape) for t in per_state]
        + [_const(bias_new.shape)],
        out_specs=[grp(gw), grp(gw)] + [st_spec(s) for s in states],
        out_shape=[jax.ShapeDtypeStruct((3, bsz * tq, gw), F32)] * 2
        + [jax.ShapeDtypeStruct(s.shape, F32) for s in states],
        scratch_shapes=[pltpu.VMEM((PAGE, 2 * gw), F32)],
        compiler_params=_cparams("parallel"), name="dilated_sample",
    )(q3, kv3, *states, *per_state, bias_new)


def _pad_even_w(w_in):
    return jnp.concatenate([w_in, jnp.zeros((w_in.shape[0], EVEN_PAD - w_in.shape[1]), w_in.dtype)], axis=1).astype(BF16)


def _even_mixer_prompt(h, gain, w_in, w_out, lb, hg_norm, qk_norm, cmp_pe, cmp_w1, cmp_w2, rel_bias, *, batch, seq, tm):
    n = batch * seq
    hg, q, cmp_rows, _, _, slcg, wing, _, gates_t, cmp_t, slc_t, win_t = _inproj_even(
        h, gain, _pad_even_w(w_in), qk_norm, tm=tm, gates_t=True, seq=seq)
    s0 = jnp.zeros((batch, HG_HEADS, HEAD_DIM, HEAD_DIM), F32)
    o_h, s_fin = _hgrn(hg, lb, hg_norm, s0, batch=batch, seq=seq)
    cw = _compress_weights(cmp_pe, cmp_w1, cmp_w2, qk_norm[1])
    npg = seq // PAGE
    table = jnp.arange(batch * npg, dtype=jnp.int32).reshape(batch, npg)
    _, kcv = _compress(cmp_rows.reshape(n // PAGE, PAGE, ROWS_NSA), table, cw, transposed=False)
    tables = _nsa_prompt_t_tables(rel_bias, seq, kcv.shape[2])
    o_nt = _nsa_prompt_t(q, gates_t, kcv, slcg, wing, qk_norm[0], tables, batch=batch, seq=seq)
    wo = w_out.astype(BF16)
    hw = HG_HEADS * HEAD_DIM
    h = _outproj2t(h, o_h, o_nt, wo[:hw], wo[hw:], tm=tm)
    keep = min(NSA_WINDOW, seq)
    return h, {"hgrn": s_fin, "cmp": _rows_from_t(cmp_t, NSA_KV), "slc": _rows_from_t(slc_t, NSA_KV),
               "win": _rows_from_t(win_t[:, :, seq - keep:], NSA_KV)}


def _rows_from_t(a, heads):
    return jnp.moveaxis(a.reshape(a.shape[0], 2, heads, HEAD_DIM, a.shape[2]), -1, 1)


def _even_mixer_sample(h, gain, w_in, w_out, lb, hg_norm, qk_norm, cmp_pe, cmp_w1, cmp_w2, rel_bias,
                       state_hgrn, cmp_pool, slc_pool, win_state, page_table, *, batch, tq, tm):
    assert tq < CMP_STRIDE and tq & (tq - 1) == 0
    hg, q, cmp_rows, slc_rows, win_rows, _, _, gates = _inproj_even(h, gain, _pad_even_w(w_in), qk_norm, tm=tm)
    o_h, s_fin = _hgrn(hg, lb, hg_norm, state_hgrn, batch=batch, seq=tq)
    cw = _compress_weights(cmp_pe, cmp_w1, cmp_w2, qk_norm[1])
    rows_t = lambda a: jnp.moveaxis(a, 1, -1).reshape(a.shape[0], ROWS_NSA, a.shape[1])
    crows, _ = _compress(rows_t(cmp_pool), page_table, cw, transposed=True)
    past = page_table.shape[1] * PAGE
    w_state = win_state.shape[1]
    tables = _nsa_sample_tables(rel_bias, past, tq, crows.shape[1], w_state)
    o_n, new_win_t = _nsa_sample(q, gates, crows, rows_t(slc_pool), page_table, slc_rows, rows_t(win_state), win_rows,
                                 qk_norm[0], tables, tq=tq)
    wo = w_out.astype(BF16)
    hw = HG_HEADS * HEAD_DIM
    h = _outproj2(h, o_h, o_n, wo[:hw], wo[hw:], tm=tm)
    new_win = jnp.moveaxis(new_win_t.reshape((batch,) + win_state.shape[2:] + (w_state,)), -1, 1)
    return h, {"hgrn": s_fin, "cmp": cmp_rows, "slc": slc_rows, "win": new_win}


def _odd_mixer_prompt(h, gain, w_in, w_out, qk_norm, rel_bias, *, batch, seq, tm):
    q3, kv3, qv1, kvv1, qv2, kvv2, *kv_t = _inproj_odd(h, gain, w_in.astype(BF16), qk_norm, tm=tm, views=True,
                                                        seq=seq)
    outs, lses = _dil_prompt([q3[0], qv1, qv2], [kv3[0], kvv1, kvv2], _dil_prompt_tables(rel_bias),
                             batch=batch, seq=seq)
    bufs = [_rows_from_t(t[:, :, seq - min(w, seq):], DIL_HEADS) for t, (w, _) in zip(kv_t, DIL_PATTERN)]
    return _outproj_merge(h, outs, lses, w_out.astype(BF16), tm=tm, views=True), bufs


def _odd_mixer_sample(h, gain, w_in, w_out, qk_norm, rel_bias, states, *, batch, tq, tm):
    q3, kv3 = _inproj_odd(h, gain, w_in.astype(BF16), qk_norm, tm=tm)
    tables = _dil_sample_tables(rel_bias, tq, [s.shape[1] for s in states])
    states_t = [jnp.moveaxis(s, 1, -1).reshape(batch, 2 * ODD_GW, s.shape[1]) for s in states]
    o3, lse3, *new_t = _dil_sample(q3, kv3, states_t, tables, tq=tq)
    new = [jnp.moveaxis(n.reshape(s.shape[:1] + s.shape[2:] + s.shape[1:2]), -1, 1) for n, s in zip(new_t, states)]
    h = _outproj_merge(h, [o3[g] for g in range(3)], [lse3[g] for g in range(3)], w_out.astype(BF16), tm=tm)
    return h, new


TM_PROMPT = 512
TM_SAMPLE = 256


def kernel(x_prompt, x_sample, p_prompt, p_sample, state_hgrn, cache_nsa_cmp_kv, cache_nsa_slc_kv, state_nsa_win_kv,
           state_dil_kv_0, state_dil_kv_1, state_dil_kv_2, page_table, rel_bias, norm_ffn1, norm_mix, norm_ffn2,
           norm_ple, w_ffn1_in, w_ffn1_out, w_ffn2_in, w_ffn2_out, w_ple_gate, w_ple_proj, w_in_even, w_out_even,
           hgrn_lb_logits, hgrn_norm, nsa_qk_norm, nsa_cmp_pe, nsa_cmp_w1, nsa_cmp_w2, w_in_odd, w_out_odd,
           dil_qk_norm):
    depth = norm_ffn1.shape[0]
    dil_states = (state_dil_kv_0, state_dil_kv_1, state_dil_kv_2)
    bf = lambda w: w.astype(BF16)
    w1i, w1o, w2i, w2o, wpg, wpp = (bf(w) for w in (w_ffn1_in, w_ffn1_out, w_ffn2_in, w_ffn2_out, w_ple_gate,
                                                     w_ple_proj))
    lb_all = jnp.cumsum(jax.nn.softmax(hgrn_lb_logits.astype(F32), axis=0), axis=0)

    def run(x, p, sample):
        batch, seq, d = x.shape
        n = batch * seq
        tm = TM_SAMPLE if sample else TM_PROMPT
        h = x.reshape(n, d)
        p = p.reshape(depth, n, p.shape[-1])
        even, odd = [], []
        for li in range(depth):
            h = _ffn(h, norm_ffn1[li], w1i[li], w1o[li], tm=tm)
            if li % 2 == 0:
                ei = li // 2
                args = (h, norm_mix[li], w_in_even[ei], w_out_even[ei], lb_all[ei], hgrn_norm[ei], nsa_qk_norm[ei],
                        nsa_cmp_pe[ei], nsa_cmp_w1[ei], nsa_cmp_w2[ei], rel_bias)
                if sample:
                    h, st = _even_mixer_sample(*args, state_hgrn[ei], cache_nsa_cmp_kv[ei], cache_nsa_slc_kv[ei],
                                               state_nsa_win_kv[ei], page_table, batch=batch, tq=seq, tm=tm)
                else:
                    h, st = _even_mixer_prompt(*args, batch=batch, seq=seq, tm=tm)
                even.append(st)
            else:
                oi = li // 2
                args = (h, norm_mix[li], w_in_odd[oi], w_out_odd[oi], dil_qk_norm[oi], rel_bias)
                if sample:
                    h, bufs = _odd_mixer_sample(*args, [s[oi] for s in dil_states], batch=batch, tq=seq, tm=tm)
                else:
                    h, bufs = _odd_mixer_prompt(*args, batch=batch, seq=seq, tm=tm)
                odd.append(bufs)
            h = _ffn(h, norm_ffn2[li], w2i[li], w2o[li], ple=(p[li], norm_ple[li], wpg[li], wpp[li]), tm=tm)
        rows = lambda a: a.reshape(batch, -1, 2, NSA_KV, HEAD_DIM)
        drows = lambda a: a.reshape(batch, -1, 2, DIL_HEADS, HEAD_DIM)
        return (h.reshape(batch, seq, d),
                jnp.stack([s["hgrn"] for s in even]), jnp.stack([rows(s["cmp"]) for s in even]),
                jnp.stack([rows(s["slc"]) for s in even]), jnp.stack([rows(s["win"]) for s in even]),
                jnp.stack([drows(b[0]) for b in odd]), jnp.stack([drows(b[1]) for b in odd]),
                jnp.stack([drows(b[2]) for b in odd]))

    y_p, hg_p, cmp_p, slc_p, win_p, d0_p, d1_p, d2_p = run(x_prompt, p_prompt, False)
    y_s, hg_s, cmp_s, slc_s, win_s, d0_s, d1_s, d2_s = run(x_sample, p_sample, True)
    return (y_p, y_s, hg_p, hg_s, cmp_p, cmp_s, slc_p, slc_s, win_p, win_s, d0_p, d0_s, d1_p, d1_s, d2_p, d2_s)
```

```python
import functools
import math

import numpy as np
import jax
import jax.numpy as jnp
from jax import lax
from jax.experimental import pallas as pl
from jax.experimental.pallas import tpu as pltpu

F32 = jnp.float32
BF16 = jnp.bfloat16
HIGHEST = lax.Precision.HIGHEST

V7X_VMEM_BYTES = 64 * 1024 * 1024
VMEM_LIMIT = V7X_VMEM_BYTES * 7 // 8
LANES = 128

EPS = 1e-6
NEG = -1e30
NEG_TEST = -1e29

HEAD_DIM = 64
HG_HEADS = 8
HG_CHUNK = 64
NSA_HEADS = 12
NSA_KV = 3
NSA_GROUP = NSA_HEADS // NSA_KV
CMP_BLOCK = 32
CMP_STRIDE = 16
SEL_BLOCK = 64
SEL_TOPK = 16
NSA_WINDOW = 512
FORCE_SCORE = 1e4
PAGE = 128
DIL_PATTERN = ((128, 1), (512, 4), (2048, 16))
DIL_HEADS = 4
RP_BUCKETS = 32
RP_MAX_DIST = 2048


def _cparams(*sem):
    return pltpu.CompilerParams(dimension_semantics=sem, vmem_limit_bytes=VMEM_LIMIT)


def _const(shape):
    nd = len(shape)
    return pl.BlockSpec(shape, lambda *_: (0,) * nd, pipeline_mode=pl.Buffered(1))


def _dot(a, b):
    return jnp.dot(a, b, preferred_element_type=F32)


def _dot_nt(a, b):
    return lax.dot_general(a, b, (((1,), (1,)), ((), ())), preferred_element_type=F32)


def _dot_tn(a, b):
    return lax.dot_general(a, b, (((0,), (0,)), ((), ())), preferred_element_type=F32)


def _rms(x, gain):
    return x * lax.rsqrt(jnp.mean(x * x, axis=-1, keepdims=True) + EPS) * gain


def _seg_ms(x, bd):
    sq = x * x
    hi = sq.astype(BF16)
    lo = (sq - hi.astype(F32)).astype(BF16)
    bw = bd.shape[0]
    cols = []
    for c0 in range(0, x.shape[1], bw):
        wd = min(bw, x.shape[1] - c0)
        blk = bd[:wd, :wd]
        cols.append(_dot(hi[:, c0:c0 + wd], blk) + _dot(lo[:, c0:c0 + wd], blk))
    return (cols[0] if len(cols) == 1 else jnp.concatenate(cols, axis=1)) * (1.0 / HEAD_DIM)


def _seg_rms(x, bd, gain):
    return x * lax.rsqrt(_seg_ms(x, bd) + EPS) * gain


def _norm_k_rows(x, bd, gain, kmask):
    bw = bd.shape[0]
    r = lax.rsqrt(_seg_ms(x[:, :bw], bd) + EPS)
    r = jnp.concatenate([r, jnp.ones((x.shape[0], x.shape[1] - bw), F32)], axis=1)
    return x * jnp.where(kmask > 0.5, r, 1.0) * gain


V7X_MXU_WIDTH = 256


def _block_diag_ones(width):
    i = np.arange(min(width, V7X_MXU_WIDTH)) // HEAD_DIM
    return jnp.asarray((i[:, None] == i[None, :]).astype(np.float32), BF16)


def _rel_bucket(dist):
    exact = RP_BUCKETS // 2
    d = jnp.maximum(dist, 0)
    log_ratio = jnp.log(jnp.maximum(d, 1).astype(F32) / exact) / math.log(RP_MAX_DIST / exact)
    large = jnp.minimum(exact + (log_ratio * (RP_BUCKETS - exact)).astype(jnp.int32), RP_BUCKETS - 1)
    return jnp.where(d < exact, d, large)


def _bias_tile(rel_bias, dist, valid):
    onehot = (_rel_bucket(dist)[..., None] == jnp.arange(RP_BUCKETS)).astype(F32)
    t = jnp.einsum('...k,kh->h...', onehot, rel_bias.astype(F32), precision=HIGHEST)
    return jnp.where(valid[None], t, NEG)


def _toeplitz(ext, n, m):
    p = n + m - 1
    u = jnp.concatenate([jnp.flip(ext[..., :m], -1), jnp.flip(ext[..., m:], -1)], axis=-1)
    reps = (1,) * (ext.ndim - 1) + (n,)
    t = jnp.tile(u, reps)[..., :n * (p - 1)].reshape(ext.shape[:-1] + (n, p - 1))
    return t[..., :m]


def _softmax_piece(s, mask, m_prev, l_prev, acc_prev, v, v_t=False):
    m_new = jnp.maximum(m_prev, jnp.max(s, axis=-1, keepdims=True))
    alpha = jnp.exp(m_prev - m_new)
    p = jnp.where(mask, jnp.exp(s - m_new), 0.0)
    l_new = alpha * l_prev + jnp.sum(p, axis=-1, keepdims=True)
    pb = p.astype(BF16)
    acc_new = alpha * acc_prev + (_dot_nt(pb, v) if v_t else _dot(pb, v))
    return m_new, l_new, acc_new


def _ffn_body(x_ref, g_ref, win_ref, wout_ref, *rest, dff, ple):
    x = x_ref[...]
    xn = _rms(x, g_ref[...]).astype(BF16)
    gu = _dot(xn, win_ref[...])
    gg = gu[:, :dff]
    a = (gg * jax.nn.sigmoid(gg) * gu[:, dff:]).astype(BF16)
    h = x + 0.5 * _dot(a, wout_ref[...])
    if ple:
        p_ref, gp_ref, wg_ref, wp_ref, o_ref = rest
        hn = _rms(h, gp_ref[...]).astype(BF16)
        gate = jax.nn.sigmoid(_dot(hn, wg_ref[...]))
        h = h + gate * _dot(p_ref[...].astype(BF16), wp_ref[...])
    else:
        (o_ref,) = rest
    o_ref[...] = h


def _ffn(h, gain, w_in, w_out, ple=None, *, tm):
    n, d = h.shape
    dff = w_out.shape[0]
    row = lambda i: (i, 0)
    in_specs = [pl.BlockSpec((tm, d), row), _const((1, d)), _const(w_in.shape), _const(w_out.shape)]
    args = [h, gain.reshape(1, d), w_in, w_out]
    if ple is not None:
        p, gp, wg, wp = ple
        in_specs += [pl.BlockSpec((tm, p.shape[1]), row), _const((1, d)), _const(wg.shape), _const(wp.shape)]
        args += [p, gp.reshape(1, d), wg, wp]
    return pl.pallas_call(
        functools.partial(_ffn_body, dff=dff, ple=ple is not None),
        grid=(n // tm,), in_specs=in_specs, out_specs=pl.BlockSpec((tm, d), row),
        out_shape=jax.ShapeDtypeStruct((n, d), F32), compiler_params=_cparams("parallel"),
        name="ffn_ple" if ple is not None else "ffn",
    )(*args)


EVEN_HG = 4 * HG_HEADS * HEAD_DIM
EVEN_Q = NSA_HEADS * HEAD_DIM
ROWS_NSA = 2 * NSA_KV * HEAD_DIM
EVEN_PAD = EVEN_HG + EVEN_Q + 3 * ROWS_NSA + LANES


GATE_ROWS = 16


def _inproj_even_body(x_ref, g_ref, w_ref, gk_ref, km_ref, bd_ref, hg_ref, q_ref, cmp_ref, *outs, prompt):
    xn = _rms(x_ref[...], g_ref[...]).astype(BF16)
    p = _dot(xn, w_ref[...])
    c0 = EVEN_HG
    c1 = c0 + EVEN_Q
    hg_ref[...] = p[:, :c0]
    q_ref[...] = p[:, c0:c1]
    cmp = p[:, c1:c1 + ROWS_NSA]
    cmp_ref[...] = cmp
    bd = bd_ref[...]
    km = km_ref[...]
    slc = _norm_k_rows(p[:, c1 + ROWS_NSA:c1 + 2 * ROWS_NSA], bd, gk_ref[0:1, :], km)
    win = _norm_k_rows(p[:, c1 + 2 * ROWS_NSA:c1 + 3 * ROWS_NSA], bd, gk_ref[1:2, :], km)
    sig = jax.nn.sigmoid(p[:, c1 + 3 * ROWS_NSA:])
    if not prompt:
        slc_ref, win_ref, gate_ref = outs
        slc_ref[...] = slc
        win_ref[...] = win
        for g in range(NSA_KV):
            gate_ref[g] = sig if g == 0 else pltpu.roll(sig, LANES - 3 * NSA_GROUP * g, 1)
        return
    slcg_ref, wing_ref, gate_t_ref, cmp_t_ref, slc_t_ref, win_t_ref = outs
    kw = NSA_KV * HEAD_DIM
    sig_t = sig.T
    for g in range(NSA_KV):
        ks = slice(HEAD_DIM * g, HEAD_DIM * (g + 1))
        vs = slice(kw + HEAD_DIM * g, kw + HEAD_DIM * (g + 1))
        slcg_ref[g] = jnp.concatenate([slc[:, ks], slc[:, vs]], axis=1)
        wing_ref[g] = jnp.concatenate([win[:, ks], win[:, vs]], axis=1)
        for half in range(sig.shape[0] // NSA_TQ):
            gate_t_ref[g, half] = sig_t[3 * NSA_GROUP * g:3 * NSA_GROUP * g + GATE_ROWS,
                                        half * NSA_TQ:(half + 1) * NSA_TQ]
    for dst_ref, rows in ((cmp_t_ref, cmp), (slc_t_ref, slc), (win_t_ref, win)):
        _store_transposed(dst_ref, rows)


def _store_transposed(dst_ref, x):
    for c in range(x.shape[1] // LANES):
        dst_ref[0, c * LANES:(c + 1) * LANES, :] = x[:, c * LANES:(c + 1) * LANES].T


def _inproj_even(h, gain, w_pad, qk_norm, *, tm, seq=None):
    n, d = h.shape
    kw = NSA_KV * HEAD_DIM
    ones = jnp.ones((kw,), F32)
    gk = jnp.stack([jnp.concatenate([jnp.tile(qk_norm[2], NSA_KV), ones]),
                    jnp.concatenate([jnp.tile(qk_norm[3], NSA_KV), ones])])
    km = jnp.concatenate([ones, 0.0 * ones]).reshape(1, ROWS_NSA)
    row = lambda i: (i, 0)
    grp = lambda i: (0, i, 0)
    shp = lambda *s: jax.ShapeDtypeStruct(s, F32)
    out_specs = [pl.BlockSpec((tm, EVEN_HG), row), pl.BlockSpec((tm, EVEN_Q), row), pl.BlockSpec((tm, ROWS_NSA), row)]
    out_shape = [shp(n, EVEN_HG), shp(n, EVEN_Q), shp(n, ROWS_NSA)]
    if seq is None:
        out_specs += [pl.BlockSpec((tm, ROWS_NSA), row)] * 2 + [pl.BlockSpec((NSA_KV, tm, LANES), grp)]
        out_shape += [shp(n, ROWS_NSA)] * 2 + [shp(NSA_KV, n, LANES)]
    else:
        tpb = seq // tm
        out_specs += [pl.BlockSpec((NSA_KV, tm, LANES), grp)] * 2
        out_shape += [shp(NSA_KV, n, LANES)] * 2
        out_specs.append(pl.BlockSpec((NSA_KV, tm // NSA_TQ, GATE_ROWS, NSA_TQ), lambda i: (0, i, 0, 0)))
        out_shape.append(shp(NSA_KV, n // NSA_TQ, GATE_ROWS, NSA_TQ))
        out_specs += [pl.BlockSpec((1, ROWS_NSA, tm), lambda i: (i // tpb, 0, i % tpb))] * 3
        out_shape += [shp(n // seq, ROWS_NSA, seq)] * 3
    return pl.pallas_call(
        functools.partial(_inproj_even_body, prompt=seq is not None), grid=(n // tm,),
        in_specs=[pl.BlockSpec((tm, d), row), _const((1, d)), _const(w_pad.shape), _const((2, ROWS_NSA)),
                  _const((1, ROWS_NSA)), _const(_block_diag_ones(ROWS_NSA).shape)],
        out_specs=out_specs, out_shape=out_shape,
        compiler_params=_cparams("parallel"), name="inproj_even",
    )(h, gain.reshape(1, d), w_pad, gk, km, _block_diag_ones(ROWS_NSA))


ODD_W = 3 * DIL_HEADS * HEAD_DIM
ODD_GW = DIL_HEADS * HEAD_DIM


def _to_residue_view(src, dst_ref, stage_ref, dil):
    rows, w = src.shape
    for blk in range(w // LANES):
        stage_ref[...] = src[:, blk * LANES:(blk + 1) * LANES]
        for r in range(dil):
            lo = r * w + blk * LANES
            dst_ref[:, lo:lo + LANES] = stage_ref[pl.ds(r, rows // dil, stride=dil), :]


def _from_residue_view(src_ref, stage_ref, dil, w):
    rows = src_ref.shape[0] * dil
    parts = []
    for blk in range(w // LANES):
        for r in range(dil):
            lo = r * w + blk * LANES
            stage_ref[pl.ds(r, rows // dil, stride=dil), :] = src_ref[:, lo:lo + LANES]
        parts.append(stage_ref[...])
    return jnp.concatenate(parts, axis=1)


def _inproj_odd_body(x_ref, g_ref, w_ref, gq_ref, gk_ref, bd_ref, *outs, prompt):
    xn = _rms(x_ref[...], g_ref[...]).astype(BF16)
    p = _dot(xn, w_ref[...])
    bd = bd_ref[...]
    q = _seg_rms(p[:, :ODD_W], bd, gq_ref[...]) * (HEAD_DIM ** -0.5)
    k = _seg_rms(p[:, ODD_W:2 * ODD_W], bd, gk_ref[...])
    v = p[:, 2 * ODD_W:]
    for g in range(3):
        s = slice(ODD_GW * g, ODD_GW * (g + 1))
        kv = jnp.concatenate([k[:, s], v[:, s]], axis=1)
        if not prompt:
            outs[0][g] = q[:, s]
            outs[1][g] = kv
            continue
        dil = DIL_PATTERN[g][1]
        if dil == 1:
            outs[2 * g][...] = q[:, s]
            outs[2 * g + 1][...] = kv
        else:
            _to_residue_view(q[:, s], outs[2 * g], outs[-1], dil)
            _to_residue_view(kv, outs[2 * g + 1], outs[-1], dil)
        _store_transposed(outs[6 + g], kv)


def _inproj_odd(h, gain, w, qk_norm, *, tm, seq=None):
    n, d = h.shape
    row = lambda i: (i, 0)
    nh = 3 * DIL_HEADS
    prompt = seq is not None
    if not prompt:
        out_specs = [pl.BlockSpec((3, tm, ODD_GW), lambda i: (0, i, 0)),
                     pl.BlockSpec((3, tm, 2 * ODD_GW), lambda i: (0, i, 0))]
        out_shape = [jax.ShapeDtypeStruct((3, n, ODD_GW), F32), jax.ShapeDtypeStruct((3, n, 2 * ODD_GW), F32)]
    else:
        out_specs, out_shape = [], []
        for _, dil in DIL_PATTERN:
            assert dil == 1 or DIL_PATTERN[0][1] == 1
            for width in (ODD_GW, 2 * ODD_GW):
                out_specs.append(pl.BlockSpec((tm // dil, dil * width), row))
                out_shape.append(jax.ShapeDtypeStruct((n // dil, dil * width), F32))
        tpb = seq // tm
        for _ in DIL_PATTERN:
            out_specs.append(pl.BlockSpec((1, 2 * ODD_GW, tm), lambda i: (i // tpb, 0, i % tpb)))
            out_shape.append(jax.ShapeDtypeStruct((n // seq, 2 * ODD_GW, seq), F32))
    return pl.pallas_call(
        functools.partial(_inproj_odd_body, prompt=prompt), grid=(n // tm,),
        in_specs=[pl.BlockSpec((tm, d), row), _const((1, d)), _const(w.shape), _const((1, ODD_W)),
                  _const((1, ODD_W)), _const(_block_diag_ones(ODD_W).shape)],
        out_specs=out_specs, out_shape=out_shape,
        scratch_shapes=[pltpu.VMEM((tm, LANES), F32)] if prompt else [],
        compiler_params=_cparams("parallel"), name="inproj_odd",
    )(h, gain.reshape(1, d), w, jnp.tile(qk_norm[0], nh).reshape(1, ODD_W),
      jnp.tile(qk_norm[1], nh).reshape(1, ODD_W), _block_diag_ones(ODD_W))


def _outproj2_body(h_ref, a_ref, b_ref, wa_ref, wb_ref, o_ref):
    o_ref[...] = (h_ref[...] + _dot(a_ref[...].astype(BF16), wa_ref[...])
                  + _dot(b_ref[...].astype(BF16), wb_ref[...]))


def _outproj2(h, a, b, wa, wb, *, tm):
    n, d = h.shape
    row = lambda i: (i, 0)
    return pl.pallas_call(
        _outproj2_body, grid=(n // tm,),
        in_specs=[pl.BlockSpec((tm, d), row), pl.BlockSpec((tm, a.shape[1]), row),
                  pl.BlockSpec((tm, b.shape[1]), row), _const(wa.shape), _const(wb.shape)],
        out_specs=pl.BlockSpec((tm, d), row), out_shape=jax.ShapeDtypeStruct((n, d), F32),
        compiler_params=_cparams("parallel"), name="outproj_even",
    )(h, a, b, wa, wb)


def _outproj2t_body(h_ref, a_ref, bt_ref, wa_ref, wb_ref, o_ref):
    acc = h_ref[...] + _dot(a_ref[...].astype(BF16), wa_ref[...])
    tq = bt_ref.shape[2]
    for i in range(bt_ref.shape[0]):
        rs = slice(i * tq, (i + 1) * tq)
        o_ref[rs, :] = acc[rs, :] + _dot_tn(bt_ref[i].astype(BF16), wb_ref[...])


def _outproj2t(h, a, bt, wa, wb, *, tm):
    n, d = h.shape
    row = lambda i: (i, 0)
    tq = bt.shape[2]
    return pl.pallas_call(
        _outproj2t_body, grid=(n // tm,),
        in_specs=[pl.BlockSpec((tm, d), row), pl.BlockSpec((tm, a.shape[1]), row),
                  pl.BlockSpec((tm // tq, bt.shape[1], tq), lambda i: (i, 0, 0)), _const(wa.shape), _const(wb.shape)],
        out_specs=pl.BlockSpec((tm, d), row), out_shape=jax.ShapeDtypeStruct((n, d), F32),
        compiler_params=_cparams("parallel"), name="outproj_even_t",
    )(h, a, bt, wa, wb)


def _outproj_merge_body(h_ref, o0_ref, o1_ref, o2_ref, l0_ref, l1_ref, l2_ref, w_ref, o_ref, *stage, views):
    if views:
        load = lambda ref, g: (ref[...] if DIL_PATTERN[g][1] == 1
                               else _from_residue_view(ref, stage[0], DIL_PATTERN[g][1], ODD_GW))
    else:
        load = lambda ref, g: ref[...]
    mixed = _merge_groups([load(r, g) for g, r in enumerate((o0_ref, o1_ref, o2_ref))],
                          [load(r, g) for g, r in enumerate((l0_ref, l1_ref, l2_ref))])
    o_ref[...] = h_ref[...] + _dot(mixed.astype(BF16), w_ref[...])


def _outproj_merge(h, outs, lses, w, *, tm, views=False):
    n, d = h.shape
    row = lambda i: (i, 0)
    parts = [pl.BlockSpec((tm * a.shape[0] // n, a.shape[1]), row) for a in list(outs) + list(lses)]
    return pl.pallas_call(
        functools.partial(_outproj_merge_body, views=views), grid=(n // tm,),
        in_specs=[pl.BlockSpec((tm, d), row)] + parts + [_const(w.shape)],
        out_specs=pl.BlockSpec((tm, d), row), out_shape=jax.ShapeDtypeStruct((n, d), F32),
        scratch_shapes=[pltpu.VMEM((tm, LANES), F32)] if views else [],
        compiler_params=_cparams("parallel"), name="outproj_odd",
    )(h, *outs, *lses, w)


def _hgrn_prefix_matrix(C):
    t = np.arange(C)[:, None]
    j = np.arange(C)[None, :]
    mats = [j <= t]
    lvl = 1
    while (1 << lvl) <= C:
        mats.append(j <= ((t >> lvl) << lvl) + (1 << lvl) // 2 - 1)
        lvl += 1
    return jnp.asarray(np.concatenate(mats, axis=0).astype(np.float32), BF16)


def _hgrn_body(hq_ref, hf_ref, hi_ref, hgt_ref, lb_ref, gn_ref, bd_ref, pm_ref, s0_ref, o_ref, sfin_ref, st_ref, *, C):
    c = pl.program_id(1)

    @pl.when(c == 0)
    def _():
        st_ref[...] = s0_ref[0]

    dk = HEAD_DIM
    w = HG_HEADS * dk
    lb = lb_ref[...]
    f = lb + (1.0 - lb) * jax.nn.sigmoid(hf_ref[...])
    lf = jnp.log(f)
    kk = 1.0 - f
    ti = lax.broadcasted_iota(jnp.int32, (C, C), 0)
    si = lax.broadcasted_iota(jnp.int32, (C, C), 1)
    p1 = lf.astype(BF16)
    r1 = lf - p1.astype(F32)
    p2 = r1.astype(BF16)
    p3 = (r1 - p2.astype(F32)).astype(BF16)
    sums = _dot(pm_ref[...], jnp.concatenate([p1, p2, p3], axis=1))
    sums = sums[:, :w] + sums[:, w:2 * w] + sums[:, 2 * w:]
    b = sums[0:C]
    q = hq_ref[...]
    v = hi_ref[...].astype(BF16)
    heads = range(HG_HEADS)
    hs = [slice(dk * h, dk * (h + 1)) for h in heads]

    qb = q.astype(BF16)
    kb = kk.astype(BF16)
    a = [jnp.where(ti == si, _dot_nt(qb[:, hs[h]], kb[:, hs[h]]), 0.0) for h in heads]
    lvl = 1
    while (1 << lvl) <= C:
        m = 1 << lvl
        half = m // 2
        r = sums[lvl * C:(lvl + 1) * C]
        e = jnp.exp(-jnp.abs(b - r))
        qt = (q * e).astype(BF16)
        kt = (kk * e).astype(BF16)
        pair = ((ti >> lvl) == (si >> lvl)) & ((ti & (m - 1)) >= half) & ((si & (m - 1)) < half)
        a = [a[h] + jnp.where(pair, _dot_nt(qt[:, hs[h]], kt[:, hs[h]]), 0.0) for h in heads]
        lvl += 1

    b_last = b[C - 1:C, :]
    qe = (q * jnp.exp(b)).astype(BF16)
    kdec = (kk * jnp.exp(b_last - b)).astype(BF16)
    e_last = jnp.exp(b_last)
    outs = []
    for h in heads:
        st = st_ref[h]
        outs.append(_dot(a[h].astype(BF16), v[:, hs[h]]) + _dot_nt(qe[:, hs[h]], st.astype(BF16)))
        st_ref[h] = st * e_last[:, hs[h]] + _dot_tn(v[:, hs[h]], kdec[:, hs[h]])
    o = jnp.concatenate(outs, axis=1)
    gt = hgt_ref[...]
    o_ref[...] = _seg_rms(o, bd_ref[...], gn_ref[...]) * (gt * jax.nn.sigmoid(gt))

    @pl.when(c == pl.num_programs(1) - 1)
    def _():
        sfin_ref[0] = st_ref[...]


def _hgrn(hg, lb, hg_norm, s0, *, batch, seq):
    C = math.gcd(seq, HG_CHUNK)
    nc = seq // C
    w = HG_HEADS * HEAD_DIM
    col = lambda j: pl.BlockSpec((C, w), lambda b, c: (b * nc + c, j))
    st_spec = pl.BlockSpec((1, HG_HEADS, HEAD_DIM, HEAD_DIM), lambda b, c: (b, 0, 0, 0))
    pm = _hgrn_prefix_matrix(C)
    o, sfin = pl.pallas_call(
        functools.partial(_hgrn_body, C=C), grid=(batch, nc),
        in_specs=[col(0), col(1), col(2), col(3), _const((1, w)), _const((1, w)), _const(_block_diag_ones(w).shape), _const(pm.shape),
                  st_spec],
        out_specs=[pl.BlockSpec((C, w), lambda b, c: (b * nc + c, 0)), st_spec],
        out_shape=[jax.ShapeDtypeStruct((batch * seq, w), F32),
                   jax.ShapeDtypeStruct((batch, HG_HEADS, HEAD_DIM, HEAD_DIM), F32)],
        scratch_shapes=[pltpu.VMEM((HG_HEADS, HEAD_DIM, HEAD_DIM), F32)],
        compiler_params=_cparams("parallel", "arbitrary"), name="hgrn_scan",
    )(hg, hg, hg, hg, lb.reshape(1, w), jnp.tile(hg_norm, HG_HEADS).reshape(1, w), _block_diag_ones(w), pm,
      jnp.swapaxes(s0, 2, 3))
    return o, jnp.swapaxes(sfin, 2, 3)


CHUNKS_PER_PAGE = PAGE // CMP_STRIDE
PAGE_LANES = CMP_STRIDE * ROWS_NSA
CMP_HID = 2 * NSA_KV * 2 * HEAD_DIM
CMP_PP = 8


def _compress_body(pt_ref, *refs, n_chunks, transposed):
    pages = refs[:CMP_PP]
    wbig_ref, pe_ref, w2_ref, gk_ref, km_ref, bd_ref, perm_ref, rows_ref, grp_ref, xs_ref = refs[CMP_PP:]
    nb = ROWS_NSA // LANES
    j = pl.program_id(1)
    perm = perm_ref[...]
    for i in range(0, CMP_PP, 2):
        pair = []
        for k in range(2):
            pg = pages[i + k][0]
            if transposed:
                pg = jnp.concatenate([pg[c * LANES:(c + 1) * LANES, :].T for c in range(nb)], axis=1)
            pair.append(_dot(perm, pg.astype(BF16)))
        r0 = pl.multiple_of((j * CMP_PP + i) * CHUNKS_PER_PAGE, 2 * CHUNKS_PER_PAGE)
        for s in range(CMP_STRIDE):
            rs = slice(s * CHUNKS_PER_PAGE, (s + 1) * CHUNKS_PER_PAGE)
            both = jnp.concatenate([pair[0][rs], pair[1][rs]], axis=0).astype(BF16)
            for c in range(nb):
                lo = (c * CMP_STRIDE + s) * LANES
                xs_ref[pl.ds(r0, 2 * CHUNKS_PER_PAGE), lo:lo + LANES] = both[:, c * LANES:(c + 1) * LANES]

    @pl.when(j == pl.num_programs(1) - 1)
    def _():
        blk_w = CMP_STRIDE * LANES
        first = lambda x: jnp.concatenate(
            [_dot(x[:, c * blk_w:(c + 1) * blk_w], wbig_ref[c]) for c in range(nb)], axis=1)
        y = first(xs_ref[...])
        ysh = pltpu.roll(y, n_chunks - 1, 0)
        hid = y + pltpu.roll(ysh, CMP_HID - HEAD_DIM, 1)
        cf = first(pe_ref[...].astype(BF16))
        lane = lax.broadcasted_iota(jnp.int32, (8, CMP_HID), 1)
        cs = jnp.where(((lane >> 6) & 1) == 0, jnp.broadcast_to(cf[0:1], (8, CMP_HID)),
                       jnp.broadcast_to(cf[1:2], (8, CMP_HID)))
        cs = cs + pltpu.roll(cs, CMP_HID - HEAD_DIM, 1)
        hid = hid + cs[0:1]
        act = (hid * jax.nn.sigmoid(hid)).astype(BF16)
        out = _dot(act, w2_ref[...])
        out = _norm_k_rows(out, bd_ref[...], gk_ref[...], km_ref[...])
        rows_ref[0] = out
        kw = NSA_KV * HEAD_DIM
        for g in range(NSA_KV):
            grp_ref[0, g] = jnp.concatenate([out[:, HEAD_DIM * g:HEAD_DIM * (g + 1)],
                                             out[:, kw + HEAD_DIM * g:kw + HEAD_DIM * (g + 1)]], axis=1)


def _compress_weights(cmp_pe, cmp_w1, cmp_w2, k_gain):
    ratio = CMP_BLOCK // CMP_STRIDE
    w1 = cmp_w1.reshape(2, ratio, CMP_STRIDE, HEAD_DIM, HEAD_DIM)
    eye_kv = jnp.eye(2, dtype=F32)
    eye_g = jnp.eye(NSA_KV, dtype=F32)
    nb = ROWS_NSA // LANES
    member_kv = np.arange(2 * NSA_KV) // NSA_KV
    w1m = w1[member_kv].reshape(nb, 2, ratio, CMP_STRIDE, HEAD_DIM, HEAD_DIM)
    wbig = jnp.einsum('chrsde,hH->cshdHre', w1m, eye_kv).reshape(nb, CMP_STRIDE * LANES, 2 * ratio * HEAD_DIM)
    wbig = wbig.astype(BF16)
    pe = cmp_pe.reshape(2, ratio, CMP_STRIDE, HEAD_DIM)
    pe_rows = pe[member_kv].reshape(nb, 2, ratio, CMP_STRIDE, HEAD_DIM).transpose(2, 0, 3, 1, 4)
    pe_rows = pe_rows.reshape(ratio, PAGE_LANES)
    pe_rows = jnp.concatenate([pe_rows, jnp.zeros((8 - ratio, PAGE_LANES), F32)], axis=0)
    sel_r0 = jnp.asarray([1.0, 0.0], F32)
    w2big = jnp.einsum('ked,kK,gG,r->kgreKGd', cmp_w2, eye_kv, eye_g, sel_r0).reshape(CMP_HID, ROWS_NSA).astype(BF16)
    kw = NSA_KV * HEAD_DIM
    gk = jnp.concatenate([jnp.tile(k_gain, NSA_KV), jnp.ones((kw,), F32)]).reshape(1, ROWS_NSA)
    km = jnp.concatenate([jnp.ones((kw,), F32), jnp.zeros((kw,), F32)]).reshape(1, ROWS_NSA)
    return wbig, pe_rows, w2big, gk, km


def _compress(pool, table, cw, *, transposed):
    wbig, pe_rows, w2big, gk, km = cw
    bsz, npg = table.shape
    n_chunks = npg * CHUNKS_PER_PAGE
    page = lambda i: pl.BlockSpec((1,) + pool.shape[1:], lambda b, j, pt: (pt[b, j * CMP_PP + i], 0, 0))
    tok = np.arange(PAGE)
    perm = (tok[None, :] == (tok[:, None] % CHUNKS_PER_PAGE) * CMP_STRIDE + tok[:, None] // CHUNKS_PER_PAGE)
    perm = jnp.asarray(perm.astype(np.float32), BF16)
    return pl.pallas_call(
        functools.partial(_compress_body, n_chunks=n_chunks, transposed=transposed),
        grid_spec=pltpu.PrefetchScalarGridSpec(
            num_scalar_prefetch=1, grid=(bsz, npg // CMP_PP),
            in_specs=[page(i) for i in range(CMP_PP)] + [
                _const(wbig.shape), _const(pe_rows.shape), _const(w2big.shape), _const((1, ROWS_NSA)),
                _const((1, ROWS_NSA)), _const(_block_diag_ones(ROWS_NSA).shape), _const(perm.shape)],
            out_specs=[pl.BlockSpec((1, n_chunks, ROWS_NSA), lambda b, j, pt: (b, 0, 0)),
                       pl.BlockSpec((1, NSA_KV, n_chunks, LANES), lambda b, j, pt: (b, 0, 0, 0))],
            scratch_shapes=[pltpu.VMEM((n_chunks, PAGE_LANES), BF16)]),
        out_shape=[jax.ShapeDtypeStruct((bsz, n_chunks, ROWS_NSA), F32),
                   jax.ShapeDtypeStruct((bsz, NSA_KV, n_chunks, LANES), F32)],
        compiler_params=_cparams("parallel", "arbitrary"), name="nsa_compress",
    )(table, *([pool] * CMP_PP), wbig, pe_rows, w2big, gk, km, _block_diag_ones(ROWS_NSA), perm)


def _select_blocks(imp, pos, n_slc, k_sel):
    lane = lax.broadcasted_iota(jnp.int32, imp.shape, 1)
    cur = pos >> 6
    forced = (lane == 0) | (lane == cur) | (lane == cur - 1)
    score = jnp.where(forced, FORCE_SCORE, jnp.where(lane <= cur, imp, -1.0))
    rank = jnp.zeros(imp.shape, F32)
    for i in range(n_slc):
        ci = score[:, i:i + 1]
        rank = rank + jnp.where(ci > score, 1.0, jnp.where(ci == score, jnp.where(lane > i, 1.0, 0.0), 0.0))
    return jnp.where(lane < n_slc, jnp.where(rank < k_sel, 1.0, 0.0), 0.0)


NSA_TQ = 256
NSA_QC = LANES


def _select_blocks_t(imp, pos, n_slc, k_sel):
    blk = lax.broadcasted_iota(jnp.int32, imp.shape, 0)
    cur = pos >> 6
    forced = (blk == 0) | (blk == cur) | (blk == cur - 1)
    score = jnp.where(forced, FORCE_SCORE, jnp.where(blk <= cur, imp, -1.0))
    rank = jnp.zeros(imp.shape, F32)
    for i in range(n_slc):
        ci = score[i:i + 1, :]
        rank = rank + jnp.where(ci > score, 1.0, jnp.where(ci == score, jnp.where(blk > i, 1.0, 0.0), 0.0))
    return jnp.where(rank < k_sel, 1.0, 0.0)


def _nsa_prompt_t_body(q_ref, gate_ref, kcv_ref, slc_ref, win_ref, gq_ref, bd_ref, eye_ref, fs_ref, fc_ref,
                       ov_ref, ex_ref, o_ref, bc_ref, bs_ref, bw_ref, qt_scr, ks_scr, kw_scr, vs_scr, vw_scr, sel_scr, m_scr, l_scr, acc_scr,
                       *, seq, n_slc, k_sel):
    tq = NSA_TQ
    qc = NSA_QC
    hd = HEAD_DIM
    n_tiles = seq // tq
    width = NSA_GROUP * tq
    n_chunks = width // qc
    eye = eye_ref[...]
    n_c = kcv_ref.shape[2]
    far = NSA_WINDOW // tq

    @pl.when(pl.program_id(1) == 0)
    def _():
        key_i = lax.broadcasted_iota(jnp.int32, (tq, tq), 0)
        qry_i = lax.broadcasted_iota(jnp.int32, (tq, tq), 1)
        blk_i = lax.broadcasted_iota(jnp.int32, (n_c, tq), 0)
        n_cmp = seq // CMP_STRIDE - CMP_BLOCK // CMP_STRIDE + 1
        span = tq + CMP_STRIDE * n_c
        for h in range(NSA_GROUP):
            hs = slice(h * tq, (h + 1) * tq)
            for d in range(n_tiles):
                u = jnp.broadcast_to(fs_ref[0, h:h + 1, d * tq:(d + 2) * tq], (tq, 2 * tq))
                tile = pltpu.roll(u, tq, 1, stride=1, stride_axis=0)[:, :tq]
                bs_ref[d, :, hs] = tile
                if d == far:
                    bw_ref[0, :, hs] = jnp.where(qry_i <= key_i, tile, NEG)
            for t in range(n_tiles):
                u = jnp.broadcast_to(fc_ref[0, h:h + 1, t * tq:t * tq + span], (n_c, span))
                tile = pltpu.roll(u, tq, 1, stride=CMP_STRIDE, stride_axis=0)[:, :tq]
                bc_ref[t, :, hs] = jnp.where(blk_i < n_cmp, tile, NEG)

    qn = (_seg_rms(q_ref[...], bd_ref[...], gq_ref[...]) * (hd ** -0.5)).astype(BF16)
    for t in range(n_tiles):
        rs = slice(t * tq, (t + 1) * tq)
        for h in range(NSA_GROUP):
            qt_scr[t, :, h * tq:(h + 1) * tq] = _dot_nt(eye, qn[rs, h * hd:(h + 1) * hd]).astype(BF16)
        for kv_ref, k_scr, v_scr in ((slc_ref, ks_scr, vs_scr), (win_ref, kw_scr, vw_scr)):
            kv = kv_ref[0, rs, :].astype(BF16)
            k_scr[t] = kv[:, :hd]
            v_scr[t] = _dot_nt(eye, kv[:, hd:]).astype(BF16)
    kc = kcv_ref[0, 0, :, :hd].astype(BF16)
    vct = _dot_nt(eye, kcv_ref[0, 0, :, hd:].astype(BF16)).astype(BF16)
    def flash_init():
        m_scr[...] = jnp.full((1, width), NEG, F32)
        l_scr[...] = jnp.zeros((1, width), F32)
        acc_scr[...] = jnp.zeros((hd, width), F32)

    def flash_update(k_scr, v_scr, t, tiles, running):
        s_alls = [_dot(k_scr[jt], qt_scr[t]) for jt, _, _ in tiles]
        if running:
            m_all = m_scr[...]
            l_all = l_scr[...]
        alphas, ls, ms, ps = [], [], [], [[] for _ in tiles]
        for c in range(n_chunks):
            cs = slice(c * qc, (c + 1) * qc)
            qoff = (c * qc) % tq
            ss = []
            for (_, bias_tile, sel_add), s_all in zip(tiles, s_alls):
                s = s_all[:, cs] + bias_tile(cs)
                ss.append(s if sel_add is None else s + sel_add[:, qoff:qoff + qc])
            m_new = functools.reduce(jnp.maximum, [jnp.max(s, axis=0, keepdims=True) for s in ss])
            if running:
                m_new = jnp.maximum(m_all[:, cs], m_new)
                alpha = jnp.exp(m_all[:, cs] - m_new)
                alphas.append(alpha)
            l_new = alpha * l_all[:, cs] if running else 0.0
            for k, s in enumerate(ss):
                p = jnp.exp(s - m_new)
                l_new = l_new + jnp.sum(p, axis=0, keepdims=True)
                ps[k].append(p.astype(BF16))
            ls.append(l_new)
            ms.append(m_new)
        cat = lambda xs: jnp.concatenate(xs, axis=1)
        pv = sum(_dot(v_scr[jt], cat(ps[k])) for k, (jt, _, _) in enumerate(tiles))
        acc_scr[...] = cat(alphas) * acc_scr[...] + pv if running else pv
        m_scr[...] = cat(ms)
        l_scr[...] = cat(ls)

    def flash_result():
        return acc_scr[...] / jnp.maximum(l_scr[...], 1e-30)

    def q_tile(t, carry):
        bias_c = bc_ref[t]
        s_c = _dot(kc, qt_scr[t]) + bias_c
        m_c = jnp.max(s_c, axis=0, keepdims=True)
        e_c = jnp.where(bias_c > NEG_TEST, jnp.exp(s_c - m_c), 0.0)
        p_c = (e_c / jnp.maximum(jnp.sum(e_c, axis=0, keepdims=True), 1e-30)).astype(BF16)
        o_c = _dot(vct, p_c)
        imp = sum(_dot(ov_ref[...], p_c[:, h * tq:(h + 1) * tq]) for h in range(NSA_GROUP))
        pos = t * tq + lax.broadcasted_iota(jnp.int32, (1, tq), 1)
        sel = _select_blocks_t(imp[:n_slc], pos, n_slc, k_sel)
        sel_scr[...] = jnp.zeros(sel_scr.shape, BF16)
        sel_scr[0:n_slc, :] = ((1.0 - sel) * NEG).astype(BF16)
        flash_init()
        slc_tile = lambda jt: (jt, lambda cs: bs_ref[t - jt, :, cs], _dot(ex_ref[jt], sel_scr[...]))

        def slc_pair(i, carry):
            flash_update(ks_scr, vs_scr, t, [slc_tile(2 * i), slc_tile(2 * i + 1)], True)
            return carry

        lax.fori_loop(0, (t + 1) // 2, slc_pair, 0)
        pl.when((t & 1) == 0)(lambda: flash_update(ks_scr, vs_scr, t, [slc_tile(t)], True))
        o_s = flash_result()
        far = NSA_WINDOW // tq
        win_tile = lambda d: (t - d, (lambda cs: bw_ref[0, :, cs]) if d == far else (lambda cs: bs_ref[d, :, cs]),
                              None)
        for n_back in range(far + 1):
            cond = (t == n_back) if n_back < far else (t >= far)
            pl.when(cond)(functools.partial(flash_update, kw_scr, vw_scr, t,
                                            [win_tile(d) for d in range(n_back, -1, -1)], False))
        o_w = flash_result()
        gates = gate_ref[0, t]
        for h in range(NSA_GROUP):
            hs = slice(h * tq, (h + 1) * tq)
            o_ref[t, h * hd:(h + 1) * hd, :] = (gates[3 * h:3 * h + 1] * o_c[:, hs]
                                                + gates[3 * h + 1:3 * h + 2] * o_s[:, hs]
                                                + gates[3 * h + 2:3 * h + 3] * o_w[:, hs])
        return carry

    lax.fori_loop(0, n_tiles, q_tile, 0)


def _nsa_prompt_t_tables(rel_bias, seq, n_c):
    tq = NSA_TQ
    n_tiles = seq // tq
    n_slc = -(-seq // SEL_BLOCK)
    ds = jnp.arange((n_tiles + 1) * tq) - tq
    fs = _bias_tile(rel_bias, ds, ds >= 0).reshape(NSA_KV, NSA_GROUP, -1)
    dc = jnp.arange(n_tiles * tq + CMP_STRIDE * n_c) - (CMP_STRIDE * n_c + CMP_BLOCK - 1)
    fc = _bias_tile(rel_bias, dc, dc >= 0).reshape(NSA_KV, NSA_GROUP, -1)
    n_cmp = seq // CMP_STRIDE - CMP_BLOCK // CMP_STRIDE + 1
    c_lo = np.arange(n_c) * CMP_STRIDE
    s_lo = np.arange(LANES) * SEL_BLOCK
    ov = ((c_lo[None, :] <= s_lo[:, None] + SEL_BLOCK - 1) & (c_lo[None, :] + CMP_BLOCK - 1 >= s_lo[:, None])
          & (np.arange(n_c) < n_cmp)[None, :] & (np.arange(LANES) < n_slc)[:, None])
    ex = (np.arange(seq)[:, None] // SEL_BLOCK == np.arange(LANES)[None, :]).reshape(n_tiles, tq, LANES)
    as_bf = lambda a: jnp.asarray(a.astype(np.float32), BF16)
    return fs, fc, as_bf(ov), as_bf(ex), n_slc


def _nsa_prompt_t(q, gates_t, kcv, slcg, wing, q_gain, tables, *, batch, seq):
    fs, fc, ov, ex, n_slc = tables
    n_c = kcv.shape[2]
    tq = NSA_TQ
    n_tiles = seq // tq
    assert n_tiles > NSA_WINDOW // tq
    gw = NSA_GROUP * HEAD_DIM
    width = NSA_GROUP * tq
    per_g = lambda shape: pl.BlockSpec((1,) + shape, lambda g, b: (g,) + (0,) * len(shape),
                                       pipeline_mode=pl.Buffered(1))
    eye = jnp.eye(HEAD_DIM, dtype=BF16)
    return pl.pallas_call(
        functools.partial(_nsa_prompt_t_body, seq=seq, n_slc=n_slc, k_sel=min(SEL_TOPK, n_slc)),
        grid=(NSA_KV, batch),
        in_specs=[pl.BlockSpec((seq, gw), lambda g, b: (b, g)),
                  pl.BlockSpec((1, n_tiles, 16, tq), lambda g, b: (g, b, 0, 0)),
                  pl.BlockSpec((1, 1, n_c, LANES), lambda g, b: (b, g, 0, 0)),
                  pl.BlockSpec((1, seq, LANES), lambda g, b: (g, b, 0)),
                  pl.BlockSpec((1, seq, LANES), lambda g, b: (g, b, 0)),
                  _const((1, gw)), _const(_block_diag_ones(gw).shape), _const(eye.shape),
                  per_g(fs.shape[1:]), per_g(fc.shape[1:]), _const(ov.shape), _const(ex.shape)],
        out_specs=pl.BlockSpec((n_tiles, gw, tq), lambda g, b: (b, g, 0)),
        out_shape=jax.ShapeDtypeStruct((batch * n_tiles, NSA_KV * gw, tq), F32),
        scratch_shapes=[pltpu.VMEM((n_tiles, n_c, width), F32), pltpu.VMEM((n_tiles, tq, width), F32),
                        pltpu.VMEM((1, tq, width), F32), pltpu.VMEM((n_tiles, HEAD_DIM, width), BF16),
                        pltpu.VMEM((n_tiles, tq, HEAD_DIM), BF16), pltpu.VMEM((n_tiles, tq, HEAD_DIM), BF16),
                        pltpu.VMEM((n_tiles, HEAD_DIM, tq), BF16), pltpu.VMEM((n_tiles, HEAD_DIM, tq), BF16),
                        pltpu.VMEM((LANES, tq), BF16), pltpu.VMEM((1, width), F32), pltpu.VMEM((1, width), F32),
                        pltpu.VMEM((HEAD_DIM, width), F32)],
        compiler_params=_cparams("arbitrary", "arbitrary"), name="nsa_prompt",
    )(q, gates_t, kcv, slcg, wing, jnp.tile(q_gain, NSA_GROUP).reshape(1, gw), _block_diag_ones(gw), eye,
      fs, fc, ov, ex)


SMP_PP = 8
NSA_KW = NSA_KV * HEAD_DIM
SEL_LANES = 2 * LANES


def _nsa_sample_body(pt_ref, q_ref, gate_ref, ckv_ref, *refs, n_slc, k_sel, past, tq):
    pages = refs[:SMP_PP]
    (slcn_ref, wst_ref, winn_ref, gq_ref, bd_ref, bc_ref, bs_ref, bsn_ref, bw_ref, bwn_ref, ov_ref, ex_ref, exn_ref,
     o_ref, nwin_ref, qbd_scr, sel_scr, kbuf_scr, vbuf_scr, s_scr, sn_scr, vn_scr, oc_scr, ow_scr,
     pad_scr) = refs[SMP_PP:]
    j = pl.program_id(1)
    hd = HEAD_DIM
    kw = NSA_KW
    rows = NSA_HEADS * tq
    init = (jnp.full((rows, 1), NEG, F32), jnp.zeros((rows, 1), F32), jnp.zeros((rows, kw), F32))

    @pl.when(j == 0)
    def _():
        qn = _seg_rms(q_ref[...], bd_ref[...], gq_ref[...]) * (hd ** -0.5)
        zero = jnp.zeros((tq, hd), F32)
        blocks = []
        for h in range(NSA_HEADS):
            qh = qn[:, hd * h:hd * (h + 1)]
            blocks.append(jnp.concatenate([qh if g == h // NSA_GROUP else zero for g in range(NSA_KV)], axis=1))
        qbd = jnp.concatenate(blocks, axis=0).astype(BF16)
        qbd_scr[...] = qbd
        ckv = ckv_ref[0]
        bias_c = bc_ref[...]
        ok_c = bias_c > NEG_TEST
        s_c = _dot_nt(qbd, ckv[:, :kw].astype(BF16)) + bias_c
        m_c = jnp.max(s_c, axis=-1, keepdims=True)
        e_c = jnp.where(ok_c, jnp.exp(s_c - m_c), 0.0)
        p_c = (e_c / jnp.maximum(jnp.sum(e_c, axis=-1, keepdims=True), 1e-30)).astype(BF16)
        oc_scr[...] = _dot(p_c, ckv[:, kw:].astype(BF16))
        d = _dot(p_c, ov_ref[...])
        gr = NSA_GROUP * tq
        imp = jnp.concatenate(
            [sum(d[g * gr + h * tq:g * gr + (h + 1) * tq] for h in range(NSA_GROUP)) for g in range(NSA_KV)], axis=0)
        pos = past + (lax.broadcasted_iota(jnp.int32, (NSA_KV * tq, 1), 0) & (tq - 1))
        sel = _select_blocks(imp, pos, n_slc, k_sel)
        sel = jnp.concatenate([sel[g * tq:(g + 1) * tq] for g in range(NSA_KV) for _ in range(NSA_GROUP)], axis=0)
        sel = ((1.0 - sel) * NEG).astype(BF16)
        sel_scr[...] = sel
        pad_scr[...] = jnp.zeros(pad_scr.shape, F32)
        pad_scr[0:tq, :] = slcn_ref[...]
        kn = pad_scr[...].astype(BF16)
        sn_scr[...] = _dot_nt(qbd, kn[:, :kw]) + bsn_ref[...] + _dot(sel, exn_ref[...])
        vn_scr[...] = kn[:, kw:]
        pad_scr[0:tq, :] = winn_ref[...]
        nwin_ref[0] = _shifted_buffer(wst_ref[0], pad_scr[...], tq)
        wst = wst_ref[0].astype(BF16)
        wn = pad_scr[...].astype(BF16)
        bias_w = bw_ref[...]
        mw = _softmax_piece(_dot(qbd, wst[:kw]) + bias_w, bias_w > NEG_TEST, *init, wst[kw:], v_t=True)
        bias_wn = bwn_ref[...]
        mw = _softmax_piece(_dot_nt(qbd, wn[:, :kw]) + bias_wn, bias_wn > NEG_TEST, *mw, wn[:, kw:])
        ow_scr[...] = mw[2] / jnp.maximum(mw[1], 1e-30)

    for i in range(SMP_PP):
        pg = pages[i][0].astype(BF16)
        kbuf_scr[:, i * PAGE:(i + 1) * PAGE] = pg[:kw]
        vbuf_scr[j, :, i * PAGE:(i + 1) * PAGE] = pg[kw:]
    s_scr[j] = _dot(qbd_scr[...], kbuf_scr[...]) + bs_ref[j] + _dot(sel_scr[...], ex_ref[j])

    @pl.when(j == pl.num_programs(1) - 1)
    def _():
        s_all = s_scr[...]
        s_new = sn_scr[...]
        m = jnp.maximum(jnp.max(jnp.max(s_all, axis=0), axis=-1, keepdims=True),
                        jnp.max(s_new, axis=-1, keepdims=True))
        p_new = jnp.exp(s_new - m)
        den = jnp.sum(p_new, axis=-1, keepdims=True)
        acc = _dot(p_new.astype(BF16), vn_scr[...])
        for st in range(s_all.shape[0]):
            p = jnp.exp(s_all[st] - m)
            den = den + jnp.sum(p, axis=-1, keepdims=True)
            acc = acc + _dot_nt(p.astype(BF16), vbuf_scr[st])
        o_s = acc / jnp.maximum(den, 1e-30)
        o_c = oc_scr[...]
        o_w = ow_scr[...]
        outs = []
        for h in range(NSA_HEADS):
            g, hg = divmod(h, NSA_GROUP)
            gt = gate_ref[g]
            rs = slice(h * tq, (h + 1) * tq)
            ls = slice(hd * g, hd * (g + 1))
            outs.append(gt[:, 3 * hg:3 * hg + 1] * o_c[rs, ls] + gt[:, 3 * hg + 1:3 * hg + 2] * o_s[rs, ls]
                        + gt[:, 3 * hg + 2:3 * hg + 3] * o_w[rs, ls])
        o_ref[...] = jnp.concatenate(outs, axis=1)


def _nsa_sample_tables(rel_bias, past, tq, n_c, w_state):
    rows = NSA_HEADS * tq
    total = past + tq
    n_cmp = total // CMP_STRIDE - CMP_BLOCK // CMP_STRIDE + 1
    n_slc = -(-total // SEL_BLOCK)
    pos = past + jnp.arange(tq)
    c_hi = jnp.arange(n_c) * CMP_STRIDE + CMP_BLOCK - 1
    dist_c = pos[:, None] - c_hi[None, :]
    bc = _bias_tile(rel_bias, dist_c, (dist_c >= 0) & (jnp.arange(n_c) < n_cmp)[None, :]).reshape(rows, n_c)
    n_steps = past // (SMP_PP * PAGE)
    d_ext = jnp.arange(tq + past - 1) + 1
    bs = _toeplitz(_bias_tile(rel_bias, d_ext, d_ext >= 0), tq, past)
    bs = bs.reshape(rows, n_steps, SMP_PP * PAGE).transpose(1, 0, 2)
    jn = jnp.arange(PAGE)
    dist_n = jnp.arange(tq)[:, None] - jn[None, :]
    new_ok = (dist_n >= 0) & (jn < tq)[None, :]
    bsn = _bias_tile(rel_bias, dist_n, new_ok).reshape(rows, PAGE)
    dist_w = jnp.arange(tq)[:, None] + w_state - jnp.arange(w_state)[None, :]
    bw = _bias_tile(rel_bias, dist_w, (dist_w >= 0) & (dist_w <= NSA_WINDOW)).reshape(rows, w_state)
    bwn = _bias_tile(rel_bias, dist_n, new_ok & (dist_n <= NSA_WINDOW)).reshape(rows, PAGE)
    c_lo = np.arange(n_c) * CMP_STRIDE
    s_lo = np.arange(SEL_LANES) * SEL_BLOCK
    ov = ((c_lo[:, None] <= s_lo[None, :] + SEL_BLOCK - 1) & (c_lo[:, None] + CMP_BLOCK - 1 >= s_lo[None, :])
          & (np.arange(n_c) < n_cmp)[:, None] & (np.arange(SEL_LANES) < n_slc)[None, :])
    blk = np.arange(SEL_LANES)
    ex = (blk[:, None] == (np.arange(past)[None, :] // SEL_BLOCK)).reshape(SEL_LANES, n_steps, SMP_PP * PAGE)
    ex = ex.transpose(1, 0, 2)
    exn = (blk[:, None] == ((past + np.arange(PAGE))[None, :] // SEL_BLOCK)) & (np.arange(PAGE) < tq)[None, :]
    as_bf = lambda a: jnp.asarray(a.astype(np.float32), BF16)
    return bc, bs, bsn, bw, bwn, as_bf(ov), as_bf(ex), as_bf(exn), n_slc


def _nsa_sample(q, gates, crows, slc_pool_t, table, slc_new, win_state_t, win_new, q_gain, tables, *, tq):
    bc, bs, bsn, bw, bwn, ov, ex, exn, n_slc = tables
    bsz, npg = table.shape
    past = npg * PAGE
    n_c = crows.shape[1]
    w_state = win_state_t.shape[2]
    rows = NSA_HEADS * tq
    qw = NSA_HEADS * HEAD_DIM
    page = lambda i: pl.BlockSpec((1, ROWS_NSA, PAGE), lambda b, j, pt: (pt[b, j * SMP_PP + i], 0, 0))
    per_b = lambda shape: pl.BlockSpec((1,) + shape, lambda b, j, pt: (b,) + (0,) * len(shape))
    new_rows = pl.BlockSpec((tq, ROWS_NSA), lambda b, j, pt: (b, 0))
    return pl.pallas_call(
        functools.partial(_nsa_sample_body, n_slc=n_slc, k_sel=min(SEL_TOPK, n_slc), past=past, tq=tq),
        grid_spec=pltpu.PrefetchScalarGridSpec(
            num_scalar_prefetch=1, grid=(bsz, npg // SMP_PP),
            in_specs=[pl.BlockSpec((tq, qw), lambda b, j, pt: (b, 0)),
                      pl.BlockSpec((NSA_KV, tq, LANES), lambda b, j, pt: (0, b, 0)),
                      per_b((n_c, ROWS_NSA))] + [page(i) for i in range(SMP_PP)] + [
                      new_rows, per_b((ROWS_NSA, w_state)), new_rows,
                      _const((1, qw)), _const(_block_diag_ones(qw).shape), _const(bc.shape), _const(bs.shape), _const(bsn.shape),
                      _const(bw.shape), _const(bwn.shape), _const(ov.shape), _const(ex.shape), _const(exn.shape)],
            out_specs=[pl.BlockSpec((tq, qw), lambda b, j, pt: (b, 0)), per_b((ROWS_NSA, w_state))],
            scratch_shapes=[pltpu.VMEM((rows, NSA_KW), BF16), pltpu.VMEM((rows, SEL_LANES), BF16),
                            pltpu.VMEM((NSA_KW, SMP_PP * PAGE), BF16),
                            pltpu.VMEM((npg // SMP_PP, NSA_KW, SMP_PP * PAGE), BF16),
                            pltpu.VMEM((npg // SMP_PP, rows, SMP_PP * PAGE), F32),
                            pltpu.VMEM((rows, PAGE), F32), pltpu.VMEM((PAGE, NSA_KW), BF16),
                            pltpu.VMEM((rows, NSA_KW), F32), pltpu.VMEM((rows, NSA_KW), F32),
                            pltpu.VMEM((PAGE, ROWS_NSA), F32)]),
        out_shape=[jax.ShapeDtypeStruct((bsz * tq, qw), F32), jax.ShapeDtypeStruct(win_state_t.shape, F32)],
        compiler_params=_cparams("parallel", "arbitrary"), name="nsa_sample",
    )(table, q, gates, crows, *([slc_pool_t] * SMP_PP), slc_new, win_state_t, win_new,
      jnp.tile(q_gain, NSA_HEADS).reshape(1, qw), _block_diag_ones(qw), bc, bs, bsn, bw, bwn, ov, ex, exn)


DIL_TQ = 128


def _head_of_lane(shape):
    return lax.broadcasted_iota(jnp.int32, shape, 1) >> 6


def _stack_heads(qt, lane_head):
    return jnp.concatenate([jnp.where(lane_head == h, qt, 0.0) for h in range(DIL_HEADS)], axis=0).astype(BF16)


def _unstack_heads(r, lane_head, t):
    return sum(jnp.where(lane_head == h, r[h * t:(h + 1) * t], 0.0) for h in range(DIL_HEADS))


def _dil_prompt_body(q_ref, kv_ref, bias_ref, o_ref, lse_ref, *, n_sub, dil):
    tq = DIL_TQ
    gw = ODD_GW
    lane_head = _head_of_lane((tq, gw))
    for r in range(dil):
        ql = slice(gw * r, gw * (r + 1))
        kl = slice(2 * gw * r, 2 * gw * r + gw)
        vl = slice(2 * gw * r + gw, 2 * gw * (r + 1))

        def tile(t, carry, ql=ql, kl=kl, vl=vl):
            start = pl.multiple_of(t * tq, tq)
            prev = pl.multiple_of(jnp.maximum(t - 1, 0) * tq, tq)
            cur_rows = pl.ds(start, tq)
            prev_rows = pl.ds(prev, tq)
            kk = jnp.concatenate([kv_ref[prev_rows, kl], kv_ref[cur_rows, kl]], axis=0).astype(BF16)
            vv = jnp.concatenate([kv_ref[prev_rows, vl], kv_ref[cur_rows, vl]], axis=0).astype(BF16)
            bias = bias_ref[jnp.minimum(t, 1)]
            s = _dot_nt(_stack_heads(q_ref[cur_rows, ql], lane_head), kk) + bias
            m = jnp.max(s, axis=-1, keepdims=True)
            e = jnp.where(bias > NEG_TEST, jnp.exp(s - m), 0.0)
            l = jnp.maximum(jnp.sum(e, axis=-1, keepdims=True), 1e-30)
            res = _dot((e / l).astype(BF16), vv)
            lse = jnp.broadcast_to(m + jnp.log(l), (DIL_HEADS * tq, gw))
            o_ref[cur_rows, ql] = _unstack_heads(res, lane_head, tq)
            lse_ref[cur_rows, ql] = _unstack_heads(lse, lane_head, tq)
            return carry

        lax.fori_loop(0, n_sub // tq, tile, 0)


def _merge_groups(outs, lses):
    mx = functools.reduce(jnp.maximum, lses)
    ws = [jnp.exp(l - mx) for l in lses]
    return sum(w * o for w, o in zip(ws, outs)) / sum(ws)


def _dil_prompt_tables(rel_bias):
    tq = DIL_TQ
    m_ext = jnp.arange(3 * tq - 1) - (tq - 1)
    cur_tile = jnp.arange(2 * tq) >= tq
    tiles = []
    for gi, (win, dil) in enumerate(DIL_PATTERN):
        assert win // dil == tq
        t = _toeplitz(_bias_tile(rel_bias, m_ext * dil, (m_ext >= 0) & (m_ext <= tq)), tq, 2 * tq)
        t = t[gi * DIL_HEADS:(gi + 1) * DIL_HEADS].reshape(DIL_HEADS * tq, 2 * tq)
        tiles.append(jnp.stack([jnp.where(cur_tile, t, NEG), t]))
    return tiles


def _dil_prompt(q_views, kv_views, tables, *, batch, seq):
    gw = ODD_GW
    outs, lses = [], []
    for gi, (_, dil) in enumerate(DIL_PATTERN):
        n_sub = seq // dil
        assert n_sub % DIL_TQ == 0
        qv, kvv = q_views[gi], kv_views[gi]
        o_spec = pl.BlockSpec((n_sub, dil * gw), lambda b: (b, 0))
        o, lse = pl.pallas_call(
            functools.partial(_dil_prompt_body, n_sub=n_sub, dil=dil), grid=(batch,),
            in_specs=[o_spec, pl.BlockSpec((n_sub, dil * 2 * gw), lambda b: (b, 0)), _const(tables[gi].shape)],
            out_specs=[o_spec, o_spec],
            out_shape=[jax.ShapeDtypeStruct(qv.shape, F32)] * 2,
            compiler_params=_cparams("parallel"), name=f"dilated_prompt_g{gi}",
        )(qv, kvv, tables[gi])
        outs.append(o)
        lses.append(lse)
    return outs, lses


def _shifted_buffer(st, new_rows_padded, tq):
    ln = st.shape[1]
    shifted = pltpu.roll(st, ln - tq, 1)
    new_t = pltpu.roll(new_rows_padded.T, LANES - tq, 1)
    lane = lax.broadcasted_iota(jnp.int32, (st.shape[0], LANES), 1)
    tail = jnp.where(lane >= LANES - tq, new_t, shifted[:, ln - LANES:])
    return tail if ln == LANES else jnp.concatenate([shifted[:, :ln - LANES], tail], axis=1)


def _dil_sample_body(q_ref, kvn_ref, st0_ref, st1_ref, st2_ref, b0_ref, b1_ref, b2_ref, bn_ref, o_ref, lse_ref,
                     n0_ref, n1_ref, n2_ref, pad_scr, *, tq):
    gw = ODD_GW
    lane_head = _head_of_lane((tq, gw))
    rows = DIL_HEADS * tq
    pad_scr[...] = jnp.zeros(pad_scr.shape, F32)
    for gi, (st_ref, b_ref, new_ref) in enumerate(((st0_ref, b0_ref, n0_ref), (st1_ref, b1_ref, n1_ref),
                                                    (st2_ref, b2_ref, n2_ref))):
        qs = _stack_heads(q_ref[gi], lane_head)
        pad_scr[0:tq, :] = kvn_ref[gi]
        new_ref[0] = _shifted_buffer(st_ref[0], pad_scr[...], tq)
        st = st_ref[0].astype(BF16)
        kn = pad_scr[...].astype(BF16)
        bias = b_ref[...]
        acc = (jnp.full((rows, 1), NEG, F32), jnp.zeros((rows, 1), F32), jnp.zeros((rows, gw), F32))
        acc = _softmax_piece(_dot(qs, st[:gw]) + bias, bias > NEG_TEST, *acc, st[gw:], v_t=True)
        bias_n = bn_ref[gi]
        m, l, a = _softmax_piece(_dot_nt(qs, kn[:, :gw]) + bias_n, bias_n > NEG_TEST, *acc, kn[:, gw:])
        l = jnp.maximum(l, 1e-30)
        o_ref[gi] = _unstack_heads(a / l, lane_head, tq)
        lse_ref[gi] = _unstack_heads(jnp.broadcast_to(m + jnp.log(l), (rows, gw)), lane_head, tq)


def _dil_sample_tables(rel_bias, tq, state_lens):
    rows = DIL_HEADS * tq
    i = jnp.arange(tq)[:, None]
    per_state, per_new = [], []
    jn = jnp.arange(PAGE)[None, :]
    for gi, ((win, dil), ln) in enumerate(zip(DIL_PATTERN, state_lens)):
        hs = slice(gi * DIL_HEADS, (gi + 1) * DIL_HEADS)
        d = i + ln - jnp.arange(ln)[None, :]
        per_state.append(_bias_tile(rel_bias, d, (d % dil == 0) & (d <= win))[hs].reshape(rows, ln))
        dn = i - jn
        per_new.append(_bias_tile(rel_bias, dn, (dn >= 0) & (dn % dil == 0) & (dn <= win) & (jn < tq))[hs]
                       .reshape(rows, PAGE))
    return per_state, jnp.stack(per_new)


def _dil_sample(q3, kv3, states, tables, *, tq):
    per_state, bias_new = tables
    bsz = states[0].shape[0]
    gw = ODD_GW
    st_spec = lambda s: pl.BlockSpec((1,) + s.shape[1:], lambda b: (b, 0, 0))
    grp = lambda w: pl.BlockSpec((3, tq, w), lambda b: (0, b, 0))
    return pl.pallas_call(
        functools.partial(_dil_sample_body, tq=tq), grid=(bsz,),
        in_specs=[grp(gw), grp(2 * gw)] + [st_spec(s) for s in states] + [_const(t.shape) for t in per_state]
        + [_const(bias_new.shape)],
        out_specs=[grp(gw), grp(gw)] + [st_spec(s) for s in states],
        out_shape=[jax.ShapeDtypeStruct((3, bsz * tq, gw), F32)] * 2
        + [jax.ShapeDtypeStruct(s.shape, F32) for s in states],
        scratch_shapes=[pltpu.VMEM((PAGE, 2 * gw), F32)],
        compiler_params=_cparams("parallel"), name="dilated_sample",
    )(q3, kv3, *states, *per_state, bias_new)


def _pad_even_w(w_in):
    return jnp.concatenate([w_in, jnp.zeros((w_in.shape[0], EVEN_PAD - w_in.shape[1]), w_in.dtype)], axis=1).astype(BF16)


def _even_mixer_prompt(h, gain, w_in, w_out, lb, hg_norm, qk_norm, cmp_pe, cmp_w1, cmp_w2, rel_bias, *, batch, seq, tm):
    n = batch * seq
    hg, q, cmp_rows, slcg, wing, gates_t, cmp_t, slc_t, win_t = _inproj_even(
        h, gain, _pad_even_w(w_in), qk_norm, tm=tm, seq=seq)
    s0 = jnp.zeros((batch, HG_HEADS, HEAD_DIM, HEAD_DIM), F32)
    o_h, s_fin = _hgrn(hg, lb, hg_norm, s0, batch=batch, seq=seq)
    cw = _compress_weights(cmp_pe, cmp_w1, cmp_w2, qk_norm[1])
    npg = seq // PAGE
    table = jnp.arange(batch * npg, dtype=jnp.int32).reshape(batch, npg)
    _, kcv = _compress(cmp_rows.reshape(n // PAGE, PAGE, ROWS_NSA), table, cw, transposed=False)
    tables = _nsa_prompt_t_tables(rel_bias, seq, kcv.shape[2])
    o_nt = _nsa_prompt_t(q, gates_t, kcv, slcg, wing, qk_norm[0], tables, batch=batch, seq=seq)
    wo = w_out.astype(BF16)
    hw = HG_HEADS * HEAD_DIM
    h = _outproj2t(h, o_h, o_nt, wo[:hw], wo[hw:], tm=tm)
    keep = min(NSA_WINDOW, seq)
    return h, {"hgrn": s_fin, "cmp": _rows_from_t(cmp_t, NSA_KV), "slc": _rows_from_t(slc_t, NSA_KV),
               "win": _rows_from_t(win_t[:, :, seq - keep:], NSA_KV)}


def _rows_from_t(a, heads):
    return jnp.moveaxis(a.reshape(a.shape[0], 2, heads, HEAD_DIM, a.shape[2]), -1, 1)


def _even_mixer_sample(h, gain, w_in, w_out, lb, hg_norm, qk_norm, cmp_pe, cmp_w1, cmp_w2, rel_bias,
                       state_hgrn, cmp_pool, slc_pool, win_state, page_table, *, batch, tq, tm):
    assert tq < CMP_STRIDE and tq & (tq - 1) == 0
    hg, q, cmp_rows, slc_rows, win_rows, gates = _inproj_even(h, gain, _pad_even_w(w_in), qk_norm, tm=tm)
    o_h, s_fin = _hgrn(hg, lb, hg_norm, state_hgrn, batch=batch, seq=tq)
    cw = _compress_weights(cmp_pe, cmp_w1, cmp_w2, qk_norm[1])
    rows_t = lambda a: jnp.moveaxis(a, 1, -1).reshape(a.shape[0], ROWS_NSA, a.shape[1])
    crows, _ = _compress(rows_t(cmp_pool), page_table, cw, transposed=True)
    past = page_table.shape[1] * PAGE
    w_state = win_state.shape[1]
    tables = _nsa_sample_tables(rel_bias, past, tq, crows.shape[1], w_state)
    o_n, new_win_t = _nsa_sample(q, gates, crows, rows_t(slc_pool), page_table, slc_rows, rows_t(win_state), win_rows,
                                 qk_norm[0], tables, tq=tq)
    wo = w_out.astype(BF16)
    hw = HG_HEADS * HEAD_DIM
    h = _outproj2(h, o_h, o_n, wo[:hw], wo[hw:], tm=tm)
    new_win = jnp.moveaxis(new_win_t.reshape((batch,) + win_state.shape[2:] + (w_state,)), -1, 1)
    return h, {"hgrn": s_fin, "cmp": cmp_rows, "slc": slc_rows, "win": new_win}


def _odd_mixer_prompt(h, gain, w_in, w_out, qk_norm, rel_bias, *, batch, seq, tm):
    qv0, kvv0, qv1, kvv1, qv2, kvv2, *kv_t = _inproj_odd(h, gain, w_in.astype(BF16), qk_norm, tm=tm, seq=seq)
    outs, lses = _dil_prompt([qv0, qv1, qv2], [kvv0, kvv1, kvv2], _dil_prompt_tables(rel_bias),
                             batch=batch, seq=seq)
    bufs = [_rows_from_t(t[:, :, seq - min(w, seq):], DIL_HEADS) for t, (w, _) in zip(kv_t, DIL_PATTERN)]
    return _outproj_merge(h, outs, lses, w_out.astype(BF16), tm=tm, views=True), bufs


def _odd_mixer_sample(h, gain, w_in, w_out, qk_norm, rel_bias, states, *, batch, tq, tm):
    q3, kv3 = _inproj_odd(h, gain, w_in.astype(BF16), qk_norm, tm=tm)
    tables = _dil_sample_tables(rel_bias, tq, [s.shape[1] for s in states])
    states_t = [jnp.moveaxis(s, 1, -1).reshape(batch, 2 * ODD_GW, s.shape[1]) for s in states]
    o3, lse3, *new_t = _dil_sample(q3, kv3, states_t, tables, tq=tq)
    new = [jnp.moveaxis(n.reshape(s.shape[:1] + s.shape[2:] + s.shape[1:2]), -1, 1) for n, s in zip(new_t, states)]
    h = _outproj_merge(h, [o3[g] for g in range(3)], [lse3[g] for g in range(3)], w_out.astype(BF16), tm=tm)
    return h, new


TM_PROMPT = 512
TM_SAMPLE = 256


def kernel(x_prompt, x_sample, p_prompt, p_sample, state_hgrn, cache_nsa_cmp_kv, cache_nsa_slc_kv, state_nsa_win_kv,
           state_dil_kv_0, state_dil_kv_1, state_dil_kv_2, page_table, rel_bias, norm_ffn1, norm_mix, norm_ffn2,
           norm_ple, w_ffn1_in, w_ffn1_out, w_ffn2_in, w_ffn2_out, w_ple_gate, w_ple_proj, w_in_even, w_out_even,
           hgrn_lb_logits, hgrn_norm, nsa_qk_norm, nsa_cmp_pe, nsa_cmp_w1, nsa_cmp_w2, w_in_odd, w_out_odd,
           dil_qk_norm):
    depth = norm_ffn1.shape[0]
    dil_states = (state_dil_kv_0, state_dil_kv_1, state_dil_kv_2)
    bf = lambda w: w.astype(BF16)
    w1i, w1o, w2i, w2o, wpg, wpp = (bf(w) for w in (w_ffn1_in, w_ffn1_out, w_ffn2_in, w_ffn2_out, w_ple_gate,
                                                     w_ple_proj))
    lb_all = jnp.cumsum(jax.nn.softmax(hgrn_lb_logits.astype(F32), axis=0), axis=0)

    def run(x, p, sample):
        batch, seq, d = x.shape
        n = batch * seq
        tm = TM_SAMPLE if sample else TM_PROMPT
        h = x.reshape(n, d)
        p = p.reshape(depth, n, p.shape[-1])
        even, odd = [], []
        for li in range(depth):
            h = _ffn(h, norm_ffn1[li], w1i[li], w1o[li], tm=tm)
            if li % 2 == 0:
                ei = li // 2
                args = (h, norm_mix[li], w_in_even[ei], w_out_even[ei], lb_all[ei], hgrn_norm[ei], nsa_qk_norm[ei],
                        nsa_cmp_pe[ei], nsa_cmp_w1[ei], nsa_cmp_w2[ei], rel_bias)
                if sample:
                    h, st = _even_mixer_sample(*args, state_hgrn[ei], cache_nsa_cmp_kv[ei], cache_nsa_slc_kv[ei],
                                               state_nsa_win_kv[ei], page_table, batch=batch, tq=seq, tm=tm)
                else:
                    h, st = _even_mixer_prompt(*args, batch=batch, seq=seq, tm=tm)
                even.append(st)
            else:
                oi = li // 2
                args = (h, norm_mix[li], w_in_odd[oi], w_out_odd[oi], dil_qk_norm[oi], rel_bias)
                if sample:
                    h, bufs = _odd_mixer_sample(*args, [s[oi] for s in dil_states], batch=batch, tq=seq, tm=tm)
                else:
                    h, bufs = _odd_mixer_prompt(*args, batch=batch, seq=seq, tm=tm)
                odd.append(bufs)
            h = _ffn(h, norm_ffn2[li], w2i[li], w2o[li], ple=(p[li], norm_ple[li], wpg[li], wpp[li]), tm=tm)
        rows = lambda a: a.reshape(batch, -1, 2, NSA_KV, HEAD_DIM)
        drows = lambda a: a.reshape(batch, -1, 2, DIL_HEADS, HEAD_DIM)
        return (h.reshape(batch, seq, d),
                jnp.stack([s["hgrn"] for s in even]), jnp.stack([rows(s["cmp"]) for s in even]),
                jnp.stack([rows(s["slc"]) for s in even]), jnp.stack([rows(s["win"]) for s in even]),
                jnp.stack([drows(b[0]) for b in odd]), jnp.stack([drows(b[1]) for b in odd]),
                jnp.stack([drows(b[2]) for b in odd]))

    y_p, hg_p, cmp_p, slc_p, win_p, d0_p, d1_p, d2_p = run(x_prompt, p_prompt, False)
    y_s, hg_s, cmp_s, slc_s, win_s, d0_s, d1_s, d2_s = run(x_sample, p_sample, True)
    return (y_p, y_s, hg_p, hg_s, cmp_p, cmp_s, slc_p, slc_s, win_p, win_s, d0_p, d0_s, d1_p, d1_s, d2_p, d2_s)
```

```python
import functools
import math

import numpy as np
import jax
import jax.numpy as jnp
from jax import lax
from jax.experimental import pallas as pl
from jax.experimental.pallas import tpu as pltpu

F32 = jnp.float32
BF16 = jnp.bfloat16
HIGHEST = lax.Precision.HIGHEST

V7X_VMEM_BYTES = 64 * 1024 * 1024
VMEM_LIMIT = V7X_VMEM_BYTES * 7 // 8
LANES = 128

EPS = 1e-6
NEG = -1e30
NEG_TEST = -1e29

HEAD_DIM = 64
HG_HEADS = 8
HG_CHUNK = 128
NSA_HEADS = 12
NSA_KV = 3
NSA_GROUP = NSA_HEADS // NSA_KV
CMP_BLOCK = 32
CMP_STRIDE = 16
SEL_BLOCK = 64
SEL_TOPK = 16
NSA_WINDOW = 512
FORCE_SCORE = 1e4
PAGE = 128
DIL_PATTERN = ((128, 1), (512, 4), (2048, 16))
DIL_HEADS = 4
RP_BUCKETS = 32
RP_MAX_DIST = 2048


def _cparams(*sem):
    return pltpu.CompilerParams(dimension_semantics=sem, vmem_limit_bytes=VMEM_LIMIT)


def _const(shape):
    nd = len(shape)
    return pl.BlockSpec(shape, lambda *_: (0,) * nd, pipeline_mode=pl.Buffered(1))


def _dot(a, b):
    return jnp.dot(a, b, preferred_element_type=F32)


def _dot_nt(a, b):
    return lax.dot_general(a, b, (((1,), (1,)), ((), ())), preferred_element_type=F32)


def _dot_tn(a, b):
    return lax.dot_general(a, b, (((0,), (0,)), ((), ())), preferred_element_type=F32)


def _rms(x, gain):
    return x * lax.rsqrt(jnp.mean(x * x, axis=-1, keepdims=True) + EPS) * gain


def _seg_ms(x, bd):
    sq = x * x
    hi = sq.astype(BF16)
    lo = (sq - hi.astype(F32)).astype(BF16)
    bw = bd.shape[0]
    cols = []
    for c0 in range(0, x.shape[1], bw):
        wd = min(bw, x.shape[1] - c0)
        blk = bd[:wd, :wd]
        cols.append(_dot(hi[:, c0:c0 + wd], blk) + _dot(lo[:, c0:c0 + wd], blk))
    return (cols[0] if len(cols) == 1 else jnp.concatenate(cols, axis=1)) * (1.0 / HEAD_DIM)


def _seg_rms(x, bd, gain):
    return x * lax.rsqrt(_seg_ms(x, bd) + EPS) * gain


def _norm_k_rows(x, bd, gain, kmask):
    bw = bd.shape[0]
    r = lax.rsqrt(_seg_ms(x[:, :bw], bd) + EPS)
    r = jnp.concatenate([r, jnp.ones((x.shape[0], x.shape[1] - bw), F32)], axis=1)
    return x * jnp.where(kmask > 0.5, r, 1.0) * gain


V7X_MXU_WIDTH = 256


def _block_diag_ones(width):
    i = np.arange(min(width, V7X_MXU_WIDTH)) // HEAD_DIM
    return jnp.asarray((i[:, None] == i[None, :]).astype(np.float32), BF16)


def _rel_bucket(dist):
    exact = RP_BUCKETS // 2
    d = jnp.maximum(dist, 0)
    log_ratio = jnp.log(jnp.maximum(d, 1).astype(F32) / exact) / math.log(RP_MAX_DIST / exact)
    large = jnp.minimum(exact + (log_ratio * (RP_BUCKETS - exact)).astype(jnp.int32), RP_BUCKETS - 1)
    return jnp.where(d < exact, d, large)


def _bias_tile(rel_bias, dist, valid):
    onehot = (_rel_bucket(dist)[..., None] == jnp.arange(RP_BUCKETS)).astype(F32)
    t = jnp.einsum('...k,kh->h...', onehot, rel_bias.astype(F32), precision=HIGHEST)
    return jnp.where(valid[None], t, NEG)


def _toeplitz(ext, n, m):
    p = n + m - 1
    u = jnp.concatenate([jnp.flip(ext[..., :m], -1), jnp.flip(ext[..., m:], -1)], axis=-1)
    reps = (1,) * (ext.ndim - 1) + (n,)
    t = jnp.tile(u, reps)[..., :n * (p - 1)].reshape(ext.shape[:-1] + (n, p - 1))
    return t[..., :m]


def _softmax_piece(s, mask, m_prev, l_prev, acc_prev, v, v_t=False):
    m_new = jnp.maximum(m_prev, jnp.max(s, axis=-1, keepdims=True))
    alpha = jnp.exp(m_prev - m_new)
    p = jnp.where(mask, jnp.exp(s - m_new), 0.0)
    l_new = alpha * l_prev + jnp.sum(p, axis=-1, keepdims=True)
    pb = p.astype(BF16)
    acc_new = alpha * acc_prev + (_dot_nt(pb, v) if v_t else _dot(pb, v))
    return m_new, l_new, acc_new


def _ffn_body(x_ref, g_ref, win_ref, wout_ref, *rest, dff, ple):
    x = x_ref[...]
    xn = _rms(x, g_ref[...]).astype(BF16)
    gu = _dot(xn, win_ref[...])
    gg = gu[:, :dff]
    a = (gg * jax.nn.sigmoid(gg) * gu[:, dff:]).astype(BF16)
    h = x + 0.5 * _dot(a, wout_ref[...])
    if ple:
        p_ref, gp_ref, wg_ref, wp_ref, o_ref = rest
        hn = _rms(h, gp_ref[...]).astype(BF16)
        gate = jax.nn.sigmoid(_dot(hn, wg_ref[...]))
        h = h + gate * _dot(p_ref[...].astype(BF16), wp_ref[...])
    else:
        (o_ref,) = rest
    o_ref[...] = h


def _layer(arr, li):
    tail = (0,) * (arr.ndim - 1)
    return pl.BlockSpec((None,) + arr.shape[1:], lambda *_: (li,) + tail, pipeline_mode=pl.Buffered(1))


def _ffn(h, li, gain, w_in, w_out, ple=None, *, tm):
    n, d = h.shape
    dff = w_out.shape[1]
    row = lambda i: (i, 0)
    in_specs = [pl.BlockSpec((tm, d), row), _layer(gain, li), _layer(w_in, li), _layer(w_out, li)]
    args = [h, gain, w_in, w_out]
    if ple is not None:
        p, gp, wg, wp = ple
        in_specs += [pl.BlockSpec((None, tm, p.shape[2]), lambda i: (li, i, 0)), _layer(gp, li), _layer(wg, li),
                     _layer(wp, li)]
        args += [p, gp, wg, wp]
    return pl.pallas_call(
        functools.partial(_ffn_body, dff=dff, ple=ple is not None),
        grid=(n // tm,), in_specs=in_specs, out_specs=pl.BlockSpec((tm, d), row),
        out_shape=jax.ShapeDtypeStruct((n, d), F32), compiler_params=_cparams("parallel"),
        name="ffn_ple" if ple is not None else "ffn",
    )(*args)


EVEN_HG = 4 * HG_HEADS * HEAD_DIM
EVEN_Q = NSA_HEADS * HEAD_DIM
ROWS_NSA = 2 * NSA_KV * HEAD_DIM
EVEN_PAD = EVEN_HG + EVEN_Q + 3 * ROWS_NSA + LANES


GATE_ROWS = 16


def _inproj_even_body(x_ref, g_ref, w_ref, gk_ref, km_ref, bd_ref, hg_ref, q_ref, cmp_ref, *outs, prompt):
    xn = _rms(x_ref[...], g_ref[...]).astype(BF16)
    p = _dot(xn, w_ref[...])
    c0 = EVEN_HG
    c1 = c0 + EVEN_Q
    hg_ref[...] = p[:, :c0]
    q_ref[...] = p[:, c0:c1]
    cmp = p[:, c1:c1 + ROWS_NSA]
    cmp_ref[...] = cmp
    bd = bd_ref[...]
    km = km_ref[...]
    slc = _norm_k_rows(p[:, c1 + ROWS_NSA:c1 + 2 * ROWS_NSA], bd, gk_ref[0:1, :], km)
    win = _norm_k_rows(p[:, c1 + 2 * ROWS_NSA:c1 + 3 * ROWS_NSA], bd, gk_ref[1:2, :], km)
    sig = jax.nn.sigmoid(p[:, c1 + 3 * ROWS_NSA:])
    if not prompt:
        slc_ref, win_ref, gate_ref = outs
        slc_ref[...] = slc
        win_ref[...] = win
        for g in range(NSA_KV):
            gate_ref[g] = sig if g == 0 else pltpu.roll(sig, LANES - 3 * NSA_GROUP * g, 1)
        return
    slcg_ref, wing_ref, gate_t_ref, cmp_t_ref, slc_t_ref, win_t_ref = outs
    kw = NSA_KV * HEAD_DIM
    sig_t = sig.T
    for g in range(NSA_KV):
        ks = slice(HEAD_DIM * g, HEAD_DIM * (g + 1))
        vs = slice(kw + HEAD_DIM * g, kw + HEAD_DIM * (g + 1))
        slcg_ref[g] = jnp.concatenate([slc[:, ks], slc[:, vs]], axis=1)
        wing_ref[g] = jnp.concatenate([win[:, ks], win[:, vs]], axis=1)
        for half in range(sig.shape[0] // NSA_TQ):
            gate_t_ref[g, half] = sig_t[3 * NSA_GROUP * g:3 * NSA_GROUP * g + GATE_ROWS,
                                        half * NSA_TQ:(half + 1) * NSA_TQ]
    for dst_ref, rows in ((cmp_t_ref, cmp), (slc_t_ref, slc), (win_t_ref, win)):
        _store_transposed(dst_ref, rows)


def _store_transposed(dst_ref, x):
    for c in range(x.shape[1] // LANES):
        dst_ref[0, c * LANES:(c + 1) * LANES, :] = x[:, c * LANES:(c + 1) * LANES].T


def _inproj_even(h, gain, w_pad, qk_norm, *, tm, seq=None):
    n, d = h.shape
    kw = NSA_KV * HEAD_DIM
    ones = jnp.ones((kw,), F32)
    gk = jnp.stack([jnp.concatenate([jnp.tile(qk_norm[2], NSA_KV), ones]),
                    jnp.concatenate([jnp.tile(qk_norm[3], NSA_KV), ones])])
    km = jnp.concatenate([ones, 0.0 * ones]).reshape(1, ROWS_NSA)
    row = lambda i: (i, 0)
    grp = lambda i: (0, i, 0)
    shp = lambda *s: jax.ShapeDtypeStruct(s, F32)
    out_specs = [pl.BlockSpec((tm, EVEN_HG), row), pl.BlockSpec((tm, EVEN_Q), row), pl.BlockSpec((tm, ROWS_NSA), row)]
    out_shape = [shp(n, EVEN_HG), shp(n, EVEN_Q), shp(n, ROWS_NSA)]
    if seq is None:
        out_specs += [pl.BlockSpec((tm, ROWS_NSA), row)] * 2 + [pl.BlockSpec((NSA_KV, tm, LANES), grp)]
        out_shape += [shp(n, ROWS_NSA)] * 2 + [shp(NSA_KV, n, LANES)]
    else:
        tpb = seq // tm
        out_specs += [pl.BlockSpec((NSA_KV, tm, LANES), grp)] * 2
        out_shape += [shp(NSA_KV, n, LANES)] * 2
        out_specs.append(pl.BlockSpec((NSA_KV, tm // NSA_TQ, GATE_ROWS, NSA_TQ), lambda i: (0, i, 0, 0)))
        out_shape.append(shp(NSA_KV, n // NSA_TQ, GATE_ROWS, NSA_TQ))
        out_specs += [pl.BlockSpec((1, ROWS_NSA, tm), lambda i: (i // tpb, 0, i % tpb))] * 3
        out_shape += [shp(n // seq, ROWS_NSA, seq)] * 3
    return pl.pallas_call(
        functools.partial(_inproj_even_body, prompt=seq is not None), grid=(n // tm,),
        in_specs=[pl.BlockSpec((tm, d), row), _const((1, d)), _const(w_pad.shape), _const((2, ROWS_NSA)),
                  _const((1, ROWS_NSA)), _const(_block_diag_ones(ROWS_NSA).shape)],
        out_specs=out_specs, out_shape=out_shape,
        compiler_params=_cparams("parallel"), name="inproj_even",
    )(h, gain.reshape(1, d), w_pad, gk, km, _block_diag_ones(ROWS_NSA))


ODD_W = 3 * DIL_HEADS * HEAD_DIM
ODD_GW = DIL_HEADS * HEAD_DIM


def _to_residue_view(src, dst_ref, stage_ref, dil):
    rows, w = src.shape
    for blk in range(w // LANES):
        stage_ref[...] = src[:, blk * LANES:(blk + 1) * LANES]
        for r in range(dil):
            lo = r * w + blk * LANES
            dst_ref[:, lo:lo + LANES] = stage_ref[pl.ds(r, rows // dil, stride=dil), :]


def _from_residue_view(src_ref, stage_ref, dil, w):
    rows = src_ref.shape[0] * dil
    parts = []
    for blk in range(w // LANES):
        for r in range(dil):
            lo = r * w + blk * LANES
            stage_ref[pl.ds(r, rows // dil, stride=dil), :] = src_ref[:, lo:lo + LANES]
        parts.append(stage_ref[...])
    return jnp.concatenate(parts, axis=1)


def _inproj_odd_body(x_ref, g_ref, w_ref, gq_ref, gk_ref, bd_ref, *outs, prompt):
    xn = _rms(x_ref[...], g_ref[...]).astype(BF16)
    p = _dot(xn, w_ref[...])
    bd = bd_ref[...]
    q = _seg_rms(p[:, :ODD_W], bd, gq_ref[...]) * (HEAD_DIM ** -0.5)
    k = _seg_rms(p[:, ODD_W:2 * ODD_W], bd, gk_ref[...])
    v = p[:, 2 * ODD_W:]
    for g in range(3):
        s = slice(ODD_GW * g, ODD_GW * (g + 1))
        kv = jnp.concatenate([k[:, s], v[:, s]], axis=1)
        if not prompt:
            outs[0][g] = q[:, s]
            outs[1][g] = kv
            continue
        dil = DIL_PATTERN[g][1]
        if dil == 1:
            outs[2 * g][...] = q[:, s]
            outs[2 * g + 1][...] = kv
        else:
            _to_residue_view(q[:, s], outs[2 * g], outs[-1], dil)
            _to_residue_view(kv, outs[2 * g + 1], outs[-1], dil)
        _store_transposed(outs[6 + g], kv)


def _inproj_odd(h, gain, w, qk_norm, *, tm, seq=None):
    n, d = h.shape
    row = lambda i: (i, 0)
    nh = 3 * DIL_HEADS
    prompt = seq is not None
    if not prompt:
        out_specs = [pl.BlockSpec((3, tm, ODD_GW), lambda i: (0, i, 0)),
                     pl.BlockSpec((3, tm, 2 * ODD_GW), lambda i: (0, i, 0))]
        out_shape = [jax.ShapeDtypeStruct((3, n, ODD_GW), F32), jax.ShapeDtypeStruct((3, n, 2 * ODD_GW), F32)]
    else:
        out_specs, out_shape = [], []
        for _, dil in DIL_PATTERN:
            assert dil == 1 or DIL_PATTERN[0][1] == 1
            for width in (ODD_GW, 2 * ODD_GW):
                out_specs.append(pl.BlockSpec((tm // dil, dil * width), row))
                out_shape.append(jax.ShapeDtypeStruct((n // dil, dil * width), F32))
        tpb = seq // tm
        for _ in DIL_PATTERN:
            out_specs.append(pl.BlockSpec((1, 2 * ODD_GW, tm), lambda i: (i // tpb, 0, i % tpb)))
            out_shape.append(jax.ShapeDtypeStruct((n // seq, 2 * ODD_GW, seq), F32))
    return pl.pallas_call(
        functools.partial(_inproj_odd_body, prompt=prompt), grid=(n // tm,),
        in_specs=[pl.BlockSpec((tm, d), row), _const((1, d)), _const(w.shape), _const((1, ODD_W)),
                  _const((1, ODD_W)), _const(_block_diag_ones(ODD_W).shape)],
        out_specs=out_specs, out_shape=out_shape,
        scratch_shapes=[pltpu.VMEM((tm, LANES), F32)] if prompt else [],
        compiler_params=_cparams("parallel"), name="inproj_odd",
    )(h, gain.reshape(1, d), w, jnp.tile(qk_norm[0], nh).reshape(1, ODD_W),
      jnp.tile(qk_norm[1], nh).reshape(1, ODD_W), _block_diag_ones(ODD_W))


def _outproj2_body(h_ref, a_ref, b_ref, w_ref, o_ref):
    ka = a_ref.shape[1]
    o_ref[...] = (h_ref[...] + _dot(a_ref[...].astype(BF16), w_ref[:ka])
                  + _dot(b_ref[...].astype(BF16), w_ref[ka:]))


def _outproj2(h, a, b, w, *, tm):
    n, d = h.shape
    row = lambda i: (i, 0)
    return pl.pallas_call(
        _outproj2_body, grid=(n // tm,),
        in_specs=[pl.BlockSpec((tm, d), row), pl.BlockSpec((tm, a.shape[1]), row),
                  pl.BlockSpec((tm, b.shape[1]), row), _const(w.shape)],
        out_specs=pl.BlockSpec((tm, d), row), out_shape=jax.ShapeDtypeStruct((n, d), F32),
        compiler_params=_cparams("parallel"), name="outproj_even",
    )(h, a, b, w)


def _outproj2t_body(h_ref, a_ref, bt_ref, w_ref, o_ref):
    ka = a_ref.shape[1]
    acc = h_ref[...] + _dot(a_ref[...].astype(BF16), w_ref[:ka])
    tq = bt_ref.shape[2]
    for i in range(bt_ref.shape[0]):
        rs = slice(i * tq, (i + 1) * tq)
        o_ref[rs, :] = acc[rs, :] + _dot_tn(bt_ref[i].astype(BF16), w_ref[ka:])


def _outproj2t(h, a, bt, w, *, tm):
    n, d = h.shape
    row = lambda i: (i, 0)
    tq = bt.shape[2]
    return pl.pallas_call(
        _outproj2t_body, grid=(n // tm,),
        in_specs=[pl.BlockSpec((tm, d), row), pl.BlockSpec((tm, a.shape[1]), row),
                  pl.BlockSpec((tm // tq, bt.shape[1], tq), lambda i: (i, 0, 0)), _const(w.shape)],
        out_specs=pl.BlockSpec((tm, d), row), out_shape=jax.ShapeDtypeStruct((n, d), F32),
        compiler_params=_cparams("parallel"), name="outproj_even_t",
    )(h, a, bt, w)


def _outproj_merge_body(h_ref, o0_ref, o1_ref, o2_ref, l0_ref, l1_ref, l2_ref, w_ref, o_ref, *stage, views):
    if views:
        load = lambda ref, g: (ref[...] if DIL_PATTERN[g][1] == 1
                               else _from_residue_view(ref, stage[0], DIL_PATTERN[g][1], ODD_GW))
    else:
        load = lambda ref, g: ref[...]
    mixed = _merge_groups([load(r, g) for g, r in enumerate((o0_ref, o1_ref, o2_ref))],
                          [load(r, g) for g, r in enumerate((l0_ref, l1_ref, l2_ref))])
    o_ref[...] = h_ref[...] + _dot(mixed.astype(BF16), w_ref[...])


def _outproj_merge(h, outs, lses, w, *, tm, views=False):
    n, d = h.shape
    row = lambda i: (i, 0)
    parts = [pl.BlockSpec((tm * a.shape[0] // n, a.shape[1]), row) for a in list(outs) + list(lses)]
    return pl.pallas_call(
        functools.partial(_outproj_merge_body, views=views), grid=(n // tm,),
        in_specs=[pl.BlockSpec((tm, d), row)] + parts + [_const(w.shape)],
        out_specs=pl.BlockSpec((tm, d), row), out_shape=jax.ShapeDtypeStruct((n, d), F32),
        scratch_shapes=[pltpu.VMEM((tm, LANES), F32)] if views else [],
        compiler_params=_cparams("parallel"), name="outproj_odd",
    )(h, *outs, *lses, w)


def _hgrn_prefix_matrix(C):
    t = np.arange(C)[:, None]
    j = np.arange(C)[None, :]
    mats = [j <= t]
    lvl = 1
    while (1 << lvl) <= C:
        mats.append(j <= ((t >> lvl) << lvl) + (1 << lvl) // 2 - 1)
        lvl += 1
    return jnp.asarray(np.concatenate(mats, axis=0).astype(np.float32), BF16)


def _hgrn_body(hq_ref, hf_ref, hi_ref, hgt_ref, lb_ref, gn_ref, bd_ref, pm_ref, s0_ref, o_ref, sfin_ref, st_ref, *, C):
    c = pl.program_id(1)

    @pl.when(c == 0)
    def _():
        st_ref[...] = s0_ref[0]

    dk = HEAD_DIM
    w = HG_HEADS * dk
    lb = lb_ref[...]
    f = lb + (1.0 - lb) * jax.nn.sigmoid(hf_ref[...])
    lf = jnp.log(f)
    kk = 1.0 - f
    ti = lax.broadcasted_iota(jnp.int32, (C, C), 0)
    si = lax.broadcasted_iota(jnp.int32, (C, C), 1)
    p1 = lf.astype(BF16)
    r1 = lf - p1.astype(F32)
    p2 = r1.astype(BF16)
    p3 = (r1 - p2.astype(F32)).astype(BF16)
    sums = _dot(pm_ref[...], jnp.concatenate([p1, p2, p3], axis=1))
    sums = sums[:, :w] + sums[:, w:2 * w] + sums[:, 2 * w:]
    b = sums[0:C]
    q = hq_ref[...]
    v = hi_ref[...].astype(BF16)
    heads = range(HG_HEADS)
    hs = [slice(dk * h, dk * (h + 1)) for h in heads]

    qb = q.astype(BF16)
    kb = kk.astype(BF16)
    a = [jnp.where(ti == si, _dot_nt(qb[:, hs[h]], kb[:, hs[h]]), 0.0) for h in heads]
    lvl = 1
    while (1 << lvl) <= C:
        m = 1 << lvl
        half = m // 2
        r = sums[lvl * C:(lvl + 1) * C]
        e = jnp.exp(-jnp.abs(b - r))
        qt = (q * e).astype(BF16)
        kt = (kk * e).astype(BF16)
        pair = ((ti >> lvl) == (si >> lvl)) & ((ti & (m - 1)) >= half) & ((si & (m - 1)) < half)
        a = [a[h] + jnp.where(pair, _dot_nt(qt[:, hs[h]], kt[:, hs[h]]), 0.0) for h in heads]
        lvl += 1

    b_last = b[C - 1:C, :]
    qe = (q * jnp.exp(b)).astype(BF16)
    kdec = (kk * jnp.exp(b_last - b)).astype(BF16)
    e_last = jnp.exp(b_last)
    outs = []
    for h in heads:
        st = st_ref[h]
        outs.append(_dot(a[h].astype(BF16), v[:, hs[h]]) + _dot_nt(qe[:, hs[h]], st.astype(BF16)))
        st_ref[h] = st * e_last[:, hs[h]] + _dot_tn(v[:, hs[h]], kdec[:, hs[h]])
    o = jnp.concatenate(outs, axis=1)
    gt = hgt_ref[...]
    o_ref[...] = _seg_rms(o, bd_ref[...], gn_ref[...]) * (gt * jax.nn.sigmoid(gt))

    @pl.when(c == pl.num_programs(1) - 1)
    def _():
        sfin_ref[0] = st_ref[...]


def _hgrn(hg, lb, hg_norm, s0, *, batch, seq):
    C = math.gcd(seq, HG_CHUNK)
    nc = seq // C
    w = HG_HEADS * HEAD_DIM
    col = lambda j: pl.BlockSpec((C, w), lambda b, c: (b * nc + c, j))
    st_spec = pl.BlockSpec((1, HG_HEADS, HEAD_DIM, HEAD_DIM), lambda b, c: (b, 0, 0, 0))
    pm = _hgrn_prefix_matrix(C)
    o, sfin = pl.pallas_call(
        functools.partial(_hgrn_body, C=C), grid=(batch, nc),
        in_specs=[col(0), col(1), col(2), col(3), _const((1, w)), _const((1, w)), _const(_block_diag_ones(w).shape), _const(pm.shape),
                  st_spec],
        out_specs=[pl.BlockSpec((C, w), lambda b, c: (b * nc + c, 0)), st_spec],
        out_shape=[jax.ShapeDtypeStruct((batch * seq, w), F32),
                   jax.ShapeDtypeStruct((batch, HG_HEADS, HEAD_DIM, HEAD_DIM), F32)],
        scratch_shapes=[pltpu.VMEM((HG_HEADS, HEAD_DIM, HEAD_DIM), F32)],
        compiler_params=_cparams("parallel", "arbitrary"), name="hgrn_scan",
    )(hg, hg, hg, hg, lb.reshape(1, w), jnp.tile(hg_norm, HG_HEADS).reshape(1, w), _block_diag_ones(w), pm,
      jnp.swapaxes(s0, 2, 3))
    return o, jnp.swapaxes(sfin, 2, 3)


CHUNKS_PER_PAGE = PAGE // CMP_STRIDE
PAGE_LANES = CMP_STRIDE * ROWS_NSA
CMP_HID = 2 * NSA_KV * 2 * HEAD_DIM
CMP_PP = 8


def _compress_body(pt_ref, *refs, n_chunks, transposed):
    pages = refs[:CMP_PP]
    wbig_ref, pe_ref, w2_ref, gk_ref, km_ref, bd_ref, perm_ref, rows_ref, grp_ref, xs_ref = refs[CMP_PP:]
    nb = ROWS_NSA // LANES
    j = pl.program_id(1)
    perm = perm_ref[...]
    for i in range(0, CMP_PP, 2):
        pair = []
        for k in range(2):
            pg = pages[i + k][0]
            if transposed:
                pg = jnp.concatenate([pg[c * LANES:(c + 1) * LANES, :].T for c in range(nb)], axis=1)
            pair.append(_dot(perm, pg.astype(BF16)))
        r0 = pl.multiple_of((j * CMP_PP + i) * CHUNKS_PER_PAGE, 2 * CHUNKS_PER_PAGE)
        for s in range(CMP_STRIDE):
            rs = slice(s * CHUNKS_PER_PAGE, (s + 1) * CHUNKS_PER_PAGE)
            both = jnp.concatenate([pair[0][rs], pair[1][rs]], axis=0).astype(BF16)
            for c in range(nb):
                lo = (c * CMP_STRIDE + s) * LANES
                xs_ref[pl.ds(r0, 2 * CHUNKS_PER_PAGE), lo:lo + LANES] = both[:, c * LANES:(c + 1) * LANES]

    @pl.when(j == pl.num_programs(1) - 1)
    def _():
        blk_w = CMP_STRIDE * LANES
        first = lambda x: jnp.concatenate(
            [_dot(x[:, c * blk_w:(c + 1) * blk_w], wbig_ref[c]) for c in range(nb)], axis=1)
        y = first(xs_ref[...])
        ysh = pltpu.roll(y, n_chunks - 1, 0)
        hid = y + pltpu.roll(ysh, CMP_HID - HEAD_DIM, 1)
        cf = first(pe_ref[...].astype(BF16))
        lane = lax.broadcasted_iota(jnp.int32, (8, CMP_HID), 1)
        cs = jnp.where(((lane >> 6) & 1) == 0, jnp.broadcast_to(cf[0:1], (8, CMP_HID)),
                       jnp.broadcast_to(cf[1:2], (8, CMP_HID)))
        cs = cs + pltpu.roll(cs, CMP_HID - HEAD_DIM, 1)
        hid = hid + cs[0:1]
        act = (hid * jax.nn.sigmoid(hid)).astype(BF16)
        out = _dot(act, w2_ref[...])
        out = _norm_k_rows(out, bd_ref[...], gk_ref[...], km_ref[...])
        rows_ref[0] = out
        kw = NSA_KV * HEAD_DIM
        for g in range(NSA_KV):
            grp_ref[0, g] = jnp.concatenate([out[:, HEAD_DIM * g:HEAD_DIM * (g + 1)],
                                             out[:, kw + HEAD_DIM * g:kw + HEAD_DIM * (g + 1)]], axis=1)


def _compress_weights(cmp_pe, cmp_w1, cmp_w2, k_gain):
    ratio = CMP_BLOCK // CMP_STRIDE
    w1 = cmp_w1.reshape(2, ratio, CMP_STRIDE, HEAD_DIM, HEAD_DIM)
    eye_kv = jnp.eye(2, dtype=F32)
    eye_g = jnp.eye(NSA_KV, dtype=F32)
    nb = ROWS_NSA // LANES
    member_kv = np.arange(2 * NSA_KV) // NSA_KV
    w1m = w1[member_kv].reshape(nb, 2, ratio, CMP_STRIDE, HEAD_DIM, HEAD_DIM)
    wbig = jnp.einsum('chrsde,hH->cshdHre', w1m, eye_kv).reshape(nb, CMP_STRIDE * LANES, 2 * ratio * HEAD_DIM)
    wbig = wbig.astype(BF16)
    pe = cmp_pe.reshape(2, ratio, CMP_STRIDE, HEAD_DIM)
    pe_rows = pe[member_kv].reshape(nb, 2, ratio, CMP_STRIDE, HEAD_DIM).transpose(2, 0, 3, 1, 4)
    pe_rows = pe_rows.reshape(ratio, PAGE_LANES)
    pe_rows = jnp.concatenate([pe_rows, jnp.zeros((8 - ratio, PAGE_LANES), F32)], axis=0)
    sel_r0 = jnp.asarray([1.0, 0.0], F32)
    w2big = jnp.einsum('ked,kK,gG,r->kgreKGd', cmp_w2, eye_kv, eye_g, sel_r0).reshape(CMP_HID, ROWS_NSA).astype(BF16)
    kw = NSA_KV * HEAD_DIM
    gk = jnp.concatenate([jnp.tile(k_gain, NSA_KV), jnp.ones((kw,), F32)]).reshape(1, ROWS_NSA)
    km = jnp.concatenate([jnp.ones((kw,), F32), jnp.zeros((kw,), F32)]).reshape(1, ROWS_NSA)
    return wbig, pe_rows, w2big, gk, km


def _compress(pool, table, cw, *, transposed):
    wbig, pe_rows, w2big, gk, km = cw
    bsz, npg = table.shape
    n_chunks = npg * CHUNKS_PER_PAGE
    page = lambda i: pl.BlockSpec((1,) + pool.shape[1:], lambda b, j, pt: (pt[b, j * CMP_PP + i], 0, 0))
    tok = np.arange(PAGE)
    perm = (tok[None, :] == (tok[:, None] % CHUNKS_PER_PAGE) * CMP_STRIDE + tok[:, None] // CHUNKS_PER_PAGE)
    perm = jnp.asarray(perm.astype(np.float32), BF16)
    return pl.pallas_call(
        functools.partial(_compress_body, n_chunks=n_chunks, transposed=transposed),
        grid_spec=pltpu.PrefetchScalarGridSpec(
            num_scalar_prefetch=1, grid=(bsz, npg // CMP_PP),
            in_specs=[page(i) for i in range(CMP_PP)] + [
                _const(wbig.shape), _const(pe_rows.shape), _const(w2big.shape), _const((1, ROWS_NSA)),
                _const((1, ROWS_NSA)), _const(_block_diag_ones(ROWS_NSA).shape), _const(perm.shape)],
            out_specs=[pl.BlockSpec((1, n_chunks, ROWS_NSA), lambda b, j, pt: (b, 0, 0)),
                       pl.BlockSpec((1, NSA_KV, n_chunks, LANES), lambda b, j, pt: (b, 0, 0, 0))],
            scratch_shapes=[pltpu.VMEM((n_chunks, PAGE_LANES), BF16)]),
        out_shape=[jax.ShapeDtypeStruct((bsz, n_chunks, ROWS_NSA), F32),
                   jax.ShapeDtypeStruct((bsz, NSA_KV, n_chunks, LANES), F32)],
        compiler_params=_cparams("parallel", "arbitrary"), name="nsa_compress",
    )(table, *([pool] * CMP_PP), wbig, pe_rows, w2big, gk, km, _block_diag_ones(ROWS_NSA), perm)


def _select_blocks(imp, pos, n_slc, k_sel):
    lane = lax.broadcasted_iota(jnp.int32, imp.shape, 1)
    cur = pos >> 6
    forced = (lane == 0) | (lane == cur) | (lane == cur - 1)
    score = jnp.where(forced, FORCE_SCORE, jnp.where(lane <= cur, imp, -1.0))
    rank = jnp.zeros(imp.shape, F32)
    for i in range(n_slc):
        ci = score[:, i:i + 1]
        rank = rank + jnp.where(ci > score, 1.0, jnp.where(ci == score, jnp.where(lane > i, 1.0, 0.0), 0.0))
    return jnp.where(lane < n_slc, jnp.where(rank < k_sel, 1.0, 0.0), 0.0)


NSA_TQ = 256
NSA_QC = LANES


def _select_blocks_t(imp, pos, n_slc, k_sel):
    blk = lax.broadcasted_iota(jnp.int32, imp.shape, 0)
    cur = pos >> 6
    forced = (blk == 0) | (blk == cur) | (blk == cur - 1)
    score = jnp.where(forced, FORCE_SCORE, jnp.where(blk <= cur, imp, -1.0))
    rank = jnp.zeros(imp.shape, F32)
    for i in range(n_slc):
        ci = score[i:i + 1, :]
        rank = rank + jnp.where(ci > score, 1.0, jnp.where(ci == score, jnp.where(blk > i, 1.0, 0.0), 0.0))
    return jnp.where(rank < k_sel, 1.0, 0.0)


def _nsa_prompt_t_body(q_ref, gate_ref, kcv_ref, slc_ref, win_ref, gq_ref, bd_ref, eye_ref, fs_ref, fc_ref,
                       ov_ref, ex_ref, o_ref, bc_ref, bs_ref, bw_ref, qt_scr, ks_scr, kw_scr, vs_scr, vw_scr, sel_scr, m_scr, l_scr, acc_scr,
                       *, seq, n_slc, k_sel):
    tq = NSA_TQ
    qc = NSA_QC
    hd = HEAD_DIM
    n_tiles = seq // tq
    width = NSA_GROUP * tq
    n_chunks = width // qc
    eye = eye_ref[...]
    n_c = kcv_ref.shape[2]
    far = NSA_WINDOW // tq

    @pl.when(pl.program_id(1) == 0)
    def _():
        key_i = lax.broadcasted_iota(jnp.int32, (tq, tq), 0)
        qry_i = lax.broadcasted_iota(jnp.int32, (tq, tq), 1)
        blk_i = lax.broadcasted_iota(jnp.int32, (n_c, tq), 0)
        n_cmp = seq // CMP_STRIDE - CMP_BLOCK // CMP_STRIDE + 1
        span = tq + CMP_STRIDE * n_c
        for h in range(NSA_GROUP):
            hs = slice(h * tq, (h + 1) * tq)
            for d in range(n_tiles):
                u = jnp.broadcast_to(fs_ref[0, h:h + 1, d * tq:(d + 2) * tq], (tq, 2 * tq))
                tile = pltpu.roll(u, tq, 1, stride=1, stride_axis=0)[:, :tq]
                bs_ref[d, :, hs] = tile
                if d == far:
                    bw_ref[0, :, hs] = jnp.where(qry_i <= key_i, tile, NEG)
            for t in range(n_tiles):
                u = jnp.broadcast_to(fc_ref[0, h:h + 1, t * tq:t * tq + span], (n_c, span))
                tile = pltpu.roll(u, tq, 1, stride=CMP_STRIDE, stride_axis=0)[:, :tq]
                bc_ref[t, :, hs] = jnp.where(blk_i < n_cmp, tile, NEG)

    qn = (_seg_rms(q_ref[...], bd_ref[...], gq_ref[...]) * (hd ** -0.5)).astype(BF16)
    for t in range(n_tiles):
        rs = slice(t * tq, (t + 1) * tq)
        for h in range(NSA_GROUP):
            qt_scr[t, :, h * tq:(h + 1) * tq] = _dot_nt(eye, qn[rs, h * hd:(h + 1) * hd]).astype(BF16)
        for kv_ref, k_scr, v_scr in ((slc_ref, ks_scr, vs_scr), (win_ref, kw_scr, vw_scr)):
            kv = kv_ref[0, rs, :].astype(BF16)
            k_scr[t] = kv[:, :hd]
            v_scr[t] = _dot_nt(eye, kv[:, hd:]).astype(BF16)
    kc = kcv_ref[0, 0, :, :hd].astype(BF16)
    vct = _dot_nt(eye, kcv_ref[0, 0, :, hd:].astype(BF16)).astype(BF16)
    def flash_init():
        m_scr[...] = jnp.full((1, width), NEG, F32)
        l_scr[...] = jnp.zeros((1, width), F32)
        acc_scr[...] = jnp.zeros((hd, width), F32)

    def flash_update(k_scr, v_scr, t, tiles, running):
        s_alls = [_dot(k_scr[jt], qt_scr[t]) for jt, _, _ in tiles]
        if running:
            m_all = m_scr[...]
            l_all = l_scr[...]
        alphas, ls, ms, ps = [], [], [], [[] for _ in tiles]
        for c in range(n_chunks):
            cs = slice(c * qc, (c + 1) * qc)
            qoff = (c * qc) % tq
            ss = []
            for (_, bias_tile, sel_add), s_all in zip(tiles, s_alls):
                s = s_all[:, cs] + bias_tile(cs)
                ss.append(s if sel_add is None else s + sel_add[:, qoff:qoff + qc])
            m_new = functools.reduce(jnp.maximum, [jnp.max(s, axis=0, keepdims=True) for s in ss])
            if running:
                m_new = jnp.maximum(m_all[:, cs], m_new)
                alpha = jnp.exp(m_all[:, cs] - m_new)
                alphas.append(alpha)
            l_new = alpha * l_all[:, cs] if running else 0.0
            for k, s in enumerate(ss):
                p = jnp.exp(s - m_new)
                l_new = l_new + jnp.sum(p, axis=0, keepdims=True)
                ps[k].append(p.astype(BF16))
            ls.append(l_new)
            ms.append(m_new)
        cat = lambda xs: jnp.concatenate(xs, axis=1)
        pv = sum(_dot(v_scr[jt], cat(ps[k])) for k, (jt, _, _) in enumerate(tiles))
        acc_scr[...] = cat(alphas) * acc_scr[...] + pv if running else pv
        m_scr[...] = cat(ms)
        l_scr[...] = cat(ls)

    def flash_result():
        return acc_scr[...] / jnp.maximum(l_scr[...], 1e-30)

    def q_tile(t, carry):
        bias_c = bc_ref[t]
        s_c = _dot(kc, qt_scr[t]) + bias_c
        m_c = jnp.max(s_c, axis=0, keepdims=True)
        e_c = jnp.where(bias_c > NEG_TEST, jnp.exp(s_c - m_c), 0.0)
        p_c = (e_c / jnp.maximum(jnp.sum(e_c, axis=0, keepdims=True), 1e-30)).astype(BF16)
        o_c = _dot(vct, p_c)
        imp = sum(_dot(ov_ref[...], p_c[:, h * tq:(h + 1) * tq]) for h in range(NSA_GROUP))
        pos = t * tq + lax.broadcasted_iota(jnp.int32, (1, tq), 1)
        sel = _select_blocks_t(imp[:n_slc], pos, n_slc, k_sel)
        sel_scr[...] = jnp.zeros(sel_scr.shape, BF16)
        sel_scr[0:n_slc, :] = ((1.0 - sel) * NEG).astype(BF16)
        flash_init()
        slc_tile = lambda jt: (jt, lambda cs: bs_ref[t - jt, :, cs], _dot(ex_ref[jt], sel_scr[...]))

        def slc_pair(i, carry):
            flash_update(ks_scr, vs_scr, t, [slc_tile(2 * i), slc_tile(2 * i + 1)], True)
            return carry

        lax.fori_loop(0, (t + 1) // 2, slc_pair, 0)
        pl.when((t & 1) == 0)(lambda: flash_update(ks_scr, vs_scr, t, [slc_tile(t)], True))
        o_s = flash_result()
        far = NSA_WINDOW // tq
        win_tile = lambda d: (t - d, (lambda cs: bw_ref[0, :, cs]) if d == far else (lambda cs: bs_ref[d, :, cs]),
                              None)
        for n_back in range(far + 1):
            cond = (t == n_back) if n_back < far else (t >= far)
            pl.when(cond)(functools.partial(flash_update, kw_scr, vw_scr, t,
                                            [win_tile(d) for d in range(n_back, -1, -1)], False))
        o_w = flash_result()
        gates = gate_ref[0, t]
        for h in range(NSA_GROUP):
            hs = slice(h * tq, (h + 1) * tq)
            o_ref[t, h * hd:(h + 1) * hd, :] = (gates[3 * h:3 * h + 1] * o_c[:, hs]
                                                + gates[3 * h + 1:3 * h + 2] * o_s[:, hs]
                                                + gates[3 * h + 2:3 * h + 3] * o_w[:, hs])
        return carry

    lax.fori_loop(0, n_tiles, q_tile, 0)


def _nsa_prompt_t_tables(rel_bias, seq, n_c):
    tq = NSA_TQ
    n_tiles = seq // tq
    n_slc = -(-seq // SEL_BLOCK)
    ds = jnp.arange((n_tiles + 1) * tq) - tq
    fs = _bias_tile(rel_bias, ds, ds >= 0).reshape(NSA_KV, NSA_GROUP, -1)
    dc = jnp.arange(n_tiles * tq + CMP_STRIDE * n_c) - (CMP_STRIDE * n_c + CMP_BLOCK - 1)
    fc = _bias_tile(rel_bias, dc, dc >= 0).reshape(NSA_KV, NSA_GROUP, -1)
    n_cmp = seq // CMP_STRIDE - CMP_BLOCK // CMP_STRIDE + 1
    c_lo = np.arange(n_c) * CMP_STRIDE
    s_lo = np.arange(LANES) * SEL_BLOCK
    ov = ((c_lo[None, :] <= s_lo[:, None] + SEL_BLOCK - 1) & (c_lo[None, :] + CMP_BLOCK - 1 >= s_lo[:, None])
          & (np.arange(n_c) < n_cmp)[None, :] & (np.arange(LANES) < n_slc)[:, None])
    ex = (np.arange(seq)[:, None] // SEL_BLOCK == np.arange(LANES)[None, :]).reshape(n_tiles, tq, LANES)
    as_bf = lambda a: jnp.asarray(a.astype(np.float32), BF16)
    return fs, fc, as_bf(ov), as_bf(ex), n_slc


def _nsa_prompt_t(q, gates_t, kcv, slcg, wing, q_gain, tables, *, batch, seq):
    fs, fc, ov, ex, n_slc = tables
    n_c = kcv.shape[2]
    tq = NSA_TQ
    n_tiles = seq // tq
    assert n_tiles > NSA_WINDOW // tq
    gw = NSA_GROUP * HEAD_DIM
    width = NSA_GROUP * tq
    per_g = lambda shape: pl.BlockSpec((1,) + shape, lambda g, b: (g,) + (0,) * len(shape),
                                       pipeline_mode=pl.Buffered(1))
    eye = jnp.eye(HEAD_DIM, dtype=BF16)
    return pl.pallas_call(
        functools.partial(_nsa_prompt_t_body, seq=seq, n_slc=n_slc, k_sel=min(SEL_TOPK, n_slc)),
        grid=(NSA_KV, batch),
        in_specs=[pl.BlockSpec((seq, gw), lambda g, b: (b, g)),
                  pl.BlockSpec((1, n_tiles, 16, tq), lambda g, b: (g, b, 0, 0)),
                  pl.BlockSpec((1, 1, n_c, LANES), lambda g, b: (b, g, 0, 0)),
                  pl.BlockSpec((1, seq, LANES), lambda g, b: (g, b, 0)),
                  pl.BlockSpec((1, seq, LANES), lambda g, b: (g, b, 0)),
                  _const((1, gw)), _const(_block_diag_ones(gw).shape), _const(eye.shape),
                  per_g(fs.shape[1:]), per_g(fc.shape[1:]), _const(ov.shape), _const(ex.shape)],
        out_specs=pl.BlockSpec((n_tiles, gw, tq), lambda g, b: (b, g, 0)),
        out_shape=jax.ShapeDtypeStruct((batch * n_tiles, NSA_KV * gw, tq), F32),
        scratch_shapes=[pltpu.VMEM((n_tiles, n_c, width), F32), pltpu.VMEM((n_tiles, tq, width), F32),
                        pltpu.VMEM((1, tq, width), F32), pltpu.VMEM((n_tiles, HEAD_DIM, width), BF16),
                        pltpu.VMEM((n_tiles, tq, HEAD_DIM), BF16), pltpu.VMEM((n_tiles, tq, HEAD_DIM), BF16),
                        pltpu.VMEM((n_tiles, HEAD_DIM, tq), BF16), pltpu.VMEM((n_tiles, HEAD_DIM, tq), BF16),
                        pltpu.VMEM((LANES, tq), BF16), pltpu.VMEM((1, width), F32), pltpu.VMEM((1, width), F32),
                        pltpu.VMEM((HEAD_DIM, width), F32)],
        compiler_params=_cparams("arbitrary", "arbitrary"), name="nsa_prompt",
    )(q, gates_t, kcv, slcg, wing, jnp.tile(q_gain, NSA_GROUP).reshape(1, gw), _block_diag_ones(gw), eye,
      fs, fc, ov, ex)


SMP_PP = 8
NSA_KW = NSA_KV * HEAD_DIM
SEL_LANES = 2 * LANES


def _nsa_sample_body(pt_ref, q_ref, gate_ref, ckv_ref, *refs, n_slc, k_sel, past, tq):
    pages = refs[:SMP_PP]
    (slcn_ref, wst_ref, winn_ref, gq_ref, bd_ref, bc_ref, bs_ref, bsn_ref, bw_ref, bwn_ref, ov_ref, ex_ref, exn_ref,
     o_ref, nwin_ref, qbd_scr, sel_scr, kbuf_scr, vbuf_scr, s_scr, sn_scr, vn_scr, oc_scr, ow_scr,
     pad_scr) = refs[SMP_PP:]
    j = pl.program_id(1)
    hd = HEAD_DIM
    kw = NSA_KW
    rows = NSA_HEADS * tq
    init = (jnp.full((rows, 1), NEG, F32), jnp.zeros((rows, 1), F32), jnp.zeros((rows, kw), F32))

    @pl.when(j == 0)
    def _():
        qn = _seg_rms(q_ref[...], bd_ref[...], gq_ref[...]) * (hd ** -0.5)
        zero = jnp.zeros((tq, hd), F32)
        blocks = []
        for h in range(NSA_HEADS):
            qh = qn[:, hd * h:hd * (h + 1)]
            blocks.append(jnp.concatenate([qh if g == h // NSA_GROUP else zero for g in range(NSA_KV)], axis=1))
        qbd = jnp.concatenate(blocks, axis=0).astype(BF16)
        qbd_scr[...] = qbd
        ckv = ckv_ref[0]
        bias_c = bc_ref[...]
        ok_c = bias_c > NEG_TEST
        s_c = _dot_nt(qbd, ckv[:, :kw].astype(BF16)) + bias_c
        m_c = jnp.max(s_c, axis=-1, keepdims=True)
        e_c = jnp.where(ok_c, jnp.exp(s_c - m_c), 0.0)
        p_c = (e_c / jnp.maximum(jnp.sum(e_c, axis=-1, keepdims=True), 1e-30)).astype(BF16)
        oc_scr[...] = _dot(p_c, ckv[:, kw:].astype(BF16))
        d = _dot(p_c, ov_ref[...])
        gr = NSA_GROUP * tq
        imp = jnp.concatenate(
            [sum(d[g * gr + h * tq:g * gr + (h + 1) * tq] for h in range(NSA_GROUP)) for g in range(NSA_KV)], axis=0)
        pos = past + (lax.broadcasted_iota(jnp.int32, (NSA_KV * tq, 1), 0) & (tq - 1))
        sel = _select_blocks(imp, pos, n_slc, k_sel)
        sel = jnp.concatenate([sel[g * tq:(g + 1) * tq] for g in range(NSA_KV) for _ in range(NSA_GROUP)], axis=0)
        sel = ((1.0 - sel) * NEG).astype(BF16)
        sel_scr[...] = sel
        pad_scr[...] = jnp.zeros(pad_scr.shape, F32)
        pad_scr[0:tq, :] = slcn_ref[...]
        kn = pad_scr[...].astype(BF16)
        sn_scr[...] = _dot_nt(qbd, kn[:, :kw]) + bsn_ref[...] + _dot(sel, exn_ref[...])
        vn_scr[...] = kn[:, kw:]
        pad_scr[0:tq, :] = winn_ref[...]
        nwin_ref[0] = _shifted_buffer(wst_ref[0], pad_scr[...], tq)
        wst = wst_ref[0].astype(BF16)
        wn = pad_scr[...].astype(BF16)
        bias_w = bw_ref[...]
        mw = _softmax_piece(_dot(qbd, wst[:kw]) + bias_w, bias_w > NEG_TEST, *init, wst[kw:], v_t=True)
        bias_wn = bwn_ref[...]
        mw = _softmax_piece(_dot_nt(qbd, wn[:, :kw]) + bias_wn, bias_wn > NEG_TEST, *mw, wn[:, kw:])
        ow_scr[...] = mw[2] / jnp.maximum(mw[1], 1e-30)

    for i in range(SMP_PP):
        pg = pages[i][0].astype(BF16)
        kbuf_scr[:, i * PAGE:(i + 1) * PAGE] = pg[:kw]
        vbuf_scr[j, :, i * PAGE:(i + 1) * PAGE] = pg[kw:]
    s_scr[j] = _dot(qbd_scr[...], kbuf_scr[...]) + bs_ref[j] + _dot(sel_scr[...], ex_ref[j])

    @pl.when(j == pl.num_programs(1) - 1)
    def _():
        s_all = s_scr[...]
        s_new = sn_scr[...]
        m = jnp.maximum(jnp.max(jnp.max(s_all, axis=0), axis=-1, keepdims=True),
                        jnp.max(s_new, axis=-1, keepdims=True))
        p_new = jnp.exp(s_new - m)
        den = jnp.sum(p_new, axis=-1, keepdims=True)
        acc = _dot(p_new.astype(BF16), vn_scr[...])
        for st in range(s_all.shape[0]):
            p = jnp.exp(s_all[st] - m)
            den = den + jnp.sum(p, axis=-1, keepdims=True)
            acc = acc + _dot_nt(p.astype(BF16), vbuf_scr[st])
        o_s = acc / jnp.maximum(den, 1e-30)
        o_c = oc_scr[...]
        o_w = ow_scr[...]
        outs = []
        for h in range(NSA_HEADS):
            g, hg = divmod(h, NSA_GROUP)
            gt = gate_ref[g]
            rs = slice(h * tq, (h + 1) * tq)
            ls = slice(hd * g, hd * (g + 1))
            outs.append(gt[:, 3 * hg:3 * hg + 1] * o_c[rs, ls] + gt[:, 3 * hg + 1:3 * hg + 2] * o_s[rs, ls]
                        + gt[:, 3 * hg + 2:3 * hg + 3] * o_w[rs, ls])
        o_ref[...] = jnp.concatenate(outs, axis=1)


def _nsa_sample_tables(rel_bias, past, tq, n_c, w_state):
    rows = NSA_HEADS * tq
    total = past + tq
    n_cmp = total // CMP_STRIDE - CMP_BLOCK // CMP_STRIDE + 1
    n_slc = -(-total // SEL_BLOCK)
    pos = past + jnp.arange(tq)
    c_hi = jnp.arange(n_c) * CMP_STRIDE + CMP_BLOCK - 1
    dist_c = pos[:, None] - c_hi[None, :]
    bc = _bias_tile(rel_bias, dist_c, (dist_c >= 0) & (jnp.arange(n_c) < n_cmp)[None, :]).reshape(rows, n_c)
    n_steps = past // (SMP_PP * PAGE)
    d_ext = jnp.arange(tq + past - 1) + 1
    bs = _toeplitz(_bias_tile(rel_bias, d_ext, d_ext >= 0), tq, past)
    bs = bs.reshape(rows, n_steps, SMP_PP * PAGE).transpose(1, 0, 2)
    jn = jnp.arange(PAGE)
    dist_n = jnp.arange(tq)[:, None] - jn[None, :]
    new_ok = (dist_n >= 0) & (jn < tq)[None, :]
    bsn = _bias_tile(rel_bias, dist_n, new_ok).reshape(rows, PAGE)
    dist_w = jnp.arange(tq)[:, None] + w_state - jnp.arange(w_state)[None, :]
    bw = _bias_tile(rel_bias, dist_w, (dist_w >= 0) & (dist_w <= NSA_WINDOW)).reshape(rows, w_state)
    bwn = _bias_tile(rel_bias, dist_n, new_ok & (dist_n <= NSA_WINDOW)).reshape(rows, PAGE)
    c_lo = np.arange(n_c) * CMP_STRIDE
    s_lo = np.arange(SEL_LANES) * SEL_BLOCK
    ov = ((c_lo[:, None] <= s_lo[None, :] + SEL_BLOCK - 1) & (c_lo[:, None] + CMP_BLOCK - 1 >= s_lo[None, :])
          & (np.arange(n_c) < n_cmp)[:, None] & (np.arange(SEL_LANES) < n_slc)[None, :])
    blk = np.arange(SEL_LANES)
    ex = (blk[:, None] == (np.arange(past)[None, :] // SEL_BLOCK)).reshape(SEL_LANES, n_steps, SMP_PP * PAGE)
    ex = ex.transpose(1, 0, 2)
    exn = (blk[:, None] == ((past + np.arange(PAGE))[None, :] // SEL_BLOCK)) & (np.arange(PAGE) < tq)[None, :]
    as_bf = lambda a: jnp.asarray(a.astype(np.float32), BF16)
    return bc, bs, bsn, bw, bwn, as_bf(ov), as_bf(ex), as_bf(exn), n_slc


def _nsa_sample(q, gates, crows, slc_pool_t, table, slc_new, win_state_t, win_new, q_gain, tables, *, tq):
    bc, bs, bsn, bw, bwn, ov, ex, exn, n_slc = tables
    bsz, npg = table.shape
    past = npg * PAGE
    n_c = crows.shape[1]
    w_state = win_state_t.shape[2]
    rows = NSA_HEADS * tq
    qw = NSA_HEADS * HEAD_DIM
    page = lambda i: pl.BlockSpec((1, ROWS_NSA, PAGE), lambda b, j, pt: (pt[b, j * SMP_PP + i], 0, 0))
    per_b = lambda shape: pl.BlockSpec((1,) + shape, lambda b, j, pt: (b,) + (0,) * len(shape))
    new_rows = pl.BlockSpec((tq, ROWS_NSA), lambda b, j, pt: (b, 0))
    return pl.pallas_call(
        functools.partial(_nsa_sample_body, n_slc=n_slc, k_sel=min(SEL_TOPK, n_slc), past=past, tq=tq),
        grid_spec=pltpu.PrefetchScalarGridSpec(
            num_scalar_prefetch=1, grid=(bsz, npg // SMP_PP),
            in_specs=[pl.BlockSpec((tq, qw), lambda b, j, pt: (b, 0)),
                      pl.BlockSpec((NSA_KV, tq, LANES), lambda b, j, pt: (0, b, 0)),
                      per_b((n_c, ROWS_NSA))] + [page(i) for i in range(SMP_PP)] + [
                      new_rows, per_b((ROWS_NSA, w_state)), new_rows,
                      _const((1, qw)), _const(_block_diag_ones(qw).shape), _const(bc.shape), _const(bs.shape), _const(bsn.shape),
                      _const(bw.shape), _const(bwn.shape), _const(ov.shape), _const(ex.shape), _const(exn.shape)],
            out_specs=[pl.BlockSpec((tq, qw), lambda b, j, pt: (b, 0)), per_b((ROWS_NSA, w_state))],
            scratch_shapes=[pltpu.VMEM((rows, NSA_KW), BF16), pltpu.VMEM((rows, SEL_LANES), BF16),
                            pltpu.VMEM((NSA_KW, SMP_PP * PAGE), BF16),
                            pltpu.VMEM((npg // SMP_PP, NSA_KW, SMP_PP * PAGE), BF16),
                            pltpu.VMEM((npg // SMP_PP, rows, SMP_PP * PAGE), F32),
                            pltpu.VMEM((rows, PAGE), F32), pltpu.VMEM((PAGE, NSA_KW), BF16),
                            pltpu.VMEM((rows, NSA_KW), F32), pltpu.VMEM((rows, NSA_KW), F32),
                            pltpu.VMEM((PAGE, ROWS_NSA), F32)]),
        out_shape=[jax.ShapeDtypeStruct((bsz * tq, qw), F32), jax.ShapeDtypeStruct(win_state_t.shape, F32)],
        compiler_params=_cparams("parallel", "arbitrary"), name="nsa_sample",
    )(table, q, gates, crows, *([slc_pool_t] * SMP_PP), slc_new, win_state_t, win_new,
      jnp.tile(q_gain, NSA_HEADS).reshape(1, qw), _block_diag_ones(qw), bc, bs, bsn, bw, bwn, ov, ex, exn)


DIL_TQ = 128


def _head_of_lane(shape):
    return lax.broadcasted_iota(jnp.int32, shape, 1) >> 6


def _stack_heads(qt, lane_head):
    return jnp.concatenate([jnp.where(lane_head == h, qt, 0.0) for h in range(DIL_HEADS)], axis=0).astype(BF16)


def _unstack_heads(r, lane_head, t):
    return sum(jnp.where(lane_head == h, r[h * t:(h + 1) * t], 0.0) for h in range(DIL_HEADS))


def _dil_prompt_body(q_ref, kv_ref, bias_ref, o_ref, lse_ref, *, n_sub, dil):
    tq = DIL_TQ
    gw = ODD_GW
    lane_head = _head_of_lane((tq, gw))
    for r in range(dil):
        ql = slice(gw * r, gw * (r + 1))
        kl = slice(2 * gw * r, 2 * gw * r + gw)
        vl = slice(2 * gw * r + gw, 2 * gw * (r + 1))

        def tile(t, carry, ql=ql, kl=kl, vl=vl):
            start = pl.multiple_of(t * tq, tq)
            prev = pl.multiple_of(jnp.maximum(t - 1, 0) * tq, tq)
            cur_rows = pl.ds(start, tq)
            prev_rows = pl.ds(prev, tq)
            kk = jnp.concatenate([kv_ref[prev_rows, kl], kv_ref[cur_rows, kl]], axis=0).astype(BF16)
            vv = jnp.concatenate([kv_ref[prev_rows, vl], kv_ref[cur_rows, vl]], axis=0).astype(BF16)
            bias = bias_ref[jnp.minimum(t, 1)]
            s = _dot_nt(_stack_heads(q_ref[cur_rows, ql], lane_head), kk) + bias
            m = jnp.max(s, axis=-1, keepdims=True)
            e = jnp.where(bias > NEG_TEST, jnp.exp(s - m), 0.0)
            l = jnp.maximum(jnp.sum(e, axis=-1, keepdims=True), 1e-30)
            res = _dot((e / l).astype(BF16), vv)
            lse = jnp.broadcast_to(m + jnp.log(l), (DIL_HEADS * tq, gw))
            o_ref[cur_rows, ql] = _unstack_heads(res, lane_head, tq)
            lse_ref[cur_rows, ql] = _unstack_heads(lse, lane_head, tq)
            return carry

        lax.fori_loop(0, n_sub // tq, tile, 0)


def _merge_groups(outs, lses):
    mx = functools.reduce(jnp.maximum, lses)
    ws = [jnp.exp(l - mx) for l in lses]
    return sum(w * o for w, o in zip(ws, outs)) / sum(ws)


def _dil_prompt_tables(rel_bias):
    tq = DIL_TQ
    m_ext = jnp.arange(3 * tq - 1) - (tq - 1)
    cur_tile = jnp.arange(2 * tq) >= tq
    tiles = []
    for gi, (win, dil) in enumerate(DIL_PATTERN):
        assert win // dil == tq
        t = _toeplitz(_bias_tile(rel_bias, m_ext * dil, (m_ext >= 0) & (m_ext <= tq)), tq, 2 * tq)
        t = t[gi * DIL_HEADS:(gi + 1) * DIL_HEADS].reshape(DIL_HEADS * tq, 2 * tq)
        tiles.append(jnp.stack([jnp.where(cur_tile, t, NEG), t]))
    return tiles


def _dil_prompt(q_views, kv_views, tables, *, batch, seq):
    gw = ODD_GW
    outs, lses = [], []
    for gi, (_, dil) in enumerate(DIL_PATTERN):
        n_sub = seq // dil
        assert n_sub % DIL_TQ == 0
        qv, kvv = q_views[gi], kv_views[gi]
        o_spec = pl.BlockSpec((n_sub, dil * gw), lambda b: (b, 0))
        o, lse = pl.pallas_call(
            functools.partial(_dil_prompt_body, n_sub=n_sub, dil=dil), grid=(batch,),
            in_specs=[o_spec, pl.BlockSpec((n_sub, dil * 2 * gw), lambda b: (b, 0)), _const(tables[gi].shape)],
            out_specs=[o_spec, o_spec],
            out_shape=[jax.ShapeDtypeStruct(qv.shape, F32)] * 2,
            compiler_params=_cparams("parallel"), name=f"dilated_prompt_g{gi}",
        )(qv, kvv, tables[gi])
        outs.append(o)
        lses.append(lse)
    return outs, lses


def _shifted_buffer(st, new_rows_padded, tq):
    ln = st.shape[1]
    shifted = pltpu.roll(st, ln - tq, 1)
    new_t = pltpu.roll(new_rows_padded.T, LANES - tq, 1)
    lane = lax.broadcasted_iota(jnp.int32, (st.shape[0], LANES), 1)
    tail = jnp.where(lane >= LANES - tq, new_t, shifted[:, ln - LANES:])
    return tail if ln == LANES else jnp.concatenate([shifted[:, :ln - LANES], tail], axis=1)


def _dil_sample_body(q_ref, kvn_ref, st0_ref, st1_ref, st2_ref, b0_ref, b1_ref, b2_ref, bn_ref, o_ref, lse_ref,
                     n0_ref, n1_ref, n2_ref, pad_scr, *, tq):
    gw = ODD_GW
    lane_head = _head_of_lane((tq, gw))
    rows = DIL_HEADS * tq
    pad_scr[...] = jnp.zeros(pad_scr.shape, F32)
    for gi, (st_ref, b_ref, new_ref) in enumerate(((st0_ref, b0_ref, n0_ref), (st1_ref, b1_ref, n1_ref),
                                                    (st2_ref, b2_ref, n2_ref))):
        qs = _stack_heads(q_ref[gi], lane_head)
        pad_scr[0:tq, :] = kvn_ref[gi]
        new_ref[0] = _shifted_buffer(st_ref[0], pad_scr[...], tq)
        st = st_ref[0].astype(BF16)
        kn = pad_scr[...].astype(BF16)
        bias = b_ref[...]
        acc = (jnp.full((rows, 1), NEG, F32), jnp.zeros((rows, 1), F32), jnp.zeros((rows, gw), F32))
        acc = _softmax_piece(_dot(qs, st[:gw]) + bias, bias > NEG_TEST, *acc, st[gw:], v_t=True)
        bias_n = bn_ref[gi]
        m, l, a = _softmax_piece(_dot_nt(qs, kn[:, :gw]) + bias_n, bias_n > NEG_TEST, *acc, kn[:, gw:])
        l = jnp.maximum(l, 1e-30)
        o_ref[gi] = _unstack_heads(a / l, lane_head, tq)
        lse_ref[gi] = _unstack_heads(jnp.broadcast_to(m + jnp.log(l), (rows, gw)), lane_head, tq)


def _dil_sample_tables(rel_bias, tq, state_lens):
    rows = DIL_HEADS * tq
    i = jnp.arange(tq)[:, None]
    per_state, per_new = [], []
    jn = jnp.arange(PAGE)[None, :]
    for gi, ((win, dil), ln) in enumerate(zip(DIL_PATTERN, state_lens)):
        hs = slice(gi * DIL_HEADS, (gi + 1) * DIL_HEADS)
        d = i + ln - jnp.arange(ln)[None, :]
        per_state.append(_bias_tile(rel_bias, d, (d % dil == 0) & (d <= win))[hs].reshape(rows, ln))
        dn = i - jn
        per_new.append(_bias_tile(rel_bias, dn, (dn >= 0) & (dn % dil == 0) & (dn <= win) & (jn < tq))[hs]
                       .reshape(rows, PAGE))
    return per_state, jnp.stack(per_new)


def _dil_sample(q3, kv3, states, tables, *, tq):
    per_state, bias_new = tables
    bsz = states[0].shape[0]
    gw = ODD_GW
    st_spec = lambda s: pl.BlockSpec((1,) + s.shape[1:], lambda b: (b, 0, 0))
    grp = lambda w: pl.BlockSpec((3, tq, w), lambda b: (0, b, 0))
    return pl.pallas_call(
        functools.partial(_dil_sample_body, tq=tq), grid=(bsz,),
        in_specs=[grp(gw), grp(2 * gw)] + [st_spec(s) for s in states] + [_const(t.shape) for t in per_state]
        + [_const(bias_new.shape)],
        out_specs=[grp(gw), grp(gw)] + [st_spec(s) for s in states],
        out_shape=[jax.ShapeDtypeStruct((3, bsz * tq, gw), F32)] * 2
        + [jax.ShapeDtypeStruct(s.shape, F32) for s in states],
        scratch_shapes=[pltpu.VMEM((PAGE, 2 * gw), F32)],
        compiler_params=_cparams("parallel"), name="dilated_sample",
    )(q3, kv3, *states, *per_state, bias_new)


def _pad_even_w(w_in):
    return jnp.concatenate([w_in, jnp.zeros((w_in.shape[0], EVEN_PAD - w_in.shape[1]), w_in.dtype)], axis=1).astype(BF16)


def _even_mixer_prompt(h, gain, w_in, w_out, lb, hg_norm, qk_norm, cmp_pe, cmp_w1, cmp_w2, rel_bias, *, batch, seq, tm):
    n = batch * seq
    hg, q, cmp_rows, slcg, wing, gates_t, cmp_t, slc_t, win_t = _inproj_even(
        h, gain, _pad_even_w(w_in), qk_norm, tm=tm, seq=seq)
    s0 = jnp.zeros((batch, HG_HEADS, HEAD_DIM, HEAD_DIM), F32)
    o_h, s_fin = _hgrn(hg, lb, hg_norm, s0, batch=batch, seq=seq)
    cw = _compress_weights(cmp_pe, cmp_w1, cmp_w2, qk_norm[1])
    npg = seq // PAGE
    table = jnp.arange(batch * npg, dtype=jnp.int32).reshape(batch, npg)
    _, kcv = _compress(cmp_rows.reshape(n // PAGE, PAGE, ROWS_NSA), table, cw, transposed=False)
    tables = _nsa_prompt_t_tables(rel_bias, seq, kcv.shape[2])
    o_nt = _nsa_prompt_t(q, gates_t, kcv, slcg, wing, qk_norm[0], tables, batch=batch, seq=seq)
    h = _outproj2t(h, o_h, o_nt, w_out.astype(BF16), tm=tm)
    keep = min(NSA_WINDOW, seq)
    return h, {"hgrn": s_fin, "cmp": _rows_from_t(cmp_t, NSA_KV), "slc": _rows_from_t(slc_t, NSA_KV),
               "win": _rows_from_t(win_t[:, :, seq - keep:], NSA_KV)}


def _rows_from_t(a, heads):
    return jnp.moveaxis(a.reshape(a.shape[0], 2, heads, HEAD_DIM, a.shape[2]), -1, 1)


def _even_mixer_sample(h, gain, w_in, w_out, lb, hg_norm, qk_norm, cmp_pe, cmp_w1, cmp_w2, rel_bias,
                       state_hgrn, cmp_pool, slc_pool, win_state, page_table, *, batch, tq, tm):
    assert tq < CMP_STRIDE and tq & (tq - 1) == 0
    hg, q, cmp_rows, slc_rows, win_rows, gates = _inproj_even(h, gain, _pad_even_w(w_in), qk_norm, tm=tm)
    o_h, s_fin = _hgrn(hg, lb, hg_norm, state_hgrn, batch=batch, seq=tq)
    cw = _compress_weights(cmp_pe, cmp_w1, cmp_w2, qk_norm[1])
    rows_t = lambda a: jnp.moveaxis(a, 1, -1).reshape(a.shape[0], ROWS_NSA, a.shape[1])
    crows, _ = _compress(rows_t(cmp_pool), page_table, cw, transposed=True)
    past = page_table.shape[1] * PAGE
    w_state = win_state.shape[1]
    tables = _nsa_sample_tables(rel_bias, past, tq, crows.shape[1], w_state)
    o_n, new_win_t = _nsa_sample(q, gates, crows, rows_t(slc_pool), page_table, slc_rows, rows_t(win_state), win_rows,
                                 qk_norm[0], tables, tq=tq)
    h = _outproj2(h, o_h, o_n, w_out.astype(BF16), tm=tm)
    new_win = jnp.moveaxis(new_win_t.reshape((batch,) + win_state.shape[2:] + (w_state,)), -1, 1)
    return h, {"hgrn": s_fin, "cmp": cmp_rows, "slc": slc_rows, "win": new_win}


def _odd_mixer_prompt(h, gain, w_in, w_out, qk_norm, rel_bias, *, batch, seq, tm):
    qv0, kvv0, qv1, kvv1, qv2, kvv2, *kv_t = _inproj_odd(h, gain, w_in.astype(BF16), qk_norm, tm=tm, seq=seq)
    outs, lses = _dil_prompt([qv0, qv1, qv2], [kvv0, kvv1, kvv2], _dil_prompt_tables(rel_bias),
                             batch=batch, seq=seq)
    bufs = [_rows_from_t(t[:, :, seq - min(w, seq):], DIL_HEADS) for t, (w, _) in zip(kv_t, DIL_PATTERN)]
    return _outproj_merge(h, outs, lses, w_out.astype(BF16), tm=tm, views=True), bufs


def _odd_mixer_sample(h, gain, w_in, w_out, qk_norm, rel_bias, states, *, batch, tq, tm):
    q3, kv3 = _inproj_odd(h, gain, w_in.astype(BF16), qk_norm, tm=tm)
    tables = _dil_sample_tables(rel_bias, tq, [s.shape[1] for s in states])
    states_t = [jnp.moveaxis(s, 1, -1).reshape(batch, 2 * ODD_GW, s.shape[1]) for s in states]
    o3, lse3, *new_t = _dil_sample(q3, kv3, states_t, tables, tq=tq)
    new = [jnp.moveaxis(n.reshape(s.shape[:1] + s.shape[2:] + s.shape[1:2]), -1, 1) for n, s in zip(new_t, states)]
    h = _outproj_merge(h, [o3[g] for g in range(3)], [lse3[g] for g in range(3)], w_out.astype(BF16), tm=tm)
    return h, new


TM_PROMPT = 512
TM_SAMPLE = 256


def kernel(x_prompt, x_sample, p_prompt, p_sample, state_hgrn, cache_nsa_cmp_kv, cache_nsa_slc_kv, state_nsa_win_kv,
           state_dil_kv_0, state_dil_kv_1, state_dil_kv_2, page_table, rel_bias, norm_ffn1, norm_mix, norm_ffn2,
           norm_ple, w_ffn1_in, w_ffn1_out, w_ffn2_in, w_ffn2_out, w_ple_gate, w_ple_proj, w_in_even, w_out_even,
           hgrn_lb_logits, hgrn_norm, nsa_qk_norm, nsa_cmp_pe, nsa_cmp_w1, nsa_cmp_w2, w_in_odd, w_out_odd,
           dil_qk_norm):
    depth = norm_ffn1.shape[0]
    dil_states = (state_dil_kv_0, state_dil_kv_1, state_dil_kv_2)
    bf = lambda w: w.astype(BF16)
    w1i, w1o, w2i, w2o, wpg, wpp = (bf(w) for w in (w_ffn1_in, w_ffn1_out, w_ffn2_in, w_ffn2_out, w_ple_gate,
                                                     w_ple_proj))
    lb_all = jnp.cumsum(jax.nn.softmax(hgrn_lb_logits.astype(F32), axis=0), axis=0)
    g_ffn1, g_ffn2, g_ple = (g.reshape(depth, 1, -1) for g in (norm_ffn1, norm_ffn2, norm_ple))

    def run(x, p, sample):
        batch, seq, d = x.shape
        n = batch * seq
        tm = TM_SAMPLE if sample else TM_PROMPT
        h = x.reshape(n, d)
        p = p.reshape(depth, n, p.shape[-1])
        even, odd = [], []
        for li in range(depth):
            h = _ffn(h, li, g_ffn1, w1i, w1o, tm=tm)
            if li % 2 == 0:
                ei = li // 2
                args = (h, norm_mix[li], w_in_even[ei], w_out_even[ei], lb_all[ei], hgrn_norm[ei], nsa_qk_norm[ei],
                        nsa_cmp_pe[ei], nsa_cmp_w1[ei], nsa_cmp_w2[ei], rel_bias)
                if sample:
                    h, st = _even_mixer_sample(*args, state_hgrn[ei], cache_nsa_cmp_kv[ei], cache_nsa_slc_kv[ei],
                                               state_nsa_win_kv[ei], page_table, batch=batch, tq=seq, tm=tm)
                else:
                    h, st = _even_mixer_prompt(*args, batch=batch, seq=seq, tm=tm)
                even.append(st)
            else:
                oi = li // 2
                args = (h, norm_mix[li], w_in_odd[oi], w_out_odd[oi], dil_qk_norm[oi], rel_bias)
                if sample:
                    h, bufs = _odd_mixer_sample(*args, [s[oi] for s in dil_states], batch=batch, tq=seq, tm=tm)
                else:
                    h, bufs = _odd_mixer_prompt(*args, batch=batch, seq=seq, tm=tm)
                odd.append(bufs)
            h = _ffn(h, li, g_ffn2, w2i, w2o, ple=(p, g_ple, wpg, wpp), tm=tm)
        rows = lambda a: a.reshape(batch, -1, 2, NSA_KV, HEAD_DIM)
        drows = lambda a: a.reshape(batch, -1, 2, DIL_HEADS, HEAD_DIM)
        return (h.reshape(batch, seq, d),
                jnp.stack([s["hgrn"] for s in even]), jnp.stack([rows(s["cmp"]) for s in even]),
                jnp.stack([rows(s["slc"]) for s in even]), jnp.stack([rows(s["win"]) for s in even]),
                jnp.stack([drows(b[0]) for b in odd]), jnp.stack([drows(b[1]) for b in odd]),
                jnp.stack([drows(b[2]) for b in odd]))

    y_p, hg_p, cmp_p, slc_p, win_p, d0_p, d1_p, d2_p = run(x_prompt, p_prompt, False)
    y_s, hg_s, cmp_s, slc_s, win_s, d0_s, d1_s, d2_s = run(x_sample, p_sample, True)
    return (y_p, y_s, hg_p, hg_s, cmp_p, cmp_s, slc_p, slc_s, win_p, win_s, d0_p, d0_s, d1_p, d1_s, d2_p, d2_s)
```

```python
import functools
import math

import numpy as np
import jax
import jax.numpy as jnp
from jax import lax
from jax.experimental import pallas as pl
from jax.experimental.pallas import tpu as pltpu

F32 = jnp.float32
BF16 = jnp.bfloat16
HIGHEST = lax.Precision.HIGHEST

V7X_VMEM_BYTES = 64 * 1024 * 1024
VMEM_LIMIT = V7X_VMEM_BYTES * 7 // 8
LANES = 128

EPS = 1e-6
NEG = -1e30
NEG_TEST = -1e29

HEAD_DIM = 64
HG_HEADS = 8
HG_CHUNK = 128
NSA_HEADS = 12
NSA_KV = 3
NSA_GROUP = NSA_HEADS // NSA_KV
CMP_BLOCK = 32
CMP_STRIDE = 16
SEL_BLOCK = 64
SEL_TOPK = 16
NSA_WINDOW = 512
FORCE_SCORE = 1e4
PAGE = 128
DIL_PATTERN = ((128, 1), (512, 4), (2048, 16))
DIL_HEADS = 4
RP_BUCKETS = 32
RP_MAX_DIST = 2048


def _cparams(*sem):
    return pltpu.CompilerParams(dimension_semantics=sem, vmem_limit_bytes=VMEM_LIMIT)


def _const(shape):
    nd = len(shape)
    return pl.BlockSpec(shape, lambda *_: (0,) * nd, pipeline_mode=pl.Buffered(1))


def _dot(a, b):
    return jnp.dot(a, b, preferred_element_type=F32)


def _dot_nt(a, b):
    return lax.dot_general(a, b, (((1,), (1,)), ((), ())), preferred_element_type=F32)


def _dot_tn(a, b):
    return lax.dot_general(a, b, (((0,), (0,)), ((), ())), preferred_element_type=F32)


def _rms(x, gain):
    return x * lax.rsqrt(jnp.mean(x * x, axis=-1, keepdims=True) + EPS) * gain


def _seg_ms(x, bd):
    sq = x * x
    hi = sq.astype(BF16)
    lo = (sq - hi.astype(F32)).astype(BF16)
    bw = bd.shape[0]
    cols = []
    for c0 in range(0, x.shape[1], bw):
        wd = min(bw, x.shape[1] - c0)
        blk = bd[:wd, :wd]
        cols.append(_dot(hi[:, c0:c0 + wd], blk) + _dot(lo[:, c0:c0 + wd], blk))
    return (cols[0] if len(cols) == 1 else jnp.concatenate(cols, axis=1)) * (1.0 / HEAD_DIM)


def _seg_rms(x, bd, gain):
    return x * lax.rsqrt(_seg_ms(x, bd) + EPS) * gain


def _norm_k_rows(x, bd, gain, kmask):
    bw = bd.shape[0]
    r = lax.rsqrt(_seg_ms(x[:, :bw], bd) + EPS)
    r = jnp.concatenate([r, jnp.ones((x.shape[0], x.shape[1] - bw), F32)], axis=1)
    return x * jnp.where(kmask > 0.5, r, 1.0) * gain


V7X_MXU_WIDTH = 256


def _block_diag_ones(width):
    i = np.arange(min(width, V7X_MXU_WIDTH)) // HEAD_DIM
    return jnp.asarray((i[:, None] == i[None, :]).astype(np.float32), BF16)


def _rel_bucket(dist):
    exact = RP_BUCKETS // 2
    d = jnp.maximum(dist, 0)
    log_ratio = jnp.log(jnp.maximum(d, 1).astype(F32) / exact) / math.log(RP_MAX_DIST / exact)
    large = jnp.minimum(exact + (log_ratio * (RP_BUCKETS - exact)).astype(jnp.int32), RP_BUCKETS - 1)
    return jnp.where(d < exact, d, large)


def _bias_tile(rel_bias, dist, valid):
    onehot = (_rel_bucket(dist)[..., None] == jnp.arange(RP_BUCKETS)).astype(F32)
    t = jnp.einsum('...k,kh->h...', onehot, rel_bias.astype(F32), precision=HIGHEST)
    return jnp.where(valid[None], t, NEG)


def _toeplitz(ext, n, m):
    p = n + m - 1
    u = jnp.concatenate([jnp.flip(ext[..., :m], -1), jnp.flip(ext[..., m:], -1)], axis=-1)
    reps = (1,) * (ext.ndim - 1) + (n,)
    t = jnp.tile(u, reps)[..., :n * (p - 1)].reshape(ext.shape[:-1] + (n, p - 1))
    return t[..., :m]


def _softmax_piece(s, mask, m_prev, l_prev, acc_prev, v, v_t=False):
    m_new = jnp.maximum(m_prev, jnp.max(s, axis=-1, keepdims=True))
    alpha = jnp.exp(m_prev - m_new)
    p = jnp.where(mask, jnp.exp(s - m_new), 0.0)
    l_new = alpha * l_prev + jnp.sum(p, axis=-1, keepdims=True)
    pb = p.astype(BF16)
    acc_new = alpha * acc_prev + (_dot_nt(pb, v) if v_t else _dot(pb, v))
    return m_new, l_new, acc_new


def _ffn_body(x_ref, g_ref, win_ref, wout_ref, *rest, dff, ple):
    x = x_ref[...]
    xn = _rms(x, g_ref[...]).astype(BF16)
    gu = _dot(xn, win_ref[...])
    gg = gu[:, :dff]
    a = (gg * jax.nn.sigmoid(gg) * gu[:, dff:]).astype(BF16)
    h = x + 0.5 * _dot(a, wout_ref[...])
    if ple:
        p_ref, gp_ref, wg_ref, wp_ref, o_ref = rest
        hn = _rms(h, gp_ref[...]).astype(BF16)
        gate = jax.nn.sigmoid(_dot(hn, wg_ref[...]))
        h = h + gate * _dot(p_ref[...].astype(BF16), wp_ref[...])
    else:
        (o_ref,) = rest
    o_ref[...] = h


def _layer(arr, li):
    tail = (0,) * (arr.ndim - 1)
    return pl.BlockSpec((None,) + arr.shape[1:], lambda *_: (li,) + tail, pipeline_mode=pl.Buffered(1))


def _ffn(h, li, gain, w_in, w_out, ple=None, *, tm):
    n, d = h.shape
    dff = w_out.shape[1]
    row = lambda i: (i, 0)
    in_specs = [pl.BlockSpec((tm, d), row), _layer(gain, li), _layer(w_in, li), _layer(w_out, li)]
    args = [h, gain, w_in, w_out]
    if ple is not None:
        p, gp, wg, wp = ple
        in_specs += [pl.BlockSpec((None, tm, p.shape[2]), lambda i: (li, i, 0)), _layer(gp, li), _layer(wg, li),
                     _layer(wp, li)]
        args += [p, gp, wg, wp]
    return pl.pallas_call(
        functools.partial(_ffn_body, dff=dff, ple=ple is not None),
        grid=(n // tm,), in_specs=in_specs, out_specs=pl.BlockSpec((tm, d), row),
        out_shape=jax.ShapeDtypeStruct((n, d), F32), compiler_params=_cparams("parallel"),
        name="ffn_ple" if ple is not None else "ffn",
    )(*args)


EVEN_HG = 4 * HG_HEADS * HEAD_DIM
EVEN_Q = NSA_HEADS * HEAD_DIM
ROWS_NSA = 2 * NSA_KV * HEAD_DIM
EVEN_PAD = EVEN_HG + EVEN_Q + 3 * ROWS_NSA + LANES


GATE_ROWS = 16


def _inproj_even_body(x_ref, g_ref, w_ref, gk_ref, km_ref, bd_ref, hg_ref, q_ref, cmp_ref, *outs, prompt):
    xn = _rms(x_ref[...], g_ref[...]).astype(BF16)
    p = _dot(xn, w_ref[...])
    c0 = EVEN_HG
    c1 = c0 + EVEN_Q
    hg_ref[...] = p[:, :c0]
    q_ref[...] = p[:, c0:c1]
    cmp = p[:, c1:c1 + ROWS_NSA]
    cmp_ref[...] = cmp
    bd = bd_ref[...]
    km = km_ref[...]
    slc = _norm_k_rows(p[:, c1 + ROWS_NSA:c1 + 2 * ROWS_NSA], bd, gk_ref[0:1, :], km)
    win = _norm_k_rows(p[:, c1 + 2 * ROWS_NSA:c1 + 3 * ROWS_NSA], bd, gk_ref[1:2, :], km)
    sig = jax.nn.sigmoid(p[:, c1 + 3 * ROWS_NSA:])
    if not prompt:
        slc_ref, win_ref, gate_ref = outs
        slc_ref[...] = slc
        win_ref[...] = win
        for g in range(NSA_KV):
            gate_ref[g] = sig if g == 0 else pltpu.roll(sig, LANES - 3 * NSA_GROUP * g, 1)
        return
    slcg_ref, wing_ref, gate_t_ref, cmp_t_ref, slc_t_ref, win_t_ref = outs
    kw = NSA_KV * HEAD_DIM
    sig_t = sig.T
    for g in range(NSA_KV):
        ks = slice(HEAD_DIM * g, HEAD_DIM * (g + 1))
        vs = slice(kw + HEAD_DIM * g, kw + HEAD_DIM * (g + 1))
        slcg_ref[g] = jnp.concatenate([slc[:, ks], slc[:, vs]], axis=1)
        wing_ref[g] = jnp.concatenate([win[:, ks], win[:, vs]], axis=1)
        for half in range(sig.shape[0] // NSA_TQ):
            gate_t_ref[g, half] = sig_t[3 * NSA_GROUP * g:3 * NSA_GROUP * g + GATE_ROWS,
                                        half * NSA_TQ:(half + 1) * NSA_TQ]
    for dst_ref, rows in ((cmp_t_ref, cmp), (slc_t_ref, slc), (win_t_ref, win)):
        _store_transposed(dst_ref, rows)


def _store_transposed(dst_ref, x):
    for c in range(x.shape[1] // LANES):
        dst_ref[0, c * LANES:(c + 1) * LANES, :] = x[:, c * LANES:(c + 1) * LANES].T


def _inproj_even(h, gain, w_pad, qk_norm, *, tm, seq=None):
    n, d = h.shape
    kw = NSA_KV * HEAD_DIM
    ones = jnp.ones((kw,), F32)
    gk = jnp.stack([jnp.concatenate([jnp.tile(qk_norm[2], NSA_KV), ones]),
                    jnp.concatenate([jnp.tile(qk_norm[3], NSA_KV), ones])])
    km = jnp.concatenate([ones, 0.0 * ones]).reshape(1, ROWS_NSA)
    row = lambda i: (i, 0)
    grp = lambda i: (0, i, 0)
    shp = lambda *s: jax.ShapeDtypeStruct(s, F32)
    out_specs = [pl.BlockSpec((tm, EVEN_HG), row), pl.BlockSpec((tm, EVEN_Q), row), pl.BlockSpec((tm, ROWS_NSA), row)]
    out_shape = [shp(n, EVEN_HG), shp(n, EVEN_Q), shp(n, ROWS_NSA)]
    if seq is None:
        out_specs += [pl.BlockSpec((tm, ROWS_NSA), row)] * 2 + [pl.BlockSpec((NSA_KV, tm, LANES), grp)]
        out_shape += [shp(n, ROWS_NSA)] * 2 + [shp(NSA_KV, n, LANES)]
    else:
        tpb = seq // tm
        out_specs += [pl.BlockSpec((NSA_KV, tm, LANES), grp)] * 2
        out_shape += [shp(NSA_KV, n, LANES)] * 2
        out_specs.append(pl.BlockSpec((NSA_KV, tm // NSA_TQ, GATE_ROWS, NSA_TQ), lambda i: (0, i, 0, 0)))
        out_shape.append(shp(NSA_KV, n // NSA_TQ, GATE_ROWS, NSA_TQ))
        out_specs += [pl.BlockSpec((1, ROWS_NSA, tm), lambda i: (i // tpb, 0, i % tpb))] * 3
        out_shape += [shp(n // seq, ROWS_NSA, seq)] * 3
    return pl.pallas_call(
        functools.partial(_inproj_even_body, prompt=seq is not None), grid=(n // tm,),
        in_specs=[pl.BlockSpec((tm, d), row), _const((1, d)), _const(w_pad.shape), _const((2, ROWS_NSA)),
                  _const((1, ROWS_NSA)), _const(_block_diag_ones(ROWS_NSA).shape)],
        out_specs=out_specs, out_shape=out_shape,
        compiler_params=_cparams("parallel"), name="inproj_even",
    )(h, gain.reshape(1, d), w_pad, gk, km, _block_diag_ones(ROWS_NSA))


ODD_W = 3 * DIL_HEADS * HEAD_DIM
ODD_GW = DIL_HEADS * HEAD_DIM


def _to_residue_view(src, dst_ref, stage_ref, dil):
    rows, w = src.shape
    for blk in range(w // LANES):
        stage_ref[...] = src[:, blk * LANES:(blk + 1) * LANES]
        for r in range(dil):
            lo = r * w + blk * LANES
            dst_ref[:, lo:lo + LANES] = stage_ref[pl.ds(r, rows // dil, stride=dil), :]


def _from_residue_view(src_ref, stage_ref, dil, w):
    rows = src_ref.shape[0] * dil
    parts = []
    for blk in range(w // LANES):
        for r in range(dil):
            lo = r * w + blk * LANES
            stage_ref[pl.ds(r, rows // dil, stride=dil), :] = src_ref[:, lo:lo + LANES]
        parts.append(stage_ref[...])
    return jnp.concatenate(parts, axis=1)


def _inproj_odd_body(x_ref, g_ref, w_ref, gq_ref, gk_ref, bd_ref, *outs, prompt):
    xn = _rms(x_ref[...], g_ref[...]).astype(BF16)
    p = _dot(xn, w_ref[...])
    bd = bd_ref[...]
    q = _seg_rms(p[:, :ODD_W], bd, gq_ref[...]) * (HEAD_DIM ** -0.5)
    k = _seg_rms(p[:, ODD_W:2 * ODD_W], bd, gk_ref[...])
    v = p[:, 2 * ODD_W:]
    for g in range(3):
        s = slice(ODD_GW * g, ODD_GW * (g + 1))
        kv = jnp.concatenate([k[:, s], v[:, s]], axis=1)
        if not prompt:
            outs[0][g] = q[:, s]
            outs[1][g] = kv
            continue
        dil = DIL_PATTERN[g][1]
        if dil == 1:
            outs[2 * g][...] = q[:, s]
            outs[2 * g + 1][...] = kv
        else:
            _to_residue_view(q[:, s], outs[2 * g], outs[-1], dil)
            _to_residue_view(kv, outs[2 * g + 1], outs[-1], dil)
        _store_transposed(outs[6 + g], kv)


def _inproj_odd(h, gain, w, qk_norm, *, tm, seq=None):
    n, d = h.shape
    row = lambda i: (i, 0)
    nh = 3 * DIL_HEADS
    prompt = seq is not None
    if not prompt:
        out_specs = [pl.BlockSpec((3, tm, ODD_GW), lambda i: (0, i, 0)),
                     pl.BlockSpec((3, tm, 2 * ODD_GW), lambda i: (0, i, 0))]
        out_shape = [jax.ShapeDtypeStruct((3, n, ODD_GW), F32), jax.ShapeDtypeStruct((3, n, 2 * ODD_GW), F32)]
    else:
        out_specs, out_shape = [], []
        for _, dil in DIL_PATTERN:
            assert dil == 1 or DIL_PATTERN[0][1] == 1
            for width in (ODD_GW, 2 * ODD_GW):
                out_specs.append(pl.BlockSpec((tm // dil, dil * width), row))
                out_shape.append(jax.ShapeDtypeStruct((n // dil, dil * width), F32))
        tpb = seq // tm
        for _ in DIL_PATTERN:
            out_specs.append(pl.BlockSpec((1, 2 * ODD_GW, tm), lambda i: (i // tpb, 0, i % tpb)))
            out_shape.append(jax.ShapeDtypeStruct((n // seq, 2 * ODD_GW, seq), F32))
    return pl.pallas_call(
        functools.partial(_inproj_odd_body, prompt=prompt), grid=(n // tm,),
        in_specs=[pl.BlockSpec((tm, d), row), _const((1, d)), _const(w.shape), _const((1, ODD_W)),
                  _const((1, ODD_W)), _const(_block_diag_ones(ODD_W).shape)],
        out_specs=out_specs, out_shape=out_shape,
        scratch_shapes=[pltpu.VMEM((tm, LANES), F32)] if prompt else [],
        compiler_params=_cparams("parallel"), name="inproj_odd",
    )(h, gain.reshape(1, d), w, jnp.tile(qk_norm[0], nh).reshape(1, ODD_W),
      jnp.tile(qk_norm[1], nh).reshape(1, ODD_W), _block_diag_ones(ODD_W))


def _outproj2_body(h_ref, a_ref, b_ref, w_ref, o_ref):
    ka = a_ref.shape[1]
    o_ref[...] = (h_ref[...] + _dot(a_ref[...].astype(BF16), w_ref[:ka])
                  + _dot(b_ref[...].astype(BF16), w_ref[ka:]))


def _outproj2(h, a, b, w, *, tm):
    n, d = h.shape
    row = lambda i: (i, 0)
    return pl.pallas_call(
        _outproj2_body, grid=(n // tm,),
        in_specs=[pl.BlockSpec((tm, d), row), pl.BlockSpec((tm, a.shape[1]), row),
                  pl.BlockSpec((tm, b.shape[1]), row), _const(w.shape)],
        out_specs=pl.BlockSpec((tm, d), row), out_shape=jax.ShapeDtypeStruct((n, d), F32),
        compiler_params=_cparams("parallel"), name="outproj_even",
    )(h, a, b, w)


def _outproj2t_body(h_ref, a_ref, bt_ref, w_ref, o_ref):
    ka = a_ref.shape[1]
    acc = h_ref[...] + _dot(a_ref[...].astype(BF16), w_ref[:ka])
    tq = bt_ref.shape[2]
    for i in range(bt_ref.shape[0]):
        rs = slice(i * tq, (i + 1) * tq)
        o_ref[rs, :] = acc[rs, :] + _dot_tn(bt_ref[i].astype(BF16), w_ref[ka:])


def _outproj2t(h, a, bt, w, *, tm):
    n, d = h.shape
    row = lambda i: (i, 0)
    tq = bt.shape[2]
    return pl.pallas_call(
        _outproj2t_body, grid=(n // tm,),
        in_specs=[pl.BlockSpec((tm, d), row), pl.BlockSpec((tm, a.shape[1]), row),
                  pl.BlockSpec((tm // tq, bt.shape[1], tq), lambda i: (i, 0, 0)), _const(w.shape)],
        out_specs=pl.BlockSpec((tm, d), row), out_shape=jax.ShapeDtypeStruct((n, d), F32),
        compiler_params=_cparams("parallel"), name="outproj_even_t",
    )(h, a, bt, w)


def _outproj_merge_body(h_ref, o0_ref, o1_ref, o2_ref, l0_ref, l1_ref, l2_ref, w_ref, o_ref, *stage, views):
    if views:
        load = lambda ref, g: (ref[...] if DIL_PATTERN[g][1] == 1
                               else _from_residue_view(ref, stage[0], DIL_PATTERN[g][1], ODD_GW))
    else:
        load = lambda ref, g: ref[...]
    mixed = _merge_groups([load(r, g) for g, r in enumerate((o0_ref, o1_ref, o2_ref))],
                          [load(r, g) for g, r in enumerate((l0_ref, l1_ref, l2_ref))])
    o_ref[...] = h_ref[...] + _dot(mixed.astype(BF16), w_ref[...])


def _outproj_merge(h, outs, lses, w, *, tm, views=False):
    n, d = h.shape
    row = lambda i: (i, 0)
    parts = [pl.BlockSpec((tm * a.shape[0] // n, a.shape[1]), row) for a in list(outs) + list(lses)]
    return pl.pallas_call(
        functools.partial(_outproj_merge_body, views=views), grid=(n // tm,),
        in_specs=[pl.BlockSpec((tm, d), row)] + parts + [_const(w.shape)],
        out_specs=pl.BlockSpec((tm, d), row), out_shape=jax.ShapeDtypeStruct((n, d), F32),
        scratch_shapes=[pltpu.VMEM((tm, LANES), F32)] if views else [],
        compiler_params=_cparams("parallel"), name="outproj_odd",
    )(h, *outs, *lses, w)


HG_MM_LEVELS = 2


def _hgrn_prefix_matrix(C):
    t = np.arange(C)[:, None]
    j = np.arange(C)[None, :]
    mats = [j <= t]
    for lvl in range(1, HG_MM_LEVELS + 1):
        if (1 << lvl) <= C:
            mats.append(j <= ((t >> lvl) << lvl) + (1 << lvl) // 2 - 1)
    return jnp.asarray(np.concatenate(mats, axis=0).astype(np.float32), BF16)


def _hgrn_body(hq_ref, hf_ref, hi_ref, hgt_ref, lb_ref, gn_ref, bd_ref, pm_ref, s0_ref, o_ref, sfin_ref, st_ref,
               *b_scr, C):
    c = pl.program_id(1)

    @pl.when(c == 0)
    def _():
        st_ref[...] = s0_ref[0]

    dk = HEAD_DIM
    w = HG_HEADS * dk
    lb = lb_ref[...]
    f = lb + (1.0 - lb) * jax.nn.sigmoid(hf_ref[...])
    lf = jnp.log(f)
    kk = 1.0 - f
    ti = lax.broadcasted_iota(jnp.int32, (C, C), 0)
    si = lax.broadcasted_iota(jnp.int32, (C, C), 1)
    p1 = lf.astype(BF16)
    r1 = lf - p1.astype(F32)
    p2 = r1.astype(BF16)
    p3 = (r1 - p2.astype(F32)).astype(BF16)
    sums = _dot(pm_ref[...], jnp.concatenate([p1, p2, p3], axis=1))
    sums = sums[:, :w] + sums[:, w:2 * w] + sums[:, 2 * w:]
    b = sums[0:C]
    for c4, scr in enumerate(b_scr):
        scr[...] = b[:, c4 * LANES:(c4 + 1) * LANES]
    q = hq_ref[...]
    v = hi_ref[...].astype(BF16)
    heads = range(HG_HEADS)
    hs = [slice(dk * h, dk * (h + 1)) for h in heads]

    qb = q.astype(BF16)
    kb = kk.astype(BF16)
    a = [jnp.where(ti == si, _dot_nt(qb[:, hs[h]], kb[:, hs[h]]), 0.0) for h in heads]
    lvl = 1
    while (1 << lvl) <= C:
        m = 1 << lvl
        half = m // 2
        if lvl <= HG_MM_LEVELS:
            r = sums[lvl * C:(lvl + 1) * C]
        else:
            ref_rows = pl.ds(half - 1, C // m, stride=m)
            r = jnp.concatenate([jnp.broadcast_to(scr[ref_rows, :][:, None, :], (C // m, m, LANES)).reshape(C, LANES)
                                 for scr in b_scr], axis=1)
        e = jnp.exp(-jnp.abs(b - r))
        qt = (q * e).astype(BF16)
        kt = (kk * e).astype(BF16)
        pair = ((ti >> lvl) == (si >> lvl)) & ((ti & (m - 1)) >= half) & ((si & (m - 1)) < half)
        a = [a[h] + jnp.where(pair, _dot_nt(qt[:, hs[h]], kt[:, hs[h]]), 0.0) for h in heads]
        lvl += 1

    b_last = b[C - 1:C, :]
    qe = (q * jnp.exp(b)).astype(BF16)
    kdec = (kk * jnp.exp(b_last - b)).astype(BF16)
    e_last = jnp.exp(b_last)
    outs = []
    for h in heads:
        st = st_ref[h]
        outs.append(_dot(a[h].astype(BF16), v[:, hs[h]]) + _dot_nt(qe[:, hs[h]], st.astype(BF16)))
        st_ref[h] = st * e_last[:, hs[h]] + _dot_tn(v[:, hs[h]], kdec[:, hs[h]])
    o = jnp.concatenate(outs, axis=1)
    gt = hgt_ref[...]
    o_ref[...] = _seg_rms(o, bd_ref[...], gn_ref[...]) * (gt * jax.nn.sigmoid(gt))

    @pl.when(c == pl.num_programs(1) - 1)
    def _():
        sfin_ref[0] = st_ref[...]


def _hgrn(hg, lb, hg_norm, s0, *, batch, seq):
    C = math.gcd(seq, HG_CHUNK)
    nc = seq // C
    w = HG_HEADS * HEAD_DIM
    col = lambda j: pl.BlockSpec((C, w), lambda b, c: (b * nc + c, j))
    st_spec = pl.BlockSpec((1, HG_HEADS, HEAD_DIM, HEAD_DIM), lambda b, c: (b, 0, 0, 0))
    pm = _hgrn_prefix_matrix(C)
    o, sfin = pl.pallas_call(
        functools.partial(_hgrn_body, C=C), grid=(batch, nc),
        in_specs=[col(0), col(1), col(2), col(3), _const((1, w)), _const((1, w)), _const(_block_diag_ones(w).shape), _const(pm.shape),
                  st_spec],
        out_specs=[pl.BlockSpec((C, w), lambda b, c: (b * nc + c, 0)), st_spec],
        out_shape=[jax.ShapeDtypeStruct((batch * seq, w), F32),
                   jax.ShapeDtypeStruct((batch, HG_HEADS, HEAD_DIM, HEAD_DIM), F32)],
        scratch_shapes=[pltpu.VMEM((HG_HEADS, HEAD_DIM, HEAD_DIM), F32)] + [pltpu.VMEM((C, LANES), F32)] * (w // LANES),
        compiler_params=_cparams("parallel", "arbitrary"), name="hgrn_scan",
    )(hg, hg, hg, hg, lb.reshape(1, w), jnp.tile(hg_norm, HG_HEADS).reshape(1, w), _block_diag_ones(w), pm,
      jnp.swapaxes(s0, 2, 3))
    return o, jnp.swapaxes(sfin, 2, 3)


CHUNKS_PER_PAGE = PAGE // CMP_STRIDE
PAGE_LANES = CMP_STRIDE * ROWS_NSA
CMP_HID = 2 * NSA_KV * 2 * HEAD_DIM
CMP_PP = 16


def _compress_body(pt_ref, *refs, n_chunks, transposed):
    pages = refs[:CMP_PP]
    wbig_ref, pe_ref, w2_ref, gk_ref, km_ref, bd_ref, perm_ref, rows_ref, grp_ref, xs_ref = refs[CMP_PP:]
    nb = ROWS_NSA // LANES
    j = pl.program_id(1)
    perm = perm_ref[...]
    for i in range(0, CMP_PP, 2):
        pair = []
        for k in range(2):
            pg = pages[i + k][0]
            if transposed:
                pg = jnp.concatenate([pg[c * LANES:(c + 1) * LANES, :].T for c in range(nb)], axis=1)
            pair.append(_dot(perm, pg.astype(BF16)))
        r0 = pl.multiple_of((j * CMP_PP + i) * CHUNKS_PER_PAGE, 2 * CHUNKS_PER_PAGE)
        for s in range(CMP_STRIDE):
            rs = slice(s * CHUNKS_PER_PAGE, (s + 1) * CHUNKS_PER_PAGE)
            both = jnp.concatenate([pair[0][rs], pair[1][rs]], axis=0).astype(BF16)
            for c in range(nb):
                lo = (c * CMP_STRIDE + s) * LANES
                xs_ref[pl.ds(r0, 2 * CHUNKS_PER_PAGE), lo:lo + LANES] = both[:, c * LANES:(c + 1) * LANES]

    @pl.when(j == pl.num_programs(1) - 1)
    def _():
        blk_w = CMP_STRIDE * LANES
        first = lambda x: jnp.concatenate(
            [_dot(x[:, c * blk_w:(c + 1) * blk_w], wbig_ref[c]) for c in range(nb)], axis=1)
        y = first(xs_ref[...])
        ysh = pltpu.roll(y, n_chunks - 1, 0)
        hid = y + pltpu.roll(ysh, CMP_HID - HEAD_DIM, 1)
        cf = first(pe_ref[...].astype(BF16))
        lane = lax.broadcasted_iota(jnp.int32, (8, CMP_HID), 1)
        cs = jnp.where(((lane >> 6) & 1) == 0, jnp.broadcast_to(cf[0:1], (8, CMP_HID)),
                       jnp.broadcast_to(cf[1:2], (8, CMP_HID)))
        cs = cs + pltpu.roll(cs, CMP_HID - HEAD_DIM, 1)
        hid = hid + cs[0:1]
        act = (hid * jax.nn.sigmoid(hid)).astype(BF16)
        out = _dot(act, w2_ref[...])
        out = _norm_k_rows(out, bd_ref[...], gk_ref[...], km_ref[...])
        rows_ref[0] = out
        kw = NSA_KV * HEAD_DIM
        for g in range(NSA_KV):
            grp_ref[0, g] = jnp.concatenate([out[:, HEAD_DIM * g:HEAD_DIM * (g + 1)],
                                             out[:, kw + HEAD_DIM * g:kw + HEAD_DIM * (g + 1)]], axis=1)


def _compress_weights(cmp_pe, cmp_w1, cmp_w2, k_gain):
    ratio = CMP_BLOCK // CMP_STRIDE
    w1 = cmp_w1.reshape(2, ratio, CMP_STRIDE, HEAD_DIM, HEAD_DIM)
    eye_kv = jnp.eye(2, dtype=F32)
    eye_g = jnp.eye(NSA_KV, dtype=F32)
    nb = ROWS_NSA // LANES
    member_kv = np.arange(2 * NSA_KV) // NSA_KV
    w1m = w1[member_kv].reshape(nb, 2, ratio, CMP_STRIDE, HEAD_DIM, HEAD_DIM)
    wbig = jnp.einsum('chrsde,hH->cshdHre', w1m, eye_kv).reshape(nb, CMP_STRIDE * LANES, 2 * ratio * HEAD_DIM)
    wbig = wbig.astype(BF16)
    pe = cmp_pe.reshape(2, ratio, CMP_STRIDE, HEAD_DIM)
    pe_rows = pe[member_kv].reshape(nb, 2, ratio, CMP_STRIDE, HEAD_DIM).transpose(2, 0, 3, 1, 4)
    pe_rows = pe_rows.reshape(ratio, PAGE_LANES)
    pe_rows = jnp.concatenate([pe_rows, jnp.zeros((8 - ratio, PAGE_LANES), F32)], axis=0)
    sel_r0 = jnp.asarray([1.0, 0.0], F32)
    w2big = jnp.einsum('ked,kK,gG,r->kgreKGd', cmp_w2, eye_kv, eye_g, sel_r0).reshape(CMP_HID, ROWS_NSA).astype(BF16)
    kw = NSA_KV * HEAD_DIM
    gk = jnp.concatenate([jnp.tile(k_gain, NSA_KV), jnp.ones((kw,), F32)]).reshape(1, ROWS_NSA)
    km = jnp.concatenate([jnp.ones((kw,), F32), jnp.zeros((kw,), F32)]).reshape(1, ROWS_NSA)
    return wbig, pe_rows, w2big, gk, km


def _compress(pool, table, cw, *, transposed):
    wbig, pe_rows, w2big, gk, km = cw
    bsz, npg = table.shape
    n_chunks = npg * CHUNKS_PER_PAGE
    page = lambda i: pl.BlockSpec((1,) + pool.shape[1:], lambda b, j, pt: (pt[b, j * CMP_PP + i], 0, 0))
    tok = np.arange(PAGE)
    perm = (tok[None, :] == (tok[:, None] % CHUNKS_PER_PAGE) * CMP_STRIDE + tok[:, None] // CHUNKS_PER_PAGE)
    perm = jnp.asarray(perm.astype(np.float32), BF16)
    return pl.pallas_call(
        functools.partial(_compress_body, n_chunks=n_chunks, transposed=transposed),
        grid_spec=pltpu.PrefetchScalarGridSpec(
            num_scalar_prefetch=1, grid=(bsz, npg // CMP_PP),
            in_specs=[page(i) for i in range(CMP_PP)] + [
                _const(wbig.shape), _const(pe_rows.shape), _const(w2big.shape), _const((1, ROWS_NSA)),
                _const((1, ROWS_NSA)), _const(_block_diag_ones(ROWS_NSA).shape), _const(perm.shape)],
            out_specs=[pl.BlockSpec((1, n_chunks, ROWS_NSA), lambda b, j, pt: (b, 0, 0)),
                       pl.BlockSpec((1, NSA_KV, n_chunks, LANES), lambda b, j, pt: (b, 0, 0, 0))],
            scratch_shapes=[pltpu.VMEM((n_chunks, PAGE_LANES), BF16)]),
        out_shape=[jax.ShapeDtypeStruct((bsz, n_chunks, ROWS_NSA), F32),
                   jax.ShapeDtypeStruct((bsz, NSA_KV, n_chunks, LANES), F32)],
        compiler_params=_cparams("parallel", "arbitrary"), name="nsa_compress",
    )(table, *([pool] * CMP_PP), wbig, pe_rows, w2big, gk, km, _block_diag_ones(ROWS_NSA), perm)


def _select_blocks(imp, pos, n_slc, k_sel):
    lane = lax.broadcasted_iota(jnp.int32, imp.shape, 1)
    cur = pos >> 6
    forced = (lane == 0) | (lane == cur) | (lane == cur - 1)
    score = jnp.where(forced, FORCE_SCORE, jnp.where(lane <= cur, imp, -1.0))
    rank = jnp.zeros(imp.shape, F32)
    for i in range(n_slc):
        ci = score[:, i:i + 1]
        rank = rank + jnp.where(ci > score, 1.0, jnp.where(ci == score, jnp.where(lane > i, 1.0, 0.0), 0.0))
    return jnp.where(lane < n_slc, jnp.where(rank < k_sel, 1.0, 0.0), 0.0)


NSA_TQ = 256
NSA_QC = LANES


def _select_blocks_t(imp, pos, n_slc, k_sel):
    blk = lax.broadcasted_iota(jnp.int32, imp.shape, 0)
    cur = pos >> 6
    forced = (blk == 0) | (blk == cur) | (blk == cur - 1)
    score = jnp.where(forced, FORCE_SCORE, jnp.where(blk <= cur, imp, -1.0))
    rank = jnp.zeros(imp.shape, F32)
    for i in range(n_slc):
        ci = score[i:i + 1, :]
        rank = rank + jnp.where(ci > score, 1.0, jnp.where(ci == score, jnp.where(blk > i, 1.0, 0.0), 0.0))
    return jnp.where(rank < k_sel, 1.0, 0.0)


def _nsa_prompt_t_body(q_ref, gate_ref, kcv_ref, slc_ref, win_ref, gq_ref, bd_ref, eye_ref, fs_ref, fc_ref,
                       ov_ref, ex_ref, o_ref, bc_ref, bs_ref, bw_ref, qt_scr, ks_scr, kw_scr, vs_scr, vw_scr, sel_scr, m_scr, l_scr, acc_scr,
                       *, seq, n_slc, k_sel):
    tq = NSA_TQ
    qc = NSA_QC
    hd = HEAD_DIM
    n_tiles = seq // tq
    width = NSA_GROUP * tq
    n_chunks = width // qc
    eye = eye_ref[...]
    n_c = kcv_ref.shape[2]
    far = NSA_WINDOW // tq

    @pl.when(pl.program_id(1) == 0)
    def _():
        key_i = lax.broadcasted_iota(jnp.int32, (tq, tq), 0)
        qry_i = lax.broadcasted_iota(jnp.int32, (tq, tq), 1)
        blk_i = lax.broadcasted_iota(jnp.int32, (n_c, tq), 0)
        n_cmp = seq // CMP_STRIDE - CMP_BLOCK // CMP_STRIDE + 1
        span = tq + CMP_STRIDE * n_c
        for h in range(NSA_GROUP):
            hs = slice(h * tq, (h + 1) * tq)
            for d in range(n_tiles):
                u = jnp.broadcast_to(fs_ref[0, h:h + 1, d * tq:(d + 2) * tq], (tq, 2 * tq))
                tile = pltpu.roll(u, tq, 1, stride=1, stride_axis=0)[:, :tq]
                bs_ref[d, :, hs] = tile
                if d == far:
                    bw_ref[0, :, hs] = jnp.where(qry_i <= key_i, tile, NEG)
            for t in range(n_tiles):
                u = jnp.broadcast_to(fc_ref[0, h:h + 1, t * tq:t * tq + span], (n_c, span))
                tile = pltpu.roll(u, tq, 1, stride=CMP_STRIDE, stride_axis=0)[:, :tq]
                bc_ref[t, :, hs] = jnp.where(blk_i < n_cmp, tile, NEG)

    qn = (_seg_rms(q_ref[...], bd_ref[...], gq_ref[...]) * (hd ** -0.5)).astype(BF16)
    for t in range(n_tiles):
        rs = slice(t * tq, (t + 1) * tq)
        for h in range(NSA_GROUP):
            qt_scr[t, :, h * tq:(h + 1) * tq] = _dot_nt(eye, qn[rs, h * hd:(h + 1) * hd]).astype(BF16)
        for kv_ref, k_scr, v_scr in ((slc_ref, ks_scr, vs_scr), (win_ref, kw_scr, vw_scr)):
            kv = kv_ref[0, rs, :].astype(BF16)
            k_scr[t] = kv[:, :hd]
            v_scr[t] = _dot_nt(eye, kv[:, hd:]).astype(BF16)
    kc = kcv_ref[0, 0, :, :hd].astype(BF16)
    vct = _dot_nt(eye, kcv_ref[0, 0, :, hd:].astype(BF16)).astype(BF16)
    def flash_init():
        m_scr[...] = jnp.full((1, width), NEG, F32)
        l_scr[...] = jnp.zeros((1, width), F32)
        acc_scr[...] = jnp.zeros((hd, width), F32)

    def flash_update(k_scr, v_scr, t, tiles, running):
        s_alls = [_dot(k_scr[jt], qt_scr[t]) for jt, _, _ in tiles]
        if running:
            m_all = m_scr[...]
            l_all = l_scr[...]
        alphas, ls, ms, ps = [], [], [], [[] for _ in tiles]
        for c in range(n_chunks):
            cs = slice(c * qc, (c + 1) * qc)
            qoff = (c * qc) % tq
            ss = []
            for (_, bias_tile, sel_add), s_all in zip(tiles, s_alls):
                s = s_all[:, cs] + bias_tile(cs)
                ss.append(s if sel_add is None else s + sel_add[:, qoff:qoff + qc])
            m_new = functools.reduce(jnp.maximum, [jnp.max(s, axis=0, keepdims=True) for s in ss])
            if running:
                m_new = jnp.maximum(m_all[:, cs], m_new)
                alpha = jnp.exp(m_all[:, cs] - m_new)
                alphas.append(alpha)
            l_new = alpha * l_all[:, cs] if running else 0.0
            for k, s in enumerate(ss):
                p = jnp.exp(s - m_new)
                l_new = l_new + jnp.sum(p, axis=0, keepdims=True)
                ps[k].append(p.astype(BF16))
            ls.append(l_new)
            ms.append(m_new)
        cat = lambda xs: jnp.concatenate(xs, axis=1)
        pv = sum(_dot(v_scr[jt], cat(ps[k])) for k, (jt, _, _) in enumerate(tiles))
        acc_scr[...] = cat(alphas) * acc_scr[...] + pv if running else pv
        m_scr[...] = cat(ms)
        l_scr[...] = cat(ls)

    def flash_result():
        return acc_scr[...] / jnp.maximum(l_scr[...], 1e-30)

    def q_tile(t, carry):
        bias_c = bc_ref[t]
        s_c = _dot(kc, qt_scr[t]) + bias_c
        m_c = jnp.max(s_c, axis=0, keepdims=True)
        e_c = jnp.where(bias_c > NEG_TEST, jnp.exp(s_c - m_c), 0.0)
        p_c = (e_c / jnp.maximum(jnp.sum(e_c, axis=0, keepdims=True), 1e-30)).astype(BF16)
        o_c = _dot(vct, p_c)
        imp = sum(_dot(ov_ref[...], p_c[:, h * tq:(h + 1) * tq]) for h in range(NSA_GROUP))
        pos = t * tq + lax.broadcasted_iota(jnp.int32, (1, tq), 1)
        sel = _select_blocks_t(imp[:n_slc], pos, n_slc, k_sel)
        sel_scr[...] = jnp.zeros(sel_scr.shape, BF16)
        sel_scr[0:n_slc, :] = ((1.0 - sel) * NEG).astype(BF16)
        flash_init()
        slc_tile = lambda jt: (jt, lambda cs: bs_ref[t - jt, :, cs], _dot(ex_ref[jt], sel_scr[...]))

        def slc_pair(i, carry):
            flash_update(ks_scr, vs_scr, t, [slc_tile(2 * i), slc_tile(2 * i + 1)], True)
            return carry

        lax.fori_loop(0, (t + 1) // 2, slc_pair, 0)
        pl.when((t & 1) == 0)(lambda: flash_update(ks_scr, vs_scr, t, [slc_tile(t)], True))
        o_s = flash_result()
        far = NSA_WINDOW // tq
        win_tile = lambda d: (t - d, (lambda cs: bw_ref[0, :, cs]) if d == far else (lambda cs: bs_ref[d, :, cs]),
                              None)
        for n_back in range(far + 1):
            cond = (t == n_back) if n_back < far else (t >= far)
            pl.when(cond)(functools.partial(flash_update, kw_scr, vw_scr, t,
                                            [win_tile(d) for d in range(n_back, -1, -1)], False))
        o_w = flash_result()
        gates = gate_ref[0, t]
        for h in range(NSA_GROUP):
            hs = slice(h * tq, (h + 1) * tq)
            o_ref[t, h * hd:(h + 1) * hd, :] = (gates[3 * h:3 * h + 1] * o_c[:, hs]
                                                + gates[3 * h + 1:3 * h + 2] * o_s[:, hs]
                                                + gates[3 * h + 2:3 * h + 3] * o_w[:, hs])
        return carry

    lax.fori_loop(0, n_tiles, q_tile, 0)


def _nsa_prompt_t_tables(rel_bias, seq, n_c):
    tq = NSA_TQ
    n_tiles = seq // tq
    n_slc = -(-seq // SEL_BLOCK)
    ds = jnp.arange((n_tiles + 1) * tq) - tq
    fs = _bias_tile(rel_bias, ds, ds >= 0).reshape(NSA_KV, NSA_GROUP, -1)
    dc = jnp.arange(n_tiles * tq + CMP_STRIDE * n_c) - (CMP_STRIDE * n_c + CMP_BLOCK - 1)
    fc = _bias_tile(rel_bias, dc, dc >= 0).reshape(NSA_KV, NSA_GROUP, -1)
    n_cmp = seq // CMP_STRIDE - CMP_BLOCK // CMP_STRIDE + 1
    c_lo = np.arange(n_c) * CMP_STRIDE
    s_lo = np.arange(LANES) * SEL_BLOCK
    ov = ((c_lo[None, :] <= s_lo[:, None] + SEL_BLOCK - 1) & (c_lo[None, :] + CMP_BLOCK - 1 >= s_lo[:, None])
          & (np.arange(n_c) < n_cmp)[None, :] & (np.arange(LANES) < n_slc)[:, None])
    ex = (np.arange(seq)[:, None] // SEL_BLOCK == np.arange(LANES)[None, :]).reshape(n_tiles, tq, LANES)
    as_bf = lambda a: jnp.asarray(a.astype(np.float32), BF16)
    return fs, fc, as_bf(ov), as_bf(ex), n_slc


def _nsa_prompt_t(q, gates_t, kcv, slcg, wing, q_gain, tables, *, batch, seq):
    fs, fc, ov, ex, n_slc = tables
    n_c = kcv.shape[2]
    tq = NSA_TQ
    n_tiles = seq // tq
    assert n_tiles > NSA_WINDOW // tq
    gw = NSA_GROUP * HEAD_DIM
    width = NSA_GROUP * tq
    per_g = lambda shape: pl.BlockSpec((1,) + shape, lambda g, b: (g,) + (0,) * len(shape),
                                       pipeline_mode=pl.Buffered(1))
    eye = jnp.eye(HEAD_DIM, dtype=BF16)
    return pl.pallas_call(
        functools.partial(_nsa_prompt_t_body, seq=seq, n_slc=n_slc, k_sel=min(SEL_TOPK, n_slc)),
        grid=(NSA_KV, batch),
        in_specs=[pl.BlockSpec((seq, gw), lambda g, b: (b, g)),
                  pl.BlockSpec((1, n_tiles, 16, tq), lambda g, b: (g, b, 0, 0)),
                  pl.BlockSpec((1, 1, n_c, LANES), lambda g, b: (b, g, 0, 0)),
                  pl.BlockSpec((1, seq, LANES), lambda g, b: (g, b, 0)),
                  pl.BlockSpec((1, seq, LANES), lambda g, b: (g, b, 0)),
                  _const((1, gw)), _const(_block_diag_ones(gw).shape), _const(eye.shape),
                  per_g(fs.shape[1:]), per_g(fc.shape[1:]), _const(ov.shape), _const(ex.shape)],
        out_specs=pl.BlockSpec((n_tiles, gw, tq), lambda g, b: (b, g, 0)),
        out_shape=jax.ShapeDtypeStruct((batch * n_tiles, NSA_KV * gw, tq), F32),
        scratch_shapes=[pltpu.VMEM((n_tiles, n_c, width), F32), pltpu.VMEM((n_tiles, tq, width), F32),
                        pltpu.VMEM((1, tq, width), F32), pltpu.VMEM((n_tiles, HEAD_DIM, width), BF16),
                        pltpu.VMEM((n_tiles, tq, HEAD_DIM), BF16), pltpu.VMEM((n_tiles, tq, HEAD_DIM), BF16),
                        pltpu.VMEM((n_tiles, HEAD_DIM, tq), BF16), pltpu.VMEM((n_tiles, HEAD_DIM, tq), BF16),
                        pltpu.VMEM((LANES, tq), BF16), pltpu.VMEM((1, width), F32), pltpu.VMEM((1, width), F32),
                        pltpu.VMEM((HEAD_DIM, width), F32)],
        compiler_params=_cparams("arbitrary", "arbitrary"), name="nsa_prompt",
    )(q, gates_t, kcv, slcg, wing, jnp.tile(q_gain, NSA_GROUP).reshape(1, gw), _block_diag_ones(gw), eye,
      fs, fc, ov, ex)


SMP_PP = 16
NSA_KW = NSA_KV * HEAD_DIM
SEL_LANES = 2 * LANES


def _nsa_sample_body(pt_ref, q_ref, gate_ref, ckv_ref, *refs, n_slc, k_sel, past, tq):
    pages = refs[:SMP_PP]
    (slcn_ref, wst_ref, winn_ref, gq_ref, bd_ref, bc_ref, bs_ref, bsn_ref, bw_ref, bwn_ref, ov_ref, ex_ref, exn_ref,
     o_ref, nwin_ref, qbd_scr, sel_scr, kbuf_scr, vbuf_scr, s_scr, sn_scr, vn_scr, oc_scr, ow_scr,
     pad_scr) = refs[SMP_PP:]
    j = pl.program_id(1)
    hd = HEAD_DIM
    kw = NSA_KW
    rows = NSA_HEADS * tq
    init = (jnp.full((rows, 1), NEG, F32), jnp.zeros((rows, 1), F32), jnp.zeros((rows, kw), F32))

    @pl.when(j == 0)
    def _():
        qn = _seg_rms(q_ref[...], bd_ref[...], gq_ref[...]) * (hd ** -0.5)
        zero = jnp.zeros((tq, hd), F32)
        blocks = []
        for h in range(NSA_HEADS):
            qh = qn[:, hd * h:hd * (h + 1)]
            blocks.append(jnp.concatenate([qh if g == h // NSA_GROUP else zero for g in range(NSA_KV)], axis=1))
        qbd = jnp.concatenate(blocks, axis=0).astype(BF16)
        qbd_scr[...] = qbd
        ckv = ckv_ref[0]
        bias_c = bc_ref[...]
        ok_c = bias_c > NEG_TEST
        s_c = _dot_nt(qbd, ckv[:, :kw].astype(BF16)) + bias_c
        m_c = jnp.max(s_c, axis=-1, keepdims=True)
        e_c = jnp.where(ok_c, jnp.exp(s_c - m_c), 0.0)
        p_c = (e_c / jnp.maximum(jnp.sum(e_c, axis=-1, keepdims=True), 1e-30)).astype(BF16)
        oc_scr[...] = _dot(p_c, ckv[:, kw:].astype(BF16))
        d = _dot(p_c, ov_ref[...])
        gr = NSA_GROUP * tq
        imp = jnp.concatenate(
            [sum(d[g * gr + h * tq:g * gr + (h + 1) * tq] for h in range(NSA_GROUP)) for g in range(NSA_KV)], axis=0)
        pos = past + (lax.broadcasted_iota(jnp.int32, (NSA_KV * tq, 1), 0) & (tq - 1))
        sel = _select_blocks(imp, pos, n_slc, k_sel)
        sel = jnp.concatenate([sel[g * tq:(g + 1) * tq] for g in range(NSA_KV) for _ in range(NSA_GROUP)], axis=0)
        sel = ((1.0 - sel) * NEG).astype(BF16)
        sel_scr[...] = sel
        pad_scr[...] = jnp.zeros(pad_scr.shape, F32)
        pad_scr[0:tq, :] = slcn_ref[...]
        kn = pad_scr[...].astype(BF16)
        sn_scr[...] = _dot_nt(qbd, kn[:, :kw]) + bsn_ref[...] + _dot(sel, exn_ref[...])
        vn_scr[...] = kn[:, kw:]
        pad_scr[0:tq, :] = winn_ref[...]
        nwin_ref[0] = _shifted_buffer(wst_ref[0], pad_scr[...], tq)
        wst = wst_ref[0].astype(BF16)
        wn = pad_scr[...].astype(BF16)
        bias_w = bw_ref[...]
        mw = _softmax_piece(_dot(qbd, wst[:kw]) + bias_w, bias_w > NEG_TEST, *init, wst[kw:], v_t=True)
        bias_wn = bwn_ref[...]
        mw = _softmax_piece(_dot_nt(qbd, wn[:, :kw]) + bias_wn, bias_wn > NEG_TEST, *mw, wn[:, kw:])
        ow_scr[...] = mw[2] / jnp.maximum(mw[1], 1e-30)

    for i in range(SMP_PP):
        pg = pages[i][0].astype(BF16)
        kbuf_scr[:, i * PAGE:(i + 1) * PAGE] = pg[:kw]
        vbuf_scr[j, :, i * PAGE:(i + 1) * PAGE] = pg[kw:]
    s_scr[j] = _dot(qbd_scr[...], kbuf_scr[...]) + bs_ref[j] + _dot(sel_scr[...], ex_ref[j])

    @pl.when(j == pl.num_programs(1) - 1)
    def _():
        s_all = s_scr[...]
        s_new = sn_scr[...]
        m = jnp.maximum(jnp.max(jnp.max(s_all, axis=0), axis=-1, keepdims=True),
                        jnp.max(s_new, axis=-1, keepdims=True))
        p_new = jnp.exp(s_new - m)
        den = jnp.sum(p_new, axis=-1, keepdims=True)
        acc = _dot(p_new.astype(BF16), vn_scr[...])
        for st in range(s_all.shape[0]):
            p = jnp.exp(s_all[st] - m)
            den = den + jnp.sum(p, axis=-1, keepdims=True)
            acc = acc + _dot_nt(p.astype(BF16), vbuf_scr[st])
        o_s = acc / jnp.maximum(den, 1e-30)
        o_c = oc_scr[...]
        o_w = ow_scr[...]
        outs = []
        for h in range(NSA_HEADS):
            g, hg = divmod(h, NSA_GROUP)
            gt = gate_ref[g]
            rs = slice(h * tq, (h + 1) * tq)
            ls = slice(hd * g, hd * (g + 1))
            outs.append(gt[:, 3 * hg:3 * hg + 1] * o_c[rs, ls] + gt[:, 3 * hg + 1:3 * hg + 2] * o_s[rs, ls]
                        + gt[:, 3 * hg + 2:3 * hg + 3] * o_w[rs, ls])
        o_ref[...] = jnp.concatenate(outs, axis=1)


def _nsa_sample_tables(rel_bias, past, tq, n_c, w_state):
    rows = NSA_HEADS * tq
    total = past + tq
    n_cmp = total // CMP_STRIDE - CMP_BLOCK // CMP_STRIDE + 1
    n_slc = -(-total // SEL_BLOCK)
    pos = past + jnp.arange(tq)
    c_hi = jnp.arange(n_c) * CMP_STRIDE + CMP_BLOCK - 1
    dist_c = pos[:, None] - c_hi[None, :]
    bc = _bias_tile(rel_bias, dist_c, (dist_c >= 0) & (jnp.arange(n_c) < n_cmp)[None, :]).reshape(rows, n_c)
    n_steps = past // (SMP_PP * PAGE)
    d_ext = jnp.arange(tq + past - 1) + 1
    bs = _toeplitz(_bias_tile(rel_bias, d_ext, d_ext >= 0), tq, past)
    bs = bs.reshape(rows, n_steps, SMP_PP * PAGE).transpose(1, 0, 2)
    jn = jnp.arange(PAGE)
    dist_n = jnp.arange(tq)[:, None] - jn[None, :]
    new_ok = (dist_n >= 0) & (jn < tq)[None, :]
    bsn = _bias_tile(rel_bias, dist_n, new_ok).reshape(rows, PAGE)
    dist_w = jnp.arange(tq)[:, None] + w_state - jnp.arange(w_state)[None, :]
    bw = _bias_tile(rel_bias, dist_w, (dist_w >= 0) & (dist_w <= NSA_WINDOW)).reshape(rows, w_state)
    bwn = _bias_tile(rel_bias, dist_n, new_ok & (dist_n <= NSA_WINDOW)).reshape(rows, PAGE)
    c_lo = np.arange(n_c) * CMP_STRIDE
    s_lo = np.arange(SEL_LANES) * SEL_BLOCK
    ov = ((c_lo[:, None] <= s_lo[None, :] + SEL_BLOCK - 1) & (c_lo[:, None] + CMP_BLOCK - 1 >= s_lo[None, :])
          & (np.arange(n_c) < n_cmp)[:, None] & (np.arange(SEL_LANES) < n_slc)[None, :])
    blk = np.arange(SEL_LANES)
    ex = (blk[:, None] == (np.arange(past)[None, :] // SEL_BLOCK)).reshape(SEL_LANES, n_steps, SMP_PP * PAGE)
    ex = ex.transpose(1, 0, 2)
    exn = (blk[:, None] == ((past + np.arange(PAGE))[None, :] // SEL_BLOCK)) & (np.arange(PAGE) < tq)[None, :]
    as_bf = lambda a: jnp.asarray(a.astype(np.float32), BF16)
    return bc, bs, bsn, bw, bwn, as_bf(ov), as_bf(ex), as_bf(exn), n_slc


def _nsa_sample(q, gates, crows, slc_pool_t, table, slc_new, win_state_t, win_new, q_gain, tables, *, tq):
    bc, bs, bsn, bw, bwn, ov, ex, exn, n_slc = tables
    bsz, npg = table.shape
    past = npg * PAGE
    n_c = crows.shape[1]
    w_state = win_state_t.shape[2]
    rows = NSA_HEADS * tq
    qw = NSA_HEADS * HEAD_DIM
    page = lambda i: pl.BlockSpec((1, ROWS_NSA, PAGE), lambda b, j, pt: (pt[b, j * SMP_PP + i], 0, 0))
    per_b = lambda shape: pl.BlockSpec((1,) + shape, lambda b, j, pt: (b,) + (0,) * len(shape))
    new_rows = pl.BlockSpec((tq, ROWS_NSA), lambda b, j, pt: (b, 0))
    return pl.pallas_call(
        functools.partial(_nsa_sample_body, n_slc=n_slc, k_sel=min(SEL_TOPK, n_slc), past=past, tq=tq),
        grid_spec=pltpu.PrefetchScalarGridSpec(
            num_scalar_prefetch=1, grid=(bsz, npg // SMP_PP),
            in_specs=[pl.BlockSpec((tq, qw), lambda b, j, pt: (b, 0)),
                      pl.BlockSpec((NSA_KV, tq, LANES), lambda b, j, pt: (0, b, 0)),
                      per_b((n_c, ROWS_NSA))] + [page(i) for i in range(SMP_PP)] + [
                      new_rows, per_b((ROWS_NSA, w_state)), new_rows,
                      _const((1, qw)), _const(_block_diag_ones(qw).shape), _const(bc.shape), _const(bs.shape), _const(bsn.shape),
                      _const(bw.shape), _const(bwn.shape), _const(ov.shape), _const(ex.shape), _const(exn.shape)],
            out_specs=[pl.BlockSpec((tq, qw), lambda b, j, pt: (b, 0)), per_b((ROWS_NSA, w_state))],
            scratch_shapes=[pltpu.VMEM((rows, NSA_KW), BF16), pltpu.VMEM((rows, SEL_LANES), BF16),
                            pltpu.VMEM((NSA_KW, SMP_PP * PAGE), BF16),
                            pltpu.VMEM((npg // SMP_PP, NSA_KW, SMP_PP * PAGE), BF16),
                            pltpu.VMEM((npg // SMP_PP, rows, SMP_PP * PAGE), F32),
                            pltpu.VMEM((rows, PAGE), F32), pltpu.VMEM((PAGE, NSA_KW), BF16),
                            pltpu.VMEM((rows, NSA_KW), F32), pltpu.VMEM((rows, NSA_KW), F32),
                            pltpu.VMEM((PAGE, ROWS_NSA), F32)]),
        out_shape=[jax.ShapeDtypeStruct((bsz * tq, qw), F32), jax.ShapeDtypeStruct(win_state_t.shape, F32)],
        compiler_params=_cparams("parallel", "arbitrary"), name="nsa_sample",
    )(table, q, gates, crows, *([slc_pool_t] * SMP_PP), slc_new, win_state_t, win_new,
      jnp.tile(q_gain, NSA_HEADS).reshape(1, qw), _block_diag_ones(qw), bc, bs, bsn, bw, bwn, ov, ex, exn)


DIL_TQ = 128


def _head_of_lane(shape):
    return lax.broadcasted_iota(jnp.int32, shape, 1) >> 6


def _stack_heads(qt, lane_head):
    return jnp.concatenate([jnp.where(lane_head == h, qt, 0.0) for h in range(DIL_HEADS)], axis=0).astype(BF16)


def _unstack_heads(r, lane_head, t):
    return sum(jnp.where(lane_head == h, r[h * t:(h + 1) * t], 0.0) for h in range(DIL_HEADS))


def _dil_prompt_body(q_ref, kv_ref, bias_ref, o_ref, lse_ref, *, n_sub, dil):
    tq = DIL_TQ
    gw = ODD_GW
    lane_head = _head_of_lane((tq, gw))
    for r in range(dil):
        ql = slice(gw * r, gw * (r + 1))
        kl = slice(2 * gw * r, 2 * gw * r + gw)
        vl = slice(2 * gw * r + gw, 2 * gw * (r + 1))

        def tile(t, carry, ql=ql, kl=kl, vl=vl):
            start = pl.multiple_of(t * tq, tq)
            prev = pl.multiple_of(jnp.maximum(t - 1, 0) * tq, tq)
            cur_rows = pl.ds(start, tq)
            prev_rows = pl.ds(prev, tq)
            kk = jnp.concatenate([kv_ref[prev_rows, kl], kv_ref[cur_rows, kl]], axis=0).astype(BF16)
            vv = jnp.concatenate([kv_ref[prev_rows, vl], kv_ref[cur_rows, vl]], axis=0).astype(BF16)
            bias = bias_ref[jnp.minimum(t, 1)]
            s = _dot_nt(_stack_heads(q_ref[cur_rows, ql], lane_head), kk) + bias
            m = jnp.max(s, axis=-1, keepdims=True)
            e = jnp.where(bias > NEG_TEST, jnp.exp(s - m), 0.0)
            l = jnp.maximum(jnp.sum(e, axis=-1, keepdims=True), 1e-30)
            res = _dot((e / l).astype(BF16), vv)
            lse = jnp.broadcast_to(m + jnp.log(l), (DIL_HEADS * tq, gw))
            o_ref[cur_rows, ql] = _unstack_heads(res, lane_head, tq)
            lse_ref[cur_rows, ql] = _unstack_heads(lse, lane_head, tq)
            return carry

        lax.fori_loop(0, n_sub // tq, tile, 0)


def _merge_groups(outs, lses):
    mx = functools.reduce(jnp.maximum, lses)
    ws = [jnp.exp(l - mx) for l in lses]
    return sum(w * o for w, o in zip(ws, outs)) / sum(ws)


def _dil_prompt_tables(rel_bias):
    tq = DIL_TQ
    m_ext = jnp.arange(3 * tq - 1) - (tq - 1)
    cur_tile = jnp.arange(2 * tq) >= tq
    tiles = []
    for gi, (win, dil) in enumerate(DIL_PATTERN):
        assert win // dil == tq
        t = _toeplitz(_bias_tile(rel_bias, m_ext * dil, (m_ext >= 0) & (m_ext <= tq)), tq, 2 * tq)
        t = t[gi * DIL_HEADS:(gi + 1) * DIL_HEADS].reshape(DIL_HEADS * tq, 2 * tq)
        tiles.append(jnp.stack([jnp.where(cur_tile, t, NEG), t]))
    return tiles


def _dil_prompt(q_views, kv_views, tables, *, batch, seq):
    gw = ODD_GW
    outs, lses = [], []
    for gi, (_, dil) in enumerate(DIL_PATTERN):
        n_sub = seq // dil
        assert n_sub % DIL_TQ == 0
        qv, kvv = q_views[gi], kv_views[gi]
        o_spec = pl.BlockSpec((n_sub, dil * gw), lambda b: (b, 0))
        o, lse = pl.pallas_call(
            functools.partial(_dil_prompt_body, n_sub=n_sub, dil=dil), grid=(batch,),
            in_specs=[o_spec, pl.BlockSpec((n_sub, dil * 2 * gw), lambda b: (b, 0)), _const(tables[gi].shape)],
            out_specs=[o_spec, o_spec],
            out_shape=[jax.ShapeDtypeStruct(qv.shape, F32)] * 2,
            compiler_params=_cparams("parallel"), name=f"dilated_prompt_g{gi}",
        )(qv, kvv, tables[gi])
        outs.append(o)
        lses.append(lse)
    return outs, lses


def _shifted_buffer(st, new_rows_padded, tq):
    ln = st.shape[1]
    shifted = pltpu.roll(st, ln - tq, 1)
    new_t = pltpu.roll(new_rows_padded.T, LANES - tq, 1)
    lane = lax.broadcasted_iota(jnp.int32, (st.shape[0], LANES), 1)
    tail = jnp.where(lane >= LANES - tq, new_t, shifted[:, ln - LANES:])
    return tail if ln == LANES else jnp.concatenate([shifted[:, :ln - LANES], tail], axis=1)


def _dil_sample_body(q_ref, kvn_ref, st0_ref, st1_ref, st2_ref, b0_ref, b1_ref, b2_ref, bn_ref, o_ref, lse_ref,
                     n0_ref, n1_ref, n2_ref, pad_scr, *, tq):
    gw = ODD_GW
    lane_head = _head_of_lane((tq, gw))
    rows = DIL_HEADS * tq
    pad_scr[...] = jnp.zeros(pad_scr.shape, F32)
    for gi, (st_ref, b_ref, new_ref) in enumerate(((st0_ref, b0_ref, n0_ref), (st1_ref, b1_ref, n1_ref),
                                                    (st2_ref, b2_ref, n2_ref))):
        qs = _stack_heads(q_ref[gi], lane_head)
        pad_scr[0:tq, :] = kvn_ref[gi]
        new_ref[0] = _shifted_buffer(st_ref[0], pad_scr[...], tq)
        st = st_ref[0].astype(BF16)
        kn = pad_scr[...].astype(BF16)
        bias = b_ref[...]
        acc = (jnp.full((rows, 1), NEG, F32), jnp.zeros((rows, 1), F32), jnp.zeros((rows, gw), F32))
        acc = _softmax_piece(_dot(qs, st[:gw]) + bias, bias > NEG_TEST, *acc, st[gw:], v_t=True)
        bias_n = bn_ref[gi]
        m, l, a = _softmax_piece(_dot_nt(qs, kn[:, :gw]) + bias_n, bias_n > NEG_TEST, *acc, kn[:, gw:])
        l = jnp.maximum(l, 1e-30)
        o_ref[gi] = _unstack_heads(a / l, lane_head, tq)
        lse_ref[gi] = _unstack_heads(jnp.broadcast_to(m + jnp.log(l), (rows, gw)), lane_head, tq)


def _dil_sample_tables(rel_bias, tq, state_lens):
    rows = DIL_HEADS * tq
    i = jnp.arange(tq)[:, None]
    per_state, per_new = [], []
    jn = jnp.arange(PAGE)[None, :]
    for gi, ((win, dil), ln) in enumerate(zip(DIL_PATTERN, state_lens)):
        hs = slice(gi * DIL_HEADS, (gi + 1) * DIL_HEADS)
        d = i + ln - jnp.arange(ln)[None, :]
        per_state.append(_bias_tile(rel_bias, d, (d % dil == 0) & (d <= win))[hs].reshape(rows, ln))
        dn = i - jn
        per_new.append(_bias_tile(rel_bias, dn, (dn >= 0) & (dn % dil == 0) & (dn <= win) & (jn < tq))[hs]
                       .reshape(rows, PAGE))
    return per_state, jnp.stack(per_new)


def _dil_sample(q3, kv3, states, tables, *, tq):
    per_state, bias_new = tables
    bsz = states[0].shape[0]
    gw = ODD_GW
    st_spec = lambda s: pl.BlockSpec((1,) + s.shape[1:], lambda b: (b, 0, 0))
    grp = lambda w: pl.BlockSpec((3, tq, w), lambda b: (0, b, 0))
    return pl.pallas_call(
        functools.partial(_dil_sample_body, tq=tq), grid=(bsz,),
        in_specs=[grp(gw), grp(2 * gw)] + [st_spec(s) for s in states] + [_const(t.shape) for t in per_state]
        + [_const(bias_new.shape)],
        out_specs=[grp(gw), grp(gw)] + [st_spec(s) for s in states],
        out_shape=[jax.ShapeDtypeStruct((3, bsz * tq, gw), F32)] * 2
        + [jax.ShapeDtypeStruct(s.shape, F32) for s in states],
        scratch_shapes=[pltpu.VMEM((PAGE, 2 * gw), F32)],
        compiler_params=_cparams("parallel"), name="dilated_sample",
    )(q3, kv3, *states, *per_state, bias_new)


def _pad_even_w(w_in):
    return jnp.concatenate([w_in, jnp.zeros((w_in.shape[0], EVEN_PAD - w_in.shape[1]), w_in.dtype)], axis=1).astype(BF16)


def _even_mixer_prompt(h, gain, w_in, w_out, lb, hg_norm, qk_norm, cmp_pe, cmp_w1, cmp_w2, rel_bias, *, batch, seq, tm):
    n = batch * seq
    hg, q, cmp_rows, slcg, wing, gates_t, cmp_t, slc_t, win_t = _inproj_even(
        h, gain, _pad_even_w(w_in), qk_norm, tm=tm, seq=seq)
    s0 = jnp.zeros((batch, HG_HEADS, HEAD_DIM, HEAD_DIM), F32)
    o_h, s_fin = _hgrn(hg, lb, hg_norm, s0, batch=batch, seq=seq)
    cw = _compress_weights(cmp_pe, cmp_w1, cmp_w2, qk_norm[1])
    npg = seq // PAGE
    table = jnp.arange(batch * npg, dtype=jnp.int32).reshape(batch, npg)
    _, kcv = _compress(cmp_rows.reshape(n // PAGE, PAGE, ROWS_NSA), table, cw, transposed=False)
    tables = _nsa_prompt_t_tables(rel_bias, seq, kcv.shape[2])
    o_nt = _nsa_prompt_t(q, gates_t, kcv, slcg, wing, qk_norm[0], tables, batch=batch, seq=seq)
    h = _outproj2t(h, o_h, o_nt, w_out.astype(BF16), tm=tm)
    keep = min(NSA_WINDOW, seq)
    return h, {"hgrn": s_fin, "cmp": _rows_from_t(cmp_t, NSA_KV), "slc": _rows_from_t(slc_t, NSA_KV),
               "win": _rows_from_t(win_t[:, :, seq - keep:], NSA_KV)}


def _rows_from_t(a, heads):
    return jnp.moveaxis(a.reshape(a.shape[0], 2, heads, HEAD_DIM, a.shape[2]), -1, 1)


def _even_mixer_sample(h, gain, w_in, w_out, lb, hg_norm, qk_norm, cmp_pe, cmp_w1, cmp_w2, rel_bias,
                       state_hgrn, cmp_pool, slc_pool, win_state, page_table, *, batch, tq, tm):
    assert tq < CMP_STRIDE and tq & (tq - 1) == 0
    hg, q, cmp_rows, slc_rows, win_rows, gates = _inproj_even(h, gain, _pad_even_w(w_in), qk_norm, tm=tm)
    o_h, s_fin = _hgrn(hg, lb, hg_norm, state_hgrn, batch=batch, seq=tq)
    cw = _compress_weights(cmp_pe, cmp_w1, cmp_w2, qk_norm[1])
    rows_t = lambda a: jnp.moveaxis(a, 1, -1).reshape(a.shape[0], ROWS_NSA, a.shape[1])
    crows, _ = _compress(rows_t(cmp_pool), page_table, cw, transposed=True)
    past = page_table.shape[1] * PAGE
    w_state = win_state.shape[1]
    tables = _nsa_sample_tables(rel_bias, past, tq, crows.shape[1], w_state)
    o_n, new_win_t = _nsa_sample(q, gates, crows, rows_t(slc_pool), page_table, slc_rows, rows_t(win_state), win_rows,
                                 qk_norm[0], tables, tq=tq)
    h = _outproj2(h, o_h, o_n, w_out.astype(BF16), tm=tm)
    new_win = jnp.moveaxis(new_win_t.reshape((batch,) + win_state.shape[2:] + (w_state,)), -1, 1)
    return h, {"hgrn": s_fin, "cmp": cmp_rows, "slc": slc_rows, "win": new_win}


def _odd_mixer_prompt(h, gain, w_in, w_out, qk_norm, rel_bias, *, batch, seq, tm):
    qv0, kvv0, qv1, kvv1, qv2, kvv2, *kv_t = _inproj_odd(h, gain, w_in.astype(BF16), qk_norm, tm=tm, seq=seq)
    outs, lses = _dil_prompt([qv0, qv1, qv2], [kvv0, kvv1, kvv2], _dil_prompt_tables(rel_bias),
                             batch=batch, seq=seq)
    bufs = [_rows_from_t(t[:, :, seq - min(w, seq):], DIL_HEADS) for t, (w, _) in zip(kv_t, DIL_PATTERN)]
    return _outproj_merge(h, outs, lses, w_out.astype(BF16), tm=tm, views=True), bufs


def _odd_mixer_sample(h, gain, w_in, w_out, qk_norm, rel_bias, states, *, batch, tq, tm):
    q3, kv3 = _inproj_odd(h, gain, w_in.astype(BF16), qk_norm, tm=tm)
    tables = _dil_sample_tables(rel_bias, tq, [s.shape[1] for s in states])
    states_t = [jnp.moveaxis(s, 1, -1).reshape(batch, 2 * ODD_GW, s.shape[1]) for s in states]
    o3, lse3, *new_t = _dil_sample(q3, kv3, states_t, tables, tq=tq)
    new = [jnp.moveaxis(n.reshape(s.shape[:1] + s.shape[2:] + s.shape[1:2]), -1, 1) for n, s in zip(new_t, states)]
    h = _outproj_merge(h, [o3[g] for g in range(3)], [lse3[g] for g in range(3)], w_out.astype(BF16), tm=tm)
    return h, new


TM_PROMPT = 512
TM_SAMPLE = 256


def kernel(x_prompt, x_sample, p_prompt, p_sample, state_hgrn, cache_nsa_cmp_kv, cache_nsa_slc_kv, state_nsa_win_kv,
           state_dil_kv_0, state_dil_kv_1, state_dil_kv_2, page_table, rel_bias, norm_ffn1, norm_mix, norm_ffn2,
           norm_ple, w_ffn1_in, w_ffn1_out, w_ffn2_in, w_ffn2_out, w_ple_gate, w_ple_proj, w_in_even, w_out_even,
           hgrn_lb_logits, hgrn_norm, nsa_qk_norm, nsa_cmp_pe, nsa_cmp_w1, nsa_cmp_w2, w_in_odd, w_out_odd,
           dil_qk_norm):
    depth = norm_ffn1.shape[0]
    dil_states = (state_dil_kv_0, state_dil_kv_1, state_dil_kv_2)
    bf = lambda w: w.astype(BF16)
    w1i, w1o, w2i, w2o, wpg, wpp = (bf(w) for w in (w_ffn1_in, w_ffn1_out, w_ffn2_in, w_ffn2_out, w_ple_gate,
                                                     w_ple_proj))
    lb_all = jnp.cumsum(jax.nn.softmax(hgrn_lb_logits.astype(F32), axis=0), axis=0)
    g_ffn1, g_ffn2, g_ple = (g.reshape(depth, 1, -1) for g in (norm_ffn1, norm_ffn2, norm_ple))

    def run(x, p, sample):
        batch, seq, d = x.shape
        n = batch * seq
        tm = TM_SAMPLE if sample else TM_PROMPT
        h = x.reshape(n, d)
        p = p.reshape(depth, n, p.shape[-1])
        even, odd = [], []
        for li in range(depth):
            h = _ffn(h, li, g_ffn1, w1i, w1o, tm=tm)
            if li % 2 == 0:
                ei = li // 2
                args = (h, norm_mix[li], w_in_even[ei], w_out_even[ei], lb_all[ei], hgrn_norm[ei], nsa_qk_norm[ei],
                        nsa_cmp_pe[ei], nsa_cmp_w1[ei], nsa_cmp_w2[ei], rel_bias)
                if sample:
                    h, st = _even_mixer_sample(*args, state_hgrn[ei], cache_nsa_cmp_kv[ei], cache_nsa_slc_kv[ei],
                                               state_nsa_win_kv[ei], page_table, batch=batch, tq=seq, tm=tm)
                else:
                    h, st = _even_mixer_prompt(*args, batch=batch, seq=seq, tm=tm)
                even.append(st)
            else:
                oi = li // 2
                args = (h, norm_mix[li], w_in_odd[oi], w_out_odd[oi], dil_qk_norm[oi], rel_bias)
                if sample:
                    h, bufs = _odd_mixer_sample(*args, [s[oi] for s in dil_states], batch=batch, tq=seq, tm=tm)
                else:
                    h, bufs = _odd_mixer_prompt(*args, batch=batch, seq=seq, tm=tm)
                odd.append(bufs)
            h = _ffn(h, li, g_ffn2, w2i, w2o, ple=(p, g_ple, wpg, wpp), tm=tm)
        rows = lambda a: a.reshape(batch, -1, 2, NSA_KV, HEAD_DIM)
        drows = lambda a: a.reshape(batch, -1, 2, DIL_HEADS, HEAD_DIM)
        return (h.reshape(batch, seq, d),
                jnp.stack([s["hgrn"] for s in even]), jnp.stack([rows(s["cmp"]) for s in even]),
                jnp.stack([rows(s["slc"]) for s in even]), jnp.stack([rows(s["win"]) for s in even]),
                jnp.stack([drows(b[0]) for b in odd]), jnp.stack([drows(b[1]) for b in odd]),
                jnp.stack([drows(b[2]) for b in odd]))

    y_p, hg_p, cmp_p, slc_p, win_p, d0_p, d1_p, d2_p = run(x_prompt, p_prompt, False)
    y_s, hg_s, cmp_s, slc_s, win_s, d0_s, d1_s, d2_s = run(x_sample, p_sample, True)
    return (y_p, y_s, hg_p, hg_s, cmp_p, cmp_s, slc_p, slc_s, win_p, win_s, d0_p, d0_s, d1_p, d1_s, d2_p, d2_s)
```

```python
import functools
import math

import numpy as np
import jax
import jax.numpy as jnp
from jax import lax
from jax.experimental import pallas as pl
from jax.experimental.pallas import tpu as pltpu

F32 = jnp.float32
BF16 = jnp.bfloat16
HIGHEST = lax.Precision.HIGHEST

V7X_VMEM_BYTES = 64 * 1024 * 1024
VMEM_LIMIT = V7X_VMEM_BYTES * 7 // 8
LANES = 128

EPS = 1e-6
NEG = -1e30
NEG_TEST = -1e29

HEAD_DIM = 64
HG_HEADS = 8
HG_CHUNK = 128
NSA_HEADS = 12
NSA_KV = 3
NSA_GROUP = NSA_HEADS // NSA_KV
CMP_BLOCK = 32
CMP_STRIDE = 16
SEL_BLOCK = 64
SEL_TOPK = 16
NSA_WINDOW = 512
FORCE_SCORE = 1e4
PAGE = 128
DIL_PATTERN = ((128, 1), (512, 4), (2048, 16))
DIL_HEADS = 4
RP_BUCKETS = 32
RP_MAX_DIST = 2048


def _cparams(*sem):
    return pltpu.CompilerParams(dimension_semantics=sem, vmem_limit_bytes=VMEM_LIMIT)


def _const(shape):
    nd = len(shape)
    return pl.BlockSpec(shape, lambda *_: (0,) * nd, pipeline_mode=pl.Buffered(1))


def _dot(a, b):
    return jnp.dot(a, b, preferred_element_type=F32)


def _dot_nt(a, b):
    return lax.dot_general(a, b, (((1,), (1,)), ((), ())), preferred_element_type=F32)


def _dot_tn(a, b):
    return lax.dot_general(a, b, (((0,), (0,)), ((), ())), preferred_element_type=F32)


def _rms(x, gain):
    return x * lax.rsqrt(jnp.mean(x * x, axis=-1, keepdims=True) + EPS) * gain


def _seg_ms(x, bd):
    sq = x * x
    hi = sq.astype(BF16)
    lo = (sq - hi.astype(F32)).astype(BF16)
    bw = bd.shape[0]
    cols = []
    for c0 in range(0, x.shape[1], bw):
        wd = min(bw, x.shape[1] - c0)
        blk = bd[:wd, :wd]
        cols.append(_dot(hi[:, c0:c0 + wd], blk) + _dot(lo[:, c0:c0 + wd], blk))
    return (cols[0] if len(cols) == 1 else jnp.concatenate(cols, axis=1)) * (1.0 / HEAD_DIM)


def _seg_rms(x, bd, gain):
    return x * lax.rsqrt(_seg_ms(x, bd) + EPS) * gain


def _norm_k_rows(x, bd, gain, kmask):
    bw = bd.shape[0]
    r = lax.rsqrt(_seg_ms(x[:, :bw], bd) + EPS)
    r = jnp.concatenate([r, jnp.ones((x.shape[0], x.shape[1] - bw), F32)], axis=1)
    return x * jnp.where(kmask > 0.5, r, 1.0) * gain


V7X_MXU_WIDTH = 256


def _block_diag_ones(width):
    i = np.arange(min(width, V7X_MXU_WIDTH)) // HEAD_DIM
    return jnp.asarray((i[:, None] == i[None, :]).astype(np.float32), BF16)


def _rel_bucket(dist):
    exact = RP_BUCKETS // 2
    d = jnp.maximum(dist, 0)
    log_ratio = jnp.log(jnp.maximum(d, 1).astype(F32) / exact) / math.log(RP_MAX_DIST / exact)
    large = jnp.minimum(exact + (log_ratio * (RP_BUCKETS - exact)).astype(jnp.int32), RP_BUCKETS - 1)
    return jnp.where(d < exact, d, large)


def _bias_tile(rel_bias, dist, valid):
    onehot = (_rel_bucket(dist)[..., None] == jnp.arange(RP_BUCKETS)).astype(F32)
    t = jnp.einsum('...k,kh->h...', onehot, rel_bias.astype(F32), precision=HIGHEST)
    return jnp.where(valid[None], t, NEG)


def _toeplitz(ext, n, m):
    p = n + m - 1
    u = jnp.concatenate([jnp.flip(ext[..., :m], -1), jnp.flip(ext[..., m:], -1)], axis=-1)
    reps = (1,) * (ext.ndim - 1) + (n,)
    t = jnp.tile(u, reps)[..., :n * (p - 1)].reshape(ext.shape[:-1] + (n, p - 1))
    return t[..., :m]


def _softmax_piece(s, mask, m_prev, l_prev, acc_prev, v, v_t=False):
    m_new = jnp.maximum(m_prev, jnp.max(s, axis=-1, keepdims=True))
    alpha = jnp.exp(m_prev - m_new)
    p = jnp.where(mask, jnp.exp(s - m_new), 0.0)
    l_new = alpha * l_prev + jnp.sum(p, axis=-1, keepdims=True)
    pb = p.astype(BF16)
    acc_new = alpha * acc_prev + (_dot_nt(pb, v) if v_t else _dot(pb, v))
    return m_new, l_new, acc_new


def _ffn_body(x_ref, g_ref, win_ref, wout_ref, *rest, dff, ple):
    x = x_ref[...]
    xn = _rms(x, g_ref[...]).astype(BF16)
    gu = _dot(xn, win_ref[...])
    gg = gu[:, :dff]
    a = (gg * jax.nn.sigmoid(gg) * gu[:, dff:]).astype(BF16)
    h = x + 0.5 * _dot(a, wout_ref[...])
    if ple:
        p_ref, gp_ref, wg_ref, wp_ref, o_ref = rest
        hn = _rms(h, gp_ref[...]).astype(BF16)
        gate = jax.nn.sigmoid(_dot(hn, wg_ref[...]))
        h = h + gate * _dot(p_ref[...].astype(BF16), wp_ref[...])
    else:
        (o_ref,) = rest
    o_ref[...] = h


def _layer(arr, li):
    tail = (0,) * (arr.ndim - 1)
    return pl.BlockSpec((None,) + arr.shape[1:], lambda *_: (li,) + tail, pipeline_mode=pl.Buffered(1))


def _ffn(h, li, gain, w_in, w_out, ple=None, *, tm):
    n, d = h.shape
    dff = w_out.shape[1]
    row = lambda i: (i, 0)
    in_specs = [pl.BlockSpec((tm, d), row), _layer(gain, li), _layer(w_in, li), _layer(w_out, li)]
    args = [h, gain, w_in, w_out]
    if ple is not None:
        p, gp, wg, wp = ple
        in_specs += [pl.BlockSpec((None, tm, p.shape[2]), lambda i: (li, i, 0)), _layer(gp, li), _layer(wg, li),
                     _layer(wp, li)]
        args += [p, gp, wg, wp]
    return pl.pallas_call(
        functools.partial(_ffn_body, dff=dff, ple=ple is not None),
        grid=(n // tm,), in_specs=in_specs, out_specs=pl.BlockSpec((tm, d), row),
        out_shape=jax.ShapeDtypeStruct((n, d), F32), compiler_params=_cparams("parallel"),
        name="ffn_ple" if ple is not None else "ffn",
    )(*args)


EVEN_HG = 4 * HG_HEADS * HEAD_DIM
EVEN_Q = NSA_HEADS * HEAD_DIM
ROWS_NSA = 2 * NSA_KV * HEAD_DIM
EVEN_PAD = EVEN_HG + EVEN_Q + 3 * ROWS_NSA + LANES


GATE_ROWS = 16


def _inproj_even_body(x_ref, g_ref, w_ref, gk_ref, km_ref, bd_ref, hg_ref, q_ref, cmp_ref, *outs, prompt):
    xn = _rms(x_ref[...], g_ref[...]).astype(BF16)
    p = _dot(xn, w_ref[...])
    c0 = EVEN_HG
    c1 = c0 + EVEN_Q
    hg_ref[...] = p[:, :c0]
    q_ref[...] = p[:, c0:c1]
    cmp = p[:, c1:c1 + ROWS_NSA]
    cmp_ref[...] = cmp
    bd = bd_ref[...]
    km = km_ref[...]
    slc = _norm_k_rows(p[:, c1 + ROWS_NSA:c1 + 2 * ROWS_NSA], bd, gk_ref[0:1, :], km)
    win = _norm_k_rows(p[:, c1 + 2 * ROWS_NSA:c1 + 3 * ROWS_NSA], bd, gk_ref[1:2, :], km)
    sig = jax.nn.sigmoid(p[:, c1 + 3 * ROWS_NSA:])
    if not prompt:
        slc_ref, win_ref, gate_ref = outs
        slc_ref[...] = slc
        win_ref[...] = win
        for g in range(NSA_KV):
            gate_ref[g] = sig if g == 0 else pltpu.roll(sig, LANES - 3 * NSA_GROUP * g, 1)
        return
    slcg_ref, wing_ref, gate_t_ref, cmp_t_ref, slc_t_ref, win_t_ref = outs
    kw = NSA_KV * HEAD_DIM
    sig_t = sig.T
    for g in range(NSA_KV):
        ks = slice(HEAD_DIM * g, HEAD_DIM * (g + 1))
        vs = slice(kw + HEAD_DIM * g, kw + HEAD_DIM * (g + 1))
        slcg_ref[g] = jnp.concatenate([slc[:, ks], slc[:, vs]], axis=1)
        wing_ref[g] = jnp.concatenate([win[:, ks], win[:, vs]], axis=1)
        for half in range(sig.shape[0] // NSA_TQ):
            gate_t_ref[g, half] = sig_t[3 * NSA_GROUP * g:3 * NSA_GROUP * g + GATE_ROWS,
                                        half * NSA_TQ:(half + 1) * NSA_TQ]
    for dst_ref, rows in ((cmp_t_ref, cmp), (slc_t_ref, slc), (win_t_ref, win)):
        _store_transposed(dst_ref, rows)


def _store_transposed(dst_ref, x):
    for c in range(x.shape[1] // LANES):
        dst_ref[0, c * LANES:(c + 1) * LANES, :] = x[:, c * LANES:(c + 1) * LANES].T


def _inproj_even(h, gain, w_pad, qk_norm, *, tm, seq=None):
    n, d = h.shape
    kw = NSA_KV * HEAD_DIM
    ones = jnp.ones((kw,), F32)
    gk = jnp.stack([jnp.concatenate([jnp.tile(qk_norm[2], NSA_KV), ones]),
                    jnp.concatenate([jnp.tile(qk_norm[3], NSA_KV), ones])])
    km = jnp.concatenate([ones, 0.0 * ones]).reshape(1, ROWS_NSA)
    row = lambda i: (i, 0)
    grp = lambda i: (0, i, 0)
    shp = lambda *s: jax.ShapeDtypeStruct(s, F32)
    out_specs = [pl.BlockSpec((tm, EVEN_HG), row), pl.BlockSpec((tm, EVEN_Q), row), pl.BlockSpec((tm, ROWS_NSA), row)]
    out_shape = [shp(n, EVEN_HG), shp(n, EVEN_Q), shp(n, ROWS_NSA)]
    if seq is None:
        out_specs += [pl.BlockSpec((tm, ROWS_NSA), row)] * 2 + [pl.BlockSpec((NSA_KV, tm, LANES), grp)]
        out_shape += [shp(n, ROWS_NSA)] * 2 + [shp(NSA_KV, n, LANES)]
    else:
        tpb = seq // tm
        out_specs += [pl.BlockSpec((NSA_KV, tm, LANES), grp)] * 2
        out_shape += [shp(NSA_KV, n, LANES)] * 2
        out_specs.append(pl.BlockSpec((NSA_KV, tm // NSA_TQ, GATE_ROWS, NSA_TQ), lambda i: (0, i, 0, 0)))
        out_shape.append(shp(NSA_KV, n // NSA_TQ, GATE_ROWS, NSA_TQ))
        out_specs += [pl.BlockSpec((1, ROWS_NSA, tm), lambda i: (i // tpb, 0, i % tpb))] * 3
        out_shape += [shp(n // seq, ROWS_NSA, seq)] * 3
    return pl.pallas_call(
        functools.partial(_inproj_even_body, prompt=seq is not None), grid=(n // tm,),
        in_specs=[pl.BlockSpec((tm, d), row), _const((1, d)), _const(w_pad.shape), _const((2, ROWS_NSA)),
                  _const((1, ROWS_NSA)), _const(_block_diag_ones(ROWS_NSA).shape)],
        out_specs=out_specs, out_shape=out_shape,
        compiler_params=_cparams("parallel"), name="inproj_even",
    )(h, gain.reshape(1, d), w_pad, gk, km, _block_diag_ones(ROWS_NSA))


ODD_W = 3 * DIL_HEADS * HEAD_DIM
ODD_GW = DIL_HEADS * HEAD_DIM


def _to_residue_view(src, dst_ref, stage_ref, dil):
    rows, w = src.shape
    for blk in range(w // LANES):
        stage_ref[...] = src[:, blk * LANES:(blk + 1) * LANES]
        for r in range(dil):
            lo = r * w + blk * LANES
            dst_ref[:, lo:lo + LANES] = stage_ref[pl.ds(r, rows // dil, stride=dil), :]


def _from_residue_view(src_ref, stage_ref, dil, w):
    rows = src_ref.shape[0] * dil
    parts = []
    for blk in range(w // LANES):
        for r in range(dil):
            lo = r * w + blk * LANES
            stage_ref[pl.ds(r, rows // dil, stride=dil), :] = src_ref[:, lo:lo + LANES]
        parts.append(stage_ref[...])
    return jnp.concatenate(parts, axis=1)


def _inproj_odd_body(x_ref, g_ref, w_ref, gq_ref, gk_ref, bd_ref, *outs, prompt):
    xn = _rms(x_ref[...], g_ref[...]).astype(BF16)
    p = _dot(xn, w_ref[...])
    bd = bd_ref[...]
    q = _seg_rms(p[:, :ODD_W], bd, gq_ref[...]) * (HEAD_DIM ** -0.5)
    k = _seg_rms(p[:, ODD_W:2 * ODD_W], bd, gk_ref[...])
    v = p[:, 2 * ODD_W:]
    for g in range(3):
        s = slice(ODD_GW * g, ODD_GW * (g + 1))
        kv = jnp.concatenate([k[:, s], v[:, s]], axis=1)
        if not prompt:
            outs[0][g] = q[:, s]
            outs[1][g] = kv
            continue
        dil = DIL_PATTERN[g][1]
        if dil == 1:
            outs[2 * g][...] = q[:, s]
            outs[2 * g + 1][...] = kv
        else:
            _to_residue_view(q[:, s], outs[2 * g], outs[-1], dil)
            _to_residue_view(kv, outs[2 * g + 1], outs[-1], dil)
        _store_transposed(outs[6 + g], kv)


def _inproj_odd(h, gain, w, qk_norm, *, tm, seq=None):
    n, d = h.shape
    row = lambda i: (i, 0)
    nh = 3 * DIL_HEADS
    prompt = seq is not None
    if not prompt:
        out_specs = [pl.BlockSpec((3, tm, ODD_GW), lambda i: (0, i, 0)),
                     pl.BlockSpec((3, tm, 2 * ODD_GW), lambda i: (0, i, 0))]
        out_shape = [jax.ShapeDtypeStruct((3, n, ODD_GW), F32), jax.ShapeDtypeStruct((3, n, 2 * ODD_GW), F32)]
    else:
        out_specs, out_shape = [], []
        for _, dil in DIL_PATTERN:
            assert dil == 1 or DIL_PATTERN[0][1] == 1
            for width in (ODD_GW, 2 * ODD_GW):
                out_specs.append(pl.BlockSpec((tm // dil, dil * width), row))
                out_shape.append(jax.ShapeDtypeStruct((n // dil, dil * width), F32))
        tpb = seq // tm
        for _ in DIL_PATTERN:
            out_specs.append(pl.BlockSpec((1, 2 * ODD_GW, tm), lambda i: (i // tpb, 0, i % tpb)))
            out_shape.append(jax.ShapeDtypeStruct((n // seq, 2 * ODD_GW, seq), F32))
    return pl.pallas_call(
        functools.partial(_inproj_odd_body, prompt=prompt), grid=(n // tm,),
        in_specs=[pl.BlockSpec((tm, d), row), _const((1, d)), _const(w.shape), _const((1, ODD_W)),
                  _const((1, ODD_W)), _const(_block_diag_ones(ODD_W).shape)],
        out_specs=out_specs, out_shape=out_shape,
        scratch_shapes=[pltpu.VMEM((tm, LANES), F32)] if prompt else [],
        compiler_params=_cparams("parallel"), name="inproj_odd",
    )(h, gain.reshape(1, d), w, jnp.tile(qk_norm[0], nh).reshape(1, ODD_W),
      jnp.tile(qk_norm[1], nh).reshape(1, ODD_W), _block_diag_ones(ODD_W))


def _outproj2_body(h_ref, a_ref, b_ref, w_ref, o_ref):
    ka = a_ref.shape[1]
    o_ref[...] = (h_ref[...] + _dot(a_ref[...].astype(BF16), w_ref[:ka])
                  + _dot(b_ref[...].astype(BF16), w_ref[ka:]))


def _outproj2(h, a, b, w, *, tm):
    n, d = h.shape
    row = lambda i: (i, 0)
    return pl.pallas_call(
        _outproj2_body, grid=(n // tm,),
        in_specs=[pl.BlockSpec((tm, d), row), pl.BlockSpec((tm, a.shape[1]), row),
                  pl.BlockSpec((tm, b.shape[1]), row), _const(w.shape)],
        out_specs=pl.BlockSpec((tm, d), row), out_shape=jax.ShapeDtypeStruct((n, d), F32),
        compiler_params=_cparams("parallel"), name="outproj_even",
    )(h, a, b, w)


def _outproj2t_body(h_ref, a_ref, bt_ref, w_ref, o_ref):
    ka = a_ref.shape[1]
    acc = h_ref[...] + _dot(a_ref[...].astype(BF16), w_ref[:ka])
    tq = bt_ref.shape[2]
    for i in range(bt_ref.shape[0]):
        rs = slice(i * tq, (i + 1) * tq)
        o_ref[rs, :] = acc[rs, :] + _dot_tn(bt_ref[i].astype(BF16), w_ref[ka:])


def _outproj2t(h, a, bt, w, *, tm):
    n, d = h.shape
    row = lambda i: (i, 0)
    tq = bt.shape[2]
    return pl.pallas_call(
        _outproj2t_body, grid=(n // tm,),
        in_specs=[pl.BlockSpec((tm, d), row), pl.BlockSpec((tm, a.shape[1]), row),
                  pl.BlockSpec((tm // tq, bt.shape[1], tq), lambda i: (i, 0, 0)), _const(w.shape)],
        out_specs=pl.BlockSpec((tm, d), row), out_shape=jax.ShapeDtypeStruct((n, d), F32),
        compiler_params=_cparams("parallel"), name="outproj_even_t",
    )(h, a, bt, w)


def _outproj_merge_body(h_ref, o0_ref, o1_ref, o2_ref, l0_ref, l1_ref, l2_ref, w_ref, o_ref, *stage, views):
    if views:
        load = lambda ref, g: (ref[...] if DIL_PATTERN[g][1] == 1
                               else _from_residue_view(ref, stage[0], DIL_PATTERN[g][1], ODD_GW))
    else:
        load = lambda ref, g: ref[...]
    mixed = _merge_groups([load(r, g) for g, r in enumerate((o0_ref, o1_ref, o2_ref))],
                          [load(r, g) for g, r in enumerate((l0_ref, l1_ref, l2_ref))])
    o_ref[...] = h_ref[...] + _dot(mixed.astype(BF16), w_ref[...])


def _outproj_merge(h, outs, lses, w, *, tm, views=False):
    n, d = h.shape
    row = lambda i: (i, 0)
    parts = [pl.BlockSpec((tm * a.shape[0] // n, a.shape[1]), row) for a in list(outs) + list(lses)]
    return pl.pallas_call(
        functools.partial(_outproj_merge_body, views=views), grid=(n // tm,),
        in_specs=[pl.BlockSpec((tm, d), row)] + parts + [_const(w.shape)],
        out_specs=pl.BlockSpec((tm, d), row), out_shape=jax.ShapeDtypeStruct((n, d), F32),
        scratch_shapes=[pltpu.VMEM((tm, LANES), F32)] if views else [],
        compiler_params=_cparams("parallel"), name="outproj_odd",
    )(h, *outs, *lses, w)


HG_MM_LEVELS = 2


def _hgrn_prefix_matrix(C):
    t = np.arange(C)[:, None]
    j = np.arange(C)[None, :]
    mats = [j <= t]
    for lvl in range(1, HG_MM_LEVELS + 1):
        if (1 << lvl) <= C:
            mats.append(j <= ((t >> lvl) << lvl) + (1 << lvl) // 2 - 1)
    return jnp.asarray(np.concatenate(mats, axis=0).astype(np.float32), BF16)


def _hgrn_body(hq_ref, hf_ref, hi_ref, hgt_ref, lb_ref, gn_ref, bd_ref, pm_ref, s0_ref, o_ref, sfin_ref, st_ref,
               *b_scr, C):
    c = pl.program_id(1)

    @pl.when(c == 0)
    def _():
        st_ref[...] = s0_ref[0]

    dk = HEAD_DIM
    w = HG_HEADS * dk
    lb = lb_ref[...]
    f = lb + (1.0 - lb) * jax.nn.sigmoid(hf_ref[...])
    lf = jnp.log(f)
    kk = 1.0 - f
    ti = lax.broadcasted_iota(jnp.int32, (C, C), 0)
    si = lax.broadcasted_iota(jnp.int32, (C, C), 1)
    p1 = lf.astype(BF16)
    r1 = lf - p1.astype(F32)
    p2 = r1.astype(BF16)
    p3 = (r1 - p2.astype(F32)).astype(BF16)
    sums = _dot(pm_ref[...], jnp.concatenate([p1, p2, p3], axis=1))
    sums = sums[:, :w] + sums[:, w:2 * w] + sums[:, 2 * w:]
    b = sums[0:C]
    for c4, scr in enumerate(b_scr):
        scr[...] = b[:, c4 * LANES:(c4 + 1) * LANES]
    q = hq_ref[...]
    v = hi_ref[...].astype(BF16)
    heads = range(HG_HEADS)
    hs = [slice(dk * h, dk * (h + 1)) for h in heads]

    qb = q.astype(BF16)
    kb = kk.astype(BF16)
    a = [jnp.where(ti == si, _dot_nt(qb[:, hs[h]], kb[:, hs[h]]), 0.0) for h in heads]
    lvl = 1
    while (1 << lvl) <= C:
        m = 1 << lvl
        half = m // 2
        if lvl <= HG_MM_LEVELS:
            r = sums[lvl * C:(lvl + 1) * C]
        else:
            ref_rows = pl.ds(half - 1, C // m, stride=m)
            r = jnp.concatenate([jnp.broadcast_to(scr[ref_rows, :][:, None, :], (C // m, m, LANES)).reshape(C, LANES)
                                 for scr in b_scr], axis=1)
        e = jnp.exp(-jnp.abs(b - r))
        qt = (q * e).astype(BF16)
        kt = (kk * e).astype(BF16)
        pair = ((ti >> lvl) == (si >> lvl)) & ((ti & (m - 1)) >= half) & ((si & (m - 1)) < half)
        a = [a[h] + jnp.where(pair, _dot_nt(qt[:, hs[h]], kt[:, hs[h]]), 0.0) for h in heads]
        lvl += 1

    b_last = b[C - 1:C, :]
    qe = (q * jnp.exp(b)).astype(BF16)
    kdec = (kk * jnp.exp(b_last - b)).astype(BF16)
    e_last = jnp.exp(b_last)
    outs = []
    for h in heads:
        st = st_ref[h]
        outs.append(_dot(a[h].astype(BF16), v[:, hs[h]]) + _dot_nt(qe[:, hs[h]], st.astype(BF16)))
        st_ref[h] = st * e_last[:, hs[h]] + _dot_tn(v[:, hs[h]], kdec[:, hs[h]])
    o = jnp.concatenate(outs, axis=1)
    gt = hgt_ref[...]
    o_ref[...] = _seg_rms(o, bd_ref[...], gn_ref[...]) * (gt * jax.nn.sigmoid(gt))

    @pl.when(c == pl.num_programs(1) - 1)
    def _():
        sfin_ref[0] = st_ref[...]


def _hgrn(hg, lb, hg_norm, s0, *, batch, seq):
    C = math.gcd(seq, HG_CHUNK)
    nc = seq // C
    w = HG_HEADS * HEAD_DIM
    col = lambda j: pl.BlockSpec((C, w), lambda b, c: (b * nc + c, j))
    st_spec = pl.BlockSpec((1, HG_HEADS, HEAD_DIM, HEAD_DIM), lambda b, c: (b, 0, 0, 0))
    pm = _hgrn_prefix_matrix(C)
    o, sfin = pl.pallas_call(
        functools.partial(_hgrn_body, C=C), grid=(batch, nc),
        in_specs=[col(0), col(1), col(2), col(3), _const((1, w)), _const((1, w)), _const(_block_diag_ones(w).shape), _const(pm.shape),
                  st_spec],
        out_specs=[pl.BlockSpec((C, w), lambda b, c: (b * nc + c, 0)), st_spec],
        out_shape=[jax.ShapeDtypeStruct((batch * seq, w), F32),
                   jax.ShapeDtypeStruct((batch, HG_HEADS, HEAD_DIM, HEAD_DIM), F32)],
        scratch_shapes=[pltpu.VMEM((HG_HEADS, HEAD_DIM, HEAD_DIM), F32)] + [pltpu.VMEM((C, LANES), F32)] * (w // LANES),
        compiler_params=_cparams("parallel", "arbitrary"), name="hgrn_scan",
    )(hg, hg, hg, hg, lb.reshape(1, w), jnp.tile(hg_norm, HG_HEADS).reshape(1, w), _block_diag_ones(w), pm,
      jnp.swapaxes(s0, 2, 3))
    return o, jnp.swapaxes(sfin, 2, 3)


CHUNKS_PER_PAGE = PAGE // CMP_STRIDE
PAGE_LANES = CMP_STRIDE * ROWS_NSA
CMP_HID = 2 * NSA_KV * 2 * HEAD_DIM
CMP_PP = 16


def _compress_body(pt_ref, *refs, n_chunks, transposed):
    pages = refs[:CMP_PP]
    wbig_ref, pe_ref, w2_ref, gk_ref, km_ref, bd_ref, perm_ref, rows_ref, grp_ref, xs_ref = refs[CMP_PP:]
    nb = ROWS_NSA // LANES
    j = pl.program_id(1)
    perm = perm_ref[...]
    for i in range(0, CMP_PP, 2):
        pair = []
        for k in range(2):
            pg = pages[i + k][0]
            if transposed:
                pg = jnp.concatenate([pg[c * LANES:(c + 1) * LANES, :].T for c in range(nb)], axis=1)
            pair.append(_dot(perm, pg.astype(BF16)))
        r0 = pl.multiple_of((j * CMP_PP + i) * CHUNKS_PER_PAGE, 2 * CHUNKS_PER_PAGE)
        for s in range(CMP_STRIDE):
            rs = slice(s * CHUNKS_PER_PAGE, (s + 1) * CHUNKS_PER_PAGE)
            both = jnp.concatenate([pair[0][rs], pair[1][rs]], axis=0).astype(BF16)
            for c in range(nb):
                lo = (c * CMP_STRIDE + s) * LANES
                xs_ref[pl.ds(r0, 2 * CHUNKS_PER_PAGE), lo:lo + LANES] = both[:, c * LANES:(c + 1) * LANES]

    @pl.when(j == pl.num_programs(1) - 1)
    def _():
        blk_w = CMP_STRIDE * LANES
        first = lambda x: jnp.concatenate(
            [_dot(x[:, c * blk_w:(c + 1) * blk_w], wbig_ref[c]) for c in range(nb)], axis=1)
        y = first(xs_ref[...])
        ysh = pltpu.roll(y, n_chunks - 1, 0)
        hid = y + pltpu.roll(ysh, CMP_HID - HEAD_DIM, 1)
        cf = first(pe_ref[...].astype(BF16))
        lane = lax.broadcasted_iota(jnp.int32, (8, CMP_HID), 1)
        cs = jnp.where(((lane >> 6) & 1) == 0, jnp.broadcast_to(cf[0:1], (8, CMP_HID)),
                       jnp.broadcast_to(cf[1:2], (8, CMP_HID)))
        cs = cs + pltpu.roll(cs, CMP_HID - HEAD_DIM, 1)
        hid = hid + cs[0:1]
        act = (hid * jax.nn.sigmoid(hid)).astype(BF16)
        out = _dot(act, w2_ref[...])
        out = _norm_k_rows(out, bd_ref[...], gk_ref[...], km_ref[...])
        rows_ref[0] = out
        kw = NSA_KV * HEAD_DIM
        for g in range(NSA_KV):
            grp_ref[0, g] = jnp.concatenate([out[:, HEAD_DIM * g:HEAD_DIM * (g + 1)],
                                             out[:, kw + HEAD_DIM * g:kw + HEAD_DIM * (g + 1)]], axis=1)


def _compress_weights(cmp_pe, cmp_w1, cmp_w2, k_gain):
    ratio = CMP_BLOCK // CMP_STRIDE
    w1 = cmp_w1.reshape(2, ratio, CMP_STRIDE, HEAD_DIM, HEAD_DIM)
    eye_kv = jnp.eye(2, dtype=F32)
    eye_g = jnp.eye(NSA_KV, dtype=F32)
    nb = ROWS_NSA // LANES
    member_kv = np.arange(2 * NSA_KV) // NSA_KV
    w1m = w1[member_kv].reshape(nb, 2, ratio, CMP_STRIDE, HEAD_DIM, HEAD_DIM)
    wbig = jnp.einsum('chrsde,hH->cshdHre', w1m, eye_kv).reshape(nb, CMP_STRIDE * LANES, 2 * ratio * HEAD_DIM)
    wbig = wbig.astype(BF16)
    pe = cmp_pe.reshape(2, ratio, CMP_STRIDE, HEAD_DIM)
    pe_rows = pe[member_kv].reshape(nb, 2, ratio, CMP_STRIDE, HEAD_DIM).transpose(2, 0, 3, 1, 4)
    pe_rows = pe_rows.reshape(ratio, PAGE_LANES)
    pe_rows = jnp.concatenate([pe_rows, jnp.zeros((8 - ratio, PAGE_LANES), F32)], axis=0)
    sel_r0 = jnp.asarray([1.0, 0.0], F32)
    w2big = jnp.einsum('ked,kK,gG,r->kgreKGd', cmp_w2, eye_kv, eye_g, sel_r0).reshape(CMP_HID, ROWS_NSA).astype(BF16)
    kw = NSA_KV * HEAD_DIM
    gk = jnp.concatenate([jnp.tile(k_gain, NSA_KV), jnp.ones((kw,), F32)]).reshape(1, ROWS_NSA)
    km = jnp.concatenate([jnp.ones((kw,), F32), jnp.zeros((kw,), F32)]).reshape(1, ROWS_NSA)
    return wbig, pe_rows, w2big, gk, km


def _compress(pool, table, cw, *, transposed):
    wbig, pe_rows, w2big, gk, km = cw
    bsz, npg = table.shape
    n_chunks = npg * CHUNKS_PER_PAGE
    page = lambda i: pl.BlockSpec((1,) + pool.shape[1:], lambda b, j, pt: (pt[b, j * CMP_PP + i], 0, 0))
    tok = np.arange(PAGE)
    perm = (tok[None, :] == (tok[:, None] % CHUNKS_PER_PAGE) * CMP_STRIDE + tok[:, None] // CHUNKS_PER_PAGE)
    perm = jnp.asarray(perm.astype(np.float32), BF16)
    return pl.pallas_call(
        functools.partial(_compress_body, n_chunks=n_chunks, transposed=transposed),
        grid_spec=pltpu.PrefetchScalarGridSpec(
            num_scalar_prefetch=1, grid=(bsz, npg // CMP_PP),
            in_specs=[page(i) for i in range(CMP_PP)] + [
                _const(wbig.shape), _const(pe_rows.shape), _const(w2big.shape), _const((1, ROWS_NSA)),
                _const((1, ROWS_NSA)), _const(_block_diag_ones(ROWS_NSA).shape), _const(perm.shape)],
            out_specs=[pl.BlockSpec((1, n_chunks, ROWS_NSA), lambda b, j, pt: (b, 0, 0)),
                       pl.BlockSpec((1, NSA_KV, n_chunks, LANES), lambda b, j, pt: (b, 0, 0, 0))],
            scratch_shapes=[pltpu.VMEM((n_chunks, PAGE_LANES), BF16)]),
        out_shape=[jax.ShapeDtypeStruct((bsz, n_chunks, ROWS_NSA), F32),
                   jax.ShapeDtypeStruct((bsz, NSA_KV, n_chunks, LANES), F32)],
        compiler_params=_cparams("parallel", "arbitrary"), name="nsa_compress",
    )(table, *([pool] * CMP_PP), wbig, pe_rows, w2big, gk, km, _block_diag_ones(ROWS_NSA), perm)


def _select_blocks(imp, pos, n_slc, k_sel):
    lane = lax.broadcasted_iota(jnp.int32, imp.shape, 1)
    cur = pos >> 6
    forced = (lane == 0) | (lane == cur) | (lane == cur - 1)
    score = jnp.where(forced, FORCE_SCORE, jnp.where(lane <= cur, imp, -1.0))
    rank = jnp.zeros(imp.shape, F32)
    for i in range(n_slc):
        ci = score[:, i:i + 1]
        rank = rank + jnp.where(ci > score, 1.0, jnp.where(ci == score, jnp.where(lane > i, 1.0, 0.0), 0.0))
    return jnp.where(lane < n_slc, jnp.where(rank < k_sel, 1.0, 0.0), 0.0)


NSA_TQ = 256
NSA_QC = LANES


def _select_blocks_t(imp, pos, n_slc, k_sel):
    blk = lax.broadcasted_iota(jnp.int32, imp.shape, 0)
    cur = pos >> 6
    forced = (blk == 0) | (blk == cur) | (blk == cur - 1)
    score = jnp.where(forced, FORCE_SCORE, jnp.where(blk <= cur, imp, -1.0))
    rank = jnp.zeros(imp.shape, F32)
    for i in range(n_slc):
        ci = score[i:i + 1, :]
        rank = rank + jnp.where(ci > score, 1.0, jnp.where(ci == score, jnp.where(blk > i, 1.0, 0.0), 0.0))
    return jnp.where(rank < k_sel, 1.0, 0.0)


def _nsa_prompt_t_body(q_ref, gate_ref, kcv_ref, slc_ref, win_ref, gq_ref, bd_ref, eye_ref, fs_ref, fc_ref,
                       ov_ref, ex_ref, o_ref, bc_ref, bs_ref, bw_ref, qt_scr, ks_scr, kw_scr, vs_scr, vw_scr, sel_scr, m_scr, l_scr, acc_scr,
                       *, seq, n_slc, k_sel):
    tq = NSA_TQ
    qc = NSA_QC
    hd = HEAD_DIM
    n_tiles = seq // tq
    width = NSA_GROUP * tq
    n_chunks = width // qc
    eye = eye_ref[...]
    n_c = kcv_ref.shape[2]
    far = NSA_WINDOW // tq

    @pl.when(pl.program_id(1) == 0)
    def _():
        key_i = lax.broadcasted_iota(jnp.int32, (tq, tq), 0)
        qry_i = lax.broadcasted_iota(jnp.int32, (tq, tq), 1)
        blk_i = lax.broadcasted_iota(jnp.int32, (n_c, tq), 0)
        n_cmp = seq // CMP_STRIDE - CMP_BLOCK // CMP_STRIDE + 1
        span = tq + CMP_STRIDE * n_c
        for h in range(NSA_GROUP):
            hs = slice(h * tq, (h + 1) * tq)
            for d in range(n_tiles):
                u = jnp.broadcast_to(fs_ref[0, h:h + 1, d * tq:(d + 2) * tq], (tq, 2 * tq))
                tile = pltpu.roll(u, tq, 1, stride=1, stride_axis=0)[:, :tq]
                bs_ref[d, :, hs] = tile
                if d == far:
                    bw_ref[0, :, hs] = jnp.where(qry_i <= key_i, tile, NEG)
            for t in range(n_tiles):
                u = jnp.broadcast_to(fc_ref[0, h:h + 1, t * tq:t * tq + span], (n_c, span))
                tile = pltpu.roll(u, tq, 1, stride=CMP_STRIDE, stride_axis=0)[:, :tq]
                bc_ref[t, :, hs] = jnp.where(blk_i < n_cmp, tile, NEG)

    qn = (_seg_rms(q_ref[...], bd_ref[...], gq_ref[...]) * (hd ** -0.5)).astype(BF16)
    for t in range(n_tiles):
        rs = slice(t * tq, (t + 1) * tq)
        for h in range(NSA_GROUP):
            qt_scr[t, :, h * tq:(h + 1) * tq] = _dot_nt(eye, qn[rs, h * hd:(h + 1) * hd]).astype(BF16)
        for kv_ref, k_scr, v_scr in ((slc_ref, ks_scr, vs_scr), (win_ref, kw_scr, vw_scr)):
            kv = kv_ref[0, rs, :].astype(BF16)
            k_scr[t] = kv[:, :hd]
            v_scr[t] = _dot_nt(eye, kv[:, hd:]).astype(BF16)
    kc = kcv_ref[0, 0, :, :hd].astype(BF16)
    vct = _dot_nt(eye, kcv_ref[0, 0, :, hd:].astype(BF16)).astype(BF16)
    def flash_init():
        m_scr[...] = jnp.full((1, width), NEG, F32)
        l_scr[...] = jnp.zeros((1, width), F32)
        acc_scr[...] = jnp.zeros((hd, width), F32)

    def flash_update(k_scr, v_scr, t, tiles, running):
        s_alls = [_dot(k_scr[jt], qt_scr[t]) for jt, _, _ in tiles]
        if running:
            m_all = m_scr[...]
            l_all = l_scr[...]
        alphas, ls, ms, ps = [], [], [], [[] for _ in tiles]
        for c in range(n_chunks):
            cs = slice(c * qc, (c + 1) * qc)
            qoff = (c * qc) % tq
            ss = []
            for (_, bias_tile, sel_add), s_all in zip(tiles, s_alls):
                s = s_all[:, cs] + bias_tile(cs)
                ss.append(s if sel_add is None else s + sel_add[:, qoff:qoff + qc])
            m_new = functools.reduce(jnp.maximum, [jnp.max(s, axis=0, keepdims=True) for s in ss])
            if running:
                m_new = jnp.maximum(m_all[:, cs], m_new)
                alpha = jnp.exp(m_all[:, cs] - m_new)
                alphas.append(alpha)
            l_new = alpha * l_all[:, cs] if running else 0.0
            for k, s in enumerate(ss):
                p = jnp.exp(s - m_new)
                l_new = l_new + jnp.sum(p, axis=0, keepdims=True)
                ps[k].append(p.astype(BF16))
            ls.append(l_new)
            ms.append(m_new)
        cat = lambda xs: jnp.concatenate(xs, axis=1)
        pv = sum(_dot(v_scr[jt], cat(ps[k])) for k, (jt, _, _) in enumerate(tiles))
        acc_scr[...] = cat(alphas) * acc_scr[...] + pv if running else pv
        m_scr[...] = cat(ms)
        l_scr[...] = cat(ls)

    def flash_result():
        return acc_scr[...] / jnp.maximum(l_scr[...], 1e-30)

    def q_tile(t, carry):
        bias_c = bc_ref[t]
        s_c = _dot(kc, qt_scr[t]) + bias_c
        m_c = jnp.max(s_c, axis=0, keepdims=True)
        e_c = jnp.where(bias_c > NEG_TEST, jnp.exp(s_c - m_c), 0.0)
        p_c = (e_c / jnp.maximum(jnp.sum(e_c, axis=0, keepdims=True), 1e-30)).astype(BF16)
        o_c = _dot(vct, p_c)
        imp = sum(_dot(ov_ref[...], p_c[:, h * tq:(h + 1) * tq]) for h in range(NSA_GROUP))
        pos = t * tq + lax.broadcasted_iota(jnp.int32, (1, tq), 1)
        sel = _select_blocks_t(imp[:n_slc], pos, n_slc, k_sel)
        sel_scr[...] = jnp.zeros(sel_scr.shape, BF16)
        sel_scr[0:n_slc, :] = ((1.0 - sel) * NEG).astype(BF16)
        flash_init()
        slc_tile = lambda jt: (jt, lambda cs: bs_ref[t - jt, :, cs], _dot(ex_ref[jt], sel_scr[...]))

        def slc_pair(i, carry):
            flash_update(ks_scr, vs_scr, t, [slc_tile(2 * i), slc_tile(2 * i + 1)], True)
            return carry

        lax.fori_loop(0, (t + 1) // 2, slc_pair, 0)
        pl.when((t & 1) == 0)(lambda: flash_update(ks_scr, vs_scr, t, [slc_tile(t)], True))
        o_s = flash_result()
        far = NSA_WINDOW // tq
        win_tile = lambda d: (t - d, (lambda cs: bw_ref[0, :, cs]) if d == far else (lambda cs: bs_ref[d, :, cs]),
                              None)
        for n_back in range(far + 1):
            cond = (t == n_back) if n_back < far else (t >= far)
            pl.when(cond)(functools.partial(flash_update, kw_scr, vw_scr, t,
                                            [win_tile(d) for d in range(n_back, -1, -1)], False))
        o_w = flash_result()
        gates = gate_ref[0, t]
        for h in range(NSA_GROUP):
            hs = slice(h * tq, (h + 1) * tq)
            o_ref[t, h * hd:(h + 1) * hd, :] = (gates[3 * h:3 * h + 1] * o_c[:, hs]
                                                + gates[3 * h + 1:3 * h + 2] * o_s[:, hs]
                                                + gates[3 * h + 2:3 * h + 3] * o_w[:, hs])
        return carry

    lax.fori_loop(0, n_tiles, q_tile, 0)


def _nsa_prompt_t_tables(rel_bias, seq, n_c):
    tq = NSA_TQ
    n_tiles = seq // tq
    n_slc = -(-seq // SEL_BLOCK)
    ds = jnp.arange((n_tiles + 1) * tq) - tq
    fs = _bias_tile(rel_bias, ds, ds >= 0).reshape(NSA_KV, NSA_GROUP, -1)
    dc = jnp.arange(n_tiles * tq + CMP_STRIDE * n_c) - (CMP_STRIDE * n_c + CMP_BLOCK - 1)
    fc = _bias_tile(rel_bias, dc, dc >= 0).reshape(NSA_KV, NSA_GROUP, -1)
    n_cmp = seq // CMP_STRIDE - CMP_BLOCK // CMP_STRIDE + 1
    c_lo = np.arange(n_c) * CMP_STRIDE
    s_lo = np.arange(LANES) * SEL_BLOCK
    ov = ((c_lo[None, :] <= s_lo[:, None] + SEL_BLOCK - 1) & (c_lo[None, :] + CMP_BLOCK - 1 >= s_lo[:, None])
          & (np.arange(n_c) < n_cmp)[None, :] & (np.arange(LANES) < n_slc)[:, None])
    ex = (np.arange(seq)[:, None] // SEL_BLOCK == np.arange(LANES)[None, :]).reshape(n_tiles, tq, LANES)
    as_bf = lambda a: jnp.asarray(a.astype(np.float32), BF16)
    return fs, fc, as_bf(ov), as_bf(ex), n_slc


def _nsa_prompt_t(q, gates_t, kcv, slcg, wing, q_gain, tables, *, batch, seq):
    fs, fc, ov, ex, n_slc = tables
    n_c = kcv.shape[2]
    tq = NSA_TQ
    n_tiles = seq // tq
    assert n_tiles > NSA_WINDOW // tq
    gw = NSA_GROUP * HEAD_DIM
    width = NSA_GROUP * tq
    per_g = lambda shape: pl.BlockSpec((1,) + shape, lambda g, b: (g,) + (0,) * len(shape),
                                       pipeline_mode=pl.Buffered(1))
    eye = jnp.eye(HEAD_DIM, dtype=BF16)
    return pl.pallas_call(
        functools.partial(_nsa_prompt_t_body, seq=seq, n_slc=n_slc, k_sel=min(SEL_TOPK, n_slc)),
        grid=(NSA_KV, batch),
        in_specs=[pl.BlockSpec((seq, gw), lambda g, b: (b, g)),
                  pl.BlockSpec((1, n_tiles, 16, tq), lambda g, b: (g, b, 0, 0)),
                  pl.BlockSpec((1, 1, n_c, LANES), lambda g, b: (b, g, 0, 0)),
                  pl.BlockSpec((1, seq, LANES), lambda g, b: (g, b, 0)),
                  pl.BlockSpec((1, seq, LANES), lambda g, b: (g, b, 0)),
                  _const((1, gw)), _const(_block_diag_ones(gw).shape), _const(eye.shape),
                  per_g(fs.shape[1:]), per_g(fc.shape[1:]), _const(ov.shape), _const(ex.shape)],
        out_specs=pl.BlockSpec((n_tiles, gw, tq), lambda g, b: (b, g, 0)),
        out_shape=jax.ShapeDtypeStruct((batch * n_tiles, NSA_KV * gw, tq), F32),
        scratch_shapes=[pltpu.VMEM((n_tiles, n_c, width), F32), pltpu.VMEM((n_tiles, tq, width), F32),
                        pltpu.VMEM((1, tq, width), F32), pltpu.VMEM((n_tiles, HEAD_DIM, width), BF16),
                        pltpu.VMEM((n_tiles, tq, HEAD_DIM), BF16), pltpu.VMEM((n_tiles, tq, HEAD_DIM), BF16),
                        pltpu.VMEM((n_tiles, HEAD_DIM, tq), BF16), pltpu.VMEM((n_tiles, HEAD_DIM, tq), BF16),
                        pltpu.VMEM((LANES, tq), BF16), pltpu.VMEM((1, width), F32), pltpu.VMEM((1, width), F32),
                        pltpu.VMEM((HEAD_DIM, width), F32)],
        compiler_params=_cparams("arbitrary", "arbitrary"), name="nsa_prompt",
    )(q, gates_t, kcv, slcg, wing, jnp.tile(q_gain, NSA_GROUP).reshape(1, gw), _block_diag_ones(gw), eye,
      fs, fc, ov, ex)


SMP_PP = 16
NSA_KW = NSA_KV * HEAD_DIM
SEL_LANES = 2 * LANES


def _nsa_sample_body(pt_ref, q_ref, gate_ref, ckv_ref, *refs, n_slc, k_sel, past, tq):
    pages = refs[:SMP_PP]
    (slcn_ref, wst_ref, winn_ref, gq_ref, bd_ref, bc_ref, bs_ref, bsn_ref, bw_ref, bwn_ref, ov_ref, ex_ref, exn_ref,
     o_ref, nwin_ref, qbd_scr, sel_scr, kbuf_scr, vbuf_scr, s_scr, sn_scr, vn_scr, oc_scr, ow_scr,
     pad_scr) = refs[SMP_PP:]
    j = pl.program_id(1)
    hd = HEAD_DIM
    kw = NSA_KW
    rows = NSA_HEADS * tq
    init = (jnp.full((rows, 1), NEG, F32), jnp.zeros((rows, 1), F32), jnp.zeros((rows, kw), F32))

    @pl.when(j == 0)
    def _():
        qn = _seg_rms(q_ref[...], bd_ref[...], gq_ref[...]) * (hd ** -0.5)
        zero = jnp.zeros((tq, hd), F32)
        blocks = []
        for h in range(NSA_HEADS):
            qh = qn[:, hd * h:hd * (h + 1)]
            blocks.append(jnp.concatenate([qh if g == h // NSA_GROUP else zero for g in range(NSA_KV)], axis=1))
        qbd = jnp.concatenate(blocks, axis=0).astype(BF16)
        qbd_scr[...] = qbd
        ckv = ckv_ref[0]
        bias_c = bc_ref[...]
        ok_c = bias_c > NEG_TEST
        s_c = _dot_nt(qbd, ckv[:, :kw].astype(BF16)) + bias_c
        m_c = jnp.max(s_c, axis=-1, keepdims=True)
        e_c = jnp.where(ok_c, jnp.exp(s_c - m_c), 0.0)
        p_c = (e_c / jnp.maximum(jnp.sum(e_c, axis=-1, keepdims=True), 1e-30)).astype(BF16)
        oc_scr[...] = _dot(p_c, ckv[:, kw:].astype(BF16))
        d = _dot(p_c, ov_ref[...])
        gr = NSA_GROUP * tq
        imp = jnp.concatenate(
            [sum(d[g * gr + h * tq:g * gr + (h + 1) * tq] for h in range(NSA_GROUP)) for g in range(NSA_KV)], axis=0)
        pos = past + (lax.broadcasted_iota(jnp.int32, (NSA_KV * tq, 1), 0) & (tq - 1))
        sel = _select_blocks(imp, pos, n_slc, k_sel)
        sel = jnp.concatenate([sel[g * tq:(g + 1) * tq] for g in range(NSA_KV) for _ in range(NSA_GROUP)], axis=0)
        sel = ((1.0 - sel) * NEG).astype(BF16)
        sel_scr[...] = sel
        pad_scr[...] = jnp.zeros(pad_scr.shape, F32)
        pad_scr[0:tq, :] = slcn_ref[...]
        kn = pad_scr[...].astype(BF16)
        sn_scr[...] = _dot_nt(qbd, kn[:, :kw]) + bsn_ref[...] + _dot(sel, exn_ref[...])
        vn_scr[...] = kn[:, kw:]
        pad_scr[0:tq, :] = winn_ref[...]
        nwin_ref[0] = _shifted_buffer(wst_ref[0], pad_scr[...], tq)
        wst = wst_ref[0].astype(BF16)
        wn = pad_scr[...].astype(BF16)
        bias_w = bw_ref[...]
        mw = _softmax_piece(_dot(qbd, wst[:kw]) + bias_w, bias_w > NEG_TEST, *init, wst[kw:], v_t=True)
        bias_wn = bwn_ref[...]
        mw = _softmax_piece(_dot_nt(qbd, wn[:, :kw]) + bias_wn, bias_wn > NEG_TEST, *mw, wn[:, kw:])
        ow_scr[...] = mw[2] / jnp.maximum(mw[1], 1e-30)

    for i in range(SMP_PP):
        pg = pages[i][0].astype(BF16)
        kbuf_scr[:, i * PAGE:(i + 1) * PAGE] = pg[:kw]
        vbuf_scr[j, :, i * PAGE:(i + 1) * PAGE] = pg[kw:]
    s_scr[j] = _dot(qbd_scr[...], kbuf_scr[...]) + bs_ref[j] + _dot(sel_scr[...], ex_ref[j])

    @pl.when(j == pl.num_programs(1) - 1)
    def _():
        s_all = s_scr[...]
        s_new = sn_scr[...]
        m = jnp.maximum(jnp.max(jnp.max(s_all, axis=0), axis=-1, keepdims=True),
                        jnp.max(s_new, axis=-1, keepdims=True))
        p_new = jnp.exp(s_new - m)
        den = jnp.sum(p_new, axis=-1, keepdims=True)
        acc = _dot(p_new.astype(BF16), vn_scr[...])
        for st in range(s_all.shape[0]):
            p = jnp.exp(s_all[st] - m)
            den = den + jnp.sum(p, axis=-1, keepdims=True)
            acc = acc + _dot_nt(p.astype(BF16), vbuf_scr[st])
        o_s = acc / jnp.maximum(den, 1e-30)
        o_c = oc_scr[...]
        o_w = ow_scr[...]
        outs = []
        for h in range(NSA_HEADS):
            g, hg = divmod(h, NSA_GROUP)
            gt = gate_ref[g]
            rs = slice(h * tq, (h + 1) * tq)
            ls = slice(hd * g, hd * (g + 1))
            outs.append(gt[:, 3 * hg:3 * hg + 1] * o_c[rs, ls] + gt[:, 3 * hg + 1:3 * hg + 2] * o_s[rs, ls]
                        + gt[:, 3 * hg + 2:3 * hg + 3] * o_w[rs, ls])
        o_ref[...] = jnp.concatenate(outs, axis=1)


def _nsa_sample_tables(rel_bias, past, tq, n_c, w_state):
    rows = NSA_HEADS * tq
    total = past + tq
    n_cmp = total // CMP_STRIDE - CMP_BLOCK // CMP_STRIDE + 1
    n_slc = -(-total // SEL_BLOCK)
    pos = past + jnp.arange(tq)
    c_hi = jnp.arange(n_c) * CMP_STRIDE + CMP_BLOCK - 1
    dist_c = pos[:, None] - c_hi[None, :]
    bc = _bias_tile(rel_bias, dist_c, (dist_c >= 0) & (jnp.arange(n_c) < n_cmp)[None, :]).reshape(rows, n_c)
    n_steps = past // (SMP_PP * PAGE)
    d_ext = jnp.arange(tq + past - 1) + 1
    bs = _toeplitz(_bias_tile(rel_bias, d_ext, d_ext >= 0), tq, past)
    bs = bs.reshape(rows, n_steps, SMP_PP * PAGE).transpose(1, 0, 2)
    jn = jnp.arange(PAGE)
    dist_n = jnp.arange(tq)[:, None] - jn[None, :]
    new_ok = (dist_n >= 0) & (jn < tq)[None, :]
    bsn = _bias_tile(rel_bias, dist_n, new_ok).reshape(rows, PAGE)
    dist_w = jnp.arange(tq)[:, None] + w_state - jnp.arange(w_state)[None, :]
    bw = _bias_tile(rel_bias, dist_w, (dist_w >= 0) & (dist_w <= NSA_WINDOW)).reshape(rows, w_state)
    bwn = _bias_tile(rel_bias, dist_n, new_ok & (dist_n <= NSA_WINDOW)).reshape(rows, PAGE)
    c_lo = np.arange(n_c) * CMP_STRIDE
    s_lo = np.arange(SEL_LANES) * SEL_BLOCK
    ov = ((c_lo[:, None] <= s_lo[None, :] + SEL_BLOCK - 1) & (c_lo[:, None] + CMP_BLOCK - 1 >= s_lo[None, :])
          & (np.arange(n_c) < n_cmp)[:, None] & (np.arange(SEL_LANES) < n_slc)[None, :])
    blk = np.arange(SEL_LANES)
    ex = (blk[:, None] == (np.arange(past)[None, :] // SEL_BLOCK)).reshape(SEL_LANES, n_steps, SMP_PP * PAGE)
    ex = ex.transpose(1, 0, 2)
    exn = (blk[:, None] == ((past + np.arange(PAGE))[None, :] // SEL_BLOCK)) & (np.arange(PAGE) < tq)[None, :]
    as_bf = lambda a: jnp.asarray(a.astype(np.float32), BF16)
    return bc, bs, bsn, bw, bwn, as_bf(ov), as_bf(ex), as_bf(exn), n_slc


def _nsa_sample(q, gates, crows, slc_pool_t, table, slc_new, win_state_t, win_new, q_gain, tables, *, tq):
    bc, bs, bsn, bw, bwn, ov, ex, exn, n_slc = tables
    bsz, npg = table.shape
    past = npg * PAGE
    n_c = crows.shape[1]
    w_state = win_state_t.shape[2]
    rows = NSA_HEADS * tq
    qw = NSA_HEADS * HEAD_DIM
    page = lambda i: pl.BlockSpec((1, ROWS_NSA, PAGE), lambda b, j, pt: (pt[b, j * SMP_PP + i], 0, 0))
    per_b = lambda shape: pl.BlockSpec((1,) + shape, lambda b, j, pt: (b,) + (0,) * len(shape))
    new_rows = pl.BlockSpec((tq, ROWS_NSA), lambda b, j, pt: (b, 0))
    return pl.pallas_call(
        functools.partial(_nsa_sample_body, n_slc=n_slc, k_sel=min(SEL_TOPK, n_slc), past=past, tq=tq),
        grid_spec=pltpu.PrefetchScalarGridSpec(
            num_scalar_prefetch=1, grid=(bsz, npg // SMP_PP),
            in_specs=[pl.BlockSpec((tq, qw), lambda b, j, pt: (b, 0)),
                      pl.BlockSpec((NSA_KV, tq, LANES), lambda b, j, pt: (0, b, 0)),
                      per_b((n_c, ROWS_NSA))] + [page(i) for i in range(SMP_PP)] + [
                      new_rows, per_b((ROWS_NSA, w_state)), new_rows,
                      _const((1, qw)), _const(_block_diag_ones(qw).shape), _const(bc.shape), _const(bs.shape), _const(bsn.shape),
                      _const(bw.shape), _const(bwn.shape), _const(ov.shape), _const(ex.shape), _const(exn.shape)],
            out_specs=[pl.BlockSpec((tq, qw), lambda b, j, pt: (b, 0)), per_b((ROWS_NSA, w_state))],
            scratch_shapes=[pltpu.VMEM((rows, NSA_KW), BF16), pltpu.VMEM((rows, SEL_LANES), BF16),
                            pltpu.VMEM((NSA_KW, SMP_PP * PAGE), BF16),
                            pltpu.VMEM((npg // SMP_PP, NSA_KW, SMP_PP * PAGE), BF16),
                            pltpu.VMEM((npg // SMP_PP, rows, SMP_PP * PAGE), F32),
                            pltpu.VMEM((rows, PAGE), F32), pltpu.VMEM((PAGE, NSA_KW), BF16),
                            pltpu.VMEM((rows, NSA_KW), F32), pltpu.VMEM((rows, NSA_KW), F32),
                            pltpu.VMEM((PAGE, ROWS_NSA), F32)]),
        out_shape=[jax.ShapeDtypeStruct((bsz * tq, qw), F32), jax.ShapeDtypeStruct(win_state_t.shape, F32)],
        compiler_params=_cparams("parallel", "arbitrary"), name="nsa_sample",
    )(table, q, gates, crows, *([slc_pool_t] * SMP_PP), slc_new, win_state_t, win_new,
      jnp.tile(q_gain, NSA_HEADS).reshape(1, qw), _block_diag_ones(qw), bc, bs, bsn, bw, bwn, ov, ex, exn)


DIL_TQ = 128


def _head_of_lane(shape):
    return lax.broadcasted_iota(jnp.int32, shape, 1) >> 6


def _stack_heads(qt, lane_head):
    return jnp.concatenate([jnp.where(lane_head == h, qt, 0.0) for h in range(DIL_HEADS)], axis=0).astype(BF16)


def _unstack_heads(r, lane_head, t):
    return sum(jnp.where(lane_head == h, r[h * t:(h + 1) * t], 0.0) for h in range(DIL_HEADS))


def _dil_prompt_body(q_ref, kv_ref, bias_ref, o_ref, lse_ref, *, n_sub, dil):
    tq = DIL_TQ
    gw = ODD_GW
    lane_head = _head_of_lane((tq, gw))
    n_t = n_sub // tq

    def tile(t, r):
        ql = slice(gw * r, gw * (r + 1))
        kl = slice(2 * gw * r, 2 * gw * r + gw)
        vl = slice(2 * gw * r + gw, 2 * gw * (r + 1))
        cur_rows = slice(t * tq, (t + 1) * tq)
        prev_rows = slice(max(t - 1, 0) * tq, (max(t - 1, 0) + 1) * tq)
        kk = jnp.concatenate([kv_ref[prev_rows, kl], kv_ref[cur_rows, kl]], axis=0).astype(BF16)
        vv = jnp.concatenate([kv_ref[prev_rows, vl], kv_ref[cur_rows, vl]], axis=0).astype(BF16)
        bias = bias_ref[min(t, 1)]
        s = _dot_nt(_stack_heads(q_ref[cur_rows, ql], lane_head), kk) + bias
        m = jnp.max(s, axis=-1, keepdims=True)
        e = jnp.where(bias > NEG_TEST, jnp.exp(s - m), 0.0)
        l = jnp.maximum(jnp.sum(e, axis=-1, keepdims=True), 1e-30)
        res = _dot((e / l).astype(BF16), vv)
        lse = jnp.broadcast_to(m + jnp.log(l), (DIL_HEADS * tq, gw))
        o_ref[cur_rows, ql] = _unstack_heads(res, lane_head, tq)
        lse_ref[cur_rows, ql] = _unstack_heads(lse, lane_head, tq)

    for r in range(dil):
        for t in range(n_t):
            tile(t, r)


def _merge_groups(outs, lses):
    mx = functools.reduce(jnp.maximum, lses)
    ws = [jnp.exp(l - mx) for l in lses]
    return sum(w * o for w, o in zip(ws, outs)) / sum(ws)


def _dil_prompt_tables(rel_bias):
    tq = DIL_TQ
    m_ext = jnp.arange(3 * tq - 1) - (tq - 1)
    cur_tile = jnp.arange(2 * tq) >= tq
    tiles = []
    for gi, (win, dil) in enumerate(DIL_PATTERN):
        assert win // dil == tq
        t = _toeplitz(_bias_tile(rel_bias, m_ext * dil, (m_ext >= 0) & (m_ext <= tq)), tq, 2 * tq)
        t = t[gi * DIL_HEADS:(gi + 1) * DIL_HEADS].reshape(DIL_HEADS * tq, 2 * tq)
        tiles.append(jnp.stack([jnp.where(cur_tile, t, NEG), t]))
    return tiles


def _dil_prompt(q_views, kv_views, tables, *, batch, seq):
    gw = ODD_GW
    outs, lses = [], []
    for gi, (_, dil) in enumerate(DIL_PATTERN):
        n_sub = seq // dil
        assert n_sub % DIL_TQ == 0
        qv, kvv = q_views[gi], kv_views[gi]
        o_spec = pl.BlockSpec((n_sub, dil * gw), lambda b: (b, 0))
        o, lse = pl.pallas_call(
            functools.partial(_dil_prompt_body, n_sub=n_sub, dil=dil), grid=(batch,),
            in_specs=[o_spec, pl.BlockSpec((n_sub, dil * 2 * gw), lambda b: (b, 0)), _const(tables[gi].shape)],
            out_specs=[o_spec, o_spec],
            out_shape=[jax.ShapeDtypeStruct(qv.shape, F32)] * 2,
            compiler_params=_cparams("parallel"), name=f"dilated_prompt_g{gi}",
        )(qv, kvv, tables[gi])
        outs.append(o)
        lses.append(lse)
    return outs, lses


def _shifted_buffer(st, new_rows_padded, tq):
    ln = st.shape[1]
    shifted = pltpu.roll(st, ln - tq, 1)
    new_t = pltpu.roll(new_rows_padded.T, LANES - tq, 1)
    lane = lax.broadcasted_iota(jnp.int32, (st.shape[0], LANES), 1)
    tail = jnp.where(lane >= LANES - tq, new_t, shifted[:, ln - LANES:])
    return tail if ln == LANES else jnp.concatenate([shifted[:, :ln - LANES], tail], axis=1)


def _dil_sample_body(q_ref, kvn_ref, st0_ref, st1_ref, st2_ref, b0_ref, b1_ref, b2_ref, bn_ref, o_ref, lse_ref,
                     n0_ref, n1_ref, n2_ref, pad_scr, *, tq):
    gw = ODD_GW
    lane_head = _head_of_lane((tq, gw))
    rows = DIL_HEADS * tq
    pad_scr[...] = jnp.zeros(pad_scr.shape, F32)
    for gi, (st_ref, b_ref, new_ref) in enumerate(((st0_ref, b0_ref, n0_ref), (st1_ref, b1_ref, n1_ref),
                                                    (st2_ref, b2_ref, n2_ref))):
        qs = _stack_heads(q_ref[gi], lane_head)
        pad_scr[0:tq, :] = kvn_ref[gi]
        new_ref[0] = _shifted_buffer(st_ref[0], pad_scr[...], tq)
        st = st_ref[0].astype(BF16)
        kn = pad_scr[...].astype(BF16)
        bias = b_ref[...]
        acc = (jnp.full((rows, 1), NEG, F32), jnp.zeros((rows, 1), F32), jnp.zeros((rows, gw), F32))
        acc = _softmax_piece(_dot(qs, st[:gw]) + bias, bias > NEG_TEST, *acc, st[gw:], v_t=True)
        bias_n = bn_ref[gi]
        m, l, a = _softmax_piece(_dot_nt(qs, kn[:, :gw]) + bias_n, bias_n > NEG_TEST, *acc, kn[:, gw:])
        l = jnp.maximum(l, 1e-30)
        o_ref[gi] = _unstack_heads(a / l, lane_head, tq)
        lse_ref[gi] = _unstack_heads(jnp.broadcast_to(m + jnp.log(l), (rows, gw)), lane_head, tq)


def _dil_sample_tables(rel_bias, tq, state_lens):
    rows = DIL_HEADS * tq
    i = jnp.arange(tq)[:, None]
    per_state, per_new = [], []
    jn = jnp.arange(PAGE)[None, :]
    for gi, ((win, dil), ln) in enumerate(zip(DIL_PATTERN, state_lens)):
        hs = slice(gi * DIL_HEADS, (gi + 1) * DIL_HEADS)
        d = i + ln - jnp.arange(ln)[None, :]
        per_state.append(_bias_tile(rel_bias, d, (d % dil == 0) & (d <= win))[hs].reshape(rows, ln))
        dn = i - jn
        per_new.append(_bias_tile(rel_bias, dn, (dn >= 0) & (dn % dil == 0) & (dn <= win) & (jn < tq))[hs]
                       .reshape(rows, PAGE))
    return per_state, jnp.stack(per_new)


def _dil_sample(q3, kv3, states, tables, *, tq):
    per_state, bias_new = tables
    bsz = states[0].shape[0]
    gw = ODD_GW
    st_spec = lambda s: pl.BlockSpec((1,) + s.shape[1:], lambda b: (b, 0, 0))
    grp = lambda w: pl.BlockSpec((3, tq, w), lambda b: (0, b, 0))
    return pl.pallas_call(
        functools.partial(_dil_sample_body, tq=tq), grid=(bsz,),
        in_specs=[grp(gw), grp(2 * gw)] + [st_spec(s) for s in states] + [_const(t.shape) for t in per_state]
        + [_const(bias_new.shape)],
        out_specs=[grp(gw), grp(gw)] + [st_spec(s) for s in states],
        out_shape=[jax.ShapeDtypeStruct((3, bsz * tq, gw), F32)] * 2
        + [jax.ShapeDtypeStruct(s.shape, F32) for s in states],
        scratch_shapes=[pltpu.VMEM((PAGE, 2 * gw), F32)],
        compiler_params=_cparams("parallel"), name="dilated_sample",
    )(q3, kv3, *states, *per_state, bias_new)


def _pad_even_w(w_in):
    return jnp.concatenate([w_in, jnp.zeros((w_in.shape[0], EVEN_PAD - w_in.shape[1]), w_in.dtype)], axis=1).astype(BF16)


def _even_mixer_prompt(h, gain, w_in, w_out, lb, hg_norm, qk_norm, cmp_pe, cmp_w1, cmp_w2, rel_bias, *, batch, seq, tm):
    n = batch * seq
    hg, q, cmp_rows, slcg, wing, gates_t, cmp_t, slc_t, win_t = _inproj_even(
        h, gain, _pad_even_w(w_in), qk_norm, tm=tm, seq=seq)
    s0 = jnp.zeros((batch, HG_HEADS, HEAD_DIM, HEAD_DIM), F32)
    o_h, s_fin = _hgrn(hg, lb, hg_norm, s0, batch=batch, seq=seq)
    cw = _compress_weights(cmp_pe, cmp_w1, cmp_w2, qk_norm[1])
    npg = seq // PAGE
    table = jnp.arange(batch * npg, dtype=jnp.int32).reshape(batch, npg)
    _, kcv = _compress(cmp_rows.reshape(n // PAGE, PAGE, ROWS_NSA), table, cw, transposed=False)
    tables = _nsa_prompt_t_tables(rel_bias, seq, kcv.shape[2])
    o_nt = _nsa_prompt_t(q, gates_t, kcv, slcg, wing, qk_norm[0], tables, batch=batch, seq=seq)
    h = _outproj2t(h, o_h, o_nt, w_out.astype(BF16), tm=tm)
    keep = min(NSA_WINDOW, seq)
    return h, {"hgrn": s_fin, "cmp": _rows_from_t(cmp_t, NSA_KV), "slc": _rows_from_t(slc_t, NSA_KV),
               "win": _rows_from_t(win_t[:, :, seq - keep:], NSA_KV)}


def _rows_from_t(a, heads):
    return jnp.moveaxis(a.reshape(a.shape[0], 2, heads, HEAD_DIM, a.shape[2]), -1, 1)


def _even_mixer_sample(h, gain, w_in, w_out, lb, hg_norm, qk_norm, cmp_pe, cmp_w1, cmp_w2, rel_bias,
                       state_hgrn, cmp_pool, slc_pool, win_state, page_table, *, batch, tq, tm):
    assert tq < CMP_STRIDE and tq & (tq - 1) == 0
    hg, q, cmp_rows, slc_rows, win_rows, gates = _inproj_even(h, gain, _pad_even_w(w_in), qk_norm, tm=tm)
    o_h, s_fin = _hgrn(hg, lb, hg_norm, state_hgrn, batch=batch, seq=tq)
    cw = _compress_weights(cmp_pe, cmp_w1, cmp_w2, qk_norm[1])
    rows_t = lambda a: jnp.moveaxis(a, 1, -1).reshape(a.shape[0], ROWS_NSA, a.shape[1])
    crows, _ = _compress(rows_t(cmp_pool), page_table, cw, transposed=True)
    past = page_table.shape[1] * PAGE
    w_state = win_state.shape[1]
    tables = _nsa_sample_tables(rel_bias, past, tq, crows.shape[1], w_state)
    o_n, new_win_t = _nsa_sample(q, gates, crows, rows_t(slc_pool), page_table, slc_rows, rows_t(win_state), win_rows,
                                 qk_norm[0], tables, tq=tq)
    h = _outproj2(h, o_h, o_n, w_out.astype(BF16), tm=tm)
    new_win = jnp.moveaxis(new_win_t.reshape((batch,) + win_state.shape[2:] + (w_state,)), -1, 1)
    return h, {"hgrn": s_fin, "cmp": cmp_rows, "slc": slc_rows, "win": new_win}


def _odd_mixer_prompt(h, gain, w_in, w_out, qk_norm, rel_bias, *, batch, seq, tm):
    qv0, kvv0, qv1, kvv1, qv2, kvv2, *kv_t = _inproj_odd(h, gain, w_in.astype(BF16), qk_norm, tm=tm, seq=seq)
    outs, lses = _dil_prompt([qv0, qv1, qv2], [kvv0, kvv1, kvv2], _dil_prompt_tables(rel_bias),
                             batch=batch, seq=seq)
    bufs = [_rows_from_t(t[:, :, seq - min(w, seq):], DIL_HEADS) for t, (w, _) in zip(kv_t, DIL_PATTERN)]
    return _outproj_merge(h, outs, lses, w_out.astype(BF16), tm=tm, views=True), bufs


def _odd_mixer_sample(h, gain, w_in, w_out, qk_norm, rel_bias, states, *, batch, tq, tm):
    q3, kv3 = _inproj_odd(h, gain, w_in.astype(BF16), qk_norm, tm=tm)
    tables = _dil_sample_tables(rel_bias, tq, [s.shape[1] for s in states])
    states_t = [jnp.moveaxis(s, 1, -1).reshape(batch, 2 * ODD_GW, s.shape[1]) for s in states]
    o3, lse3, *new_t = _dil_sample(q3, kv3, states_t, tables, tq=tq)
    new = [jnp.moveaxis(n.reshape(s.shape[:1] + s.shape[2:] + s.shape[1:2]), -1, 1) for n, s in zip(new_t, states)]
    h = _outproj_merge(h, [o3[g] for g in range(3)], [lse3[g] for g in range(3)], w_out.astype(BF16), tm=tm)
    return h, new


TM_PROMPT = 512
TM_SAMPLE = 256


def kernel(x_prompt, x_sample, p_prompt, p_sample, state_hgrn, cache_nsa_cmp_kv, cache_nsa_slc_kv, state_nsa_win_kv,
           state_dil_kv_0, state_dil_kv_1, state_dil_kv_2, page_table, rel_bias, norm_ffn1, norm_mix, norm_ffn2,
           norm_ple, w_ffn1_in, w_ffn1_out, w_ffn2_in, w_ffn2_out, w_ple_gate, w_ple_proj, w_in_even, w_out_even,
           hgrn_lb_logits, hgrn_norm, nsa_qk_norm, nsa_cmp_pe, nsa_cmp_w1, nsa_cmp_w2, w_in_odd, w_out_odd,
           dil_qk_norm):
    depth = norm_ffn1.shape[0]
    dil_states = (state_dil_kv_0, state_dil_kv_1, state_dil_kv_2)
    bf = lambda w: w.astype(BF16)
    w1i, w1o, w2i, w2o, wpg, wpp = (bf(w) for w in (w_ffn1_in, w_ffn1_out, w_ffn2_in, w_ffn2_out, w_ple_gate,
                                                     w_ple_proj))
    lb_all = jnp.cumsum(jax.nn.softmax(hgrn_lb_logits.astype(F32), axis=0), axis=0)
    g_ffn1, g_ffn2, g_ple = (g.reshape(depth, 1, -1) for g in (norm_ffn1, norm_ffn2, norm_ple))

    def run(x, p, sample):
        batch, seq, d = x.shape
        n = batch * seq
        tm = TM_SAMPLE if sample else TM_PROMPT
        h = x.reshape(n, d)
        p = p.reshape(depth, n, p.shape[-1])
        even, odd = [], []
        for li in range(depth):
            h = _ffn(h, li, g_ffn1, w1i, w1o, tm=tm)
            if li % 2 == 0:
                ei = li // 2
                args = (h, norm_mix[li], w_in_even[ei], w_out_even[ei], lb_all[ei], hgrn_norm[ei], nsa_qk_norm[ei],
                        nsa_cmp_pe[ei], nsa_cmp_w1[ei], nsa_cmp_w2[ei], rel_bias)
                if sample:
                    h, st = _even_mixer_sample(*args, state_hgrn[ei], cache_nsa_cmp_kv[ei], cache_nsa_slc_kv[ei],
                                               state_nsa_win_kv[ei], page_table, batch=batch, tq=seq, tm=tm)
                else:
                    h, st = _even_mixer_prompt(*args, batch=batch, seq=seq, tm=tm)
                even.append(st)
            else:
                oi = li // 2
                args = (h, norm_mix[li], w_in_odd[oi], w_out_odd[oi], dil_qk_norm[oi], rel_bias)
                if sample:
                    h, bufs = _odd_mixer_sample(*args, [s[oi] for s in dil_states], batch=batch, tq=seq, tm=tm)
                else:
                    h, bufs = _odd_mixer_prompt(*args, batch=batch, seq=seq, tm=tm)
                odd.append(bufs)
            h = _ffn(h, li, g_ffn2, w2i, w2o, ple=(p, g_ple, wpg, wpp), tm=tm)
        rows = lambda a: a.reshape(batch, -1, 2, NSA_KV, HEAD_DIM)
        drows = lambda a: a.reshape(batch, -1, 2, DIL_HEADS, HEAD_DIM)
        return (h.reshape(batch, seq, d),
                jnp.stack([s["hgrn"] for s in even]), jnp.stack([rows(s["cmp"]) for s in even]),
                jnp.stack([rows(s["slc"]) for s in even]), jnp.stack([rows(s["win"]) for s in even]),
                jnp.stack([drows(b[0]) for b in odd]), jnp.stack([drows(b[1]) for b in odd]),
                jnp.stack([drows(b[2]) for b in odd]))

    y_p, hg_p, cmp_p, slc_p, win_p, d0_p, d1_p, d2_p = run(x_prompt, p_prompt, False)
    y_s, hg_s, cmp_s, slc_s, win_s, d0_s, d1_s, d2_s = run(x_sample, p_sample, True)
    return (y_p, y_s, hg_p, hg_s, cmp_p, cmp_s, slc_p, slc_s, win_p, win_s, d0_p, d0_s, d1_p, d1_s, d2_p, d2_s)
```

```python
import functools
import math

import numpy as np
import jax
import jax.numpy as jnp
from jax import lax
from jax.experimental import pallas as pl
from jax.experimental.pallas import tpu as pltpu

F32 = jnp.float32
BF16 = jnp.bfloat16
HIGHEST = lax.Precision.HIGHEST

V7X_VMEM_BYTES = 64 * 1024 * 1024
VMEM_LIMIT = V7X_VMEM_BYTES * 7 // 8
LANES = 128

EPS = 1e-6
NEG = -1e30
NEG_TEST = -1e29

HEAD_DIM = 64
HG_HEADS = 8
HG_CHUNK = 128
NSA_HEADS = 12
NSA_KV = 3
NSA_GROUP = NSA_HEADS // NSA_KV
CMP_BLOCK = 32
CMP_STRIDE = 16
SEL_BLOCK = 64
SEL_TOPK = 16
NSA_WINDOW = 512
FORCE_SCORE = 1e4
PAGE = 128
DIL_PATTERN = ((128, 1), (512, 4), (2048, 16))
DIL_HEADS = 4
RP_BUCKETS = 32
RP_MAX_DIST = 2048


def _cparams(*sem):
    return pltpu.CompilerParams(dimension_semantics=sem, vmem_limit_bytes=VMEM_LIMIT)


def _const(shape):
    nd = len(shape)
    return pl.BlockSpec(shape, lambda *_: (0,) * nd, pipeline_mode=pl.Buffered(1))


def _dot(a, b):
    return jnp.dot(a, b, preferred_element_type=F32)


def _dot_nt(a, b):
    return lax.dot_general(a, b, (((1,), (1,)), ((), ())), preferred_element_type=F32)


def _dot_tn(a, b):
    return lax.dot_general(a, b, (((0,), (0,)), ((), ())), preferred_element_type=F32)


def _rms(x, gain):
    return x * lax.rsqrt(jnp.mean(x * x, axis=-1, keepdims=True) + EPS) * gain


def _seg_ms(x, bd):
    sq = x * x
    hi = sq.astype(BF16)
    lo = (sq - hi.astype(F32)).astype(BF16)
    bw = bd.shape[0]
    cols = []
    for c0 in range(0, x.shape[1], bw):
        wd = min(bw, x.shape[1] - c0)
        blk = bd[:wd, :wd]
        cols.append(_dot(hi[:, c0:c0 + wd], blk) + _dot(lo[:, c0:c0 + wd], blk))
    return (cols[0] if len(cols) == 1 else jnp.concatenate(cols, axis=1)) * (1.0 / HEAD_DIM)


def _seg_rms(x, bd, gain):
    return x * lax.rsqrt(_seg_ms(x, bd) + EPS) * gain


def _norm_k_rows(x, bd, gain, kmask):
    bw = bd.shape[0]
    r = lax.rsqrt(_seg_ms(x[:, :bw], bd) + EPS)
    r = jnp.concatenate([r, jnp.ones((x.shape[0], x.shape[1] - bw), F32)], axis=1)
    return x * jnp.where(kmask > 0.5, r, 1.0) * gain


V7X_MXU_WIDTH = 256


def _block_diag_ones(width):
    i = np.arange(min(width, V7X_MXU_WIDTH)) // HEAD_DIM
    return jnp.asarray((i[:, None] == i[None, :]).astype(np.float32), BF16)


def _rel_bucket(dist):
    exact = RP_BUCKETS // 2
    d = jnp.maximum(dist, 0)
    log_ratio = jnp.log(jnp.maximum(d, 1).astype(F32) / exact) / math.log(RP_MAX_DIST / exact)
    large = jnp.minimum(exact + (log_ratio * (RP_BUCKETS - exact)).astype(jnp.int32), RP_BUCKETS - 1)
    return jnp.where(d < exact, d, large)


def _bias_tile(rel_bias, dist, valid):
    onehot = (_rel_bucket(dist)[..., None] == jnp.arange(RP_BUCKETS)).astype(F32)
    t = jnp.einsum('...k,kh->h...', onehot, rel_bias.astype(F32), precision=HIGHEST)
    return jnp.where(valid[None], t, NEG)


def _toeplitz(ext, n, m):
    p = n + m - 1
    u = jnp.concatenate([jnp.flip(ext[..., :m], -1), jnp.flip(ext[..., m:], -1)], axis=-1)
    reps = (1,) * (ext.ndim - 1) + (n,)
    t = jnp.tile(u, reps)[..., :n * (p - 1)].reshape(ext.shape[:-1] + (n, p - 1))
    return t[..., :m]


def _softmax_piece(s, mask, m_prev, l_prev, acc_prev, v, v_t=False):
    m_new = jnp.maximum(m_prev, jnp.max(s, axis=-1, keepdims=True))
    alpha = jnp.exp(m_prev - m_new)
    p = jnp.where(mask, jnp.exp(s - m_new), 0.0)
    l_new = alpha * l_prev + jnp.sum(p, axis=-1, keepdims=True)
    pb = p.astype(BF16)
    acc_new = alpha * acc_prev + (_dot_nt(pb, v) if v_t else _dot(pb, v))
    return m_new, l_new, acc_new


def _ffn_body(x_ref, g_ref, win_ref, wout_ref, *rest, dff, ple):
    x = x_ref[...]
    xn = _rms(x, g_ref[...]).astype(BF16)
    gu = _dot(xn, win_ref[...])
    gg = gu[:, :dff]
    a = (gg * jax.nn.sigmoid(gg) * gu[:, dff:]).astype(BF16)
    h = x + 0.5 * _dot(a, wout_ref[...])
    if ple:
        p_ref, gp_ref, wg_ref, wp_ref, o_ref = rest
        hn = _rms(h, gp_ref[...]).astype(BF16)
        gate = jax.nn.sigmoid(_dot(hn, wg_ref[...]))
        h = h + gate * _dot(p_ref[...].astype(BF16), wp_ref[...])
    else:
        (o_ref,) = rest
    o_ref[...] = h


def _layer(arr, li):
    tail = (0,) * (arr.ndim - 1)
    return pl.BlockSpec((None,) + arr.shape[1:], lambda *_: (li,) + tail, pipeline_mode=pl.Buffered(1))


def _ffn(h, li, gain, w_in, w_out, ple=None, *, tm):
    n, d = h.shape
    dff = w_out.shape[1]
    row = lambda i: (i, 0)
    in_specs = [pl.BlockSpec((tm, d), row), _layer(gain, li), _layer(w_in, li), _layer(w_out, li)]
    args = [h, gain, w_in, w_out]
    if ple is not None:
        p, gp, wg, wp = ple
        in_specs += [pl.BlockSpec((None, tm, p.shape[2]), lambda i: (li, i, 0)), _layer(gp, li), _layer(wg, li),
                     _layer(wp, li)]
        args += [p, gp, wg, wp]
    return pl.pallas_call(
        functools.partial(_ffn_body, dff=dff, ple=ple is not None),
        grid=(n // tm,), in_specs=in_specs, out_specs=pl.BlockSpec((tm, d), row),
        out_shape=jax.ShapeDtypeStruct((n, d), F32), compiler_params=_cparams("parallel"),
        name="ffn_ple" if ple is not None else "ffn",
    )(*args)


EVEN_HG = 4 * HG_HEADS * HEAD_DIM
EVEN_Q = NSA_HEADS * HEAD_DIM
ROWS_NSA = 2 * NSA_KV * HEAD_DIM
EVEN_PAD = EVEN_HG + EVEN_Q + 3 * ROWS_NSA + LANES


GATE_ROWS = 16


def _inproj_even_body(x_ref, g_ref, w_ref, gk_ref, km_ref, bd_ref, hg_ref, q_ref, cmp_ref, *outs, prompt):
    xn = _rms(x_ref[...], g_ref[...]).astype(BF16)
    p = _dot(xn, w_ref[...])
    c0 = EVEN_HG
    c1 = c0 + EVEN_Q
    hg_ref[...] = p[:, :c0]
    q_ref[...] = p[:, c0:c1]
    cmp = p[:, c1:c1 + ROWS_NSA]
    cmp_ref[...] = cmp
    bd = bd_ref[...]
    km = km_ref[...]
    slc = _norm_k_rows(p[:, c1 + ROWS_NSA:c1 + 2 * ROWS_NSA], bd, gk_ref[0:1, :], km)
    win = _norm_k_rows(p[:, c1 + 2 * ROWS_NSA:c1 + 3 * ROWS_NSA], bd, gk_ref[1:2, :], km)
    sig = jax.nn.sigmoid(p[:, c1 + 3 * ROWS_NSA:])
    if not prompt:
        slc_ref, win_ref, gate_ref = outs
        slc_ref[...] = slc
        win_ref[...] = win
        for g in range(NSA_KV):
            gate_ref[g] = sig if g == 0 else pltpu.roll(sig, LANES - 3 * NSA_GROUP * g, 1)
        return
    slcg_ref, wing_ref, gate_t_ref, cmp_t_ref, slc_t_ref, win_t_ref = outs
    kw = NSA_KV * HEAD_DIM
    sig_t = sig.T
    for g in range(NSA_KV):
        ks = slice(HEAD_DIM * g, HEAD_DIM * (g + 1))
        vs = slice(kw + HEAD_DIM * g, kw + HEAD_DIM * (g + 1))
        slcg_ref[g] = jnp.concatenate([slc[:, ks], slc[:, vs]], axis=1)
        wing_ref[g] = jnp.concatenate([win[:, ks], win[:, vs]], axis=1)
        for half in range(sig.shape[0] // NSA_TQ):
            gate_t_ref[g, half] = sig_t[3 * NSA_GROUP * g:3 * NSA_GROUP * g + GATE_ROWS,
                                        half * NSA_TQ:(half + 1) * NSA_TQ]
    for dst_ref, rows in ((cmp_t_ref, cmp), (slc_t_ref, slc), (win_t_ref, win)):
        _store_transposed(dst_ref, rows)


def _store_transposed(dst_ref, x):
    for c in range(x.shape[1] // LANES):
        dst_ref[0, c * LANES:(c + 1) * LANES, :] = x[:, c * LANES:(c + 1) * LANES].T


def _inproj_even(h, gain, w_pad, qk_norm, *, tm, seq=None):
    n, d = h.shape
    kw = NSA_KV * HEAD_DIM
    ones = jnp.ones((kw,), F32)
    gk = jnp.stack([jnp.concatenate([jnp.tile(qk_norm[2], NSA_KV), ones]),
                    jnp.concatenate([jnp.tile(qk_norm[3], NSA_KV), ones])])
    km = jnp.concatenate([ones, 0.0 * ones]).reshape(1, ROWS_NSA)
    row = lambda i: (i, 0)
    grp = lambda i: (0, i, 0)
    shp = lambda *s: jax.ShapeDtypeStruct(s, F32)
    out_specs = [pl.BlockSpec((tm, EVEN_HG), row), pl.BlockSpec((tm, EVEN_Q), row), pl.BlockSpec((tm, ROWS_NSA), row)]
    out_shape = [shp(n, EVEN_HG), shp(n, EVEN_Q), shp(n, ROWS_NSA)]
    if seq is None:
        out_specs += [pl.BlockSpec((tm, ROWS_NSA), row)] * 2 + [pl.BlockSpec((NSA_KV, tm, LANES), grp)]
        out_shape += [shp(n, ROWS_NSA)] * 2 + [shp(NSA_KV, n, LANES)]
    else:
        tpb = seq // tm
        out_specs += [pl.BlockSpec((NSA_KV, tm, LANES), grp)] * 2
        out_shape += [shp(NSA_KV, n, LANES)] * 2
        out_specs.append(pl.BlockSpec((NSA_KV, tm // NSA_TQ, GATE_ROWS, NSA_TQ), lambda i: (0, i, 0, 0)))
        out_shape.append(shp(NSA_KV, n // NSA_TQ, GATE_ROWS, NSA_TQ))
        out_specs += [pl.BlockSpec((1, ROWS_NSA, tm), lambda i: (i // tpb, 0, i % tpb))] * 3
        out_shape += [shp(n // seq, ROWS_NSA, seq)] * 3
    return pl.pallas_call(
        functools.partial(_inproj_even_body, prompt=seq is not None), grid=(n // tm,),
        in_specs=[pl.BlockSpec((tm, d), row), _const((1, d)), _const(w_pad.shape), _const((2, ROWS_NSA)),
                  _const((1, ROWS_NSA)), _const(_block_diag_ones(ROWS_NSA).shape)],
        out_specs=out_specs, out_shape=out_shape,
        compiler_params=_cparams("parallel"), name="inproj_even",
    )(h, gain.reshape(1, d), w_pad, gk, km, _block_diag_ones(ROWS_NSA))


ODD_W = 3 * DIL_HEADS * HEAD_DIM
ODD_GW = DIL_HEADS * HEAD_DIM


def _to_residue_view(src, dst_ref, stage_ref, dil):
    rows, w = src.shape
    for blk in range(w // LANES):
        stage_ref[...] = src[:, blk * LANES:(blk + 1) * LANES]
        for r in range(dil):
            lo = r * w + blk * LANES
            dst_ref[:, lo:lo + LANES] = stage_ref[pl.ds(r, rows // dil, stride=dil), :]


def _from_residue_view(src_ref, stage_ref, dil, w):
    rows = src_ref.shape[0] * dil
    parts = []
    for blk in range(w // LANES):
        for r in range(dil):
            lo = r * w + blk * LANES
            stage_ref[pl.ds(r, rows // dil, stride=dil), :] = src_ref[:, lo:lo + LANES]
        parts.append(stage_ref[...])
    return jnp.concatenate(parts, axis=1)


def _inproj_odd_body(x_ref, g_ref, w_ref, gq_ref, gk_ref, bd_ref, *outs, prompt):
    xn = _rms(x_ref[...], g_ref[...]).astype(BF16)
    p = _dot(xn, w_ref[...])
    bd = bd_ref[...]
    q = _seg_rms(p[:, :ODD_W], bd, gq_ref[...]) * (HEAD_DIM ** -0.5)
    k = _seg_rms(p[:, ODD_W:2 * ODD_W], bd, gk_ref[...])
    v = p[:, 2 * ODD_W:]
    for g in range(3):
        s = slice(ODD_GW * g, ODD_GW * (g + 1))
        kv = jnp.concatenate([k[:, s], v[:, s]], axis=1)
        if not prompt:
            outs[0][g] = q[:, s]
            outs[1][g] = kv
            continue
        dil = DIL_PATTERN[g][1]
        if dil == 1:
            outs[2 * g][...] = q[:, s]
            outs[2 * g + 1][...] = kv
        else:
            _to_residue_view(q[:, s], outs[2 * g], outs[-1], dil)
            _to_residue_view(kv, outs[2 * g + 1], outs[-1], dil)
        _store_transposed(outs[6 + g], kv)


def _inproj_odd(h, gain, w, qk_norm, *, tm, seq=None):
    n, d = h.shape
    row = lambda i: (i, 0)
    nh = 3 * DIL_HEADS
    prompt = seq is not None
    if not prompt:
        out_specs = [pl.BlockSpec((3, tm, ODD_GW), lambda i: (0, i, 0)),
                     pl.BlockSpec((3, tm, 2 * ODD_GW), lambda i: (0, i, 0))]
        out_shape = [jax.ShapeDtypeStruct((3, n, ODD_GW), F32), jax.ShapeDtypeStruct((3, n, 2 * ODD_GW), F32)]
    else:
        out_specs, out_shape = [], []
        for _, dil in DIL_PATTERN:
            assert dil == 1 or DIL_PATTERN[0][1] == 1
            for width in (ODD_GW, 2 * ODD_GW):
                out_specs.append(pl.BlockSpec((tm // dil, dil * width), row))
                out_shape.append(jax.ShapeDtypeStruct((n // dil, dil * width), F32))
        tpb = seq // tm
        for _ in DIL_PATTERN:
            out_specs.append(pl.BlockSpec((1, 2 * ODD_GW, tm), lambda i: (i // tpb, 0, i % tpb)))
            out_shape.append(jax.ShapeDtypeStruct((n // seq, 2 * ODD_GW, seq), F32))
    return pl.pallas_call(
        functools.partial(_inproj_odd_body, prompt=prompt), grid=(n // tm,),
        in_specs=[pl.BlockSpec((tm, d), row), _const((1, d)), _const(w.shape), _const((1, ODD_W)),
                  _const((1, ODD_W)), _const(_block_diag_ones(ODD_W).shape)],
        out_specs=out_specs, out_shape=out_shape,
        scratch_shapes=[pltpu.VMEM((tm, LANES), F32)] if prompt else [],
        compiler_params=_cparams("parallel"), name="inproj_odd",
    )(h, gain.reshape(1, d), w, jnp.tile(qk_norm[0], nh).reshape(1, ODD_W),
      jnp.tile(qk_norm[1], nh).reshape(1, ODD_W), _block_diag_ones(ODD_W))


def _outproj2_body(h_ref, a_ref, b_ref, w_ref, o_ref):
    ka = a_ref.shape[1]
    o_ref[...] = (h_ref[...] + _dot(a_ref[...].astype(BF16), w_ref[:ka])
                  + _dot(b_ref[...].astype(BF16), w_ref[ka:]))


def _outproj2(h, a, b, w, *, tm):
    n, d = h.shape
    row = lambda i: (i, 0)
    return pl.pallas_call(
        _outproj2_body, grid=(n // tm,),
        in_specs=[pl.BlockSpec((tm, d), row), pl.BlockSpec((tm, a.shape[1]), row),
                  pl.BlockSpec((tm, b.shape[1]), row), _const(w.shape)],
        out_specs=pl.BlockSpec((tm, d), row), out_shape=jax.ShapeDtypeStruct((n, d), F32),
        compiler_params=_cparams("parallel"), name="outproj_even",
    )(h, a, b, w)


def _outproj2t_body(h_ref, a_ref, bt_ref, w_ref, o_ref):
    ka = a_ref.shape[1]
    acc = h_ref[...] + _dot(a_ref[...].astype(BF16), w_ref[:ka])
    tq = bt_ref.shape[2]
    for i in range(bt_ref.shape[0]):
        rs = slice(i * tq, (i + 1) * tq)
        o_ref[rs, :] = acc[rs, :] + _dot_tn(bt_ref[i].astype(BF16), w_ref[ka:])


def _outproj2t(h, a, bt, w, *, tm):
    n, d = h.shape
    row = lambda i: (i, 0)
    tq = bt.shape[2]
    return pl.pallas_call(
        _outproj2t_body, grid=(n // tm,),
        in_specs=[pl.BlockSpec((tm, d), row), pl.BlockSpec((tm, a.shape[1]), row),
                  pl.BlockSpec((tm // tq, bt.shape[1], tq), lambda i: (i, 0, 0)), _const(w.shape)],
        out_specs=pl.BlockSpec((tm, d), row), out_shape=jax.ShapeDtypeStruct((n, d), F32),
        compiler_params=_cparams("parallel"), name="outproj_even_t",
    )(h, a, bt, w)


def _outproj_merge_body(h_ref, o0_ref, o1_ref, o2_ref, l0_ref, l1_ref, l2_ref, w_ref, o_ref, *stage, views):
    if views:
        load = lambda ref, g: (ref[...] if DIL_PATTERN[g][1] == 1
                               else _from_residue_view(ref, stage[0], DIL_PATTERN[g][1], ODD_GW))
    else:
        load = lambda ref, g: ref[...]
    mixed = _merge_groups([load(r, g) for g, r in enumerate((o0_ref, o1_ref, o2_ref))],
                          [load(r, g) for g, r in enumerate((l0_ref, l1_ref, l2_ref))])
    o_ref[...] = h_ref[...] + _dot(mixed.astype(BF16), w_ref[...])


def _outproj_merge(h, outs, lses, w, *, tm, views=False):
    n, d = h.shape
    row = lambda i: (i, 0)
    parts = [pl.BlockSpec((tm * a.shape[0] // n, a.shape[1]), row) for a in list(outs) + list(lses)]
    return pl.pallas_call(
        functools.partial(_outproj_merge_body, views=views), grid=(n // tm,),
        in_specs=[pl.BlockSpec((tm, d), row)] + parts + [_const(w.shape)],
        out_specs=pl.BlockSpec((tm, d), row), out_shape=jax.ShapeDtypeStruct((n, d), F32),
        scratch_shapes=[pltpu.VMEM((tm, LANES), F32)] if views else [],
        compiler_params=_cparams("parallel"), name="outproj_odd",
    )(h, *outs, *lses, w)


HG_MM_LEVELS = 2


def _hgrn_prefix_matrix(C):
    t = np.arange(C)[:, None]
    j = np.arange(C)[None, :]
    mats = [j <= t]
    for lvl in range(1, HG_MM_LEVELS + 1):
        if (1 << lvl) <= C:
            mats.append(j <= ((t >> lvl) << lvl) + (1 << lvl) // 2 - 1)
    return jnp.asarray(np.concatenate(mats, axis=0).astype(np.float32), BF16)


def _hgrn_body(hq_ref, hf_ref, hi_ref, hgt_ref, lb_ref, gn_ref, bd_ref, pm_ref, s0_ref, o_ref, sfin_ref, st_ref,
               *b_scr, C):
    c = pl.program_id(1)

    @pl.when(c == 0)
    def _():
        st_ref[...] = s0_ref[0]

    dk = HEAD_DIM
    w = HG_HEADS * dk
    lb = lb_ref[...]
    f = lb + (1.0 - lb) * jax.nn.sigmoid(hf_ref[...])
    lf = jnp.log(f)
    kk = 1.0 - f
    ti = lax.broadcasted_iota(jnp.int32, (C, C), 0)
    si = lax.broadcasted_iota(jnp.int32, (C, C), 1)
    p1 = lf.astype(BF16)
    r1 = lf - p1.astype(F32)
    p2 = r1.astype(BF16)
    p3 = (r1 - p2.astype(F32)).astype(BF16)
    sums = _dot(pm_ref[...], jnp.concatenate([p1, p2, p3], axis=1))
    sums = sums[:, :w] + sums[:, w:2 * w] + sums[:, 2 * w:]
    b = sums[0:C]
    for c4, scr in enumerate(b_scr):
        scr[...] = b[:, c4 * LANES:(c4 + 1) * LANES]
    q = hq_ref[...]
    v = hi_ref[...].astype(BF16)
    heads = range(HG_HEADS)
    hs = [slice(dk * h, dk * (h + 1)) for h in heads]

    qb = q.astype(BF16)
    kb = kk.astype(BF16)
    a = [jnp.where(ti == si, _dot_nt(qb[:, hs[h]], kb[:, hs[h]]), 0.0) for h in heads]
    lvl = 1
    while (1 << lvl) <= C:
        m = 1 << lvl
        half = m // 2
        if lvl <= HG_MM_LEVELS:
            r = sums[lvl * C:(lvl + 1) * C]
        else:
            ref_rows = pl.ds(half - 1, C // m, stride=m)
            r = jnp.concatenate([jnp.broadcast_to(scr[ref_rows, :][:, None, :], (C // m, m, LANES)).reshape(C, LANES)
                                 for scr in b_scr], axis=1)
        e = jnp.exp(-jnp.abs(b - r))
        qt = (q * e).astype(BF16)
        kt = (kk * e).astype(BF16)
        pair = ((ti >> lvl) == (si >> lvl)) & ((ti & (m - 1)) >= half) & ((si & (m - 1)) < half)
        a = [a[h] + jnp.where(pair, _dot_nt(qt[:, hs[h]], kt[:, hs[h]]), 0.0) for h in heads]
        lvl += 1

    b_last = b[C - 1:C, :]
    qe = (q * jnp.exp(b)).astype(BF16)
    kdec = (kk * jnp.exp(b_last - b)).astype(BF16)
    e_last = jnp.exp(b_last)
    outs = []
    for h in heads:
        st = st_ref[h]
        outs.append(_dot(a[h].astype(BF16), v[:, hs[h]]) + _dot_nt(qe[:, hs[h]], st.astype(BF16)))
        st_ref[h] = st * e_last[:, hs[h]] + _dot_tn(v[:, hs[h]], kdec[:, hs[h]])
    o = jnp.concatenate(outs, axis=1)
    gt = hgt_ref[...]
    o_ref[...] = _seg_rms(o, bd_ref[...], gn_ref[...]) * (gt * jax.nn.sigmoid(gt))

    @pl.when(c == pl.num_programs(1) - 1)
    def _():
        sfin_ref[0] = st_ref[...]


def _hgrn(hg, lb, hg_norm, s0, *, batch, seq):
    C = math.gcd(seq, HG_CHUNK)
    nc = seq // C
    w = HG_HEADS * HEAD_DIM
    col = lambda j: pl.BlockSpec((C, w), lambda b, c: (b * nc + c, j))
    st_spec = pl.BlockSpec((1, HG_HEADS, HEAD_DIM, HEAD_DIM), lambda b, c: (b, 0, 0, 0))
    pm = _hgrn_prefix_matrix(C)
    o, sfin = pl.pallas_call(
        functools.partial(_hgrn_body, C=C), grid=(batch, nc),
        in_specs=[col(0), col(1), col(2), col(3), _const((1, w)), _const((1, w)), _const(_block_diag_ones(w).shape), _const(pm.shape),
                  st_spec],
        out_specs=[pl.BlockSpec((C, w), lambda b, c: (b * nc + c, 0)), st_spec],
        out_shape=[jax.ShapeDtypeStruct((batch * seq, w), F32),
                   jax.ShapeDtypeStruct((batch, HG_HEADS, HEAD_DIM, HEAD_DIM), F32)],
        scratch_shapes=[pltpu.VMEM((HG_HEADS, HEAD_DIM, HEAD_DIM), F32)] + [pltpu.VMEM((C, LANES), F32)] * (w // LANES),
        compiler_params=_cparams("parallel", "arbitrary"), name="hgrn_scan",
    )(hg, hg, hg, hg, lb.reshape(1, w), jnp.tile(hg_norm, HG_HEADS).reshape(1, w), _block_diag_ones(w), pm,
      jnp.swapaxes(s0, 2, 3))
    return o, jnp.swapaxes(sfin, 2, 3)


CHUNKS_PER_PAGE = PAGE // CMP_STRIDE
PAGE_LANES = CMP_STRIDE * ROWS_NSA
CMP_HID = 2 * NSA_KV * 2 * HEAD_DIM
CMP_PP = 16


def _compress_body(pt_ref, *refs, n_chunks, transposed):
    pages = refs[:CMP_PP]
    wbig_ref, pe_ref, w2_ref, gk_ref, km_ref, bd_ref, perm_ref, rows_ref, grp_ref, xs_ref = refs[CMP_PP:]
    nb = ROWS_NSA // LANES
    j = pl.program_id(1)
    perm = perm_ref[...]
    for i in range(0, CMP_PP, 2):
        pair = []
        for k in range(2):
            pg = pages[i + k][0]
            if transposed:
                pg = jnp.concatenate([pg[c * LANES:(c + 1) * LANES, :].T for c in range(nb)], axis=1)
            pair.append(_dot(perm, pg.astype(BF16)))
        r0 = pl.multiple_of((j * CMP_PP + i) * CHUNKS_PER_PAGE, 2 * CHUNKS_PER_PAGE)
        for s in range(CMP_STRIDE):
            rs = slice(s * CHUNKS_PER_PAGE, (s + 1) * CHUNKS_PER_PAGE)
            both = jnp.concatenate([pair[0][rs], pair[1][rs]], axis=0).astype(BF16)
            for c in range(nb):
                lo = (c * CMP_STRIDE + s) * LANES
                xs_ref[pl.ds(r0, 2 * CHUNKS_PER_PAGE), lo:lo + LANES] = both[:, c * LANES:(c + 1) * LANES]

    @pl.when(j == pl.num_programs(1) - 1)
    def _():
        blk_w = CMP_STRIDE * LANES
        first = lambda x: jnp.concatenate(
            [_dot(x[:, c * blk_w:(c + 1) * blk_w], wbig_ref[c]) for c in range(nb)], axis=1)
        y = first(xs_ref[...])
        ysh = pltpu.roll(y, n_chunks - 1, 0)
        hid = y + pltpu.roll(ysh, CMP_HID - HEAD_DIM, 1)
        cf = first(pe_ref[...].astype(BF16))
        lane = lax.broadcasted_iota(jnp.int32, (8, CMP_HID), 1)
        cs = jnp.where(((lane >> 6) & 1) == 0, jnp.broadcast_to(cf[0:1], (8, CMP_HID)),
                       jnp.broadcast_to(cf[1:2], (8, CMP_HID)))
        cs = cs + pltpu.roll(cs, CMP_HID - HEAD_DIM, 1)
        hid = hid + cs[0:1]
        act = (hid * jax.nn.sigmoid(hid)).astype(BF16)
        out = _dot(act, w2_ref[...])
        out = _norm_k_rows(out, bd_ref[...], gk_ref[...], km_ref[...])
        rows_ref[0] = out
        kw = NSA_KV * HEAD_DIM
        for g in range(NSA_KV):
            grp_ref[0, g] = jnp.concatenate([out[:, HEAD_DIM * g:HEAD_DIM * (g + 1)],
                                             out[:, kw + HEAD_DIM * g:kw + HEAD_DIM * (g + 1)]], axis=1)


def _compress_weights(cmp_pe, cmp_w1, cmp_w2, k_gain):
    ratio = CMP_BLOCK // CMP_STRIDE
    w1 = cmp_w1.reshape(2, ratio, CMP_STRIDE, HEAD_DIM, HEAD_DIM)
    eye_kv = jnp.eye(2, dtype=F32)
    eye_g = jnp.eye(NSA_KV, dtype=F32)
    nb = ROWS_NSA // LANES
    member_kv = np.arange(2 * NSA_KV) // NSA_KV
    w1m = w1[member_kv].reshape(nb, 2, ratio, CMP_STRIDE, HEAD_DIM, HEAD_DIM)
    wbig = jnp.einsum('chrsde,hH->cshdHre', w1m, eye_kv).reshape(nb, CMP_STRIDE * LANES, 2 * ratio * HEAD_DIM)
    wbig = wbig.astype(BF16)
    pe = cmp_pe.reshape(2, ratio, CMP_STRIDE, HEAD_DIM)
    pe_rows = pe[member_kv].reshape(nb, 2, ratio, CMP_STRIDE, HEAD_DIM).transpose(2, 0, 3, 1, 4)
    pe_rows = pe_rows.reshape(ratio, PAGE_LANES)
    pe_rows = jnp.concatenate([pe_rows, jnp.zeros((8 - ratio, PAGE_LANES), F32)], axis=0)
    sel_r0 = jnp.asarray([1.0, 0.0], F32)
    w2big = jnp.einsum('ked,kK,gG,r->kgreKGd', cmp_w2, eye_kv, eye_g, sel_r0).reshape(CMP_HID, ROWS_NSA).astype(BF16)
    kw = NSA_KV * HEAD_DIM
    gk = jnp.concatenate([jnp.tile(k_gain, NSA_KV), jnp.ones((kw,), F32)]).reshape(1, ROWS_NSA)
    km = jnp.concatenate([jnp.ones((kw,), F32), jnp.zeros((kw,), F32)]).reshape(1, ROWS_NSA)
    return wbig, pe_rows, w2big, gk, km


def _compress(pool, table, cw, *, transposed):
    wbig, pe_rows, w2big, gk, km = cw
    bsz, npg = table.shape
    n_chunks = npg * CHUNKS_PER_PAGE
    page = lambda i: pl.BlockSpec((1,) + pool.shape[1:], lambda b, j, pt: (pt[b, j * CMP_PP + i], 0, 0))
    tok = np.arange(PAGE)
    perm = (tok[None, :] == (tok[:, None] % CHUNKS_PER_PAGE) * CMP_STRIDE + tok[:, None] // CHUNKS_PER_PAGE)
    perm = jnp.asarray(perm.astype(np.float32), BF16)
    return pl.pallas_call(
        functools.partial(_compress_body, n_chunks=n_chunks, transposed=transposed),
        grid_spec=pltpu.PrefetchScalarGridSpec(
            num_scalar_prefetch=1, grid=(bsz, npg // CMP_PP),
            in_specs=[page(i) for i in range(CMP_PP)] + [
                _const(wbig.shape), _const(pe_rows.shape), _const(w2big.shape), _const((1, ROWS_NSA)),
                _const((1, ROWS_NSA)), _const(_block_diag_ones(ROWS_NSA).shape), _const(perm.shape)],
            out_specs=[pl.BlockSpec((1, n_chunks, ROWS_NSA), lambda b, j, pt: (b, 0, 0)),
                       pl.BlockSpec((1, NSA_KV, n_chunks, LANES), lambda b, j, pt: (b, 0, 0, 0))],
            scratch_shapes=[pltpu.VMEM((n_chunks, PAGE_LANES), BF16)]),
        out_shape=[jax.ShapeDtypeStruct((bsz, n_chunks, ROWS_NSA), F32),
                   jax.ShapeDtypeStruct((bsz, NSA_KV, n_chunks, LANES), F32)],
        compiler_params=_cparams("parallel", "arbitrary"), name="nsa_compress",
    )(table, *([pool] * CMP_PP), wbig, pe_rows, w2big, gk, km, _block_diag_ones(ROWS_NSA), perm)


def _select_blocks(imp, pos, n_slc, k_sel):
    lane = lax.broadcasted_iota(jnp.int32, imp.shape, 1)
    cur = pos >> 6
    forced = (lane == 0) | (lane == cur) | (lane == cur - 1)
    score = jnp.where(forced, FORCE_SCORE, jnp.where(lane <= cur, imp, -1.0))
    rank = jnp.zeros(imp.shape, F32)
    for i in range(n_slc):
        ci = score[:, i:i + 1]
        rank = rank + jnp.where(ci > score, 1.0, jnp.where(ci == score, jnp.where(lane > i, 1.0, 0.0), 0.0))
    return jnp.where(lane < n_slc, jnp.where(rank < k_sel, 1.0, 0.0), 0.0)


NSA_TQ = 256
NSA_QC = LANES


def _select_blocks_t(imp, pos, n_slc, k_sel):
    blk = lax.broadcasted_iota(jnp.int32, imp.shape, 0)
    cur = pos >> 6
    forced = (blk == 0) | (blk == cur) | (blk == cur - 1)
    score = jnp.where(forced, FORCE_SCORE, jnp.where(blk <= cur, imp, -1.0))
    rank = jnp.zeros(imp.shape, F32)
    for i in range(n_slc):
        ci = score[i:i + 1, :]
        rank = rank + jnp.where(ci > score, 1.0, jnp.where(ci == score, jnp.where(blk > i, 1.0, 0.0), 0.0))
    return jnp.where(rank < k_sel, 1.0, 0.0)


def _nsa_prompt_t_body(q_ref, gate_ref, kcv_ref, slc_ref, win_ref, gq_ref, bd_ref, eye_ref, fs_ref, fc_ref,
                       ov_ref, ex_ref, o_ref, bc_ref, bs_ref, bw_ref, qt_scr, ks_scr, kw_scr, vs_scr, vw_scr, sel_scr, m_scr, l_scr, acc_scr,
                       *, seq, n_slc, k_sel):
    tq = NSA_TQ
    qc = NSA_QC
    hd = HEAD_DIM
    n_tiles = seq // tq
    width = NSA_GROUP * tq
    n_chunks = width // qc
    eye = eye_ref[...]
    n_c = kcv_ref.shape[2]
    far = NSA_WINDOW // tq

    @pl.when(pl.program_id(1) == 0)
    def _():
        key_i = lax.broadcasted_iota(jnp.int32, (tq, tq), 0)
        qry_i = lax.broadcasted_iota(jnp.int32, (tq, tq), 1)
        blk_i = lax.broadcasted_iota(jnp.int32, (n_c, tq), 0)
        n_cmp = seq // CMP_STRIDE - CMP_BLOCK // CMP_STRIDE + 1
        span = tq + CMP_STRIDE * n_c
        for h in range(NSA_GROUP):
            hs = slice(h * tq, (h + 1) * tq)
            for d in range(n_tiles):
                u = jnp.broadcast_to(fs_ref[0, h:h + 1, d * tq:(d + 2) * tq], (tq, 2 * tq))
                tile = pltpu.roll(u, tq, 1, stride=1, stride_axis=0)[:, :tq]
                bs_ref[d, :, hs] = tile
                if d == far:
                    bw_ref[0, :, hs] = jnp.where(qry_i <= key_i, tile, NEG)
            for t in range(n_tiles):
                u = jnp.broadcast_to(fc_ref[0, h:h + 1, t * tq:t * tq + span], (n_c, span))
                tile = pltpu.roll(u, tq, 1, stride=CMP_STRIDE, stride_axis=0)[:, :tq]
                bc_ref[t, :, hs] = jnp.where(blk_i < n_cmp, tile, NEG)

    qn = (_seg_rms(q_ref[...], bd_ref[...], gq_ref[...]) * (hd ** -0.5)).astype(BF16)
    for t in range(n_tiles):
        rs = slice(t * tq, (t + 1) * tq)
        for h in range(NSA_GROUP):
            qt_scr[t, :, h * tq:(h + 1) * tq] = _dot_nt(eye, qn[rs, h * hd:(h + 1) * hd]).astype(BF16)
        for kv_ref, k_scr, v_scr in ((slc_ref, ks_scr, vs_scr), (win_ref, kw_scr, vw_scr)):
            kv = kv_ref[0, rs, :].astype(BF16)
            k_scr[t] = kv[:, :hd]
            v_scr[t] = _dot_nt(eye, kv[:, hd:]).astype(BF16)
    kc = kcv_ref[0, 0, :, :hd].astype(BF16)
    vct = _dot_nt(eye, kcv_ref[0, 0, :, hd:].astype(BF16)).astype(BF16)
    def flash_init():
        m_scr[...] = jnp.full((1, width), NEG, F32)
        l_scr[...] = jnp.zeros((1, width), F32)
        acc_scr[...] = jnp.zeros((hd, width), F32)

    def flash_update(k_scr, v_scr, t, tiles, running):
        s_alls = [_dot(k_scr[jt], qt_scr[t]) for jt, _, _ in tiles]
        if running:
            m_all = m_scr[...]
            l_all = l_scr[...]
        alphas, ls, ms, ps = [], [], [], [[] for _ in tiles]
        for c in range(n_chunks):
            cs = slice(c * qc, (c + 1) * qc)
            qoff = (c * qc) % tq
            ss = []
            for (_, bias_tile, sel_add), s_all in zip(tiles, s_alls):
                s = s_all[:, cs] + bias_tile(cs)
                ss.append(s if sel_add is None else s + sel_add[:, qoff:qoff + qc])
            m_new = functools.reduce(jnp.maximum, [jnp.max(s, axis=0, keepdims=True) for s in ss])
            if running:
                m_new = jnp.maximum(m_all[:, cs], m_new)
                alpha = jnp.exp(m_all[:, cs] - m_new)
                alphas.append(alpha)
            l_new = alpha * l_all[:, cs] if running else 0.0
            for k, s in enumerate(ss):
                p = jnp.exp(s - m_new)
                l_new = l_new + jnp.sum(p, axis=0, keepdims=True)
                ps[k].append(p.astype(BF16))
            ls.append(l_new)
            ms.append(m_new)
        cat = lambda xs: jnp.concatenate(xs, axis=1)
        pv = sum(_dot(v_scr[jt], cat(ps[k])) for k, (jt, _, _) in enumerate(tiles))
        acc_scr[...] = cat(alphas) * acc_scr[...] + pv if running else pv
        m_scr[...] = cat(ms)
        l_scr[...] = cat(ls)

    def flash_result():
        return acc_scr[...] / jnp.maximum(l_scr[...], 1e-30)

    def q_tile(t, carry):
        bias_c = bc_ref[t]
        s_c = _dot(kc, qt_scr[t]) + bias_c
        m_c = jnp.max(s_c, axis=0, keepdims=True)
        e_c = jnp.where(bias_c > NEG_TEST, jnp.exp(s_c - m_c), 0.0)
        p_c = (e_c / jnp.maximum(jnp.sum(e_c, axis=0, keepdims=True), 1e-30)).astype(BF16)
        o_c = _dot(vct, p_c)
        imp = sum(_dot(ov_ref[...], p_c[:, h * tq:(h + 1) * tq]) for h in range(NSA_GROUP))
        pos = t * tq + lax.broadcasted_iota(jnp.int32, (1, tq), 1)
        sel = _select_blocks_t(imp[:n_slc], pos, n_slc, k_sel)
        sel_scr[...] = jnp.zeros(sel_scr.shape, BF16)
        sel_scr[0:n_slc, :] = ((1.0 - sel) * NEG).astype(BF16)
        slc_tile = lambda jt: (jt, lambda cs: bs_ref[t - jt, :, cs], _dot(ex_ref[jt], sel_scr[...]))
        for j0 in range(0, t + 1, 2):
            flash_update(ks_scr, vs_scr, t, [slc_tile(j) for j in range(j0, min(j0 + 2, t + 1))], j0 > 0)
        o_s = flash_result()
        far = NSA_WINDOW // tq
        win_tile = lambda d: (t - d, (lambda cs: bw_ref[0, :, cs]) if d == far else (lambda cs: bs_ref[d, :, cs]),
                              None)
        flash_update(kw_scr, vw_scr, t, [win_tile(d) for d in range(min(t, far), -1, -1)], False)
        o_w = flash_result()
        gates = gate_ref[0, t]
        for h in range(NSA_GROUP):
            hs = slice(h * tq, (h + 1) * tq)
            o_ref[t, h * hd:(h + 1) * hd, :] = (gates[3 * h:3 * h + 1] * o_c[:, hs]
                                                + gates[3 * h + 1:3 * h + 2] * o_s[:, hs]
                                                + gates[3 * h + 2:3 * h + 3] * o_w[:, hs])
        return carry

    for t in range(n_tiles):
        q_tile(t, 0)


def _nsa_prompt_t_tables(rel_bias, seq, n_c):
    tq = NSA_TQ
    n_tiles = seq // tq
    n_slc = -(-seq // SEL_BLOCK)
    ds = jnp.arange((n_tiles + 1) * tq) - tq
    fs = _bias_tile(rel_bias, ds, ds >= 0).reshape(NSA_KV, NSA_GROUP, -1)
    dc = jnp.arange(n_tiles * tq + CMP_STRIDE * n_c) - (CMP_STRIDE * n_c + CMP_BLOCK - 1)
    fc = _bias_tile(rel_bias, dc, dc >= 0).reshape(NSA_KV, NSA_GROUP, -1)
    n_cmp = seq // CMP_STRIDE - CMP_BLOCK // CMP_STRIDE + 1
    c_lo = np.arange(n_c) * CMP_STRIDE
    s_lo = np.arange(LANES) * SEL_BLOCK
    ov = ((c_lo[None, :] <= s_lo[:, None] + SEL_BLOCK - 1) & (c_lo[None, :] + CMP_BLOCK - 1 >= s_lo[:, None])
          & (np.arange(n_c) < n_cmp)[None, :] & (np.arange(LANES) < n_slc)[:, None])
    ex = (np.arange(seq)[:, None] // SEL_BLOCK == np.arange(LANES)[None, :]).reshape(n_tiles, tq, LANES)
    as_bf = lambda a: jnp.asarray(a.astype(np.float32), BF16)
    return fs, fc, as_bf(ov), as_bf(ex), n_slc


def _nsa_prompt_t(q, gates_t, kcv, slcg, wing, q_gain, tables, *, batch, seq):
    fs, fc, ov, ex, n_slc = tables
    n_c = kcv.shape[2]
    tq = NSA_TQ
    n_tiles = seq // tq
    assert n_tiles > NSA_WINDOW // tq
    gw = NSA_GROUP * HEAD_DIM
    width = NSA_GROUP * tq
    per_g = lambda shape: pl.BlockSpec((1,) + shape, lambda g, b: (g,) + (0,) * len(shape),
                                       pipeline_mode=pl.Buffered(1))
    eye = jnp.eye(HEAD_DIM, dtype=BF16)
    return pl.pallas_call(
        functools.partial(_nsa_prompt_t_body, seq=seq, n_slc=n_slc, k_sel=min(SEL_TOPK, n_slc)),
        grid=(NSA_KV, batch),
        in_specs=[pl.BlockSpec((seq, gw), lambda g, b: (b, g)),
                  pl.BlockSpec((1, n_tiles, 16, tq), lambda g, b: (g, b, 0, 0)),
                  pl.BlockSpec((1, 1, n_c, LANES), lambda g, b: (b, g, 0, 0)),
                  pl.BlockSpec((1, seq, LANES), lambda g, b: (g, b, 0)),
                  pl.BlockSpec((1, seq, LANES), lambda g, b: (g, b, 0)),
                  _const((1, gw)), _const(_block_diag_ones(gw).shape), _const(eye.shape),
                  per_g(fs.shape[1:]), per_g(fc.shape[1:]), _const(ov.shape), _const(ex.shape)],
        out_specs=pl.BlockSpec((n_tiles, gw, tq), lambda g, b: (b, g, 0)),
        out_shape=jax.ShapeDtypeStruct((batch * n_tiles, NSA_KV * gw, tq), F32),
        scratch_shapes=[pltpu.VMEM((n_tiles, n_c, width), F32), pltpu.VMEM((n_tiles, tq, width), F32),
                        pltpu.VMEM((1, tq, width), F32), pltpu.VMEM((n_tiles, HEAD_DIM, width), BF16),
                        pltpu.VMEM((n_tiles, tq, HEAD_DIM), BF16), pltpu.VMEM((n_tiles, tq, HEAD_DIM), BF16),
                        pltpu.VMEM((n_tiles, HEAD_DIM, tq), BF16), pltpu.VMEM((n_tiles, HEAD_DIM, tq), BF16),
                        pltpu.VMEM((LANES, tq), BF16), pltpu.VMEM((1, width), F32), pltpu.VMEM((1, width), F32),
                        pltpu.VMEM((HEAD_DIM, width), F32)],
        compiler_params=_cparams("arbitrary", "arbitrary"), name="nsa_prompt",
    )(q, gates_t, kcv, slcg, wing, jnp.tile(q_gain, NSA_GROUP).reshape(1, gw), _block_diag_ones(gw), eye,
      fs, fc, ov, ex)


SMP_PP = 16
NSA_KW = NSA_KV * HEAD_DIM
SEL_LANES = 2 * LANES


def _nsa_sample_body(pt_ref, q_ref, gate_ref, ckv_ref, *refs, n_slc, k_sel, past, tq):
    pages = refs[:SMP_PP]
    (slcn_ref, wst_ref, winn_ref, gq_ref, bd_ref, bc_ref, bs_ref, bsn_ref, bw_ref, bwn_ref, ov_ref, ex_ref, exn_ref,
     o_ref, nwin_ref, qbd_scr, sel_scr, kbuf_scr, vbuf_scr, s_scr, sn_scr, vn_scr, oc_scr, ow_scr,
     pad_scr) = refs[SMP_PP:]
    j = pl.program_id(1)
    hd = HEAD_DIM
    kw = NSA_KW
    rows = NSA_HEADS * tq
    init = (jnp.full((rows, 1), NEG, F32), jnp.zeros((rows, 1), F32), jnp.zeros((rows, kw), F32))

    @pl.when(j == 0)
    def _():
        qn = _seg_rms(q_ref[...], bd_ref[...], gq_ref[...]) * (hd ** -0.5)
        zero = jnp.zeros((tq, hd), F32)
        blocks = []
        for h in range(NSA_HEADS):
            qh = qn[:, hd * h:hd * (h + 1)]
            blocks.append(jnp.concatenate([qh if g == h // NSA_GROUP else zero for g in range(NSA_KV)], axis=1))
        qbd = jnp.concatenate(blocks, axis=0).astype(BF16)
        qbd_scr[...] = qbd
        ckv = ckv_ref[0]
        bias_c = bc_ref[...]
        ok_c = bias_c > NEG_TEST
        s_c = _dot_nt(qbd, ckv[:, :kw].astype(BF16)) + bias_c
        m_c = jnp.max(s_c, axis=-1, keepdims=True)
        e_c = jnp.where(ok_c, jnp.exp(s_c - m_c), 0.0)
        p_c = (e_c / jnp.maximum(jnp.sum(e_c, axis=-1, keepdims=True), 1e-30)).astype(BF16)
        oc_scr[...] = _dot(p_c, ckv[:, kw:].astype(BF16))
        d = _dot(p_c, ov_ref[...])
        gr = NSA_GROUP * tq
        imp = jnp.concatenate(
            [sum(d[g * gr + h * tq:g * gr + (h + 1) * tq] for h in range(NSA_GROUP)) for g in range(NSA_KV)], axis=0)
        pos = past + (lax.broadcasted_iota(jnp.int32, (NSA_KV * tq, 1), 0) & (tq - 1))
        sel = _select_blocks(imp, pos, n_slc, k_sel)
        sel = jnp.concatenate([sel[g * tq:(g + 1) * tq] for g in range(NSA_KV) for _ in range(NSA_GROUP)], axis=0)
        sel = ((1.0 - sel) * NEG).astype(BF16)
        sel_scr[...] = sel
        pad_scr[...] = jnp.zeros(pad_scr.shape, F32)
        pad_scr[0:tq, :] = slcn_ref[...]
        kn = pad_scr[...].astype(BF16)
        sn_scr[...] = _dot_nt(qbd, kn[:, :kw]) + bsn_ref[...] + _dot(sel, exn_ref[...])
        vn_scr[...] = kn[:, kw:]
        pad_scr[0:tq, :] = winn_ref[...]
        nwin_ref[0] = _shifted_buffer(wst_ref[0], pad_scr[...], tq)
        wst = wst_ref[0].astype(BF16)
        wn = pad_scr[...].astype(BF16)
        bias_w = bw_ref[...]
        mw = _softmax_piece(_dot(qbd, wst[:kw]) + bias_w, bias_w > NEG_TEST, *init, wst[kw:], v_t=True)
        bias_wn = bwn_ref[...]
        mw = _softmax_piece(_dot_nt(qbd, wn[:, :kw]) + bias_wn, bias_wn > NEG_TEST, *mw, wn[:, kw:])
        ow_scr[...] = mw[2] / jnp.maximum(mw[1], 1e-30)

    for i in range(SMP_PP):
        pg = pages[i][0].astype(BF16)
        kbuf_scr[:, i * PAGE:(i + 1) * PAGE] = pg[:kw]
        vbuf_scr[j, :, i * PAGE:(i + 1) * PAGE] = pg[kw:]
    s_scr[j] = _dot(qbd_scr[...], kbuf_scr[...]) + bs_ref[j] + _dot(sel_scr[...], ex_ref[j])

    @pl.when(j == pl.num_programs(1) - 1)
    def _():
        s_all = s_scr[...]
        s_new = sn_scr[...]
        m = jnp.maximum(jnp.max(jnp.max(s_all, axis=0), axis=-1, keepdims=True),
                        jnp.max(s_new, axis=-1, keepdims=True))
        p_new = jnp.exp(s_new - m)
        den = jnp.sum(p_new, axis=-1, keepdims=True)
        acc = _dot(p_new.astype(BF16), vn_scr[...])
        for st in range(s_all.shape[0]):
            p = jnp.exp(s_all[st] - m)
            den = den + jnp.sum(p, axis=-1, keepdims=True)
            acc = acc + _dot_nt(p.astype(BF16), vbuf_scr[st])
        o_s = acc / jnp.maximum(den, 1e-30)
        o_c = oc_scr[...]
        o_w = ow_scr[...]
        outs = []
        for h in range(NSA_HEADS):
            g, hg = divmod(h, NSA_GROUP)
            gt = gate_ref[g]
            rs = slice(h * tq, (h + 1) * tq)
            ls = slice(hd * g, hd * (g + 1))
            outs.append(gt[:, 3 * hg:3 * hg + 1] * o_c[rs, ls] + gt[:, 3 * hg + 1:3 * hg + 2] * o_s[rs, ls]
                        + gt[:, 3 * hg + 2:3 * hg + 3] * o_w[rs, ls])
        o_ref[...] = jnp.concatenate(outs, axis=1)


def _nsa_sample_tables(rel_bias, past, tq, n_c, w_state):
    rows = NSA_HEADS * tq
    total = past + tq
    n_cmp = total // CMP_STRIDE - CMP_BLOCK // CMP_STRIDE + 1
    n_slc = -(-total // SEL_BLOCK)
    pos = past + jnp.arange(tq)
    c_hi = jnp.arange(n_c) * CMP_STRIDE + CMP_BLOCK - 1
    dist_c = pos[:, None] - c_hi[None, :]
    bc = _bias_tile(rel_bias, dist_c, (dist_c >= 0) & (jnp.arange(n_c) < n_cmp)[None, :]).reshape(rows, n_c)
    n_steps = past // (SMP_PP * PAGE)
    d_ext = jnp.arange(tq + past - 1) + 1
    bs = _toeplitz(_bias_tile(rel_bias, d_ext, d_ext >= 0), tq, past)
    bs = bs.reshape(rows, n_steps, SMP_PP * PAGE).transpose(1, 0, 2)
    jn = jnp.arange(PAGE)
    dist_n = jnp.arange(tq)[:, None] - jn[None, :]
    new_ok = (dist_n >= 0) & (jn < tq)[None, :]
    bsn = _bias_tile(rel_bias, dist_n, new_ok).reshape(rows, PAGE)
    dist_w = jnp.arange(tq)[:, None] + w_state - jnp.arange(w_state)[None, :]
    bw = _bias_tile(rel_bias, dist_w, (dist_w >= 0) & (dist_w <= NSA_WINDOW)).reshape(rows, w_state)
    bwn = _bias_tile(rel_bias, dist_n, new_ok & (dist_n <= NSA_WINDOW)).reshape(rows, PAGE)
    c_lo = np.arange(n_c) * CMP_STRIDE
    s_lo = np.arange(SEL_LANES) * SEL_BLOCK
    ov = ((c_lo[:, None] <= s_lo[None, :] + SEL_BLOCK - 1) & (c_lo[:, None] + CMP_BLOCK - 1 >= s_lo[None, :])
          & (np.arange(n_c) < n_cmp)[:, None] & (np.arange(SEL_LANES) < n_slc)[None, :])
    blk = np.arange(SEL_LANES)
    ex = (blk[:, None] == (np.arange(past)[None, :] // SEL_BLOCK)).reshape(SEL_LANES, n_steps, SMP_PP * PAGE)
    ex = ex.transpose(1, 0, 2)
    exn = (blk[:, None] == ((past + np.arange(PAGE))[None, :] // SEL_BLOCK)) & (np.arange(PAGE) < tq)[None, :]
    as_bf = lambda a: jnp.asarray(a.astype(np.float32), BF16)
    return bc, bs, bsn, bw, bwn, as_bf(ov), as_bf(ex), as_bf(exn), n_slc


def _nsa_sample(q, gates, crows, slc_pool_t, table, slc_new, win_state_t, win_new, q_gain, tables, *, tq):
    bc, bs, bsn, bw, bwn, ov, ex, exn, n_slc = tables
    bsz, npg = table.shape
    past = npg * PAGE
    n_c = crows.shape[1]
    w_state = win_state_t.shape[2]
    rows = NSA_HEADS * tq
    qw = NSA_HEADS * HEAD_DIM
    page = lambda i: pl.BlockSpec((1, ROWS_NSA, PAGE), lambda b, j, pt: (pt[b, j * SMP_PP + i], 0, 0))
    per_b = lambda shape: pl.BlockSpec((1,) + shape, lambda b, j, pt: (b,) + (0,) * len(shape))
    new_rows = pl.BlockSpec((tq, ROWS_NSA), lambda b, j, pt: (b, 0))
    return pl.pallas_call(
        functools.partial(_nsa_sample_body, n_slc=n_slc, k_sel=min(SEL_TOPK, n_slc), past=past, tq=tq),
        grid_spec=pltpu.PrefetchScalarGridSpec(
            num_scalar_prefetch=1, grid=(bsz, npg // SMP_PP),
            in_specs=[pl.BlockSpec((tq, qw), lambda b, j, pt: (b, 0)),
                      pl.BlockSpec((NSA_KV, tq, LANES), lambda b, j, pt: (0, b, 0)),
                      per_b((n_c, ROWS_NSA))] + [page(i) for i in range(SMP_PP)] + [
                      new_rows, per_b((ROWS_NSA, w_state)), new_rows,
                      _const((1, qw)), _const(_block_diag_ones(qw).shape), _const(bc.shape), _const(bs.shape), _const(bsn.shape),
                      _const(bw.shape), _const(bwn.shape), _const(ov.shape), _const(ex.shape), _const(exn.shape)],
            out_specs=[pl.BlockSpec((tq, qw), lambda b, j, pt: (b, 0)), per_b((ROWS_NSA, w_state))],
            scratch_shapes=[pltpu.VMEM((rows, NSA_KW), BF16), pltpu.VMEM((rows, SEL_LANES), BF16),
                            pltpu.VMEM((NSA_KW, SMP_PP * PAGE), BF16),
                            pltpu.VMEM((npg // SMP_PP, NSA_KW, SMP_PP * PAGE), BF16),
                            pltpu.VMEM((npg // SMP_PP, rows, SMP_PP * PAGE), F32),
                            pltpu.VMEM((rows, PAGE), F32), pltpu.VMEM((PAGE, NSA_KW), BF16),
                            pltpu.VMEM((rows, NSA_KW), F32), pltpu.VMEM((rows, NSA_KW), F32),
                            pltpu.VMEM((PAGE, ROWS_NSA), F32)]),
        out_shape=[jax.ShapeDtypeStruct((bsz * tq, qw), F32), jax.ShapeDtypeStruct(win_state_t.shape, F32)],
        compiler_params=_cparams("parallel", "arbitrary"), name="nsa_sample",
    )(table, q, gates, crows, *([slc_pool_t] * SMP_PP), slc_new, win_state_t, win_new,
      jnp.tile(q_gain, NSA_HEADS).reshape(1, qw), _block_diag_ones(qw), bc, bs, bsn, bw, bwn, ov, ex, exn)


DIL_TQ = 128


def _head_of_lane(shape):
    return lax.broadcasted_iota(jnp.int32, shape, 1) >> 6


def _stack_heads(qt, lane_head):
    return jnp.concatenate([jnp.where(lane_head == h, qt, 0.0) for h in range(DIL_HEADS)], axis=0).astype(BF16)


def _unstack_heads(r, lane_head, t):
    return sum(jnp.where(lane_head == h, r[h * t:(h + 1) * t], 0.0) for h in range(DIL_HEADS))


def _dil_prompt_body(q_ref, kv_ref, bias_ref, o_ref, lse_ref, *, n_sub, dil):
    tq = DIL_TQ
    gw = ODD_GW
    lane_head = _head_of_lane((tq, gw))
    n_t = n_sub // tq

    def tile(t, r):
        ql = slice(gw * r, gw * (r + 1))
        kl = slice(2 * gw * r, 2 * gw * r + gw)
        vl = slice(2 * gw * r + gw, 2 * gw * (r + 1))
        cur_rows = slice(t * tq, (t + 1) * tq)
        prev_rows = slice(max(t - 1, 0) * tq, (max(t - 1, 0) + 1) * tq)
        kk = jnp.concatenate([kv_ref[prev_rows, kl], kv_ref[cur_rows, kl]], axis=0).astype(BF16)
        vv = jnp.concatenate([kv_ref[prev_rows, vl], kv_ref[cur_rows, vl]], axis=0).astype(BF16)
        bias = bias_ref[min(t, 1)]
        s = _dot_nt(_stack_heads(q_ref[cur_rows, ql], lane_head), kk) + bias
        m = jnp.max(s, axis=-1, keepdims=True)
        e = jnp.where(bias > NEG_TEST, jnp.exp(s - m), 0.0)
        l = jnp.maximum(jnp.sum(e, axis=-1, keepdims=True), 1e-30)
        res = _dot((e / l).astype(BF16), vv)
        lse = jnp.broadcast_to(m + jnp.log(l), (DIL_HEADS * tq, gw))
        o_ref[cur_rows, ql] = _unstack_heads(res, lane_head, tq)
        lse_ref[cur_rows, ql] = _unstack_heads(lse, lane_head, tq)

    for r in range(dil):
        for t in range(n_t):
            tile(t, r)


def _merge_groups(outs, lses):
    mx = functools.reduce(jnp.maximum, lses)
    ws = [jnp.exp(l - mx) for l in lses]
    return sum(w * o for w, o in zip(ws, outs)) / sum(ws)


def _dil_prompt_tables(rel_bias):
    tq = DIL_TQ
    m_ext = jnp.arange(3 * tq - 1) - (tq - 1)
    cur_tile = jnp.arange(2 * tq) >= tq
    tiles = []
    for gi, (win, dil) in enumerate(DIL_PATTERN):
        assert win // dil == tq
        t = _toeplitz(_bias_tile(rel_bias, m_ext * dil, (m_ext >= 0) & (m_ext <= tq)), tq, 2 * tq)
        t = t[gi * DIL_HEADS:(gi + 1) * DIL_HEADS].reshape(DIL_HEADS * tq, 2 * tq)
        tiles.append(jnp.stack([jnp.where(cur_tile, t, NEG), t]))
    return tiles


def _dil_prompt(q_views, kv_views, tables, *, batch, seq):
    gw = ODD_GW
    outs, lses = [], []
    for gi, (_, dil) in enumerate(DIL_PATTERN):
        n_sub = seq // dil
        assert n_sub % DIL_TQ == 0
        qv, kvv = q_views[gi], kv_views[gi]
        o_spec = pl.BlockSpec((n_sub, dil * gw), lambda b: (b, 0))
        o, lse = pl.pallas_call(
            functools.partial(_dil_prompt_body, n_sub=n_sub, dil=dil), grid=(batch,),
            in_specs=[o_spec, pl.BlockSpec((n_sub, dil * 2 * gw), lambda b: (b, 0)), _const(tables[gi].shape)],
            out_specs=[o_spec, o_spec],
            out_shape=[jax.ShapeDtypeStruct(qv.shape, F32)] * 2,
            compiler_params=_cparams("parallel"), name=f"dilated_prompt_g{gi}",
        )(qv, kvv, tables[gi])
        outs.append(o)
        lses.append(lse)
    return outs, lses


def _shifted_buffer(st, new_rows_padded, tq):
    ln = st.shape[1]
    shifted = pltpu.roll(st, ln - tq, 1)
    new_t = pltpu.roll(new_rows_padded.T, LANES - tq, 1)
    lane = lax.broadcasted_iota(jnp.int32, (st.shape[0], LANES), 1)
    tail = jnp.where(lane >= LANES - tq, new_t, shifted[:, ln - LANES:])
    return tail if ln == LANES else jnp.concatenate([shifted[:, :ln - LANES], tail], axis=1)


def _dil_sample_body(q_ref, kvn_ref, st0_ref, st1_ref, st2_ref, b0_ref, b1_ref, b2_ref, bn_ref, o_ref, lse_ref,
                     n0_ref, n1_ref, n2_ref, pad_scr, *, tq):
    gw = ODD_GW
    lane_head = _head_of_lane((tq, gw))
    rows = DIL_HEADS * tq
    pad_scr[...] = jnp.zeros(pad_scr.shape, F32)
    for gi, (st_ref, b_ref, new_ref) in enumerate(((st0_ref, b0_ref, n0_ref), (st1_ref, b1_ref, n1_ref),
                                                    (st2_ref, b2_ref, n2_ref))):
        qs = _stack_heads(q_ref[gi], lane_head)
        pad_scr[0:tq, :] = kvn_ref[gi]
        new_ref[0] = _shifted_buffer(st_ref[0], pad_scr[...], tq)
        st = st_ref[0].astype(BF16)
        kn = pad_scr[...].astype(BF16)
        bias = b_ref[...]
        acc = (jnp.full((rows, 1), NEG, F32), jnp.zeros((rows, 1), F32), jnp.zeros((rows, gw), F32))
        acc = _softmax_piece(_dot(qs, st[:gw]) + bias, bias > NEG_TEST, *acc, st[gw:], v_t=True)
        bias_n = bn_ref[gi]
        m, l, a = _softmax_piece(_dot_nt(qs, kn[:, :gw]) + bias_n, bias_n > NEG_TEST, *acc, kn[:, gw:])
        l = jnp.maximum(l, 1e-30)
        o_ref[gi] = _unstack_heads(a / l, lane_head, tq)
        lse_ref[gi] = _unstack_heads(jnp.broadcast_to(m + jnp.log(l), (rows, gw)), lane_head, tq)


def _dil_sample_tables(rel_bias, tq, state_lens):
    rows = DIL_HEADS * tq
    i = jnp.arange(tq)[:, None]
    per_state, per_new = [], []
    jn = jnp.arange(PAGE)[None, :]
    for gi, ((win, dil), ln) in enumerate(zip(DIL_PATTERN, state_lens)):
        hs = slice(gi * DIL_HEADS, (gi + 1) * DIL_HEADS)
        d = i + ln - jnp.arange(ln)[None, :]
        per_state.append(_bias_tile(rel_bias, d, (d % dil == 0) & (d <= win))[hs].reshape(rows, ln))
        dn = i - jn
        per_new.append(_bias_tile(rel_bias, dn, (dn >= 0) & (dn % dil == 0) & (dn <= win) & (jn < tq))[hs]
                       .reshape(rows, PAGE))
    return per_state, jnp.stack(per_new)


def _dil_sample(q3, kv3, states, tables, *, tq):
    per_state, bias_new = tables
    bsz = states[0].shape[0]
    gw = ODD_GW
    st_spec = lambda s: pl.BlockSpec((1,) + s.shape[1:], lambda b: (b, 0, 0))
    grp = lambda w: pl.BlockSpec((3, tq, w), lambda b: (0, b, 0))
    return pl.pallas_call(
        functools.partial(_dil_sample_body, tq=tq), grid=(bsz,),
        in_specs=[grp(gw), grp(2 * gw)] + [st_spec(s) for s in states] + [_const(t.shape) for t in per_state]
        + [_const(bias_new.shape)],
        out_specs=[grp(gw), grp(gw)] + [st_spec(s) for s in states],
        out_shape=[jax.ShapeDtypeStruct((3, bsz * tq, gw), F32)] * 2
        + [jax.ShapeDtypeStruct(s.shape, F32) for s in states],
        scratch_shapes=[pltpu.VMEM((PAGE, 2 * gw), F32)],
        compiler_params=_cparams("parallel"), name="dilated_sample",
    )(q3, kv3, *states, *per_state, bias_new)


def _pad_even_w(w_in):
    return jnp.concatenate([w_in, jnp.zeros((w_in.shape[0], EVEN_PAD - w_in.shape[1]), w_in.dtype)], axis=1).astype(BF16)


def _even_mixer_prompt(h, gain, w_in, w_out, lb, hg_norm, qk_norm, cmp_pe, cmp_w1, cmp_w2, rel_bias, *, batch, seq, tm):
    n = batch * seq
    hg, q, cmp_rows, slcg, wing, gates_t, cmp_t, slc_t, win_t = _inproj_even(
        h, gain, _pad_even_w(w_in), qk_norm, tm=tm, seq=seq)
    s0 = jnp.zeros((batch, HG_HEADS, HEAD_DIM, HEAD_DIM), F32)
    o_h, s_fin = _hgrn(hg, lb, hg_norm, s0, batch=batch, seq=seq)
    cw = _compress_weights(cmp_pe, cmp_w1, cmp_w2, qk_norm[1])
    npg = seq // PAGE
    table = jnp.arange(batch * npg, dtype=jnp.int32).reshape(batch, npg)
    _, kcv = _compress(cmp_rows.reshape(n // PAGE, PAGE, ROWS_NSA), table, cw, transposed=False)
    tables = _nsa_prompt_t_tables(rel_bias, seq, kcv.shape[2])
    o_nt = _nsa_prompt_t(q, gates_t, kcv, slcg, wing, qk_norm[0], tables, batch=batch, seq=seq)
    h = _outproj2t(h, o_h, o_nt, w_out.astype(BF16), tm=tm)
    keep = min(NSA_WINDOW, seq)
    return h, {"hgrn": s_fin, "cmp": _rows_from_t(cmp_t, NSA_KV), "slc": _rows_from_t(slc_t, NSA_KV),
               "win": _rows_from_t(win_t[:, :, seq - keep:], NSA_KV)}


def _rows_from_t(a, heads):
    return jnp.moveaxis(a.reshape(a.shape[0], 2, heads, HEAD_DIM, a.shape[2]), -1, 1)


def _even_mixer_sample(h, gain, w_in, w_out, lb, hg_norm, qk_norm, cmp_pe, cmp_w1, cmp_w2, rel_bias,
                       state_hgrn, cmp_pool, slc_pool, win_state, page_table, *, batch, tq, tm):
    assert tq < CMP_STRIDE and tq & (tq - 1) == 0
    hg, q, cmp_rows, slc_rows, win_rows, gates = _inproj_even(h, gain, _pad_even_w(w_in), qk_norm, tm=tm)
    o_h, s_fin = _hgrn(hg, lb, hg_norm, state_hgrn, batch=batch, seq=tq)
    cw = _compress_weights(cmp_pe, cmp_w1, cmp_w2, qk_norm[1])
    rows_t = lambda a: jnp.moveaxis(a, 1, -1).reshape(a.shape[0], ROWS_NSA, a.shape[1])
    crows, _ = _compress(rows_t(cmp_pool), page_table, cw, transposed=True)
    past = page_table.shape[1] * PAGE
    w_state = win_state.shape[1]
    tables = _nsa_sample_tables(rel_bias, past, tq, crows.shape[1], w_state)
    o_n, new_win_t = _nsa_sample(q, gates, crows, rows_t(slc_pool), page_table, slc_rows, rows_t(win_state), win_rows,
                                 qk_norm[0], tables, tq=tq)
    h = _outproj2(h, o_h, o_n, w_out.astype(BF16), tm=tm)
    new_win = jnp.moveaxis(new_win_t.reshape((batch,) + win_state.shape[2:] + (w_state,)), -1, 1)
    return h, {"hgrn": s_fin, "cmp": cmp_rows, "slc": slc_rows, "win": new_win}


def _odd_mixer_prompt(h, gain, w_in, w_out, qk_norm, rel_bias, *, batch, seq, tm):
    qv0, kvv0, qv1, kvv1, qv2, kvv2, *kv_t = _inproj_odd(h, gain, w_in.astype(BF16), qk_norm, tm=tm, seq=seq)
    outs, lses = _dil_prompt([qv0, qv1, qv2], [kvv0, kvv1, kvv2], _dil_prompt_tables(rel_bias),
                             batch=batch, seq=seq)
    bufs = [_rows_from_t(t[:, :, seq - min(w, seq):], DIL_HEADS) for t, (w, _) in zip(kv_t, DIL_PATTERN)]
    return _outproj_merge(h, outs, lses, w_out.astype(BF16), tm=tm, views=True), bufs


def _odd_mixer_sample(h, gain, w_in, w_out, qk_norm, rel_bias, states, *, batch, tq, tm):
    q3, kv3 = _inproj_odd(h, gain, w_in.astype(BF16), qk_norm, tm=tm)
    tables = _dil_sample_tables(rel_bias, tq, [s.shape[1] for s in states])
    states_t = [jnp.moveaxis(s, 1, -1).reshape(batch, 2 * ODD_GW, s.shape[1]) for s in states]
    o3, lse3, *new_t = _dil_sample(q3, kv3, states_t, tables, tq=tq)
    new = [jnp.moveaxis(n.reshape(s.shape[:1] + s.shape[2:] + s.shape[1:2]), -1, 1) for n, s in zip(new_t, states)]
    h = _outproj_merge(h, [o3[g] for g in range(3)], [lse3[g] for g in range(3)], w_out.astype(BF16), tm=tm)
    return h, new


TM_PROMPT = 512
TM_SAMPLE = 256


def kernel(x_prompt, x_sample, p_prompt, p_sample, state_hgrn, cache_nsa_cmp_kv, cache_nsa_slc_kv, state_nsa_win_kv,
           state_dil_kv_0, state_dil_kv_1, state_dil_kv_2, page_table, rel_bias, norm_ffn1, norm_mix, norm_ffn2,
           norm_ple, w_ffn1_in, w_ffn1_out, w_ffn2_in, w_ffn2_out, w_ple_gate, w_ple_proj, w_in_even, w_out_even,
           hgrn_lb_logits, hgrn_norm, nsa_qk_norm, nsa_cmp_pe, nsa_cmp_w1, nsa_cmp_w2, w_in_odd, w_out_odd,
           dil_qk_norm):
    depth = norm_ffn1.shape[0]
    dil_states = (state_dil_kv_0, state_dil_kv_1, state_dil_kv_2)
    bf = lambda w: w.astype(BF16)
    w1i, w1o, w2i, w2o, wpg, wpp = (bf(w) for w in (w_ffn1_in, w_ffn1_out, w_ffn2_in, w_ffn2_out, w_ple_gate,
                                                     w_ple_proj))
    lb_all = jnp.cumsum(jax.nn.softmax(hgrn_lb_logits.astype(F32), axis=0), axis=0)
    g_ffn1, g_ffn2, g_ple = (g.reshape(depth, 1, -1) for g in (norm_ffn1, norm_ffn2, norm_ple))

    def run(x, p, sample):
        batch, seq, d = x.shape
        n = batch * seq
        tm = TM_SAMPLE if sample else TM_PROMPT
        h = x.reshape(n, d)
        p = p.reshape(depth, n, p.shape[-1])
        even, odd = [], []
        for li in range(depth):
            h = _ffn(h, li, g_ffn1, w1i, w1o, tm=tm)
            if li % 2 == 0:
                ei = li // 2
                args = (h, norm_mix[li], w_in_even[ei], w_out_even[ei], lb_all[ei], hgrn_norm[ei], nsa_qk_norm[ei],
                        nsa_cmp_pe[ei], nsa_cmp_w1[ei], nsa_cmp_w2[ei], rel_bias)
                if sample:
                    h, st = _even_mixer_sample(*args, state_hgrn[ei], cache_nsa_cmp_kv[ei], cache_nsa_slc_kv[ei],
                                               state_nsa_win_kv[ei], page_table, batch=batch, tq=seq, tm=tm)
                else:
                    h, st = _even_mixer_prompt(*args, batch=batch, seq=seq, tm=tm)
                even.append(st)
            else:
                oi = li // 2
                args = (h, norm_mix[li], w_in_odd[oi], w_out_odd[oi], dil_qk_norm[oi], rel_bias)
                if sample:
                    h, bufs = _odd_mixer_sample(*args, [s[oi] for s in dil_states], batch=batch, tq=seq, tm=tm)
                else:
                    h, bufs = _odd_mixer_prompt(*args, batch=batch, seq=seq, tm=tm)
                odd.append(bufs)
            h = _ffn(h, li, g_ffn2, w2i, w2o, ple=(p, g_ple, wpg, wpp), tm=tm)
        rows = lambda a: a.reshape(batch, -1, 2, NSA_KV, HEAD_DIM)
        drows = lambda a: a.reshape(batch, -1, 2, DIL_HEADS, HEAD_DIM)
        return (h.reshape(batch, seq, d),
                jnp.stack([s["hgrn"] for s in even]), jnp.stack([rows(s["cmp"]) for s in even]),
                jnp.stack([rows(s["slc"]) for s in even]), jnp.stack([rows(s["win"]) for s in even]),
                jnp.stack([drows(b[0]) for b in odd]), jnp.stack([drows(b[1]) for b in odd]),
                jnp.stack([drows(b[2]) for b in odd]))

    y_p, hg_p, cmp_p, slc_p, win_p, d0_p, d1_p, d2_p = run(x_prompt, p_prompt, False)
    y_s, hg_s, cmp_s, slc_s, win_s, d0_s, d1_s, d2_s = run(x_sample, p_sample, True)
    return (y_p, y_s, hg_p, hg_s, cmp_p, cmp_s, slc_p, slc_s, win_p, win_s, d0_p, d0_s, d1_p, d1_s, d2_p, d2_s)
```

```python
import functools
import math

import numpy as np
import jax
import jax.numpy as jnp
from jax import lax
from jax.experimental import pallas as pl
from jax.experimental.pallas import tpu as pltpu

F32 = jnp.float32
BF16 = jnp.bfloat16
HIGHEST = lax.Precision.HIGHEST

V7X_VMEM_BYTES = 64 * 1024 * 1024
VMEM_LIMIT = V7X_VMEM_BYTES * 7 // 8
LANES = 128

EPS = 1e-6
NEG = -1e30
NEG_TEST = -1e29

HEAD_DIM = 64
HG_HEADS = 8
HG_CHUNK = 128
NSA_HEADS = 12
NSA_KV = 3
NSA_GROUP = NSA_HEADS // NSA_KV
CMP_BLOCK = 32
CMP_STRIDE = 16
SEL_BLOCK = 64
SEL_TOPK = 16
NSA_WINDOW = 512
FORCE_SCORE = 1e4
PAGE = 128
DIL_PATTERN = ((128, 1), (512, 4), (2048, 16))
DIL_HEADS = 4
RP_BUCKETS = 32
RP_MAX_DIST = 2048


def _cparams(*sem):
    return pltpu.CompilerParams(dimension_semantics=sem, vmem_limit_bytes=VMEM_LIMIT)


def _const(shape):
    nd = len(shape)
    return pl.BlockSpec(shape, lambda *_: (0,) * nd, pipeline_mode=pl.Buffered(1))


def _dot(a, b):
    return jnp.dot(a, b, preferred_element_type=F32)


def _dot_nt(a, b):
    return lax.dot_general(a, b, (((1,), (1,)), ((), ())), preferred_element_type=F32)


def _dot_tn(a, b):
    return lax.dot_general(a, b, (((0,), (0,)), ((), ())), preferred_element_type=F32)


def _rms(x, gain):
    return x * lax.rsqrt(jnp.mean(x * x, axis=-1, keepdims=True) + EPS) * gain


def _seg_ms(x, bd):
    sq = x * x
    hi = sq.astype(BF16)
    lo = (sq - hi.astype(F32)).astype(BF16)
    bw = bd.shape[0]
    cols = []
    for c0 in range(0, x.shape[1], bw):
        wd = min(bw, x.shape[1] - c0)
        blk = bd[:wd, :wd]
        cols.append(_dot(hi[:, c0:c0 + wd], blk) + _dot(lo[:, c0:c0 + wd], blk))
    return (cols[0] if len(cols) == 1 else jnp.concatenate(cols, axis=1)) * (1.0 / HEAD_DIM)


def _seg_rms(x, bd, gain):
    return x * lax.rsqrt(_seg_ms(x, bd) + EPS) * gain


def _norm_k_rows(x, bd, gain, kmask):
    bw = bd.shape[0]
    r = lax.rsqrt(_seg_ms(x[:, :bw], bd) + EPS)
    r = jnp.concatenate([r, jnp.ones((x.shape[0], x.shape[1] - bw), F32)], axis=1)
    return x * jnp.where(kmask > 0.5, r, 1.0) * gain


V7X_MXU_WIDTH = 256


def _block_diag_ones(width):
    i = np.arange(min(width, V7X_MXU_WIDTH)) // HEAD_DIM
    return jnp.asarray((i[:, None] == i[None, :]).astype(np.float32), BF16)


def _rel_bucket(dist):
    exact = RP_BUCKETS // 2
    d = jnp.maximum(dist, 0)
    log_ratio = jnp.log(jnp.maximum(d, 1).astype(F32) / exact) / math.log(RP_MAX_DIST / exact)
    large = jnp.minimum(exact + (log_ratio * (RP_BUCKETS - exact)).astype(jnp.int32), RP_BUCKETS - 1)
    return jnp.where(d < exact, d, large)


def _bias_tile(rel_bias, dist, valid):
    onehot = (_rel_bucket(dist)[..., None] == jnp.arange(RP_BUCKETS)).astype(F32)
    t = jnp.einsum('...k,kh->h...', onehot, rel_bias.astype(F32), precision=HIGHEST)
    return jnp.where(valid[None], t, NEG)


def _toeplitz(ext, n, m):
    p = n + m - 1
    u = jnp.concatenate([jnp.flip(ext[..., :m], -1), jnp.flip(ext[..., m:], -1)], axis=-1)
    reps = (1,) * (ext.ndim - 1) + (n,)
    t = jnp.tile(u, reps)[..., :n * (p - 1)].reshape(ext.shape[:-1] + (n, p - 1))
    return t[..., :m]


def _softmax_piece(s, mask, m_prev, l_prev, acc_prev, v, v_t=False):
    m_new = jnp.maximum(m_prev, jnp.max(s, axis=-1, keepdims=True))
    alpha = jnp.exp(m_prev - m_new)
    p = jnp.where(mask, jnp.exp(s - m_new), 0.0)
    l_new = alpha * l_prev + jnp.sum(p, axis=-1, keepdims=True)
    pb = p.astype(BF16)
    acc_new = alpha * acc_prev + (_dot_nt(pb, v) if v_t else _dot(pb, v))
    return m_new, l_new, acc_new


def _ffn_body(x_ref, g_ref, win_ref, wout_ref, *rest, dff, ple):
    x = x_ref[...]
    xn = _rms(x, g_ref[...]).astype(BF16)
    gu = _dot(xn, win_ref[...])
    gg = gu[:, :dff]
    a = (gg * jax.nn.sigmoid(gg) * gu[:, dff:]).astype(BF16)
    h = x + 0.5 * _dot(a, wout_ref[...])
    if ple:
        p_ref, gp_ref, wg_ref, wp_ref, o_ref = rest
        hn = _rms(h, gp_ref[...]).astype(BF16)
        gate = jax.nn.sigmoid(_dot(hn, wg_ref[...]))
        h = h + gate * _dot(p_ref[...].astype(BF16), wp_ref[...])
    else:
        (o_ref,) = rest
    o_ref[...] = h


def _layer(arr, li):
    tail = (0,) * (arr.ndim - 1)
    return pl.BlockSpec((None,) + arr.shape[1:], lambda *_: (li,) + tail, pipeline_mode=pl.Buffered(1))


def _ffn(h, li, gain, w_in, w_out, ple=None, *, tm):
    n, d = h.shape
    dff = w_out.shape[1]
    row = lambda i: (i, 0)
    in_specs = [pl.BlockSpec((tm, d), row), _layer(gain, li), _layer(w_in, li), _layer(w_out, li)]
    args = [h, gain, w_in, w_out]
    if ple is not None:
        p, gp, wg, wp = ple
        in_specs += [pl.BlockSpec((None, tm, p.shape[2]), lambda i: (li, i, 0)), _layer(gp, li), _layer(wg, li),
                     _layer(wp, li)]
        args += [p, gp, wg, wp]
    return pl.pallas_call(
        functools.partial(_ffn_body, dff=dff, ple=ple is not None),
        grid=(n // tm,), in_specs=in_specs, out_specs=pl.BlockSpec((tm, d), row),
        out_shape=jax.ShapeDtypeStruct((n, d), F32), compiler_params=_cparams("parallel"),
        name="ffn_ple" if ple is not None else "ffn",
    )(*args)


EVEN_HG = 4 * HG_HEADS * HEAD_DIM
EVEN_Q = NSA_HEADS * HEAD_DIM
ROWS_NSA = 2 * NSA_KV * HEAD_DIM
EVEN_PAD = EVEN_HG + EVEN_Q + 3 * ROWS_NSA + LANES


GATE_ROWS = 16


def _inproj_even_body(x_ref, g_ref, w_ref, gk_ref, km_ref, bd_ref, hg_ref, q_ref, cmp_ref, *outs, prompt):
    xn = _rms(x_ref[...], g_ref[...]).astype(BF16)
    p = _dot(xn, w_ref[...])
    c0 = EVEN_HG
    c1 = c0 + EVEN_Q
    hg_ref[...] = p[:, :c0]
    q_ref[...] = p[:, c0:c1]
    cmp = p[:, c1:c1 + ROWS_NSA]
    cmp_ref[...] = cmp
    bd = bd_ref[...]
    km = km_ref[...]
    slc = _norm_k_rows(p[:, c1 + ROWS_NSA:c1 + 2 * ROWS_NSA], bd, gk_ref[0:1, :], km)
    win = _norm_k_rows(p[:, c1 + 2 * ROWS_NSA:c1 + 3 * ROWS_NSA], bd, gk_ref[1:2, :], km)
    sig = jax.nn.sigmoid(p[:, c1 + 3 * ROWS_NSA:])
    if not prompt:
        slc_ref, win_ref, gate_ref = outs
        slc_ref[...] = slc
        win_ref[...] = win
        for g in range(NSA_KV):
            gate_ref[g] = sig if g == 0 else pltpu.roll(sig, LANES - 3 * NSA_GROUP * g, 1)
        return
    slcg_ref, wing_ref, gate_t_ref, cmp_t_ref, slc_t_ref, win_t_ref = outs
    kw = NSA_KV * HEAD_DIM
    sig_t = sig.T
    for g in range(NSA_KV):
        ks = slice(HEAD_DIM * g, HEAD_DIM * (g + 1))
        vs = slice(kw + HEAD_DIM * g, kw + HEAD_DIM * (g + 1))
        slcg_ref[g] = jnp.concatenate([slc[:, ks], slc[:, vs]], axis=1)
        wing_ref[g] = jnp.concatenate([win[:, ks], win[:, vs]], axis=1)
        for half in range(sig.shape[0] // NSA_TQ):
            gate_t_ref[g, half] = sig_t[3 * NSA_GROUP * g:3 * NSA_GROUP * g + GATE_ROWS,
                                        half * NSA_TQ:(half + 1) * NSA_TQ]
    for dst_ref, rows in ((cmp_t_ref, cmp), (slc_t_ref, slc), (win_t_ref, win)):
        _store_transposed(dst_ref, rows)


def _store_transposed(dst_ref, x):
    for c in range(x.shape[1] // LANES):
        dst_ref[0, c * LANES:(c + 1) * LANES, :] = x[:, c * LANES:(c + 1) * LANES].T


def _inproj_even(h, gain, w_pad, qk_norm, *, tm, seq=None):
    n, d = h.shape
    kw = NSA_KV * HEAD_DIM
    ones = jnp.ones((kw,), F32)
    gk = jnp.stack([jnp.concatenate([jnp.tile(qk_norm[2], NSA_KV), ones]),
                    jnp.concatenate([jnp.tile(qk_norm[3], NSA_KV), ones])])
    km = jnp.concatenate([ones, 0.0 * ones]).reshape(1, ROWS_NSA)
    row = lambda i: (i, 0)
    grp = lambda i: (0, i, 0)
    shp = lambda *s: jax.ShapeDtypeStruct(s, F32)
    out_specs = [pl.BlockSpec((tm, EVEN_HG), row), pl.BlockSpec((tm, EVEN_Q), row), pl.BlockSpec((tm, ROWS_NSA), row)]
    out_shape = [shp(n, EVEN_HG), shp(n, EVEN_Q), shp(n, ROWS_NSA)]
    if seq is None:
        out_specs += [pl.BlockSpec((tm, ROWS_NSA), row)] * 2 + [pl.BlockSpec((NSA_KV, tm, LANES), grp)]
        out_shape += [shp(n, ROWS_NSA)] * 2 + [shp(NSA_KV, n, LANES)]
    else:
        tpb = seq // tm
        out_specs += [pl.BlockSpec((NSA_KV, tm, LANES), grp)] * 2
        out_shape += [shp(NSA_KV, n, LANES)] * 2
        out_specs.append(pl.BlockSpec((NSA_KV, tm // NSA_TQ, GATE_ROWS, NSA_TQ), lambda i: (0, i, 0, 0)))
        out_shape.append(shp(NSA_KV, n // NSA_TQ, GATE_ROWS, NSA_TQ))
        out_specs += [pl.BlockSpec((1, ROWS_NSA, tm), lambda i: (i // tpb, 0, i % tpb))] * 3
        out_shape += [shp(n // seq, ROWS_NSA, seq)] * 3
    return pl.pallas_call(
        functools.partial(_inproj_even_body, prompt=seq is not None), grid=(n // tm,),
        in_specs=[pl.BlockSpec((tm, d), row), _const((1, d)), _const(w_pad.shape), _const((2, ROWS_NSA)),
                  _const((1, ROWS_NSA)), _const(_block_diag_ones(ROWS_NSA).shape)],
        out_specs=out_specs, out_shape=out_shape,
        compiler_params=_cparams("parallel"), name="inproj_even",
    )(h, gain.reshape(1, d), w_pad, gk, km, _block_diag_ones(ROWS_NSA))


ODD_W = 3 * DIL_HEADS * HEAD_DIM
ODD_GW = DIL_HEADS * HEAD_DIM


def _to_residue_view(src, dst_ref, stage_ref, dil):
    rows, w = src.shape
    for blk in range(w // LANES):
        stage_ref[...] = src[:, blk * LANES:(blk + 1) * LANES]
        for r in range(dil):
            lo = r * w + blk * LANES
            dst_ref[:, lo:lo + LANES] = stage_ref[pl.ds(r, rows // dil, stride=dil), :]


def _from_residue_view(src_ref, stage_ref, dil, w):
    rows = src_ref.shape[0] * dil
    parts = []
    for blk in range(w // LANES):
        for r in range(dil):
            lo = r * w + blk * LANES
            stage_ref[pl.ds(r, rows // dil, stride=dil), :] = src_ref[:, lo:lo + LANES]
        parts.append(stage_ref[...])
    return jnp.concatenate(parts, axis=1)


def _inproj_odd_body(x_ref, g_ref, w_ref, gq_ref, gk_ref, bd_ref, *outs, prompt):
    xn = _rms(x_ref[...], g_ref[...]).astype(BF16)
    p = _dot(xn, w_ref[...])
    bd = bd_ref[...]
    q = _seg_rms(p[:, :ODD_W], bd, gq_ref[...]) * (HEAD_DIM ** -0.5)
    k = _seg_rms(p[:, ODD_W:2 * ODD_W], bd, gk_ref[...])
    v = p[:, 2 * ODD_W:]
    for g in range(3):
        s = slice(ODD_GW * g, ODD_GW * (g + 1))
        kv = jnp.concatenate([k[:, s], v[:, s]], axis=1)
        if not prompt:
            outs[0][g] = q[:, s]
            outs[1][g] = kv
            continue
        dil = DIL_PATTERN[g][1]
        if dil == 1:
            outs[2 * g][...] = q[:, s]
            outs[2 * g + 1][...] = kv
        else:
            _to_residue_view(q[:, s], outs[2 * g], outs[-1], dil)
            _to_residue_view(kv, outs[2 * g + 1], outs[-1], dil)
        _store_transposed(outs[6 + g], kv)


def _inproj_odd(h, gain, w, qk_norm, *, tm, seq=None):
    n, d = h.shape
    row = lambda i: (i, 0)
    nh = 3 * DIL_HEADS
    prompt = seq is not None
    if not prompt:
        out_specs = [pl.BlockSpec((3, tm, ODD_GW), lambda i: (0, i, 0)),
                     pl.BlockSpec((3, tm, 2 * ODD_GW), lambda i: (0, i, 0))]
        out_shape = [jax.ShapeDtypeStruct((3, n, ODD_GW), F32), jax.ShapeDtypeStruct((3, n, 2 * ODD_GW), F32)]
    else:
        out_specs, out_shape = [], []
        for _, dil in DIL_PATTERN:
            assert dil == 1 or DIL_PATTERN[0][1] == 1
            for width in (ODD_GW, 2 * ODD_GW):
                out_specs.append(pl.BlockSpec((tm // dil, dil * width), row))
                out_shape.append(jax.ShapeDtypeStruct((n // dil, dil * width), F32))
        tpb = seq // tm
        for _ in DIL_PATTERN:
            out_specs.append(pl.BlockSpec((1, 2 * ODD_GW, tm), lambda i: (i // tpb, 0, i % tpb)))
            out_shape.append(jax.ShapeDtypeStruct((n // seq, 2 * ODD_GW, seq), F32))
    return pl.pallas_call(
        functools.partial(_inproj_odd_body, prompt=prompt), grid=(n // tm,),
        in_specs=[pl.BlockSpec((tm, d), row), _const((1, d)), _const(w.shape), _const((1, ODD_W)),
                  _const((1, ODD_W)), _const(_block_diag_ones(ODD_W).shape)],
        out_specs=out_specs, out_shape=out_shape,
        scratch_shapes=[pltpu.VMEM((tm, LANES), F32)] if prompt else [],
        compiler_params=_cparams("parallel"), name="inproj_odd",
    )(h, gain.reshape(1, d), w, jnp.tile(qk_norm[0], nh).reshape(1, ODD_W),
      jnp.tile(qk_norm[1], nh).reshape(1, ODD_W), _block_diag_ones(ODD_W))


def _outproj2_body(h_ref, a_ref, b_ref, w_ref, o_ref):
    ka = a_ref.shape[1]
    o_ref[...] = (h_ref[...] + _dot(a_ref[...].astype(BF16), w_ref[:ka])
                  + _dot(b_ref[...].astype(BF16), w_ref[ka:]))


def _outproj2(h, a, b, w, *, tm):
    n, d = h.shape
    row = lambda i: (i, 0)
    return pl.pallas_call(
        _outproj2_body, grid=(n // tm,),
        in_specs=[pl.BlockSpec((tm, d), row), pl.BlockSpec((tm, a.shape[1]), row),
                  pl.BlockSpec((tm, b.shape[1]), row), _const(w.shape)],
        out_specs=pl.BlockSpec((tm, d), row), out_shape=jax.ShapeDtypeStruct((n, d), F32),
        compiler_params=_cparams("parallel"), name="outproj_even",
    )(h, a, b, w)


def _outproj2t_body(h_ref, a_ref, bt_ref, w_ref, o_ref):
    ka = a_ref.shape[1]
    acc = h_ref[...] + _dot(a_ref[...].astype(BF16), w_ref[:ka])
    tq = bt_ref.shape[2]
    for i in range(bt_ref.shape[0]):
        rs = slice(i * tq, (i + 1) * tq)
        o_ref[rs, :] = acc[rs, :] + _dot_tn(bt_ref[i].astype(BF16), w_ref[ka:])


def _outproj2t(h, a, bt, w, *, tm):
    n, d = h.shape
    row = lambda i: (i, 0)
    tq = bt.shape[2]
    return pl.pallas_call(
        _outproj2t_body, grid=(n // tm,),
        in_specs=[pl.BlockSpec((tm, d), row), pl.BlockSpec((tm, a.shape[1]), row),
                  pl.BlockSpec((tm // tq, bt.shape[1], tq), lambda i: (i, 0, 0)), _const(w.shape)],
        out_specs=pl.BlockSpec((tm, d), row), out_shape=jax.ShapeDtypeStruct((n, d), F32),
        compiler_params=_cparams("parallel"), name="outproj_even_t",
    )(h, a, bt, w)


def _outproj_merge_body(h_ref, o0_ref, o1_ref, o2_ref, l0_ref, l1_ref, l2_ref, w_ref, o_ref, *stage, views):
    if views:
        load = lambda ref, g: (ref[...] if DIL_PATTERN[g][1] == 1
                               else _from_residue_view(ref, stage[0], DIL_PATTERN[g][1], ODD_GW))
    else:
        load = lambda ref, g: ref[...]
    mixed = _merge_groups([load(r, g) for g, r in enumerate((o0_ref, o1_ref, o2_ref))],
                          [load(r, g) for g, r in enumerate((l0_ref, l1_ref, l2_ref))])
    o_ref[...] = h_ref[...] + _dot(mixed.astype(BF16), w_ref[...])


def _outproj_merge(h, outs, lses, w, *, tm, views=False):
    n, d = h.shape
    row = lambda i: (i, 0)
    parts = [pl.BlockSpec((tm * a.shape[0] // n, a.shape[1]), row) for a in list(outs) + list(lses)]
    return pl.pallas_call(
        functools.partial(_outproj_merge_body, views=views), grid=(n // tm,),
        in_specs=[pl.BlockSpec((tm, d), row)] + parts + [_const(w.shape)],
        out_specs=pl.BlockSpec((tm, d), row), out_shape=jax.ShapeDtypeStruct((n, d), F32),
        scratch_shapes=[pltpu.VMEM((tm, LANES), F32)] if views else [],
        compiler_params=_cparams("parallel"), name="outproj_odd",
    )(h, *outs, *lses, w)


HG_MM_LEVELS = 2


def _hgrn_prefix_matrix(C):
    t = np.arange(C)[:, None]
    j = np.arange(C)[None, :]
    mats = [j <= t]
    for lvl in range(1, HG_MM_LEVELS + 1):
        if (1 << lvl) <= C:
            mats.append(j <= ((t >> lvl) << lvl) + (1 << lvl) // 2 - 1)
    return jnp.asarray(np.concatenate(mats, axis=0).astype(np.float32), BF16)


def _hgrn_body(hq_ref, hf_ref, hi_ref, hgt_ref, lb_ref, gn_ref, bd_ref, pm_ref, s0_ref, o_ref, sfin_ref, st_ref,
               *b_scr, C, U):
    c = pl.program_id(1)

    @pl.when(c == 0)
    def _():
        st_ref[...] = s0_ref[0]

    nb = len(b_scr) // U
    for u in range(U):
        _hgrn_chunk(hq_ref, hf_ref, hi_ref, hgt_ref, lb_ref, gn_ref, bd_ref, pm_ref, o_ref, st_ref,
                    b_scr[u * nb:(u + 1) * nb], slice(u * C, (u + 1) * C), C)

    @pl.when(c == pl.num_programs(1) - 1)
    def _():
        sfin_ref[0] = st_ref[...]


def _hgrn_chunk(hq_ref, hf_ref, hi_ref, hgt_ref, lb_ref, gn_ref, bd_ref, pm_ref, o_ref, st_ref, b_scr, rs, C):
    dk = HEAD_DIM
    w = HG_HEADS * dk
    lb = lb_ref[...]
    f = lb + (1.0 - lb) * jax.nn.sigmoid(hf_ref[rs, :])
    lf = jnp.log(f)
    kk = 1.0 - f
    ti = lax.broadcasted_iota(jnp.int32, (C, C), 0)
    si = lax.broadcasted_iota(jnp.int32, (C, C), 1)
    p1 = lf.astype(BF16)
    r1 = lf - p1.astype(F32)
    p2 = r1.astype(BF16)
    p3 = (r1 - p2.astype(F32)).astype(BF16)
    sums = _dot(pm_ref[...], jnp.concatenate([p1, p2, p3], axis=1))
    sums = sums[:, :w] + sums[:, w:2 * w] + sums[:, 2 * w:]
    b = sums[0:C]
    for c4, scr in enumerate(b_scr):
        scr[...] = b[:, c4 * LANES:(c4 + 1) * LANES]
    q = hq_ref[rs, :]
    v = hi_ref[rs, :].astype(BF16)
    heads = range(HG_HEADS)
    hs = [slice(dk * h, dk * (h + 1)) for h in heads]

    qb = q.astype(BF16)
    kb = kk.astype(BF16)
    a = [jnp.where(ti == si, _dot_nt(qb[:, hs[h]], kb[:, hs[h]]), 0.0) for h in heads]
    lvl = 1
    while (1 << lvl) <= C:
        m = 1 << lvl
        half = m // 2
        if lvl <= HG_MM_LEVELS:
            r = sums[lvl * C:(lvl + 1) * C]
        else:
            ref_rows = pl.ds(half - 1, C // m, stride=m)
            r = jnp.concatenate([jnp.broadcast_to(scr[ref_rows, :][:, None, :], (C // m, m, LANES)).reshape(C, LANES)
                                 for scr in b_scr], axis=1)
        e = jnp.exp(-jnp.abs(b - r))
        qt = (q * e).astype(BF16)
        kt = (kk * e).astype(BF16)
        pair = ((ti >> lvl) == (si >> lvl)) & ((ti & (m - 1)) >= half) & ((si & (m - 1)) < half)
        a = [a[h] + jnp.where(pair, _dot_nt(qt[:, hs[h]], kt[:, hs[h]]), 0.0) for h in heads]
        lvl += 1

    b_last = b[C - 1:C, :]
    qe = (q * jnp.exp(b)).astype(BF16)
    kdec = (kk * jnp.exp(b_last - b)).astype(BF16)
    e_last = jnp.exp(b_last)
    outs = []
    for h in heads:
        st = st_ref[h]
        outs.append(_dot(a[h].astype(BF16), v[:, hs[h]]) + _dot_nt(qe[:, hs[h]], st.astype(BF16)))
        st_ref[h] = st * e_last[:, hs[h]] + _dot_tn(v[:, hs[h]], kdec[:, hs[h]])
    o = jnp.concatenate(outs, axis=1)
    gt = hgt_ref[rs, :]
    o_ref[rs, :] = _seg_rms(o, bd_ref[...], gn_ref[...]) * (gt * jax.nn.sigmoid(gt))


HG_CHUNKS_PER_STEP = 2


def _hgrn(hg, lb, hg_norm, s0, *, batch, seq):
    C = math.gcd(seq, HG_CHUNK)
    U = math.gcd(seq // C, HG_CHUNKS_PER_STEP)
    nc = seq // (C * U)
    w = HG_HEADS * HEAD_DIM
    col = lambda j: pl.BlockSpec((U * C, w), lambda b, c: (b * nc + c, j))
    st_spec = pl.BlockSpec((1, HG_HEADS, HEAD_DIM, HEAD_DIM), lambda b, c: (b, 0, 0, 0))
    pm = _hgrn_prefix_matrix(C)
    o, sfin = pl.pallas_call(
        functools.partial(_hgrn_body, C=C, U=U), grid=(batch, nc),
        in_specs=[col(0), col(1), col(2), col(3), _const((1, w)), _const((1, w)), _const(_block_diag_ones(w).shape), _const(pm.shape),
                  st_spec],
        out_specs=[pl.BlockSpec((U * C, w), lambda b, c: (b * nc + c, 0)), st_spec],
        out_shape=[jax.ShapeDtypeStruct((batch * seq, w), F32),
                   jax.ShapeDtypeStruct((batch, HG_HEADS, HEAD_DIM, HEAD_DIM), F32)],
        scratch_shapes=[pltpu.VMEM((HG_HEADS, HEAD_DIM, HEAD_DIM), F32)]
        + [pltpu.VMEM((C, LANES), F32)] * (U * w // LANES),
        compiler_params=_cparams("parallel", "arbitrary"), name="hgrn_scan",
    )(hg, hg, hg, hg, lb.reshape(1, w), jnp.tile(hg_norm, HG_HEADS).reshape(1, w), _block_diag_ones(w), pm,
      jnp.swapaxes(s0, 2, 3))
    return o, jnp.swapaxes(sfin, 2, 3)


CHUNKS_PER_PAGE = PAGE // CMP_STRIDE
PAGE_LANES = CMP_STRIDE * ROWS_NSA
CMP_HID = 2 * NSA_KV * 2 * HEAD_DIM
CMP_PP = 16


def _compress_body(pt_ref, *refs, n_chunks, transposed):
    pages = refs[:CMP_PP]
    wbig_ref, pe_ref, w2_ref, gk_ref, km_ref, bd_ref, perm_ref, rows_ref, grp_ref, xs_ref = refs[CMP_PP:]
    nb = ROWS_NSA // LANES
    j = pl.program_id(1)
    perm = perm_ref[...]
    for i in range(0, CMP_PP, 2):
        pair = []
        for k in range(2):
            pg = pages[i + k][0]
            if transposed:
                pg = jnp.concatenate([pg[c * LANES:(c + 1) * LANES, :].T for c in range(nb)], axis=1)
            pair.append(_dot(perm, pg.astype(BF16)))
        r0 = pl.multiple_of((j * CMP_PP + i) * CHUNKS_PER_PAGE, 2 * CHUNKS_PER_PAGE)
        for s in range(CMP_STRIDE):
            rs = slice(s * CHUNKS_PER_PAGE, (s + 1) * CHUNKS_PER_PAGE)
            both = jnp.concatenate([pair[0][rs], pair[1][rs]], axis=0).astype(BF16)
            for c in range(nb):
                lo = (c * CMP_STRIDE + s) * LANES
                xs_ref[pl.ds(r0, 2 * CHUNKS_PER_PAGE), lo:lo + LANES] = both[:, c * LANES:(c + 1) * LANES]

    @pl.when(j == pl.num_programs(1) - 1)
    def _():
        blk_w = CMP_STRIDE * LANES
        first = lambda x: jnp.concatenate(
            [_dot(x[:, c * blk_w:(c + 1) * blk_w], wbig_ref[c]) for c in range(nb)], axis=1)
        y = first(xs_ref[...])
        ysh = pltpu.roll(y, n_chunks - 1, 0)
        hid = y + pltpu.roll(ysh, CMP_HID - HEAD_DIM, 1)
        cf = first(pe_ref[...].astype(BF16))
        lane = lax.broadcasted_iota(jnp.int32, (8, CMP_HID), 1)
        cs = jnp.where(((lane >> 6) & 1) == 0, jnp.broadcast_to(cf[0:1], (8, CMP_HID)),
                       jnp.broadcast_to(cf[1:2], (8, CMP_HID)))
        cs = cs + pltpu.roll(cs, CMP_HID - HEAD_DIM, 1)
        hid = hid + cs[0:1]
        act = (hid * jax.nn.sigmoid(hid)).astype(BF16)
        out = _dot(act, w2_ref[...])
        out = _norm_k_rows(out, bd_ref[...], gk_ref[...], km_ref[...])
        rows_ref[0] = out
        kw = NSA_KV * HEAD_DIM
        for g in range(NSA_KV):
            grp_ref[0, g] = jnp.concatenate([out[:, HEAD_DIM * g:HEAD_DIM * (g + 1)],
                                             out[:, kw + HEAD_DIM * g:kw + HEAD_DIM * (g + 1)]], axis=1)


def _compress_weights(cmp_pe, cmp_w1, cmp_w2, k_gain):
    ratio = CMP_BLOCK // CMP_STRIDE
    w1 = cmp_w1.reshape(2, ratio, CMP_STRIDE, HEAD_DIM, HEAD_DIM)
    eye_kv = jnp.eye(2, dtype=F32)
    eye_g = jnp.eye(NSA_KV, dtype=F32)
    nb = ROWS_NSA // LANES
    member_kv = np.arange(2 * NSA_KV) // NSA_KV
    w1m = w1[member_kv].reshape(nb, 2, ratio, CMP_STRIDE, HEAD_DIM, HEAD_DIM)
    wbig = jnp.einsum('chrsde,hH->cshdHre', w1m, eye_kv).reshape(nb, CMP_STRIDE * LANES, 2 * ratio * HEAD_DIM)
    wbig = wbig.astype(BF16)
    pe = cmp_pe.reshape(2, ratio, CMP_STRIDE, HEAD_DIM)
    pe_rows = pe[member_kv].reshape(nb, 2, ratio, CMP_STRIDE, HEAD_DIM).transpose(2, 0, 3, 1, 4)
    pe_rows = pe_rows.reshape(ratio, PAGE_LANES)
    pe_rows = jnp.concatenate([pe_rows, jnp.zeros((8 - ratio, PAGE_LANES), F32)], axis=0)
    sel_r0 = jnp.asarray([1.0, 0.0], F32)
    w2big = jnp.einsum('ked,kK,gG,r->kgreKGd', cmp_w2, eye_kv, eye_g, sel_r0).reshape(CMP_HID, ROWS_NSA).astype(BF16)
    kw = NSA_KV * HEAD_DIM
    gk = jnp.concatenate([jnp.tile(k_gain, NSA_KV), jnp.ones((kw,), F32)]).reshape(1, ROWS_NSA)
    km = jnp.concatenate([jnp.ones((kw,), F32), jnp.zeros((kw,), F32)]).reshape(1, ROWS_NSA)
    return wbig, pe_rows, w2big, gk, km


def _compress(pool, table, cw, *, transposed):
    wbig, pe_rows, w2big, gk, km = cw
    bsz, npg = table.shape
    n_chunks = npg * CHUNKS_PER_PAGE
    page = lambda i: pl.BlockSpec((1,) + pool.shape[1:], lambda b, j, pt: (pt[b, j * CMP_PP + i], 0, 0))
    tok = np.arange(PAGE)
    perm = (tok[None, :] == (tok[:, None] % CHUNKS_PER_PAGE) * CMP_STRIDE + tok[:, None] // CHUNKS_PER_PAGE)
    perm = jnp.asarray(perm.astype(np.float32), BF16)
    return pl.pallas_call(
        functools.partial(_compress_body, n_chunks=n_chunks, transposed=transposed),
        grid_spec=pltpu.PrefetchScalarGridSpec(
            num_scalar_prefetch=1, grid=(bsz, npg // CMP_PP),
            in_specs=[page(i) for i in range(CMP_PP)] + [
                _const(wbig.shape), _const(pe_rows.shape), _const(w2big.shape), _const((1, ROWS_NSA)),
                _const((1, ROWS_NSA)), _const(_block_diag_ones(ROWS_NSA).shape), _const(perm.shape)],
            out_specs=[pl.BlockSpec((1, n_chunks, ROWS_NSA), lambda b, j, pt: (b, 0, 0)),
                       pl.BlockSpec((1, NSA_KV, n_chunks, LANES), lambda b, j, pt: (b, 0, 0, 0))],
            scratch_shapes=[pltpu.VMEM((n_chunks, PAGE_LANES), BF16)]),
        out_shape=[jax.ShapeDtypeStruct((bsz, n_chunks, ROWS_NSA), F32),
                   jax.ShapeDtypeStruct((bsz, NSA_KV, n_chunks, LANES), F32)],
        compiler_params=_cparams("parallel", "arbitrary"), name="nsa_compress",
    )(table, *([pool] * CMP_PP), wbig, pe_rows, w2big, gk, km, _block_diag_ones(ROWS_NSA), perm)


def _select_blocks(imp, pos, n_slc, k_sel):
    lane = lax.broadcasted_iota(jnp.int32, imp.shape, 1)
    cur = pos >> 6
    forced = (lane == 0) | (lane == cur) | (lane == cur - 1)
    score = jnp.where(forced, FORCE_SCORE, jnp.where(lane <= cur, imp, -1.0))
    rank = jnp.zeros(imp.shape, F32)
    for i in range(n_slc):
        ci = score[:, i:i + 1]
        rank = rank + jnp.where(ci > score, 1.0, jnp.where(ci == score, jnp.where(lane > i, 1.0, 0.0), 0.0))
    return jnp.where(lane < n_slc, jnp.where(rank < k_sel, 1.0, 0.0), 0.0)


NSA_TQ = 256
NSA_QC = LANES


def _select_blocks_t(imp, pos, n_slc, k_sel):
    blk = lax.broadcasted_iota(jnp.int32, imp.shape, 0)
    cur = pos >> 6
    forced = (blk == 0) | (blk == cur) | (blk == cur - 1)
    score = jnp.where(forced, FORCE_SCORE, jnp.where(blk <= cur, imp, -1.0))
    rank = jnp.zeros(imp.shape, F32)
    for i in range(n_slc):
        ci = score[i:i + 1, :]
        rank = rank + jnp.where(ci > score, 1.0, jnp.where(ci == score, jnp.where(blk > i, 1.0, 0.0), 0.0))
    return jnp.where(rank < k_sel, 1.0, 0.0)


def _nsa_prompt_t_body(q_ref, gate_ref, kcv_ref, slc_ref, win_ref, gq_ref, bd_ref, eye_ref, fs_ref, fc_ref,
                       ov_ref, ex_ref, o_ref, bc_ref, bs_ref, bw_ref, qt_scr, ks_scr, kw_scr, vs_scr, vw_scr, sel_scr, m_scr, l_scr, acc_scr,
                       *, seq, n_slc, k_sel):
    tq = NSA_TQ
    qc = NSA_QC
    hd = HEAD_DIM
    n_tiles = seq // tq
    width = NSA_GROUP * tq
    n_chunks = width // qc
    eye = eye_ref[...]
    n_c = kcv_ref.shape[2]
    far = NSA_WINDOW // tq

    @pl.when(pl.program_id(1) == 0)
    def _():
        key_i = lax.broadcasted_iota(jnp.int32, (tq, tq), 0)
        qry_i = lax.broadcasted_iota(jnp.int32, (tq, tq), 1)
        blk_i = lax.broadcasted_iota(jnp.int32, (n_c, tq), 0)
        n_cmp = seq // CMP_STRIDE - CMP_BLOCK // CMP_STRIDE + 1
        span = tq + CMP_STRIDE * n_c
        for h in range(NSA_GROUP):
            hs = slice(h * tq, (h + 1) * tq)
            for d in range(n_tiles):
                u = jnp.broadcast_to(fs_ref[0, h:h + 1, d * tq:(d + 2) * tq], (tq, 2 * tq))
                tile = pltpu.roll(u, tq, 1, stride=1, stride_axis=0)[:, :tq]
                bs_ref[d, :, hs] = tile
                if d == far:
                    bw_ref[0, :, hs] = jnp.where(qry_i <= key_i, tile, NEG)
            for t in range(n_tiles):
                u = jnp.broadcast_to(fc_ref[0, h:h + 1, t * tq:t * tq + span], (n_c, span))
                tile = pltpu.roll(u, tq, 1, stride=CMP_STRIDE, stride_axis=0)[:, :tq]
                bc_ref[t, :, hs] = jnp.where(blk_i < n_cmp, tile, NEG)

    qn = (_seg_rms(q_ref[...], bd_ref[...], gq_ref[...]) * (hd ** -0.5)).astype(BF16)
    for t in range(n_tiles):
        rs = slice(t * tq, (t + 1) * tq)
        for h in range(NSA_GROUP):
            qt_scr[t, :, h * tq:(h + 1) * tq] = _dot_nt(eye, qn[rs, h * hd:(h + 1) * hd]).astype(BF16)
        for kv_ref, k_scr, v_scr in ((slc_ref, ks_scr, vs_scr), (win_ref, kw_scr, vw_scr)):
            kv = kv_ref[0, rs, :].astype(BF16)
            k_scr[t] = kv[:, :hd]
            v_scr[t] = _dot_nt(eye, kv[:, hd:]).astype(BF16)
    kc = kcv_ref[0, 0, :, :hd].astype(BF16)
    vct = _dot_nt(eye, kcv_ref[0, 0, :, hd:].astype(BF16)).astype(BF16)
    def flash_init():
        m_scr[...] = jnp.full((1, width), NEG, F32)
        l_scr[...] = jnp.zeros((1, width), F32)
        acc_scr[...] = jnp.zeros((hd, width), F32)

    def flash_update(k_scr, v_scr, t, tiles, running):
        s_alls = [_dot(k_scr[jt], qt_scr[t]) for jt, _, _ in tiles]
        if running:
            m_all = m_scr[...]
            l_all = l_scr[...]
        alphas, ls, ms, ps = [], [], [], [[] for _ in tiles]
        for c in range(n_chunks):
            cs = slice(c * qc, (c + 1) * qc)
            qoff = (c * qc) % tq
            ss = []
            for (_, bias_tile, sel_add), s_all in zip(tiles, s_alls):
                s = s_all[:, cs] + bias_tile(cs)
                ss.append(s if sel_add is None else s + sel_add[:, qoff:qoff + qc])
            m_new = functools.reduce(jnp.maximum, [jnp.max(s, axis=0, keepdims=True) for s in ss])
            if running:
                m_new = jnp.maximum(m_all[:, cs], m_new)
                alpha = jnp.exp(m_all[:, cs] - m_new)
                alphas.append(alpha)
            l_new = alpha * l_all[:, cs] if running else 0.0
            for k, s in enumerate(ss):
                p = jnp.exp(s - m_new)
                l_new = l_new + jnp.sum(p, axis=0, keepdims=True)
                ps[k].append(p.astype(BF16))
            ls.append(l_new)
            ms.append(m_new)
        cat = lambda xs: jnp.concatenate(xs, axis=1)
        pv = sum(_dot(v_scr[jt], cat(ps[k])) for k, (jt, _, _) in enumerate(tiles))
        acc_scr[...] = cat(alphas) * acc_scr[...] + pv if running else pv
        m_scr[...] = cat(ms)
        l_scr[...] = cat(ls)

    def flash_result():
        return acc_scr[...] / jnp.maximum(l_scr[...], 1e-30)

    def q_tile(t, carry):
        bias_c = bc_ref[t]
        s_c = _dot(kc, qt_scr[t]) + bias_c
        m_c = jnp.max(s_c, axis=0, keepdims=True)
        e_c = jnp.where(bias_c > NEG_TEST, jnp.exp(s_c - m_c), 0.0)
        p_c = (e_c / jnp.maximum(jnp.sum(e_c, axis=0, keepdims=True), 1e-30)).astype(BF16)
        o_c = _dot(vct, p_c)
        imp = sum(_dot(ov_ref[...], p_c[:, h * tq:(h + 1) * tq]) for h in range(NSA_GROUP))
        pos = t * tq + lax.broadcasted_iota(jnp.int32, (1, tq), 1)
        sel = _select_blocks_t(imp[:n_slc], pos, n_slc, k_sel)
        sel_scr[...] = jnp.zeros(sel_scr.shape, BF16)
        sel_scr[0:n_slc, :] = ((1.0 - sel) * NEG).astype(BF16)
        slc_tile = lambda jt: (jt, lambda cs: bs_ref[t - jt, :, cs], _dot(ex_ref[jt], sel_scr[...]))
        for j0 in range(0, t + 1, 2):
            flash_update(ks_scr, vs_scr, t, [slc_tile(j) for j in range(j0, min(j0 + 2, t + 1))], j0 > 0)
        o_s = flash_result()
        far = NSA_WINDOW // tq
        win_tile = lambda d: (t - d, (lambda cs: bw_ref[0, :, cs]) if d == far else (lambda cs: bs_ref[d, :, cs]),
                              None)
        flash_update(kw_scr, vw_scr, t, [win_tile(d) for d in range(min(t, far), -1, -1)], False)
        o_w = flash_result()
        gates = gate_ref[0, t]
        for h in range(NSA_GROUP):
            hs = slice(h * tq, (h + 1) * tq)
            o_ref[t, h * hd:(h + 1) * hd, :] = (gates[3 * h:3 * h + 1] * o_c[:, hs]
                                                + gates[3 * h + 1:3 * h + 2] * o_s[:, hs]
                                                + gates[3 * h + 2:3 * h + 3] * o_w[:, hs])
        return carry

    for t in range(n_tiles):
        q_tile(t, 0)


def _nsa_prompt_t_tables(rel_bias, seq, n_c):
    tq = NSA_TQ
    n_tiles = seq // tq
    n_slc = -(-seq // SEL_BLOCK)
    ds = jnp.arange((n_tiles + 1) * tq) - tq
    fs = _bias_tile(rel_bias, ds, ds >= 0).reshape(NSA_KV, NSA_GROUP, -1)
    dc = jnp.arange(n_tiles * tq + CMP_STRIDE * n_c) - (CMP_STRIDE * n_c + CMP_BLOCK - 1)
    fc = _bias_tile(rel_bias, dc, dc >= 0).reshape(NSA_KV, NSA_GROUP, -1)
    n_cmp = seq // CMP_STRIDE - CMP_BLOCK // CMP_STRIDE + 1
    c_lo = np.arange(n_c) * CMP_STRIDE
    s_lo = np.arange(LANES) * SEL_BLOCK
    ov = ((c_lo[None, :] <= s_lo[:, None] + SEL_BLOCK - 1) & (c_lo[None, :] + CMP_BLOCK - 1 >= s_lo[:, None])
          & (np.arange(n_c) < n_cmp)[None, :] & (np.arange(LANES) < n_slc)[:, None])
    ex = (np.arange(seq)[:, None] // SEL_BLOCK == np.arange(LANES)[None, :]).reshape(n_tiles, tq, LANES)
    as_bf = lambda a: jnp.asarray(a.astype(np.float32), BF16)
    return fs, fc, as_bf(ov), as_bf(ex), n_slc


def _nsa_prompt_t(q, gates_t, kcv, slcg, wing, q_gain, tables, *, batch, seq):
    fs, fc, ov, ex, n_slc = tables
    n_c = kcv.shape[2]
    tq = NSA_TQ
    n_tiles = seq // tq
    assert n_tiles > NSA_WINDOW // tq
    gw = NSA_GROUP * HEAD_DIM
    width = NSA_GROUP * tq
    per_g = lambda shape: pl.BlockSpec((1,) + shape, lambda g, b: (g,) + (0,) * len(shape),
                                       pipeline_mode=pl.Buffered(1))
    eye = jnp.eye(HEAD_DIM, dtype=BF16)
    return pl.pallas_call(
        functools.partial(_nsa_prompt_t_body, seq=seq, n_slc=n_slc, k_sel=min(SEL_TOPK, n_slc)),
        grid=(NSA_KV, batch),
        in_specs=[pl.BlockSpec((seq, gw), lambda g, b: (b, g)),
                  pl.BlockSpec((1, n_tiles, 16, tq), lambda g, b: (g, b, 0, 0)),
                  pl.BlockSpec((1, 1, n_c, LANES), lambda g, b: (b, g, 0, 0)),
                  pl.BlockSpec((1, seq, LANES), lambda g, b: (g, b, 0)),
                  pl.BlockSpec((1, seq, LANES), lambda g, b: (g, b, 0)),
                  _const((1, gw)), _const(_block_diag_ones(gw).shape), _const(eye.shape),
                  per_g(fs.shape[1:]), per_g(fc.shape[1:]), _const(ov.shape), _const(ex.shape)],
        out_specs=pl.BlockSpec((n_tiles, gw, tq), lambda g, b: (b, g, 0)),
        out_shape=jax.ShapeDtypeStruct((batch * n_tiles, NSA_KV * gw, tq), F32),
        scratch_shapes=[pltpu.VMEM((n_tiles, n_c, width), F32), pltpu.VMEM((n_tiles, tq, width), F32),
                        pltpu.VMEM((1, tq, width), F32), pltpu.VMEM((n_tiles, HEAD_DIM, width), BF16),
                        pltpu.VMEM((n_tiles, tq, HEAD_DIM), BF16), pltpu.VMEM((n_tiles, tq, HEAD_DIM), BF16),
                        pltpu.VMEM((n_tiles, HEAD_DIM, tq), BF16), pltpu.VMEM((n_tiles, HEAD_DIM, tq), BF16),
                        pltpu.VMEM((LANES, tq), BF16), pltpu.VMEM((1, width), F32), pltpu.VMEM((1, width), F32),
                        pltpu.VMEM((HEAD_DIM, width), F32)],
        compiler_params=_cparams("arbitrary", "arbitrary"), name="nsa_prompt",
    )(q, gates_t, kcv, slcg, wing, jnp.tile(q_gain, NSA_GROUP).reshape(1, gw), _block_diag_ones(gw), eye,
      fs, fc, ov, ex)


SMP_PP = 16
NSA_KW = NSA_KV * HEAD_DIM
SEL_LANES = 2 * LANES


def _nsa_sample_body(pt_ref, q_ref, gate_ref, ckv_ref, *refs, n_slc, k_sel, past, tq):
    pages = refs[:SMP_PP]
    (slcn_ref, wst_ref, winn_ref, gq_ref, bd_ref, bc_ref, bs_ref, bsn_ref, bw_ref, bwn_ref, ov_ref, ex_ref, exn_ref,
     o_ref, nwin_ref, qbd_scr, sel_scr, kbuf_scr, vbuf_scr, s_scr, sn_scr, vn_scr, oc_scr, ow_scr,
     pad_scr) = refs[SMP_PP:]
    j = pl.program_id(1)
    hd = HEAD_DIM
    kw = NSA_KW
    rows = NSA_HEADS * tq
    init = (jnp.full((rows, 1), NEG, F32), jnp.zeros((rows, 1), F32), jnp.zeros((rows, kw), F32))

    @pl.when(j == 0)
    def _():
        qn = _seg_rms(q_ref[...], bd_ref[...], gq_ref[...]) * (hd ** -0.5)
        zero = jnp.zeros((tq, hd), F32)
        blocks = []
        for h in range(NSA_HEADS):
            qh = qn[:, hd * h:hd * (h + 1)]
            blocks.append(jnp.concatenate([qh if g == h // NSA_GROUP else zero for g in range(NSA_KV)], axis=1))
        qbd = jnp.concatenate(blocks, axis=0).astype(BF16)
        qbd_scr[...] = qbd
        ckv = ckv_ref[0]
        bias_c = bc_ref[...]
        ok_c = bias_c > NEG_TEST
        s_c = _dot_nt(qbd, ckv[:, :kw].astype(BF16)) + bias_c
        m_c = jnp.max(s_c, axis=-1, keepdims=True)
        e_c = jnp.where(ok_c, jnp.exp(s_c - m_c), 0.0)
        p_c = (e_c / jnp.maximum(jnp.sum(e_c, axis=-1, keepdims=True), 1e-30)).astype(BF16)
        oc_scr[...] = _dot(p_c, ckv[:, kw:].astype(BF16))
        d = _dot(p_c, ov_ref[...])
        gr = NSA_GROUP * tq
        imp = jnp.concatenate(
            [sum(d[g * gr + h * tq:g * gr + (h + 1) * tq] for h in range(NSA_GROUP)) for g in range(NSA_KV)], axis=0)
        pos = past + (lax.broadcasted_iota(jnp.int32, (NSA_KV * tq, 1), 0) & (tq - 1))
        sel = _select_blocks(imp, pos, n_slc, k_sel)
        sel = jnp.concatenate([sel[g * tq:(g + 1) * tq] for g in range(NSA_KV) for _ in range(NSA_GROUP)], axis=0)
        sel = ((1.0 - sel) * NEG).astype(BF16)
        sel_scr[...] = sel
        pad_scr[...] = jnp.zeros(pad_scr.shape, F32)
        pad_scr[0:tq, :] = slcn_ref[...]
        kn = pad_scr[...].astype(BF16)
        sn_scr[...] = _dot_nt(qbd, kn[:, :kw]) + bsn_ref[...] + _dot(sel, exn_ref[...])
        vn_scr[...] = kn[:, kw:]
        pad_scr[0:tq, :] = winn_ref[...]
        nwin_ref[0] = _shifted_buffer(wst_ref[0], pad_scr[...], tq)
        wst = wst_ref[0].astype(BF16)
        wn = pad_scr[...].astype(BF16)
        bias_w = bw_ref[...]
        mw = _softmax_piece(_dot(qbd, wst[:kw]) + bias_w, bias_w > NEG_TEST, *init, wst[kw:], v_t=True)
        bias_wn = bwn_ref[...]
        mw = _softmax_piece(_dot_nt(qbd, wn[:, :kw]) + bias_wn, bias_wn > NEG_TEST, *mw, wn[:, kw:])
        ow_scr[...] = mw[2] / jnp.maximum(mw[1], 1e-30)

    for i in range(SMP_PP):
        pg = pages[i][0].astype(BF16)
        kbuf_scr[:, i * PAGE:(i + 1) * PAGE] = pg[:kw]
        vbuf_scr[j, :, i * PAGE:(i + 1) * PAGE] = pg[kw:]
    s_scr[j] = _dot(qbd_scr[...], kbuf_scr[...]) + bs_ref[j] + _dot(sel_scr[...], ex_ref[j])

    @pl.when(j == pl.num_programs(1) - 1)
    def _():
        s_all = s_scr[...]
        s_new = sn_scr[...]
        m = jnp.maximum(jnp.max(jnp.max(s_all, axis=0), axis=-1, keepdims=True),
                        jnp.max(s_new, axis=-1, keepdims=True))
        p_new = jnp.exp(s_new - m)
        den = jnp.sum(p_new, axis=-1, keepdims=True)
        acc = _dot(p_new.astype(BF16), vn_scr[...])
        for st in range(s_all.shape[0]):
            p = jnp.exp(s_all[st] - m)
            den = den + jnp.sum(p, axis=-1, keepdims=True)
            acc = acc + _dot_nt(p.astype(BF16), vbuf_scr[st])
        o_s = acc / jnp.maximum(den, 1e-30)
        o_c = oc_scr[...]
        o_w = ow_scr[...]
        outs = []
        for h in range(NSA_HEADS):
            g, hg = divmod(h, NSA_GROUP)
            gt = gate_ref[g]
            rs = slice(h * tq, (h + 1) * tq)
            ls = slice(hd * g, hd * (g + 1))
            outs.append(gt[:, 3 * hg:3 * hg + 1] * o_c[rs, ls] + gt[:, 3 * hg + 1:3 * hg + 2] * o_s[rs, ls]
                        + gt[:, 3 * hg + 2:3 * hg + 3] * o_w[rs, ls])
        o_ref[...] = jnp.concatenate(outs, axis=1)


def _nsa_sample_tables(rel_bias, past, tq, n_c, w_state):
    rows = NSA_HEADS * tq
    total = past + tq
    n_cmp = total // CMP_STRIDE - CMP_BLOCK // CMP_STRIDE + 1
    n_slc = -(-total // SEL_BLOCK)
    pos = past + jnp.arange(tq)
    c_hi = jnp.arange(n_c) * CMP_STRIDE + CMP_BLOCK - 1
    dist_c = pos[:, None] - c_hi[None, :]
    bc = _bias_tile(rel_bias, dist_c, (dist_c >= 0) & (jnp.arange(n_c) < n_cmp)[None, :]).reshape(rows, n_c)
    n_steps = past // (SMP_PP * PAGE)
    d_ext = jnp.arange(tq + past - 1) + 1
    bs = _toeplitz(_bias_tile(rel_bias, d_ext, d_ext >= 0), tq, past)
    bs = bs.reshape(rows, n_steps, SMP_PP * PAGE).transpose(1, 0, 2)
    jn = jnp.arange(PAGE)
    dist_n = jnp.arange(tq)[:, None] - jn[None, :]
    new_ok = (dist_n >= 0) & (jn < tq)[None, :]
    bsn = _bias_tile(rel_bias, dist_n, new_ok).reshape(rows, PAGE)
    dist_w = jnp.arange(tq)[:, None] + w_state - jnp.arange(w_state)[None, :]
    bw = _bias_tile(rel_bias, dist_w, (dist_w >= 0) & (dist_w <= NSA_WINDOW)).reshape(rows, w_state)
    bwn = _bias_tile(rel_bias, dist_n, new_ok & (dist_n <= NSA_WINDOW)).reshape(rows, PAGE)
    c_lo = np.arange(n_c) * CMP_STRIDE
    s_lo = np.arange(SEL_LANES) * SEL_BLOCK
    ov = ((c_lo[:, None] <= s_lo[None, :] + SEL_BLOCK - 1) & (c_lo[:, None] + CMP_BLOCK - 1 >= s_lo[None, :])
          & (np.arange(n_c) < n_cmp)[:, None] & (np.arange(SEL_LANES) < n_slc)[None, :])
    blk = np.arange(SEL_LANES)
    ex = (blk[:, None] == (np.arange(past)[None, :] // SEL_BLOCK)).reshape(SEL_LANES, n_steps, SMP_PP * PAGE)
    ex = ex.transpose(1, 0, 2)
    exn = (blk[:, None] == ((past + np.arange(PAGE))[None, :] // SEL_BLOCK)) & (np.arange(PAGE) < tq)[None, :]
    as_bf = lambda a: jnp.asarray(a.astype(np.float32), BF16)
    return bc, bs, bsn, bw, bwn, as_bf(ov), as_bf(ex), as_bf(exn), n_slc


def _nsa_sample(q, gates, crows, slc_pool_t, table, slc_new, win_state_t, win_new, q_gain, tables, *, tq):
    bc, bs, bsn, bw, bwn, ov, ex, exn, n_slc = tables
    bsz, npg = table.shape
    past = npg * PAGE
    n_c = crows.shape[1]
    w_state = win_state_t.shape[2]
    rows = NSA_HEADS * tq
    qw = NSA_HEADS * HEAD_DIM
    page = lambda i: pl.BlockSpec((1, ROWS_NSA, PAGE), lambda b, j, pt: (pt[b, j * SMP_PP + i], 0, 0))
    per_b = lambda shape: pl.BlockSpec((1,) + shape, lambda b, j, pt: (b,) + (0,) * len(shape))
    new_rows = pl.BlockSpec((tq, ROWS_NSA), lambda b, j, pt: (b, 0))
    return pl.pallas_call(
        functools.partial(_nsa_sample_body, n_slc=n_slc, k_sel=min(SEL_TOPK, n_slc), past=past, tq=tq),
        grid_spec=pltpu.PrefetchScalarGridSpec(
            num_scalar_prefetch=1, grid=(bsz, npg // SMP_PP),
            in_specs=[pl.BlockSpec((tq, qw), lambda b, j, pt: (b, 0)),
                      pl.BlockSpec((NSA_KV, tq, LANES), lambda b, j, pt: (0, b, 0)),
                      per_b((n_c, ROWS_NSA))] + [page(i) for i in range(SMP_PP)] + [
                      new_rows, per_b((ROWS_NSA, w_state)), new_rows,
                      _const((1, qw)), _const(_block_diag_ones(qw).shape), _const(bc.shape), _const(bs.shape), _const(bsn.shape),
                      _const(bw.shape), _const(bwn.shape), _const(ov.shape), _const(ex.shape), _const(exn.shape)],
            out_specs=[pl.BlockSpec((tq, qw), lambda b, j, pt: (b, 0)), per_b((ROWS_NSA, w_state))],
            scratch_shapes=[pltpu.VMEM((rows, NSA_KW), BF16), pltpu.VMEM((rows, SEL_LANES), BF16),
                            pltpu.VMEM((NSA_KW, SMP_PP * PAGE), BF16),
                            pltpu.VMEM((npg // SMP_PP, NSA_KW, SMP_PP * PAGE), BF16),
                            pltpu.VMEM((npg // SMP_PP, rows, SMP_PP * PAGE), F32),
                            pltpu.VMEM((rows, PAGE), F32), pltpu.VMEM((PAGE, NSA_KW), BF16),
                            pltpu.VMEM((rows, NSA_KW), F32), pltpu.VMEM((rows, NSA_KW), F32),
                            pltpu.VMEM((PAGE, ROWS_NSA), F32)]),
        out_shape=[jax.ShapeDtypeStruct((bsz * tq, qw), F32), jax.ShapeDtypeStruct(win_state_t.shape, F32)],
        compiler_params=_cparams("parallel", "arbitrary"), name="nsa_sample",
    )(table, q, gates, crows, *([slc_pool_t] * SMP_PP), slc_new, win_state_t, win_new,
      jnp.tile(q_gain, NSA_HEADS).reshape(1, qw), _block_diag_ones(qw), bc, bs, bsn, bw, bwn, ov, ex, exn)


DIL_TQ = 128


def _head_of_lane(shape):
    return lax.broadcasted_iota(jnp.int32, shape, 1) >> 6


def _stack_heads(qt, lane_head):
    return jnp.concatenate([jnp.where(lane_head == h, qt, 0.0) for h in range(DIL_HEADS)], axis=0).astype(BF16)


def _unstack_heads(r, lane_head, t):
    return sum(jnp.where(lane_head == h, r[h * t:(h + 1) * t], 0.0) for h in range(DIL_HEADS))


def _dil_prompt_body(q_ref, kv_ref, bias_ref, o_ref, lse_ref, *, n_sub, dil):
    tq = DIL_TQ
    gw = ODD_GW
    lane_head = _head_of_lane((tq, gw))
    n_t = n_sub // tq

    def tile(t, r):
        ql = slice(gw * r, gw * (r + 1))
        kl = slice(2 * gw * r, 2 * gw * r + gw)
        vl = slice(2 * gw * r + gw, 2 * gw * (r + 1))
        cur_rows = slice(t * tq, (t + 1) * tq)
        prev_rows = slice(max(t - 1, 0) * tq, (max(t - 1, 0) + 1) * tq)
        kk = jnp.concatenate([kv_ref[prev_rows, kl], kv_ref[cur_rows, kl]], axis=0).astype(BF16)
        vv = jnp.concatenate([kv_ref[prev_rows, vl], kv_ref[cur_rows, vl]], axis=0).astype(BF16)
        bias = bias_ref[min(t, 1)]
        s = _dot_nt(_stack_heads(q_ref[cur_rows, ql], lane_head), kk) + bias
        m = jnp.max(s, axis=-1, keepdims=True)
        e = jnp.where(bias > NEG_TEST, jnp.exp(s - m), 0.0)
        l = jnp.maximum(jnp.sum(e, axis=-1, keepdims=True), 1e-30)
        res = _dot((e / l).astype(BF16), vv)
        lse = jnp.broadcast_to(m + jnp.log(l), (DIL_HEADS * tq, gw))
        o_ref[cur_rows, ql] = _unstack_heads(res, lane_head, tq)
        lse_ref[cur_rows, ql] = _unstack_heads(lse, lane_head, tq)

    for r in range(dil):
        for t in range(n_t):
            tile(t, r)


def _merge_groups(outs, lses):
    mx = functools.reduce(jnp.maximum, lses)
    ws = [jnp.exp(l - mx) for l in lses]
    return sum(w * o for w, o in zip(ws, outs)) / sum(ws)


def _dil_prompt_tables(rel_bias):
    tq = DIL_TQ
    m_ext = jnp.arange(3 * tq - 1) - (tq - 1)
    cur_tile = jnp.arange(2 * tq) >= tq
    tiles = []
    for gi, (win, dil) in enumerate(DIL_PATTERN):
        assert win // dil == tq
        t = _toeplitz(_bias_tile(rel_bias, m_ext * dil, (m_ext >= 0) & (m_ext <= tq)), tq, 2 * tq)
        t = t[gi * DIL_HEADS:(gi + 1) * DIL_HEADS].reshape(DIL_HEADS * tq, 2 * tq)
        tiles.append(jnp.stack([jnp.where(cur_tile, t, NEG), t]))
    return tiles


def _dil_prompt(q_views, kv_views, tables, *, batch, seq):
    gw = ODD_GW
    outs, lses = [], []
    for gi, (_, dil) in enumerate(DIL_PATTERN):
        n_sub = seq // dil
        assert n_sub % DIL_TQ == 0
        qv, kvv = q_views[gi], kv_views[gi]
        o_spec = pl.BlockSpec((n_sub, dil * gw), lambda b: (b, 0))
        o, lse = pl.pallas_call(
            functools.partial(_dil_prompt_body, n_sub=n_sub, dil=dil), grid=(batch,),
            in_specs=[o_spec, pl.BlockSpec((n_sub, dil * 2 * gw), lambda b: (b, 0)), _const(tables[gi].shape)],
            out_specs=[o_spec, o_spec],
            out_shape=[jax.ShapeDtypeStruct(qv.shape, F32)] * 2,
            compiler_params=_cparams("parallel"), name=f"dilated_prompt_g{gi}",
        )(qv, kvv, tables[gi])
        outs.append(o)
        lses.append(lse)
    return outs, lses


def _shifted_buffer(st, new_rows_padded, tq):
    ln = st.shape[1]
    shifted = pltpu.roll(st, ln - tq, 1)
    new_t = pltpu.roll(new_rows_padded.T, LANES - tq, 1)
    lane = lax.broadcasted_iota(jnp.int32, (st.shape[0], LANES), 1)
    tail = jnp.where(lane >= LANES - tq, new_t, shifted[:, ln - LANES:])
    return tail if ln == LANES else jnp.concatenate([shifted[:, :ln - LANES], tail], axis=1)


def _dil_sample_body(q_ref, kvn_ref, st0_ref, st1_ref, st2_ref, b0_ref, b1_ref, b2_ref, bn_ref, o_ref, lse_ref,
                     n0_ref, n1_ref, n2_ref, pad_scr, *, tq):
    gw = ODD_GW
    lane_head = _head_of_lane((tq, gw))
    rows = DIL_HEADS * tq
    pad_scr[...] = jnp.zeros(pad_scr.shape, F32)
    for gi, (st_ref, b_ref, new_ref) in enumerate(((st0_ref, b0_ref, n0_ref), (st1_ref, b1_ref, n1_ref),
                                                    (st2_ref, b2_ref, n2_ref))):
        qs = _stack_heads(q_ref[gi], lane_head)
        pad_scr[0:tq, :] = kvn_ref[gi]
        new_ref[0] = _shifted_buffer(st_ref[0], pad_scr[...], tq)
        st = st_ref[0].astype(BF16)
        kn = pad_scr[...].astype(BF16)
        bias = b_ref[...]
        acc = (jnp.full((rows, 1), NEG, F32), jnp.zeros((rows, 1), F32), jnp.zeros((rows, gw), F32))
        acc = _softmax_piece(_dot(qs, st[:gw]) + bias, bias > NEG_TEST, *acc, st[gw:], v_t=True)
        bias_n = bn_ref[gi]
        m, l, a = _softmax_piece(_dot_nt(qs, kn[:, :gw]) + bias_n, bias_n > NEG_TEST, *acc, kn[:, gw:])
        l = jnp.maximum(l, 1e-30)
        o_ref[gi] = _unstack_heads(a / l, lane_head, tq)
        lse_ref[gi] = _unstack_heads(jnp.broadcast_to(m + jnp.log(l), (rows, gw)), lane_head, tq)


def _dil_sample_tables(rel_bias, tq, state_lens):
    rows = DIL_HEADS * tq
    i = jnp.arange(tq)[:, None]
    per_state, per_new = [], []
    jn = jnp.arange(PAGE)[None, :]
    for gi, ((win, dil), ln) in enumerate(zip(DIL_PATTERN, state_lens)):
        hs = slice(gi * DIL_HEADS, (gi + 1) * DIL_HEADS)
        d = i + ln - jnp.arange(ln)[None, :]
        per_state.append(_bias_tile(rel_bias, d, (d % dil == 0) & (d <= win))[hs].reshape(rows, ln))
        dn = i - jn
        per_new.append(_bias_tile(rel_bias, dn, (dn >= 0) & (dn % dil == 0) & (dn <= win) & (jn < tq))[hs]
                       .reshape(rows, PAGE))
    return per_state, jnp.stack(per_new)


def _dil_sample(q3, kv3, states, tables, *, tq):
    per_state, bias_new = tables
    bsz = states[0].shape[0]
    gw = ODD_GW
    st_spec = lambda s: pl.BlockSpec((1,) + s.shape[1:], lambda b: (b, 0, 0))
    grp = lambda w: pl.BlockSpec((3, tq, w), lambda b: (0, b, 0))
    return pl.pallas_call(
        functools.partial(_dil_sample_body, tq=tq), grid=(bsz,),
        in_specs=[grp(gw), grp(2 * gw)] + [st_spec(s) for s in states] + [_const(t.shape) for t in per_state]
        + [_const(bias_new.shape)],
        out_specs=[grp(gw), grp(gw)] + [st_spec(s) for s in states],
        out_shape=[jax.ShapeDtypeStruct((3, bsz * tq, gw), F32)] * 2
        + [jax.ShapeDtypeStruct(s.shape, F32) for s in states],
        scratch_shapes=[pltpu.VMEM((PAGE, 2 * gw), F32)],
        compiler_params=_cparams("parallel"), name="dilated_sample",
    )(q3, kv3, *states, *per_state, bias_new)


def _pad_even_w(w_in):
    return jnp.concatenate([w_in, jnp.zeros((w_in.shape[0], EVEN_PAD - w_in.shape[1]), w_in.dtype)], axis=1).astype(BF16)


def _even_mixer_prompt(h, gain, w_in, w_out, lb, hg_norm, qk_norm, cmp_pe, cmp_w1, cmp_w2, rel_bias, *, batch, seq, tm):
    n = batch * seq
    hg, q, cmp_rows, slcg, wing, gates_t, cmp_t, slc_t, win_t = _inproj_even(
        h, gain, _pad_even_w(w_in), qk_norm, tm=tm, seq=seq)
    s0 = jnp.zeros((batch, HG_HEADS, HEAD_DIM, HEAD_DIM), F32)
    o_h, s_fin = _hgrn(hg, lb, hg_norm, s0, batch=batch, seq=seq)
    cw = _compress_weights(cmp_pe, cmp_w1, cmp_w2, qk_norm[1])
    npg = seq // PAGE
    table = jnp.arange(batch * npg, dtype=jnp.int32).reshape(batch, npg)
    _, kcv = _compress(cmp_rows.reshape(n // PAGE, PAGE, ROWS_NSA), table, cw, transposed=False)
    tables = _nsa_prompt_t_tables(rel_bias, seq, kcv.shape[2])
    o_nt = _nsa_prompt_t(q, gates_t, kcv, slcg, wing, qk_norm[0], tables, batch=batch, seq=seq)
    h = _outproj2t(h, o_h, o_nt, w_out.astype(BF16), tm=tm)
    keep = min(NSA_WINDOW, seq)
    return h, {"hgrn": s_fin, "cmp": _rows_from_t(cmp_t, NSA_KV), "slc": _rows_from_t(slc_t, NSA_KV),
               "win": _rows_from_t(win_t[:, :, seq - keep:], NSA_KV)}


def _rows_from_t(a, heads):
    return jnp.moveaxis(a.reshape(a.shape[0], 2, heads, HEAD_DIM, a.shape[2]), -1, 1)


def _even_mixer_sample(h, gain, w_in, w_out, lb, hg_norm, qk_norm, cmp_pe, cmp_w1, cmp_w2, rel_bias,
                       state_hgrn, cmp_pool, slc_pool, win_state, page_table, *, batch, tq, tm):
    assert tq < CMP_STRIDE and tq & (tq - 1) == 0
    hg, q, cmp_rows, slc_rows, win_rows, gates = _inproj_even(h, gain, _pad_even_w(w_in), qk_norm, tm=tm)
    o_h, s_fin = _hgrn(hg, lb, hg_norm, state_hgrn, batch=batch, seq=tq)
    cw = _compress_weights(cmp_pe, cmp_w1, cmp_w2, qk_norm[1])
    rows_t = lambda a: jnp.moveaxis(a, 1, -1).reshape(a.shape[0], ROWS_NSA, a.shape[1])
    crows, _ = _compress(rows_t(cmp_pool), page_table, cw, transposed=True)
    past = page_table.shape[1] * PAGE
    w_state = win_state.shape[1]
    tables = _nsa_sample_tables(rel_bias, past, tq, crows.shape[1], w_state)
    o_n, new_win_t = _nsa_sample(q, gates, crows, rows_t(slc_pool), page_table, slc_rows, rows_t(win_state), win_rows,
                                 qk_norm[0], tables, tq=tq)
    h = _outproj2(h, o_h, o_n, w_out.astype(BF16), tm=tm)
    new_win = jnp.moveaxis(new_win_t.reshape((batch,) + win_state.shape[2:] + (w_state,)), -1, 1)
    return h, {"hgrn": s_fin, "cmp": cmp_rows, "slc": slc_rows, "win": new_win}


def _odd_mixer_prompt(h, gain, w_in, w_out, qk_norm, rel_bias, *, batch, seq, tm):
    qv0, kvv0, qv1, kvv1, qv2, kvv2, *kv_t = _inproj_odd(h, gain, w_in.astype(BF16), qk_norm, tm=tm, seq=seq)
    outs, lses = _dil_prompt([qv0, qv1, qv2], [kvv0, kvv1, kvv2], _dil_prompt_tables(rel_bias),
                             batch=batch, seq=seq)
    bufs = [_rows_from_t(t[:, :, seq - min(w, seq):], DIL_HEADS) for t, (w, _) in zip(kv_t, DIL_PATTERN)]
    return _outproj_merge(h, outs, lses, w_out.astype(BF16), tm=tm, views=True), bufs


def _odd_mixer_sample(h, gain, w_in, w_out, qk_norm, rel_bias, states, *, batch, tq, tm):
    q3, kv3 = _inproj_odd(h, gain, w_in.astype(BF16), qk_norm, tm=tm)
    tables = _dil_sample_tables(rel_bias, tq, [s.shape[1] for s in states])
    states_t = [jnp.moveaxis(s, 1, -1).reshape(batch, 2 * ODD_GW, s.shape[1]) for s in states]
    o3, lse3, *new_t = _dil_sample(q3, kv3, states_t, tables, tq=tq)
    new = [jnp.moveaxis(n.reshape(s.shape[:1] + s.shape[2:] + s.shape[1:2]), -1, 1) for n, s in zip(new_t, states)]
    h = _outproj_merge(h, [o3[g] for g in range(3)], [lse3[g] for g in range(3)], w_out.astype(BF16), tm=tm)
    return h, new


TM_PROMPT = 512
TM_SAMPLE = 256


def kernel(x_prompt, x_sample, p_prompt, p_sample, state_hgrn, cache_nsa_cmp_kv, cache_nsa_slc_kv, state_nsa_win_kv,
           state_dil_kv_0, state_dil_kv_1, state_dil_kv_2, page_table, rel_bias, norm_ffn1, norm_mix, norm_ffn2,
           norm_ple, w_ffn1_in, w_ffn1_out, w_ffn2_in, w_ffn2_out, w_ple_gate, w_ple_proj, w_in_even, w_out_even,
           hgrn_lb_logits, hgrn_norm, nsa_qk_norm, nsa_cmp_pe, nsa_cmp_w1, nsa_cmp_w2, w_in_odd, w_out_odd,
           dil_qk_norm):
    depth = norm_ffn1.shape[0]
    dil_states = (state_dil_kv_0, state_dil_kv_1, state_dil_kv_2)
    bf = lambda w: w.astype(BF16)
    w1i, w1o, w2i, w2o, wpg, wpp = (bf(w) for w in (w_ffn1_in, w_ffn1_out, w_ffn2_in, w_ffn2_out, w_ple_gate,
                                                     w_ple_proj))
    lb_all = jnp.cumsum(jax.nn.softmax(hgrn_lb_logits.astype(F32), axis=0), axis=0)
    g_ffn1, g_ffn2, g_ple = (g.reshape(depth, 1, -1) for g in (norm_ffn1, norm_ffn2, norm_ple))

    def run(x, p, sample):
        batch, seq, d = x.shape
        n = batch * seq
        tm = TM_SAMPLE if sample else TM_PROMPT
        h = x.reshape(n, d)
        p = p.reshape(depth, n, p.shape[-1])
        even, odd = [], []
        for li in range(depth):
            h = _ffn(h, li, g_ffn1, w1i, w1o, tm=tm)
            if li % 2 == 0:
                ei = li // 2
                args = (h, norm_mix[li], w_in_even[ei], w_out_even[ei], lb_all[ei], hgrn_norm[ei], nsa_qk_norm[ei],
                        nsa_cmp_pe[ei], nsa_cmp_w1[ei], nsa_cmp_w2[ei], rel_bias)
                if sample:
                    h, st = _even_mixer_sample(*args, state_hgrn[ei], cache_nsa_cmp_kv[ei], cache_nsa_slc_kv[ei],
                                               state_nsa_win_kv[ei], page_table, batch=batch, tq=seq, tm=tm)
                else:
                    h, st = _even_mixer_prompt(*args, batch=batch, seq=seq, tm=tm)
                even.append(st)
            else:
                oi = li // 2
                args = (h, norm_mix[li], w_in_odd[oi], w_out_odd[oi], dil_qk_norm[oi], rel_bias)
                if sample:
                    h, bufs = _odd_mixer_sample(*args, [s[oi] for s in dil_states], batch=batch, tq=seq, tm=tm)
                else:
                    h, bufs = _odd_mixer_prompt(*args, batch=batch, seq=seq, tm=tm)
                odd.append(bufs)
            h = _ffn(h, li, g_ffn2, w2i, w2o, ple=(p, g_ple, wpg, wpp), tm=tm)
        rows = lambda a: a.reshape(batch, -1, 2, NSA_KV, HEAD_DIM)
        drows = lambda a: a.reshape(batch, -1, 2, DIL_HEADS, HEAD_DIM)
        return (h.reshape(batch, seq, d),
                jnp.stack([s["hgrn"] for s in even]), jnp.stack([rows(s["cmp"]) for s in even]),
                jnp.stack([rows(s["slc"]) for s in even]), jnp.stack([rows(s["win"]) for s in even]),
                jnp.stack([drows(b[0]) for b in odd]), jnp.stack([drows(b[1]) for b in odd]),
                jnp.stack([drows(b[2]) for b in odd]))

    y_p, hg_p, cmp_p, slc_p, win_p, d0_p, d1_p, d2_p = run(x_prompt, p_prompt, False)
    y_s, hg_s, cmp_s, slc_s, win_s, d0_s, d1_s, d2_s = run(x_sample, p_sample, True)
    return (y_p, y_s, hg_p, hg_s, cmp_p, cmp_s, slc_p, slc_s, win_p, win_s, d0_p, d0_s, d1_p, d1_s, d2_p, d2_s)
```

```python
import functools
import math

import numpy as np
import jax
import jax.numpy as jnp
from jax import lax
from jax.experimental import pallas as pl
from jax.experimental.pallas import tpu as pltpu

F32 = jnp.float32
BF16 = jnp.bfloat16
HIGHEST = lax.Precision.HIGHEST

V7X_VMEM_BYTES = 64 * 1024 * 1024
VMEM_LIMIT = V7X_VMEM_BYTES * 7 // 8
LANES = 128

EPS = 1e-6
NEG = -1e30
NEG_TEST = -1e29

HEAD_DIM = 64
HG_HEADS = 8
HG_CHUNK = 128
NSA_HEADS = 12
NSA_KV = 3
NSA_GROUP = NSA_HEADS // NSA_KV
CMP_BLOCK = 32
CMP_STRIDE = 16
SEL_BLOCK = 64
SEL_TOPK = 16
NSA_WINDOW = 512
FORCE_SCORE = 1e4
PAGE = 128
DIL_PATTERN = ((128, 1), (512, 4), (2048, 16))
DIL_HEADS = 4
RP_BUCKETS = 32
RP_MAX_DIST = 2048


def _cparams(*sem):
    return pltpu.CompilerParams(dimension_semantics=sem, vmem_limit_bytes=VMEM_LIMIT)


def _const(shape):
    nd = len(shape)
    return pl.BlockSpec(shape, lambda *_: (0,) * nd, pipeline_mode=pl.Buffered(1))


def _dot(a, b):
    return jnp.dot(a, b, preferred_element_type=F32)


def _dot_nt(a, b):
    return lax.dot_general(a, b, (((1,), (1,)), ((), ())), preferred_element_type=F32)


def _dot_tn(a, b):
    return lax.dot_general(a, b, (((0,), (0,)), ((), ())), preferred_element_type=F32)


def _rms(x, gain):
    return x * lax.rsqrt(jnp.mean(x * x, axis=-1, keepdims=True) + EPS) * gain


def _seg_ms(x, bd):
    sq = x * x
    hi = sq.astype(BF16)
    lo = (sq - hi.astype(F32)).astype(BF16)
    bw = bd.shape[0]
    cols = []
    for c0 in range(0, x.shape[1], bw):
        wd = min(bw, x.shape[1] - c0)
        blk = bd[:wd, :wd]
        cols.append(_dot(hi[:, c0:c0 + wd], blk) + _dot(lo[:, c0:c0 + wd], blk))
    return (cols[0] if len(cols) == 1 else jnp.concatenate(cols, axis=1)) * (1.0 / HEAD_DIM)


def _seg_rms(x, bd, gain):
    return x * lax.rsqrt(_seg_ms(x, bd) + EPS) * gain


def _norm_k_rows(x, bd, gain, kmask):
    bw = bd.shape[0]
    r = lax.rsqrt(_seg_ms(x[:, :bw], bd) + EPS)
    r = jnp.concatenate([r, jnp.ones((x.shape[0], x.shape[1] - bw), F32)], axis=1)
    return x * jnp.where(kmask > 0.5, r, 1.0) * gain


V7X_MXU_WIDTH = 256


def _block_diag_ones(width):
    i = np.arange(min(width, V7X_MXU_WIDTH)) // HEAD_DIM
    return jnp.asarray((i[:, None] == i[None, :]).astype(np.float32), BF16)


def _rel_bucket(dist):
    exact = RP_BUCKETS // 2
    d = jnp.maximum(dist, 0)
    log_ratio = jnp.log(jnp.maximum(d, 1).astype(F32) / exact) / math.log(RP_MAX_DIST / exact)
    large = jnp.minimum(exact + (log_ratio * (RP_BUCKETS - exact)).astype(jnp.int32), RP_BUCKETS - 1)
    return jnp.where(d < exact, d, large)


def _bias_tile(rel_bias, dist, valid):
    onehot = (_rel_bucket(dist)[..., None] == jnp.arange(RP_BUCKETS)).astype(F32)
    t = jnp.einsum('...k,kh->h...', onehot, rel_bias.astype(F32), precision=HIGHEST)
    return jnp.where(valid[None], t, NEG)


def _toeplitz(ext, n, m):
    p = n + m - 1
    u = jnp.concatenate([jnp.flip(ext[..., :m], -1), jnp.flip(ext[..., m:], -1)], axis=-1)
    reps = (1,) * (ext.ndim - 1) + (n,)
    t = jnp.tile(u, reps)[..., :n * (p - 1)].reshape(ext.shape[:-1] + (n, p - 1))
    return t[..., :m]


def _softmax_piece(s, mask, m_prev, l_prev, acc_prev, v, v_t=False):
    m_new = jnp.maximum(m_prev, jnp.max(s, axis=-1, keepdims=True))
    alpha = jnp.exp(m_prev - m_new)
    p = jnp.where(mask, jnp.exp(s - m_new), 0.0)
    l_new = alpha * l_prev + jnp.sum(p, axis=-1, keepdims=True)
    pb = p.astype(BF16)
    acc_new = alpha * acc_prev + (_dot_nt(pb, v) if v_t else _dot(pb, v))
    return m_new, l_new, acc_new


def _ffn_body(x_ref, g_ref, win_ref, wout_ref, *rest, dff, ple):
    x = x_ref[...]
    xn = _rms(x, g_ref[...]).astype(BF16)
    gu = _dot(xn, win_ref[...])
    gg = gu[:, :dff]
    a = (gg * jax.nn.sigmoid(gg) * gu[:, dff:]).astype(BF16)
    h = x + 0.5 * _dot(a, wout_ref[...])
    if ple:
        p_ref, gp_ref, wg_ref, wp_ref, o_ref = rest
        hn = _rms(h, gp_ref[...]).astype(BF16)
        gate = jax.nn.sigmoid(_dot(hn, wg_ref[...]))
        h = h + gate * _dot(p_ref[...].astype(BF16), wp_ref[...])
    else:
        (o_ref,) = rest
    o_ref[...] = h


def _layer(arr, li):
    tail = (0,) * (arr.ndim - 1)
    return pl.BlockSpec((None,) + arr.shape[1:], lambda *_: (li,) + tail, pipeline_mode=pl.Buffered(1))


def _ffn(h, li, gain, w_in, w_out, ple=None, *, tm):
    n, d = h.shape
    dff = w_out.shape[1]
    row = lambda i: (i, 0)
    in_specs = [pl.BlockSpec((tm, d), row), _layer(gain, li), _layer(w_in, li), _layer(w_out, li)]
    args = [h, gain, w_in, w_out]
    if ple is not None:
        p, gp, wg, wp = ple
        in_specs += [pl.BlockSpec((None, tm, p.shape[2]), lambda i: (li, i, 0)), _layer(gp, li), _layer(wg, li),
                     _layer(wp, li)]
        args += [p, gp, wg, wp]
    return pl.pallas_call(
        functools.partial(_ffn_body, dff=dff, ple=ple is not None),
        grid=(n // tm,), in_specs=in_specs, out_specs=pl.BlockSpec((tm, d), row),
        out_shape=jax.ShapeDtypeStruct((n, d), F32), compiler_params=_cparams("parallel"),
        name="ffn_ple" if ple is not None else "ffn",
    )(*args)


EVEN_HG = 4 * HG_HEADS * HEAD_DIM
EVEN_Q = NSA_HEADS * HEAD_DIM
ROWS_NSA = 2 * NSA_KV * HEAD_DIM
EVEN_PAD = EVEN_HG + EVEN_Q + 3 * ROWS_NSA + LANES


GATE_ROWS = 16


def _inproj_even_body(x_ref, g_ref, w_ref, gk_ref, km_ref, bd_ref, hg_ref, q_ref, cmp_ref, *outs, prompt):
    xn = _rms(x_ref[...], g_ref[...]).astype(BF16)
    p = _dot(xn, w_ref[...])
    c0 = EVEN_HG
    c1 = c0 + EVEN_Q
    hg_ref[...] = p[:, :c0]
    q_ref[...] = p[:, c0:c1]
    cmp = p[:, c1:c1 + ROWS_NSA]
    cmp_ref[...] = cmp
    bd = bd_ref[...]
    km = km_ref[...]
    slc = _norm_k_rows(p[:, c1 + ROWS_NSA:c1 + 2 * ROWS_NSA], bd, gk_ref[0:1, :], km)
    win = _norm_k_rows(p[:, c1 + 2 * ROWS_NSA:c1 + 3 * ROWS_NSA], bd, gk_ref[1:2, :], km)
    sig = jax.nn.sigmoid(p[:, c1 + 3 * ROWS_NSA:])
    if not prompt:
        slc_ref, win_ref, gate_ref = outs
        slc_ref[...] = slc
        win_ref[...] = win
        for g in range(NSA_KV):
            gate_ref[g] = sig if g == 0 else pltpu.roll(sig, LANES - 3 * NSA_GROUP * g, 1)
        return
    slcg_ref, wing_ref, gate_t_ref, cmp_t_ref, slc_t_ref, win_t_ref = outs
    kw = NSA_KV * HEAD_DIM
    sig_t = sig.T
    for g in range(NSA_KV):
        ks = slice(HEAD_DIM * g, HEAD_DIM * (g + 1))
        vs = slice(kw + HEAD_DIM * g, kw + HEAD_DIM * (g + 1))
        slcg_ref[g] = jnp.concatenate([slc[:, ks], slc[:, vs]], axis=1)
        wing_ref[g] = jnp.concatenate([win[:, ks], win[:, vs]], axis=1)
        for half in range(sig.shape[0] // NSA_TQ):
            gate_t_ref[g, half] = sig_t[3 * NSA_GROUP * g:3 * NSA_GROUP * g + GATE_ROWS,
                                        half * NSA_TQ:(half + 1) * NSA_TQ]
    for dst_ref, rows in ((cmp_t_ref, cmp), (slc_t_ref, slc), (win_t_ref, win)):
        _store_transposed(dst_ref, rows)


def _store_transposed(dst_ref, x):
    for c in range(x.shape[1] // LANES):
        dst_ref[0, c * LANES:(c + 1) * LANES, :] = x[:, c * LANES:(c + 1) * LANES].T


def _inproj_even(h, gain, w_pad, qk_norm, *, tm, seq=None):
    n, d = h.shape
    kw = NSA_KV * HEAD_DIM
    ones = jnp.ones((kw,), F32)
    gk = jnp.stack([jnp.concatenate([jnp.tile(qk_norm[2], NSA_KV), ones]),
                    jnp.concatenate([jnp.tile(qk_norm[3], NSA_KV), ones])])
    km = jnp.concatenate([ones, 0.0 * ones]).reshape(1, ROWS_NSA)
    row = lambda i: (i, 0)
    grp = lambda i: (0, i, 0)
    shp = lambda *s: jax.ShapeDtypeStruct(s, F32)
    out_specs = [pl.BlockSpec((tm, EVEN_HG), row), pl.BlockSpec((tm, EVEN_Q), row), pl.BlockSpec((tm, ROWS_NSA), row)]
    out_shape = [shp(n, EVEN_HG), shp(n, EVEN_Q), shp(n, ROWS_NSA)]
    if seq is None:
        out_specs += [pl.BlockSpec((tm, ROWS_NSA), row)] * 2 + [pl.BlockSpec((NSA_KV, tm, LANES), grp)]
        out_shape += [shp(n, ROWS_NSA)] * 2 + [shp(NSA_KV, n, LANES)]
    else:
        tpb = seq // tm
        out_specs += [pl.BlockSpec((NSA_KV, tm, LANES), grp)] * 2
        out_shape += [shp(NSA_KV, n, LANES)] * 2
        out_specs.append(pl.BlockSpec((NSA_KV, tm // NSA_TQ, GATE_ROWS, NSA_TQ), lambda i: (0, i, 0, 0)))
        out_shape.append(shp(NSA_KV, n // NSA_TQ, GATE_ROWS, NSA_TQ))
        out_specs += [pl.BlockSpec((1, ROWS_NSA, tm), lambda i: (i // tpb, 0, i % tpb))] * 3
        out_shape += [shp(n // seq, ROWS_NSA, seq)] * 3
    return pl.pallas_call(
        functools.partial(_inproj_even_body, prompt=seq is not None), grid=(n // tm,),
        in_specs=[pl.BlockSpec((tm, d), row), _const((1, d)), _const(w_pad.shape), _const((2, ROWS_NSA)),
                  _const((1, ROWS_NSA)), _const(_block_diag_ones(ROWS_NSA).shape)],
        out_specs=out_specs, out_shape=out_shape,
        compiler_params=_cparams("parallel"), name="inproj_even",
    )(h, gain.reshape(1, d), w_pad, gk, km, _block_diag_ones(ROWS_NSA))


ODD_W = 3 * DIL_HEADS * HEAD_DIM
ODD_GW = DIL_HEADS * HEAD_DIM


def _to_residue_view(src, dst_ref, stage_ref, dil):
    rows, w = src.shape
    for blk in range(w // LANES):
        stage_ref[...] = src[:, blk * LANES:(blk + 1) * LANES]
        for r in range(dil):
            lo = r * w + blk * LANES
            dst_ref[:, lo:lo + LANES] = stage_ref[pl.ds(r, rows // dil, stride=dil), :]


def _from_residue_view(src_ref, stage_ref, dil, w):
    rows = src_ref.shape[0] * dil
    parts = []
    for blk in range(w // LANES):
        for r in range(dil):
            lo = r * w + blk * LANES
            stage_ref[pl.ds(r, rows // dil, stride=dil), :] = src_ref[:, lo:lo + LANES]
        parts.append(stage_ref[...])
    return jnp.concatenate(parts, axis=1)


def _inproj_odd_body(x_ref, g_ref, w_ref, gq_ref, gk_ref, bd_ref, *outs, prompt):
    xn = _rms(x_ref[...], g_ref[...]).astype(BF16)
    p = _dot(xn, w_ref[...])
    bd = bd_ref[...]
    q = _seg_rms(p[:, :ODD_W], bd, gq_ref[...]) * (HEAD_DIM ** -0.5)
    k = _seg_rms(p[:, ODD_W:2 * ODD_W], bd, gk_ref[...])
    v = p[:, 2 * ODD_W:]
    for g in range(3):
        s = slice(ODD_GW * g, ODD_GW * (g + 1))
        kv = jnp.concatenate([k[:, s], v[:, s]], axis=1)
        if not prompt:
            outs[0][g] = q[:, s]
            outs[1][g] = kv
            continue
        dil = DIL_PATTERN[g][1]
        if dil == 1:
            outs[2 * g][...] = q[:, s]
            outs[2 * g + 1][...] = kv
        else:
            _to_residue_view(q[:, s], outs[2 * g], outs[-1], dil)
            _to_residue_view(kv, outs[2 * g + 1], outs[-1], dil)
        _store_transposed(outs[6 + g], kv)


def _inproj_odd(h, gain, w, qk_norm, *, tm, seq=None):
    n, d = h.shape
    row = lambda i: (i, 0)
    nh = 3 * DIL_HEADS
    prompt = seq is not None
    if not prompt:
        out_specs = [pl.BlockSpec((3, tm, ODD_GW), lambda i: (0, i, 0)),
                     pl.BlockSpec((3, tm, 2 * ODD_GW), lambda i: (0, i, 0))]
        out_shape = [jax.ShapeDtypeStruct((3, n, ODD_GW), F32), jax.ShapeDtypeStruct((3, n, 2 * ODD_GW), F32)]
    else:
        out_specs, out_shape = [], []
        for _, dil in DIL_PATTERN:
            assert dil == 1 or DIL_PATTERN[0][1] == 1
            for width in (ODD_GW, 2 * ODD_GW):
                out_specs.append(pl.BlockSpec((tm // dil, dil * width), row))
                out_shape.append(jax.ShapeDtypeStruct((n // dil, dil * width), F32))
        tpb = seq // tm
        for _ in DIL_PATTERN:
            out_specs.append(pl.BlockSpec((1, 2 * ODD_GW, tm), lambda i: (i // tpb, 0, i % tpb)))
            out_shape.append(jax.ShapeDtypeStruct((n // seq, 2 * ODD_GW, seq), F32))
    return pl.pallas_call(
        functools.partial(_inproj_odd_body, prompt=prompt), grid=(n // tm,),
        in_specs=[pl.BlockSpec((tm, d), row), _const((1, d)), _const(w.shape), _const((1, ODD_W)),
                  _const((1, ODD_W)), _const(_block_diag_ones(ODD_W).shape)],
        out_specs=out_specs, out_shape=out_shape,
        scratch_shapes=[pltpu.VMEM((tm, LANES), F32)] if prompt else [],
        compiler_params=_cparams("parallel"), name="inproj_odd",
    )(h, gain.reshape(1, d), w, jnp.tile(qk_norm[0], nh).reshape(1, ODD_W),
      jnp.tile(qk_norm[1], nh).reshape(1, ODD_W), _block_diag_ones(ODD_W))


def _outproj2_body(h_ref, a_ref, b_ref, w_ref, o_ref):
    ka = a_ref.shape[1]
    o_ref[...] = (h_ref[...] + _dot(a_ref[...].astype(BF16), w_ref[:ka])
                  + _dot(b_ref[...].astype(BF16), w_ref[ka:]))


def _outproj2(h, a, b, w, *, tm):
    n, d = h.shape
    row = lambda i: (i, 0)
    return pl.pallas_call(
        _outproj2_body, grid=(n // tm,),
        in_specs=[pl.BlockSpec((tm, d), row), pl.BlockSpec((tm, a.shape[1]), row),
                  pl.BlockSpec((tm, b.shape[1]), row), _const(w.shape)],
        out_specs=pl.BlockSpec((tm, d), row), out_shape=jax.ShapeDtypeStruct((n, d), F32),
        compiler_params=_cparams("parallel"), name="outproj_even",
    )(h, a, b, w)


def _outproj2t_body(h_ref, a_ref, bt_ref, w_ref, o_ref):
    ka = a_ref.shape[1]
    acc = h_ref[...] + _dot(a_ref[...].astype(BF16), w_ref[:ka])
    tq = bt_ref.shape[2]
    for i in range(bt_ref.shape[0]):
        rs = slice(i * tq, (i + 1) * tq)
        o_ref[rs, :] = acc[rs, :] + _dot_tn(bt_ref[i].astype(BF16), w_ref[ka:])


def _outproj2t(h, a, bt, w, *, tm):
    n, d = h.shape
    row = lambda i: (i, 0)
    tq = bt.shape[2]
    return pl.pallas_call(
        _outproj2t_body, grid=(n // tm,),
        in_specs=[pl.BlockSpec((tm, d), row), pl.BlockSpec((tm, a.shape[1]), row),
                  pl.BlockSpec((tm // tq, bt.shape[1], tq), lambda i: (i, 0, 0)), _const(w.shape)],
        out_specs=pl.BlockSpec((tm, d), row), out_shape=jax.ShapeDtypeStruct((n, d), F32),
        compiler_params=_cparams("parallel"), name="outproj_even_t",
    )(h, a, bt, w)


def _outproj_merge_body(h_ref, o0_ref, o1_ref, o2_ref, l0_ref, l1_ref, l2_ref, w_ref, o_ref, *stage, views):
    if views:
        load = lambda ref, g: (ref[...] if DIL_PATTERN[g][1] == 1
                               else _from_residue_view(ref, stage[0], DIL_PATTERN[g][1], ODD_GW))
    else:
        load = lambda ref, g: ref[...]
    mixed = _merge_groups([load(r, g) for g, r in enumerate((o0_ref, o1_ref, o2_ref))],
                          [load(r, g) for g, r in enumerate((l0_ref, l1_ref, l2_ref))])
    o_ref[...] = h_ref[...] + _dot(mixed.astype(BF16), w_ref[...])


def _outproj_merge(h, outs, lses, w, *, tm, views=False):
    n, d = h.shape
    row = lambda i: (i, 0)
    parts = [pl.BlockSpec((tm * a.shape[0] // n, a.shape[1]), row) for a in list(outs) + list(lses)]
    return pl.pallas_call(
        functools.partial(_outproj_merge_body, views=views), grid=(n // tm,),
        in_specs=[pl.BlockSpec((tm, d), row)] + parts + [_const(w.shape)],
        out_specs=pl.BlockSpec((tm, d), row), out_shape=jax.ShapeDtypeStruct((n, d), F32),
        scratch_shapes=[pltpu.VMEM((tm, LANES), F32)] if views else [],
        compiler_params=_cparams("parallel"), name="outproj_odd",
    )(h, *outs, *lses, w)


HG_MM_LEVELS = 2


def _hgrn_prefix_matrix(C):
    t = np.arange(C)[:, None]
    j = np.arange(C)[None, :]
    mats = [j <= t]
    for lvl in range(1, HG_MM_LEVELS + 1):
        if (1 << lvl) <= C:
            mats.append(j <= ((t >> lvl) << lvl) + (1 << lvl) // 2 - 1)
    return jnp.asarray(np.concatenate(mats, axis=0).astype(np.float32), BF16)


def _hgrn_body(hq_ref, hf_ref, hi_ref, hgt_ref, lb_ref, gn_ref, bd_ref, pm_ref, s0_ref, o_ref, sfin_ref, st_ref,
               *b_scr, C, U):
    c = pl.program_id(1)

    @pl.when(c == 0)
    def _():
        st_ref[...] = s0_ref[0]

    nb = len(b_scr) // U
    for u in range(U):
        _hgrn_chunk(hq_ref, hf_ref, hi_ref, hgt_ref, lb_ref, gn_ref, bd_ref, pm_ref, o_ref, st_ref,
                    b_scr[u * nb:(u + 1) * nb], slice(u * C, (u + 1) * C), C)

    @pl.when(c == pl.num_programs(1) - 1)
    def _():
        sfin_ref[0] = st_ref[...]


def _hgrn_chunk(hq_ref, hf_ref, hi_ref, hgt_ref, lb_ref, gn_ref, bd_ref, pm_ref, o_ref, st_ref, b_scr, rs, C):
    dk = HEAD_DIM
    w = HG_HEADS * dk
    lb = lb_ref[...]
    f = lb + (1.0 - lb) * jax.nn.sigmoid(hf_ref[rs, :])
    lf = jnp.log(f)
    kk = 1.0 - f
    ti = lax.broadcasted_iota(jnp.int32, (C, C), 0)
    si = lax.broadcasted_iota(jnp.int32, (C, C), 1)
    p1 = lf.astype(BF16)
    r1 = lf - p1.astype(F32)
    p2 = r1.astype(BF16)
    p3 = (r1 - p2.astype(F32)).astype(BF16)
    sums = _dot(pm_ref[...], jnp.concatenate([p1, p2, p3], axis=1))
    sums = sums[:, :w] + sums[:, w:2 * w] + sums[:, 2 * w:]
    b = sums[0:C]
    for c4, scr in enumerate(b_scr):
        scr[...] = b[:, c4 * LANES:(c4 + 1) * LANES]
    q = hq_ref[rs, :]
    v = hi_ref[rs, :].astype(BF16)
    heads = range(HG_HEADS)
    hs = [slice(dk * h, dk * (h + 1)) for h in heads]

    qb = q.astype(BF16)
    kb = kk.astype(BF16)
    a = [jnp.where(ti == si, _dot_nt(qb[:, hs[h]], kb[:, hs[h]]), 0.0) for h in heads]
    lvl = 1
    while (1 << lvl) <= C:
        m = 1 << lvl
        half = m // 2
        if lvl <= HG_MM_LEVELS:
            r = sums[lvl * C:(lvl + 1) * C]
        else:
            ref_rows = pl.ds(half - 1, C // m, stride=m)
            r = jnp.concatenate([jnp.broadcast_to(scr[ref_rows, :][:, None, :], (C // m, m, LANES)).reshape(C, LANES)
                                 for scr in b_scr], axis=1)
        e = jnp.exp(-jnp.abs(b - r))
        qt = (q * e).astype(BF16)
        kt = (kk * e).astype(BF16)
        pair = ((ti >> lvl) == (si >> lvl)) & ((ti & (m - 1)) >= half) & ((si & (m - 1)) < half)
        a = [a[h] + jnp.where(pair, _dot_nt(qt[:, hs[h]], kt[:, hs[h]]), 0.0) for h in heads]
        lvl += 1

    b_last = b[C - 1:C, :]
    qe = (q * jnp.exp(b)).astype(BF16)
    kdec = (kk * jnp.exp(b_last - b)).astype(BF16)
    e_last = jnp.exp(b_last)
    outs = []
    for h in heads:
        st = st_ref[h]
        outs.append(_dot(a[h].astype(BF16), v[:, hs[h]]) + _dot_nt(qe[:, hs[h]], st.astype(BF16)))
        st_ref[h] = st * e_last[:, hs[h]] + _dot_tn(v[:, hs[h]], kdec[:, hs[h]])
    o = jnp.concatenate(outs, axis=1)
    gt = hgt_ref[rs, :]
    o_ref[rs, :] = _seg_rms(o, bd_ref[...], gn_ref[...]) * (gt * jax.nn.sigmoid(gt))


HG_CHUNKS_PER_STEP = 2


def _hgrn(hg, lb, hg_norm, s0, *, batch, seq):
    C = math.gcd(seq, HG_CHUNK)
    U = math.gcd(seq // C, HG_CHUNKS_PER_STEP)
    nc = seq // (C * U)
    w = HG_HEADS * HEAD_DIM
    col = lambda j: pl.BlockSpec((U * C, w), lambda b, c: (b * nc + c, j))
    st_spec = pl.BlockSpec((1, HG_HEADS, HEAD_DIM, HEAD_DIM), lambda b, c: (b, 0, 0, 0))
    pm = _hgrn_prefix_matrix(C)
    o, sfin = pl.pallas_call(
        functools.partial(_hgrn_body, C=C, U=U), grid=(batch, nc),
        in_specs=[col(0), col(1), col(2), col(3), _const((1, w)), _const((1, w)), _const(_block_diag_ones(w).shape), _const(pm.shape),
                  st_spec],
        out_specs=[pl.BlockSpec((U * C, w), lambda b, c: (b * nc + c, 0)), st_spec],
        out_shape=[jax.ShapeDtypeStruct((batch * seq, w), F32),
                   jax.ShapeDtypeStruct((batch, HG_HEADS, HEAD_DIM, HEAD_DIM), F32)],
        scratch_shapes=[pltpu.VMEM((HG_HEADS, HEAD_DIM, HEAD_DIM), F32)]
        + [pltpu.VMEM((C, LANES), F32)] * (U * w // LANES),
        compiler_params=_cparams("parallel", "arbitrary"), name="hgrn_scan",
    )(hg, hg, hg, hg, lb.reshape(1, w), jnp.tile(hg_norm, HG_HEADS).reshape(1, w), _block_diag_ones(w), pm,
      jnp.swapaxes(s0, 2, 3))
    return o, jnp.swapaxes(sfin, 2, 3)


CHUNKS_PER_PAGE = PAGE // CMP_STRIDE
PAGE_LANES = CMP_STRIDE * ROWS_NSA
CMP_HID = 2 * NSA_KV * 2 * HEAD_DIM
CMP_PP_MAX = 32


def _compress_body(pt_ref, *refs, n_chunks, transposed, CMP_PP):
    pages = refs[:CMP_PP]
    wbig_ref, pe_ref, w2_ref, gk_ref, km_ref, bd_ref, perm_ref, rows_ref, grp_ref, xs_ref = refs[CMP_PP:]
    nb = ROWS_NSA // LANES
    j = pl.program_id(1)
    perm = perm_ref[...]
    for i in range(0, CMP_PP, 2):
        pair = []
        for k in range(2):
            pg = pages[i + k][0]
            if transposed:
                pg = jnp.concatenate([pg[c * LANES:(c + 1) * LANES, :].T for c in range(nb)], axis=1)
            pair.append(_dot(perm, pg.astype(BF16)))
        r0 = pl.multiple_of((j * CMP_PP + i) * CHUNKS_PER_PAGE, 2 * CHUNKS_PER_PAGE)
        for s in range(CMP_STRIDE):
            rs = slice(s * CHUNKS_PER_PAGE, (s + 1) * CHUNKS_PER_PAGE)
            both = jnp.concatenate([pair[0][rs], pair[1][rs]], axis=0).astype(BF16)
            for c in range(nb):
                lo = (c * CMP_STRIDE + s) * LANES
                xs_ref[pl.ds(r0, 2 * CHUNKS_PER_PAGE), lo:lo + LANES] = both[:, c * LANES:(c + 1) * LANES]

    @pl.when(j == pl.num_programs(1) - 1)
    def _():
        blk_w = CMP_STRIDE * LANES
        first = lambda x: jnp.concatenate(
            [_dot(x[:, c * blk_w:(c + 1) * blk_w], wbig_ref[c]) for c in range(nb)], axis=1)
        y = first(xs_ref[...])
        ysh = pltpu.roll(y, n_chunks - 1, 0)
        hid = y + pltpu.roll(ysh, CMP_HID - HEAD_DIM, 1)
        cf = first(pe_ref[...].astype(BF16))
        lane = lax.broadcasted_iota(jnp.int32, (8, CMP_HID), 1)
        cs = jnp.where(((lane >> 6) & 1) == 0, jnp.broadcast_to(cf[0:1], (8, CMP_HID)),
                       jnp.broadcast_to(cf[1:2], (8, CMP_HID)))
        cs = cs + pltpu.roll(cs, CMP_HID - HEAD_DIM, 1)
        hid = hid + cs[0:1]
        act = (hid * jax.nn.sigmoid(hid)).astype(BF16)
        out = _dot(act, w2_ref[...])
        out = _norm_k_rows(out, bd_ref[...], gk_ref[...], km_ref[...])
        rows_ref[0] = out
        kw = NSA_KV * HEAD_DIM
        for g in range(NSA_KV):
            grp_ref[0, g] = jnp.concatenate([out[:, HEAD_DIM * g:HEAD_DIM * (g + 1)],
                                             out[:, kw + HEAD_DIM * g:kw + HEAD_DIM * (g + 1)]], axis=1)


def _compress_weights(cmp_pe, cmp_w1, cmp_w2, k_gain):
    ratio = CMP_BLOCK // CMP_STRIDE
    w1 = cmp_w1.reshape(2, ratio, CMP_STRIDE, HEAD_DIM, HEAD_DIM)
    eye_kv = jnp.eye(2, dtype=F32)
    eye_g = jnp.eye(NSA_KV, dtype=F32)
    nb = ROWS_NSA // LANES
    member_kv = np.arange(2 * NSA_KV) // NSA_KV
    w1m = w1[member_kv].reshape(nb, 2, ratio, CMP_STRIDE, HEAD_DIM, HEAD_DIM)
    wbig = jnp.einsum('chrsde,hH->cshdHre', w1m, eye_kv).reshape(nb, CMP_STRIDE * LANES, 2 * ratio * HEAD_DIM)
    wbig = wbig.astype(BF16)
    pe = cmp_pe.reshape(2, ratio, CMP_STRIDE, HEAD_DIM)
    pe_rows = pe[member_kv].reshape(nb, 2, ratio, CMP_STRIDE, HEAD_DIM).transpose(2, 0, 3, 1, 4)
    pe_rows = pe_rows.reshape(ratio, PAGE_LANES)
    pe_rows = jnp.concatenate([pe_rows, jnp.zeros((8 - ratio, PAGE_LANES), F32)], axis=0)
    sel_r0 = jnp.asarray([1.0, 0.0], F32)
    w2big = jnp.einsum('ked,kK,gG,r->kgreKGd', cmp_w2, eye_kv, eye_g, sel_r0).reshape(CMP_HID, ROWS_NSA).astype(BF16)
    kw = NSA_KV * HEAD_DIM
    gk = jnp.concatenate([jnp.tile(k_gain, NSA_KV), jnp.ones((kw,), F32)]).reshape(1, ROWS_NSA)
    km = jnp.concatenate([jnp.ones((kw,), F32), jnp.zeros((kw,), F32)]).reshape(1, ROWS_NSA)
    return wbig, pe_rows, w2big, gk, km


def _compress(pool, table, cw, *, transposed):
    wbig, pe_rows, w2big, gk, km = cw
    bsz, npg = table.shape
    n_chunks = npg * CHUNKS_PER_PAGE
    CMP_PP = math.gcd(npg, CMP_PP_MAX)
    assert CMP_PP % 2 == 0
    page = lambda i: pl.BlockSpec((1,) + pool.shape[1:], lambda b, j, pt: (pt[b, j * CMP_PP + i], 0, 0))
    tok = np.arange(PAGE)
    perm = (tok[None, :] == (tok[:, None] % CHUNKS_PER_PAGE) * CMP_STRIDE + tok[:, None] // CHUNKS_PER_PAGE)
    perm = jnp.asarray(perm.astype(np.float32), BF16)
    return pl.pallas_call(
        functools.partial(_compress_body, n_chunks=n_chunks, transposed=transposed, CMP_PP=CMP_PP),
        grid_spec=pltpu.PrefetchScalarGridSpec(
            num_scalar_prefetch=1, grid=(bsz, npg // CMP_PP),
            in_specs=[page(i) for i in range(CMP_PP)] + [
                _const(wbig.shape), _const(pe_rows.shape), _const(w2big.shape), _const((1, ROWS_NSA)),
                _const((1, ROWS_NSA)), _const(_block_diag_ones(ROWS_NSA).shape), _const(perm.shape)],
            out_specs=[pl.BlockSpec((1, n_chunks, ROWS_NSA), lambda b, j, pt: (b, 0, 0)),
                       pl.BlockSpec((1, NSA_KV, n_chunks, LANES), lambda b, j, pt: (b, 0, 0, 0))],
            scratch_shapes=[pltpu.VMEM((n_chunks, PAGE_LANES), BF16)]),
        out_shape=[jax.ShapeDtypeStruct((bsz, n_chunks, ROWS_NSA), F32),
                   jax.ShapeDtypeStruct((bsz, NSA_KV, n_chunks, LANES), F32)],
        compiler_params=_cparams("parallel", "arbitrary"), name="nsa_compress",
    )(table, *([pool] * CMP_PP), wbig, pe_rows, w2big, gk, km, _block_diag_ones(ROWS_NSA), perm)


def _select_blocks(imp, pos, n_slc, k_sel):
    lane = lax.broadcasted_iota(jnp.int32, imp.shape, 1)
    cur = pos >> 6
    forced = (lane == 0) | (lane == cur) | (lane == cur - 1)
    score = jnp.where(forced, FORCE_SCORE, jnp.where(lane <= cur, imp, -1.0))
    rank = jnp.zeros(imp.shape, F32)
    for i in range(n_slc):
        ci = score[:, i:i + 1]
        rank = rank + jnp.where(ci > score, 1.0, jnp.where(ci == score, jnp.where(lane > i, 1.0, 0.0), 0.0))
    return jnp.where(lane < n_slc, jnp.where(rank < k_sel, 1.0, 0.0), 0.0)


NSA_TQ = 256
NSA_QC = LANES


def _select_blocks_t(imp, pos, n_slc, k_sel):
    blk = lax.broadcasted_iota(jnp.int32, imp.shape, 0)
    cur = pos >> 6
    forced = (blk == 0) | (blk == cur) | (blk == cur - 1)
    score = jnp.where(forced, FORCE_SCORE, jnp.where(blk <= cur, imp, -1.0))
    rank = jnp.zeros(imp.shape, F32)
    for i in range(n_slc):
        ci = score[i:i + 1, :]
        rank = rank + jnp.where(ci > score, 1.0, jnp.where(ci == score, jnp.where(blk > i, 1.0, 0.0), 0.0))
    return jnp.where(rank < k_sel, 1.0, 0.0)


def _nsa_prompt_t_body(q_ref, gate_ref, kcv_ref, slc_ref, win_ref, gq_ref, bd_ref, eye_ref, fs_ref, fc_ref,
                       ov_ref, ex_ref, o_ref, bc_ref, bs_ref, bw_ref, qt_scr, ks_scr, kw_scr, vs_scr, vw_scr, sel_scr, m_scr, l_scr, acc_scr,
                       *, seq, n_slc, k_sel):
    tq = NSA_TQ
    qc = NSA_QC
    hd = HEAD_DIM
    n_tiles = seq // tq
    width = NSA_GROUP * tq
    n_chunks = width // qc
    eye = eye_ref[...]
    n_c = kcv_ref.shape[2]
    far = NSA_WINDOW // tq

    @pl.when(pl.program_id(1) == 0)
    def _():
        key_i = lax.broadcasted_iota(jnp.int32, (tq, tq), 0)
        qry_i = lax.broadcasted_iota(jnp.int32, (tq, tq), 1)
        blk_i = lax.broadcasted_iota(jnp.int32, (n_c, tq), 0)
        n_cmp = seq // CMP_STRIDE - CMP_BLOCK // CMP_STRIDE + 1
        span = tq + CMP_STRIDE * n_c
        for h in range(NSA_GROUP):
            hs = slice(h * tq, (h + 1) * tq)
            for d in range(n_tiles):
                u = jnp.broadcast_to(fs_ref[0, h:h + 1, d * tq:(d + 2) * tq], (tq, 2 * tq))
                tile = pltpu.roll(u, tq, 1, stride=1, stride_axis=0)[:, :tq]
                bs_ref[d, :, hs] = tile
                if d == far:
                    bw_ref[0, :, hs] = jnp.where(qry_i <= key_i, tile, NEG)
            for t in range(n_tiles):
                u = jnp.broadcast_to(fc_ref[0, h:h + 1, t * tq:t * tq + span], (n_c, span))
                tile = pltpu.roll(u, tq, 1, stride=CMP_STRIDE, stride_axis=0)[:, :tq]
                bc_ref[t, :, hs] = jnp.where(blk_i < n_cmp, tile, NEG)

    qn = (_seg_rms(q_ref[...], bd_ref[...], gq_ref[...]) * (hd ** -0.5)).astype(BF16)
    for t in range(n_tiles):
        rs = slice(t * tq, (t + 1) * tq)
        for h in range(NSA_GROUP):
            qt_scr[t, :, h * tq:(h + 1) * tq] = _dot_nt(eye, qn[rs, h * hd:(h + 1) * hd]).astype(BF16)
        for kv_ref, k_scr, v_scr in ((slc_ref, ks_scr, vs_scr), (win_ref, kw_scr, vw_scr)):
            kv = kv_ref[0, rs, :].astype(BF16)
            k_scr[t] = kv[:, :hd]
            v_scr[t] = _dot_nt(eye, kv[:, hd:]).astype(BF16)
    kc = kcv_ref[0, 0, :, :hd].astype(BF16)
    vct = _dot_nt(eye, kcv_ref[0, 0, :, hd:].astype(BF16)).astype(BF16)
    def flash_init():
        m_scr[...] = jnp.full((1, width), NEG, F32)
        l_scr[...] = jnp.zeros((1, width), F32)
        acc_scr[...] = jnp.zeros((hd, width), F32)

    def flash_update(k_scr, v_scr, t, tiles, running):
        s_alls = [_dot(k_scr[jt], qt_scr[t]) for jt, _, _ in tiles]
        if running:
            m_all = m_scr[...]
            l_all = l_scr[...]
        alphas, ls, ms, ps = [], [], [], [[] for _ in tiles]
        for c in range(n_chunks):
            cs = slice(c * qc, (c + 1) * qc)
            qoff = (c * qc) % tq
            ss = []
            for (_, bias_tile, sel_add), s_all in zip(tiles, s_alls):
                s = s_all[:, cs] + bias_tile(cs)
                ss.append(s if sel_add is None else s + sel_add[:, qoff:qoff + qc])
            m_new = functools.reduce(jnp.maximum, [jnp.max(s, axis=0, keepdims=True) for s in ss])
            if running:
                m_new = jnp.maximum(m_all[:, cs], m_new)
                alpha = jnp.exp(m_all[:, cs] - m_new)
                alphas.append(alpha)
            l_new = alpha * l_all[:, cs] if running else 0.0
            for k, s in enumerate(ss):
                p = jnp.exp(s - m_new)
                l_new = l_new + jnp.sum(p, axis=0, keepdims=True)
                ps[k].append(p.astype(BF16))
            ls.append(l_new)
            ms.append(m_new)
        cat = lambda xs: jnp.concatenate(xs, axis=1)
        pv = sum(_dot(v_scr[jt], cat(ps[k])) for k, (jt, _, _) in enumerate(tiles))
        acc_scr[...] = cat(alphas) * acc_scr[...] + pv if running else pv
        m_scr[...] = cat(ms)
        l_scr[...] = cat(ls)

    def flash_result():
        return acc_scr[...] / jnp.maximum(l_scr[...], 1e-30)

    def q_tile(t, carry):
        bias_c = bc_ref[t]
        s_c = _dot(kc, qt_scr[t]) + bias_c
        m_c = jnp.max(s_c, axis=0, keepdims=True)
        e_c = jnp.where(bias_c > NEG_TEST, jnp.exp(s_c - m_c), 0.0)
        p_c = (e_c / jnp.maximum(jnp.sum(e_c, axis=0, keepdims=True), 1e-30)).astype(BF16)
        o_c = _dot(vct, p_c)
        imp = sum(_dot(ov_ref[...], p_c[:, h * tq:(h + 1) * tq]) for h in range(NSA_GROUP))
        pos = t * tq + lax.broadcasted_iota(jnp.int32, (1, tq), 1)
        sel = _select_blocks_t(imp[:n_slc], pos, n_slc, k_sel)
        sel_scr[...] = jnp.zeros(sel_scr.shape, BF16)
        sel_scr[0:n_slc, :] = ((1.0 - sel) * NEG).astype(BF16)
        slc_tile = lambda jt: (jt, lambda cs: bs_ref[t - jt, :, cs], _dot(ex_ref[jt], sel_scr[...]))
        for j0 in range(0, t + 1, 2):
            flash_update(ks_scr, vs_scr, t, [slc_tile(j) for j in range(j0, min(j0 + 2, t + 1))], j0 > 0)
        o_s = flash_result()
        far = NSA_WINDOW // tq
        win_tile = lambda d: (t - d, (lambda cs: bw_ref[0, :, cs]) if d == far else (lambda cs: bs_ref[d, :, cs]),
                              None)
        flash_update(kw_scr, vw_scr, t, [win_tile(d) for d in range(min(t, far), -1, -1)], False)
        o_w = flash_result()
        gates = gate_ref[0, t]
        for h in range(NSA_GROUP):
            hs = slice(h * tq, (h + 1) * tq)
            o_ref[t, h * hd:(h + 1) * hd, :] = (gates[3 * h:3 * h + 1] * o_c[:, hs]
                                                + gates[3 * h + 1:3 * h + 2] * o_s[:, hs]
                                                + gates[3 * h + 2:3 * h + 3] * o_w[:, hs])
        return carry

    for t in range(n_tiles):
        q_tile(t, 0)


def _nsa_prompt_t_tables(rel_bias, seq, n_c):
    tq = NSA_TQ
    n_tiles = seq // tq
    n_slc = -(-seq // SEL_BLOCK)
    ds = jnp.arange((n_tiles + 1) * tq) - tq
    fs = _bias_tile(rel_bias, ds, ds >= 0).reshape(NSA_KV, NSA_GROUP, -1)
    dc = jnp.arange(n_tiles * tq + CMP_STRIDE * n_c) - (CMP_STRIDE * n_c + CMP_BLOCK - 1)
    fc = _bias_tile(rel_bias, dc, dc >= 0).reshape(NSA_KV, NSA_GROUP, -1)
    n_cmp = seq // CMP_STRIDE - CMP_BLOCK // CMP_STRIDE + 1
    c_lo = np.arange(n_c) * CMP_STRIDE
    s_lo = np.arange(LANES) * SEL_BLOCK
    ov = ((c_lo[None, :] <= s_lo[:, None] + SEL_BLOCK - 1) & (c_lo[None, :] + CMP_BLOCK - 1 >= s_lo[:, None])
          & (np.arange(n_c) < n_cmp)[None, :] & (np.arange(LANES) < n_slc)[:, None])
    ex = (np.arange(seq)[:, None] // SEL_BLOCK == np.arange(LANES)[None, :]).reshape(n_tiles, tq, LANES)
    as_bf = lambda a: jnp.asarray(a.astype(np.float32), BF16)
    return fs, fc, as_bf(ov), as_bf(ex), n_slc


def _nsa_prompt_t(q, gates_t, kcv, slcg, wing, q_gain, tables, *, batch, seq):
    fs, fc, ov, ex, n_slc = tables
    n_c = kcv.shape[2]
    tq = NSA_TQ
    n_tiles = seq // tq
    assert n_tiles > NSA_WINDOW // tq
    gw = NSA_GROUP * HEAD_DIM
    width = NSA_GROUP * tq
    per_g = lambda shape: pl.BlockSpec((1,) + shape, lambda g, b: (g,) + (0,) * len(shape),
                                       pipeline_mode=pl.Buffered(1))
    eye = jnp.eye(HEAD_DIM, dtype=BF16)
    return pl.pallas_call(
        functools.partial(_nsa_prompt_t_body, seq=seq, n_slc=n_slc, k_sel=min(SEL_TOPK, n_slc)),
        grid=(NSA_KV, batch),
        in_specs=[pl.BlockSpec((seq, gw), lambda g, b: (b, g)),
                  pl.BlockSpec((1, n_tiles, 16, tq), lambda g, b: (g, b, 0, 0)),
                  pl.BlockSpec((1, 1, n_c, LANES), lambda g, b: (b, g, 0, 0)),
                  pl.BlockSpec((1, seq, LANES), lambda g, b: (g, b, 0)),
                  pl.BlockSpec((1, seq, LANES), lambda g, b: (g, b, 0)),
                  _const((1, gw)), _const(_block_diag_ones(gw).shape), _const(eye.shape),
                  per_g(fs.shape[1:]), per_g(fc.shape[1:]), _const(ov.shape), _const(ex.shape)],
        out_specs=pl.BlockSpec((n_tiles, gw, tq), lambda g, b: (b, g, 0)),
        out_shape=jax.ShapeDtypeStruct((batch * n_tiles, NSA_KV * gw, tq), F32),
        scratch_shapes=[pltpu.VMEM((n_tiles, n_c, width), F32), pltpu.VMEM((n_tiles, tq, width), F32),
                        pltpu.VMEM((1, tq, width), F32), pltpu.VMEM((n_tiles, HEAD_DIM, width), BF16),
                        pltpu.VMEM((n_tiles, tq, HEAD_DIM), BF16), pltpu.VMEM((n_tiles, tq, HEAD_DIM), BF16),
                        pltpu.VMEM((n_tiles, HEAD_DIM, tq), BF16), pltpu.VMEM((n_tiles, HEAD_DIM, tq), BF16),
                        pltpu.VMEM((LANES, tq), BF16), pltpu.VMEM((1, width), F32), pltpu.VMEM((1, width), F32),
                        pltpu.VMEM((HEAD_DIM, width), F32)],
        compiler_params=_cparams("arbitrary", "arbitrary"), name="nsa_prompt",
    )(q, gates_t, kcv, slcg, wing, jnp.tile(q_gain, NSA_GROUP).reshape(1, gw), _block_diag_ones(gw), eye,
      fs, fc, ov, ex)


SMP_PP = 32
NSA_KW = NSA_KV * HEAD_DIM
SEL_LANES = 2 * LANES


def _nsa_sample_body(pt_ref, q_ref, gate_ref, ckv_ref, *refs, n_slc, k_sel, past, tq):
    pages = refs[:SMP_PP]
    (slcn_ref, wst_ref, winn_ref, gq_ref, bd_ref, bc_ref, bs_ref, bsn_ref, bw_ref, bwn_ref, ov_ref, ex_ref, exn_ref,
     o_ref, nwin_ref, qbd_scr, sel_scr, kbuf_scr, vbuf_scr, s_scr, sn_scr, vn_scr, oc_scr, ow_scr,
     pad_scr) = refs[SMP_PP:]
    j = pl.program_id(1)
    hd = HEAD_DIM
    kw = NSA_KW
    rows = NSA_HEADS * tq
    init = (jnp.full((rows, 1), NEG, F32), jnp.zeros((rows, 1), F32), jnp.zeros((rows, kw), F32))

    @pl.when(j == 0)
    def _():
        qn = _seg_rms(q_ref[...], bd_ref[...], gq_ref[...]) * (hd ** -0.5)
        zero = jnp.zeros((tq, hd), F32)
        blocks = []
        for h in range(NSA_HEADS):
            qh = qn[:, hd * h:hd * (h + 1)]
            blocks.append(jnp.concatenate([qh if g == h // NSA_GROUP else zero for g in range(NSA_KV)], axis=1))
        qbd = jnp.concatenate(blocks, axis=0).astype(BF16)
        qbd_scr[...] = qbd
        ckv = ckv_ref[0]
        bias_c = bc_ref[...]
        ok_c = bias_c > NEG_TEST
        s_c = _dot_nt(qbd, ckv[:, :kw].astype(BF16)) + bias_c
        m_c = jnp.max(s_c, axis=-1, keepdims=True)
        e_c = jnp.where(ok_c, jnp.exp(s_c - m_c), 0.0)
        p_c = (e_c / jnp.maximum(jnp.sum(e_c, axis=-1, keepdims=True), 1e-30)).astype(BF16)
        oc_scr[...] = _dot(p_c, ckv[:, kw:].astype(BF16))
        d = _dot(p_c, ov_ref[...])
        gr = NSA_GROUP * tq
        imp = jnp.concatenate(
            [sum(d[g * gr + h * tq:g * gr + (h + 1) * tq] for h in range(NSA_GROUP)) for g in range(NSA_KV)], axis=0)
        pos = past + (lax.broadcasted_iota(jnp.int32, (NSA_KV * tq, 1), 0) & (tq - 1))
        sel = _select_blocks(imp, pos, n_slc, k_sel)
        sel = jnp.concatenate([sel[g * tq:(g + 1) * tq] for g in range(NSA_KV) for _ in range(NSA_GROUP)], axis=0)
        sel = ((1.0 - sel) * NEG).astype(BF16)
        sel_scr[...] = sel
        pad_scr[...] = jnp.zeros(pad_scr.shape, F32)
        pad_scr[0:tq, :] = slcn_ref[...]
        kn = pad_scr[...].astype(BF16)
        sn_scr[...] = _dot_nt(qbd, kn[:, :kw]) + bsn_ref[...] + _dot(sel, exn_ref[...])
        vn_scr[...] = kn[:, kw:]
        pad_scr[0:tq, :] = winn_ref[...]
        nwin_ref[0] = _shifted_buffer(wst_ref[0], pad_scr[...], tq)
        wst = wst_ref[0].astype(BF16)
        wn = pad_scr[...].astype(BF16)
        bias_w = bw_ref[...]
        mw = _softmax_piece(_dot(qbd, wst[:kw]) + bias_w, bias_w > NEG_TEST, *init, wst[kw:], v_t=True)
        bias_wn = bwn_ref[...]
        mw = _softmax_piece(_dot_nt(qbd, wn[:, :kw]) + bias_wn, bias_wn > NEG_TEST, *mw, wn[:, kw:])
        ow_scr[...] = mw[2] / jnp.maximum(mw[1], 1e-30)

    for i in range(SMP_PP):
        pg = pages[i][0].astype(BF16)
        kbuf_scr[:, i * PAGE:(i + 1) * PAGE] = pg[:kw]
        vbuf_scr[j, :, i * PAGE:(i + 1) * PAGE] = pg[kw:]
    s_scr[j] = _dot(qbd_scr[...], kbuf_scr[...]) + bs_ref[j] + _dot(sel_scr[...], ex_ref[j])

    @pl.when(j == pl.num_programs(1) - 1)
    def _():
        s_all = s_scr[...]
        s_new = sn_scr[...]
        m = jnp.maximum(jnp.max(jnp.max(s_all, axis=0), axis=-1, keepdims=True),
                        jnp.max(s_new, axis=-1, keepdims=True))
        p_new = jnp.exp(s_new - m)
        den = jnp.sum(p_new, axis=-1, keepdims=True)
        acc = _dot(p_new.astype(BF16), vn_scr[...])
        for st in range(s_all.shape[0]):
            p = jnp.exp(s_all[st] - m)
            den = den + jnp.sum(p, axis=-1, keepdims=True)
            acc = acc + _dot_nt(p.astype(BF16), vbuf_scr[st])
        o_s = acc / jnp.maximum(den, 1e-30)
        o_c = oc_scr[...]
        o_w = ow_scr[...]
        outs = []
        for h in range(NSA_HEADS):
            g, hg = divmod(h, NSA_GROUP)
            gt = gate_ref[g]
            rs = slice(h * tq, (h + 1) * tq)
            ls = slice(hd * g, hd * (g + 1))
            outs.append(gt[:, 3 * hg:3 * hg + 1] * o_c[rs, ls] + gt[:, 3 * hg + 1:3 * hg + 2] * o_s[rs, ls]
                        + gt[:, 3 * hg + 2:3 * hg + 3] * o_w[rs, ls])
        o_ref[...] = jnp.concatenate(outs, axis=1)


def _nsa_sample_tables(rel_bias, past, tq, n_c, w_state):
    rows = NSA_HEADS * tq
    total = past + tq
    n_cmp = total // CMP_STRIDE - CMP_BLOCK // CMP_STRIDE + 1
    n_slc = -(-total // SEL_BLOCK)
    pos = past + jnp.arange(tq)
    c_hi = jnp.arange(n_c) * CMP_STRIDE + CMP_BLOCK - 1
    dist_c = pos[:, None] - c_hi[None, :]
    bc = _bias_tile(rel_bias, dist_c, (dist_c >= 0) & (jnp.arange(n_c) < n_cmp)[None, :]).reshape(rows, n_c)
    n_steps = past // (SMP_PP * PAGE)
    d_ext = jnp.arange(tq + past - 1) + 1
    bs = _toeplitz(_bias_tile(rel_bias, d_ext, d_ext >= 0), tq, past)
    bs = bs.reshape(rows, n_steps, SMP_PP * PAGE).transpose(1, 0, 2)
    jn = jnp.arange(PAGE)
    dist_n = jnp.arange(tq)[:, None] - jn[None, :]
    new_ok = (dist_n >= 0) & (jn < tq)[None, :]
    bsn = _bias_tile(rel_bias, dist_n, new_ok).reshape(rows, PAGE)
    dist_w = jnp.arange(tq)[:, None] + w_state - jnp.arange(w_state)[None, :]
    bw = _bias_tile(rel_bias, dist_w, (dist_w >= 0) & (dist_w <= NSA_WINDOW)).reshape(rows, w_state)
    bwn = _bias_tile(rel_bias, dist_n, new_ok & (dist_n <= NSA_WINDOW)).reshape(rows, PAGE)
    c_lo = np.arange(n_c) * CMP_STRIDE
    s_lo = np.arange(SEL_LANES) * SEL_BLOCK
    ov = ((c_lo[:, None] <= s_lo[None, :] + SEL_BLOCK - 1) & (c_lo[:, None] + CMP_BLOCK - 1 >= s_lo[None, :])
          & (np.arange(n_c) < n_cmp)[:, None] & (np.arange(SEL_LANES) < n_slc)[None, :])
    blk = np.arange(SEL_LANES)
    ex = (blk[:, None] == (np.arange(past)[None, :] // SEL_BLOCK)).reshape(SEL_LANES, n_steps, SMP_PP * PAGE)
    ex = ex.transpose(1, 0, 2)
    exn = (blk[:, None] == ((past + np.arange(PAGE))[None, :] // SEL_BLOCK)) & (np.arange(PAGE) < tq)[None, :]
    as_bf = lambda a: jnp.asarray(a.astype(np.float32), BF16)
    return bc, bs, bsn, bw, bwn, as_bf(ov), as_bf(ex), as_bf(exn), n_slc


def _nsa_sample(q, gates, crows, slc_pool_t, table, slc_new, win_state_t, win_new, q_gain, tables, *, tq):
    bc, bs, bsn, bw, bwn, ov, ex, exn, n_slc = tables
    bsz, npg = table.shape
    past = npg * PAGE
    n_c = crows.shape[1]
    w_state = win_state_t.shape[2]
    rows = NSA_HEADS * tq
    qw = NSA_HEADS * HEAD_DIM
    page = lambda i: pl.BlockSpec((1, ROWS_NSA, PAGE), lambda b, j, pt: (pt[b, j * SMP_PP + i], 0, 0))
    per_b = lambda shape: pl.BlockSpec((1,) + shape, lambda b, j, pt: (b,) + (0,) * len(shape))
    new_rows = pl.BlockSpec((tq, ROWS_NSA), lambda b, j, pt: (b, 0))
    return pl.pallas_call(
        functools.partial(_nsa_sample_body, n_slc=n_slc, k_sel=min(SEL_TOPK, n_slc), past=past, tq=tq),
        grid_spec=pltpu.PrefetchScalarGridSpec(
            num_scalar_prefetch=1, grid=(bsz, npg // SMP_PP),
            in_specs=[pl.BlockSpec((tq, qw), lambda b, j, pt: (b, 0)),
                      pl.BlockSpec((NSA_KV, tq, LANES), lambda b, j, pt: (0, b, 0)),
                      per_b((n_c, ROWS_NSA))] + [page(i) for i in range(SMP_PP)] + [
                      new_rows, per_b((ROWS_NSA, w_state)), new_rows,
                      _const((1, qw)), _const(_block_diag_ones(qw).shape), _const(bc.shape), _const(bs.shape), _const(bsn.shape),
                      _const(bw.shape), _const(bwn.shape), _const(ov.shape), _const(ex.shape), _const(exn.shape)],
            out_specs=[pl.BlockSpec((tq, qw), lambda b, j, pt: (b, 0)), per_b((ROWS_NSA, w_state))],
            scratch_shapes=[pltpu.VMEM((rows, NSA_KW), BF16), pltpu.VMEM((rows, SEL_LANES), BF16),
                            pltpu.VMEM((NSA_KW, SMP_PP * PAGE), BF16),
                            pltpu.VMEM((npg // SMP_PP, NSA_KW, SMP_PP * PAGE), BF16),
                            pltpu.VMEM((npg // SMP_PP, rows, SMP_PP * PAGE), F32),
                            pltpu.VMEM((rows, PAGE), F32), pltpu.VMEM((PAGE, NSA_KW), BF16),
                            pltpu.VMEM((rows, NSA_KW), F32), pltpu.VMEM((rows, NSA_KW), F32),
                            pltpu.VMEM((PAGE, ROWS_NSA), F32)]),
        out_shape=[jax.ShapeDtypeStruct((bsz * tq, qw), F32), jax.ShapeDtypeStruct(win_state_t.shape, F32)],
        compiler_params=_cparams("parallel", "arbitrary"), name="nsa_sample",
    )(table, q, gates, crows, *([slc_pool_t] * SMP_PP), slc_new, win_state_t, win_new,
      jnp.tile(q_gain, NSA_HEADS).reshape(1, qw), _block_diag_ones(qw), bc, bs, bsn, bw, bwn, ov, ex, exn)


DIL_TQ = 128


def _head_of_lane(shape):
    return lax.broadcasted_iota(jnp.int32, shape, 1) >> 6


def _stack_heads(qt, lane_head):
    return jnp.concatenate([jnp.where(lane_head == h, qt, 0.0) for h in range(DIL_HEADS)], axis=0).astype(BF16)


def _unstack_heads(r, lane_head, t):
    return sum(jnp.where(lane_head == h, r[h * t:(h + 1) * t], 0.0) for h in range(DIL_HEADS))


def _dil_prompt_body(q_ref, kv_ref, bias_ref, o_ref, lse_ref, *, n_sub, dil):
    tq = DIL_TQ
    gw = ODD_GW
    lane_head = _head_of_lane((tq, gw))
    n_t = n_sub // tq

    def tile(t, r):
        ql = slice(gw * r, gw * (r + 1))
        kl = slice(2 * gw * r, 2 * gw * r + gw)
        vl = slice(2 * gw * r + gw, 2 * gw * (r + 1))
        cur_rows = slice(t * tq, (t + 1) * tq)
        prev_rows = slice(max(t - 1, 0) * tq, (max(t - 1, 0) + 1) * tq)
        kk = jnp.concatenate([kv_ref[prev_rows, kl], kv_ref[cur_rows, kl]], axis=0).astype(BF16)
        vv = jnp.concatenate([kv_ref[prev_rows, vl], kv_ref[cur_rows, vl]], axis=0).astype(BF16)
        bias = bias_ref[min(t, 1)]
        s = _dot_nt(_stack_heads(q_ref[cur_rows, ql], lane_head), kk) + bias
        m = jnp.max(s, axis=-1, keepdims=True)
        e = jnp.where(bias > NEG_TEST, jnp.exp(s - m), 0.0)
        l = jnp.maximum(jnp.sum(e, axis=-1, keepdims=True), 1e-30)
        res = _dot((e / l).astype(BF16), vv)
        lse = jnp.broadcast_to(m + jnp.log(l), (DIL_HEADS * tq, gw))
        o_ref[cur_rows, ql] = _unstack_heads(res, lane_head, tq)
        lse_ref[cur_rows, ql] = _unstack_heads(lse, lane_head, tq)

    for r in range(dil):
        for t in range(n_t):
            tile(t, r)


def _merge_groups(outs, lses):
    mx = functools.reduce(jnp.maximum, lses)
    ws = [jnp.exp(l - mx) for l in lses]
    return sum(w * o for w, o in zip(ws, outs)) / sum(ws)


def _dil_prompt_tables(rel_bias):
    tq = DIL_TQ
    m_ext = jnp.arange(3 * tq - 1) - (tq - 1)
    cur_tile = jnp.arange(2 * tq) >= tq
    tiles = []
    for gi, (win, dil) in enumerate(DIL_PATTERN):
        assert win // dil == tq
        t = _toeplitz(_bias_tile(rel_bias, m_ext * dil, (m_ext >= 0) & (m_ext <= tq)), tq, 2 * tq)
        t = t[gi * DIL_HEADS:(gi + 1) * DIL_HEADS].reshape(DIL_HEADS * tq, 2 * tq)
        tiles.append(jnp.stack([jnp.where(cur_tile, t, NEG), t]))
    return tiles


def _dil_prompt(q_views, kv_views, tables, *, batch, seq):
    gw = ODD_GW
    outs, lses = [], []
    for gi, (_, dil) in enumerate(DIL_PATTERN):
        n_sub = seq // dil
        assert n_sub % DIL_TQ == 0
        qv, kvv = q_views[gi], kv_views[gi]
        o_spec = pl.BlockSpec((n_sub, dil * gw), lambda b: (b, 0))
        o, lse = pl.pallas_call(
            functools.partial(_dil_prompt_body, n_sub=n_sub, dil=dil), grid=(batch,),
            in_specs=[o_spec, pl.BlockSpec((n_sub, dil * 2 * gw), lambda b: (b, 0)), _const(tables[gi].shape)],
            out_specs=[o_spec, o_spec],
            out_shape=[jax.ShapeDtypeStruct(qv.shape, F32)] * 2,
            compiler_params=_cparams("parallel"), name=f"dilated_prompt_g{gi}",
        )(qv, kvv, tables[gi])
        outs.append(o)
        lses.append(lse)
    return outs, lses


def _shifted_buffer(st, new_rows_padded, tq):
    ln = st.shape[1]
    shifted = pltpu.roll(st, ln - tq, 1)
    new_t = pltpu.roll(new_rows_padded.T, LANES - tq, 1)
    lane = lax.broadcasted_iota(jnp.int32, (st.shape[0], LANES), 1)
    tail = jnp.where(lane >= LANES - tq, new_t, shifted[:, ln - LANES:])
    return tail if ln == LANES else jnp.concatenate([shifted[:, :ln - LANES], tail], axis=1)


def _dil_sample_body(q_ref, kvn_ref, st0_ref, st1_ref, st2_ref, b0_ref, b1_ref, b2_ref, bn_ref, o_ref, lse_ref,
                     n0_ref, n1_ref, n2_ref, pad_scr, *, tq):
    gw = ODD_GW
    lane_head = _head_of_lane((tq, gw))
    rows = DIL_HEADS * tq
    pad_scr[...] = jnp.zeros(pad_scr.shape, F32)
    for gi, (st_ref, b_ref, new_ref) in enumerate(((st0_ref, b0_ref, n0_ref), (st1_ref, b1_ref, n1_ref),
                                                    (st2_ref, b2_ref, n2_ref))):
        qs = _stack_heads(q_ref[gi], lane_head)
        pad_scr[0:tq, :] = kvn_ref[gi]
        new_ref[0] = _shifted_buffer(st_ref[0], pad_scr[...], tq)
        st = st_ref[0].astype(BF16)
        kn = pad_scr[...].astype(BF16)
        bias = b_ref[...]
        acc = (jnp.full((rows, 1), NEG, F32), jnp.zeros((rows, 1), F32), jnp.zeros((rows, gw), F32))
        acc = _softmax_piece(_dot(qs, st[:gw]) + bias, bias > NEG_TEST, *acc, st[gw:], v_t=True)
        bias_n = bn_ref[gi]
        m, l, a = _softmax_piece(_dot_nt(qs, kn[:, :gw]) + bias_n, bias_n > NEG_TEST, *acc, kn[:, gw:])
        l = jnp.maximum(l, 1e-30)
        o_ref[gi] = _unstack_heads(a / l, lane_head, tq)
        lse_ref[gi] = _unstack_heads(jnp.broadcast_to(m + jnp.log(l), (rows, gw)), lane_head, tq)


def _dil_sample_tables(rel_bias, tq, state_lens):
    rows = DIL_HEADS * tq
    i = jnp.arange(tq)[:, None]
    per_state, per_new = [], []
    jn = jnp.arange(PAGE)[None, :]
    for gi, ((win, dil), ln) in enumerate(zip(DIL_PATTERN, state_lens)):
        hs = slice(gi * DIL_HEADS, (gi + 1) * DIL_HEADS)
        d = i + ln - jnp.arange(ln)[None, :]
        per_state.append(_bias_tile(rel_bias, d, (d % dil == 0) & (d <= win))[hs].reshape(rows, ln))
        dn = i - jn
        per_new.append(_bias_tile(rel_bias, dn, (dn >= 0) & (dn % dil == 0) & (dn <= win) & (jn < tq))[hs]
                       .reshape(rows, PAGE))
    return per_state, jnp.stack(per_new)


def _dil_sample(q3, kv3, states, tables, *, tq):
    per_state, bias_new = tables
    bsz = states[0].shape[0]
    gw = ODD_GW
    st_spec = lambda s: pl.BlockSpec((1,) + s.shape[1:], lambda b: (b, 0, 0))
    grp = lambda w: pl.BlockSpec((3, tq, w), lambda b: (0, b, 0))
    return pl.pallas_call(
        functools.partial(_dil_sample_body, tq=tq), grid=(bsz,),
        in_specs=[grp(gw), grp(2 * gw)] + [st_spec(s) for s in states] + [_const(t.shape) for t in per_state]
        + [_const(bias_new.shape)],
        out_specs=[grp(gw), grp(gw)] + [st_spec(s) for s in states],
        out_shape=[jax.ShapeDtypeStruct((3, bsz * tq, gw), F32)] * 2
        + [jax.ShapeDtypeStruct(s.shape, F32) for s in states],
        scratch_shapes=[pltpu.VMEM((PAGE, 2 * gw), F32)],
        compiler_params=_cparams("parallel"), name="dilated_sample",
    )(q3, kv3, *states, *per_state, bias_new)


def _pad_even_w(w_in):
    return jnp.concatenate([w_in, jnp.zeros((w_in.shape[0], EVEN_PAD - w_in.shape[1]), w_in.dtype)], axis=1).astype(BF16)


def _even_mixer_prompt(h, gain, w_in, w_out, lb, hg_norm, qk_norm, cmp_pe, cmp_w1, cmp_w2, rel_bias, *, batch, seq, tm):
    n = batch * seq
    hg, q, cmp_rows, slcg, wing, gates_t, cmp_t, slc_t, win_t = _inproj_even(
        h, gain, _pad_even_w(w_in), qk_norm, tm=tm, seq=seq)
    s0 = jnp.zeros((batch, HG_HEADS, HEAD_DIM, HEAD_DIM), F32)
    o_h, s_fin = _hgrn(hg, lb, hg_norm, s0, batch=batch, seq=seq)
    cw = _compress_weights(cmp_pe, cmp_w1, cmp_w2, qk_norm[1])
    npg = seq // PAGE
    table = jnp.arange(batch * npg, dtype=jnp.int32).reshape(batch, npg)
    _, kcv = _compress(cmp_rows.reshape(n // PAGE, PAGE, ROWS_NSA), table, cw, transposed=False)
    tables = _nsa_prompt_t_tables(rel_bias, seq, kcv.shape[2])
    o_nt = _nsa_prompt_t(q, gates_t, kcv, slcg, wing, qk_norm[0], tables, batch=batch, seq=seq)
    h = _outproj2t(h, o_h, o_nt, w_out.astype(BF16), tm=tm)
    keep = min(NSA_WINDOW, seq)
    return h, {"hgrn": s_fin, "cmp": _rows_from_t(cmp_t, NSA_KV), "slc": _rows_from_t(slc_t, NSA_KV),
               "win": _rows_from_t(win_t[:, :, seq - keep:], NSA_KV)}


def _rows_from_t(a, heads):
    return jnp.moveaxis(a.reshape(a.shape[0], 2, heads, HEAD_DIM, a.shape[2]), -1, 1)


def _even_mixer_sample(h, gain, w_in, w_out, lb, hg_norm, qk_norm, cmp_pe, cmp_w1, cmp_w2, rel_bias,
                       state_hgrn, cmp_pool, slc_pool, win_state, page_table, *, batch, tq, tm):
    assert tq < CMP_STRIDE and tq & (tq - 1) == 0
    hg, q, cmp_rows, slc_rows, win_rows, gates = _inproj_even(h, gain, _pad_even_w(w_in), qk_norm, tm=tm)
    o_h, s_fin = _hgrn(hg, lb, hg_norm, state_hgrn, batch=batch, seq=tq)
    cw = _compress_weights(cmp_pe, cmp_w1, cmp_w2, qk_norm[1])
    rows_t = lambda a: jnp.moveaxis(a, 1, -1).reshape(a.shape[0], ROWS_NSA, a.shape[1])
    crows, _ = _compress(rows_t(cmp_pool), page_table, cw, transposed=True)
    past = page_table.shape[1] * PAGE
    w_state = win_state.shape[1]
    tables = _nsa_sample_tables(rel_bias, past, tq, crows.shape[1], w_state)
    o_n, new_win_t = _nsa_sample(q, gates, crows, rows_t(slc_pool), page_table, slc_rows, rows_t(win_state), win_rows,
                                 qk_norm[0], tables, tq=tq)
    h = _outproj2(h, o_h, o_n, w_out.astype(BF16), tm=tm)
    new_win = jnp.moveaxis(new_win_t.reshape((batch,) + win_state.shape[2:] + (w_state,)), -1, 1)
    return h, {"hgrn": s_fin, "cmp": cmp_rows, "slc": slc_rows, "win": new_win}


def _odd_mixer_prompt(h, gain, w_in, w_out, qk_norm, rel_bias, *, batch, seq, tm):
    qv0, kvv0, qv1, kvv1, qv2, kvv2, *kv_t = _inproj_odd(h, gain, w_in.astype(BF16), qk_norm, tm=tm, seq=seq)
    outs, lses = _dil_prompt([qv0, qv1, qv2], [kvv0, kvv1, kvv2], _dil_prompt_tables(rel_bias),
                             batch=batch, seq=seq)
    bufs = [_rows_from_t(t[:, :, seq - min(w, seq):], DIL_HEADS) for t, (w, _) in zip(kv_t, DIL_PATTERN)]
    return _outproj_merge(h, outs, lses, w_out.astype(BF16), tm=tm, views=True), bufs


def _odd_mixer_sample(h, gain, w_in, w_out, qk_norm, rel_bias, states, *, batch, tq, tm):
    q3, kv3 = _inproj_odd(h, gain, w_in.astype(BF16), qk_norm, tm=tm)
    tables = _dil_sample_tables(rel_bias, tq, [s.shape[1] for s in states])
    states_t = [jnp.moveaxis(s, 1, -1).reshape(batch, 2 * ODD_GW, s.shape[1]) for s in states]
    o3, lse3, *new_t = _dil_sample(q3, kv3, states_t, tables, tq=tq)
    new = [jnp.moveaxis(n.reshape(s.shape[:1] + s.shape[2:] + s.shape[1:2]), -1, 1) for n, s in zip(new_t, states)]
    h = _outproj_merge(h, [o3[g] for g in range(3)], [lse3[g] for g in range(3)], w_out.astype(BF16), tm=tm)
    return h, new


TM_PROMPT = 512
TM_SAMPLE = 256


def kernel(x_prompt, x_sample, p_prompt, p_sample, state_hgrn, cache_nsa_cmp_kv, cache_nsa_slc_kv, state_nsa_win_kv,
           state_dil_kv_0, state_dil_kv_1, state_dil_kv_2, page_table, rel_bias, norm_ffn1, norm_mix, norm_ffn2,
           norm_ple, w_ffn1_in, w_ffn1_out, w_ffn2_in, w_ffn2_out, w_ple_gate, w_ple_proj, w_in_even, w_out_even,
           hgrn_lb_logits, hgrn_norm, nsa_qk_norm, nsa_cmp_pe, nsa_cmp_w1, nsa_cmp_w2, w_in_odd, w_out_odd,
           dil_qk_norm):
    depth = norm_ffn1.shape[0]
    dil_states = (state_dil_kv_0, state_dil_kv_1, state_dil_kv_2)
    bf = lambda w: w.astype(BF16)
    w1i, w1o, w2i, w2o, wpg, wpp = (bf(w) for w in (w_ffn1_in, w_ffn1_out, w_ffn2_in, w_ffn2_out, w_ple_gate,
                                                     w_ple_proj))
    lb_all = jnp.cumsum(jax.nn.softmax(hgrn_lb_logits.astype(F32), axis=0), axis=0)
    g_ffn1, g_ffn2, g_ple = (g.reshape(depth, 1, -1) for g in (norm_ffn1, norm_ffn2, norm_ple))

    def run(x, p, sample):
        batch, seq, d = x.shape
        n = batch * seq
        tm = TM_SAMPLE if sample else TM_PROMPT
        h = x.reshape(n, d)
        p = p.reshape(depth, n, p.shape[-1])
        even, odd = [], []
        for li in range(depth):
            h = _ffn(h, li, g_ffn1, w1i, w1o, tm=tm)
            if li % 2 == 0:
                ei = li // 2
                args = (h, norm_mix[li], w_in_even[ei], w_out_even[ei], lb_all[ei], hgrn_norm[ei], nsa_qk_norm[ei],
                        nsa_cmp_pe[ei], nsa_cmp_w1[ei], nsa_cmp_w2[ei], rel_bias)
                if sample:
                    h, st = _even_mixer_sample(*args, state_hgrn[ei], cache_nsa_cmp_kv[ei], cache_nsa_slc_kv[ei],
                                               state_nsa_win_kv[ei], page_table, batch=batch, tq=seq, tm=tm)
                else:
                    h, st = _even_mixer_prompt(*args, batch=batch, seq=seq, tm=tm)
                even.append(st)
            else:
                oi = li // 2
                args = (h, norm_mix[li], w_in_odd[oi], w_out_odd[oi], dil_qk_norm[oi], rel_bias)
                if sample:
                    h, bufs = _odd_mixer_sample(*args, [s[oi] for s in dil_states], batch=batch, tq=seq, tm=tm)
                else:
                    h, bufs = _odd_mixer_prompt(*args, batch=batch, seq=seq, tm=tm)
                odd.append(bufs)
            h = _ffn(h, li, g_ffn2, w2i, w2o, ple=(p, g_ple, wpg, wpp), tm=tm)
        rows = lambda a: a.reshape(batch, -1, 2, NSA_KV, HEAD_DIM)
        drows = lambda a: a.reshape(batch, -1, 2, DIL_HEADS, HEAD_DIM)
        return (h.reshape(batch, seq, d),
                jnp.stack([s["hgrn"] for s in even]), jnp.stack([rows(s["cmp"]) for s in even]),
                jnp.stack([rows(s["slc"]) for s in even]), jnp.stack([rows(s["win"]) for s in even]),
                jnp.stack([drows(b[0]) for b in odd]), jnp.stack([drows(b[1]) for b in odd]),
                jnp.stack([drows(b[2]) for b in odd]))

    y_p, hg_p, cmp_p, slc_p, win_p, d0_p, d1_p, d2_p = run(x_prompt, p_prompt, False)
    y_s, hg_s, cmp_s, slc_s, win_s, d0_s, d1_s, d2_s = run(x_sample, p_sample, True)
    return (y_p, y_s, hg_p, hg_s, cmp_p, cmp_s, slc_p, slc_s, win_p, win_s, d0_p, d0_s, d1_p, d1_s, d2_p, d2_s)
```

```python
import functools
import math

import numpy as np
import jax
import jax.numpy as jnp
from jax import lax
from jax.experimental import pallas as pl
from jax.experimental.pallas import tpu as pltpu

F32 = jnp.float32
BF16 = jnp.bfloat16
HIGHEST = lax.Precision.HIGHEST

V7X_VMEM_BYTES = 64 * 1024 * 1024
VMEM_LIMIT = V7X_VMEM_BYTES * 7 // 8
LANES = 128

EPS = 1e-6
NEG = -1e30
NEG_TEST = -1e29

HEAD_DIM = 64
HG_HEADS = 8
HG_CHUNK = 128
NSA_HEADS = 12
NSA_KV = 3
NSA_GROUP = NSA_HEADS // NSA_KV
CMP_BLOCK = 32
CMP_STRIDE = 16
SEL_BLOCK = 64
SEL_TOPK = 16
NSA_WINDOW = 512
FORCE_SCORE = 1e4
PAGE = 128
DIL_PATTERN = ((128, 1), (512, 4), (2048, 16))
DIL_HEADS = 4
RP_BUCKETS = 32
RP_MAX_DIST = 2048


def _cparams(*sem):
    return pltpu.CompilerParams(dimension_semantics=sem, vmem_limit_bytes=VMEM_LIMIT)


def _const(shape):
    nd = len(shape)
    return pl.BlockSpec(shape, lambda *_: (0,) * nd, pipeline_mode=pl.Buffered(1))


def _dot(a, b):
    return jnp.dot(a, b, preferred_element_type=F32)


def _dot_nt(a, b):
    return lax.dot_general(a, b, (((1,), (1,)), ((), ())), preferred_element_type=F32)


def _dot_tn(a, b):
    return lax.dot_general(a, b, (((0,), (0,)), ((), ())), preferred_element_type=F32)


def _rms(x, gain):
    return x * lax.rsqrt(jnp.mean(x * x, axis=-1, keepdims=True) + EPS) * gain


def _seg_ms(x, bd):
    sq = x * x
    hi = sq.astype(BF16)
    lo = (sq - hi.astype(F32)).astype(BF16)
    bw = bd.shape[0]
    cols = []
    for c0 in range(0, x.shape[1], bw):
        wd = min(bw, x.shape[1] - c0)
        blk = bd[:wd, :wd]
        cols.append(_dot(hi[:, c0:c0 + wd], blk) + _dot(lo[:, c0:c0 + wd], blk))
    return (cols[0] if len(cols) == 1 else jnp.concatenate(cols, axis=1)) * (1.0 / HEAD_DIM)


def _seg_rms(x, bd, gain):
    return x * lax.rsqrt(_seg_ms(x, bd) + EPS) * gain


def _norm_k_rows(x, bd, gain, kmask):
    bw = bd.shape[0]
    r = lax.rsqrt(_seg_ms(x[:, :bw], bd) + EPS)
    r = jnp.concatenate([r, jnp.ones((x.shape[0], x.shape[1] - bw), F32)], axis=1)
    return x * jnp.where(kmask > 0.5, r, 1.0) * gain


V7X_MXU_WIDTH = 256


def _block_diag_ones(width):
    i = np.arange(min(width, V7X_MXU_WIDTH)) // HEAD_DIM
    return jnp.asarray((i[:, None] == i[None, :]).astype(np.float32), BF16)


def _rel_bucket(dist):
    exact = RP_BUCKETS // 2
    d = jnp.maximum(dist, 0)
    log_ratio = jnp.log(jnp.maximum(d, 1).astype(F32) / exact) / math.log(RP_MAX_DIST / exact)
    large = jnp.minimum(exact + (log_ratio * (RP_BUCKETS - exact)).astype(jnp.int32), RP_BUCKETS - 1)
    return jnp.where(d < exact, d, large)


def _bias_tile(rel_bias, dist, valid):
    onehot = (_rel_bucket(dist)[..., None] == jnp.arange(RP_BUCKETS)).astype(F32)
    t = jnp.einsum('...k,kh->h...', onehot, rel_bias.astype(F32), precision=HIGHEST)
    return jnp.where(valid[None], t, NEG)


def _toeplitz(ext, n, m):
    p = n + m - 1
    u = jnp.concatenate([jnp.flip(ext[..., :m], -1), jnp.flip(ext[..., m:], -1)], axis=-1)
    reps = (1,) * (ext.ndim - 1) + (n,)
    t = jnp.tile(u, reps)[..., :n * (p - 1)].reshape(ext.shape[:-1] + (n, p - 1))
    return t[..., :m]


def _softmax_piece(s, mask, m_prev, l_prev, acc_prev, v, v_t=False):
    m_new = jnp.maximum(m_prev, jnp.max(s, axis=-1, keepdims=True))
    alpha = jnp.exp(m_prev - m_new)
    p = jnp.where(mask, jnp.exp(s - m_new), 0.0)
    l_new = alpha * l_prev + jnp.sum(p, axis=-1, keepdims=True)
    pb = p.astype(BF16)
    acc_new = alpha * acc_prev + (_dot_nt(pb, v) if v_t else _dot(pb, v))
    return m_new, l_new, acc_new


def _ffn_body(x_ref, g_ref, win_ref, wout_ref, *rest, dff, ple):
    x = x_ref[...]
    xn = _rms(x, g_ref[...]).astype(BF16)
    gu = _dot(xn, win_ref[...])
    gg = gu[:, :dff]
    a = (gg * jax.nn.sigmoid(gg) * gu[:, dff:]).astype(BF16)
    h = x + 0.5 * _dot(a, wout_ref[...])
    if ple:
        p_ref, gp_ref, wg_ref, wp_ref, o_ref = rest
        hn = _rms(h, gp_ref[...]).astype(BF16)
        gate = jax.nn.sigmoid(_dot(hn, wg_ref[...]))
        h = h + gate * _dot(p_ref[...].astype(BF16), wp_ref[...])
    else:
        (o_ref,) = rest
    o_ref[...] = h


def _layer(arr, li):
    tail = (0,) * (arr.ndim - 1)
    return pl.BlockSpec((None,) + arr.shape[1:], lambda *_: (li,) + tail, pipeline_mode=pl.Buffered(1))


def _ffn(h, li, gain, w_in, w_out, ple=None, *, tm):
    n, d = h.shape
    dff = w_out.shape[1]
    row = lambda i: (i, 0)
    in_specs = [pl.BlockSpec((tm, d), row), _layer(gain, li), _layer(w_in, li), _layer(w_out, li)]
    args = [h, gain, w_in, w_out]
    if ple is not None:
        p, gp, wg, wp = ple
        in_specs += [pl.BlockSpec((None, tm, p.shape[2]), lambda i: (li, i, 0)), _layer(gp, li), _layer(wg, li),
                     _layer(wp, li)]
        args += [p, gp, wg, wp]
    return pl.pallas_call(
        functools.partial(_ffn_body, dff=dff, ple=ple is not None),
        grid=(n // tm,), in_specs=in_specs, out_specs=pl.BlockSpec((tm, d), row),
        out_shape=jax.ShapeDtypeStruct((n, d), F32), compiler_params=_cparams("parallel"),
        name="ffn_ple" if ple is not None else "ffn",
    )(*args)


EVEN_HG = 4 * HG_HEADS * HEAD_DIM
EVEN_Q = NSA_HEADS * HEAD_DIM
ROWS_NSA = 2 * NSA_KV * HEAD_DIM
EVEN_PAD = EVEN_HG + EVEN_Q + 3 * ROWS_NSA + LANES


GATE_ROWS = 16


def _inproj_even_body(x_ref, g_ref, w_ref, gk_ref, km_ref, bd_ref, hg_ref, q_ref, cmp_ref, *outs, prompt):
    xn = _rms(x_ref[...], g_ref[...]).astype(BF16)
    p = _dot(xn, w_ref[...])
    c0 = EVEN_HG
    c1 = c0 + EVEN_Q
    hg_ref[...] = p[:, :c0]
    q_ref[...] = p[:, c0:c1]
    cmp = p[:, c1:c1 + ROWS_NSA]
    cmp_ref[...] = cmp
    bd = bd_ref[...]
    km = km_ref[...]
    slc = _norm_k_rows(p[:, c1 + ROWS_NSA:c1 + 2 * ROWS_NSA], bd, gk_ref[0:1, :], km)
    win = _norm_k_rows(p[:, c1 + 2 * ROWS_NSA:c1 + 3 * ROWS_NSA], bd, gk_ref[1:2, :], km)
    sig = jax.nn.sigmoid(p[:, c1 + 3 * ROWS_NSA:])
    if not prompt:
        slc_ref, win_ref, gate_ref = outs
        slc_ref[...] = slc
        win_ref[...] = win
        for g in range(NSA_KV):
            gate_ref[g] = sig if g == 0 else pltpu.roll(sig, LANES - 3 * NSA_GROUP * g, 1)
        return
    slcg_ref, wing_ref, gate_t_ref, cmp_t_ref, slc_t_ref, win_t_ref = outs
    kw = NSA_KV * HEAD_DIM
    sig_t = sig.T
    for g in range(NSA_KV):
        ks = slice(HEAD_DIM * g, HEAD_DIM * (g + 1))
        vs = slice(kw + HEAD_DIM * g, kw + HEAD_DIM * (g + 1))
        slcg_ref[g] = jnp.concatenate([slc[:, ks], slc[:, vs]], axis=1)
        wing_ref[g] = jnp.concatenate([win[:, ks], win[:, vs]], axis=1)
        for half in range(sig.shape[0] // NSA_TQ):
            gate_t_ref[g, half] = sig_t[3 * NSA_GROUP * g:3 * NSA_GROUP * g + GATE_ROWS,
                                        half * NSA_TQ:(half + 1) * NSA_TQ]
    for dst_ref, rows in ((cmp_t_ref, cmp), (slc_t_ref, slc), (win_t_ref, win)):
        _store_transposed(dst_ref, rows)


def _store_transposed(dst_ref, x):
    for c in range(x.shape[1] // LANES):
        dst_ref[0, c * LANES:(c + 1) * LANES, :] = x[:, c * LANES:(c + 1) * LANES].T


def _inproj_even(h, gain, w_pad, qk_norm, *, tm, seq=None):
    n, d = h.shape
    kw = NSA_KV * HEAD_DIM
    ones = jnp.ones((kw,), F32)
    gk = jnp.stack([jnp.concatenate([jnp.tile(qk_norm[2], NSA_KV), ones]),
                    jnp.concatenate([jnp.tile(qk_norm[3], NSA_KV), ones])])
    km = jnp.concatenate([ones, 0.0 * ones]).reshape(1, ROWS_NSA)
    row = lambda i: (i, 0)
    grp = lambda i: (0, i, 0)
    shp = lambda *s: jax.ShapeDtypeStruct(s, F32)
    out_specs = [pl.BlockSpec((tm, EVEN_HG), row), pl.BlockSpec((tm, EVEN_Q), row), pl.BlockSpec((tm, ROWS_NSA), row)]
    out_shape = [shp(n, EVEN_HG), shp(n, EVEN_Q), shp(n, ROWS_NSA)]
    if seq is None:
        out_specs += [pl.BlockSpec((tm, ROWS_NSA), row)] * 2 + [pl.BlockSpec((NSA_KV, tm, LANES), grp)]
        out_shape += [shp(n, ROWS_NSA)] * 2 + [shp(NSA_KV, n, LANES)]
    else:
        tpb = seq // tm
        out_specs += [pl.BlockSpec((NSA_KV, tm, LANES), grp)] * 2
        out_shape += [shp(NSA_KV, n, LANES)] * 2
        out_specs.append(pl.BlockSpec((NSA_KV, tm // NSA_TQ, GATE_ROWS, NSA_TQ), lambda i: (0, i, 0, 0)))
        out_shape.append(shp(NSA_KV, n // NSA_TQ, GATE_ROWS, NSA_TQ))
        out_specs += [pl.BlockSpec((1, ROWS_NSA, tm), lambda i: (i // tpb, 0, i % tpb))] * 3
        out_shape += [shp(n // seq, ROWS_NSA, seq)] * 3
    return pl.pallas_call(
        functools.partial(_inproj_even_body, prompt=seq is not None), grid=(n // tm,),
        in_specs=[pl.BlockSpec((tm, d), row), _const((1, d)), _const(w_pad.shape), _const((2, ROWS_NSA)),
                  _const((1, ROWS_NSA)), _const(_block_diag_ones(ROWS_NSA).shape)],
        out_specs=out_specs, out_shape=out_shape,
        compiler_params=_cparams("parallel"), name="inproj_even",
    )(h, gain.reshape(1, d), w_pad, gk, km, _block_diag_ones(ROWS_NSA))


ODD_W = 3 * DIL_HEADS * HEAD_DIM
ODD_GW = DIL_HEADS * HEAD_DIM


def _to_residue_view(src, dst_ref, stage_ref, dil):
    rows, w = src.shape
    for blk in range(w // LANES):
        stage_ref[...] = src[:, blk * LANES:(blk + 1) * LANES]
        for r in range(dil):
            lo = r * w + blk * LANES
            dst_ref[:, lo:lo + LANES] = stage_ref[pl.ds(r, rows // dil, stride=dil), :]


def _from_residue_view(src_ref, stage_ref, dil, w):
    rows = src_ref.shape[0] * dil
    parts = []
    for blk in range(w // LANES):
        for r in range(dil):
            lo = r * w + blk * LANES
            stage_ref[pl.ds(r, rows // dil, stride=dil), :] = src_ref[:, lo:lo + LANES]
        parts.append(stage_ref[...])
    return jnp.concatenate(parts, axis=1)


def _inproj_odd_body(x_ref, g_ref, w_ref, gq_ref, gk_ref, bd_ref, *outs, prompt):
    xn = _rms(x_ref[...], g_ref[...]).astype(BF16)
    p = _dot(xn, w_ref[...])
    bd = bd_ref[...]
    q = _seg_rms(p[:, :ODD_W], bd, gq_ref[...]) * (HEAD_DIM ** -0.5)
    k = _seg_rms(p[:, ODD_W:2 * ODD_W], bd, gk_ref[...])
    v = p[:, 2 * ODD_W:]
    for g in range(3):
        s = slice(ODD_GW * g, ODD_GW * (g + 1))
        kv = jnp.concatenate([k[:, s], v[:, s]], axis=1)
        if not prompt:
            outs[0][g] = q[:, s]
            outs[1][g] = kv
            continue
        dil = DIL_PATTERN[g][1]
        if dil == 1:
            outs[2 * g][...] = q[:, s]
            outs[2 * g + 1][...] = kv
        else:
            _to_residue_view(q[:, s], outs[2 * g], outs[-1], dil)
            _to_residue_view(kv, outs[2 * g + 1], outs[-1], dil)
        _store_transposed(outs[6 + g], kv)


def _inproj_odd(h, gain, w, qk_norm, *, tm, seq=None):
    n, d = h.shape
    row = lambda i: (i, 0)
    nh = 3 * DIL_HEADS
    prompt = seq is not None
    if not prompt:
        out_specs = [pl.BlockSpec((3, tm, ODD_GW), lambda i: (0, i, 0)),
                     pl.BlockSpec((3, tm, 2 * ODD_GW), lambda i: (0, i, 0))]
        out_shape = [jax.ShapeDtypeStruct((3, n, ODD_GW), F32), jax.ShapeDtypeStruct((3, n, 2 * ODD_GW), F32)]
    else:
        out_specs, out_shape = [], []
        for _, dil in DIL_PATTERN:
            assert dil == 1 or DIL_PATTERN[0][1] == 1
            for width in (ODD_GW, 2 * ODD_GW):
                out_specs.append(pl.BlockSpec((tm // dil, dil * width), row))
                out_shape.append(jax.ShapeDtypeStruct((n // dil, dil * width), F32))
        tpb = seq // tm
        for _ in DIL_PATTERN:
            out_specs.append(pl.BlockSpec((1, 2 * ODD_GW, tm), lambda i: (i // tpb, 0, i % tpb)))
            out_shape.append(jax.ShapeDtypeStruct((n // seq, 2 * ODD_GW, seq), F32))
    return pl.pallas_call(
        functools.partial(_inproj_odd_body, prompt=prompt), grid=(n // tm,),
        in_specs=[pl.BlockSpec((tm, d), row), _const((1, d)), _const(w.shape), _const((1, ODD_W)),
                  _const((1, ODD_W)), _const(_block_diag_ones(ODD_W).shape)],
        out_specs=out_specs, out_shape=out_shape,
        scratch_shapes=[pltpu.VMEM((tm, LANES), F32)] if prompt else [],
        compiler_params=_cparams("parallel"), name="inproj_odd",
    )(h, gain.reshape(1, d), w, jnp.tile(qk_norm[0], nh).reshape(1, ODD_W),
      jnp.tile(qk_norm[1], nh).reshape(1, ODD_W), _block_diag_ones(ODD_W))


def _outproj2_body(h_ref, a_ref, b_ref, w_ref, o_ref):
    ka = a_ref.shape[1]
    o_ref[...] = (h_ref[...] + _dot(a_ref[...].astype(BF16), w_ref[:ka])
                  + _dot(b_ref[...].astype(BF16), w_ref[ka:]))


def _outproj2(h, a, b, w, *, tm):
    n, d = h.shape
    row = lambda i: (i, 0)
    return pl.pallas_call(
        _outproj2_body, grid=(n // tm,),
        in_specs=[pl.BlockSpec((tm, d), row), pl.BlockSpec((tm, a.shape[1]), row),
                  pl.BlockSpec((tm, b.shape[1]), row), _const(w.shape)],
        out_specs=pl.BlockSpec((tm, d), row), out_shape=jax.ShapeDtypeStruct((n, d), F32),
        compiler_params=_cparams("parallel"), name="outproj_even",
    )(h, a, b, w)


def _outproj2t_body(h_ref, a_ref, bt_ref, w_ref, o_ref):
    ka = a_ref.shape[1]
    acc = h_ref[...] + _dot(a_ref[...].astype(BF16), w_ref[:ka])
    tq = bt_ref.shape[2]
    for i in range(bt_ref.shape[0]):
        rs = slice(i * tq, (i + 1) * tq)
        o_ref[rs, :] = acc[rs, :] + _dot_tn(bt_ref[i].astype(BF16), w_ref[ka:])


def _outproj2t(h, a, bt, w, *, tm):
    n, d = h.shape
    row = lambda i: (i, 0)
    tq = bt.shape[2]
    return pl.pallas_call(
        _outproj2t_body, grid=(n // tm,),
        in_specs=[pl.BlockSpec((tm, d), row), pl.BlockSpec((tm, a.shape[1]), row),
                  pl.BlockSpec((tm // tq, bt.shape[1], tq), lambda i: (i, 0, 0)), _const(w.shape)],
        out_specs=pl.BlockSpec((tm, d), row), out_shape=jax.ShapeDtypeStruct((n, d), F32),
        compiler_params=_cparams("parallel"), name="outproj_even_t",
    )(h, a, bt, w)


def _outproj_merge_body(h_ref, o0_ref, o1_ref, o2_ref, l0_ref, l1_ref, l2_ref, w_ref, o_ref, *stage, views):
    if views:
        load = lambda ref, g: (ref[...] if DIL_PATTERN[g][1] == 1
                               else _from_residue_view(ref, stage[0], DIL_PATTERN[g][1], ODD_GW))
    else:
        load = lambda ref, g: ref[...]
    mixed = _merge_groups([load(r, g) for g, r in enumerate((o0_ref, o1_ref, o2_ref))],
                          [load(r, g) for g, r in enumerate((l0_ref, l1_ref, l2_ref))])
    o_ref[...] = h_ref[...] + _dot(mixed.astype(BF16), w_ref[...])


def _outproj_merge(h, outs, lses, w, *, tm, views=False):
    n, d = h.shape
    row = lambda i: (i, 0)
    parts = [pl.BlockSpec((tm * a.shape[0] // n, a.shape[1]), row) for a in list(outs) + list(lses)]
    return pl.pallas_call(
        functools.partial(_outproj_merge_body, views=views), grid=(n // tm,),
        in_specs=[pl.BlockSpec((tm, d), row)] + parts + [_const(w.shape)],
        out_specs=pl.BlockSpec((tm, d), row), out_shape=jax.ShapeDtypeStruct((n, d), F32),
        scratch_shapes=[pltpu.VMEM((tm, LANES), F32)] if views else [],
        compiler_params=_cparams("parallel"), name="outproj_odd",
    )(h, *outs, *lses, w)


HG_MM_LEVELS = 2


def _hgrn_prefix_matrix(C):
    t = np.arange(C)[:, None]
    j = np.arange(C)[None, :]
    mats = [j <= t]
    for lvl in range(1, HG_MM_LEVELS + 1):
        if (1 << lvl) <= C:
            mats.append(j <= ((t >> lvl) << lvl) + (1 << lvl) // 2 - 1)
    return jnp.asarray(np.concatenate(mats, axis=0).astype(np.float32), BF16)


def _hgrn_body(hq_ref, hf_ref, hi_ref, hgt_ref, lb_ref, gn_ref, bd_ref, pm_ref, s0_ref, o_ref, sfin_ref, st_ref,
               *b_scr, C, U):
    c = pl.program_id(1)

    @pl.when(c == 0)
    def _():
        st_ref[...] = s0_ref[0]

    nb = len(b_scr) // U
    for u in range(U):
        _hgrn_chunk(hq_ref, hf_ref, hi_ref, hgt_ref, lb_ref, gn_ref, bd_ref, pm_ref, o_ref, st_ref,
                    b_scr[u * nb:(u + 1) * nb], slice(u * C, (u + 1) * C), C)

    @pl.when(c == pl.num_programs(1) - 1)
    def _():
        sfin_ref[0] = st_ref[...]


def _hgrn_chunk(hq_ref, hf_ref, hi_ref, hgt_ref, lb_ref, gn_ref, bd_ref, pm_ref, o_ref, st_ref, b_scr, rs, C):
    dk = HEAD_DIM
    w = HG_HEADS * dk
    lb = lb_ref[...]
    f = lb + (1.0 - lb) * jax.nn.sigmoid(hf_ref[rs, :])
    lf = jnp.log(f)
    kk = 1.0 - f
    ti = lax.broadcasted_iota(jnp.int32, (C, C), 0)
    si = lax.broadcasted_iota(jnp.int32, (C, C), 1)
    p1 = lf.astype(BF16)
    r1 = lf - p1.astype(F32)
    p2 = r1.astype(BF16)
    p3 = (r1 - p2.astype(F32)).astype(BF16)
    sums = _dot(pm_ref[...], jnp.concatenate([p1, p2, p3], axis=1))
    sums = sums[:, :w] + sums[:, w:2 * w] + sums[:, 2 * w:]
    b = sums[0:C]
    for c4, scr in enumerate(b_scr):
        scr[...] = b[:, c4 * LANES:(c4 + 1) * LANES]
    q = hq_ref[rs, :]
    v = hi_ref[rs, :].astype(BF16)
    heads = range(HG_HEADS)
    hs = [slice(dk * h, dk * (h + 1)) for h in heads]

    qb = q.astype(BF16)
    kb = kk.astype(BF16)
    a = [jnp.where(ti == si, _dot_nt(qb[:, hs[h]], kb[:, hs[h]]), 0.0) for h in heads]
    lvl = 1
    while (1 << lvl) <= C:
        m = 1 << lvl
        half = m // 2
        if lvl <= HG_MM_LEVELS:
            r = sums[lvl * C:(lvl + 1) * C]
        else:
            ref_rows = pl.ds(half - 1, C // m, stride=m)
            r = jnp.concatenate([jnp.broadcast_to(scr[ref_rows, :][:, None, :], (C // m, m, LANES)).reshape(C, LANES)
                                 for scr in b_scr], axis=1)
        e = jnp.exp(-jnp.abs(b - r))
        qt = (q * e).astype(BF16)
        kt = (kk * e).astype(BF16)
        pair = ((ti >> lvl) == (si >> lvl)) & ((ti & (m - 1)) >= half) & ((si & (m - 1)) < half)
        a = [a[h] + jnp.where(pair, _dot_nt(qt[:, hs[h]], kt[:, hs[h]]), 0.0) for h in heads]
        lvl += 1

    b_last = b[C - 1:C, :]
    qe = (q * jnp.exp(b)).astype(BF16)
    kdec = (kk * jnp.exp(b_last - b)).astype(BF16)
    e_last = jnp.exp(b_last)
    outs = []
    for h in heads:
        st = st_ref[h]
        outs.append(_dot(a[h].astype(BF16), v[:, hs[h]]) + _dot_nt(qe[:, hs[h]], st.astype(BF16)))
        st_ref[h] = st * e_last[:, hs[h]] + _dot_tn(v[:, hs[h]], kdec[:, hs[h]])
    o = jnp.concatenate(outs, axis=1)
    gt = hgt_ref[rs, :]
    o_ref[rs, :] = _seg_rms(o, bd_ref[...], gn_ref[...]) * (gt * jax.nn.sigmoid(gt))


HG_CHUNKS_PER_STEP = 2


def _hgrn(hg, lb, hg_norm, s0, *, batch, seq):
    C = math.gcd(seq, HG_CHUNK)
    U = math.gcd(seq // C, HG_CHUNKS_PER_STEP)
    nc = seq // (C * U)
    w = HG_HEADS * HEAD_DIM
    col = lambda j: pl.BlockSpec((U * C, w), lambda b, c: (b * nc + c, j))
    st_spec = pl.BlockSpec((1, HG_HEADS, HEAD_DIM, HEAD_DIM), lambda b, c: (b, 0, 0, 0))
    pm = _hgrn_prefix_matrix(C)
    o, sfin = pl.pallas_call(
        functools.partial(_hgrn_body, C=C, U=U), grid=(batch, nc),
        in_specs=[col(0), col(1), col(2), col(3), _const((1, w)), _const((1, w)), _const(_block_diag_ones(w).shape), _const(pm.shape),
                  st_spec],
        out_specs=[pl.BlockSpec((U * C, w), lambda b, c: (b * nc + c, 0)), st_spec],
        out_shape=[jax.ShapeDtypeStruct((batch * seq, w), F32),
                   jax.ShapeDtypeStruct((batch, HG_HEADS, HEAD_DIM, HEAD_DIM), F32)],
        scratch_shapes=[pltpu.VMEM((HG_HEADS, HEAD_DIM, HEAD_DIM), F32)]
        + [pltpu.VMEM((C, LANES), F32)] * (U * w // LANES),
        compiler_params=_cparams("parallel", "arbitrary"), name="hgrn_scan",
    )(hg, hg, hg, hg, lb.reshape(1, w), jnp.tile(hg_norm, HG_HEADS).reshape(1, w), _block_diag_ones(w), pm,
      jnp.swapaxes(s0, 2, 3))
    return o, jnp.swapaxes(sfin, 2, 3)


CHUNKS_PER_PAGE = PAGE // CMP_STRIDE
PAGE_LANES = CMP_STRIDE * ROWS_NSA
CMP_HID = 2 * NSA_KV * 2 * HEAD_DIM
CMP_PP_MAX = 64


def _compress_body(pt_ref, *refs, n_chunks, transposed, CMP_PP):
    pages = refs[:CMP_PP]
    wbig_ref, pe_ref, w2_ref, gk_ref, km_ref, bd_ref, perm_ref, rows_ref, grp_ref, xs_ref = refs[CMP_PP:]
    nb = ROWS_NSA // LANES
    j = pl.program_id(1)
    perm = perm_ref[...]
    for i in range(0, CMP_PP, 2):
        pair = []
        for k in range(2):
            pg = pages[i + k][0]
            if transposed:
                pg = jnp.concatenate([pg[c * LANES:(c + 1) * LANES, :].T for c in range(nb)], axis=1)
            pair.append(_dot(perm, pg.astype(BF16)))
        r0 = pl.multiple_of((j * CMP_PP + i) * CHUNKS_PER_PAGE, 2 * CHUNKS_PER_PAGE)
        for s in range(CMP_STRIDE):
            rs = slice(s * CHUNKS_PER_PAGE, (s + 1) * CHUNKS_PER_PAGE)
            both = jnp.concatenate([pair[0][rs], pair[1][rs]], axis=0).astype(BF16)
            for c in range(nb):
                lo = (c * CMP_STRIDE + s) * LANES
                xs_ref[pl.ds(r0, 2 * CHUNKS_PER_PAGE), lo:lo + LANES] = both[:, c * LANES:(c + 1) * LANES]

    @pl.when(j == pl.num_programs(1) - 1)
    def _():
        blk_w = CMP_STRIDE * LANES
        first = lambda x: jnp.concatenate(
            [_dot(x[:, c * blk_w:(c + 1) * blk_w], wbig_ref[c]) for c in range(nb)], axis=1)
        y = first(xs_ref[...])
        ysh = pltpu.roll(y, n_chunks - 1, 0)
        hid = y + pltpu.roll(ysh, CMP_HID - HEAD_DIM, 1)
        cf = first(pe_ref[...].astype(BF16))
        lane = lax.broadcasted_iota(jnp.int32, (8, CMP_HID), 1)
        cs = jnp.where(((lane >> 6) & 1) == 0, jnp.broadcast_to(cf[0:1], (8, CMP_HID)),
                       jnp.broadcast_to(cf[1:2], (8, CMP_HID)))
        cs = cs + pltpu.roll(cs, CMP_HID - HEAD_DIM, 1)
        hid = hid + cs[0:1]
        act = (hid * jax.nn.sigmoid(hid)).astype(BF16)
        out = _dot(act, w2_ref[...])
        out = _norm_k_rows(out, bd_ref[...], gk_ref[...], km_ref[...])
        rows_ref[0] = out
        kw = NSA_KV * HEAD_DIM
        for g in range(NSA_KV):
            grp_ref[0, g] = jnp.concatenate([out[:, HEAD_DIM * g:HEAD_DIM * (g + 1)],
                                             out[:, kw + HEAD_DIM * g:kw + HEAD_DIM * (g + 1)]], axis=1)


def _compress_weights(cmp_pe, cmp_w1, cmp_w2, k_gain):
    ratio = CMP_BLOCK // CMP_STRIDE
    w1 = cmp_w1.reshape(2, ratio, CMP_STRIDE, HEAD_DIM, HEAD_DIM)
    eye_kv = jnp.eye(2, dtype=F32)
    eye_g = jnp.eye(NSA_KV, dtype=F32)
    nb = ROWS_NSA // LANES
    member_kv = np.arange(2 * NSA_KV) // NSA_KV
    w1m = w1[member_kv].reshape(nb, 2, ratio, CMP_STRIDE, HEAD_DIM, HEAD_DIM)
    wbig = jnp.einsum('chrsde,hH->cshdHre', w1m, eye_kv).reshape(nb, CMP_STRIDE * LANES, 2 * ratio * HEAD_DIM)
    wbig = wbig.astype(BF16)
    pe = cmp_pe.reshape(2, ratio, CMP_STRIDE, HEAD_DIM)
    pe_rows = pe[member_kv].reshape(nb, 2, ratio, CMP_STRIDE, HEAD_DIM).transpose(2, 0, 3, 1, 4)
    pe_rows = pe_rows.reshape(ratio, PAGE_LANES)
    pe_rows = jnp.concatenate([pe_rows, jnp.zeros((8 - ratio, PAGE_LANES), F32)], axis=0)
    sel_r0 = jnp.asarray([1.0, 0.0], F32)
    w2big = jnp.einsum('ked,kK,gG,r->kgreKGd', cmp_w2, eye_kv, eye_g, sel_r0).reshape(CMP_HID, ROWS_NSA).astype(BF16)
    kw = NSA_KV * HEAD_DIM
    gk = jnp.concatenate([jnp.tile(k_gain, NSA_KV), jnp.ones((kw,), F32)]).reshape(1, ROWS_NSA)
    km = jnp.concatenate([jnp.ones((kw,), F32), jnp.zeros((kw,), F32)]).reshape(1, ROWS_NSA)
    return wbig, pe_rows, w2big, gk, km


def _compress(pool, table, cw, *, transposed):
    wbig, pe_rows, w2big, gk, km = cw
    bsz, npg = table.shape
    n_chunks = npg * CHUNKS_PER_PAGE
    CMP_PP = math.gcd(npg, CMP_PP_MAX)
    assert CMP_PP % 2 == 0
    page = lambda i: pl.BlockSpec((1,) + pool.shape[1:], lambda b, j, pt: (pt[b, j * CMP_PP + i], 0, 0))
    tok = np.arange(PAGE)
    perm = (tok[None, :] == (tok[:, None] % CHUNKS_PER_PAGE) * CMP_STRIDE + tok[:, None] // CHUNKS_PER_PAGE)
    perm = jnp.asarray(perm.astype(np.float32), BF16)
    return pl.pallas_call(
        functools.partial(_compress_body, n_chunks=n_chunks, transposed=transposed, CMP_PP=CMP_PP),
        grid_spec=pltpu.PrefetchScalarGridSpec(
            num_scalar_prefetch=1, grid=(bsz, npg // CMP_PP),
            in_specs=[page(i) for i in range(CMP_PP)] + [
                _const(wbig.shape), _const(pe_rows.shape), _const(w2big.shape), _const((1, ROWS_NSA)),
                _const((1, ROWS_NSA)), _const(_block_diag_ones(ROWS_NSA).shape), _const(perm.shape)],
            out_specs=[pl.BlockSpec((1, n_chunks, ROWS_NSA), lambda b, j, pt: (b, 0, 0)),
                       pl.BlockSpec((1, NSA_KV, n_chunks, LANES), lambda b, j, pt: (b, 0, 0, 0))],
            scratch_shapes=[pltpu.VMEM((n_chunks, PAGE_LANES), BF16)]),
        out_shape=[jax.ShapeDtypeStruct((bsz, n_chunks, ROWS_NSA), F32),
                   jax.ShapeDtypeStruct((bsz, NSA_KV, n_chunks, LANES), F32)],
        compiler_params=_cparams("parallel", "arbitrary"), name="nsa_compress",
    )(table, *([pool] * CMP_PP), wbig, pe_rows, w2big, gk, km, _block_diag_ones(ROWS_NSA), perm)


def _select_blocks(imp, pos, n_slc, k_sel):
    lane = lax.broadcasted_iota(jnp.int32, imp.shape, 1)
    cur = pos >> 6
    forced = (lane == 0) | (lane == cur) | (lane == cur - 1)
    score = jnp.where(forced, FORCE_SCORE, jnp.where(lane <= cur, imp, -1.0))
    rank = jnp.zeros(imp.shape, F32)
    for i in range(n_slc):
        ci = score[:, i:i + 1]
        rank = rank + jnp.where(ci > score, 1.0, jnp.where(ci == score, jnp.where(lane > i, 1.0, 0.0), 0.0))
    return jnp.where(lane < n_slc, jnp.where(rank < k_sel, 1.0, 0.0), 0.0)


NSA_TQ = 256
NSA_QC = LANES


def _select_blocks_t(imp, pos, n_slc, k_sel):
    blk = lax.broadcasted_iota(jnp.int32, imp.shape, 0)
    cur = pos >> 6
    forced = (blk == 0) | (blk == cur) | (blk == cur - 1)
    score = jnp.where(forced, FORCE_SCORE, jnp.where(blk <= cur, imp, -1.0))
    rank = jnp.zeros(imp.shape, F32)
    for i in range(n_slc):
        ci = score[i:i + 1, :]
        rank = rank + jnp.where(ci > score, 1.0, jnp.where(ci == score, jnp.where(blk > i, 1.0, 0.0), 0.0))
    return jnp.where(rank < k_sel, 1.0, 0.0)


def _nsa_prompt_t_body(q_ref, gate_ref, kcv_ref, slc_ref, win_ref, gq_ref, bd_ref, eye_ref, fs_ref, fc_ref,
                       ov_ref, ex_ref, o_ref, bc_ref, bs_ref, bw_ref, qt_scr, ks_scr, kw_scr, vs_scr, vw_scr, sel_scr, m_scr, l_scr, acc_scr,
                       *, seq, n_slc, k_sel):
    tq = NSA_TQ
    qc = NSA_QC
    hd = HEAD_DIM
    n_tiles = seq // tq
    width = NSA_GROUP * tq
    n_chunks = width // qc
    eye = eye_ref[...]
    n_c = kcv_ref.shape[2]
    far = NSA_WINDOW // tq

    @pl.when(pl.program_id(1) == 0)
    def _():
        key_i = lax.broadcasted_iota(jnp.int32, (tq, tq), 0)
        qry_i = lax.broadcasted_iota(jnp.int32, (tq, tq), 1)
        blk_i = lax.broadcasted_iota(jnp.int32, (n_c, tq), 0)
        n_cmp = seq // CMP_STRIDE - CMP_BLOCK // CMP_STRIDE + 1
        span = tq + CMP_STRIDE * n_c
        for h in range(NSA_GROUP):
            hs = slice(h * tq, (h + 1) * tq)
            for d in range(n_tiles):
                u = jnp.broadcast_to(fs_ref[0, h:h + 1, d * tq:(d + 2) * tq], (tq, 2 * tq))
                tile = pltpu.roll(u, tq, 1, stride=1, stride_axis=0)[:, :tq]
                bs_ref[d, :, hs] = tile
                if d == far:
                    bw_ref[0, :, hs] = jnp.where(qry_i <= key_i, tile, NEG)
            for t in range(n_tiles):
                u = jnp.broadcast_to(fc_ref[0, h:h + 1, t * tq:t * tq + span], (n_c, span))
                tile = pltpu.roll(u, tq, 1, stride=CMP_STRIDE, stride_axis=0)[:, :tq]
                bc_ref[t, :, hs] = jnp.where(blk_i < n_cmp, tile, NEG)

    qn = (_seg_rms(q_ref[...], bd_ref[...], gq_ref[...]) * (hd ** -0.5)).astype(BF16)
    for t in range(n_tiles):
        rs = slice(t * tq, (t + 1) * tq)
        for h in range(NSA_GROUP):
            qt_scr[t, :, h * tq:(h + 1) * tq] = _dot_nt(eye, qn[rs, h * hd:(h + 1) * hd]).astype(BF16)
        for kv_ref, k_scr, v_scr in ((slc_ref, ks_scr, vs_scr), (win_ref, kw_scr, vw_scr)):
            kv = kv_ref[0, rs, :].astype(BF16)
            k_scr[t] = kv[:, :hd]
            v_scr[t] = _dot_nt(eye, kv[:, hd:]).astype(BF16)
    kc = kcv_ref[0, 0, :, :hd].astype(BF16)
    vct = _dot_nt(eye, kcv_ref[0, 0, :, hd:].astype(BF16)).astype(BF16)
    def flash_init():
        m_scr[...] = jnp.full((1, width), NEG, F32)
        l_scr[...] = jnp.zeros((1, width), F32)
        acc_scr[...] = jnp.zeros((hd, width), F32)

    def flash_update(k_scr, v_scr, t, tiles, running):
        s_alls = [_dot(k_scr[jt], qt_scr[t]) for jt, _, _ in tiles]
        if running:
            m_all = m_scr[...]
            l_all = l_scr[...]
        alphas, ls, ms, ps = [], [], [], [[] for _ in tiles]
        for c in range(n_chunks):
            cs = slice(c * qc, (c + 1) * qc)
            qoff = (c * qc) % tq
            ss = []
            for (_, bias_tile, sel_add), s_all in zip(tiles, s_alls):
                s = s_all[:, cs] + bias_tile(cs)
                ss.append(s if sel_add is None else s + sel_add[:, qoff:qoff + qc])
            m_new = functools.reduce(jnp.maximum, [jnp.max(s, axis=0, keepdims=True) for s in ss])
            if running:
                m_new = jnp.maximum(m_all[:, cs], m_new)
                alpha = jnp.exp(m_all[:, cs] - m_new)
                alphas.append(alpha)
            l_new = alpha * l_all[:, cs] if running else 0.0
            for k, s in enumerate(ss):
                p = jnp.exp(s - m_new)
                l_new = l_new + jnp.sum(p, axis=0, keepdims=True)
                ps[k].append(p.astype(BF16))
            ls.append(l_new)
            ms.append(m_new)
        cat = lambda xs: jnp.concatenate(xs, axis=1)
        pv = sum(_dot(v_scr[jt], cat(ps[k])) for k, (jt, _, _) in enumerate(tiles))
        acc_scr[...] = cat(alphas) * acc_scr[...] + pv if running else pv
        m_scr[...] = cat(ms)
        l_scr[...] = cat(ls)

    def flash_result():
        return acc_scr[...] / jnp.maximum(l_scr[...], 1e-30)

    def q_tile(t, carry):
        bias_c = bc_ref[t]
        s_c = _dot(kc, qt_scr[t]) + bias_c
        m_c = jnp.max(s_c, axis=0, keepdims=True)
        e_c = jnp.where(bias_c > NEG_TEST, jnp.exp(s_c - m_c), 0.0)
        p_c = (e_c / jnp.maximum(jnp.sum(e_c, axis=0, keepdims=True), 1e-30)).astype(BF16)
        o_c = _dot(vct, p_c)
        imp = sum(_dot(ov_ref[...], p_c[:, h * tq:(h + 1) * tq]) for h in range(NSA_GROUP))
        pos = t * tq + lax.broadcasted_iota(jnp.int32, (1, tq), 1)
        sel = _select_blocks_t(imp[:n_slc], pos, n_slc, k_sel)
        sel_scr[...] = jnp.zeros(sel_scr.shape, BF16)
        sel_scr[0:n_slc, :] = ((1.0 - sel) * NEG).astype(BF16)
        slc_tile = lambda jt: (jt, lambda cs: bs_ref[t - jt, :, cs], _dot(ex_ref[jt], sel_scr[...]))
        for j0 in range(0, t + 1, 2):
            flash_update(ks_scr, vs_scr, t, [slc_tile(j) for j in range(j0, min(j0 + 2, t + 1))], j0 > 0)
        o_s = flash_result()
        far = NSA_WINDOW // tq
        win_tile = lambda d: (t - d, (lambda cs: bw_ref[0, :, cs]) if d == far else (lambda cs: bs_ref[d, :, cs]),
                              None)
        flash_update(kw_scr, vw_scr, t, [win_tile(d) for d in range(min(t, far), -1, -1)], False)
        o_w = flash_result()
        gates = gate_ref[0, t]
        for h in range(NSA_GROUP):
            hs = slice(h * tq, (h + 1) * tq)
            o_ref[t, h * hd:(h + 1) * hd, :] = (gates[3 * h:3 * h + 1] * o_c[:, hs]
                                                + gates[3 * h + 1:3 * h + 2] * o_s[:, hs]
                                                + gates[3 * h + 2:3 * h + 3] * o_w[:, hs])
        return carry

    for t in range(n_tiles):
        q_tile(t, 0)


def _nsa_prompt_t_tables(rel_bias, seq, n_c):
    tq = NSA_TQ
    n_tiles = seq // tq
    n_slc = -(-seq // SEL_BLOCK)
    ds = jnp.arange((n_tiles + 1) * tq) - tq
    fs = _bias_tile(rel_bias, ds, ds >= 0).reshape(NSA_KV, NSA_GROUP, -1)
    dc = jnp.arange(n_tiles * tq + CMP_STRIDE * n_c) - (CMP_STRIDE * n_c + CMP_BLOCK - 1)
    fc = _bias_tile(rel_bias, dc, dc >= 0).reshape(NSA_KV, NSA_GROUP, -1)
    n_cmp = seq // CMP_STRIDE - CMP_BLOCK // CMP_STRIDE + 1
    c_lo = np.arange(n_c) * CMP_STRIDE
    s_lo = np.arange(LANES) * SEL_BLOCK
    ov = ((c_lo[None, :] <= s_lo[:, None] + SEL_BLOCK - 1) & (c_lo[None, :] + CMP_BLOCK - 1 >= s_lo[:, None])
          & (np.arange(n_c) < n_cmp)[None, :] & (np.arange(LANES) < n_slc)[:, None])
    ex = (np.arange(seq)[:, None] // SEL_BLOCK == np.arange(LANES)[None, :]).reshape(n_tiles, tq, LANES)
    as_bf = lambda a: jnp.asarray(a.astype(np.float32), BF16)
    return fs, fc, as_bf(ov), as_bf(ex), n_slc


def _nsa_prompt_t(q, gates_t, kcv, slcg, wing, q_gain, tables, *, batch, seq):
    fs, fc, ov, ex, n_slc = tables
    n_c = kcv.shape[2]
    tq = NSA_TQ
    n_tiles = seq // tq
    assert n_tiles > NSA_WINDOW // tq
    gw = NSA_GROUP * HEAD_DIM
    width = NSA_GROUP * tq
    per_g = lambda shape: pl.BlockSpec((1,) + shape, lambda g, b: (g,) + (0,) * len(shape),
                                       pipeline_mode=pl.Buffered(1))
    eye = jnp.eye(HEAD_DIM, dtype=BF16)
    return pl.pallas_call(
        functools.partial(_nsa_prompt_t_body, seq=seq, n_slc=n_slc, k_sel=min(SEL_TOPK, n_slc)),
        grid=(NSA_KV, batch),
        in_specs=[pl.BlockSpec((seq, gw), lambda g, b: (b, g)),
                  pl.BlockSpec((1, n_tiles, 16, tq), lambda g, b: (g, b, 0, 0)),
                  pl.BlockSpec((1, 1, n_c, LANES), lambda g, b: (b, g, 0, 0)),
                  pl.BlockSpec((1, seq, LANES), lambda g, b: (g, b, 0)),
                  pl.BlockSpec((1, seq, LANES), lambda g, b: (g, b, 0)),
                  _const((1, gw)), _const(_block_diag_ones(gw).shape), _const(eye.shape),
                  per_g(fs.shape[1:]), per_g(fc.shape[1:]), _const(ov.shape), _const(ex.shape)],
        out_specs=pl.BlockSpec((n_tiles, gw, tq), lambda g, b: (b, g, 0)),
        out_shape=jax.ShapeDtypeStruct((batch * n_tiles, NSA_KV * gw, tq), F32),
        scratch_shapes=[pltpu.VMEM((n_tiles, n_c, width), F32), pltpu.VMEM((n_tiles, tq, width), F32),
                        pltpu.VMEM((1, tq, width), F32), pltpu.VMEM((n_tiles, HEAD_DIM, width), BF16),
                        pltpu.VMEM((n_tiles, tq, HEAD_DIM), BF16), pltpu.VMEM((n_tiles, tq, HEAD_DIM), BF16),
                        pltpu.VMEM((n_tiles, HEAD_DIM, tq), BF16), pltpu.VMEM((n_tiles, HEAD_DIM, tq), BF16),
                        pltpu.VMEM((LANES, tq), BF16), pltpu.VMEM((1, width), F32), pltpu.VMEM((1, width), F32),
                        pltpu.VMEM((HEAD_DIM, width), F32)],
        compiler_params=_cparams("arbitrary", "arbitrary"), name="nsa_prompt",
    )(q, gates_t, kcv, slcg, wing, jnp.tile(q_gain, NSA_GROUP).reshape(1, gw), _block_diag_ones(gw), eye,
      fs, fc, ov, ex)


SMP_PP = 64
NSA_KW = NSA_KV * HEAD_DIM
SEL_LANES = 2 * LANES


def _nsa_sample_body(pt_ref, q_ref, gate_ref, ckv_ref, *refs, n_slc, k_sel, past, tq):
    pages = refs[:SMP_PP]
    (slcn_ref, wst_ref, winn_ref, gq_ref, bd_ref, bc_ref, bs_ref, bsn_ref, bw_ref, bwn_ref, ov_ref, ex_ref, exn_ref,
     o_ref, nwin_ref, qbd_scr, sel_scr, kbuf_scr, vbuf_scr, s_scr, sn_scr, vn_scr, oc_scr, ow_scr,
     pad_scr) = refs[SMP_PP:]
    j = pl.program_id(1)
    hd = HEAD_DIM
    kw = NSA_KW
    rows = NSA_HEADS * tq
    init = (jnp.full((rows, 1), NEG, F32), jnp.zeros((rows, 1), F32), jnp.zeros((rows, kw), F32))

    @pl.when(j == 0)
    def _():
        qn = _seg_rms(q_ref[...], bd_ref[...], gq_ref[...]) * (hd ** -0.5)
        zero = jnp.zeros((tq, hd), F32)
        blocks = []
        for h in range(NSA_HEADS):
            qh = qn[:, hd * h:hd * (h + 1)]
            blocks.append(jnp.concatenate([qh if g == h // NSA_GROUP else zero for g in range(NSA_KV)], axis=1))
        qbd = jnp.concatenate(blocks, axis=0).astype(BF16)
        qbd_scr[...] = qbd
        ckv = ckv_ref[0]
        bias_c = bc_ref[...]
        ok_c = bias_c > NEG_TEST
        s_c = _dot_nt(qbd, ckv[:, :kw].astype(BF16)) + bias_c
        m_c = jnp.max(s_c, axis=-1, keepdims=True)
        e_c = jnp.where(ok_c, jnp.exp(s_c - m_c), 0.0)
        p_c = (e_c / jnp.maximum(jnp.sum(e_c, axis=-1, keepdims=True), 1e-30)).astype(BF16)
        oc_scr[...] = _dot(p_c, ckv[:, kw:].astype(BF16))
        d = _dot(p_c, ov_ref[...])
        gr = NSA_GROUP * tq
        imp = jnp.concatenate(
            [sum(d[g * gr + h * tq:g * gr + (h + 1) * tq] for h in range(NSA_GROUP)) for g in range(NSA_KV)], axis=0)
        pos = past + (lax.broadcasted_iota(jnp.int32, (NSA_KV * tq, 1), 0) & (tq - 1))
        sel = _select_blocks(imp, pos, n_slc, k_sel)
        sel = jnp.concatenate([sel[g * tq:(g + 1) * tq] for g in range(NSA_KV) for _ in range(NSA_GROUP)], axis=0)
        sel = ((1.0 - sel) * NEG).astype(BF16)
        sel_scr[...] = sel
        pad_scr[...] = jnp.zeros(pad_scr.shape, F32)
        pad_scr[0:tq, :] = slcn_ref[...]
        kn = pad_scr[...].astype(BF16)
        sn_scr[...] = _dot_nt(qbd, kn[:, :kw]) + bsn_ref[...] + _dot(sel, exn_ref[...])
        vn_scr[...] = kn[:, kw:]
        pad_scr[0:tq, :] = winn_ref[...]
        nwin_ref[0] = _shifted_buffer(wst_ref[0], pad_scr[...], tq)
        wst = wst_ref[0].astype(BF16)
        wn = pad_scr[...].astype(BF16)
        bias_w = bw_ref[...]
        mw = _softmax_piece(_dot(qbd, wst[:kw]) + bias_w, bias_w > NEG_TEST, *init, wst[kw:], v_t=True)
        bias_wn = bwn_ref[...]
        mw = _softmax_piece(_dot_nt(qbd, wn[:, :kw]) + bias_wn, bias_wn > NEG_TEST, *mw, wn[:, kw:])
        ow_scr[...] = mw[2] / jnp.maximum(mw[1], 1e-30)

    for i in range(SMP_PP):
        pg = pages[i][0].astype(BF16)
        kbuf_scr[:, i * PAGE:(i + 1) * PAGE] = pg[:kw]
        vbuf_scr[j, :, i * PAGE:(i + 1) * PAGE] = pg[kw:]
    s_scr[j] = _dot(qbd_scr[...], kbuf_scr[...]) + bs_ref[j] + _dot(sel_scr[...], ex_ref[j])

    @pl.when(j == pl.num_programs(1) - 1)
    def _():
        s_all = s_scr[...]
        s_new = sn_scr[...]
        m = jnp.maximum(jnp.max(jnp.max(s_all, axis=0), axis=-1, keepdims=True),
                        jnp.max(s_new, axis=-1, keepdims=True))
        p_new = jnp.exp(s_new - m)
        den = jnp.sum(p_new, axis=-1, keepdims=True)
        acc = _dot(p_new.astype(BF16), vn_scr[...])
        for st in range(s_all.shape[0]):
            p = jnp.exp(s_all[st] - m)
            den = den + jnp.sum(p, axis=-1, keepdims=True)
            acc = acc + _dot_nt(p.astype(BF16), vbuf_scr[st])
        o_s = acc / jnp.maximum(den, 1e-30)
        o_c = oc_scr[...]
        o_w = ow_scr[...]
        outs = []
        for h in range(NSA_HEADS):
            g, hg = divmod(h, NSA_GROUP)
            gt = gate_ref[g]
            rs = slice(h * tq, (h + 1) * tq)
            ls = slice(hd * g, hd * (g + 1))
            outs.append(gt[:, 3 * hg:3 * hg + 1] * o_c[rs, ls] + gt[:, 3 * hg + 1:3 * hg + 2] * o_s[rs, ls]
                        + gt[:, 3 * hg + 2:3 * hg + 3] * o_w[rs, ls])
        o_ref[...] = jnp.concatenate(outs, axis=1)


def _nsa_sample_tables(rel_bias, past, tq, n_c, w_state):
    rows = NSA_HEADS * tq
    total = past + tq
    n_cmp = total // CMP_STRIDE - CMP_BLOCK // CMP_STRIDE + 1
    n_slc = -(-total // SEL_BLOCK)
    pos = past + jnp.arange(tq)
    c_hi = jnp.arange(n_c) * CMP_STRIDE + CMP_BLOCK - 1
    dist_c = pos[:, None] - c_hi[None, :]
    bc = _bias_tile(rel_bias, dist_c, (dist_c >= 0) & (jnp.arange(n_c) < n_cmp)[None, :]).reshape(rows, n_c)
    n_steps = past // (SMP_PP * PAGE)
    d_ext = jnp.arange(tq + past - 1) + 1
    bs = _toeplitz(_bias_tile(rel_bias, d_ext, d_ext >= 0), tq, past)
    bs = bs.reshape(rows, n_steps, SMP_PP * PAGE).transpose(1, 0, 2)
    jn = jnp.arange(PAGE)
    dist_n = jnp.arange(tq)[:, None] - jn[None, :]
    new_ok = (dist_n >= 0) & (jn < tq)[None, :]
    bsn = _bias_tile(rel_bias, dist_n, new_ok).reshape(rows, PAGE)
    dist_w = jnp.arange(tq)[:, None] + w_state - jnp.arange(w_state)[None, :]
    bw = _bias_tile(rel_bias, dist_w, (dist_w >= 0) & (dist_w <= NSA_WINDOW)).reshape(rows, w_state)
    bwn = _bias_tile(rel_bias, dist_n, new_ok & (dist_n <= NSA_WINDOW)).reshape(rows, PAGE)
    c_lo = np.arange(n_c) * CMP_STRIDE
    s_lo = np.arange(SEL_LANES) * SEL_BLOCK
    ov = ((c_lo[:, None] <= s_lo[None, :] + SEL_BLOCK - 1) & (c_lo[:, None] + CMP_BLOCK - 1 >= s_lo[None, :])
          & (np.arange(n_c) < n_cmp)[:, None] & (np.arange(SEL_LANES) < n_slc)[None, :])
    blk = np.arange(SEL_LANES)
    ex = (blk[:, None] == (np.arange(past)[None, :] // SEL_BLOCK)).reshape(SEL_LANES, n_steps, SMP_PP * PAGE)
    ex = ex.transpose(1, 0, 2)
    exn = (blk[:, None] == ((past + np.arange(PAGE))[None, :] // SEL_BLOCK)) & (np.arange(PAGE) < tq)[None, :]
    as_bf = lambda a: jnp.asarray(a.astype(np.float32), BF16)
    return bc, bs, bsn, bw, bwn, as_bf(ov), as_bf(ex), as_bf(exn), n_slc


def _nsa_sample(q, gates, crows, slc_pool_t, table, slc_new, win_state_t, win_new, q_gain, tables, *, tq):
    bc, bs, bsn, bw, bwn, ov, ex, exn, n_slc = tables
    bsz, npg = table.shape
    past = npg * PAGE
    n_c = crows.shape[1]
    w_state = win_state_t.shape[2]
    rows = NSA_HEADS * tq
    qw = NSA_HEADS * HEAD_DIM
    page = lambda i: pl.BlockSpec((1, ROWS_NSA, PAGE), lambda b, j, pt: (pt[b, j * SMP_PP + i], 0, 0))
    per_b = lambda shape: pl.BlockSpec((1,) + shape, lambda b, j, pt: (b,) + (0,) * len(shape))
    new_rows = pl.BlockSpec((tq, ROWS_NSA), lambda b, j, pt: (b, 0))
    return pl.pallas_call(
        functools.partial(_nsa_sample_body, n_slc=n_slc, k_sel=min(SEL_TOPK, n_slc), past=past, tq=tq),
        grid_spec=pltpu.PrefetchScalarGridSpec(
            num_scalar_prefetch=1, grid=(bsz, npg // SMP_PP),
            in_specs=[pl.BlockSpec((tq, qw), lambda b, j, pt: (b, 0)),
                      pl.BlockSpec((NSA_KV, tq, LANES), lambda b, j, pt: (0, b, 0)),
                      per_b((n_c, ROWS_NSA))] + [page(i) for i in range(SMP_PP)] + [
                      new_rows, per_b((ROWS_NSA, w_state)), new_rows,
                      _const((1, qw)), _const(_block_diag_ones(qw).shape), _const(bc.shape), _const(bs.shape), _const(bsn.shape),
                      _const(bw.shape), _const(bwn.shape), _const(ov.shape), _const(ex.shape), _const(exn.shape)],
            out_specs=[pl.BlockSpec((tq, qw), lambda b, j, pt: (b, 0)), per_b((ROWS_NSA, w_state))],
            scratch_shapes=[pltpu.VMEM((rows, NSA_KW), BF16), pltpu.VMEM((rows, SEL_LANES), BF16),
                            pltpu.VMEM((NSA_KW, SMP_PP * PAGE), BF16),
                            pltpu.VMEM((npg // SMP_PP, NSA_KW, SMP_PP * PAGE), BF16),
                            pltpu.VMEM((npg // SMP_PP, rows, SMP_PP * PAGE), F32),
                            pltpu.VMEM((rows, PAGE), F32), pltpu.VMEM((PAGE, NSA_KW), BF16),
                            pltpu.VMEM((rows, NSA_KW), F32), pltpu.VMEM((rows, NSA_KW), F32),
                            pltpu.VMEM((PAGE, ROWS_NSA), F32)]),
        out_shape=[jax.ShapeDtypeStruct((bsz * tq, qw), F32), jax.ShapeDtypeStruct(win_state_t.shape, F32)],
        compiler_params=_cparams("parallel", "arbitrary"), name="nsa_sample",
    )(table, q, gates, crows, *([slc_pool_t] * SMP_PP), slc_new, win_state_t, win_new,
      jnp.tile(q_gain, NSA_HEADS).reshape(1, qw), _block_diag_ones(qw), bc, bs, bsn, bw, bwn, ov, ex, exn)


DIL_TQ = 128


def _head_of_lane(shape):
    return lax.broadcasted_iota(jnp.int32, shape, 1) >> 6


def _stack_heads(qt, lane_head):
    return jnp.concatenate([jnp.where(lane_head == h, qt, 0.0) for h in range(DIL_HEADS)], axis=0).astype(BF16)


def _unstack_heads(r, lane_head, t):
    return sum(jnp.where(lane_head == h, r[h * t:(h + 1) * t], 0.0) for h in range(DIL_HEADS))


def _dil_prompt_body(q_ref, kv_ref, bias_ref, o_ref, lse_ref, *, n_sub, dil):
    tq = DIL_TQ
    gw = ODD_GW
    lane_head = _head_of_lane((tq, gw))
    n_t = n_sub // tq

    def tile(t, r):
        ql = slice(gw * r, gw * (r + 1))
        kl = slice(2 * gw * r, 2 * gw * r + gw)
        vl = slice(2 * gw * r + gw, 2 * gw * (r + 1))
        cur_rows = slice(t * tq, (t + 1) * tq)
        prev_rows = slice(max(t - 1, 0) * tq, (max(t - 1, 0) + 1) * tq)
        kk = jnp.concatenate([kv_ref[prev_rows, kl], kv_ref[cur_rows, kl]], axis=0).astype(BF16)
        vv = jnp.concatenate([kv_ref[prev_rows, vl], kv_ref[cur_rows, vl]], axis=0).astype(BF16)
        bias = bias_ref[min(t, 1)]
        s = _dot_nt(_stack_heads(q_ref[cur_rows, ql], lane_head), kk) + bias
        m = jnp.max(s, axis=-1, keepdims=True)
        e = jnp.where(bias > NEG_TEST, jnp.exp(s - m), 0.0)
        l = jnp.maximum(jnp.sum(e, axis=-1, keepdims=True), 1e-30)
        res = _dot((e / l).astype(BF16), vv)
        lse = jnp.broadcast_to(m + jnp.log(l), (DIL_HEADS * tq, gw))
        o_ref[cur_rows, ql] = _unstack_heads(res, lane_head, tq)
        lse_ref[cur_rows, ql] = _unstack_heads(lse, lane_head, tq)

    for r in range(dil):
        for t in range(n_t):
            tile(t, r)


def _merge_groups(outs, lses):
    mx = functools.reduce(jnp.maximum, lses)
    ws = [jnp.exp(l - mx) for l in lses]
    return sum(w * o for w, o in zip(ws, outs)) / sum(ws)


def _dil_prompt_tables(rel_bias):
    tq = DIL_TQ
    m_ext = jnp.arange(3 * tq - 1) - (tq - 1)
    cur_tile = jnp.arange(2 * tq) >= tq
    tiles = []
    for gi, (win, dil) in enumerate(DIL_PATTERN):
        assert win // dil == tq
        t = _toeplitz(_bias_tile(rel_bias, m_ext * dil, (m_ext >= 0) & (m_ext <= tq)), tq, 2 * tq)
        t = t[gi * DIL_HEADS:(gi + 1) * DIL_HEADS].reshape(DIL_HEADS * tq, 2 * tq)
        tiles.append(jnp.stack([jnp.where(cur_tile, t, NEG), t]))
    return tiles


def _dil_prompt(q_views, kv_views, tables, *, batch, seq):
    gw = ODD_GW
    outs, lses = [], []
    for gi, (_, dil) in enumerate(DIL_PATTERN):
        n_sub = seq // dil
        assert n_sub % DIL_TQ == 0
        qv, kvv = q_views[gi], kv_views[gi]
        o_spec = pl.BlockSpec((n_sub, dil * gw), lambda b: (b, 0))
        o, lse = pl.pallas_call(
            functools.partial(_dil_prompt_body, n_sub=n_sub, dil=dil), grid=(batch,),
            in_specs=[o_spec, pl.BlockSpec((n_sub, dil * 2 * gw), lambda b: (b, 0)), _const(tables[gi].shape)],
            out_specs=[o_spec, o_spec],
            out_shape=[jax.ShapeDtypeStruct(qv.shape, F32)] * 2,
            compiler_params=_cparams("parallel"), name=f"dilated_prompt_g{gi}",
        )(qv, kvv, tables[gi])
        outs.append(o)
        lses.append(lse)
    return outs, lses


def _shifted_buffer(st, new_rows_padded, tq):
    ln = st.shape[1]
    shifted = pltpu.roll(st, ln - tq, 1)
    new_t = pltpu.roll(new_rows_padded.T, LANES - tq, 1)
    lane = lax.broadcasted_iota(jnp.int32, (st.shape[0], LANES), 1)
    tail = jnp.where(lane >= LANES - tq, new_t, shifted[:, ln - LANES:])
    return tail if ln == LANES else jnp.concatenate([shifted[:, :ln - LANES], tail], axis=1)


def _dil_sample_body(q_ref, kvn_ref, st0_ref, st1_ref, st2_ref, b0_ref, b1_ref, b2_ref, bn_ref, o_ref, lse_ref,
                     n0_ref, n1_ref, n2_ref, pad_scr, *, tq):
    gw = ODD_GW
    lane_head = _head_of_lane((tq, gw))
    rows = DIL_HEADS * tq
    pad_scr[...] = jnp.zeros(pad_scr.shape, F32)
    for gi, (st_ref, b_ref, new_ref) in enumerate(((st0_ref, b0_ref, n0_ref), (st1_ref, b1_ref, n1_ref),
                                                    (st2_ref, b2_ref, n2_ref))):
        qs = _stack_heads(q_ref[gi], lane_head)
        pad_scr[0:tq, :] = kvn_ref[gi]
        new_ref[0] = _shifted_buffer(st_ref[0], pad_scr[...], tq)
        st = st_ref[0].astype(BF16)
        kn = pad_scr[...].astype(BF16)
        bias = b_ref[...]
        acc = (jnp.full((rows, 1), NEG, F32), jnp.zeros((rows, 1), F32), jnp.zeros((rows, gw), F32))
        acc = _softmax_piece(_dot(qs, st[:gw]) + bias, bias > NEG_TEST, *acc, st[gw:], v_t=True)
        bias_n = bn_ref[gi]
        m, l, a = _softmax_piece(_dot_nt(qs, kn[:, :gw]) + bias_n, bias_n > NEG_TEST, *acc, kn[:, gw:])
        l = jnp.maximum(l, 1e-30)
        o_ref[gi] = _unstack_heads(a / l, lane_head, tq)
        lse_ref[gi] = _unstack_heads(jnp.broadcast_to(m + jnp.log(l), (rows, gw)), lane_head, tq)


def _dil_sample_tables(rel_bias, tq, state_lens):
    rows = DIL_HEADS * tq
    i = jnp.arange(tq)[:, None]
    per_state, per_new = [], []
    jn = jnp.arange(PAGE)[None, :]
    for gi, ((win, dil), ln) in enumerate(zip(DIL_PATTERN, state_lens)):
        hs = slice(gi * DIL_HEADS, (gi + 1) * DIL_HEADS)
        d = i + ln - jnp.arange(ln)[None, :]
        per_state.append(_bias_tile(rel_bias, d, (d % dil == 0) & (d <= win))[hs].reshape(rows, ln))
        dn = i - jn
        per_new.append(_bias_tile(rel_bias, dn, (dn >= 0) & (dn % dil == 0) & (dn <= win) & (jn < tq))[hs]
                       .reshape(rows, PAGE))
    return per_state, jnp.stack(per_new)


def _dil_sample(q3, kv3, states, tables, *, tq):
    per_state, bias_new = tables
    bsz = states[0].shape[0]
    gw = ODD_GW
    st_spec = lambda s: pl.BlockSpec((1,) + s.shape[1:], lambda b: (b, 0, 0))
    grp = lambda w: pl.BlockSpec((3, tq, w), lambda b: (0, b, 0))
    return pl.pallas_call(
        functools.partial(_dil_sample_body, tq=tq), grid=(bsz,),
        in_specs=[grp(gw), grp(2 * gw)] + [st_spec(s) for s in states] + [_const(t.shape) for t in per_state]
        + [_const(bias_new.shape)],
        out_specs=[grp(gw), grp(gw)] + [st_spec(s) for s in states],
        out_shape=[jax.ShapeDtypeStruct((3, bsz * tq, gw), F32)] * 2
        + [jax.ShapeDtypeStruct(s.shape, F32) for s in states],
        scratch_shapes=[pltpu.VMEM((PAGE, 2 * gw), F32)],
        compiler_params=_cparams("parallel"), name="dilated_sample",
    )(q3, kv3, *states, *per_state, bias_new)


def _pad_even_w(w_in):
    return jnp.concatenate([w_in, jnp.zeros((w_in.shape[0], EVEN_PAD - w_in.shape[1]), w_in.dtype)], axis=1).astype(BF16)


def _even_mixer_prompt(h, gain, w_in, w_out, lb, hg_norm, qk_norm, cmp_pe, cmp_w1, cmp_w2, rel_bias, *, batch, seq, tm):
    n = batch * seq
    hg, q, cmp_rows, slcg, wing, gates_t, cmp_t, slc_t, win_t = _inproj_even(
        h, gain, _pad_even_w(w_in), qk_norm, tm=tm, seq=seq)
    s0 = jnp.zeros((batch, HG_HEADS, HEAD_DIM, HEAD_DIM), F32)
    o_h, s_fin = _hgrn(hg, lb, hg_norm, s0, batch=batch, seq=seq)
    cw = _compress_weights(cmp_pe, cmp_w1, cmp_w2, qk_norm[1])
    npg = seq // PAGE
    table = jnp.arange(batch * npg, dtype=jnp.int32).reshape(batch, npg)
    _, kcv = _compress(cmp_rows.reshape(n // PAGE, PAGE, ROWS_NSA), table, cw, transposed=False)
    tables = _nsa_prompt_t_tables(rel_bias, seq, kcv.shape[2])
    o_nt = _nsa_prompt_t(q, gates_t, kcv, slcg, wing, qk_norm[0], tables, batch=batch, seq=seq)
    h = _outproj2t(h, o_h, o_nt, w_out.astype(BF16), tm=tm)
    keep = min(NSA_WINDOW, seq)
    return h, {"hgrn": s_fin, "cmp": _rows_from_t(cmp_t, NSA_KV), "slc": _rows_from_t(slc_t, NSA_KV),
               "win": _rows_from_t(win_t[:, :, seq - keep:], NSA_KV)}


def _rows_from_t(a, heads):
    return jnp.moveaxis(a.reshape(a.shape[0], 2, heads, HEAD_DIM, a.shape[2]), -1, 1)


def _even_mixer_sample(h, gain, w_in, w_out, lb, hg_norm, qk_norm, cmp_pe, cmp_w1, cmp_w2, rel_bias,
                       state_hgrn, cmp_pool, slc_pool, win_state, page_table, *, batch, tq, tm):
    assert tq < CMP_STRIDE and tq & (tq - 1) == 0
    hg, q, cmp_rows, slc_rows, win_rows, gates = _inproj_even(h, gain, _pad_even_w(w_in), qk_norm, tm=tm)
    o_h, s_fin = _hgrn(hg, lb, hg_norm, state_hgrn, batch=batch, seq=tq)
    cw = _compress_weights(cmp_pe, cmp_w1, cmp_w2, qk_norm[1])
    rows_t = lambda a: jnp.moveaxis(a, 1, -1).reshape(a.shape[0], ROWS_NSA, a.shape[1])
    crows, _ = _compress(rows_t(cmp_pool), page_table, cw, transposed=True)
    past = page_table.shape[1] * PAGE
    w_state = win_state.shape[1]
    tables = _nsa_sample_tables(rel_bias, past, tq, crows.shape[1], w_state)
    o_n, new_win_t = _nsa_sample(q, gates, crows, rows_t(slc_pool), page_table, slc_rows, rows_t(win_state), win_rows,
                                 qk_norm[0], tables, tq=tq)
    h = _outproj2(h, o_h, o_n, w_out.astype(BF16), tm=tm)
    new_win = jnp.moveaxis(new_win_t.reshape((batch,) + win_state.shape[2:] + (w_state,)), -1, 1)
    return h, {"hgrn": s_fin, "cmp": cmp_rows, "slc": slc_rows, "win": new_win}


def _odd_mixer_prompt(h, gain, w_in, w_out, qk_norm, rel_bias, *, batch, seq, tm):
    qv0, kvv0, qv1, kvv1, qv2, kvv2, *kv_t = _inproj_odd(h, gain, w_in.astype(BF16), qk_norm, tm=tm, seq=seq)
    outs, lses = _dil_prompt([qv0, qv1, qv2], [kvv0, kvv1, kvv2], _dil_prompt_tables(rel_bias),
                             batch=batch, seq=seq)
    bufs = [_rows_from_t(t[:, :, seq - min(w, seq):], DIL_HEADS) for t, (w, _) in zip(kv_t, DIL_PATTERN)]
    return _outproj_merge(h, outs, lses, w_out.astype(BF16), tm=tm, views=True), bufs


def _odd_mixer_sample(h, gain, w_in, w_out, qk_norm, rel_bias, states, *, batch, tq, tm):
    q3, kv3 = _inproj_odd(h, gain, w_in.astype(BF16), qk_norm, tm=tm)
    tables = _dil_sample_tables(rel_bias, tq, [s.shape[1] for s in states])
    states_t = [jnp.moveaxis(s, 1, -1).reshape(batch, 2 * ODD_GW, s.shape[1]) for s in states]
    o3, lse3, *new_t = _dil_sample(q3, kv3, states_t, tables, tq=tq)
    new = [jnp.moveaxis(n.reshape(s.shape[:1] + s.shape[2:] + s.shape[1:2]), -1, 1) for n, s in zip(new_t, states)]
    h = _outproj_merge(h, [o3[g] for g in range(3)], [lse3[g] for g in range(3)], w_out.astype(BF16), tm=tm)
    return h, new


TM_PROMPT = 512
TM_SAMPLE = 256


def kernel(x_prompt, x_sample, p_prompt, p_sample, state_hgrn, cache_nsa_cmp_kv, cache_nsa_slc_kv, state_nsa_win_kv,
           state_dil_kv_0, state_dil_kv_1, state_dil_kv_2, page_table, rel_bias, norm_ffn1, norm_mix, norm_ffn2,
           norm_ple, w_ffn1_in, w_ffn1_out, w_ffn2_in, w_ffn2_out, w_ple_gate, w_ple_proj, w_in_even, w_out_even,
           hgrn_lb_logits, hgrn_norm, nsa_qk_norm, nsa_cmp_pe, nsa_cmp_w1, nsa_cmp_w2, w_in_odd, w_out_odd,
           dil_qk_norm):
    depth = norm_ffn1.shape[0]
    dil_states = (state_dil_kv_0, state_dil_kv_1, state_dil_kv_2)
    bf = lambda w: w.astype(BF16)
    w1i, w1o, w2i, w2o, wpg, wpp = (bf(w) for w in (w_ffn1_in, w_ffn1_out, w_ffn2_in, w_ffn2_out, w_ple_gate,
                                                     w_ple_proj))
    lb_all = jnp.cumsum(jax.nn.softmax(hgrn_lb_logits.astype(F32), axis=0), axis=0)
    g_ffn1, g_ffn2, g_ple = (g.reshape(depth, 1, -1) for g in (norm_ffn1, norm_ffn2, norm_ple))

    def run(x, p, sample):
        batch, seq, d = x.shape
        n = batch * seq
        tm = TM_SAMPLE if sample else TM_PROMPT
        h = x.reshape(n, d)
        p = p.reshape(depth, n, p.shape[-1])
        even, odd = [], []
        for li in range(depth):
            h = _ffn(h, li, g_ffn1, w1i, w1o, tm=tm)
            if li % 2 == 0:
                ei = li // 2
                args = (h, norm_mix[li], w_in_even[ei], w_out_even[ei], lb_all[ei], hgrn_norm[ei], nsa_qk_norm[ei],
                        nsa_cmp_pe[ei], nsa_cmp_w1[ei], nsa_cmp_w2[ei], rel_bias)
                if sample:
                    h, st = _even_mixer_sample(*args, state_hgrn[ei], cache_nsa_cmp_kv[ei], cache_nsa_slc_kv[ei],
                                               state_nsa_win_kv[ei], page_table, batch=batch, tq=seq, tm=tm)
                else:
                    h, st = _even_mixer_prompt(*args, batch=batch, seq=seq, tm=tm)
                even.append(st)
            else:
                oi = li // 2
                args = (h, norm_mix[li], w_in_odd[oi], w_out_odd[oi], dil_qk_norm[oi], rel_bias)
                if sample:
                    h, bufs = _odd_mixer_sample(*args, [s[oi] for s in dil_states], batch=batch, tq=seq, tm=tm)
                else:
                    h, bufs = _odd_mixer_prompt(*args, batch=batch, seq=seq, tm=tm)
                odd.append(bufs)
            h = _ffn(h, li, g_ffn2, w2i, w2o, ple=(p, g_ple, wpg, wpp), tm=tm)
        rows = lambda a: a.reshape(batch, -1, 2, NSA_KV, HEAD_DIM)
        drows = lambda a: a.reshape(batch, -1, 2, DIL_HEADS, HEAD_DIM)
        return (h.reshape(batch, seq, d),
                jnp.stack([s["hgrn"] for s in even]), jnp.stack([rows(s["cmp"]) for s in even]),
                jnp.stack([rows(s["slc"]) for s in even]), jnp.stack([rows(s["win"]) for s in even]),
                jnp.stack([drows(b[0]) for b in odd]), jnp.stack([drows(b[1]) for b in odd]),
                jnp.stack([drows(b[2]) for b in odd]))

    y_p, hg_p, cmp_p, slc_p, win_p, d0_p, d1_p, d2_p = run(x_prompt, p_prompt, False)
    y_s, hg_s, cmp_s, slc_s, win_s, d0_s, d1_s, d2_s = run(x_sample, p_sample, True)
    return (y_p, y_s, hg_p, hg_s, cmp_p, cmp_s, slc_p, slc_s, win_p, win_s, d0_p, d0_s, d1_p, d1_s, d2_p, d2_s)
```
